```python
import jax, jax.numpy as jnp
from jax import lax
import numpy as np

D_MODEL = 1024
BATCH = 8
SEQ = 4096
DEPTH = 2

HEAD_DIM = 64
N_SB_HEADS = 12
SB_WIDTH = N_SB_HEADS * HEAD_DIM
N_MEM_HEADS = 4
MEM_WIDTH = N_MEM_HEADS * HEAD_DIM
MEM_TOKENS = 256
MIX_WIDTH = SB_WIDTH + MEM_WIDTH
POOL_WINDOWS = (2, 4, 8, 16)
N_POOL_GROUPS = len(POOL_WINDOWS)
POOL_WIDTH = SB_WIDTH
POOL_GROUP = POOL_WIDTH // N_POOL_GROUPS
D_FF = -(-8 * D_MODEL // (3 * 256)) * 256
SB_BLOCK = 128
N_A_LAYERS = DEPTH // 2
N_B_LAYERS = DEPTH - N_A_LAYERS
EPS = 1e-6

kernel_name = "yoco_pool_stickbreak_hybrid"


def rmsnorm(x, g):
    xf = x.astype(jnp.float32)
    y = xf * lax.rsqrt(jnp.mean(xf * xf, axis=-1, keepdims=True) + EPS)
    return (y * g.astype(jnp.float32)).astype(x.dtype)


def swiglu(h, w_gu, w_down):
    gate, up = jnp.split(h @ w_gu, 2, axis=-1)
    return (jax.nn.silu(gate) * up) @ w_down


def multiscale_pool(u):
    s = u.shape[1]
    uf = u.astype(jnp.float32)
    cs = jnp.cumsum(uf, axis=1)
    pos = jnp.arange(s)
    outs = []
    for g, w in enumerate(POOL_WINDOWS):
        c = cs[..., g * POOL_GROUP:(g + 1) * POOL_GROUP]
        prev = jnp.pad(c, ((0, 0), (w, 0), (0, 0)))[:, :s]
        cnt = jnp.minimum(pos + 1, w).astype(jnp.float32)[None, :, None]
        outs.append((c - prev) / cnt)
    pooled = jnp.concatenate(outs, axis=-1)
    return (pooled - uf).astype(u.dtype)


def memory_kv(mem, mem_norm, w_mem_kv):
    b, m, _ = mem.shape
    k, v = jnp.split(rmsnorm(mem, mem_norm) @ w_mem_kv, 2, axis=-1)
    return (k.reshape(b, m, N_MEM_HEADS, HEAD_DIM), v.reshape(b, m, N_MEM_HEADS, HEAD_DIM))


def memory_attention(q, mk, mv):
    b, s, _ = q.shape
    qh = q.reshape(b, s, N_MEM_HEADS, HEAD_DIM)
    logits = jnp.einsum("bshd,bmhd->bhsm", qh, mk).astype(jnp.float32) * (HEAD_DIM ** -0.5)
    p = jax.nn.softmax(logits, axis=-1).astype(mv.dtype)
    return jnp.einsum("bhsm,bmhd->bshd", p, mv).reshape(b, s, MEM_WIDTH)


def stick_breaking_attention(q, k, v):
    s = q.shape[2]
    scale = HEAD_DIM ** -0.5
    outs = []
    for i in range(s // SB_BLOCK):
        q0 = i * SB_BLOCK
        end = q0 + SB_BLOCK
        qb = q[:, :, q0:end]
        kb = k[:, :, :end]
        vb = v[:, :, :end]
        z = jnp.einsum("bhqd,bhkd->bhqk", qb, kb).astype(jnp.float32) * scale
        tpos = q0 + jnp.arange(SB_BLOCK)
        spos = jnp.arange(end)
        mask = spos[None, :] < tpos[:, None]
        log_not = jnp.where(mask, jax.nn.log_sigmoid(-z), 0.0)
        later = lax.cumsum(log_not, axis=3, reverse=True) - log_not
        wts = jnp.where(mask, jnp.exp(jax.nn.log_sigmoid(z) + later), 0.0)
        outs.append(jnp.einsum("bhqk,bhkd->bhqd", wts.astype(vb.dtype), vb))
    return jnp.concatenate(outs, axis=2)


def pool_layer(x, mem, mem_norm, norm_mix, w_in, w_group, scale, w_mem_kv, w_out, norm_ffn, w_gu, w_down):
    b, s, _ = x.shape
    proj = rmsnorm(x, norm_mix) @ w_in
    u_pool, q_mem = proj[..., :POOL_WIDTH], proj[..., POOL_WIDTH:]
    pooled = multiscale_pool(u_pool).reshape(b, s, N_POOL_GROUPS, POOL_GROUP)
    grouped = jnp.einsum("bsgc,gcd->bsgd", pooled, w_group).reshape(b, s, POOL_WIDTH) * scale
    mk, mv = memory_kv(mem, mem_norm, w_mem_kv)
    mem_out = memory_attention(q_mem, mk, mv)
    x = x + jnp.concatenate([grouped, mem_out], axis=-1) @ w_out
    return x + swiglu(rmsnorm(x, norm_ffn), w_gu, w_down)


def sb_layer(x, mem, k_sh, v_sh, mem_norm, norm_mix, w_q, w_mem_kv, w_out, norm_ffn, w_gu, w_down):
    b, s, _ = x.shape
    proj = rmsnorm(x, norm_mix) @ w_q
    q_sb = proj[..., :SB_WIDTH].reshape(b, s, N_SB_HEADS, HEAD_DIM).transpose(0, 2, 1, 3)
    q_mem = proj[..., SB_WIDTH:]
    sb_out = stick_breaking_attention(q_sb, k_sh, v_sh).transpose(0, 2, 1, 3).reshape(b, s, SB_WIDTH)
    mk, mv = memory_kv(mem, mem_norm, w_mem_kv)
    mem_out = memory_attention(q_mem, mk, mv)
    x = x + jnp.concatenate([sb_out, mem_out], axis=-1) @ w_out
    return x + swiglu(rmsnorm(x, norm_ffn), w_gu, w_down)


def _fwd_setup_inputs(seed: int = 0) -> dict:
    key = jax.random.key(seed)
    ks = jax.random.split(key, 24)
    f32 = jnp.float32
    out_gain = (2.0 * DEPTH) ** -0.5

    def w(k, shape, fan_in, gain=1.0):
        return jax.random.normal(k, shape, f32) * (gain * fan_in ** -0.5)

    def g(k, shape):
        return 1.0 + 0.05 * jax.random.normal(k, shape, f32)

    na, nb = N_A_LAYERS, N_B_LAYERS
    return {
        "x": jax.random.normal(ks[0], (BATCH, SEQ, D_MODEL), f32),
        "mem": jax.random.normal(ks[1], (BATCH, MEM_TOKENS, D_MODEL), f32),
        "mem_norm": g(ks[2], (D_MODEL,)),
        "a_norm_mix": g(ks[3], (na, D_MODEL)),
        "a_w_in": w(ks[4], (na, D_MODEL, MIX_WIDTH), D_MODEL),
        "a_w_group": w(ks[5], (na, N_POOL_GROUPS, POOL_GROUP, POOL_GROUP), POOL_GROUP),
        "a_scale": g(ks[6], (na, POOL_WIDTH)),
        "a_w_mem_kv": w(ks[7], (na, D_MODEL, 2 * MEM_WIDTH), D_MODEL),
        "a_w_out": w(ks[8], (na, MIX_WIDTH, D_MODEL), MIX_WIDTH, out_gain),
        "a_norm_ffn": g(ks[9], (na, D_MODEL)),
        "a_w_gu": w(ks[10], (na, D_MODEL, 2 * D_FF), D_MODEL),
        "a_w_down": w(ks[11], (na, D_FF, D_MODEL), D_FF, out_gain),
        "kv_norm": g(ks[12], (D_MODEL,)),
        "w_kv": w(ks[13], (D_MODEL, 2 * SB_WIDTH), D_MODEL),
        "b_norm_mix": g(ks[14], (nb, D_MODEL)),
        "b_w_q": w(ks[15], (nb, D_MODEL, MIX_WIDTH), D_MODEL),
        "b_w_mem_kv": w(ks[16], (nb, D_MODEL, 2 * MEM_WIDTH), D_MODEL),
        "b_w_out": w(ks[17], (nb, MIX_WIDTH, D_MODEL), MIX_WIDTH, out_gain),
        "b_norm_ffn": g(ks[18], (nb, D_MODEL)),
        "b_w_gu": w(ks[19], (nb, D_MODEL, 2 * D_FF), D_MODEL),
        "b_w_down": w(ks[20], (nb, D_FF, D_MODEL), D_FF, out_gain),
        "final_norm": g(ks[21], (D_MODEL,)),
    }


def _fwd_reference(x, mem, mem_norm, a_norm_mix, a_w_in, a_w_group, a_scale, a_w_mem_kv, a_w_out,
              a_norm_ffn, a_w_gu, a_w_down, kv_norm, w_kv, b_norm_mix, b_w_q, b_w_mem_kv,
              b_w_out, b_norm_ffn, b_w_gu, b_w_down, final_norm):
    b, s, _ = x.shape
    k_sh = v_sh = None
    for layer in range(DEPTH):
        if layer < N_A_LAYERS:
            i = layer
            x = pool_layer(x, mem, mem_norm, a_norm_mix[i], a_w_in[i], a_w_group[i], a_scale[i],
                           a_w_mem_kv[i], a_w_out[i], a_norm_ffn[i], a_w_gu[i], a_w_down[i])
        else:
            if layer == N_A_LAYERS:
                kv = rmsnorm(x, kv_norm) @ w_kv
                k_sh = kv[..., :SB_WIDTH].reshape(b, s, N_SB_HEADS, HEAD_DIM).transpose(0, 2, 1, 3)
                v_sh = kv[..., SB_WIDTH:].reshape(b, s, N_SB_HEADS, HEAD_DIM).transpose(0, 2, 1, 3)
            j = layer - N_A_LAYERS
            x = sb_layer(x, mem, k_sh, v_sh, mem_norm, b_norm_mix[j], b_w_q[j], b_w_mem_kv[j],
                         b_w_out[j], b_norm_ffn[j], b_w_gu[j], b_w_down[j])
    return rmsnorm(x, final_norm)


import jax as _jax
import jax.numpy as _jnp

TWIN_FORMAT = 'train_step'
FWD_PARAMS = ['x', 'mem', 'mem_norm', 'a_norm_mix', 'a_w_in', 'a_w_group', 'a_scale', 'a_w_mem_kv', 'a_w_out', 'a_norm_ffn', 'a_w_gu', 'a_w_down', 'kv_norm', 'w_kv', 'b_norm_mix', 'b_w_q', 'b_w_mem_kv', 'b_w_out', 'b_norm_ffn', 'b_w_gu', 'b_w_down', 'final_norm']
TWIN_WEIGHTS = ['mem_norm', 'a_norm_mix', 'a_w_in', 'a_w_group', 'a_scale', 'a_w_mem_kv', 'a_w_out', 'a_norm_ffn', 'a_w_gu', 'a_w_down', 'kv_norm', 'w_kv', 'b_norm_mix', 'b_w_q', 'b_w_mem_kv', 'b_w_out', 'b_norm_ffn', 'b_w_gu', 'b_w_down', 'final_norm']
TWIN_DIFF_INPUT = 'x'
TWIN_INPUTS = ['x', 'mem', 'mem_norm', 'a_norm_mix', 'a_w_in', 'a_w_group', 'a_scale', 'a_w_mem_kv', 'a_w_out', 'a_norm_ffn', 'a_w_gu', 'a_w_down', 'kv_norm', 'w_kv', 'b_norm_mix', 'b_w_q', 'b_w_mem_kv', 'b_w_out', 'b_norm_ffn', 'b_w_gu', 'b_w_down', 'final_norm', 'loss_target', 'm_mem_norm', 'm_a_norm_mix', 'm_a_w_in', 'm_a_w_group', 'm_a_scale', 'm_a_w_mem_kv', 'm_a_w_out', 'm_a_norm_ffn', 'm_a_w_gu', 'm_a_w_down', 'm_kv_norm', 'm_w_kv', 'm_b_norm_mix', 'm_b_w_q', 'm_b_w_mem_kv', 'm_b_w_out', 'm_b_norm_ffn', 'm_b_w_gu', 'm_b_w_down', 'm_final_norm', 'v_mem_norm', 'v_a_norm_mix', 'v_a_w_in', 'v_a_w_group', 'v_a_scale', 'v_a_w_mem_kv', 'v_a_w_out', 'v_a_norm_ffn', 'v_a_w_gu', 'v_a_w_down', 'v_kv_norm', 'v_w_kv', 'v_b_norm_mix', 'v_b_w_q', 'v_b_w_mem_kv', 'v_b_w_out', 'v_b_norm_ffn', 'v_b_w_gu', 'v_b_w_down', 'v_final_norm']
TWIN_OUTPUTS = ['loss', 'grad_x', 'grad_mem_norm', 'grad_a_norm_mix', 'grad_a_w_in', 'grad_a_w_group', 'grad_a_scale', 'grad_a_w_mem_kv', 'grad_a_w_out', 'grad_a_norm_ffn', 'grad_a_w_gu', 'grad_a_w_down', 'grad_kv_norm', 'grad_w_kv', 'grad_b_norm_mix', 'grad_b_w_q', 'grad_b_w_mem_kv', 'grad_b_w_out', 'grad_b_norm_ffn', 'grad_b_w_gu', 'grad_b_w_down', 'grad_final_norm', 'delta_mem_norm', 'delta_a_norm_mix', 'delta_a_w_in', 'delta_a_w_group', 'delta_a_scale', 'delta_a_w_mem_kv', 'delta_a_w_out', 'delta_a_norm_ffn', 'delta_a_w_gu', 'delta_a_w_down', 'delta_kv_norm', 'delta_w_kv', 'delta_b_norm_mix', 'delta_b_w_q', 'delta_b_w_mem_kv', 'delta_b_w_out', 'delta_b_norm_ffn', 'delta_b_w_gu', 'delta_b_w_down', 'delta_final_norm', 'new_m_mem_norm', 'new_m_a_norm_mix', 'new_m_a_w_in', 'new_m_a_w_group', 'new_m_a_scale', 'new_m_a_w_mem_kv', 'new_m_a_w_out', 'new_m_a_norm_ffn', 'new_m_a_w_gu', 'new_m_a_w_down', 'new_m_kv_norm', 'new_m_w_kv', 'new_m_b_norm_mix', 'new_m_b_w_q', 'new_m_b_w_mem_kv', 'new_m_b_w_out', 'new_m_b_norm_ffn', 'new_m_b_w_gu', 'new_m_b_w_down', 'new_m_final_norm', 'new_v_mem_norm', 'new_v_a_norm_mix', 'new_v_a_w_in', 'new_v_a_w_group', 'new_v_a_scale', 'new_v_a_w_mem_kv', 'new_v_a_w_out', 'new_v_a_norm_ffn', 'new_v_a_w_gu', 'new_v_a_w_down', 'new_v_kv_norm', 'new_v_w_kv', 'new_v_b_norm_mix', 'new_v_b_w_q', 'new_v_b_w_mem_kv', 'new_v_b_w_out', 'new_v_b_norm_ffn', 'new_v_b_w_gu', 'new_v_b_w_down', 'new_v_final_norm']
TWIN_LEAF_KINDS = {'loss': 'loss', 'grad_x': 'grad_x', 'grad_mem_norm': 'grad_w', 'grad_a_norm_mix': 'grad_w', 'grad_a_w_in': 'grad_w', 'grad_a_w_group': 'grad_w', 'grad_a_scale': 'grad_w', 'grad_a_w_mem_kv': 'grad_w', 'grad_a_w_out': 'grad_w', 'grad_a_norm_ffn': 'grad_w', 'grad_a_w_gu': 'grad_w', 'grad_a_w_down': 'grad_w', 'grad_kv_norm': 'grad_w', 'grad_w_kv': 'grad_w', 'grad_b_norm_mix': 'grad_w', 'grad_b_w_q': 'grad_w', 'grad_b_w_mem_kv': 'grad_w', 'grad_b_w_out': 'grad_w', 'grad_b_norm_ffn': 'grad_w', 'grad_b_w_gu': 'grad_w', 'grad_b_w_down': 'grad_w', 'grad_final_norm': 'grad_w', 'delta_mem_norm': 'delta_w', 'delta_a_norm_mix': 'delta_w', 'delta_a_w_in': 'delta_w', 'delta_a_w_group': 'delta_w', 'delta_a_scale': 'delta_w', 'delta_a_w_mem_kv': 'delta_w', 'delta_a_w_out': 'delta_w', 'delta_a_norm_ffn': 'delta_w', 'delta_a_w_gu': 'delta_w', 'delta_a_w_down': 'delta_w', 'delta_kv_norm': 'delta_w', 'delta_w_kv': 'delta_w', 'delta_b_norm_mix': 'delta_w', 'delta_b_w_q': 'delta_w', 'delta_b_w_mem_kv': 'delta_w', 'delta_b_w_out': 'delta_w', 'delta_b_norm_ffn': 'delta_w', 'delta_b_w_gu': 'delta_w', 'delta_b_w_down': 'delta_w', 'delta_final_norm': 'delta_w', 'new_m_mem_norm': 'new_m', 'new_m_a_norm_mix': 'new_m', 'new_m_a_w_in': 'new_m', 'new_m_a_w_group': 'new_m', 'new_m_a_scale': 'new_m', 'new_m_a_w_mem_kv': 'new_m', 'new_m_a_w_out': 'new_m', 'new_m_a_norm_ffn': 'new_m', 'new_m_a_w_gu': 'new_m', 'new_m_a_w_down': 'new_m', 'new_m_kv_norm': 'new_m', 'new_m_w_kv': 'new_m', 'new_m_b_norm_mix': 'new_m', 'new_m_b_w_q': 'new_m', 'new_m_b_w_mem_kv': 'new_m', 'new_m_b_w_out': 'new_m', 'new_m_b_norm_ffn': 'new_m', 'new_m_b_w_gu': 'new_m', 'new_m_b_w_down': 'new_m', 'new_m_final_norm': 'new_m', 'new_v_mem_norm': 'new_v', 'new_v_a_norm_mix': 'new_v', 'new_v_a_w_in': 'new_v', 'new_v_a_w_group': 'new_v', 'new_v_a_scale': 'new_v', 'new_v_a_w_mem_kv': 'new_v', 'new_v_a_w_out': 'new_v', 'new_v_a_norm_ffn': 'new_v', 'new_v_a_w_gu': 'new_v', 'new_v_a_w_down': 'new_v', 'new_v_kv_norm': 'new_v', 'new_v_w_kv': 'new_v', 'new_v_b_norm_mix': 'new_v', 'new_v_b_w_q': 'new_v', 'new_v_b_w_mem_kv': 'new_v', 'new_v_b_w_out': 'new_v', 'new_v_b_norm_ffn': 'new_v', 'new_v_b_w_gu': 'new_v', 'new_v_b_w_down': 'new_v', 'new_v_final_norm': 'new_v'}


def _forward(args):
    return _fwd_reference(*[args[k] for k in FWD_PARAMS])


def _output_shape():
    out = _jax.eval_shape(lambda: _forward(_fwd_setup_inputs(0)))
    return out.shape, out.dtype

N_MICROBATCH = 1
ADAM_LR = 0.001
ADAM_B1 = 0.9
ADAM_B2 = 0.999
ADAM_EPS = 1e-08
ADAM_WD = 0.01
ADAM_STEP = 10
PER_EXAMPLE_BATCH_AXIS = {'x': 0, 'mem': 0, 'loss_target': 0}
SHARED_INPUTS = []
_WEIGHT_DTYPES = {'mem_norm': _jnp.float32, 'a_norm_mix': _jnp.float32, 'a_w_in': _jnp.float32, 'a_w_group': _jnp.float32, 'a_scale': _jnp.float32, 'a_w_mem_kv': _jnp.float32, 'a_w_out': _jnp.float32, 'a_norm_ffn': _jnp.float32, 'a_w_gu': _jnp.float32, 'a_w_down': _jnp.float32, 'kv_norm': _jnp.float32, 'w_kv': _jnp.float32, 'b_norm_mix': _jnp.float32, 'b_w_q': _jnp.float32, 'b_w_mem_kv': _jnp.float32, 'b_w_out': _jnp.float32, 'b_norm_ffn': _jnp.float32, 'b_w_gu': _jnp.float32, 'b_w_down': _jnp.float32, 'final_norm': _jnp.float32}
MOMENT_SCALE = {'mem_norm': 9.533825e-03, 'a_norm_mix': 7.113713e-02, 'a_w_in': 7.024356e-02, 'a_w_group': 8.085408e-02, 'a_scale': 7.876568e-02, 'a_w_mem_kv': 9.893811e-03, 'a_w_out': 1.402370e-01, 'a_norm_ffn': 7.735656e-02, 'a_w_gu': 3.085532e-02, 'a_w_down': 1.012179e-01, 'kv_norm': 4.844509e-02, 'w_kv': 4.059372e-02, 'b_norm_mix': 2.132920e-02, 'b_w_q': 2.172153e-02, 'b_w_mem_kv': 8.665243e-03, 'b_w_out': 9.085147e-02, 'b_norm_ffn': 6.654533e-02, 'b_w_gu': 2.773242e-02, 'b_w_down': 9.089270e-02, 'final_norm': 3.207183e+01}


def _to_microbatches(a, axis):
    t = _jnp.moveaxis(a, axis, 0)
    t = t.reshape((N_MICROBATCH, t.shape[0] // N_MICROBATCH) + t.shape[1:])
    return _jnp.moveaxis(t, 1, axis + 1)


def setup_inputs(seed: int = 0) -> dict:
    inp = _fwd_setup_inputs(seed)
    key = _jax.random.fold_in(_jax.random.key(seed), 7919)
    shape, _ = _output_shape()
    out = dict(inp)
    out["loss_target"] = _jax.random.normal(_jax.random.fold_in(key, 0), shape, _jnp.float32)
    for i, name in enumerate(TWIN_WEIGHTS):
        w = inp[name].astype(_jnp.float32)
        if MOMENT_SCALE is None:
            s = _jnp.sqrt(_jnp.mean(_jnp.square(w)) + 1e-30)
        else:
            s = MOMENT_SCALE[name]
        km, kv = _jax.random.split(_jax.random.fold_in(key, i + 1))
        out[name] = w
        out["m_" + name] = s * _jax.random.normal(km, w.shape, _jnp.float32)
        out["v_" + name] = (s * s) * _jax.random.uniform(kv, w.shape, _jnp.float32, 0.5, 1.5)
    if N_MICROBATCH > 1:
        for name, axis in PER_EXAMPLE_BATCH_AXIS.items():
            out[name] = _to_microbatches(out[name], axis)
    return {'x': out['x'], 'mem': out['mem'], 'mem_norm': out['mem_norm'], 'a_norm_mix': out['a_norm_mix'], 'a_w_in': out['a_w_in'], 'a_w_group': out['a_w_group'], 'a_scale': out['a_scale'], 'a_w_mem_kv': out['a_w_mem_kv'], 'a_w_out': out['a_w_out'], 'a_norm_ffn': out['a_norm_ffn'], 'a_w_gu': out['a_w_gu'], 'a_w_down': out['a_w_down'], 'kv_norm': out['kv_norm'], 'w_kv': out['w_kv'], 'b_norm_mix': out['b_norm_mix'], 'b_w_q': out['b_w_q'], 'b_w_mem_kv': out['b_w_mem_kv'], 'b_w_out': out['b_w_out'], 'b_norm_ffn': out['b_norm_ffn'], 'b_w_gu': out['b_w_gu'], 'b_w_down': out['b_w_down'], 'final_norm': out['final_norm'], 'loss_target': out['loss_target'], 'm_mem_norm': out['m_mem_norm'], 'm_a_norm_mix': out['m_a_norm_mix'], 'm_a_w_in': out['m_a_w_in'], 'm_a_w_group': out['m_a_w_group'], 'm_a_scale': out['m_a_scale'], 'm_a_w_mem_kv': out['m_a_w_mem_kv'], 'm_a_w_out': out['m_a_w_out'], 'm_a_norm_ffn': out['m_a_norm_ffn'], 'm_a_w_gu': out['m_a_w_gu'], 'm_a_w_down': out['m_a_w_down'], 'm_kv_norm': out['m_kv_norm'], 'm_w_kv': out['m_w_kv'], 'm_b_norm_mix': out['m_b_norm_mix'], 'm_b_w_q': out['m_b_w_q'], 'm_b_w_mem_kv': out['m_b_w_mem_kv'], 'm_b_w_out': out['m_b_w_out'], 'm_b_norm_ffn': out['m_b_norm_ffn'], 'm_b_w_gu': out['m_b_w_gu'], 'm_b_w_down': out['m_b_w_down'], 'm_final_norm': out['m_final_norm'], 'v_mem_norm': out['v_mem_norm'], 'v_a_norm_mix': out['v_a_norm_mix'], 'v_a_w_in': out['v_a_w_in'], 'v_a_w_group': out['v_a_w_group'], 'v_a_scale': out['v_a_scale'], 'v_a_w_mem_kv': out['v_a_w_mem_kv'], 'v_a_w_out': out['v_a_w_out'], 'v_a_norm_ffn': out['v_a_norm_ffn'], 'v_a_w_gu': out['v_a_w_gu'], 'v_a_w_down': out['v_a_w_down'], 'v_kv_norm': out['v_kv_norm'], 'v_w_kv': out['v_w_kv'], 'v_b_norm_mix': out['v_b_norm_mix'], 'v_b_w_q': out['v_b_w_q'], 'v_b_w_mem_kv': out['v_b_w_mem_kv'], 'v_b_w_out': out['v_b_w_out'], 'v_b_norm_ffn': out['v_b_norm_ffn'], 'v_b_w_gu': out['v_b_w_gu'], 'v_b_w_down': out['v_b_w_down'], 'v_final_norm': out['v_final_norm']}


def _loss(weights, diff, rest, loss_target):
    with _jax.named_scope("forward"):
        args = {**rest, TWIN_DIFF_INPUT: diff, **{k: w.astype(_WEIGHT_DTYPES[k]) for k, w in weights.items()}}
        y = _forward(args)
    with _jax.named_scope("loss_head"):
        err = _jnp.square(y.astype(_jnp.float32) - loss_target)
        return 0.5 * _jnp.sum(_jnp.mean(err, axis=-1)) if err.ndim else 0.5 * err


def _adamw(w, g, m, v):
    m = ADAM_B1 * m + (1.0 - ADAM_B1) * g
    v = ADAM_B2 * v + (1.0 - ADAM_B2) * _jnp.square(g)
    m_hat = m / (1.0 - ADAM_B1 ** ADAM_STEP)
    v_hat = v / (1.0 - ADAM_B2 ** ADAM_STEP)
    delta = -ADAM_LR * (m_hat / (_jnp.sqrt(v_hat) + ADAM_EPS) + ADAM_WD * w)
    return delta, m, v


def reference(x, mem, mem_norm, a_norm_mix, a_w_in, a_w_group, a_scale, a_w_mem_kv, a_w_out, a_norm_ffn, a_w_gu, a_w_down, kv_norm, w_kv, b_norm_mix, b_w_q, b_w_mem_kv, b_w_out, b_norm_ffn, b_w_gu, b_w_down, final_norm, loss_target, m_mem_norm, m_a_norm_mix, m_a_w_in, m_a_w_group, m_a_scale, m_a_w_mem_kv, m_a_w_out, m_a_norm_ffn, m_a_w_gu, m_a_w_down, m_kv_norm, m_w_kv, m_b_norm_mix, m_b_w_q, m_b_w_mem_kv, m_b_w_out, m_b_norm_ffn, m_b_w_gu, m_b_w_down, m_final_norm, v_mem_norm, v_a_norm_mix, v_a_w_in, v_a_w_group, v_a_scale, v_a_w_mem_kv, v_a_w_out, v_a_norm_ffn, v_a_w_gu, v_a_w_down, v_kv_norm, v_w_kv, v_b_norm_mix, v_b_w_q, v_b_w_mem_kv, v_b_w_out, v_b_norm_ffn, v_b_w_gu, v_b_w_down, v_final_norm):
    given = dict(x=x, mem=mem, mem_norm=mem_norm, a_norm_mix=a_norm_mix, a_w_in=a_w_in, a_w_group=a_w_group, a_scale=a_scale, a_w_mem_kv=a_w_mem_kv, a_w_out=a_w_out, a_norm_ffn=a_norm_ffn, a_w_gu=a_w_gu, a_w_down=a_w_down, kv_norm=kv_norm, w_kv=w_kv, b_norm_mix=b_norm_mix, b_w_q=b_w_q, b_w_mem_kv=b_w_mem_kv, b_w_out=b_w_out, b_norm_ffn=b_norm_ffn, b_w_gu=b_w_gu, b_w_down=b_w_down, final_norm=final_norm, loss_target=loss_target, m_mem_norm=m_mem_norm, m_a_norm_mix=m_a_norm_mix, m_a_w_in=m_a_w_in, m_a_w_group=m_a_w_group, m_a_scale=m_a_scale, m_a_w_mem_kv=m_a_w_mem_kv, m_a_w_out=m_a_w_out, m_a_norm_ffn=m_a_norm_ffn, m_a_w_gu=m_a_w_gu, m_a_w_down=m_a_w_down, m_kv_norm=m_kv_norm, m_w_kv=m_w_kv, m_b_norm_mix=m_b_norm_mix, m_b_w_q=m_b_w_q, m_b_w_mem_kv=m_b_w_mem_kv, m_b_w_out=m_b_w_out, m_b_norm_ffn=m_b_norm_ffn, m_b_w_gu=m_b_w_gu, m_b_w_down=m_b_w_down, m_final_norm=m_final_norm, v_mem_norm=v_mem_norm, v_a_norm_mix=v_a_norm_mix, v_a_w_in=v_a_w_in, v_a_w_group=v_a_w_group, v_a_scale=v_a_scale, v_a_w_mem_kv=v_a_w_mem_kv, v_a_w_out=v_a_w_out, v_a_norm_ffn=v_a_norm_ffn, v_a_w_gu=v_a_w_gu, v_a_w_down=v_a_w_down, v_kv_norm=v_kv_norm, v_w_kv=v_w_kv, v_b_norm_mix=v_b_norm_mix, v_b_w_q=v_b_w_q, v_b_w_mem_kv=v_b_w_mem_kv, v_b_w_out=v_b_w_out, v_b_norm_ffn=v_b_norm_ffn, v_b_w_gu=v_b_w_gu, v_b_w_down=v_b_w_down, v_final_norm=v_final_norm)
    weights = {n: given[n] for n in TWIN_WEIGHTS}
    shared = {n: given[n] for n in SHARED_INPUTS}
    per_example = {n: given[n] for n in ['x', 'mem']}
    grad_fn = _jax.value_and_grad(_loss, argnums=(0, 1))

    def one_microbatch(ex, loss_target):
        ex = dict(ex)
        diff = ex.pop(TWIN_DIFF_INPUT)
        return grad_fn(weights, diff, {**shared, **ex}, loss_target)

    if N_MICROBATCH == 1:
        loss, (grad_w, grad_x) = one_microbatch(per_example, given["loss_target"])
    else:
        def body(carry, xs):
            loss_sum, grad_sum = carry
            l_k, (gw_k, gx_k) = one_microbatch(xs[0], xs[1])
            with _jax.named_scope("update"):
                return (loss_sum + l_k, _jax.tree.map(_jnp.add, grad_sum, gw_k)), gx_k

        init = (_jnp.zeros((), _jnp.float32), _jax.tree.map(_jnp.zeros_like, weights))
        (loss, grad_w), grad_x = _jax.lax.scan(body, init, (per_example, given["loss_target"]))
    with _jax.named_scope("update"):
        delta_w, new_m, new_v = {}, {}, {}
        for n in TWIN_WEIGHTS:
            delta_w[n], new_m[n], new_v[n] = _adamw(weights[n], grad_w[n], given["m_" + n], given["v_" + n])
    return (loss, grad_x, *[grad_w[n] for n in TWIN_WEIGHTS], *[delta_w[n] for n in TWIN_WEIGHTS],
            *[new_m[n] for n in TWIN_WEIGHTS], *[new_v[n] for n in TWIN_WEIGHTS])
```

```python
import functools

import jax
import jax.numpy as jnp
from jax import lax
from jax.experimental import pallas as pl
from jax.experimental.pallas import tpu as pltpu

F32 = jnp.float32
BF16 = jnp.bfloat16

N_DEV = 8
HEAD_DIM = 64
N_SB_HEADS = 12
SB_WIDTH = N_SB_HEADS * HEAD_DIM
MEM_WIDTH = 4 * HEAD_DIM
POOL_WINDOWS = (2, 4, 8, 16)
POOL_GROUP = SB_WIDTH // 4
POOL_HALO = 16
EPS = 1e-6
QK_SCALE = HEAD_DIM ** -0.5
LANES = 128
PACK_COLS = 1024

ADAM_LR = 0.001
ADAM_B1 = 0.9
ADAM_B2 = 0.999
ADAM_EPS = 1e-08
ADAM_WD = 0.01
ADAM_STEP = 10

VMEM_LIMIT = 48 * 1024 * 1024

_NT = (((1,), (1,)), ((), ()))
_TN = (((0,), (0,)), ((), ()))
_NN = (((1,), (0,)), ((), ()))


def _params(*sem):
    return pltpu.CompilerParams(dimension_semantics=sem, vmem_limit_bytes=VMEM_LIMIT)


def _dot(a, b, dims=_NN):
    return lax.dot_general(a, b, dims, preferred_element_type=F32)


def exchange(a2a, ag, name):
    n_a2a, n_arr = len(a2a), len(a2a) + len(ag)

    def body(*refs):
        ins, outs = refs[:n_arr], refs[n_arr:2 * n_arr]
        send_sems, recv_sems, local_sems = refs[2 * n_arr:]
        x, y, c = lax.axis_index("x"), lax.axis_index("y"), lax.axis_index("c")
        me = 4 * x + 2 * y + c

        def peer_of(k):
            px = 1 - x if k & 4 else x
            py = 1 - y if k & 2 else y
            pc = 1 - c if k & 1 else c
            return (px, py, pc), 4 * px + 2 * py + pc

        def remote(a, k):
            peer, pid = peer_of(k)
            src = ins[a].at[pid] if a < n_a2a else ins[a]
            return pltpu.make_async_remote_copy(
                src_ref=src, dst_ref=outs[a].at[me], send_sem=send_sems.at[a, k - 1],
                recv_sem=recv_sems.at[a, k - 1], device_id=peer, device_id_type=pl.DeviceIdType.MESH)

        def arrival(a, k):
            _, pid = peer_of(k)
            src = ins[a].at[pid] if a < n_a2a else ins[a]
            return pltpu.make_async_remote_copy(
                src_ref=src, dst_ref=outs[a].at[pid], send_sem=send_sems.at[a, k - 1],
                recv_sem=recv_sems.at[a, k - 1], device_id=peer_of(k)[0], device_id_type=pl.DeviceIdType.MESH)

        def local(a):
            src = ins[a].at[me] if a < n_a2a else ins[a]
            return pltpu.make_async_copy(src, outs[a].at[me], local_sems.at[a])

        for a in range(n_arr):
            local(a).start()
        order = (6, 7, 4, 5, 2, 3, 1)
        for k in order:
            for a in range(n_arr):
                remote(a, k).start()
        for k in order:
            for a in range(n_arr):
                arrival(a, k).wait_recv()
        for k in order:
            for a in range(n_arr):
                remote(a, k).wait_send()
        for a in range(n_arr):
            local(a).wait()

    arrays = list(a2a) + list(ag)
    out_shape = [jax.ShapeDtypeStruct(v.shape, v.dtype) for v in a2a]
    out_shape += [jax.ShapeDtypeStruct((N_DEV,) + v.shape, v.dtype) for v in ag]
    any_spec = pl.BlockSpec(memory_space=pl.ANY)
    return pl.pallas_call(
        body, name=name, out_shape=out_shape,
        in_specs=[any_spec] * n_arr, out_specs=[any_spec] * n_arr,
        scratch_shapes=[pltpu.SemaphoreType.DMA((n_arr, N_DEV - 1)), pltpu.SemaphoreType.DMA((n_arr, N_DEV - 1)),
                        pltpu.SemaphoreType.DMA((n_arr,))],
    )(*arrays)


def _rows_tile(n, want):
    t = min(want, n)
    assert n % t == 0, (n, t)
    return t


def rmsnorm_fwd(x, g, name):
    s, d = x.shape
    tm = _rows_tile(s, 512)

    def body(x_ref, g_ref, h_ref):
        xv = x_ref[...]
        r = lax.rsqrt(jnp.mean(xv * xv, axis=-1, keepdims=True) + EPS)
        h_ref[...] = ((xv * r) * g_ref[...]).astype(h_ref.dtype)

    return pl.pallas_call(
        body, name=name, grid=(s // tm,),
        in_specs=[pl.BlockSpec((tm, d), lambda i: (i, 0)), pl.BlockSpec((1, d), lambda i: (0, 0))],
        out_specs=pl.BlockSpec((tm, d), lambda i: (i, 0)),
        out_shape=jax.ShapeDtypeStruct((s, d), BF16),
        compiler_params=_params("parallel"),
    )(x, g.reshape(1, d))


def rmsnorm_bwd(x, dres, pairs, name):
    s, d = x.shape
    tm = _rows_tile(s, 256)
    n = len(pairs)

    def body(*refs):
        x_ref, dres_ref = refs[0], refs[1]
        g_refs, dh_refs = refs[2:2 + n], refs[2 + n:2 + 2 * n]
        dx_ref, dxb_ref = refs[2 + 2 * n], refs[3 + 2 * n]
        dg_refs = refs[4 + 2 * n:]
        i = pl.program_id(0)
        xv = x_ref[...]
        r = lax.rsqrt(jnp.mean(xv * xv, axis=-1, keepdims=True) + EPS)
        xhat = xv * r
        acc = dres_ref[...]
        for k in range(n):
            dh = dh_refs[k][...].astype(F32)
            part = jnp.sum(dh * xhat, axis=0, keepdims=True)

            @pl.when(i == 0)
            def _(k=k, part=part):
                dg_refs[k][...] = part

            @pl.when(i > 0)
            def _(k=k, part=part):
                dg_refs[k][...] += part

            dxh = dh * g_refs[k][...]
            acc = acc + r * (dxh - xhat * jnp.mean(dxh * xhat, axis=-1, keepdims=True))
        dx_ref[...] = acc
        dxb_ref[...] = acc.astype(BF16)

    row = pl.BlockSpec((tm, d), lambda i: (i, 0))
    vec = pl.BlockSpec((1, d), lambda i: (0, 0))
    outs = pl.pallas_call(
        body, name=name, grid=(s // tm,),
        in_specs=[row, row] + [vec] * n + [row] * n,
        out_specs=[row, row] + [vec] * n,
        out_shape=[jax.ShapeDtypeStruct((s, d), F32), jax.ShapeDtypeStruct((s, d), BF16)]
        + [jax.ShapeDtypeStruct((1, d), F32)] * n,
        compiler_params=_params("arbitrary"),
    )(x, dres, *[g.reshape(1, d) for g, _ in pairs], *[dh for _, dh in pairs])
    return outs[0], outs[1], outs[2:]


def loss_head(x, tgt, g, name):
    s, d = x.shape
    tm = _rows_tile(s, 256)

    def body(x_ref, t_ref, g_ref, dx_ref, dxb_ref, dg_ref, loss_ref):
        i = pl.program_id(0)
        xv = x_ref[...]
        gv = g_ref[...]
        r = lax.rsqrt(jnp.mean(xv * xv, axis=-1, keepdims=True) + EPS)
        xhat = xv * r
        diff = xhat * gv - t_ref[...]
        part_loss = 0.5 * jnp.sum(jnp.mean(diff * diff, axis=-1, keepdims=True), axis=0, keepdims=True)
        dy = diff * (1.0 / d)
        part_dg = jnp.sum(dy * xhat, axis=0, keepdims=True)

        @pl.when(i == 0)
        def _():
            dg_ref[...] = part_dg
            loss_ref[...] = jnp.broadcast_to(part_loss, loss_ref.shape)

        @pl.when(i > 0)
        def _():
            dg_ref[...] += part_dg
            loss_ref[...] += jnp.broadcast_to(part_loss, loss_ref.shape)

        dxh = dy * gv
        dx = r * (dxh - xhat * jnp.mean(dxh * xhat, axis=-1, keepdims=True))
        dx_ref[...] = dx
        dxb_ref[...] = dx.astype(BF16)

    row = pl.BlockSpec((tm, d), lambda i: (i, 0))
    vec = pl.BlockSpec((1, d), lambda i: (0, 0))
    return pl.pallas_call(
        body, name=name, grid=(s // tm,),
        in_specs=[row, row, vec], out_specs=[row, row, vec, pl.BlockSpec((1, LANES), lambda i: (0, 0))],
        out_shape=[jax.ShapeDtypeStruct((s, d), F32), jax.ShapeDtypeStruct((s, d), BF16),
                   jax.ShapeDtypeStruct((1, d), F32), jax.ShapeDtypeStruct((1, LANES), F32)],
        compiler_params=_params("arbitrary"),
    )(x, tgt, g.reshape(1, d))


def matmul(pairs, mode, out_dtype, name, res=None, tm=512, tn=512):
    a0, b0 = pairs[0]
    m = a0.shape[1] if mode == "tn" else a0.shape[0]
    n = b0.shape[0] if mode == "nt" else b0.shape[1]
    tm, tn = _rows_tile(m, tm), _rows_tile(n, tn)
    dims = {"nn": _NN, "nt": _NT, "tn": _TN}[mode]
    npairs = len(pairs)

    def body(*refs):
        o_ref = refs[-1]
        acc = None
        for p in range(npairs):
            d = _dot(refs[2 * p][...].astype(BF16), refs[2 * p + 1][...].astype(BF16), dims)
            acc = d if acc is None else acc + d
        if res is not None:
            acc = acc + refs[2 * npairs][...]
        o_ref[...] = acc.astype(o_ref.dtype)

    in_specs, args = [], []
    for a, b in pairs:
        if mode == "tn":
            in_specs.append(pl.BlockSpec((a.shape[0], tm), lambda i, j: (0, i)))
        else:
            in_specs.append(pl.BlockSpec((tm, a.shape[1]), lambda i, j: (i, 0)))
        if mode == "nt":
            in_specs.append(pl.BlockSpec((tn, b.shape[1]), lambda i, j: (j, 0)))
        else:
            in_specs.append(pl.BlockSpec((b.shape[0], tn), lambda i, j: (0, j)))
        args += [a, b]
    if res is not None:
        in_specs.append(pl.BlockSpec((tm, tn), lambda i, j: (i, j)))
        args.append(res)
    return pl.pallas_call(
        body, name=name, grid=(m // tm, n // tn), in_specs=in_specs,
        out_specs=pl.BlockSpec((tm, tn), lambda i, j: (i, j)),
        out_shape=jax.ShapeDtypeStruct((m, n), out_dtype),
        compiler_params=_params("parallel", "parallel"),
    )(*args)


def _sigmoid(v):
    return 1.0 / (1.0 + jnp.exp(-v))


def ffn_up(h, wg, wu, name):
    s, d = h.shape
    f = wg.shape[1]
    tm, tn = _rows_tile(s, 512), _rows_tile(f, 256)

    def body(h_ref, wg_ref, wu_ref, gate_ref, up_ref, act_ref):
        hv = h_ref[...]
        gate = _dot(hv, wg_ref[...])
        up = _dot(hv, wu_ref[...])
        gate_ref[...] = gate
        up_ref[...] = up
        act_ref[...] = ((gate * _sigmoid(gate)) * up).astype(BF16)

    wspec = pl.BlockSpec((d, tn), lambda i, j: (0, j))
    tile = pl.BlockSpec((tm, tn), lambda i, j: (i, j))
    return pl.pallas_call(
        body, name=name, grid=(s // tm, f // tn),
        in_specs=[pl.BlockSpec((tm, d), lambda i, j: (i, 0)), wspec, wspec],
        out_specs=[tile, tile, tile],
        out_shape=[jax.ShapeDtypeStruct((s, f), F32), jax.ShapeDtypeStruct((s, f), F32),
                   jax.ShapeDtypeStruct((s, f), BF16)],
        compiler_params=_params("parallel", "parallel"),
    )(h, wg, wu)


def ffn_bwd_act(dxo, wdown, gate, up, name):
    s, d = dxo.shape
    f = wdown.shape[0]
    tm, tn = _rows_tile(s, 512), _rows_tile(f, 256)

    def body(dx_ref, w_ref, gate_ref, up_ref, dgate_ref, dup_ref):
        dact = _dot(dx_ref[...], w_ref[...], _NT)
        gate = gate_ref[...]
        sig = _sigmoid(gate)
        dup_ref[...] = (dact * (gate * sig)).astype(BF16)
        dgate_ref[...] = ((dact * up_ref[...]) * (sig * (1.0 + gate * (1.0 - sig)))).astype(BF16)

    tile = pl.BlockSpec((tm, tn), lambda i, j: (i, j))
    return pl.pallas_call(
        body, name=name, grid=(s // tm, f // tn),
        in_specs=[pl.BlockSpec((tm, d), lambda i, j: (i, 0)), pl.BlockSpec((tn, d), lambda i, j: (j, 0)), tile, tile],
        out_specs=[tile, tile],
        out_shape=[jax.ShapeDtypeStruct((s, f), BF16), jax.ShapeDtypeStruct((s, f), BF16)],
        compiler_params=_params("parallel", "parallel"),
    )(dxo, wdown, gate, up)


def _pool_select(col, by_window):
    out = by_window[3]
    for g in (2, 1, 0):
        out = jnp.where(col < (g + 1) * POOL_GROUP, by_window[g], out)
    return out


def _pool_counts(pos):
    return [jnp.minimum(pos + 1, w).astype(F32) for w in POOL_WINDOWS]


def pool_fwd(proj, wbd, scale, name):
    s = proj.shape[0]
    tm = _rows_tile(s, 256)
    nt = s // tm
    width = SB_WIDTH
    per = tm // POOL_HALO

    def body(u_ref, halo_ref, w_ref, sc_ref, pooled_ref, grouped_ref):
        i = pl.program_id(0)
        u = u_ref[...]
        halo = jnp.where(i > 0, halo_ref[...], 0.0)
        ext = jnp.concatenate([halo, u], axis=0)
        sums = []
        acc = ext
        for k in (1, 2, 4, 8):
            acc = acc + pltpu.roll(acc, k, 0)
            sums.append(acc[POOL_HALO:])
        pos = i * tm + lax.broadcasted_iota(jnp.int32, (tm, width), 0)
        col = lax.broadcasted_iota(jnp.int32, (tm, width), 1)
        means = [sm / cnt for sm, cnt in zip(sums, _pool_counts(pos))]
        pooled = (_pool_select(col, means) - u).astype(BF16)
        pooled_ref[...] = pooled
        grouped_ref[...] = (_dot(pooled, w_ref[...]) * sc_ref[...]).astype(BF16)

    row = pl.BlockSpec((tm, width), lambda i: (i, 0))
    return pl.pallas_call(
        body, name=name, grid=(nt,),
        in_specs=[row, pl.BlockSpec((POOL_HALO, width), lambda i: (jnp.maximum(i * per - 1, 0), 0)),
                  pl.BlockSpec((width, width), lambda i: (0, 0)), pl.BlockSpec((1, width), lambda i: (0, 0))],
        out_specs=[row, row],
        out_shape=[jax.ShapeDtypeStruct((s, width), BF16), jax.ShapeDtypeStruct((s, width), BF16)],
        compiler_params=_params("parallel"),
    )(proj, proj, wbd, scale.reshape(1, width))


def pool_bwd(dcat, pooled, wbd, scale, name):
    s = dcat.shape[0]
    tm = _rows_tile(s, 256)
    nt = s // tm
    width = SB_WIDTH
    per = tm // POOL_HALO
    n_ext = tm + POOL_HALO

    def body(dg_ref, halo_ref, pooled_ref, w_ref, sc_ref, du_ref, dw_ref, dsc_ref):
        i = pl.program_id(0)
        w = w_ref[...]
        sc = sc_ref[...]
        dg = dg_ref[...]
        pooled = pooled_ref[...]
        pg = _dot(pooled, w)
        dpg = (dg * sc).astype(BF16)
        part_sc = jnp.sum(dg * pg, axis=0, keepdims=True)
        part_w = _dot(pooled, dpg, _TN)

        @pl.when(i == 0)
        def _():
            dsc_ref[...] = part_sc
            dw_ref[...] = part_w

        @pl.when(i > 0)
        def _():
            dsc_ref[...] += part_sc
            dw_ref[...] += part_w

        dpooled = _dot(dpg, w, _NT)
        halo_dpg = (jnp.where(i < nt - 1, halo_ref[...], 0.0) * sc).astype(BF16)
        ext = jnp.concatenate([dpooled, _dot(halo_dpg, w, _NT)], axis=0)
        pos = i * tm + lax.broadcasted_iota(jnp.int32, (n_ext, width), 0)
        col = lax.broadcasted_iota(jnp.int32, (tm, width), 1)
        outs = []
        for k_idx, cnt in enumerate(_pool_counts(pos)):
            acc = ext / cnt
            for k in (1, 2, 4, 8)[:k_idx + 1]:
                acc = acc + pltpu.roll(acc, n_ext - k, 0)
            outs.append(acc[:tm])
        du_ref[...] = (_pool_select(col, outs) - dpooled).astype(BF16)

    row = pl.BlockSpec((tm, width), lambda i: (i, 0))
    return pl.pallas_call(
        body, name=name, grid=(nt,),
        in_specs=[row, pl.BlockSpec((POOL_HALO, width), lambda i: (jnp.minimum((i + 1) * per, s // POOL_HALO - 1), 0)),
                  row, pl.BlockSpec((width, width), lambda i: (0, 0)), pl.BlockSpec((1, width), lambda i: (0, 0))],
        out_specs=[row, pl.BlockSpec((width, width), lambda i: (0, 0)), pl.BlockSpec((1, width), lambda i: (0, 0))],
        out_shape=[jax.ShapeDtypeStruct((s, width), BF16), jax.ShapeDtypeStruct((width, width), F32),
                   jax.ShapeDtypeStruct((1, width), F32)],
        compiler_params=_params("arbitrary"),
    )(dcat, dcat, pooled, wbd, scale.reshape(1, width))


def _head_masks(shape):
    lane = lax.broadcasted_iota(jnp.int32, shape, 1)
    return lane < HEAD_DIM, lane >= HEAD_DIM


def _mem_probs(qh, kp):
    logits = _dot(qh, kp, _NT)
    logits = logits - jnp.max(logits, axis=-1, keepdims=True)
    p = jnp.exp(logits)
    return p / jnp.sum(p, axis=-1, keepdims=True)


def mem_attn_fwd(proj, mkv, name):
    s = proj.shape[0]
    m = mkv.shape[0]
    tm = _rows_tile(s, 256)
    q_blk = SB_WIDTH // MEM_WIDTH

    def body(q_ref, kv_ref, o_ref):
        masks = _head_masks((tm, LANES))
        for pr in range(MEM_WIDTH // LANES):
            q = q_ref[:, pr * LANES:(pr + 1) * LANES] * QK_SCALE
            kp = kv_ref[:, pr * LANES:(pr + 1) * LANES]
            vp = kv_ref[:, MEM_WIDTH + pr * LANES:MEM_WIDTH + (pr + 1) * LANES]
            outs = []
            for hm in masks:
                p = _mem_probs(jnp.where(hm, q, 0.0).astype(BF16), kp)
                outs.append(_dot(p.astype(BF16), vp))
            o_ref[:, pr * LANES:(pr + 1) * LANES] = jnp.where(masks[0], outs[0], outs[1]).astype(BF16)

    return pl.pallas_call(
        body, name=name, grid=(s // tm,),
        in_specs=[pl.BlockSpec((tm, MEM_WIDTH), lambda i: (i, q_blk)), pl.BlockSpec((m, 2 * MEM_WIDTH), lambda i: (0, 0))],
        out_specs=pl.BlockSpec((tm, MEM_WIDTH), lambda i: (i, 0)),
        out_shape=jax.ShapeDtypeStruct((s, MEM_WIDTH), BF16),
        compiler_params=_params("parallel"),
    )(proj, mkv)


def mem_attn_bwd(proj, mkv, dcat, name):
    s = proj.shape[0]
    m = mkv.shape[0]
    tm = _rows_tile(s, 256)
    q_blk = SB_WIDTH // MEM_WIDTH

    def body(q_ref, kv_ref, do_ref, dq_ref, dkv_ref):
        i = pl.program_id(0)

        @pl.when(i == 0)
        def _():
            dkv_ref[...] = jnp.zeros_like(dkv_ref)

        masks = _head_masks((tm, LANES))
        for pr in range(MEM_WIDTH // LANES):
            q = q_ref[:, pr * LANES:(pr + 1) * LANES] * QK_SCALE
            do = do_ref[:, pr * LANES:(pr + 1) * LANES]
            kp = kv_ref[:, pr * LANES:(pr + 1) * LANES]
            vp = kv_ref[:, MEM_WIDTH + pr * LANES:MEM_WIDTH + (pr + 1) * LANES]
            dqs = []
            dk = jnp.zeros((m, LANES), F32)
            dv = jnp.zeros((m, LANES), F32)
            for hm in masks:
                qh = jnp.where(hm, q, 0.0).astype(BF16)
                doh = jnp.where(hm, do, 0.0).astype(BF16)
                p = _mem_probs(qh, kp)
                dp = _dot(doh, vp, _NT)
                ds = (p * (dp - jnp.sum(dp * p, axis=-1, keepdims=True))).astype(BF16)
                dqs.append(_dot(ds, kp))
                dk = dk + _dot(ds, qh, _TN)
                dv = dv + _dot(p.astype(BF16), doh, _TN)
            dq_ref[:, pr * LANES:(pr + 1) * LANES] = (jnp.where(masks[0], dqs[0], dqs[1]) * QK_SCALE).astype(BF16)
            dkv_ref[:, pr * LANES:(pr + 1) * LANES] += dk
            dkv_ref[:, MEM_WIDTH + pr * LANES:MEM_WIDTH + (pr + 1) * LANES] += dv

    return pl.pallas_call(
        body, name=name, grid=(s // tm,),
        in_specs=[pl.BlockSpec((tm, MEM_WIDTH), lambda i: (i, q_blk)), pl.BlockSpec((m, 2 * MEM_WIDTH), lambda i: (0, 0)),
                  pl.BlockSpec((tm, MEM_WIDTH), lambda i: (i, q_blk))],
        out_specs=[pl.BlockSpec((tm, MEM_WIDTH), lambda i: (i, 0)), pl.BlockSpec((m, 2 * MEM_WIDTH), lambda i: (0, 0))],
        out_shape=[jax.ShapeDtypeStruct((s, MEM_WIDTH), BF16), jax.ShapeDtypeStruct((m, 2 * MEM_WIDTH), F32)],
        compiler_params=_params("arbitrary"),
    )(proj, mkv, dcat)


def _sb_block_terms(qh, k_blk, valid):
    z = _dot(qh, k_blk, _NT)
    ez = jnp.exp(-jnp.abs(z))
    sp = jnp.maximum(z, 0.0) + jnp.log(1.0 + ez)
    log_not = -sp
    if valid is not None:
        log_not = jnp.where(valid, log_not, 0.0)
    return z, ez, log_not, z - sp


def _split_dot(v, tri, dims):
    hi = v.astype(BF16)
    lo = (v - hi.astype(F32)).astype(BF16)
    return _dot(hi, tri, dims) + _dot(lo, tri, dims)


def sb_fwd(projb, kv, name):
    s = projb.shape[0]
    tq = _rows_tile(s, LANES)
    nb = s // tq
    n_pairs = SB_WIDTH // LANES

    def body(q_ref, k_ref, v_ref, o_ref, c_ref):
        i = pl.program_id(1)
        masks = _head_masks((tq, LANES))
        qs = q_ref[...] * QK_SCALE
        row = lax.broadcasted_iota(jnp.int32, (tq, tq), 0)
        col = lax.broadcasted_iota(jnp.int32, (tq, tq), 1)
        valid = row > col
        tri = valid.astype(BF16)
        lane_b = lax.broadcasted_iota(jnp.int32, (tq, nb), 1)
        accs = []
        for hh in range(2):
            qh = jnp.where(masks[hh], qs, 0.0).astype(BF16)

            def block(j, carry, mask):
                c, acc, cmat = carry
                off = pl.multiple_of(j * tq, tq)
                _, _, log_not, log_beta = _sb_block_terms(qh, k_ref[pl.ds(off, tq), :], mask)
                later = _split_dot(log_not, tri, _NN) + c
                w = jnp.exp(log_beta + later)
                if mask is not None:
                    w = jnp.where(mask, w, 0.0)
                acc = acc + _dot(w.astype(BF16), v_ref[pl.ds(off, tq), :])
                cmat = jnp.where(lane_b == j, c, cmat)
                return c + jnp.sum(log_not, axis=1, keepdims=True), acc, cmat

            carry = (jnp.zeros((tq, 1), F32), jnp.zeros((tq, LANES), F32), jnp.zeros((tq, nb), F32))
            carry = block(i, carry, valid)
            carry = lax.fori_loop(0, i, lambda jj, cr: block(i - 1 - jj, cr, None), carry)
            accs.append(carry[1])
            c_ref[hh] = carry[2]
        o_ref[...] = jnp.where(masks[0], accs[0], accs[1]).astype(BF16)

    return pl.pallas_call(
        body, name=name, grid=(n_pairs, nb),
        in_specs=[pl.BlockSpec((tq, LANES), lambda p, i: (i, p)),
                  pl.BlockSpec((s, LANES), lambda p, i: (0, p)),
                  pl.BlockSpec((s, LANES), lambda p, i: (0, n_pairs + p))],
        out_specs=[pl.BlockSpec((tq, LANES), lambda p, i: (i, p)),
                   pl.BlockSpec((2, tq, nb), lambda p, i: (p, i, 0))],
        out_shape=[jax.ShapeDtypeStruct((s, SB_WIDTH), BF16), jax.ShapeDtypeStruct((N_SB_HEADS, s, nb), F32)],
        compiler_params=_params("parallel", "parallel"),
    )(projb, kv, kv)


def sb_bwd(projb, kv, dcat, csave, name):
    s = projb.shape[0]
    tq = _rows_tile(s, LANES)
    nb = s // tq
    n_pairs = SB_WIDTH // LANES

    def body(q_ref, k_ref, v_ref, do_ref, c_ref, dq_ref, dk_ref, dv_ref):
        i = pl.program_id(1)

        @pl.when(i == 0)
        def _():
            dk_ref[...] = jnp.zeros_like(dk_ref)
            dv_ref[...] = jnp.zeros_like(dv_ref)

        masks = _head_masks((tq, LANES))
        qs = q_ref[...] * QK_SCALE
        do = do_ref[...]
        row = lax.broadcasted_iota(jnp.int32, (tq, tq), 0)
        col = lax.broadcasted_iota(jnp.int32, (tq, tq), 1)
        valid = row > col
        tri = valid.astype(BF16)
        lane_b = lax.broadcasted_iota(jnp.int32, (tq, nb), 1)
        dqs = []
        for hh in range(2):
            qh = jnp.where(masks[hh], qs, 0.0).astype(BF16)
            doh = jnp.where(masks[hh], do, 0.0).astype(BF16)
            cmat = c_ref[hh]

            def block(j, carry, mask):
                e_before, dq = carry
                off = pl.multiple_of(j * tq, tq)
                k_blk = k_ref[pl.ds(off, tq), :]
                z, ez, log_not, log_beta = _sb_block_terms(qh, k_blk, mask)
                c = jnp.sum(jnp.where(lane_b == j, cmat, 0.0), axis=1, keepdims=True)
                w = jnp.exp(log_beta + _split_dot(log_not, tri, _NN) + c)
                if mask is not None:
                    w = jnp.where(mask, w, 0.0)
                dv_ref[pl.ds(off, tq), :] += _dot(w.astype(BF16), doh, _TN)
                e = w * _dot(doh, v_ref[pl.ds(off, tq), :], _NT)
                e_pre = _split_dot(e, tri, _NT) + e_before
                inv = 1.0 / (1.0 + ez)
                sig = jnp.where(z >= 0.0, inv, ez * inv)
                dz = e * (1.0 - sig) - sig * e_pre
                if mask is not None:
                    dz = jnp.where(mask, dz, 0.0)
                dz = dz.astype(BF16)
                dk_ref[pl.ds(off, tq), :] += _dot(dz, qh, _TN)
                return e_before + jnp.sum(e, axis=1, keepdims=True), dq + _dot(dz, k_blk)

            carry = (jnp.zeros((tq, 1), F32), jnp.zeros((tq, LANES), F32))
            carry = lax.fori_loop(0, i, lambda j, cr: block(j, cr, None), carry)
            carry = block(i, carry, valid)
            dqs.append(carry[1])
        dq_ref[...] = (jnp.where(masks[0], dqs[0], dqs[1]) * QK_SCALE).astype(BF16)

    tile = pl.BlockSpec((tq, LANES), lambda p, i: (i, p))
    full = pl.BlockSpec((s, LANES), lambda p, i: (0, p))
    return pl.pallas_call(
        body, name=name, grid=(n_pairs, nb),
        in_specs=[tile, full, pl.BlockSpec((s, LANES), lambda p, i: (0, n_pairs + p)), tile,
                  pl.BlockSpec((2, tq, nb), lambda p, i: (p, i, 0))],
        out_specs=[tile, full, full],
        out_shape=[jax.ShapeDtypeStruct((s, SB_WIDTH), BF16), jax.ShapeDtypeStruct((s, SB_WIDTH), F32),
                   jax.ShapeDtypeStruct((s, SB_WIDTH), F32)],
        compiler_params=_params("parallel", "arbitrary"),
    )(projb, kv, kv, dcat, csave)


def reduce_adamw(parts, w, m, v, name):
    n, r, c = parts.shape
    tr = 128 if r % 128 == 0 else r
    bias1 = 1.0 - ADAM_B1 ** ADAM_STEP
    bias2 = 1.0 - ADAM_B2 ** ADAM_STEP

    def body(p_ref, w_ref, m_ref, v_ref, g_ref, d_ref, nm_ref, nv_ref):
        g = p_ref[0].astype(F32)
        for k in range(1, n):
            g = g + p_ref[k].astype(F32)
        new_m = ADAM_B1 * m_ref[...] + (1.0 - ADAM_B1) * g
        new_v = ADAM_B2 * v_ref[...] + (1.0 - ADAM_B2) * (g * g)
        m_hat = new_m / bias1
        v_hat = new_v / bias2
        g_ref[...] = g
        d_ref[...] = -ADAM_LR * (m_hat / (jnp.sqrt(v_hat) + ADAM_EPS) + ADAM_WD * w_ref[...])
        nm_ref[...] = new_m
        nv_ref[...] = new_v

    row = pl.BlockSpec((tr, c), lambda i: (i, 0))
    return pl.pallas_call(
        body, name=name, grid=(r // tr,),
        in_specs=[pl.BlockSpec((n, tr, c), lambda i: (0, i, 0)), row, row, row],
        out_specs=[row, row, row, row],
        out_shape=[jax.ShapeDtypeStruct((r, c), F32)] * 4,
        compiler_params=_params("parallel"),
    )(parts, w, m, v)


BIG = ("a_w_in", "a_w_mem_kv", "a_w_out", "a_w_gu", "a_w_down", "w_kv", "b_w_q", "b_w_mem_kv", "b_w_out", "b_w_gu",
       "b_w_down")
COL_SHARDED = ("a_w_gu", "w_kv", "b_w_gu")
SMALL_SHARDED = ("a_norm_mix", "a_scale", "a_norm_ffn")
SMALL_REPL = ("mem_norm", "kv_norm", "b_norm_mix", "b_norm_ffn", "final_norm")
WEIGHTS = ("mem_norm", "a_norm_mix", "a_w_in", "a_w_group", "a_scale", "a_w_mem_kv", "a_w_out", "a_norm_ffn", "a_w_gu",
           "a_w_down", "kv_norm", "w_kv", "b_norm_mix", "b_w_q", "b_w_mem_kv", "b_w_out", "b_norm_ffn", "b_w_gu",
           "b_w_down", "final_norm")


def _pack_rows(arrays):
    return jnp.concatenate([a.reshape(-1, PACK_COLS) for a in arrays], axis=0)


def _full_from_gathered(name, g, shard_shape):
    r, c = shard_shape
    g = g.reshape(N_DEV, r, c)
    if name in COL_SHARDED:
        return jnp.transpose(g, (1, 0, 2)).reshape(r, N_DEV * c)
    return g.reshape(N_DEV * r, c)


def _blocks_of_full(name, full, shard_shape):
    r, c = shard_shape
    if name in COL_SHARDED:
        g = jnp.transpose(full.reshape(r, N_DEV, c), (1, 0, 2))
    else:
        g = full.reshape(N_DEV, r, c)
    return g.reshape(N_DEV, -1, PACK_COLS)


def _pad_row(v):
    v = v.reshape(1, -1)
    return jnp.pad(v, ((0, 0), (0, PACK_COLS - v.shape[1])))


def kernel(x, mem, mem_norm, a_norm_mix, a_w_in, a_w_group, a_scale, a_w_mem_kv, a_w_out, a_norm_ffn, a_w_gu, a_w_down, kv_norm, w_kv, b_norm_mix, b_w_q, b_w_mem_kv, b_w_out, b_norm_ffn, b_w_gu, b_w_down, final_norm, loss_target, m_mem_norm, m_a_norm_mix, m_a_w_in, m_a_w_group, m_a_scale, m_a_w_mem_kv, m_a_w_out, m_a_norm_ffn, m_a_w_gu, m_a_w_down, m_kv_norm, m_w_kv, m_b_norm_mix, m_b_w_q, m_b_w_mem_kv, m_b_w_out, m_b_norm_ffn, m_b_w_gu, m_b_w_down, m_final_norm, v_mem_norm, v_a_norm_mix, v_a_w_in, v_a_w_group, v_a_scale, v_a_w_mem_kv, v_a_w_out, v_a_norm_ffn, v_a_w_gu, v_a_w_down, v_kv_norm, v_w_kv, v_b_norm_mix, v_b_w_q, v_b_w_mem_kv, v_b_w_out, v_b_norm_ffn, v_b_w_gu, v_b_w_down, v_final_norm):
    w = dict(mem_norm=mem_norm, a_norm_mix=a_norm_mix, a_w_in=a_w_in, a_w_group=a_w_group, a_scale=a_scale,
             a_w_mem_kv=a_w_mem_kv, a_w_out=a_w_out, a_norm_ffn=a_norm_ffn, a_w_gu=a_w_gu, a_w_down=a_w_down,
             kv_norm=kv_norm, w_kv=w_kv, b_norm_mix=b_norm_mix, b_w_q=b_w_q, b_w_mem_kv=b_w_mem_kv, b_w_out=b_w_out,
             b_norm_ffn=b_norm_ffn, b_w_gu=b_w_gu, b_w_down=b_w_down, final_norm=final_norm)
    mom = dict(mem_norm=m_mem_norm, a_norm_mix=m_a_norm_mix, a_w_in=m_a_w_in, a_w_group=m_a_w_group, a_scale=m_a_scale,
               a_w_mem_kv=m_a_w_mem_kv, a_w_out=m_a_w_out, a_norm_ffn=m_a_norm_ffn, a_w_gu=m_a_w_gu,
               a_w_down=m_a_w_down, kv_norm=m_kv_norm, w_kv=m_w_kv, b_norm_mix=m_b_norm_mix, b_w_q=m_b_w_q,
               b_w_mem_kv=m_b_w_mem_kv, b_w_out=m_b_w_out, b_norm_ffn=m_b_norm_ffn, b_w_gu=m_b_w_gu,
               b_w_down=m_b_w_down, final_norm=m_final_norm)
    var = dict(mem_norm=v_mem_norm, a_norm_mix=v_a_norm_mix, a_w_in=v_a_w_in, a_w_group=v_a_w_group, a_scale=v_a_scale,
               a_w_mem_kv=v_a_w_mem_kv, a_w_out=v_a_w_out, a_norm_ffn=v_a_norm_ffn, a_w_gu=v_a_w_gu,
               a_w_down=v_a_w_down, kv_norm=v_kv_norm, w_kv=v_w_kv, b_norm_mix=v_b_norm_mix, b_w_q=v_b_w_q,
               b_w_mem_kv=v_b_w_mem_kv, b_w_out=v_b_w_out, b_norm_ffn=v_b_norm_ffn, b_w_gu=v_b_w_gu,
               b_w_down=v_b_w_down, final_norm=v_final_norm)

    me = 4 * lax.axis_index("x") + 2 * lax.axis_index("y") + lax.axis_index("c")
    shard2d = {n: w[n].shape[-2:] for n in BIG}
    rows = {n: shard2d[n][0] * shard2d[n][1] // PACK_COLS for n in BIG}
    offs, o = {}, 0
    for n in BIG:
        offs[n] = o
        o += rows[n]

    big_send = _pack_rows([w[n].astype(BF16) for n in BIG])
    small_send = jnp.concatenate([_pad_row(w[n]) for n in SMALL_SHARDED]
                                 + [jnp.zeros((8 - len(SMALL_SHARDED), PACK_COLS), F32)], axis=0)
    big_all, small_all = exchange([], [big_send, small_send], "gather_weights")
    full = {n: _full_from_gathered(n, big_all[:, offs[n]:offs[n] + rows[n]], shard2d[n]) for n in BIG}
    a_norm_mix_f = small_all[:, 0, :a_norm_mix.shape[1]].reshape(-1)
    a_scale_f = small_all[:, 1, :a_scale.shape[1]].reshape(-1)
    a_norm_ffn_f = small_all[:, 2, :a_norm_ffn.shape[1]].reshape(-1)

    d_ff = full["a_w_down"].shape[0]
    wbd = jnp.zeros((SB_WIDTH, SB_WIDTH), BF16)
    for g in range(4):
        sl = slice(g * POOL_GROUP, (g + 1) * POOL_GROUP)
        wbd = wbd.at[sl, sl].set(a_w_group[0, g].astype(BF16))

    xs, mems, tgt = x[0], mem[0], loss_target[0]

    memn = rmsnorm_fwd(mems, mem_norm, "memn")
    h0 = rmsnorm_fwd(xs, a_norm_mix_f, "h0")
    proj = matmul([(h0, full["a_w_in"])], "nn", F32, "proj_a")
    mkv_a = matmul([(memn, full["a_w_mem_kv"])], "nn", BF16, "mkv_a")
    pooled, grouped = pool_fwd(proj, wbd, a_scale_f, "pool_fwd")
    memout_a = mem_attn_fwd(proj, mkv_a, "mem_fwd_a")
    cat_a = jnp.concatenate([grouped, memout_a], axis=1)
    x1 = matmul([(cat_a, full["a_w_out"])], "nn", F32, "out_a", res=xs)
    h1 = rmsnorm_fwd(x1, a_norm_ffn_f, "h1")
    wg_a, wu_a = full["a_w_gu"][:, :d_ff], full["a_w_gu"][:, d_ff:]
    gate_a, up_a, act_a = ffn_up(h1, wg_a, wu_a, "ffn_up_a")
    x2 = matmul([(act_a, full["a_w_down"])], "nn", F32, "down_a", res=x1)

    hk = rmsnorm_fwd(x2, kv_norm, "hk")
    kv = matmul([(hk, full["w_kv"])], "nn", BF16, "kv")
    h3 = rmsnorm_fwd(x2, b_norm_mix[0], "h3")
    projb = matmul([(h3, full["b_w_q"])], "nn", F32, "proj_b")
    mkv_b = matmul([(memn, full["b_w_mem_kv"])], "nn", BF16, "mkv_b")
    sb, csave = sb_fwd(projb, kv, "sb_fwd")
    memout_b = mem_attn_fwd(projb, mkv_b, "mem_fwd_b")
    cat_b = jnp.concatenate([sb, memout_b], axis=1)
    x3 = matmul([(cat_b, full["b_w_out"])], "nn", F32, "out_b", res=x2)
    h4 = rmsnorm_fwd(x3, b_norm_ffn[0], "h4")
    wg_b, wu_b = full["b_w_gu"][:, :d_ff], full["b_w_gu"][:, d_ff:]
    gate_b, up_b, act_b = ffn_up(h4, wg_b, wu_b, "ffn_up_b")
    x4 = matmul([(act_b, full["b_w_down"])], "nn", F32, "down_b", res=x3)

    grads = {}
    dx4, dx4_bf, grads["final_norm"], loss_part = loss_head(x4, tgt, final_norm, "loss_head")

    def ffn_backward(tag, x_in, dx_out, dx_out_bf, h, gate, up, act, wg, wu, wdown, g_ffn, extra_pairs=()):
        dgate, dup = ffn_bwd_act(dx_out_bf, wdown, gate, up, "ffn_bwd_act_" + tag)
        d_wdown = matmul([(act, dx_out_bf)], "tn", BF16, "dw_down_" + tag, tm=256)
        dh = matmul([(dgate, wg), (dup, wu)], "nt", F32, "dh_ffn_" + tag)
        d_wg = matmul([(h, dgate)], "tn", BF16, "dw_gate_" + tag, tn=256)
        d_wu = matmul([(h, dup)], "tn", BF16, "dw_up_" + tag, tn=256)
        dx_in, dx_in_bf, dgs = rmsnorm_bwd(x_in, dx_out, [(g_ffn, dh)] + list(extra_pairs), "norm_bwd_ffn_" + tag)
        return dx_in, dx_in_bf, dgs, d_wdown, jnp.concatenate([d_wg, d_wu], axis=1)

    dx3, dx3_bf, dgs, grads["b_w_down"], grads["b_w_gu"] = ffn_backward(
        "b", x3, dx4, dx4_bf, h4, gate_b, up_b, act_b, wg_b, wu_b, full["b_w_down"], b_norm_ffn[0])
    grads["b_norm_ffn"] = dgs[0]

    dcat_b = matmul([(dx3_bf, full["b_w_out"])], "nt", F32, "dcat_b")
    grads["b_w_out"] = matmul([(cat_b, dx3_bf)], "tn", BF16, "dw_out_b")
    dq, dk, dv = sb_bwd(projb, kv, dcat_b, csave, "sb_bwd")
    dqm_b, dmkv_b = mem_attn_bwd(projb, mkv_b, dcat_b, "mem_bwd_b")
    dprojb = jnp.concatenate([dq, dqm_b], axis=1)
    dh3 = matmul([(dprojb, full["b_w_q"])], "nt", F32, "dh3")
    grads["b_w_q"] = matmul([(h3, dprojb)], "tn", BF16, "dw_q_b")
    dkv = jnp.concatenate([dk, dv], axis=1).astype(BF16)
    dhk = matmul([(dkv, full["w_kv"])], "nt", F32, "dhk")
    grads["w_kv"] = matmul([(hk, dkv)], "tn", BF16, "dw_kv")
    dx2, dx2_bf, dgs = rmsnorm_bwd(x2, dx3, [(b_norm_mix[0], dh3), (kv_norm, dhk)], "norm_bwd_x2")
    grads["b_norm_mix"], grads["kv_norm"] = dgs
    dmkv_b_bf = dmkv_b.astype(BF16)
    grads["b_w_mem_kv"] = matmul([(memn, dmkv_b_bf)], "tn", BF16, "dw_mkv_b")

    dx1, dx1_bf, dgs, grads["a_w_down"], grads["a_w_gu"] = ffn_backward(
        "a", x1, dx2, dx2_bf, h1, gate_a, up_a, act_a, wg_a, wu_a, full["a_w_down"], a_norm_ffn_f)
    grads["a_norm_ffn"] = dgs[0]

    dcat_a = matmul([(dx1_bf, full["a_w_out"])], "nt", F32, "dcat_a")
    grads["a_w_out"] = matmul([(cat_a, dx1_bf)], "tn", BF16, "dw_out_a")
    du, d_wbd, grads["a_scale"] = pool_bwd(dcat_a, pooled, wbd, a_scale_f, "pool_bwd")
    grads["a_w_group"] = jnp.stack(
        [d_wbd[g * POOL_GROUP:(g + 1) * POOL_GROUP, g * POOL_GROUP:(g + 1) * POOL_GROUP] for g in range(4)])
    dqm_a, dmkv_a = mem_attn_bwd(proj, mkv_a, dcat_a, "mem_bwd_a")
    dproj = jnp.concatenate([du, dqm_a], axis=1)
    dh0 = matmul([(dproj, full["a_w_in"])], "nt", F32, "dh0")
    grads["a_w_in"] = matmul([(h0, dproj)], "tn", BF16, "dw_in_a")
    dx, _, dgs = rmsnorm_bwd(xs, dx1, [(a_norm_mix_f, dh0)], "norm_bwd_x")
    grads["a_norm_mix"] = dgs[0]
    dmkv_a_bf = dmkv_a.astype(BF16)
    grads["a_w_mem_kv"] = matmul([(memn, dmkv_a_bf)], "tn", BF16, "dw_mkv_a")
    dmemn = matmul([(dmkv_a_bf, full["a_w_mem_kv"]), (dmkv_b_bf, full["b_w_mem_kv"])], "nt", F32, "dmemn")
    _, _, dgs = rmsnorm_bwd(mems, jnp.zeros_like(mems), [(mem_norm, dmemn)], "norm_bwd_mem")
    grads["mem_norm"] = dgs[0]

    big_grads = jnp.concatenate([_blocks_of_full(n, grads[n], shard2d[n]) for n in BIG], axis=1)
    small_names = SMALL_REPL + SMALL_SHARDED
    small_rows = [_pad_row(grads[n]) for n in small_names] + [_pad_row(loss_part[:, :1])]
    small_rows.append(grads["a_w_group"].reshape(-1, PACK_COLS))
    n_small = sum(r.shape[0] for r in small_rows)
    pad_small = -n_small % 8
    small_grads = jnp.concatenate(small_rows + [jnp.zeros((pad_small, PACK_COLS), F32)], axis=0)
    big_recv, small_recv = exchange([big_grads], [small_grads], "exchange_grads")

    def place(v, n):
        v = v.reshape(1, -1)
        return lax.dynamic_update_slice(jnp.zeros((1, PACK_COLS), F32), v, (0, me * v.shape[1]))

    def small_pack(src):
        rws = [_pad_row(src[n]) for n in SMALL_REPL] + [place(src[n], n) for n in SMALL_SHARDED]
        rws += [jnp.zeros((1, PACK_COLS), F32), src["a_w_group"].reshape(-1, PACK_COLS),
                jnp.zeros((pad_small, PACK_COLS), F32)]
        return jnp.concatenate(rws, axis=0)

    big_out = reduce_adamw(big_recv, _pack_rows([w[n] for n in BIG]), _pack_rows([mom[n] for n in BIG]),
                           _pack_rows([var[n] for n in BIG]), "adamw_big")
    small_out = reduce_adamw(small_recv, small_pack(w), small_pack(mom), small_pack(var), "adamw_small")

    def unpack(kind):
        out = {}
        for n in BIG:
            out[n] = big_out[kind][offs[n]:offs[n] + rows[n]].reshape(w[n].shape)
        so = small_out[kind]
        for r, n in enumerate(SMALL_REPL):
            out[n] = so[r, :w[n].shape[-1]].reshape(w[n].shape)
        for r, n in enumerate(SMALL_SHARDED):
            width = w[n].shape[-1]
            out[n] = lax.dynamic_slice(so, (len(SMALL_REPL) + r, me * width), (1, width)).reshape(w[n].shape)
        r0 = len(small_names) + 1
        out["a_w_group"] = so[r0:r0 + a_w_group.size // PACK_COLS].reshape(a_w_group.shape)
        return out

    loss = small_out[0][len(small_names), 0]
    results = [loss, dx.reshape(x.shape)]
    for kind in range(4):
        out = unpack(kind)
        results += [out[n] for n in WEIGHTS]
    return tuple(results)
```

```python
import functools

import jax
import jax.numpy as jnp
from jax import lax
from jax.experimental import pallas as pl
from jax.experimental.pallas import tpu as pltpu

F32 = jnp.float32
BF16 = jnp.bfloat16

N_DEV = 8
HEAD_DIM = 64
N_SB_HEADS = 12
SB_WIDTH = N_SB_HEADS * HEAD_DIM
MEM_WIDTH = 4 * HEAD_DIM
POOL_WINDOWS = (2, 4, 8, 16)
POOL_GROUP = SB_WIDTH // 4
POOL_HALO = 16
EPS = 1e-6
QK_SCALE = HEAD_DIM ** -0.5
LANES = 128
PACK_COLS = 1024

ADAM_LR = 0.001
ADAM_B1 = 0.9
ADAM_B2 = 0.999
ADAM_EPS = 1e-08
ADAM_WD = 0.01
ADAM_STEP = 10

VMEM_LIMIT = 48 * 1024 * 1024

_NT = (((1,), (1,)), ((), ()))
_TN = (((0,), (0,)), ((), ()))
_NN = (((1,), (0,)), ((), ()))


def _params(*sem):
    return pltpu.CompilerParams(dimension_semantics=sem, vmem_limit_bytes=VMEM_LIMIT)


def _dot(a, b, dims=_NN):
    return lax.dot_general(a, b, dims, preferred_element_type=F32)


def exchange(a2a, ag, name):
    n_a2a, n_arr = len(a2a), len(a2a) + len(ag)

    def body(*refs):
        ins, outs = refs[:n_arr], refs[n_arr:2 * n_arr]
        send_sems, recv_sems, local_sems = refs[2 * n_arr:]
        x, y, c = lax.axis_index("x"), lax.axis_index("y"), lax.axis_index("c")
        me = 4 * x + 2 * y + c

        def peer_of(k):
            px = 1 - x if k & 4 else x
            py = 1 - y if k & 2 else y
            pc = 1 - c if k & 1 else c
            return (px, py, pc), 4 * px + 2 * py + pc

        def remote(a, k):
            peer, pid = peer_of(k)
            src = ins[a].at[pid] if a < n_a2a else ins[a]
            return pltpu.make_async_remote_copy(
                src_ref=src, dst_ref=outs[a].at[me], send_sem=send_sems.at[a, k - 1],
                recv_sem=recv_sems.at[a, k - 1], device_id=peer, device_id_type=pl.DeviceIdType.MESH)

        def arrival(a, k):
            _, pid = peer_of(k)
            src = ins[a].at[pid] if a < n_a2a else ins[a]
            return pltpu.make_async_remote_copy(
                src_ref=src, dst_ref=outs[a].at[pid], send_sem=send_sems.at[a, k - 1],
                recv_sem=recv_sems.at[a, k - 1], device_id=peer_of(k)[0], device_id_type=pl.DeviceIdType.MESH)

        def local(a):
            src = ins[a].at[me] if a < n_a2a else ins[a]
            return pltpu.make_async_copy(src, outs[a].at[me], local_sems.at[a])

        for a in range(n_arr):
            local(a).start()
        order = (6, 7, 4, 5, 2, 3, 1)
        for k in order:
            for a in range(n_arr):
                remote(a, k).start()
        for k in order:
            for a in range(n_arr):
                arrival(a, k).wait_recv()
        for k in order:
            for a in range(n_arr):
                remote(a, k).wait_send()
        for a in range(n_arr):
            local(a).wait()

    arrays = list(a2a) + list(ag)
    out_shape = [jax.ShapeDtypeStruct(v.shape, v.dtype) for v in a2a]
    out_shape += [jax.ShapeDtypeStruct((N_DEV,) + v.shape, v.dtype) for v in ag]
    any_spec = pl.BlockSpec(memory_space=pl.ANY)
    return pl.pallas_call(
        body, name=name, out_shape=out_shape,
        in_specs=[any_spec] * n_arr, out_specs=[any_spec] * n_arr,
        scratch_shapes=[pltpu.SemaphoreType.DMA((n_arr, N_DEV - 1)), pltpu.SemaphoreType.DMA((n_arr, N_DEV - 1)),
                        pltpu.SemaphoreType.DMA((n_arr,))],
    )(*arrays)


def _rows_tile(n, want):
    t = min(want, n)
    assert n % t == 0, (n, t)
    return t


def rmsnorm_fwd(x, g, name):
    s, d = x.shape
    tm = _rows_tile(s, 512)

    def body(x_ref, g_ref, h_ref):
        xv = x_ref[...]
        r = lax.rsqrt(jnp.mean(xv * xv, axis=-1, keepdims=True) + EPS)
        h_ref[...] = ((xv * r) * g_ref[...]).astype(h_ref.dtype)

    return pl.pallas_call(
        body, name=name, grid=(s // tm,),
        in_specs=[pl.BlockSpec((tm, d), lambda i: (i, 0)), pl.BlockSpec((1, d), lambda i: (0, 0))],
        out_specs=pl.BlockSpec((tm, d), lambda i: (i, 0)),
        out_shape=jax.ShapeDtypeStruct((s, d), BF16),
        compiler_params=_params("parallel"),
    )(x, g.reshape(1, d))


def rmsnorm_bwd(x, dres, pairs, name):
    s, d = x.shape
    tm = _rows_tile(s, 256)
    n = len(pairs)

    def body(*refs):
        x_ref, dres_ref = refs[0], refs[1]
        g_refs, dh_refs = refs[2:2 + n], refs[2 + n:2 + 2 * n]
        dx_ref, dxb_ref = refs[2 + 2 * n], refs[3 + 2 * n]
        dg_refs = refs[4 + 2 * n:]
        i = pl.program_id(0)
        xv = x_ref[...]
        r = lax.rsqrt(jnp.mean(xv * xv, axis=-1, keepdims=True) + EPS)
        xhat = xv * r
        acc = dres_ref[...]
        for k in range(n):
            dh = dh_refs[k][...].astype(F32)
            part = jnp.sum(dh * xhat, axis=0, keepdims=True)

            @pl.when(i == 0)
            def _(k=k, part=part):
                dg_refs[k][...] = part

            @pl.when(i > 0)
            def _(k=k, part=part):
                dg_refs[k][...] += part

            dxh = dh * g_refs[k][...]
            acc = acc + r * (dxh - xhat * jnp.mean(dxh * xhat, axis=-1, keepdims=True))
        dx_ref[...] = acc
        dxb_ref[...] = acc.astype(BF16)

    row = pl.BlockSpec((tm, d), lambda i: (i, 0))
    vec = pl.BlockSpec((1, d), lambda i: (0, 0))
    outs = pl.pallas_call(
        body, name=name, grid=(s // tm,),
        in_specs=[row, row] + [vec] * n + [row] * n,
        out_specs=[row, row] + [vec] * n,
        out_shape=[jax.ShapeDtypeStruct((s, d), F32), jax.ShapeDtypeStruct((s, d), BF16)]
        + [jax.ShapeDtypeStruct((1, d), F32)] * n,
        compiler_params=_params("arbitrary"),
    )(x, dres, *[g.reshape(1, d) for g, _ in pairs], *[dh for _, dh in pairs])
    return outs[0], outs[1], outs[2:]


def loss_head(x, tgt, g, name):
    s, d = x.shape
    tm = _rows_tile(s, 256)

    def body(x_ref, t_ref, g_ref, dx_ref, dxb_ref, dg_ref, loss_ref):
        i = pl.program_id(0)
        xv = x_ref[...]
        gv = g_ref[...]
        r = lax.rsqrt(jnp.mean(xv * xv, axis=-1, keepdims=True) + EPS)
        xhat = xv * r
        diff = xhat * gv - t_ref[...]
        part_loss = 0.5 * jnp.sum(jnp.mean(diff * diff, axis=-1, keepdims=True), axis=0, keepdims=True)
        dy = diff * (1.0 / d)
        part_dg = jnp.sum(dy * xhat, axis=0, keepdims=True)

        @pl.when(i == 0)
        def _():
            dg_ref[...] = part_dg
            loss_ref[...] = jnp.broadcast_to(part_loss, loss_ref.shape)

        @pl.when(i > 0)
        def _():
            dg_ref[...] += part_dg
            loss_ref[...] += jnp.broadcast_to(part_loss, loss_ref.shape)

        dxh = dy * gv
        dx = r * (dxh - xhat * jnp.mean(dxh * xhat, axis=-1, keepdims=True))
        dx_ref[...] = dx
        dxb_ref[...] = dx.astype(BF16)

    row = pl.BlockSpec((tm, d), lambda i: (i, 0))
    vec = pl.BlockSpec((1, d), lambda i: (0, 0))
    return pl.pallas_call(
        body, name=name, grid=(s // tm,),
        in_specs=[row, row, vec], out_specs=[row, row, vec, pl.BlockSpec((1, LANES), lambda i: (0, 0))],
        out_shape=[jax.ShapeDtypeStruct((s, d), F32), jax.ShapeDtypeStruct((s, d), BF16),
                   jax.ShapeDtypeStruct((1, d), F32), jax.ShapeDtypeStruct((1, LANES), F32)],
        compiler_params=_params("arbitrary"),
    )(x, tgt, g.reshape(1, d))


def matmul(pairs, mode, out_dtype, name, res=None, tm=512, tn=512):
    a0, b0 = pairs[0]
    m = a0.shape[1] if mode == "tn" else a0.shape[0]
    n = b0.shape[0] if mode == "nt" else b0.shape[1]
    tm, tn = _rows_tile(m, tm), _rows_tile(n, tn)
    dims = {"nn": _NN, "nt": _NT, "tn": _TN}[mode]
    npairs = len(pairs)

    def body(*refs):
        o_ref = refs[-1]
        acc = None
        for p in range(npairs):
            d = _dot(refs[2 * p][...].astype(BF16), refs[2 * p + 1][...].astype(BF16), dims)
            acc = d if acc is None else acc + d
        if res is not None:
            acc = acc + refs[2 * npairs][...]
        o_ref[...] = acc.astype(o_ref.dtype)

    in_specs, args = [], []
    for a, b in pairs:
        if mode == "tn":
            in_specs.append(pl.BlockSpec((a.shape[0], tm), lambda i, j: (0, i)))
        else:
            in_specs.append(pl.BlockSpec((tm, a.shape[1]), lambda i, j: (i, 0)))
        if mode == "nt":
            in_specs.append(pl.BlockSpec((tn, b.shape[1]), lambda i, j: (j, 0)))
        else:
            in_specs.append(pl.BlockSpec((b.shape[0], tn), lambda i, j: (0, j)))
        args += [a, b]
    if res is not None:
        in_specs.append(pl.BlockSpec((tm, tn), lambda i, j: (i, j)))
        args.append(res)
    return pl.pallas_call(
        body, name=name, grid=(m // tm, n // tn), in_specs=in_specs,
        out_specs=pl.BlockSpec((tm, tn), lambda i, j: (i, j)),
        out_shape=jax.ShapeDtypeStruct((m, n), out_dtype),
        compiler_params=_params("parallel", "parallel"),
    )(*args)


def _sigmoid(v):
    return 1.0 / (1.0 + jnp.exp(-v))


def ffn_up(h, wg, wu, name):
    s, d = h.shape
    f = wg.shape[1]
    tm, tn = _rows_tile(s, 512), _rows_tile(f, 256)

    def body(h_ref, wg_ref, wu_ref, gate_ref, up_ref, act_ref):
        hv = h_ref[...]
        gate = _dot(hv, wg_ref[...])
        up = _dot(hv, wu_ref[...])
        gate_ref[...] = gate
        up_ref[...] = up
        act_ref[...] = ((gate * _sigmoid(gate)) * up).astype(BF16)

    wspec = pl.BlockSpec((d, tn), lambda i, j: (0, j))
    tile = pl.BlockSpec((tm, tn), lambda i, j: (i, j))
    return pl.pallas_call(
        body, name=name, grid=(s // tm, f // tn),
        in_specs=[pl.BlockSpec((tm, d), lambda i, j: (i, 0)), wspec, wspec],
        out_specs=[tile, tile, tile],
        out_shape=[jax.ShapeDtypeStruct((s, f), F32), jax.ShapeDtypeStruct((s, f), F32),
                   jax.ShapeDtypeStruct((s, f), BF16)],
        compiler_params=_params("parallel", "parallel"),
    )(h, wg, wu)


def ffn_bwd_act(dxo, wdown, gate, up, name):
    s, d = dxo.shape
    f = wdown.shape[0]
    tm, tn = _rows_tile(s, 512), _rows_tile(f, 256)

    def body(dx_ref, w_ref, gate_ref, up_ref, dgate_ref, dup_ref):
        dact = _dot(dx_ref[...], w_ref[...], _NT)
        gate = gate_ref[...]
        sig = _sigmoid(gate)
        dup_ref[...] = (dact * (gate * sig)).astype(BF16)
        dgate_ref[...] = ((dact * up_ref[...]) * (sig * (1.0 + gate * (1.0 - sig)))).astype(BF16)

    tile = pl.BlockSpec((tm, tn), lambda i, j: (i, j))
    return pl.pallas_call(
        body, name=name, grid=(s // tm, f // tn),
        in_specs=[pl.BlockSpec((tm, d), lambda i, j: (i, 0)), pl.BlockSpec((tn, d), lambda i, j: (j, 0)), tile, tile],
        out_specs=[tile, tile],
        out_shape=[jax.ShapeDtypeStruct((s, f), BF16), jax.ShapeDtypeStruct((s, f), BF16)],
        compiler_params=_params("parallel", "parallel"),
    )(dxo, wdown, gate, up)


def _pool_select(col, by_window):
    out = by_window[3]
    for g in (2, 1, 0):
        out = jnp.where(col < (g + 1) * POOL_GROUP, by_window[g], out)
    return out


def _pool_counts(pos):
    return [jnp.minimum(pos + 1, w).astype(F32) for w in POOL_WINDOWS]


def pool_fwd(proj, wbd, scale, name):
    s = proj.shape[0]
    tm = _rows_tile(s, 256)
    nt = s // tm
    width = SB_WIDTH
    per = tm // POOL_HALO

    def body(u_ref, halo_ref, w_ref, sc_ref, pooled_ref, grouped_ref):
        i = pl.program_id(0)
        u = u_ref[...]
        halo = jnp.where(i > 0, halo_ref[...], 0.0)
        ext = jnp.concatenate([halo, u], axis=0)
        sums = []
        acc = ext
        for k in (1, 2, 4, 8):
            acc = acc + pltpu.roll(acc, k, 0)
            sums.append(acc[POOL_HALO:])
        pos = i * tm + lax.broadcasted_iota(jnp.int32, (tm, width), 0)
        col = lax.broadcasted_iota(jnp.int32, (tm, width), 1)
        means = [sm / cnt for sm, cnt in zip(sums, _pool_counts(pos))]
        pooled = (_pool_select(col, means) - u).astype(BF16)
        pooled_ref[...] = pooled
        grouped_ref[...] = (_dot(pooled, w_ref[...]) * sc_ref[...]).astype(BF16)

    row = pl.BlockSpec((tm, width), lambda i: (i, 0))
    return pl.pallas_call(
        body, name=name, grid=(nt,),
        in_specs=[row, pl.BlockSpec((POOL_HALO, width), lambda i: (jnp.maximum(i * per - 1, 0), 0)),
                  pl.BlockSpec((width, width), lambda i: (0, 0)), pl.BlockSpec((1, width), lambda i: (0, 0))],
        out_specs=[row, row],
        out_shape=[jax.ShapeDtypeStruct((s, width), BF16), jax.ShapeDtypeStruct((s, width), BF16)],
        compiler_params=_params("parallel"),
    )(proj, proj, wbd, scale.reshape(1, width))


def pool_bwd(dcat, pooled, wbd, scale, name):
    s = dcat.shape[0]
    tm = _rows_tile(s, 256)
    nt = s // tm
    width = SB_WIDTH
    per = tm // POOL_HALO
    n_ext = tm + POOL_HALO

    def body(dg_ref, halo_ref, pooled_ref, w_ref, sc_ref, du_ref, dw_ref, dsc_ref):
        i = pl.program_id(0)
        w = w_ref[...]
        sc = sc_ref[...]
        dg = dg_ref[...]
        pooled = pooled_ref[...]
        pg = _dot(pooled, w)
        dpg = (dg * sc).astype(BF16)
        part_sc = jnp.sum(dg * pg, axis=0, keepdims=True)
        part_w = _dot(pooled, dpg, _TN)

        @pl.when(i == 0)
        def _():
            dsc_ref[...] = part_sc
            dw_ref[...] = part_w

        @pl.when(i > 0)
        def _():
            dsc_ref[...] += part_sc
            dw_ref[...] += part_w

        dpooled = _dot(dpg, w, _NT)
        halo_dpg = (jnp.where(i < nt - 1, halo_ref[...], 0.0) * sc).astype(BF16)
        ext = jnp.concatenate([dpooled, _dot(halo_dpg, w, _NT)], axis=0)
        pos = i * tm + lax.broadcasted_iota(jnp.int32, (n_ext, width), 0)
        col = lax.broadcasted_iota(jnp.int32, (tm, width), 1)
        outs = []
        for k_idx, cnt in enumerate(_pool_counts(pos)):
            acc = ext / cnt
            for k in (1, 2, 4, 8)[:k_idx + 1]:
                acc = acc + pltpu.roll(acc, n_ext - k, 0)
            outs.append(acc[:tm])
        du_ref[...] = (_pool_select(col, outs) - dpooled).astype(BF16)

    row = pl.BlockSpec((tm, width), lambda i: (i, 0))
    return pl.pallas_call(
        body, name=name, grid=(nt,),
        in_specs=[row, pl.BlockSpec((POOL_HALO, width), lambda i: (jnp.minimum((i + 1) * per, s // POOL_HALO - 1), 0)),
                  row, pl.BlockSpec((width, width), lambda i: (0, 0)), pl.BlockSpec((1, width), lambda i: (0, 0))],
        out_specs=[row, pl.BlockSpec((width, width), lambda i: (0, 0)), pl.BlockSpec((1, width), lambda i: (0, 0))],
        out_shape=[jax.ShapeDtypeStruct((s, width), BF16), jax.ShapeDtypeStruct((width, width), F32),
                   jax.ShapeDtypeStruct((1, width), F32)],
        compiler_params=_params("arbitrary"),
    )(dcat, dcat, pooled, wbd, scale.reshape(1, width))


def _head_masks(shape):
    lane = lax.broadcasted_iota(jnp.int32, shape, 1)
    return lane < HEAD_DIM, lane >= HEAD_DIM


def _mem_probs(qh, kp):
    logits = _dot(qh, kp, _NT)
    logits = logits - jnp.max(logits, axis=-1, keepdims=True)
    p = jnp.exp(logits)
    return p / jnp.sum(p, axis=-1, keepdims=True)


def mem_attn_fwd(proj, mkv, name):
    s = proj.shape[0]
    m = mkv.shape[0]
    tm = _rows_tile(s, 256)
    q_blk = SB_WIDTH // MEM_WIDTH

    def body(q_ref, kv_ref, o_ref):
        masks = _head_masks((tm, LANES))
        for pr in range(MEM_WIDTH // LANES):
            q = q_ref[:, pr * LANES:(pr + 1) * LANES] * QK_SCALE
            kp = kv_ref[:, pr * LANES:(pr + 1) * LANES]
            vp = kv_ref[:, MEM_WIDTH + pr * LANES:MEM_WIDTH + (pr + 1) * LANES]
            outs = []
            for hm in masks:
                p = _mem_probs(jnp.where(hm, q, 0.0).astype(BF16), kp)
                outs.append(_dot(p.astype(BF16), vp))
            o_ref[:, pr * LANES:(pr + 1) * LANES] = jnp.where(masks[0], outs[0], outs[1]).astype(BF16)

    return pl.pallas_call(
        body, name=name, grid=(s // tm,),
        in_specs=[pl.BlockSpec((tm, MEM_WIDTH), lambda i: (i, q_blk)), pl.BlockSpec((m, 2 * MEM_WIDTH), lambda i: (0, 0))],
        out_specs=pl.BlockSpec((tm, MEM_WIDTH), lambda i: (i, 0)),
        out_shape=jax.ShapeDtypeStruct((s, MEM_WIDTH), BF16),
        compiler_params=_params("parallel"),
    )(proj, mkv)


def mem_attn_bwd(proj, mkv, dcat, name):
    s = proj.shape[0]
    m = mkv.shape[0]
    tm = _rows_tile(s, 256)
    q_blk = SB_WIDTH // MEM_WIDTH

    def body(q_ref, kv_ref, do_ref, dq_ref, dkv_ref):
        i = pl.program_id(0)

        @pl.when(i == 0)
        def _():
            dkv_ref[...] = jnp.zeros_like(dkv_ref)

        masks = _head_masks((tm, LANES))
        for pr in range(MEM_WIDTH // LANES):
            q = q_ref[:, pr * LANES:(pr + 1) * LANES] * QK_SCALE
            do = do_ref[:, pr * LANES:(pr + 1) * LANES]
            kp = kv_ref[:, pr * LANES:(pr + 1) * LANES]
            vp = kv_ref[:, MEM_WIDTH + pr * LANES:MEM_WIDTH + (pr + 1) * LANES]
            dqs = []
            dk = jnp.zeros((m, LANES), F32)
            dv = jnp.zeros((m, LANES), F32)
            for hm in masks:
                qh = jnp.where(hm, q, 0.0).astype(BF16)
                doh = jnp.where(hm, do, 0.0).astype(BF16)
                p = _mem_probs(qh, kp)
                dp = _dot(doh, vp, _NT)
                ds = (p * (dp - jnp.sum(dp * p, axis=-1, keepdims=True))).astype(BF16)
                dqs.append(_dot(ds, kp))
                dk = dk + _dot(ds, qh, _TN)
                dv = dv + _dot(p.astype(BF16), doh, _TN)
            dq_ref[:, pr * LANES:(pr + 1) * LANES] = (jnp.where(masks[0], dqs[0], dqs[1]) * QK_SCALE).astype(BF16)
            dkv_ref[:, pr * LANES:(pr + 1) * LANES] += dk
            dkv_ref[:, MEM_WIDTH + pr * LANES:MEM_WIDTH + (pr + 1) * LANES] += dv

    return pl.pallas_call(
        body, name=name, grid=(s // tm,),
        in_specs=[pl.BlockSpec((tm, MEM_WIDTH), lambda i: (i, q_blk)), pl.BlockSpec((m, 2 * MEM_WIDTH), lambda i: (0, 0)),
                  pl.BlockSpec((tm, MEM_WIDTH), lambda i: (i, q_blk))],
        out_specs=[pl.BlockSpec((tm, MEM_WIDTH), lambda i: (i, 0)), pl.BlockSpec((m, 2 * MEM_WIDTH), lambda i: (0, 0))],
        out_shape=[jax.ShapeDtypeStruct((s, MEM_WIDTH), BF16), jax.ShapeDtypeStruct((m, 2 * MEM_WIDTH), F32)],
        compiler_params=_params("arbitrary"),
    )(proj, mkv, dcat)


SB_ROWS = 256
SB_KEYS = 512
SB_SUB = LANES


def _sb_tri(later):
    r = lax.broadcasted_iota(jnp.int32, (2 * SB_SUB, 2 * SB_SUB), 0) % SB_SUB
    c = lax.broadcasted_iota(jnp.int32, (2 * SB_SUB, 2 * SB_SUB), 1)
    tri = (r > c) if later else (r < c)
    return (tri | (c >= SB_SUB)).astype(BF16)


def _sb_sums(v, tri, n_sub):
    out = []
    for m in range(n_sub):
        vm = v[:, m * SB_SUB:(m + 1) * SB_SUB]
        hi = vm.astype(BF16)
        lo = (vm - hi.astype(F32)).astype(BF16)
        r = _dot(jnp.concatenate([hi, lo], axis=1), tri)
        out.append((r[:, :SB_SUB], r[:, SB_SUB:]))
    return out


def _sb_logs(nqh, k_blk, valid):
    nz = _dot(nqh, k_blk, _NT)
    log_not = jnp.minimum(nz, 0.0) - jnp.log(1.0 + jnp.exp(-jnp.abs(nz)))
    log_beta = log_not - nz
    if valid is not None:
        log_not = jnp.where(valid, log_not, 0.0)
    return log_not, log_beta


def _sb_weights(log_not, log_beta, tri_later, carry_in, valid, n_sub):
    sums = _sb_sums(log_not, tri_later, n_sub)
    run = carry_in
    later = [None] * n_sub
    for m in reversed(range(n_sub)):
        later[m] = sums[m][0] + run
        run = run + sums[m][1]
    w = jnp.exp(log_beta + jnp.concatenate(later, axis=1))
    if valid is not None:
        w = jnp.where(valid, w, 0.0)
    return w, run


def _sb_valid(i, j, tq, tk):
    qpos = i * tq + lax.broadcasted_iota(jnp.int32, (tq, tk), 0)
    kpos = j * tk + lax.broadcasted_iota(jnp.int32, (tq, tk), 1)
    return qpos > kpos


def sb_fwd(projb, kv, name):
    s = projb.shape[0]
    tq, tk = _rows_tile(s, SB_ROWS), _rows_tile(s, SB_KEYS)
    n_sub = tk // SB_SUB
    n_pairs = SB_WIDTH // LANES
    assert s // tk <= LANES and tk % tq == 0

    def body(q_ref, k_ref, v_ref, o_ref, c_ref):
        i = pl.program_id(1)
        j_diag = (i * tq) // tk
        masks = _head_masks((tq, LANES))
        nq = q_ref[...] * (-QK_SCALE)
        tri = _sb_tri(later=True)
        lane = lax.broadcasted_iota(jnp.int32, (tq, LANES), 1)
        accs = []
        for hh in range(2):
            nqh = jnp.where(masks[hh], nq, 0.0).astype(BF16)

            def block(j, carry, valid):
                c, acc, cmat = carry
                off = pl.multiple_of(j * tk, tk)
                log_not, log_beta = _sb_logs(nqh, k_ref[pl.ds(off, tk), :], valid)
                w, c_next = _sb_weights(log_not, log_beta, tri, c, valid, n_sub)
                acc = acc + _dot(w.astype(BF16), v_ref[pl.ds(off, tk), :])
                return c_next, acc, jnp.where(lane == j, c, cmat)

            zero = jnp.zeros((tq, LANES), F32)
            carry = block(j_diag, (zero, zero, zero), _sb_valid(i, j_diag, tq, tk))
            carry = lax.fori_loop(0, j_diag, lambda jj, cr: block(j_diag - 1 - jj, cr, None), carry)
            accs.append(carry[1])
            c_ref[hh] = carry[2]
        o_ref[...] = jnp.where(masks[0], accs[0], accs[1]).astype(BF16)

    return pl.pallas_call(
        body, name=name, grid=(n_pairs, s // tq),
        in_specs=[pl.BlockSpec((tq, LANES), lambda p, i: (i, p)),
                  pl.BlockSpec((s, LANES), lambda p, i: (0, p)),
                  pl.BlockSpec((s, LANES), lambda p, i: (0, n_pairs + p))],
        out_specs=[pl.BlockSpec((tq, LANES), lambda p, i: (i, p)),
                   pl.BlockSpec((2, tq, LANES), lambda p, i: (p, i, 0))],
        out_shape=[jax.ShapeDtypeStruct((s, SB_WIDTH), BF16), jax.ShapeDtypeStruct((N_SB_HEADS, s, LANES), F32)],
        compiler_params=_params("parallel", "parallel"),
    )(projb, kv, kv)


def sb_bwd(projb, kv, dcat, csave, name):
    s = projb.shape[0]
    tq, tk = _rows_tile(s, SB_ROWS), _rows_tile(s, SB_KEYS)
    n_sub = tk // SB_SUB
    n_pairs = SB_WIDTH // LANES

    def body(q_ref, k_ref, v_ref, do_ref, c_ref, dq_ref, dk_ref, dv_ref):
        i = pl.program_id(1)
        j_diag = (i * tq) // tk

        @pl.when(i == 0)
        def _():
            dk_ref[...] = jnp.zeros_like(dk_ref)
            dv_ref[...] = jnp.zeros_like(dv_ref)

        masks = _head_masks((tq, LANES))
        qs = q_ref[...] * QK_SCALE
        do = do_ref[...]
        tri_later = _sb_tri(later=True)
        tri_before = _sb_tri(later=False)
        lane = lax.broadcasted_iota(jnp.int32, (tq, LANES), 1)
        dqs = []
        for hh in range(2):
            qh = jnp.where(masks[hh], qs, 0.0).astype(BF16)
            nqh = -qh
            doh = jnp.where(masks[hh], do, 0.0).astype(BF16)
            cmat = c_ref[hh]

            def block(j, carry, valid):
                e_before, dq = carry
                off = pl.multiple_of(j * tk, tk)
                k_blk = k_ref[pl.ds(off, tk), :]
                log_not, log_beta = _sb_logs(nqh, k_blk, valid)
                c = jnp.sum(jnp.where(lane == j, cmat, 0.0), axis=1, keepdims=True)
                w, _ = _sb_weights(log_not, log_beta, tri_later, jnp.broadcast_to(c, (tq, LANES)), valid, n_sub)
                dv_ref[pl.ds(off, tk), :] += _dot(w.astype(BF16), doh, _TN)
                e = w * _dot(doh, v_ref[pl.ds(off, tk), :], _NT)
                sums = _sb_sums(e, tri_before, n_sub)
                run = e_before
                e_pre = []
                for m in range(n_sub):
                    e_pre.append(sums[m][0] + run)
                    run = run + sums[m][1]
                dz = e - jnp.exp(log_beta) * (e + jnp.concatenate(e_pre, axis=1))
                if valid is not None:
                    dz = jnp.where(valid, dz, 0.0)
                dz = dz.astype(BF16)
                dk_ref[pl.ds(off, tk), :] += _dot(dz, qh, _TN)
                return run, dq + _dot(dz, k_blk)

            zero = jnp.zeros((tq, LANES), F32)
            carry = lax.fori_loop(0, j_diag, lambda j, cr: block(j, cr, None), (zero, zero))
            carry = block(j_diag, carry, _sb_valid(i, j_diag, tq, tk))
            dqs.append(carry[1])
        dq_ref[...] = (jnp.where(masks[0], dqs[0], dqs[1]) * QK_SCALE).astype(BF16)

    tile = pl.BlockSpec((tq, LANES), lambda p, i: (i, p))
    full = pl.BlockSpec((s, LANES), lambda p, i: (0, p))
    return pl.pallas_call(
        body, name=name, grid=(n_pairs, s // tq),
        in_specs=[tile, full, pl.BlockSpec((s, LANES), lambda p, i: (0, n_pairs + p)), tile,
                  pl.BlockSpec((2, tq, LANES), lambda p, i: (p, i, 0))],
        out_specs=[tile, full, full],
        out_shape=[jax.ShapeDtypeStruct((s, SB_WIDTH), BF16), jax.ShapeDtypeStruct((s, SB_WIDTH), F32),
                   jax.ShapeDtypeStruct((s, SB_WIDTH), F32)],
        compiler_params=_params("parallel", "arbitrary"),
    )(projb, kv, kv, dcat, csave)


def reduce_adamw(parts, w, m, v, name):
    n, r, c = parts.shape
    tr = 128 if r % 128 == 0 else r
    bias1 = 1.0 - ADAM_B1 ** ADAM_STEP
    bias2 = 1.0 - ADAM_B2 ** ADAM_STEP

    def body(p_ref, w_ref, m_ref, v_ref, g_ref, d_ref, nm_ref, nv_ref):
        g = p_ref[0].astype(F32)
        for k in range(1, n):
            g = g + p_ref[k].astype(F32)
        new_m = ADAM_B1 * m_ref[...] + (1.0 - ADAM_B1) * g
        new_v = ADAM_B2 * v_ref[...] + (1.0 - ADAM_B2) * (g * g)
        m_hat = new_m / bias1
        v_hat = new_v / bias2
        g_ref[...] = g
        d_ref[...] = -ADAM_LR * (m_hat / (jnp.sqrt(v_hat) + ADAM_EPS) + ADAM_WD * w_ref[...])
        nm_ref[...] = new_m
        nv_ref[...] = new_v

    row = pl.BlockSpec((tr, c), lambda i: (i, 0))
    return pl.pallas_call(
        body, name=name, grid=(r // tr,),
        in_specs=[pl.BlockSpec((n, tr, c), lambda i: (0, i, 0)), row, row, row],
        out_specs=[row, row, row, row],
        out_shape=[jax.ShapeDtypeStruct((r, c), F32)] * 4,
        compiler_params=_params("parallel"),
    )(parts, w, m, v)


BIG = ("a_w_in", "a_w_mem_kv", "a_w_out", "a_w_gu", "a_w_down", "w_kv", "b_w_q", "b_w_mem_kv", "b_w_out", "b_w_gu",
       "b_w_down")
COL_SHARDED = ("a_w_gu", "w_kv", "b_w_gu")
SMALL_SHARDED = ("a_norm_mix", "a_scale", "a_norm_ffn")
SMALL_REPL = ("mem_norm", "kv_norm", "b_norm_mix", "b_norm_ffn", "final_norm")
WEIGHTS = ("mem_norm", "a_norm_mix", "a_w_in", "a_w_group", "a_scale", "a_w_mem_kv", "a_w_out", "a_norm_ffn", "a_w_gu",
           "a_w_down", "kv_norm", "w_kv", "b_norm_mix", "b_w_q", "b_w_mem_kv", "b_w_out", "b_norm_ffn", "b_w_gu",
           "b_w_down", "final_norm")


def _pack_rows(arrays):
    return jnp.concatenate([a.reshape(-1, PACK_COLS) for a in arrays], axis=0)


def _full_from_gathered(name, g, shard_shape):
    r, c = shard_shape
    g = g.reshape(N_DEV, r, c)
    if name in COL_SHARDED:
        return jnp.transpose(g, (1, 0, 2)).reshape(r, N_DEV * c)
    return g.reshape(N_DEV * r, c)


def _blocks_of_full(name, full, shard_shape):
    r, c = shard_shape
    if name in COL_SHARDED:
        g = jnp.transpose(full.reshape(r, N_DEV, c), (1, 0, 2))
    else:
        g = full.reshape(N_DEV, r, c)
    return g.reshape(N_DEV, -1, PACK_COLS)


def _pad_row(v):
    v = v.reshape(1, -1)
    return jnp.pad(v, ((0, 0), (0, PACK_COLS - v.shape[1])))


def kernel(x, mem, mem_norm, a_norm_mix, a_w_in, a_w_group, a_scale, a_w_mem_kv, a_w_out, a_norm_ffn, a_w_gu, a_w_down, kv_norm, w_kv, b_norm_mix, b_w_q, b_w_mem_kv, b_w_out, b_norm_ffn, b_w_gu, b_w_down, final_norm, loss_target, m_mem_norm, m_a_norm_mix, m_a_w_in, m_a_w_group, m_a_scale, m_a_w_mem_kv, m_a_w_out, m_a_norm_ffn, m_a_w_gu, m_a_w_down, m_kv_norm, m_w_kv, m_b_norm_mix, m_b_w_q, m_b_w_mem_kv, m_b_w_out, m_b_norm_ffn, m_b_w_gu, m_b_w_down, m_final_norm, v_mem_norm, v_a_norm_mix, v_a_w_in, v_a_w_group, v_a_scale, v_a_w_mem_kv, v_a_w_out, v_a_norm_ffn, v_a_w_gu, v_a_w_down, v_kv_norm, v_w_kv, v_b_norm_mix, v_b_w_q, v_b_w_mem_kv, v_b_w_out, v_b_norm_ffn, v_b_w_gu, v_b_w_down, v_final_norm):
    w = dict(mem_norm=mem_norm, a_norm_mix=a_norm_mix, a_w_in=a_w_in, a_w_group=a_w_group, a_scale=a_scale,
             a_w_mem_kv=a_w_mem_kv, a_w_out=a_w_out, a_norm_ffn=a_norm_ffn, a_w_gu=a_w_gu, a_w_down=a_w_down,
             kv_norm=kv_norm, w_kv=w_kv, b_norm_mix=b_norm_mix, b_w_q=b_w_q, b_w_mem_kv=b_w_mem_kv, b_w_out=b_w_out,
             b_norm_ffn=b_norm_ffn, b_w_gu=b_w_gu, b_w_down=b_w_down, final_norm=final_norm)
    mom = dict(mem_norm=m_mem_norm, a_norm_mix=m_a_norm_mix, a_w_in=m_a_w_in, a_w_group=m_a_w_group, a_scale=m_a_scale,
               a_w_mem_kv=m_a_w_mem_kv, a_w_out=m_a_w_out, a_norm_ffn=m_a_norm_ffn, a_w_gu=m_a_w_gu,
               a_w_down=m_a_w_down, kv_norm=m_kv_norm, w_kv=m_w_kv, b_norm_mix=m_b_norm_mix, b_w_q=m_b_w_q,
               b_w_mem_kv=m_b_w_mem_kv, b_w_out=m_b_w_out, b_norm_ffn=m_b_norm_ffn, b_w_gu=m_b_w_gu,
               b_w_down=m_b_w_down, final_norm=m_final_norm)
    var = dict(mem_norm=v_mem_norm, a_norm_mix=v_a_norm_mix, a_w_in=v_a_w_in, a_w_group=v_a_w_group, a_scale=v_a_scale,
               a_w_mem_kv=v_a_w_mem_kv, a_w_out=v_a_w_out, a_norm_ffn=v_a_norm_ffn, a_w_gu=v_a_w_gu,
               a_w_down=v_a_w_down, kv_norm=v_kv_norm, w_kv=v_w_kv, b_norm_mix=v_b_norm_mix, b_w_q=v_b_w_q,
               b_w_mem_kv=v_b_w_mem_kv, b_w_out=v_b_w_out, b_norm_ffn=v_b_norm_ffn, b_w_gu=v_b_w_gu,
               b_w_down=v_b_w_down, final_norm=v_final_norm)

    me = 4 * lax.axis_index("x") + 2 * lax.axis_index("y") + lax.axis_index("c")
    shard2d = {n: w[n].shape[-2:] for n in BIG}
    rows = {n: shard2d[n][0] * shard2d[n][1] // PACK_COLS for n in BIG}
    offs, o = {}, 0
    for n in BIG:
        offs[n] = o
        o += rows[n]

    big_send = _pack_rows([w[n].astype(BF16) for n in BIG])
    small_send = jnp.concatenate([_pad_row(w[n]) for n in SMALL_SHARDED]
                                 + [jnp.zeros((8 - len(SMALL_SHARDED), PACK_COLS), F32)], axis=0)
    big_all, small_all = exchange([], [big_send, small_send], "gather_weights")
    full = {n: _full_from_gathered(n, big_all[:, offs[n]:offs[n] + rows[n]], shard2d[n]) for n in BIG}
    a_norm_mix_f = small_all[:, 0, :a_norm_mix.shape[1]].reshape(-1)
    a_scale_f = small_all[:, 1, :a_scale.shape[1]].reshape(-1)
    a_norm_ffn_f = small_all[:, 2, :a_norm_ffn.shape[1]].reshape(-1)

    d_ff = full["a_w_down"].shape[0]
    wbd = jnp.zeros((SB_WIDTH, SB_WIDTH), BF16)
    for g in range(4):
        sl = slice(g * POOL_GROUP, (g + 1) * POOL_GROUP)
        wbd = wbd.at[sl, sl].set(a_w_group[0, g].astype(BF16))

    xs, mems, tgt = x[0], mem[0], loss_target[0]

    memn = rmsnorm_fwd(mems, mem_norm, "memn")
    h0 = rmsnorm_fwd(xs, a_norm_mix_f, "h0")
    proj = matmul([(h0, full["a_w_in"])], "nn", F32, "proj_a")
    mkv_a = matmul([(memn, full["a_w_mem_kv"])], "nn", BF16, "mkv_a")
    pooled, grouped = pool_fwd(proj, wbd, a_scale_f, "pool_fwd")
    memout_a = mem_attn_fwd(proj, mkv_a, "mem_fwd_a")
    cat_a = jnp.concatenate([grouped, memout_a], axis=1)
    x1 = matmul([(cat_a, full["a_w_out"])], "nn", F32, "out_a", res=xs)
    h1 = rmsnorm_fwd(x1, a_norm_ffn_f, "h1")
    wg_a, wu_a = full["a_w_gu"][:, :d_ff], full["a_w_gu"][:, d_ff:]
    gate_a, up_a, act_a = ffn_up(h1, wg_a, wu_a, "ffn_up_a")
    x2 = matmul([(act_a, full["a_w_down"])], "nn", F32, "down_a", res=x1)

    hk = rmsnorm_fwd(x2, kv_norm, "hk")
    kv = matmul([(hk, full["w_kv"])], "nn", BF16, "kv")
    h3 = rmsnorm_fwd(x2, b_norm_mix[0], "h3")
    projb = matmul([(h3, full["b_w_q"])], "nn", F32, "proj_b")
    mkv_b = matmul([(memn, full["b_w_mem_kv"])], "nn", BF16, "mkv_b")
    sb, csave = sb_fwd(projb, kv, "sb_fwd")
    memout_b = mem_attn_fwd(projb, mkv_b, "mem_fwd_b")
    cat_b = jnp.concatenate([sb, memout_b], axis=1)
    x3 = matmul([(cat_b, full["b_w_out"])], "nn", F32, "out_b", res=x2)
    h4 = rmsnorm_fwd(x3, b_norm_ffn[0], "h4")
    wg_b, wu_b = full["b_w_gu"][:, :d_ff], full["b_w_gu"][:, d_ff:]
    gate_b, up_b, act_b = ffn_up(h4, wg_b, wu_b, "ffn_up_b")
    x4 = matmul([(act_b, full["b_w_down"])], "nn", F32, "down_b", res=x3)

    grads = {}
    dx4, dx4_bf, grads["final_norm"], loss_part = loss_head(x4, tgt, final_norm, "loss_head")

    def ffn_backward(tag, x_in, dx_out, dx_out_bf, h, gate, up, act, wg, wu, wdown, g_ffn, extra_pairs=()):
        dgate, dup = ffn_bwd_act(dx_out_bf, wdown, gate, up, "ffn_bwd_act_" + tag)
        d_wdown = matmul([(act, dx_out_bf)], "tn", BF16, "dw_down_" + tag, tm=256)
        dh = matmul([(dgate, wg), (dup, wu)], "nt", F32, "dh_ffn_" + tag)
        d_wg = matmul([(h, dgate)], "tn", BF16, "dw_gate_" + tag, tn=256)
        d_wu = matmul([(h, dup)], "tn", BF16, "dw_up_" + tag, tn=256)
        dx_in, dx_in_bf, dgs = rmsnorm_bwd(x_in, dx_out, [(g_ffn, dh)] + list(extra_pairs), "norm_bwd_ffn_" + tag)
        return dx_in, dx_in_bf, dgs, d_wdown, jnp.concatenate([d_wg, d_wu], axis=1)

    dx3, dx3_bf, dgs, grads["b_w_down"], grads["b_w_gu"] = ffn_backward(
        "b", x3, dx4, dx4_bf, h4, gate_b, up_b, act_b, wg_b, wu_b, full["b_w_down"], b_norm_ffn[0])
    grads["b_norm_ffn"] = dgs[0]

    dcat_b = matmul([(dx3_bf, full["b_w_out"])], "nt", F32, "dcat_b")
    grads["b_w_out"] = matmul([(cat_b, dx3_bf)], "tn", BF16, "dw_out_b")
    dq, dk, dv = sb_bwd(projb, kv, dcat_b, csave, "sb_bwd")
    dqm_b, dmkv_b = mem_attn_bwd(projb, mkv_b, dcat_b, "mem_bwd_b")
    dprojb = jnp.concatenate([dq, dqm_b], axis=1)
    dh3 = matmul([(dprojb, full["b_w_q"])], "nt", F32, "dh3")
    grads["b_w_q"] = matmul([(h3, dprojb)], "tn", BF16, "dw_q_b")
    dkv = jnp.concatenate([dk, dv], axis=1).astype(BF16)
    dhk = matmul([(dkv, full["w_kv"])], "nt", F32, "dhk")
    grads["w_kv"] = matmul([(hk, dkv)], "tn", BF16, "dw_kv")
    dx2, dx2_bf, dgs = rmsnorm_bwd(x2, dx3, [(b_norm_mix[0], dh3), (kv_norm, dhk)], "norm_bwd_x2")
    grads["b_norm_mix"], grads["kv_norm"] = dgs
    dmkv_b_bf = dmkv_b.astype(BF16)
    grads["b_w_mem_kv"] = matmul([(memn, dmkv_b_bf)], "tn", BF16, "dw_mkv_b")

    dx1, dx1_bf, dgs, grads["a_w_down"], grads["a_w_gu"] = ffn_backward(
        "a", x1, dx2, dx2_bf, h1, gate_a, up_a, act_a, wg_a, wu_a, full["a_w_down"], a_norm_ffn_f)
    grads["a_norm_ffn"] = dgs[0]

    dcat_a = matmul([(dx1_bf, full["a_w_out"])], "nt", F32, "dcat_a")
    grads["a_w_out"] = matmul([(cat_a, dx1_bf)], "tn", BF16, "dw_out_a")
    du, d_wbd, grads["a_scale"] = pool_bwd(dcat_a, pooled, wbd, a_scale_f, "pool_bwd")
    grads["a_w_group"] = jnp.stack(
        [d_wbd[g * POOL_GROUP:(g + 1) * POOL_GROUP, g * POOL_GROUP:(g + 1) * POOL_GROUP] for g in range(4)])
    dqm_a, dmkv_a = mem_attn_bwd(proj, mkv_a, dcat_a, "mem_bwd_a")
    dproj = jnp.concatenate([du, dqm_a], axis=1)
    dh0 = matmul([(dproj, full["a_w_in"])], "nt", F32, "dh0")
    grads["a_w_in"] = matmul([(h0, dproj)], "tn", BF16, "dw_in_a")
    dx, _, dgs = rmsnorm_bwd(xs, dx1, [(a_norm_mix_f, dh0)], "norm_bwd_x")
    grads["a_norm_mix"] = dgs[0]
    dmkv_a_bf = dmkv_a.astype(BF16)
    grads["a_w_mem_kv"] = matmul([(memn, dmkv_a_bf)], "tn", BF16, "dw_mkv_a")
    dmemn = matmul([(dmkv_a_bf, full["a_w_mem_kv"]), (dmkv_b_bf, full["b_w_mem_kv"])], "nt", F32, "dmemn")
    _, _, dgs = rmsnorm_bwd(mems, jnp.zeros_like(mems), [(mem_norm, dmemn)], "norm_bwd_mem")
    grads["mem_norm"] = dgs[0]

    big_grads = jnp.concatenate([_blocks_of_full(n, grads[n], shard2d[n]) for n in BIG], axis=1)
    small_names = SMALL_REPL + SMALL_SHARDED
    small_rows = [_pad_row(grads[n]) for n in small_names] + [_pad_row(loss_part[:, :1])]
    small_rows.append(grads["a_w_group"].reshape(-1, PACK_COLS))
    n_small = sum(r.shape[0] for r in small_rows)
    pad_small = -n_small % 8
    small_grads = jnp.concatenate(small_rows + [jnp.zeros((pad_small, PACK_COLS), F32)], axis=0)
    big_recv, small_recv = exchange([big_grads], [small_grads], "exchange_grads")

    def place(v, n):
        v = v.reshape(1, -1)
        return lax.dynamic_update_slice(jnp.zeros((1, PACK_COLS), F32), v, (0, me * v.shape[1]))

    def small_pack(src):
        rws = [_pad_row(src[n]) for n in SMALL_REPL] + [place(src[n], n) for n in SMALL_SHARDED]
        rws += [jnp.zeros((1, PACK_COLS), F32), src["a_w_group"].reshape(-1, PACK_COLS),
                jnp.zeros((pad_small, PACK_COLS), F32)]
        return jnp.concatenate(rws, axis=0)

    big_out = reduce_adamw(big_recv, _pack_rows([w[n] for n in BIG]), _pack_rows([mom[n] for n in BIG]),
                           _pack_rows([var[n] for n in BIG]), "adamw_big")
    small_out = reduce_adamw(small_recv, small_pack(w), small_pack(mom), small_pack(var), "adamw_small")

    def unpack(kind):
        out = {}
        for n in BIG:
            out[n] = big_out[kind][offs[n]:offs[n] + rows[n]].reshape(w[n].shape)
        so = small_out[kind]
        for r, n in enumerate(SMALL_REPL):
            out[n] = so[r, :w[n].shape[-1]].reshape(w[n].shape)
        for r, n in enumerate(SMALL_SHARDED):
            width = w[n].shape[-1]
            out[n] = lax.dynamic_slice(so, (len(SMALL_REPL) + r, me * width), (1, width)).reshape(w[n].shape)
        r0 = len(small_names) + 1
        out["a_w_group"] = so[r0:r0 + a_w_group.size // PACK_COLS].reshape(a_w_group.shape)
        return out

    loss = small_out[0][len(small_names), 0]
    results = [loss, dx.reshape(x.shape)]
    for kind in range(4):
        out = unpack(kind)
        results += [out[n] for n in WEIGHTS]
    return tuple(results)
```

```python
import functools

import jax
import jax.numpy as jnp
from jax import lax
from jax.experimental import pallas as pl
from jax.experimental.pallas import tpu as pltpu

F32 = jnp.float32
BF16 = jnp.bfloat16

N_DEV = 8
HEAD_DIM = 64
N_SB_HEADS = 12
SB_WIDTH = N_SB_HEADS * HEAD_DIM
MEM_WIDTH = 4 * HEAD_DIM
POOL_WINDOWS = (2, 4, 8, 16)
POOL_GROUP = SB_WIDTH // 4
POOL_HALO = 16
EPS = 1e-6
QK_SCALE = HEAD_DIM ** -0.5
LANES = 128
PACK_COLS = 1024

ADAM_LR = 0.001
ADAM_B1 = 0.9
ADAM_B2 = 0.999
ADAM_EPS = 1e-08
ADAM_WD = 0.01
ADAM_STEP = 10

VMEM_LIMIT = 48 * 1024 * 1024

_NT = (((1,), (1,)), ((), ()))
_TN = (((0,), (0,)), ((), ()))
_NN = (((1,), (0,)), ((), ()))


def _params(*sem):
    return pltpu.CompilerParams(dimension_semantics=sem, vmem_limit_bytes=VMEM_LIMIT)


def _dot(a, b, dims=_NN):
    return lax.dot_general(a, b, dims, preferred_element_type=F32)


def exchange(a2a, ag, name):
    n_a2a, n_arr = len(a2a), len(a2a) + len(ag)

    def body(*refs):
        start, wait = _exchange_ops(refs[:n_arr], refs[n_arr:2 * n_arr], refs[2 * n_arr:], n_a2a)
        start()
        wait()

    return pl.pallas_call(
        body, name=name, out_shape=_exchange_out_shape(a2a, ag),
        in_specs=[_ANY] * n_arr, out_specs=[_ANY] * n_arr,
        scratch_shapes=_exchange_sems(n_arr),
    )(*a2a, *ag)


_ANY = pl.BlockSpec(memory_space=pl.ANY)


def _exchange_out_shape(a2a, ag):
    out_shape = [jax.ShapeDtypeStruct(v.shape, v.dtype) for v in a2a]
    return out_shape + [jax.ShapeDtypeStruct((N_DEV,) + v.shape, v.dtype) for v in ag]


def _exchange_sems(n_arr):
    return [pltpu.SemaphoreType.DMA((n_arr, N_DEV - 1)), pltpu.SemaphoreType.DMA((n_arr, N_DEV - 1)),
            pltpu.SemaphoreType.DMA((n_arr,))]


def _exchange_ops(ins, outs, sems, n_a2a):
    send_sems, recv_sems, local_sems = sems
    n_arr = len(ins)
    x, y, c = lax.axis_index("x"), lax.axis_index("y"), lax.axis_index("c")
    me = 4 * x + 2 * y + c

    def peer_of(k):
        px = 1 - x if k & 4 else x
        py = 1 - y if k & 2 else y
        pc = 1 - c if k & 1 else c
        return (px, py, pc), 4 * px + 2 * py + pc

    def copy(a, k, landing):
        peer, pid = peer_of(k)
        src = ins[a].at[pid] if a < n_a2a else ins[a]
        return pltpu.make_async_remote_copy(
            src_ref=src, dst_ref=outs[a].at[pid if landing else me], send_sem=send_sems.at[a, k - 1],
            recv_sem=recv_sems.at[a, k - 1], device_id=peer, device_id_type=pl.DeviceIdType.MESH)

    def local(a):
        src = ins[a].at[me] if a < n_a2a else ins[a]
        return pltpu.make_async_copy(src, outs[a].at[me], local_sems.at[a])

    order = (6, 7, 4, 5, 2, 3, 1)

    def start():
        for a in range(n_arr):
            local(a).start()
        for k in order:
            for a in range(n_arr):
                copy(a, k, landing=False).start()

    def wait():
        for k in order:
            for a in range(n_arr):
                copy(a, k, landing=True).wait_recv()
        for k in order:
            for a in range(n_arr):
                copy(a, k, landing=False).wait_send()
        for a in range(n_arr):
            local(a).wait()

    return start, wait


def _rows_tile(n, want):
    t = min(want, n)
    assert n % t == 0, (n, t)
    return t


def rmsnorm_fwd(x, g, name):
    s, d = x.shape
    tm = _rows_tile(s, 512)

    def body(x_ref, g_ref, h_ref):
        xv = x_ref[...]
        r = lax.rsqrt(jnp.mean(xv * xv, axis=-1, keepdims=True) + EPS)
        h_ref[...] = ((xv * r) * g_ref[...]).astype(h_ref.dtype)

    return pl.pallas_call(
        body, name=name, grid=(s // tm,),
        in_specs=[pl.BlockSpec((tm, d), lambda i: (i, 0)), pl.BlockSpec((1, d), lambda i: (0, 0))],
        out_specs=pl.BlockSpec((tm, d), lambda i: (i, 0)),
        out_shape=jax.ShapeDtypeStruct((s, d), BF16),
        compiler_params=_params("parallel"),
    )(x, g.reshape(1, d))


def rmsnorm_bwd(x, dres, pairs, name):
    s, d = x.shape
    tm = _rows_tile(s, 256)
    n = len(pairs)

    def body(*refs):
        x_ref, dres_ref = refs[0], refs[1]
        g_refs, dh_refs = refs[2:2 + n], refs[2 + n:2 + 2 * n]
        dx_ref, dxb_ref = refs[2 + 2 * n], refs[3 + 2 * n]
        dg_refs = refs[4 + 2 * n:]
        i = pl.program_id(0)
        xv = x_ref[...]
        r = lax.rsqrt(jnp.mean(xv * xv, axis=-1, keepdims=True) + EPS)
        xhat = xv * r
        acc = dres_ref[...]
        for k in range(n):
            dh = dh_refs[k][...].astype(F32)
            part = jnp.sum(dh * xhat, axis=0, keepdims=True)

            @pl.when(i == 0)
            def _(k=k, part=part):
                dg_refs[k][...] = part

            @pl.when(i > 0)
            def _(k=k, part=part):
                dg_refs[k][...] += part

            dxh = dh * g_refs[k][...]
            acc = acc + r * (dxh - xhat * jnp.mean(dxh * xhat, axis=-1, keepdims=True))
        dx_ref[...] = acc
        dxb_ref[...] = acc.astype(BF16)

    row = pl.BlockSpec((tm, d), lambda i: (i, 0))
    vec = pl.BlockSpec((1, d), lambda i: (0, 0))
    outs = pl.pallas_call(
        body, name=name, grid=(s // tm,),
        in_specs=[row, row] + [vec] * n + [row] * n,
        out_specs=[row, row] + [vec] * n,
        out_shape=[jax.ShapeDtypeStruct((s, d), F32), jax.ShapeDtypeStruct((s, d), BF16)]
        + [jax.ShapeDtypeStruct((1, d), F32)] * n,
        compiler_params=_params("arbitrary"),
    )(x, dres, *[g.reshape(1, d) for g, _ in pairs], *[dh for _, dh in pairs])
    return outs[0], outs[1], outs[2:]


def loss_head(x, tgt, g, name):
    s, d = x.shape
    tm = _rows_tile(s, 256)

    def body(x_ref, t_ref, g_ref, dx_ref, dxb_ref, dg_ref, loss_ref):
        i = pl.program_id(0)
        xv = x_ref[...]
        gv = g_ref[...]
        r = lax.rsqrt(jnp.mean(xv * xv, axis=-1, keepdims=True) + EPS)
        xhat = xv * r
        diff = xhat * gv - t_ref[...]
        part_loss = 0.5 * jnp.sum(jnp.mean(diff * diff, axis=-1, keepdims=True), axis=0, keepdims=True)
        dy = diff * (1.0 / d)
        part_dg = jnp.sum(dy * xhat, axis=0, keepdims=True)

        @pl.when(i == 0)
        def _():
            dg_ref[...] = part_dg
            loss_ref[...] = jnp.broadcast_to(part_loss, loss_ref.shape)

        @pl.when(i > 0)
        def _():
            dg_ref[...] += part_dg
            loss_ref[...] += jnp.broadcast_to(part_loss, loss_ref.shape)

        dxh = dy * gv
        dx = r * (dxh - xhat * jnp.mean(dxh * xhat, axis=-1, keepdims=True))
        dx_ref[...] = dx
        dxb_ref[...] = dx.astype(BF16)

    row = pl.BlockSpec((tm, d), lambda i: (i, 0))
    vec = pl.BlockSpec((1, d), lambda i: (0, 0))
    return pl.pallas_call(
        body, name=name, grid=(s // tm,),
        in_specs=[row, row, vec], out_specs=[row, row, vec, pl.BlockSpec((1, LANES), lambda i: (0, 0))],
        out_shape=[jax.ShapeDtypeStruct((s, d), F32), jax.ShapeDtypeStruct((s, d), BF16),
                   jax.ShapeDtypeStruct((1, d), F32), jax.ShapeDtypeStruct((1, LANES), F32)],
        compiler_params=_params("arbitrary"),
    )(x, tgt, g.reshape(1, d))


def matmul(pairs, mode, out_dtype, name, res=None, tm=512, tn=512, roles=(None, None), count=1):
    a0, b0 = pairs[0]
    m = a0.shape[-1] if mode == "tn" else a0.shape[-2]
    n = b0.shape[-2] if mode == "nt" else b0.shape[-1]
    tm, tn = _rows_tile(m, tm), _rows_tile(n, tn)
    dims = {"nn": _NN, "nt": _NT, "tn": _TN}[mode]
    npairs = len(pairs)
    batched, reducing = "batch" in roles, "reduce" in roles
    assert not (batched and reducing) and (res is None or not batched)
    n_in = 2 * npairs + (res is not None)

    def product(refs):
        acc = None
        for p in range(npairs):
            d = _dot(refs[2 * p][...].astype(BF16), refs[2 * p + 1][...].astype(BF16), dims)
            acc = d if acc is None else acc + d
        return acc

    def body(*refs):
        o_ref = refs[n_in]
        acc = product(refs)
        if not reducing:
            if res is not None:
                acc = acc + refs[2 * npairs][...]
            o_ref[...] = acc.astype(o_ref.dtype)
            return
        acc_ref = refs[n_in + 1]
        kk = pl.program_id(3)

        @pl.when(kk == 0)
        def _():
            acc_ref[...] = acc if res is None else acc + refs[2 * npairs][...]

        @pl.when(kk > 0)
        def _():
            acc_ref[...] += acc

        @pl.when(kk == count - 1)
        def _():
            o_ref[...] = acc_ref[...].astype(o_ref.dtype)

    def spec(arr, role, block, index):
        if arr.ndim == 2:
            return pl.BlockSpec(block, lambda bb, i, j, kk: index(i, j))
        assert role in ("batch", "reduce") and arr.shape[0] == count, (name, role, arr.shape)
        if role == "batch":
            return pl.BlockSpec((None,) + block, lambda bb, i, j, kk: (bb,) + index(i, j))
        return pl.BlockSpec((None,) + block, lambda bb, i, j, kk: (kk,) + index(i, j))

    in_specs, args = [], []
    for a, b in pairs:
        if mode == "tn":
            in_specs.append(spec(a, roles[0], (a.shape[-2], tm), lambda i, j: (0, i)))
        else:
            in_specs.append(spec(a, roles[0], (tm, a.shape[-1]), lambda i, j: (i, 0)))
        if mode == "nt":
            in_specs.append(spec(b, roles[1], (tn, b.shape[-1]), lambda i, j: (j, 0)))
        else:
            in_specs.append(spec(b, roles[1], (b.shape[-2], tn), lambda i, j: (0, j)))
        args += [a, b]
    if res is not None:
        in_specs.append(pl.BlockSpec((tm, tn), lambda bb, i, j, kk: (i, j)))
        args.append(res)
    if batched:
        out_spec = pl.BlockSpec((None, tm, tn), lambda bb, i, j, kk: (bb, i, j))
        out_shape = jax.ShapeDtypeStruct((count, m, n), out_dtype)
    else:
        out_spec = pl.BlockSpec((tm, tn), lambda bb, i, j, kk: (i, j))
        out_shape = jax.ShapeDtypeStruct((m, n), out_dtype)
    return pl.pallas_call(
        body, name=name, grid=(count if batched else 1, m // tm, n // tn, count if reducing else 1),
        in_specs=in_specs, out_specs=out_spec, out_shape=out_shape,
        scratch_shapes=[pltpu.VMEM((tm, tn), F32)] if reducing else [],
        compiler_params=_params("parallel", "parallel", "parallel", "arbitrary"),
    )(*args)


def _sigmoid(v):
    return 1.0 / (1.0 + jnp.exp(-v))


def ffn_up(h, wgu, name):
    s, d = h.shape
    _, n_sh, _, fs = wgu.shape
    tm = _rows_tile(s, 512)

    def body(h_ref, w_ref, gu_ref, act_ref):
        hv = h_ref[...]
        gate = _dot(hv, w_ref[0])
        up = _dot(hv, w_ref[1])
        gu_ref[0] = gate.astype(BF16)
        gu_ref[1] = up.astype(BF16)
        act_ref[...] = ((gate * _sigmoid(gate)) * up).astype(BF16)

    return pl.pallas_call(
        body, name=name, grid=(n_sh, s // tm),
        in_specs=[pl.BlockSpec((tm, d), lambda j, i: (i, 0)), pl.BlockSpec((2, None, d, fs), lambda j, i: (0, j, 0, 0))],
        out_specs=[pl.BlockSpec((2, None, tm, fs), lambda j, i: (0, j, i, 0)),
                   pl.BlockSpec((None, tm, fs), lambda j, i: (j, i, 0))],
        out_shape=[jax.ShapeDtypeStruct((2, n_sh, s, fs), BF16), jax.ShapeDtypeStruct((n_sh, s, fs), BF16)],
        compiler_params=_params("parallel", "parallel"),
    )(h, wgu)


def ffn_bwd_act(dxo, wdown, gu, name):
    s, d = dxo.shape
    n_sh, fs, _ = wdown.shape
    tm = _rows_tile(s, 512)

    def body(dx_ref, w_ref, gu_ref, dgu_ref):
        dact = _dot(dx_ref[...], w_ref[...], _NT)
        gate = gu_ref[0].astype(F32)
        sig = _sigmoid(gate)
        dgu_ref[0] = ((dact * gu_ref[1].astype(F32)) * (sig * (1.0 + gate * (1.0 - sig)))).astype(BF16)
        dgu_ref[1] = (dact * (gate * sig)).astype(BF16)

    tile = pl.BlockSpec((2, None, tm, fs), lambda j, i: (0, j, i, 0))
    return pl.pallas_call(
        body, name=name, grid=(n_sh, s // tm),
        in_specs=[pl.BlockSpec((tm, d), lambda j, i: (i, 0)), pl.BlockSpec((None, fs, d), lambda j, i: (j, 0, 0)), tile],
        out_specs=tile,
        out_shape=jax.ShapeDtypeStruct((2, n_sh, s, fs), BF16),
        compiler_params=_params("parallel", "parallel"),
    )(dxo, wdown, gu)


def _pool_select(col, by_window):
    out = by_window[3]
    for g in (2, 1, 0):
        out = jnp.where(col < (g + 1) * POOL_GROUP, by_window[g], out)
    return out


def _pool_counts(pos):
    return [jnp.minimum(pos + 1, w).astype(F32) for w in POOL_WINDOWS]


def pool_fwd(proj, wbd, scale, name):
    s = proj.shape[0]
    tm = _rows_tile(s, 256)
    nt = s // tm
    width = SB_WIDTH
    per = tm // POOL_HALO

    def body(u_ref, halo_ref, w_ref, sc_ref, pooled_ref, grouped_ref):
        i = pl.program_id(0)
        u = u_ref[...]
        halo = jnp.where(i > 0, halo_ref[...], 0.0)
        ext = jnp.concatenate([halo, u], axis=0)
        sums = []
        acc = ext
        for k in (1, 2, 4, 8):
            acc = acc + pltpu.roll(acc, k, 0)
            sums.append(acc[POOL_HALO:])
        pos = i * tm + lax.broadcasted_iota(jnp.int32, (tm, width), 0)
        col = lax.broadcasted_iota(jnp.int32, (tm, width), 1)
        means = [sm / cnt for sm, cnt in zip(sums, _pool_counts(pos))]
        pooled = (_pool_select(col, means) - u).astype(BF16)
        pooled_ref[...] = pooled
        grouped_ref[...] = (_dot(pooled, w_ref[...]) * sc_ref[...]).astype(BF16)

    row = pl.BlockSpec((tm, width), lambda i: (i, 0))
    return pl.pallas_call(
        body, name=name, grid=(nt,),
        in_specs=[row, pl.BlockSpec((POOL_HALO, width), lambda i: (jnp.maximum(i * per - 1, 0), 0)),
                  pl.BlockSpec((width, width), lambda i: (0, 0)), pl.BlockSpec((1, width), lambda i: (0, 0))],
        out_specs=[row, row],
        out_shape=[jax.ShapeDtypeStruct((s, width), BF16), jax.ShapeDtypeStruct((s, width + MEM_WIDTH), BF16)],
        compiler_params=_params("parallel"),
    )(proj, proj, wbd, scale.reshape(1, width))


def pool_bwd(dcat, pooled, wbd, scale, name):
    s = dcat.shape[0]
    tm = _rows_tile(s, 256)
    nt = s // tm
    width = SB_WIDTH
    per = tm // POOL_HALO
    n_ext = tm + POOL_HALO

    def body(dg_ref, halo_ref, pooled_ref, w_ref, sc_ref, du_ref, dw_ref, dsc_ref):
        i = pl.program_id(0)
        w = w_ref[...]
        sc = sc_ref[...]
        dg = dg_ref[...]
        pooled = pooled_ref[...]
        pg = _dot(pooled, w)
        dpg = (dg * sc).astype(BF16)
        part_sc = jnp.sum(dg * pg, axis=0, keepdims=True)
        part_w = _dot(pooled, dpg, _TN)

        @pl.when(i == 0)
        def _():
            dsc_ref[...] = part_sc
            dw_ref[...] = part_w

        @pl.when(i > 0)
        def _():
            dsc_ref[...] += part_sc
            dw_ref[...] += part_w

        dpooled = _dot(dpg, w, _NT)
        halo_dpg = (jnp.where(i < nt - 1, halo_ref[...], 0.0) * sc).astype(BF16)
        ext = jnp.concatenate([dpooled, _dot(halo_dpg, w, _NT)], axis=0)
        pos = i * tm + lax.broadcasted_iota(jnp.int32, (n_ext, width), 0)
        col = lax.broadcasted_iota(jnp.int32, (tm, width), 1)
        outs = []
        for k_idx, cnt in enumerate(_pool_counts(pos)):
            acc = ext / cnt
            for k in (1, 2, 4, 8)[:k_idx + 1]:
                acc = acc + pltpu.roll(acc, n_ext - k, 0)
            outs.append(acc[:tm])
        du_ref[...] = (_pool_select(col, outs) - dpooled).astype(BF16)

    row = pl.BlockSpec((tm, width), lambda i: (i, 0))
    return pl.pallas_call(
        body, name=name, grid=(nt,),
        in_specs=[row, pl.BlockSpec((POOL_HALO, width), lambda i: (jnp.minimum((i + 1) * per, s // POOL_HALO - 1), 0)),
                  row, pl.BlockSpec((width, width), lambda i: (0, 0)), pl.BlockSpec((1, width), lambda i: (0, 0))],
        out_specs=[row, pl.BlockSpec((width, width), lambda i: (0, 0)), pl.BlockSpec((1, width), lambda i: (0, 0))],
        out_shape=[jax.ShapeDtypeStruct((s, width + MEM_WIDTH), BF16), jax.ShapeDtypeStruct((width, width), F32),
                   jax.ShapeDtypeStruct((1, width), F32)],
        compiler_params=_params("arbitrary"),
    )(dcat, dcat, pooled, wbd, scale.reshape(1, width))


def _head_masks(shape):
    lane = lax.broadcasted_iota(jnp.int32, shape, 1)
    return lane < HEAD_DIM, lane >= HEAD_DIM


def _mem_probs(qh, kp):
    logits = _dot(qh, kp, _NT)
    logits = logits - jnp.max(logits, axis=-1, keepdims=True)
    p = jnp.exp(logits)
    return p / jnp.sum(p, axis=-1, keepdims=True)


def mem_attn_fwd(proj, mkv, cat, name):
    s = proj.shape[0]
    m = mkv.shape[0]
    tm = _rows_tile(s, 256)
    q_blk = SB_WIDTH // MEM_WIDTH

    def body(q_ref, kv_ref, _, o_ref):
        masks = _head_masks((tm, LANES))
        for pr in range(MEM_WIDTH // LANES):
            q = q_ref[:, pr * LANES:(pr + 1) * LANES] * QK_SCALE
            kp = kv_ref[:, pr * LANES:(pr + 1) * LANES]
            vp = kv_ref[:, MEM_WIDTH + pr * LANES:MEM_WIDTH + (pr + 1) * LANES]
            outs = []
            for hm in masks:
                p = _mem_probs(jnp.where(hm, q, 0.0).astype(BF16), kp)
                outs.append(_dot(p.astype(BF16), vp))
            o_ref[:, pr * LANES:(pr + 1) * LANES] = jnp.where(masks[0], outs[0], outs[1]).astype(BF16)

    return pl.pallas_call(
        body, name=name, grid=(s // tm,),
        in_specs=[pl.BlockSpec((tm, MEM_WIDTH), lambda i: (i, q_blk)), pl.BlockSpec((m, 2 * MEM_WIDTH), lambda i: (0, 0)),
                  _ANY],
        out_specs=pl.BlockSpec((tm, MEM_WIDTH), lambda i: (i, q_blk)),
        out_shape=jax.ShapeDtypeStruct(cat.shape, BF16),
        input_output_aliases={2: 0},
        compiler_params=_params("parallel"),
    )(proj, mkv, cat)


def mem_attn_bwd(proj, mkv, dcat, dproj, name):
    s = proj.shape[0]
    m = mkv.shape[0]
    tm = _rows_tile(s, 256)
    q_blk = SB_WIDTH // MEM_WIDTH

    def body(q_ref, kv_ref, do_ref, _, dq_ref, dkv_ref):
        i = pl.program_id(0)

        @pl.when(i == 0)
        def _():
            dkv_ref[...] = jnp.zeros_like(dkv_ref)

        masks = _head_masks((tm, LANES))
        for pr in range(MEM_WIDTH // LANES):
            q = q_ref[:, pr * LANES:(pr + 1) * LANES] * QK_SCALE
            do = do_ref[:, pr * LANES:(pr + 1) * LANES]
            kp = kv_ref[:, pr * LANES:(pr + 1) * LANES]
            vp = kv_ref[:, MEM_WIDTH + pr * LANES:MEM_WIDTH + (pr + 1) * LANES]
            dqs = []
            dk = jnp.zeros((m, LANES), F32)
            dv = jnp.zeros((m, LANES), F32)
            for hm in masks:
                qh = jnp.where(hm, q, 0.0).astype(BF16)
                doh = jnp.where(hm, do, 0.0).astype(BF16)
                p = _mem_probs(qh, kp)
                dp = _dot(doh, vp, _NT)
                ds = (p * (dp - jnp.sum(dp * p, axis=-1, keepdims=True))).astype(BF16)
                dqs.append(_dot(ds, kp))
                dk = dk + _dot(ds, qh, _TN)
                dv = dv + _dot(p.astype(BF16), doh, _TN)
            dq_ref[:, pr * LANES:(pr + 1) * LANES] = (jnp.where(masks[0], dqs[0], dqs[1]) * QK_SCALE).astype(BF16)
            dkv_ref[:, pr * LANES:(pr + 1) * LANES] += dk
            dkv_ref[:, MEM_WIDTH + pr * LANES:MEM_WIDTH + (pr + 1) * LANES] += dv

    return pl.pallas_call(
        body, name=name, grid=(s // tm,),
        in_specs=[pl.BlockSpec((tm, MEM_WIDTH), lambda i: (i, q_blk)), pl.BlockSpec((m, 2 * MEM_WIDTH), lambda i: (0, 0)),
                  pl.BlockSpec((tm, MEM_WIDTH), lambda i: (i, q_blk)), _ANY],
        out_specs=[pl.BlockSpec((tm, MEM_WIDTH), lambda i: (i, q_blk)), pl.BlockSpec((m, 2 * MEM_WIDTH), lambda i: (0, 0))],
        out_shape=[jax.ShapeDtypeStruct(dproj.shape, BF16), jax.ShapeDtypeStruct((m, 2 * MEM_WIDTH), F32)],
        input_output_aliases={3: 0},
        compiler_params=_params("arbitrary"),
    )(proj, mkv, dcat, dproj)


SB_ROWS = 256
SB_KEYS = 512
SB_SUB = LANES


def _sb_tri(later):
    r = lax.broadcasted_iota(jnp.int32, (2 * SB_SUB, 2 * SB_SUB), 0) % SB_SUB
    c = lax.broadcasted_iota(jnp.int32, (2 * SB_SUB, 2 * SB_SUB), 1)
    tri = (r > c) if later else (r < c)
    return (tri | (c >= SB_SUB)).astype(BF16)


def _sb_sums(v, tri, n_sub):
    out = []
    for m in range(n_sub):
        vm = v[:, m * SB_SUB:(m + 1) * SB_SUB]
        hi = vm.astype(BF16)
        lo = (vm - hi.astype(F32)).astype(BF16)
        r = _dot(jnp.concatenate([hi, lo], axis=1), tri)
        out.append((r[:, :SB_SUB], r[:, SB_SUB:]))
    return out


def _sb_logs(nqh, k_blk, valid):
    nz = _dot(nqh, k_blk, _NT)
    log_not = jnp.minimum(nz, 0.0) - jnp.log(1.0 + jnp.exp(-jnp.abs(nz)))
    log_beta = log_not - nz
    if valid is not None:
        log_not = jnp.where(valid, log_not, 0.0)
    return log_not, log_beta


def _sb_weights(log_not, log_beta, tri_later, carry_in, valid, n_sub):
    sums = _sb_sums(log_not, tri_later, n_sub)
    run = carry_in
    later = [None] * n_sub
    for m in reversed(range(n_sub)):
        later[m] = sums[m][0] + run
        run = run + sums[m][1]
    w = jnp.exp(log_beta + jnp.concatenate(later, axis=1))
    if valid is not None:
        w = jnp.where(valid, w, 0.0)
    return w, run


def _sb_valid(i, j, tq, tk):
    qpos = i * tq + lax.broadcasted_iota(jnp.int32, (tq, tk), 0)
    kpos = j * tk + lax.broadcasted_iota(jnp.int32, (tq, tk), 1)
    return qpos > kpos


def sb_fwd(projb, kv, name, gather=()):
    s = projb.shape[0]
    tq, tk = _rows_tile(s, SB_ROWS), _rows_tile(s, SB_KEYS)
    n_sub = tk // SB_SUB
    n_pairs = SB_WIDTH // LANES
    assert s // tk <= LANES and tk % tq == 0

    n_g = len(gather)
    n_i = s // tq

    def body(*refs):
        q_ref, k_ref, v_ref = refs[:3]
        o_ref, c_ref = refs[3 + n_g:5 + n_g]
        i = pl.program_id(1)
        if n_g:
            start, wait = _exchange_ops(refs[3:3 + n_g], refs[5 + n_g:5 + 2 * n_g], refs[5 + 2 * n_g:], 0)
            pl.when((pl.program_id(0) == 0) & (i == 0))(start)
        j_diag = (i * tq) // tk
        masks = _head_masks((tq, LANES))
        nq = q_ref[...] * (-QK_SCALE)
        tri = _sb_tri(later=True)
        lane = lax.broadcasted_iota(jnp.int32, (tq, LANES), 1)
        accs = []
        for hh in range(2):
            nqh = jnp.where(masks[hh], nq, 0.0).astype(BF16)

            def block(j, carry, valid):
                c, acc, cmat = carry
                off = pl.multiple_of(j * tk, tk)
                log_not, log_beta = _sb_logs(nqh, k_ref[pl.ds(off, tk), :], valid)
                w, c_next = _sb_weights(log_not, log_beta, tri, c, valid, n_sub)
                acc = acc + _dot(w.astype(BF16), v_ref[pl.ds(off, tk), :])
                return c_next, acc, jnp.where(lane == j, c, cmat)

            zero = jnp.zeros((tq, LANES), F32)
            carry = block(j_diag, (zero, zero, zero), _sb_valid(i, j_diag, tq, tk))
            carry = lax.fori_loop(0, j_diag, lambda jj, cr: block(j_diag - 1 - jj, cr, None), carry)
            accs.append(carry[1])
            c_ref[hh] = carry[2]
        o_ref[...] = jnp.where(masks[0], accs[0], accs[1]).astype(BF16)
        if n_g:
            pl.when((pl.program_id(0) == n_pairs - 1) & (i == n_i - 1))(wait)

    outs = pl.pallas_call(
        body, name=name, grid=(n_pairs, n_i),
        in_specs=[pl.BlockSpec((tq, LANES), lambda p, i: (i, p)),
                  pl.BlockSpec((s, LANES), lambda p, i: (0, p)),
                  pl.BlockSpec((s, LANES), lambda p, i: (0, n_pairs + p))] + [_ANY] * n_g,
        out_specs=[pl.BlockSpec((tq, LANES), lambda p, i: (i, p)),
                   pl.BlockSpec((2, tq, LANES), lambda p, i: (p, i, 0))] + [_ANY] * n_g,
        out_shape=[jax.ShapeDtypeStruct((s, SB_WIDTH + MEM_WIDTH), BF16),
                   jax.ShapeDtypeStruct((N_SB_HEADS, s, LANES), F32)] + _exchange_out_shape([], gather),
        scratch_shapes=_exchange_sems(n_g) if n_g else [],
        compiler_params=_params("arbitrary", "arbitrary"),
    )(projb, kv, kv, *gather)
    return outs[0], outs[1], outs[2:]


def sb_bwd(projb, kv, dcat, csave, name, scatter=()):
    s = projb.shape[0]
    tq, tk = _rows_tile(s, SB_ROWS), _rows_tile(s, SB_KEYS)
    n_sub = tk // SB_SUB
    n_pairs = SB_WIDTH // LANES
    n_x = len(scatter)
    n_i = s // tq

    def body(*refs):
        q_ref, k_ref, v_ref, do_ref, c_ref = refs[:5]
        dq_ref, dkb_ref, dvb_ref = refs[5 + n_x:8 + n_x]
        dk_ref, dv_ref = refs[8 + 2 * n_x:10 + 2 * n_x]
        i = pl.program_id(1)
        if n_x:
            start, wait = _exchange_ops(refs[5:5 + n_x], refs[8 + n_x:8 + 2 * n_x], refs[10 + 2 * n_x:], n_x)
            pl.when((pl.program_id(0) == 0) & (i == 0))(start)
        j_diag = (i * tq) // tk

        @pl.when(i == 0)
        def _():
            dk_ref[...] = jnp.zeros_like(dk_ref)
            dv_ref[...] = jnp.zeros_like(dv_ref)

        masks = _head_masks((tq, LANES))
        qs = q_ref[...] * QK_SCALE
        do = do_ref[...]
        tri_later = _sb_tri(later=True)
        tri_before = _sb_tri(later=False)
        lane = lax.broadcasted_iota(jnp.int32, (tq, LANES), 1)
        dqs = []
        for hh in range(2):
            qh = jnp.where(masks[hh], qs, 0.0).astype(BF16)
            nqh = -qh
            doh = jnp.where(masks[hh], do, 0.0).astype(BF16)
            cmat = c_ref[hh]

            def block(j, carry, valid):
                e_before, dq = carry
                off = pl.multiple_of(j * tk, tk)
                k_blk = k_ref[pl.ds(off, tk), :]
                log_not, log_beta = _sb_logs(nqh, k_blk, valid)
                c = jnp.sum(jnp.where(lane == j, cmat, 0.0), axis=1, keepdims=True)
                w, _ = _sb_weights(log_not, log_beta, tri_later, jnp.broadcast_to(c, (tq, LANES)), valid, n_sub)
                dv_ref[pl.ds(off, tk), :] += _dot(w.astype(BF16), doh, _TN)
                e = w * _dot(doh, v_ref[pl.ds(off, tk), :], _NT)
                sums = _sb_sums(e, tri_before, n_sub)
                run = e_before
                e_pre = []
                for m in range(n_sub):
                    e_pre.append(sums[m][0] + run)
                    run = run + sums[m][1]
                dz = e - jnp.exp(log_beta) * (e + jnp.concatenate(e_pre, axis=1))
                if valid is not None:
                    dz = jnp.where(valid, dz, 0.0)
                dz = dz.astype(BF16)
                dk_ref[pl.ds(off, tk), :] += _dot(dz, qh, _TN)
                return run, dq + _dot(dz, k_blk)

            zero = jnp.zeros((tq, LANES), F32)
            carry = lax.fori_loop(0, j_diag, lambda j, cr: block(j, cr, None), (zero, zero))
            carry = block(j_diag, carry, _sb_valid(i, j_diag, tq, tk))
            dqs.append(carry[1])
        dq_ref[...] = (jnp.where(masks[0], dqs[0], dqs[1]) * QK_SCALE).astype(BF16)

        @pl.when(i == n_i - 1)
        def _():
            dkb_ref[...] = dk_ref[...].astype(BF16)
            dvb_ref[...] = dv_ref[...].astype(BF16)

        if n_x:
            pl.when((pl.program_id(0) == n_pairs - 1) & (i == n_i - 1))(wait)

    tile = pl.BlockSpec((tq, LANES), lambda p, i: (i, p))
    full = pl.BlockSpec((s, LANES), lambda p, i: (0, p))
    outs = pl.pallas_call(
        body, name=name, grid=(n_pairs, n_i),
        in_specs=[tile, full, pl.BlockSpec((s, LANES), lambda p, i: (0, n_pairs + p)), tile,
                  pl.BlockSpec((2, tq, LANES), lambda p, i: (p, i, 0))] + [_ANY] * n_x,
        out_specs=[tile, full, full] + [_ANY] * n_x,
        out_shape=[jax.ShapeDtypeStruct((s, SB_WIDTH + MEM_WIDTH), BF16), jax.ShapeDtypeStruct((s, SB_WIDTH), BF16),
                   jax.ShapeDtypeStruct((s, SB_WIDTH), BF16)] + _exchange_out_shape(scatter, []),
        scratch_shapes=[pltpu.VMEM((s, LANES), F32), pltpu.VMEM((s, LANES), F32)]
        + (_exchange_sems(n_x) if n_x else []),
        compiler_params=_params("arbitrary", "arbitrary"),
    )(projb, kv, kv, dcat, csave, *scatter)
    return outs[0], outs[1], outs[2], outs[3:]


def reduce_adamw(parts, w, m, v, name):
    n, r, c = parts.shape
    tr = next((t for t in (128, 176) if r % t == 0), r)
    bias1 = 1.0 - ADAM_B1 ** ADAM_STEP
    bias2 = 1.0 - ADAM_B2 ** ADAM_STEP

    def body(p_ref, w_ref, m_ref, v_ref, g_ref, d_ref, nm_ref, nv_ref):
        g = p_ref[0].astype(F32)
        for k in range(1, n):
            g = g + p_ref[k].astype(F32)
        new_m = ADAM_B1 * m_ref[...] + (1.0 - ADAM_B1) * g
        new_v = ADAM_B2 * v_ref[...] + (1.0 - ADAM_B2) * (g * g)
        m_hat = new_m / bias1
        v_hat = new_v / bias2
        g_ref[...] = g
        d_ref[...] = -ADAM_LR * (m_hat / (jnp.sqrt(v_hat) + ADAM_EPS) + ADAM_WD * w_ref[...])
        nm_ref[...] = new_m
        nv_ref[...] = new_v

    row = pl.BlockSpec((tr, c), lambda i: (i, 0))
    return pl.pallas_call(
        body, name=name, grid=(r // tr,),
        in_specs=[pl.BlockSpec((n, tr, c), lambda i: (0, i, 0)), row, row, row],
        out_specs=[row, row, row, row],
        out_shape=[jax.ShapeDtypeStruct((r, c), F32)] * 4,
        compiler_params=_params("parallel"),
    )(parts, w, m, v)


EARLY = ("a_w_in", "a_w_mem_kv", "a_w_out", "a_w_gu", "a_w_down", "w_kv", "b_w_q", "b_w_mem_kv")
LATE = ("b_w_out", "b_w_gu", "b_w_down")
BIG = EARLY + LATE
SMALL_SHARDED = ("a_norm_mix", "a_scale", "a_norm_ffn")
SMALL_REPL = ("mem_norm", "kv_norm", "b_norm_mix", "b_norm_ffn", "final_norm")
WEIGHTS = ("mem_norm", "a_norm_mix", "a_w_in", "a_w_group", "a_scale", "a_w_mem_kv", "a_w_out", "a_norm_ffn", "a_w_gu",
           "a_w_down", "kv_norm", "w_kv", "b_norm_mix", "b_w_q", "b_w_mem_kv", "b_w_out", "b_norm_ffn", "b_w_gu",
           "b_w_down", "final_norm")


def _pad_row(v):
    v = v.reshape(1, -1)
    return jnp.pad(v, ((0, 0), (0, PACK_COLS - v.shape[1])))


def kernel(x, mem, mem_norm, a_norm_mix, a_w_in, a_w_group, a_scale, a_w_mem_kv, a_w_out, a_norm_ffn, a_w_gu, a_w_down, kv_norm, w_kv, b_norm_mix, b_w_q, b_w_mem_kv, b_w_out, b_norm_ffn, b_w_gu, b_w_down, final_norm, loss_target, m_mem_norm, m_a_norm_mix, m_a_w_in, m_a_w_group, m_a_scale, m_a_w_mem_kv, m_a_w_out, m_a_norm_ffn, m_a_w_gu, m_a_w_down, m_kv_norm, m_w_kv, m_b_norm_mix, m_b_w_q, m_b_w_mem_kv, m_b_w_out, m_b_norm_ffn, m_b_w_gu, m_b_w_down, m_final_norm, v_mem_norm, v_a_norm_mix, v_a_w_in, v_a_w_group, v_a_scale, v_a_w_mem_kv, v_a_w_out, v_a_norm_ffn, v_a_w_gu, v_a_w_down, v_kv_norm, v_w_kv, v_b_norm_mix, v_b_w_q, v_b_w_mem_kv, v_b_w_out, v_b_norm_ffn, v_b_w_gu, v_b_w_down, v_final_norm):
    w = dict(mem_norm=mem_norm, a_norm_mix=a_norm_mix, a_w_in=a_w_in, a_w_group=a_w_group, a_scale=a_scale,
             a_w_mem_kv=a_w_mem_kv, a_w_out=a_w_out, a_norm_ffn=a_norm_ffn, a_w_gu=a_w_gu, a_w_down=a_w_down,
             kv_norm=kv_norm, w_kv=w_kv, b_norm_mix=b_norm_mix, b_w_q=b_w_q, b_w_mem_kv=b_w_mem_kv, b_w_out=b_w_out,
             b_norm_ffn=b_norm_ffn, b_w_gu=b_w_gu, b_w_down=b_w_down, final_norm=final_norm)
    mom = dict(mem_norm=m_mem_norm, a_norm_mix=m_a_norm_mix, a_w_in=m_a_w_in, a_w_group=m_a_w_group, a_scale=m_a_scale,
               a_w_mem_kv=m_a_w_mem_kv, a_w_out=m_a_w_out, a_norm_ffn=m_a_norm_ffn, a_w_gu=m_a_w_gu,
               a_w_down=m_a_w_down, kv_norm=m_kv_norm, w_kv=m_w_kv, b_norm_mix=m_b_norm_mix, b_w_q=m_b_w_q,
               b_w_mem_kv=m_b_w_mem_kv, b_w_out=m_b_w_out, b_norm_ffn=m_b_norm_ffn, b_w_gu=m_b_w_gu,
               b_w_down=m_b_w_down, final_norm=m_final_norm)
    var = dict(mem_norm=v_mem_norm, a_norm_mix=v_a_norm_mix, a_w_in=v_a_w_in, a_w_group=v_a_w_group, a_scale=v_a_scale,
               a_w_mem_kv=v_a_w_mem_kv, a_w_out=v_a_w_out, a_norm_ffn=v_a_norm_ffn, a_w_gu=v_a_w_gu,
               a_w_down=v_a_w_down, kv_norm=v_kv_norm, w_kv=v_w_kv, b_norm_mix=v_b_norm_mix, b_w_q=v_b_w_q,
               b_w_mem_kv=v_b_w_mem_kv, b_w_out=v_b_w_out, b_norm_ffn=v_b_norm_ffn, b_w_gu=v_b_w_gu,
               b_w_down=v_b_w_down, final_norm=v_final_norm)

    me = 4 * lax.axis_index("x") + 2 * lax.axis_index("y") + lax.axis_index("c")
    shard2d = {n: w[n].shape[-2:] for n in BIG}
    shard = {n: w[n].reshape(shard2d[n]).astype(BF16) for n in BIG}
    n_sh = N_DEV // 2
    s_len = x.shape[1]

    small_send = jnp.concatenate([_pad_row(w[n]) for n in SMALL_SHARDED]
                                 + [jnp.zeros((8 - len(SMALL_SHARDED), PACK_COLS), F32)], axis=0)
    got = exchange([], [shard[n] for n in EARLY] + [small_send], "gather_weights")
    gathered = dict(zip(EARLY, got))
    small_all = got[-1]
    a_norm_mix_f = small_all[:, 0, :a_norm_mix.shape[1]].reshape(-1)
    a_scale_f = small_all[:, 1, :a_scale.shape[1]].reshape(-1)
    a_norm_ffn_f = small_all[:, 2, :a_norm_ffn.shape[1]].reshape(-1)

    def rows_full(n):
        return gathered[n].reshape(-1, shard2d[n][1])

    def ffn_views(tag):
        gu8, down = gathered[tag + "_w_gu"], gathered[tag + "_w_down"]
        return gu8.reshape((2, n_sh) + gu8.shape[1:]), gu8, down.reshape(n_sh, -1, down.shape[2])

    wkv = jnp.transpose(gathered["w_kv"], (1, 0, 2)).reshape(shard2d["w_kv"][0], -1)
    wbd = jnp.zeros((SB_WIDTH, SB_WIDTH), BF16)
    for g in range(4):
        sl = slice(g * POOL_GROUP, (g + 1) * POOL_GROUP)
        wbd = wbd.at[sl, sl].set(a_w_group[0, g].astype(BF16))

    xs, mems, tgt = x[0], mem[0], loss_target[0]

    memn = rmsnorm_fwd(mems, mem_norm, "memn")
    h0 = rmsnorm_fwd(xs, a_norm_mix_f, "h0")
    proj = matmul([(h0, rows_full("a_w_in"))], "nn", F32, "proj_a")
    mkv_a = matmul([(memn, rows_full("a_w_mem_kv"))], "nn", BF16, "mkv_a")
    pooled, cat_a = pool_fwd(proj, wbd, a_scale_f, "pool_fwd")
    cat_a = mem_attn_fwd(proj, mkv_a, cat_a, "mem_fwd_a")
    x1 = matmul([(cat_a, rows_full("a_w_out"))], "nn", F32, "out_a", res=xs)
    h1 = rmsnorm_fwd(x1, a_norm_ffn_f, "h1")
    wgu_a, wgu8_a, wdown_a = ffn_views("a")
    gu_a, act_a = ffn_up(h1, wgu_a, "ffn_up_a")
    x2 = matmul([(act_a, wdown_a)], "nn", F32, "down_a", res=x1, roles=("reduce", "reduce"), count=n_sh)

    hk = rmsnorm_fwd(x2, kv_norm, "hk")
    kv = matmul([(hk, wkv)], "nn", BF16, "kv")
    h3 = rmsnorm_fwd(x2, b_norm_mix[0], "h3")
    projb = matmul([(h3, rows_full("b_w_q"))], "nn", F32, "proj_b")
    mkv_b = matmul([(memn, rows_full("b_w_mem_kv"))], "nn", BF16, "mkv_b")
    cat_b, csave, got = sb_fwd(projb, kv, "sb_fwd", gather=[shard[n] for n in LATE])
    gathered.update(zip(LATE, got))
    cat_b = mem_attn_fwd(projb, mkv_b, cat_b, "mem_fwd_b")
    x3 = matmul([(cat_b, rows_full("b_w_out"))], "nn", F32, "out_b", res=x2)
    h4 = rmsnorm_fwd(x3, b_norm_ffn[0], "h4")
    wgu_b, wgu8_b, wdown_b = ffn_views("b")
    gu_b, act_b = ffn_up(h4, wgu_b, "ffn_up_b")
    x4 = matmul([(act_b, wdown_b)], "nn", F32, "down_b", res=x3, roles=("reduce", "reduce"), count=n_sh)

    grads = {}
    dx4, dx4_bf, grads["final_norm"], loss_part = loss_head(x4, tgt, final_norm, "loss_head")

    def blocks(n, g):
        return g.reshape((N_DEV,) + tuple(shard2d[n]))

    def ffn_backward(tag, x_in, dx_out, dx_out_bf, h, gu, act, wgu8, wdown, g_ffn):
        fs = wgu8.shape[2]
        dgu8 = ffn_bwd_act(dx_out_bf, wdown, gu, "ffn_bwd_act_" + tag).reshape(N_DEV, s_len, fs)
        d_wdown = matmul([(act, dx_out_bf)], "tn", BF16, "dw_down_" + tag, tm=fs, roles=("batch", None), count=n_sh)
        dh = matmul([(dgu8, wgu8)], "nt", F32, "dh_ffn_" + tag, roles=("reduce", "reduce"), count=N_DEV)
        d_wgu = matmul([(h, dgu8)], "tn", BF16, "dw_gu_" + tag, tn=fs, roles=(None, "batch"), count=N_DEV)
        dx_in, dx_in_bf, dgs = rmsnorm_bwd(x_in, dx_out, [(g_ffn, dh)], "norm_bwd_ffn_" + tag)
        return dx_in, dx_in_bf, dgs[0], blocks(tag + "_w_down", d_wdown), d_wgu

    dx3, dx3_bf, grads["b_norm_ffn"], grads["b_w_down"], grads["b_w_gu"] = ffn_backward(
        "b", x3, dx4, dx4_bf, h4, gu_b, act_b, wgu8_b, wdown_b, b_norm_ffn[0])

    dcat_b = matmul([(dx3_bf, rows_full("b_w_out"))], "nt", F32, "dcat_b")
    grads["b_w_out"] = blocks("b_w_out", matmul([(cat_b, dx3_bf)], "tn", BF16, "dw_out_b"))
    dprojb, dk, dv, recv_late = sb_bwd(projb, kv, dcat_b, csave, "sb_bwd", scatter=[grads[n] for n in LATE])
    dprojb, dmkv_b = mem_attn_bwd(projb, mkv_b, dcat_b, dprojb, "mem_bwd_b")
    dh3 = matmul([(dprojb, rows_full("b_w_q"))], "nt", F32, "dh3")
    grads["b_w_q"] = blocks("b_w_q", matmul([(h3, dprojb)], "tn", BF16, "dw_q_b"))
    dhk = matmul([(dk, wkv[:, :SB_WIDTH]), (dv, wkv[:, SB_WIDTH:])], "nt", F32, "dhk")
    d_wkv = jnp.concatenate([matmul([(hk, dk)], "tn", BF16, "dw_k", tn=SB_WIDTH),
                             matmul([(hk, dv)], "tn", BF16, "dw_v", tn=SB_WIDTH)], axis=1)
    grads["w_kv"] = jnp.transpose(d_wkv.reshape(d_wkv.shape[0], N_DEV, -1), (1, 0, 2))
    dx2, dx2_bf, dgs = rmsnorm_bwd(x2, dx3, [(b_norm_mix[0], dh3), (kv_norm, dhk)], "norm_bwd_x2")
    grads["b_norm_mix"], grads["kv_norm"] = dgs
    dmkv_b_bf = dmkv_b.astype(BF16)
    grads["b_w_mem_kv"] = blocks("b_w_mem_kv", matmul([(memn, dmkv_b_bf)], "tn", BF16, "dw_mkv_b"))

    dx1, dx1_bf, grads["a_norm_ffn"], grads["a_w_down"], grads["a_w_gu"] = ffn_backward(
        "a", x1, dx2, dx2_bf, h1, gu_a, act_a, wgu8_a, wdown_a, a_norm_ffn_f)

    dcat_a = matmul([(dx1_bf, rows_full("a_w_out"))], "nt", F32, "dcat_a")
    grads["a_w_out"] = blocks("a_w_out", matmul([(cat_a, dx1_bf)], "tn", BF16, "dw_out_a"))
    dproj, d_wbd, grads["a_scale"] = pool_bwd(dcat_a, pooled, wbd, a_scale_f, "pool_bwd")
    grads["a_w_group"] = jnp.stack(
        [d_wbd[g * POOL_GROUP:(g + 1) * POOL_GROUP, g * POOL_GROUP:(g + 1) * POOL_GROUP] for g in range(4)])
    dproj, dmkv_a = mem_attn_bwd(proj, mkv_a, dcat_a, dproj, "mem_bwd_a")
    dh0 = matmul([(dproj, rows_full("a_w_in"))], "nt", F32, "dh0")
    grads["a_w_in"] = blocks("a_w_in", matmul([(h0, dproj)], "tn", BF16, "dw_in_a"))
    dx, _, dgs = rmsnorm_bwd(xs, dx1, [(a_norm_mix_f, dh0)], "norm_bwd_x")
    grads["a_norm_mix"] = dgs[0]
    dmkv_a_bf = dmkv_a.astype(BF16)
    grads["a_w_mem_kv"] = blocks("a_w_mem_kv", matmul([(memn, dmkv_a_bf)], "tn", BF16, "dw_mkv_a"))
    dmemn = matmul([(dmkv_a_bf, rows_full("a_w_mem_kv")), (dmkv_b_bf, rows_full("b_w_mem_kv"))], "nt", F32, "dmemn")
    _, _, dgs = rmsnorm_bwd(mems, jnp.zeros_like(mems), [(mem_norm, dmemn)], "norm_bwd_mem")
    grads["mem_norm"] = dgs[0]

    small_names = SMALL_REPL + SMALL_SHARDED
    small_rows = [_pad_row(grads[n]) for n in small_names] + [_pad_row(loss_part[:, :1])]
    small_rows.append(grads["a_w_group"].reshape(-1, PACK_COLS))
    n_small = sum(r.shape[0] for r in small_rows)
    pad_small = -n_small % 8
    small_grads = jnp.concatenate(small_rows + [jnp.zeros((pad_small, PACK_COLS), F32)], axis=0)
    got = exchange([grads[n] for n in EARLY], [small_grads], "exchange_grads")
    recv = dict(zip(EARLY, got))
    recv.update(zip(LATE, recv_late))
    small_recv = got[-1]

    def place(v, n):
        v = v.reshape(1, -1)
        return lax.dynamic_update_slice(jnp.zeros((1, PACK_COLS), F32), v, (0, me * v.shape[1]))

    def small_pack(src):
        rws = [_pad_row(src[n]) for n in SMALL_REPL] + [place(src[n], n) for n in SMALL_SHARDED]
        rws += [jnp.zeros((1, PACK_COLS), F32), src["a_w_group"].reshape(-1, PACK_COLS),
                jnp.zeros((pad_small, PACK_COLS), F32)]
        return jnp.concatenate(rws, axis=0)

    big_out = {n: reduce_adamw(recv[n], w[n].reshape(shard2d[n]), mom[n].reshape(shard2d[n]),
                               var[n].reshape(shard2d[n]), "adamw_" + n) for n in BIG}
    small_out = reduce_adamw(small_recv, small_pack(w), small_pack(mom), small_pack(var), "adamw_small")

    def unpack(kind):
        out = {n: big_out[n][kind].reshape(w[n].shape) for n in BIG}
        so = small_out[kind]
        for r, n in enumerate(SMALL_REPL):
            out[n] = so[r, :w[n].shape[-1]].reshape(w[n].shape)
        for r, n in enumerate(SMALL_SHARDED):
            width = w[n].shape[-1]
            out[n] = lax.dynamic_slice(so, (len(SMALL_REPL) + r, me * width), (1, width)).reshape(w[n].shape)
        r0 = len(small_names) + 1
        out["a_w_group"] = so[r0:r0 + a_w_group.size // PACK_COLS].reshape(a_w_group.shape)
        return out

    loss = small_out[0][len(small_names), 0]
    results = [loss, dx.reshape(x.shape)]
    for kind in range(4):
        out = unpack(kind)
        results += [out[n] for n in WEIGHTS]
    return tuple(results)
```

```python
import functools

import jax
import jax.numpy as jnp
from jax import lax
from jax.experimental import pallas as pl
from jax.experimental.pallas import tpu as pltpu

F32 = jnp.float32
BF16 = jnp.bfloat16

N_DEV = 8
HEAD_DIM = 64
N_SB_HEADS = 12
SB_WIDTH = N_SB_HEADS * HEAD_DIM
MEM_WIDTH = 4 * HEAD_DIM
POOL_WINDOWS = (2, 4, 8, 16)
POOL_GROUP = SB_WIDTH // 4
POOL_HALO = 16
EPS = 1e-6
QK_SCALE = HEAD_DIM ** -0.5
LANES = 128
PACK_COLS = 1024

ADAM_LR = 0.001
ADAM_B1 = 0.9
ADAM_B2 = 0.999
ADAM_EPS = 1e-08
ADAM_WD = 0.01
ADAM_STEP = 10

VMEM_LIMIT = 48 * 1024 * 1024

_NT = (((1,), (1,)), ((), ()))
_TN = (((0,), (0,)), ((), ()))
_NN = (((1,), (0,)), ((), ()))


def _params(*sem):
    return pltpu.CompilerParams(dimension_semantics=sem, vmem_limit_bytes=VMEM_LIMIT)


def _dot(a, b, dims=_NN):
    return lax.dot_general(a, b, dims, preferred_element_type=F32)


def exchange(a2a, ag, name):
    n_a2a, n_arr = len(a2a), len(a2a) + len(ag)

    def body(*refs):
        start, wait = _exchange_ops(refs[:n_arr], refs[n_arr:2 * n_arr], refs[2 * n_arr:], n_a2a)
        start()
        wait()

    return pl.pallas_call(
        body, name=name, out_shape=_exchange_out_shape(a2a, ag),
        in_specs=[_ANY] * n_arr, out_specs=[_ANY] * n_arr,
        scratch_shapes=_exchange_sems(n_arr),
    )(*a2a, *ag)


_ANY = pl.BlockSpec(memory_space=pl.ANY)


def _exchange_out_shape(a2a, ag):
    out_shape = [jax.ShapeDtypeStruct(v.shape, v.dtype) for v in a2a]
    return out_shape + [jax.ShapeDtypeStruct((N_DEV,) + v.shape, v.dtype) for v in ag]


def _exchange_sems(n_arr):
    return [pltpu.SemaphoreType.DMA((n_arr, N_DEV - 1)), pltpu.SemaphoreType.DMA((n_arr, N_DEV - 1)),
            pltpu.SemaphoreType.DMA((n_arr,))]


def _exchange_ops(ins, outs, sems, n_a2a):
    send_sems, recv_sems, local_sems = sems
    n_arr = len(ins)
    x, y, c = lax.axis_index("x"), lax.axis_index("y"), lax.axis_index("c")
    me = 4 * x + 2 * y + c

    def peer_of(k):
        px = 1 - x if k & 4 else x
        py = 1 - y if k & 2 else y
        pc = 1 - c if k & 1 else c
        return (px, py, pc), 4 * px + 2 * py + pc

    def copy(a, k, landing):
        peer, pid = peer_of(k)
        src = ins[a].at[pid] if a < n_a2a else ins[a]
        return pltpu.make_async_remote_copy(
            src_ref=src, dst_ref=outs[a].at[pid if landing else me], send_sem=send_sems.at[a, k - 1],
            recv_sem=recv_sems.at[a, k - 1], device_id=peer, device_id_type=pl.DeviceIdType.MESH)

    def local(a):
        src = ins[a].at[me] if a < n_a2a else ins[a]
        return pltpu.make_async_copy(src, outs[a].at[me], local_sems.at[a])

    order = (6, 7, 4, 5, 2, 3, 1)

    def start():
        for a in range(n_arr):
            local(a).start()
        for k in order:
            for a in range(n_arr):
                copy(a, k, landing=False).start()

    def wait():
        for k in order:
            for a in range(n_arr):
                copy(a, k, landing=True).wait_recv()
        for k in order:
            for a in range(n_arr):
                copy(a, k, landing=False).wait_send()
        for a in range(n_arr):
            local(a).wait()

    return start, wait


def _rows_tile(n, want):
    t = min(want, n)
    assert n % t == 0, (n, t)
    return t


def rmsnorm_fwd(x, g, name):
    s, d = x.shape
    tm = _rows_tile(s, 512)

    def body(x_ref, g_ref, h_ref):
        xv = x_ref[...]
        r = lax.rsqrt(jnp.mean(xv * xv, axis=-1, keepdims=True) + EPS)
        h_ref[...] = ((xv * r) * g_ref[...]).astype(h_ref.dtype)

    return pl.pallas_call(
        body, name=name, grid=(s // tm,),
        in_specs=[pl.BlockSpec((tm, d), lambda i: (i, 0)), pl.BlockSpec((1, d), lambda i: (0, 0))],
        out_specs=pl.BlockSpec((tm, d), lambda i: (i, 0)),
        out_shape=jax.ShapeDtypeStruct((s, d), BF16),
        compiler_params=_params("parallel"),
    )(x, g.reshape(1, d))


def rmsnorm_bwd(x, dres, pairs, name):
    s, d = x.shape
    tm = _rows_tile(s, 256)
    n = len(pairs)

    def body(*refs):
        x_ref, dres_ref = refs[0], refs[1]
        g_refs, dh_refs = refs[2:2 + n], refs[2 + n:2 + 2 * n]
        dx_ref, dxb_ref = refs[2 + 2 * n], refs[3 + 2 * n]
        dg_refs = refs[4 + 2 * n:]
        i = pl.program_id(0)
        xv = x_ref[...]
        r = lax.rsqrt(jnp.mean(xv * xv, axis=-1, keepdims=True) + EPS)
        xhat = xv * r
        acc = dres_ref[...]
        for k in range(n):
            dh = dh_refs[k][...].astype(F32)
            part = jnp.sum(dh * xhat, axis=0, keepdims=True)

            @pl.when(i == 0)
            def _(k=k, part=part):
                dg_refs[k][...] = part

            @pl.when(i > 0)
            def _(k=k, part=part):
                dg_refs[k][...] += part

            dxh = dh * g_refs[k][...]
            acc = acc + r * (dxh - xhat * jnp.mean(dxh * xhat, axis=-1, keepdims=True))
        dx_ref[...] = acc
        dxb_ref[...] = acc.astype(BF16)

    row = pl.BlockSpec((tm, d), lambda i: (i, 0))
    vec = pl.BlockSpec((1, d), lambda i: (0, 0))
    outs = pl.pallas_call(
        body, name=name, grid=(s // tm,),
        in_specs=[row, row] + [vec] * n + [row] * n,
        out_specs=[row, row] + [vec] * n,
        out_shape=[jax.ShapeDtypeStruct((s, d), F32), jax.ShapeDtypeStruct((s, d), BF16)]
        + [jax.ShapeDtypeStruct((1, d), F32)] * n,
        compiler_params=_params("arbitrary"),
    )(x, dres, *[g.reshape(1, d) for g, _ in pairs], *[dh for _, dh in pairs])
    return outs[0], outs[1], outs[2:]


def loss_head(x, tgt, g, name):
    s, d = x.shape
    tm = _rows_tile(s, 256)

    def body(x_ref, t_ref, g_ref, dx_ref, dxb_ref, dg_ref, loss_ref):
        i = pl.program_id(0)
        xv = x_ref[...]
        gv = g_ref[...]
        r = lax.rsqrt(jnp.mean(xv * xv, axis=-1, keepdims=True) + EPS)
        xhat = xv * r
        diff = xhat * gv - t_ref[...]
        part_loss = 0.5 * jnp.sum(jnp.mean(diff * diff, axis=-1, keepdims=True), axis=0, keepdims=True)
        dy = diff * (1.0 / d)
        part_dg = jnp.sum(dy * xhat, axis=0, keepdims=True)

        @pl.when(i == 0)
        def _():
            dg_ref[...] = part_dg
            loss_ref[...] = jnp.broadcast_to(part_loss, loss_ref.shape)

        @pl.when(i > 0)
        def _():
            dg_ref[...] += part_dg
            loss_ref[...] += jnp.broadcast_to(part_loss, loss_ref.shape)

        dxh = dy * gv
        dx = r * (dxh - xhat * jnp.mean(dxh * xhat, axis=-1, keepdims=True))
        dx_ref[...] = dx
        dxb_ref[...] = dx.astype(BF16)

    row = pl.BlockSpec((tm, d), lambda i: (i, 0))
    vec = pl.BlockSpec((1, d), lambda i: (0, 0))
    return pl.pallas_call(
        body, name=name, grid=(s // tm,),
        in_specs=[row, row, vec], out_specs=[row, row, vec, pl.BlockSpec((1, LANES), lambda i: (0, 0))],
        out_shape=[jax.ShapeDtypeStruct((s, d), F32), jax.ShapeDtypeStruct((s, d), BF16),
                   jax.ShapeDtypeStruct((1, d), F32), jax.ShapeDtypeStruct((1, LANES), F32)],
        compiler_params=_params("arbitrary"),
    )(x, tgt, g.reshape(1, d))


def matmul(pairs, mode, out_dtype, name, res=None, tm=512, tn=512, roles=(None, None), count=1):
    a0, b0 = pairs[0]
    m = a0.shape[-1] if mode == "tn" else a0.shape[-2]
    n = b0.shape[-2] if mode == "nt" else b0.shape[-1]
    tm, tn = _rows_tile(m, tm), _rows_tile(n, tn)
    dims = {"nn": _NN, "nt": _NT, "tn": _TN}[mode]
    npairs = len(pairs)
    batched, reducing = "batch" in roles, "reduce" in roles
    assert not (batched and reducing) and (res is None or not batched)
    n_in = 2 * npairs + (res is not None)

    def body(*refs):
        o_ref = refs[n_in]
        acc = None
        for p in range(npairs):
            a_ref, b_ref = refs[2 * p], refs[2 * p + 1]
            for kk in range(count if reducing else 1):
                a = a_ref[kk] if roles[0] == "reduce" else a_ref[...]
                b = b_ref[kk] if roles[1] == "reduce" else b_ref[...]
                d = _dot(a.astype(BF16), b.astype(BF16), dims)
                acc = d if acc is None else acc + d
        if res is not None:
            acc = acc + refs[2 * npairs][...]
        o_ref[...] = acc.astype(o_ref.dtype)

    def spec(arr, role, block, index):
        if arr.ndim == 2:
            return pl.BlockSpec(block, lambda bb, i, j: index(i, j))
        assert role in ("batch", "reduce") and arr.shape[0] == count, (name, role, arr.shape)
        if role == "batch":
            return pl.BlockSpec((None,) + block, lambda bb, i, j: (bb,) + index(i, j))
        return pl.BlockSpec((count,) + block, lambda bb, i, j: (0,) + index(i, j))

    in_specs, args = [], []
    for a, b in pairs:
        if mode == "tn":
            in_specs.append(spec(a, roles[0], (a.shape[-2], tm), lambda i, j: (0, i)))
        else:
            in_specs.append(spec(a, roles[0], (tm, a.shape[-1]), lambda i, j: (i, 0)))
        if mode == "nt":
            in_specs.append(spec(b, roles[1], (tn, b.shape[-1]), lambda i, j: (j, 0)))
        else:
            in_specs.append(spec(b, roles[1], (b.shape[-2], tn), lambda i, j: (0, j)))
        args += [a, b]
    if res is not None:
        in_specs.append(pl.BlockSpec((tm, tn), lambda bb, i, j: (i, j)))
        args.append(res)
    if batched:
        out_spec = pl.BlockSpec((None, tm, tn), lambda bb, i, j: (bb, i, j))
        out_shape = jax.ShapeDtypeStruct((count, m, n), out_dtype)
    else:
        out_spec = pl.BlockSpec((tm, tn), lambda bb, i, j: (i, j))
        out_shape = jax.ShapeDtypeStruct((m, n), out_dtype)
    return pl.pallas_call(
        body, name=name, grid=(count if batched else 1, m // tm, n // tn),
        in_specs=in_specs, out_specs=out_spec, out_shape=out_shape,
        compiler_params=_params("parallel", "parallel", "parallel"),
    )(*args)


def _sigmoid(v):
    return 1.0 / (1.0 + jnp.exp(-v))


def ffn_up(h, wgu, name):
    s, d = h.shape
    _, n_sh, _, fs = wgu.shape
    tm = _rows_tile(s, 512)

    def body(h_ref, w_ref, gu_ref, act_ref):
        hv = h_ref[...]
        gate = _dot(hv, w_ref[0])
        up = _dot(hv, w_ref[1])
        gu_ref[0] = gate.astype(BF16)
        gu_ref[1] = up.astype(BF16)
        act_ref[...] = ((gate * _sigmoid(gate)) * up).astype(BF16)

    return pl.pallas_call(
        body, name=name, grid=(n_sh, s // tm),
        in_specs=[pl.BlockSpec((tm, d), lambda j, i: (i, 0)), pl.BlockSpec((2, None, d, fs), lambda j, i: (0, j, 0, 0))],
        out_specs=[pl.BlockSpec((2, None, tm, fs), lambda j, i: (0, j, i, 0)),
                   pl.BlockSpec((None, tm, fs), lambda j, i: (j, i, 0))],
        out_shape=[jax.ShapeDtypeStruct((2, n_sh, s, fs), BF16), jax.ShapeDtypeStruct((n_sh, s, fs), BF16)],
        compiler_params=_params("parallel", "parallel"),
    )(h, wgu)


def ffn_bwd_act(dxo, wdown, gu, name):
    s, d = dxo.shape
    n_sh, fs, _ = wdown.shape
    tm = _rows_tile(s, 512)

    def body(dx_ref, w_ref, gu_ref, dgu_ref):
        dact = _dot(dx_ref[...], w_ref[...], _NT)
        gate = gu_ref[0].astype(F32)
        sig = _sigmoid(gate)
        dgu_ref[0] = ((dact * gu_ref[1].astype(F32)) * (sig * (1.0 + gate * (1.0 - sig)))).astype(BF16)
        dgu_ref[1] = (dact * (gate * sig)).astype(BF16)

    tile = pl.BlockSpec((2, None, tm, fs), lambda j, i: (0, j, i, 0))
    return pl.pallas_call(
        body, name=name, grid=(n_sh, s // tm),
        in_specs=[pl.BlockSpec((tm, d), lambda j, i: (i, 0)), pl.BlockSpec((None, fs, d), lambda j, i: (j, 0, 0)), tile],
        out_specs=tile,
        out_shape=jax.ShapeDtypeStruct((2, n_sh, s, fs), BF16),
        compiler_params=_params("parallel", "parallel"),
    )(dxo, wdown, gu)


def _pool_select(col, by_window):
    out = by_window[3]
    for g in (2, 1, 0):
        out = jnp.where(col < (g + 1) * POOL_GROUP, by_window[g], out)
    return out


def _pool_counts(pos):
    return [jnp.minimum(pos + 1, w).astype(F32) for w in POOL_WINDOWS]


def pool_fwd(proj, wbd, scale, name):
    s = proj.shape[0]
    tm = _rows_tile(s, 256)
    nt = s // tm
    width = SB_WIDTH
    per = tm // POOL_HALO

    def body(u_ref, halo_ref, w_ref, sc_ref, pooled_ref, grouped_ref):
        i = pl.program_id(0)
        u = u_ref[...]
        halo = jnp.where(i > 0, halo_ref[...], 0.0)
        ext = jnp.concatenate([halo, u], axis=0)
        sums = []
        acc = ext
        for k in (1, 2, 4, 8):
            acc = acc + pltpu.roll(acc, k, 0)
            sums.append(acc[POOL_HALO:])
        pos = i * tm + lax.broadcasted_iota(jnp.int32, (tm, width), 0)
        col = lax.broadcasted_iota(jnp.int32, (tm, width), 1)
        means = [sm / cnt for sm, cnt in zip(sums, _pool_counts(pos))]
        pooled = (_pool_select(col, means) - u).astype(BF16)
        pooled_ref[...] = pooled
        grouped_ref[...] = (_dot(pooled, w_ref[...]) * sc_ref[...]).astype(BF16)

    row = pl.BlockSpec((tm, width), lambda i: (i, 0))
    return pl.pallas_call(
        body, name=name, grid=(nt,),
        in_specs=[row, pl.BlockSpec((POOL_HALO, width), lambda i: (jnp.maximum(i * per - 1, 0), 0)),
                  pl.BlockSpec((width, width), lambda i: (0, 0)), pl.BlockSpec((1, width), lambda i: (0, 0))],
        out_specs=[row, row],
        out_shape=[jax.ShapeDtypeStruct((s, width), BF16), jax.ShapeDtypeStruct((s, width + MEM_WIDTH), BF16)],
        compiler_params=_params("parallel"),
    )(proj, proj, wbd, scale.reshape(1, width))


def pool_bwd(dcat, pooled, wbd, scale, name):
    s = dcat.shape[0]
    tm = _rows_tile(s, 256)
    nt = s // tm
    width = SB_WIDTH
    per = tm // POOL_HALO
    n_ext = tm + POOL_HALO

    def body(dg_ref, halo_ref, pooled_ref, w_ref, sc_ref, du_ref, dw_ref, dsc_ref):
        i = pl.program_id(0)
        w = w_ref[...]
        sc = sc_ref[...]
        dg = dg_ref[...]
        pooled = pooled_ref[...]
        pg = _dot(pooled, w)
        dpg = (dg * sc).astype(BF16)
        part_sc = jnp.sum(dg * pg, axis=0, keepdims=True)
        part_w = _dot(pooled, dpg, _TN)

        @pl.when(i == 0)
        def _():
            dsc_ref[...] = part_sc
            dw_ref[...] = part_w

        @pl.when(i > 0)
        def _():
            dsc_ref[...] += part_sc
            dw_ref[...] += part_w

        dpooled = _dot(dpg, w, _NT)
        halo_dpg = (jnp.where(i < nt - 1, halo_ref[...], 0.0) * sc).astype(BF16)
        ext = jnp.concatenate([dpooled, _dot(halo_dpg, w, _NT)], axis=0)
        pos = i * tm + lax.broadcasted_iota(jnp.int32, (n_ext, width), 0)
        col = lax.broadcasted_iota(jnp.int32, (tm, width), 1)
        outs = []
        for k_idx, cnt in enumerate(_pool_counts(pos)):
            acc = ext / cnt
            for k in (1, 2, 4, 8)[:k_idx + 1]:
                acc = acc + pltpu.roll(acc, n_ext - k, 0)
            outs.append(acc[:tm])
        du_ref[...] = (_pool_select(col, outs) - dpooled).astype(BF16)

    row = pl.BlockSpec((tm, width), lambda i: (i, 0))
    return pl.pallas_call(
        body, name=name, grid=(nt,),
        in_specs=[row, pl.BlockSpec((POOL_HALO, width), lambda i: (jnp.minimum((i + 1) * per, s // POOL_HALO - 1), 0)),
                  row, pl.BlockSpec((width, width), lambda i: (0, 0)), pl.BlockSpec((1, width), lambda i: (0, 0))],
        out_specs=[row, pl.BlockSpec((width, width), lambda i: (0, 0)), pl.BlockSpec((1, width), lambda i: (0, 0))],
        out_shape=[jax.ShapeDtypeStruct((s, width + MEM_WIDTH), BF16), jax.ShapeDtypeStruct((width, width), F32),
                   jax.ShapeDtypeStruct((1, width), F32)],
        compiler_params=_params("arbitrary"),
    )(dcat, dcat, pooled, wbd, scale.reshape(1, width))


def _head_masks(shape):
    lane = lax.broadcasted_iota(jnp.int32, shape, 1)
    return lane < HEAD_DIM, lane >= HEAD_DIM


def _mem_probs(qh, kp):
    logits = _dot(qh, kp, _NT)
    logits = logits - jnp.max(logits, axis=-1, keepdims=True)
    p = jnp.exp(logits)
    return p / jnp.sum(p, axis=-1, keepdims=True)


def mem_attn_fwd(proj, mkv, cat, name):
    s = proj.shape[0]
    m = mkv.shape[0]
    tm = _rows_tile(s, 256)
    q_blk = SB_WIDTH // MEM_WIDTH

    def body(q_ref, kv_ref, _, o_ref):
        masks = _head_masks((tm, LANES))
        for pr in range(MEM_WIDTH // LANES):
            q = q_ref[:, pr * LANES:(pr + 1) * LANES] * QK_SCALE
            kp = kv_ref[:, pr * LANES:(pr + 1) * LANES]
            vp = kv_ref[:, MEM_WIDTH + pr * LANES:MEM_WIDTH + (pr + 1) * LANES]
            outs = []
            for hm in masks:
                p = _mem_probs(jnp.where(hm, q, 0.0).astype(BF16), kp)
                outs.append(_dot(p.astype(BF16), vp))
            o_ref[:, pr * LANES:(pr + 1) * LANES] = jnp.where(masks[0], outs[0], outs[1]).astype(BF16)

    return pl.pallas_call(
        body, name=name, grid=(s // tm,),
        in_specs=[pl.BlockSpec((tm, MEM_WIDTH), lambda i: (i, q_blk)), pl.BlockSpec((m, 2 * MEM_WIDTH), lambda i: (0, 0)),
                  _ANY],
        out_specs=pl.BlockSpec((tm, MEM_WIDTH), lambda i: (i, q_blk)),
        out_shape=jax.ShapeDtypeStruct(cat.shape, BF16),
        input_output_aliases={2: 0},
        compiler_params=_params("parallel"),
    )(proj, mkv, cat)


def mem_attn_bwd(proj, mkv, dcat, dproj, name):
    s = proj.shape[0]
    m = mkv.shape[0]
    tm = _rows_tile(s, 256)
    q_blk = SB_WIDTH // MEM_WIDTH

    def body(q_ref, kv_ref, do_ref, _, dq_ref, dkv_ref):
        i = pl.program_id(0)

        @pl.when(i == 0)
        def _():
            dkv_ref[...] = jnp.zeros_like(dkv_ref)

        masks = _head_masks((tm, LANES))
        for pr in range(MEM_WIDTH // LANES):
            q = q_ref[:, pr * LANES:(pr + 1) * LANES] * QK_SCALE
            do = do_ref[:, pr * LANES:(pr + 1) * LANES]
            kp = kv_ref[:, pr * LANES:(pr + 1) * LANES]
            vp = kv_ref[:, MEM_WIDTH + pr * LANES:MEM_WIDTH + (pr + 1) * LANES]
            dqs = []
            dk = jnp.zeros((m, LANES), F32)
            dv = jnp.zeros((m, LANES), F32)
            for hm in masks:
                qh = jnp.where(hm, q, 0.0).astype(BF16)
                doh = jnp.where(hm, do, 0.0).astype(BF16)
                p = _mem_probs(qh, kp)
                dp = _dot(doh, vp, _NT)
                ds = (p * (dp - jnp.sum(dp * p, axis=-1, keepdims=True))).astype(BF16)
                dqs.append(_dot(ds, kp))
                dk = dk + _dot(ds, qh, _TN)
                dv = dv + _dot(p.astype(BF16), doh, _TN)
            dq_ref[:, pr * LANES:(pr + 1) * LANES] = (jnp.where(masks[0], dqs[0], dqs[1]) * QK_SCALE).astype(BF16)
            dkv_ref[:, pr * LANES:(pr + 1) * LANES] += dk
            dkv_ref[:, MEM_WIDTH + pr * LANES:MEM_WIDTH + (pr + 1) * LANES] += dv

    return pl.pallas_call(
        body, name=name, grid=(s // tm,),
        in_specs=[pl.BlockSpec((tm, MEM_WIDTH), lambda i: (i, q_blk)), pl.BlockSpec((m, 2 * MEM_WIDTH), lambda i: (0, 0)),
                  pl.BlockSpec((tm, MEM_WIDTH), lambda i: (i, q_blk)), _ANY],
        out_specs=[pl.BlockSpec((tm, MEM_WIDTH), lambda i: (i, q_blk)), pl.BlockSpec((m, 2 * MEM_WIDTH), lambda i: (0, 0))],
        out_shape=[jax.ShapeDtypeStruct(dproj.shape, BF16), jax.ShapeDtypeStruct((m, 2 * MEM_WIDTH), F32)],
        input_output_aliases={3: 0},
        compiler_params=_params("arbitrary"),
    )(proj, mkv, dcat, dproj)


SB_ROWS = 256
SB_KEYS = 512
SB_SUB = LANES


def _sb_tri(later):
    r = lax.broadcasted_iota(jnp.int32, (2 * SB_SUB, 2 * SB_SUB), 0) % SB_SUB
    c = lax.broadcasted_iota(jnp.int32, (2 * SB_SUB, 2 * SB_SUB), 1)
    tri = (r > c) if later else (r < c)
    return (tri | (c >= SB_SUB)).astype(BF16)


def _sb_sums(v, tri, n_sub):
    out = []
    for m in range(n_sub):
        vm = v[:, m * SB_SUB:(m + 1) * SB_SUB]
        hi = vm.astype(BF16)
        lo = (vm - hi.astype(F32)).astype(BF16)
        r = _dot(jnp.concatenate([hi, lo], axis=1), tri)
        out.append((r[:, :SB_SUB], r[:, SB_SUB:]))
    return out


def _sb_logs(nqh, k_blk, valid):
    nz = _dot(nqh, k_blk, _NT)
    log_not = jnp.minimum(nz, 0.0) - jnp.log(1.0 + jnp.exp(-jnp.abs(nz)))
    log_beta = log_not - nz
    if valid is not None:
        log_not = jnp.where(valid, log_not, 0.0)
    return log_not, log_beta


def _sb_weights(log_not, log_beta, tri_later, carry_in, valid, n_sub):
    sums = _sb_sums(log_not, tri_later, n_sub)
    run = carry_in
    later = [None] * n_sub
    for m in reversed(range(n_sub)):
        later[m] = sums[m][0] + run
        run = run + sums[m][1]
    w = jnp.exp(log_beta + jnp.concatenate(later, axis=1))
    if valid is not None:
        w = jnp.where(valid, w, 0.0)
    return w, run


def _sb_valid(i, j, tq, tk):
    qpos = i * tq + lax.broadcasted_iota(jnp.int32, (tq, tk), 0)
    kpos = j * tk + lax.broadcasted_iota(jnp.int32, (tq, tk), 1)
    return qpos > kpos


def sb_fwd(projb, kv, name, gather=()):
    s = projb.shape[0]
    tq, tk = _rows_tile(s, SB_ROWS), _rows_tile(s, SB_KEYS)
    n_sub = tk // SB_SUB
    n_pairs = SB_WIDTH // LANES
    assert s // tk <= LANES and tk % tq == 0

    n_g = len(gather)
    n_i = s // tq

    def body(*refs):
        q_ref, k_ref, v_ref = refs[:3]
        o_ref, c_ref = refs[3 + n_g:5 + n_g]
        i = pl.program_id(1)
        if n_g:
            start, wait = _exchange_ops(refs[3:3 + n_g], refs[5 + n_g:5 + 2 * n_g], refs[5 + 2 * n_g:], 0)
            pl.when((pl.program_id(0) == 0) & (i == 0))(start)
        j_diag = (i * tq) // tk
        masks = _head_masks((tq, LANES))
        nq = q_ref[...] * (-QK_SCALE)
        tri = _sb_tri(later=True)
        lane = lax.broadcasted_iota(jnp.int32, (tq, LANES), 1)
        nqh = [jnp.where(hm, nq, 0.0).astype(BF16) for hm in masks]

        def block(j, carry, valid):
            off = pl.multiple_of(j * tk, tk)
            k_blk, v_blk = k_ref[pl.ds(off, tk), :], v_ref[pl.ds(off, tk), :]
            out = []
            for hh in range(2):
                c, acc, cmat = carry[3 * hh:3 * hh + 3]
                log_not, log_beta = _sb_logs(nqh[hh], k_blk, valid)
                w, c_next = _sb_weights(log_not, log_beta, tri, c, valid, n_sub)
                out += [c_next, acc + _dot(w.astype(BF16), v_blk), jnp.where(lane == j, c, cmat)]
            return tuple(out)

        zero = jnp.zeros((tq, LANES), F32)
        carry = block(j_diag, (zero,) * 6, _sb_valid(i, j_diag, tq, tk))
        carry = lax.fori_loop(0, j_diag, lambda jj, cr: block(j_diag - 1 - jj, cr, None), carry)
        c_ref[0] = carry[2]
        c_ref[1] = carry[5]
        o_ref[...] = jnp.where(masks[0], carry[1], carry[4]).astype(BF16)
        if n_g:
            pl.when((pl.program_id(0) == n_pairs - 1) & (i == n_i - 1))(wait)

    outs = pl.pallas_call(
        body, name=name, grid=(n_pairs, n_i),
        in_specs=[pl.BlockSpec((tq, LANES), lambda p, i: (i, p)),
                  pl.BlockSpec((s, LANES), lambda p, i: (0, p)),
                  pl.BlockSpec((s, LANES), lambda p, i: (0, n_pairs + p))] + [_ANY] * n_g,
        out_specs=[pl.BlockSpec((tq, LANES), lambda p, i: (i, p)),
                   pl.BlockSpec((2, tq, LANES), lambda p, i: (p, i, 0))] + [_ANY] * n_g,
        out_shape=[jax.ShapeDtypeStruct((s, SB_WIDTH + MEM_WIDTH), BF16),
                   jax.ShapeDtypeStruct((N_SB_HEADS, s, LANES), F32)] + _exchange_out_shape([], gather),
        scratch_shapes=_exchange_sems(n_g) if n_g else [],
        compiler_params=_params("arbitrary", "arbitrary"),
    )(projb, kv, kv, *gather)
    return outs[0], outs[1], outs[2:]


def sb_bwd(projb, kv, dcat, csave, name, scatter=()):
    s = projb.shape[0]
    tq, tk = _rows_tile(s, SB_ROWS), _rows_tile(s, SB_KEYS)
    n_sub = tk // SB_SUB
    n_pairs = SB_WIDTH // LANES
    n_x = len(scatter)
    n_i = s // tq

    def body(*refs):
        q_ref, k_ref, v_ref, do_ref, c_ref = refs[:5]
        dq_ref, dkb_ref, dvb_ref = refs[5 + n_x:8 + n_x]
        dk_ref, dv_ref = refs[8 + 2 * n_x:10 + 2 * n_x]
        i = pl.program_id(1)
        if n_x:
            start, wait = _exchange_ops(refs[5:5 + n_x], refs[8 + n_x:8 + 2 * n_x], refs[10 + 2 * n_x:], n_x)
            pl.when((pl.program_id(0) == 0) & (i == 0))(start)
        j_diag = (i * tq) // tk

        @pl.when(i == 0)
        def _():
            dk_ref[...] = jnp.zeros_like(dk_ref)
            dv_ref[...] = jnp.zeros_like(dv_ref)

        masks = _head_masks((tq, LANES))
        qs = q_ref[...] * QK_SCALE
        do = do_ref[...]
        tri_later = _sb_tri(later=True)
        tri_before = _sb_tri(later=False)
        lane = lax.broadcasted_iota(jnp.int32, (tq, LANES), 1)
        qh = [jnp.where(hm, qs, 0.0).astype(BF16) for hm in masks]
        nqh = [jnp.where(hm, -qs, 0.0).astype(BF16) for hm in masks]
        doh = [jnp.where(hm, do, 0.0).astype(BF16) for hm in masks]

        def block(j, carry, valid):
            off = pl.multiple_of(j * tk, tk)
            k_blk, v_blk = k_ref[pl.ds(off, tk), :], v_ref[pl.ds(off, tk), :]
            out, dk, dv = [], None, None
            for hh in range(2):
                e_before, dq = carry[2 * hh:2 * hh + 2]
                log_not, log_beta = _sb_logs(nqh[hh], k_blk, valid)
                c = jnp.sum(jnp.where(lane == j, c_ref[hh], 0.0), axis=1, keepdims=True)
                w, _ = _sb_weights(log_not, log_beta, tri_later, jnp.broadcast_to(c, (tq, LANES)), valid, n_sub)
                dv_h = _dot(w.astype(BF16), doh[hh], _TN)
                e = w * _dot(doh[hh], v_blk, _NT)
                sums = _sb_sums(e, tri_before, n_sub)
                run = e_before
                e_pre = []
                for m in range(n_sub):
                    e_pre.append(sums[m][0] + run)
                    run = run + sums[m][1]
                dz = e - jnp.exp(log_beta) * (e + jnp.concatenate(e_pre, axis=1))
                if valid is not None:
                    dz = jnp.where(valid, dz, 0.0)
                dz = dz.astype(BF16)
                dk_h = _dot(dz, qh[hh], _TN)
                dk, dv = (dk_h, dv_h) if hh == 0 else (dk + dk_h, dv + dv_h)
                out += [run, dq + _dot(dz, k_blk)]
            dk_ref[pl.ds(off, tk), :] += dk
            dv_ref[pl.ds(off, tk), :] += dv
            return tuple(out)

        zero = jnp.zeros((tq, LANES), F32)
        carry = lax.fori_loop(0, j_diag, lambda j, cr: block(j, cr, None), (zero,) * 4)
        carry = block(j_diag, carry, _sb_valid(i, j_diag, tq, tk))
        dq_ref[...] = (jnp.where(masks[0], carry[1], carry[3]) * QK_SCALE).astype(BF16)

        @pl.when(i == n_i - 1)
        def _():
            dkb_ref[...] = dk_ref[...].astype(BF16)
            dvb_ref[...] = dv_ref[...].astype(BF16)

        if n_x:
            pl.when((pl.program_id(0) == n_pairs - 1) & (i == n_i - 1))(wait)

    tile = pl.BlockSpec((tq, LANES), lambda p, i: (i, p))
    full = pl.BlockSpec((s, LANES), lambda p, i: (0, p))
    outs = pl.pallas_call(
        body, name=name, grid=(n_pairs, n_i),
        in_specs=[tile, full, pl.BlockSpec((s, LANES), lambda p, i: (0, n_pairs + p)), tile,
                  pl.BlockSpec((2, tq, LANES), lambda p, i: (p, i, 0))] + [_ANY] * n_x,
        out_specs=[tile, full, full] + [_ANY] * n_x,
        out_shape=[jax.ShapeDtypeStruct((s, SB_WIDTH + MEM_WIDTH), BF16), jax.ShapeDtypeStruct((s, SB_WIDTH), BF16),
                   jax.ShapeDtypeStruct((s, SB_WIDTH), BF16)] + _exchange_out_shape(scatter, []),
        scratch_shapes=[pltpu.VMEM((s, LANES), F32), pltpu.VMEM((s, LANES), F32)]
        + (_exchange_sems(n_x) if n_x else []),
        compiler_params=_params("arbitrary", "arbitrary"),
    )(projb, kv, kv, dcat, csave, *scatter)
    return outs[0], outs[1], outs[2], outs[3:]


def reduce_adamw(parts, w, m, v, name):
    n, r, c = parts.shape
    tr = next((t for t in (128, 176) if r % t == 0), r)
    bias1 = 1.0 - ADAM_B1 ** ADAM_STEP
    bias2 = 1.0 - ADAM_B2 ** ADAM_STEP

    def body(p_ref, w_ref, m_ref, v_ref, g_ref, d_ref, nm_ref, nv_ref):
        g = p_ref[0].astype(F32)
        for k in range(1, n):
            g = g + p_ref[k].astype(F32)
        new_m = ADAM_B1 * m_ref[...] + (1.0 - ADAM_B1) * g
        new_v = ADAM_B2 * v_ref[...] + (1.0 - ADAM_B2) * (g * g)
        m_hat = new_m / bias1
        v_hat = new_v / bias2
        g_ref[...] = g
        d_ref[...] = -ADAM_LR * (m_hat / (jnp.sqrt(v_hat) + ADAM_EPS) + ADAM_WD * w_ref[...])
        nm_ref[...] = new_m
        nv_ref[...] = new_v

    row = pl.BlockSpec((tr, c), lambda i: (i, 0))
    return pl.pallas_call(
        body, name=name, grid=(r // tr,),
        in_specs=[pl.BlockSpec((n, tr, c), lambda i: (0, i, 0)), row, row, row],
        out_specs=[row, row, row, row],
        out_shape=[jax.ShapeDtypeStruct((r, c), F32)] * 4,
        compiler_params=_params("parallel"),
    )(parts, w, m, v)


EARLY = ("a_w_in", "a_w_mem_kv", "a_w_out", "a_w_gu", "a_w_down", "w_kv", "b_w_q", "b_w_mem_kv")
LATE = ("b_w_out", "b_w_gu", "b_w_down")
BIG = EARLY + LATE
SMALL_SHARDED = ("a_norm_mix", "a_scale", "a_norm_ffn")
SMALL_REPL = ("mem_norm", "kv_norm", "b_norm_mix", "b_norm_ffn", "final_norm")
WEIGHTS = ("mem_norm", "a_norm_mix", "a_w_in", "a_w_group", "a_scale", "a_w_mem_kv", "a_w_out", "a_norm_ffn", "a_w_gu",
           "a_w_down", "kv_norm", "w_kv", "b_norm_mix", "b_w_q", "b_w_mem_kv", "b_w_out", "b_norm_ffn", "b_w_gu",
           "b_w_down", "final_norm")


def _pad_row(v):
    v = v.reshape(1, -1)
    return jnp.pad(v, ((0, 0), (0, PACK_COLS - v.shape[1])))


def kernel(x, mem, mem_norm, a_norm_mix, a_w_in, a_w_group, a_scale, a_w_mem_kv, a_w_out, a_norm_ffn, a_w_gu, a_w_down, kv_norm, w_kv, b_norm_mix, b_w_q, b_w_mem_kv, b_w_out, b_norm_ffn, b_w_gu, b_w_down, final_norm, loss_target, m_mem_norm, m_a_norm_mix, m_a_w_in, m_a_w_group, m_a_scale, m_a_w_mem_kv, m_a_w_out, m_a_norm_ffn, m_a_w_gu, m_a_w_down, m_kv_norm, m_w_kv, m_b_norm_mix, m_b_w_q, m_b_w_mem_kv, m_b_w_out, m_b_norm_ffn, m_b_w_gu, m_b_w_down, m_final_norm, v_mem_norm, v_a_norm_mix, v_a_w_in, v_a_w_group, v_a_scale, v_a_w_mem_kv, v_a_w_out, v_a_norm_ffn, v_a_w_gu, v_a_w_down, v_kv_norm, v_w_kv, v_b_norm_mix, v_b_w_q, v_b_w_mem_kv, v_b_w_out, v_b_norm_ffn, v_b_w_gu, v_b_w_down, v_final_norm):
    w = dict(mem_norm=mem_norm, a_norm_mix=a_norm_mix, a_w_in=a_w_in, a_w_group=a_w_group, a_scale=a_scale,
             a_w_mem_kv=a_w_mem_kv, a_w_out=a_w_out, a_norm_ffn=a_norm_ffn, a_w_gu=a_w_gu, a_w_down=a_w_down,
             kv_norm=kv_norm, w_kv=w_kv, b_norm_mix=b_norm_mix, b_w_q=b_w_q, b_w_mem_kv=b_w_mem_kv, b_w_out=b_w_out,
             b_norm_ffn=b_norm_ffn, b_w_gu=b_w_gu, b_w_down=b_w_down, final_norm=final_norm)
    mom = dict(mem_norm=m_mem_norm, a_norm_mix=m_a_norm_mix, a_w_in=m_a_w_in, a_w_group=m_a_w_group, a_scale=m_a_scale,
               a_w_mem_kv=m_a_w_mem_kv, a_w_out=m_a_w_out, a_norm_ffn=m_a_norm_ffn, a_w_gu=m_a_w_gu,
               a_w_down=m_a_w_down, kv_norm=m_kv_norm, w_kv=m_w_kv, b_norm_mix=m_b_norm_mix, b_w_q=m_b_w_q,
               b_w_mem_kv=m_b_w_mem_kv, b_w_out=m_b_w_out, b_norm_ffn=m_b_norm_ffn, b_w_gu=m_b_w_gu,
               b_w_down=m_b_w_down, final_norm=m_final_norm)
    var = dict(mem_norm=v_mem_norm, a_norm_mix=v_a_norm_mix, a_w_in=v_a_w_in, a_w_group=v_a_w_group, a_scale=v_a_scale,
               a_w_mem_kv=v_a_w_mem_kv, a_w_out=v_a_w_out, a_norm_ffn=v_a_norm_ffn, a_w_gu=v_a_w_gu,
               a_w_down=v_a_w_down, kv_norm=v_kv_norm, w_kv=v_w_kv, b_norm_mix=v_b_norm_mix, b_w_q=v_b_w_q,
               b_w_mem_kv=v_b_w_mem_kv, b_w_out=v_b_w_out, b_norm_ffn=v_b_norm_ffn, b_w_gu=v_b_w_gu,
               b_w_down=v_b_w_down, final_norm=v_final_norm)

    me = 4 * lax.axis_index("x") + 2 * lax.axis_index("y") + lax.axis_index("c")
    shard2d = {n: w[n].shape[-2:] for n in BIG}
    shard = {n: w[n].reshape(shard2d[n]).astype(BF16) for n in BIG}
    n_sh = N_DEV // 2
    s_len = x.shape[1]

    small_send = jnp.concatenate([_pad_row(w[n]) for n in SMALL_SHARDED]
                                 + [jnp.zeros((8 - len(SMALL_SHARDED), PACK_COLS), F32)], axis=0)
    got = exchange([], [shard[n] for n in EARLY] + [small_send], "gather_weights")
    gathered = dict(zip(EARLY, got))
    small_all = got[-1]
    a_norm_mix_f = small_all[:, 0, :a_norm_mix.shape[1]].reshape(-1)
    a_scale_f = small_all[:, 1, :a_scale.shape[1]].reshape(-1)
    a_norm_ffn_f = small_all[:, 2, :a_norm_ffn.shape[1]].reshape(-1)

    def rows_full(n):
        return gathered[n].reshape(-1, shard2d[n][1])

    def ffn_views(tag):
        gu8, down = gathered[tag + "_w_gu"], gathered[tag + "_w_down"]
        return gu8.reshape((2, n_sh) + gu8.shape[1:]), gu8, down.reshape(n_sh, -1, down.shape[2])

    wkv = jnp.transpose(gathered["w_kv"], (1, 0, 2)).reshape(shard2d["w_kv"][0], -1)
    wbd = jnp.zeros((SB_WIDTH, SB_WIDTH), BF16)
    for g in range(4):
        sl = slice(g * POOL_GROUP, (g + 1) * POOL_GROUP)
        wbd = wbd.at[sl, sl].set(a_w_group[0, g].astype(BF16))

    xs, mems, tgt = x[0], mem[0], loss_target[0]

    memn = rmsnorm_fwd(mems, mem_norm, "memn")
    h0 = rmsnorm_fwd(xs, a_norm_mix_f, "h0")
    proj = matmul([(h0, rows_full("a_w_in"))], "nn", F32, "proj_a")
    mkv_a = matmul([(memn, rows_full("a_w_mem_kv"))], "nn", BF16, "mkv_a")
    pooled, cat_a = pool_fwd(proj, wbd, a_scale_f, "pool_fwd")
    cat_a = mem_attn_fwd(proj, mkv_a, cat_a, "mem_fwd_a")
    x1 = matmul([(cat_a, rows_full("a_w_out"))], "nn", F32, "out_a", res=xs)
    h1 = rmsnorm_fwd(x1, a_norm_ffn_f, "h1")
    wgu_a, wgu8_a, wdown_a = ffn_views("a")
    gu_a, act_a = ffn_up(h1, wgu_a, "ffn_up_a")
    x2 = matmul([(act_a, wdown_a)], "nn", F32, "down_a", res=x1, roles=("reduce", "reduce"), count=n_sh)

    hk = rmsnorm_fwd(x2, kv_norm, "hk")
    kv = matmul([(hk, wkv)], "nn", BF16, "kv")
    h3 = rmsnorm_fwd(x2, b_norm_mix[0], "h3")
    projb = matmul([(h3, rows_full("b_w_q"))], "nn", F32, "proj_b")
    mkv_b = matmul([(memn, rows_full("b_w_mem_kv"))], "nn", BF16, "mkv_b")
    cat_b, csave, got = sb_fwd(projb, kv, "sb_fwd", gather=[shard[n] for n in LATE])
    gathered.update(zip(LATE, got))
    cat_b = mem_attn_fwd(projb, mkv_b, cat_b, "mem_fwd_b")
    x3 = matmul([(cat_b, rows_full("b_w_out"))], "nn", F32, "out_b", res=x2)
    h4 = rmsnorm_fwd(x3, b_norm_ffn[0], "h4")
    wgu_b, wgu8_b, wdown_b = ffn_views("b")
    gu_b, act_b = ffn_up(h4, wgu_b, "ffn_up_b")
    x4 = matmul([(act_b, wdown_b)], "nn", F32, "down_b", res=x3, roles=("reduce", "reduce"), count=n_sh)

    grads = {}
    dx4, dx4_bf, grads["final_norm"], loss_part = loss_head(x4, tgt, final_norm, "loss_head")

    def blocks(n, g):
        return g.reshape((N_DEV,) + tuple(shard2d[n]))

    def ffn_backward(tag, x_in, dx_out, dx_out_bf, h, gu, act, wgu8, wdown, g_ffn):
        fs = wgu8.shape[2]
        dgu8 = ffn_bwd_act(dx_out_bf, wdown, gu, "ffn_bwd_act_" + tag).reshape(N_DEV, s_len, fs)
        d_wdown = matmul([(act, dx_out_bf)], "tn", BF16, "dw_down_" + tag, tm=fs, roles=("batch", None), count=n_sh)
        dh = matmul([(dgu8, wgu8)], "nt", F32, "dh_ffn_" + tag, roles=("reduce", "reduce"), count=N_DEV)
        d_wgu = matmul([(h, dgu8)], "tn", BF16, "dw_gu_" + tag, tn=fs, roles=(None, "batch"), count=N_DEV)
        dx_in, dx_in_bf, dgs = rmsnorm_bwd(x_in, dx_out, [(g_ffn, dh)], "norm_bwd_ffn_" + tag)
        return dx_in, dx_in_bf, dgs[0], blocks(tag + "_w_down", d_wdown), d_wgu

    dx3, dx3_bf, grads["b_norm_ffn"], grads["b_w_down"], grads["b_w_gu"] = ffn_backward(
        "b", x3, dx4, dx4_bf, h4, gu_b, act_b, wgu8_b, wdown_b, b_norm_ffn[0])

    dcat_b = matmul([(dx3_bf, rows_full("b_w_out"))], "nt", F32, "dcat_b")
    grads["b_w_out"] = blocks("b_w_out", matmul([(cat_b, dx3_bf)], "tn", BF16, "dw_out_b"))
    dprojb, dk, dv, recv_late = sb_bwd(projb, kv, dcat_b, csave, "sb_bwd", scatter=[grads[n] for n in LATE])
    dprojb, dmkv_b = mem_attn_bwd(projb, mkv_b, dcat_b, dprojb, "mem_bwd_b")
    dh3 = matmul([(dprojb, rows_full("b_w_q"))], "nt", F32, "dh3")
    grads["b_w_q"] = blocks("b_w_q", matmul([(h3, dprojb)], "tn", BF16, "dw_q_b"))
    dhk = matmul([(dk, wkv[:, :SB_WIDTH]), (dv, wkv[:, SB_WIDTH:])], "nt", F32, "dhk")
    d_wkv = jnp.concatenate([matmul([(hk, dk)], "tn", BF16, "dw_k", tn=SB_WIDTH),
                             matmul([(hk, dv)], "tn", BF16, "dw_v", tn=SB_WIDTH)], axis=1)
    grads["w_kv"] = jnp.transpose(d_wkv.reshape(d_wkv.shape[0], N_DEV, -1), (1, 0, 2))
    dx2, dx2_bf, dgs = rmsnorm_bwd(x2, dx3, [(b_norm_mix[0], dh3), (kv_norm, dhk)], "norm_bwd_x2")
    grads["b_norm_mix"], grads["kv_norm"] = dgs
    dmkv_b_bf = dmkv_b.astype(BF16)
    grads["b_w_mem_kv"] = blocks("b_w_mem_kv", matmul([(memn, dmkv_b_bf)], "tn", BF16, "dw_mkv_b"))

    dx1, dx1_bf, grads["a_norm_ffn"], grads["a_w_down"], grads["a_w_gu"] = ffn_backward(
        "a", x1, dx2, dx2_bf, h1, gu_a, act_a, wgu8_a, wdown_a, a_norm_ffn_f)

    dcat_a = matmul([(dx1_bf, rows_full("a_w_out"))], "nt", F32, "dcat_a")
    grads["a_w_out"] = blocks("a_w_out", matmul([(cat_a, dx1_bf)], "tn", BF16, "dw_out_a"))
    dproj, d_wbd, grads["a_scale"] = pool_bwd(dcat_a, pooled, wbd, a_scale_f, "pool_bwd")
    grads["a_w_group"] = jnp.stack(
        [d_wbd[g * POOL_GROUP:(g + 1) * POOL_GROUP, g * POOL_GROUP:(g + 1) * POOL_GROUP] for g in range(4)])
    dproj, dmkv_a = mem_attn_bwd(proj, mkv_a, dcat_a, dproj, "mem_bwd_a")
    dh0 = matmul([(dproj, rows_full("a_w_in"))], "nt", F32, "dh0")
    grads["a_w_in"] = blocks("a_w_in", matmul([(h0, dproj)], "tn", BF16, "dw_in_a"))
    dx, _, dgs = rmsnorm_bwd(xs, dx1, [(a_norm_mix_f, dh0)], "norm_bwd_x")
    grads["a_norm_mix"] = dgs[0]
    dmkv_a_bf = dmkv_a.astype(BF16)
    grads["a_w_mem_kv"] = blocks("a_w_mem_kv", matmul([(memn, dmkv_a_bf)], "tn", BF16, "dw_mkv_a"))
    dmemn = matmul([(dmkv_a_bf, rows_full("a_w_mem_kv")), (dmkv_b_bf, rows_full("b_w_mem_kv"))], "nt", F32, "dmemn")
    _, _, dgs = rmsnorm_bwd(mems, jnp.zeros_like(mems), [(mem_norm, dmemn)], "norm_bwd_mem")
    grads["mem_norm"] = dgs[0]

    small_names = SMALL_REPL + SMALL_SHARDED
    small_rows = [_pad_row(grads[n]) for n in small_names] + [_pad_row(loss_part[:, :1])]
    small_rows.append(grads["a_w_group"].reshape(-1, PACK_COLS))
    n_small = sum(r.shape[0] for r in small_rows)
    pad_small = -n_small % 8
    small_grads = jnp.concatenate(small_rows + [jnp.zeros((pad_small, PACK_COLS), F32)], axis=0)
    got = exchange([grads[n] for n in EARLY], [small_grads], "exchange_grads")
    recv = dict(zip(EARLY, got))
    recv.update(zip(LATE, recv_late))
    small_recv = got[-1]

    def place(v, n):
        v = v.reshape(1, -1)
        return lax.dynamic_update_slice(jnp.zeros((1, PACK_COLS), F32), v, (0, me * v.shape[1]))

    def small_pack(src):
        rws = [_pad_row(src[n]) for n in SMALL_REPL] + [place(src[n], n) for n in SMALL_SHARDED]
        rws += [jnp.zeros((1, PACK_COLS), F32), src["a_w_group"].reshape(-1, PACK_COLS),
                jnp.zeros((pad_small, PACK_COLS), F32)]
        return jnp.concatenate(rws, axis=0)

    big_out = {n: reduce_adamw(recv[n], w[n].reshape(shard2d[n]), mom[n].reshape(shard2d[n]),
                               var[n].reshape(shard2d[n]), "adamw_" + n) for n in BIG}
    small_out = reduce_adamw(small_recv, small_pack(w), small_pack(mom), small_pack(var), "adamw_small")

    def unpack(kind):
        out = {n: big_out[n][kind].reshape(w[n].shape) for n in BIG}
        so = small_out[kind]
        for r, n in enumerate(SMALL_REPL):
            out[n] = so[r, :w[n].shape[-1]].reshape(w[n].shape)
        for r, n in enumerate(SMALL_SHARDED):
            width = w[n].shape[-1]
            out[n] = lax.dynamic_slice(so, (len(SMALL_REPL) + r, me * width), (1, width)).reshape(w[n].shape)
        r0 = len(small_names) + 1
        out["a_w_group"] = so[r0:r0 + a_w_group.size // PACK_COLS].reshape(a_w_group.shape)
        return out

    loss = small_out[0][len(small_names), 0]
    results = [loss, dx.reshape(x.shape)]
    for kind in range(4):
        out = unpack(kind)
        results += [out[n] for n in WEIGHTS]
    return tuple(results)
```

```python
import functools

import jax
import jax.numpy as jnp
from jax import lax
from jax.experimental import pallas as pl
from jax.experimental.pallas import tpu as pltpu

F32 = jnp.float32
BF16 = jnp.bfloat16

N_DEV = 8
HEAD_DIM = 64
N_SB_HEADS = 12
SB_WIDTH = N_SB_HEADS * HEAD_DIM
MEM_WIDTH = 4 * HEAD_DIM
POOL_WINDOWS = (2, 4, 8, 16)
POOL_GROUP = SB_WIDTH // 4
POOL_HALO = 16
EPS = 1e-6
QK_SCALE = HEAD_DIM ** -0.5
LANES = 128
PACK_COLS = 1024

ADAM_LR = 0.001
ADAM_B1 = 0.9
ADAM_B2 = 0.999
ADAM_EPS = 1e-08
ADAM_WD = 0.01
ADAM_STEP = 10

VMEM_LIMIT = 48 * 1024 * 1024

_NT = (((1,), (1,)), ((), ()))
_TN = (((0,), (0,)), ((), ()))
_NN = (((1,), (0,)), ((), ()))


def _params(*sem):
    return pltpu.CompilerParams(dimension_semantics=sem, vmem_limit_bytes=VMEM_LIMIT)


def _dot(a, b, dims=_NN):
    return lax.dot_general(a, b, dims, preferred_element_type=F32)


def exchange(a2a, ag, name):
    n_a2a, n_arr = len(a2a), len(a2a) + len(ag)

    def body(*refs):
        start, wait = _exchange_ops(refs[:n_arr], refs[n_arr:2 * n_arr], refs[2 * n_arr:], n_a2a)
        start()
        wait()

    return pl.pallas_call(
        body, name=name, out_shape=_exchange_out_shape(a2a, ag),
        in_specs=[_ANY] * n_arr, out_specs=[_ANY] * n_arr,
        scratch_shapes=_exchange_sems(n_arr),
    )(*a2a, *ag)


_ANY = pl.BlockSpec(memory_space=pl.ANY)
_HBM = pl.BlockSpec(memory_space=pltpu.HBM)
_SEM = pl.BlockSpec(memory_space=pltpu.SEMAPHORE)
_DATAFLOW = pltpu.SideEffectType.DATAFLOW_SIDE_EFFECTING


def exchange_start(a2a, ag, after, name):
    n_a2a, n_arr = len(a2a), len(a2a) + len(ag)
    arrays = [pltpu.with_memory_space_constraint(v, pltpu.HBM) for v in list(a2a) + list(ag)]
    shapes = _exchange_out_shape(a2a, ag)
    lands = [pltpu.with_memory_space_constraint(lax.empty(sh.shape, sh.dtype), pltpu.HBM) for sh in shapes]

    def body(*refs):
        ins, outs = refs[:n_arr], refs[n_arr:2 * n_arr]
        send_sems, recv_sems, token = refs[4 * n_arr + 1:4 * n_arr + 4]
        start, _ = _exchange_ops(ins, outs, (send_sems, recv_sems, None), n_a2a, local_copies=False)
        start()
        token[...] = jnp.zeros_like(token)

    outs = pl.pallas_call(
        body, name=name,
        out_shape=[pltpu.HBM(v.shape, v.dtype) for v in arrays] + [pltpu.HBM(sh.shape, sh.dtype) for sh in shapes]
        + [pltpu.SemaphoreType.DMA((n_arr * (N_DEV - 1),)), pltpu.SemaphoreType.DMA((n_arr * (N_DEV - 1),)),
           jax.ShapeDtypeStruct((8, LANES), F32)],
        in_specs=[_HBM] * (2 * n_arr) + [_ANY],
        out_specs=[_HBM] * (2 * n_arr) + [_SEM, _SEM, pl.BlockSpec(memory_space=pltpu.VMEM)],
        input_output_aliases={i: i for i in range(2 * n_arr)},
        compiler_params=pltpu.CompilerParams(has_side_effects=_DATAFLOW),
    )(*arrays, *lands, after)
    return dict(ins=outs[:n_arr], lands=outs[n_arr:2 * n_arr], sems=outs[2 * n_arr:2 * n_arr + 2], n_a2a=n_a2a,
                token=outs[2 * n_arr + 2])


def exchange_wait(handle, after, name):
    ins, lands, sems, n_a2a = handle["ins"], handle["lands"], handle["sems"], handle["n_a2a"]
    n_arr = len(ins)

    def body(*refs):
        in_refs, land_refs = refs[:n_arr], refs[n_arr:2 * n_arr]
        send_sems, recv_sems = refs[2 * n_arr:2 * n_arr + 2]
        _, wait = _exchange_ops(in_refs, land_refs, (send_sems, recv_sems, None), n_a2a, local_copies=False)
        wait()

    outs = pl.pallas_call(
        body, name=name,
        out_shape=[pltpu.HBM(v.shape, v.dtype) for v in list(ins) + list(lands)],
        in_specs=[_HBM] * (2 * n_arr) + [_SEM, _SEM, _ANY], out_specs=[_HBM] * (2 * n_arr),
        input_output_aliases={i: i for i in range(2 * n_arr)},
        compiler_params=pltpu.CompilerParams(has_side_effects=_DATAFLOW),
    )(*ins, *lands, *sems, after)
    return outs[n_arr:]


def _exchange_out_shape(a2a, ag):
    out_shape = [jax.ShapeDtypeStruct(v.shape, v.dtype) for v in a2a]
    return out_shape + [jax.ShapeDtypeStruct((N_DEV,) + v.shape, v.dtype) for v in ag]


def _exchange_sems(n_arr):
    return [pltpu.SemaphoreType.DMA((n_arr * (N_DEV - 1),)), pltpu.SemaphoreType.DMA((n_arr * (N_DEV - 1),)),
            pltpu.SemaphoreType.DMA((n_arr,))]


def _exchange_ops(ins, outs, sems, n_a2a, local_copies=True):
    send_sems, recv_sems, local_sems = sems
    n_arr = len(ins)
    x, y, c = lax.axis_index("x"), lax.axis_index("y"), lax.axis_index("c")
    me = 4 * x + 2 * y + c

    def peer_of(k):
        px = 1 - x if k & 4 else x
        py = 1 - y if k & 2 else y
        pc = 1 - c if k & 1 else c
        return (px, py, pc), 4 * px + 2 * py + pc

    def copy(a, k, landing):
        peer, pid = peer_of(k)
        src = ins[a].at[pid] if a < n_a2a else ins[a]
        return pltpu.make_async_remote_copy(
            src_ref=src, dst_ref=outs[a].at[pid if landing else me], send_sem=send_sems.at[a * (N_DEV - 1) + k - 1],
            recv_sem=recv_sems.at[a * (N_DEV - 1) + k - 1], device_id=peer, device_id_type=pl.DeviceIdType.MESH)

    def local(a):
        src = ins[a].at[me] if a < n_a2a else ins[a]
        return pltpu.make_async_copy(src, outs[a].at[me], local_sems.at[a])

    order = (6, 7, 4, 5, 2, 3, 1)

    def start():
        if local_copies:
            for a in range(n_arr):
                local(a).start()
        for a in range(n_arr):
            for k in order:
                copy(a, k, landing=False).start()

    def wait():
        for a in range(n_arr):
            for k in order:
                copy(a, k, landing=True).wait_recv()
        for a in range(n_arr):
            for k in order:
                copy(a, k, landing=False).wait_send()
        if local_copies:
            for a in range(n_arr):
                local(a).wait()

    return start, wait


def _rows_tile(n, want):
    t = min(want, n)
    assert n % t == 0, (n, t)
    return t


def rmsnorm_fwd(x, g, name):
    s, d = x.shape
    tm = _rows_tile(s, 512)

    def body(x_ref, g_ref, h_ref):
        xv = x_ref[...]
        r = lax.rsqrt(jnp.mean(xv * xv, axis=-1, keepdims=True) + EPS)
        h_ref[...] = ((xv * r) * g_ref[...]).astype(h_ref.dtype)

    return pl.pallas_call(
        body, name=name, grid=(s // tm,),
        in_specs=[pl.BlockSpec((tm, d), lambda i: (i, 0)), pl.BlockSpec((1, d), lambda i: (0, 0))],
        out_specs=pl.BlockSpec((tm, d), lambda i: (i, 0)),
        out_shape=jax.ShapeDtypeStruct((s, d), BF16),
        compiler_params=_params("parallel"),
    )(x, g.reshape(1, d))


def rmsnorm_bwd(x, dres, pairs, name):
    s, d = x.shape
    tm = _rows_tile(s, 256)
    n = len(pairs)

    def body(*refs):
        x_ref, dres_ref = refs[0], refs[1]
        g_refs, dh_refs = refs[2:2 + n], refs[2 + n:2 + 2 * n]
        dx_ref, dxb_ref = refs[2 + 2 * n], refs[3 + 2 * n]
        dg_refs = refs[4 + 2 * n:]
        i = pl.program_id(0)
        xv = x_ref[...]
        r = lax.rsqrt(jnp.mean(xv * xv, axis=-1, keepdims=True) + EPS)
        xhat = xv * r
        acc = dres_ref[...]
        for k in range(n):
            dh = dh_refs[k][...].astype(F32)
            part = jnp.sum(dh * xhat, axis=0, keepdims=True)

            @pl.when(i == 0)
            def _(k=k, part=part):
                dg_refs[k][...] = part

            @pl.when(i > 0)
            def _(k=k, part=part):
                dg_refs[k][...] += part

            dxh = dh * g_refs[k][...]
            acc = acc + r * (dxh - xhat * jnp.mean(dxh * xhat, axis=-1, keepdims=True))
        dx_ref[...] = acc
        dxb_ref[...] = acc.astype(BF16)

    row = pl.BlockSpec((tm, d), lambda i: (i, 0))
    vec = pl.BlockSpec((1, d), lambda i: (0, 0))
    outs = pl.pallas_call(
        body, name=name, grid=(s // tm,),
        in_specs=[row, row] + [vec] * n + [row] * n,
        out_specs=[row, row] + [vec] * n,
        out_shape=[jax.ShapeDtypeStruct((s, d), F32), jax.ShapeDtypeStruct((s, d), BF16)]
        + [jax.ShapeDtypeStruct((1, d), F32)] * n,
        compiler_params=_params("arbitrary"),
    )(x, dres, *[g.reshape(1, d) for g, _ in pairs], *[dh for _, dh in pairs])
    return outs[0], outs[1], outs[2:]


def loss_head(x, tgt, g, name):
    s, d = x.shape
    tm = _rows_tile(s, 256)

    def body(x_ref, t_ref, g_ref, dx_ref, dxb_ref, dg_ref, loss_ref):
        i = pl.program_id(0)
        xv = x_ref[...]
        gv = g_ref[...]
        r = lax.rsqrt(jnp.mean(xv * xv, axis=-1, keepdims=True) + EPS)
        xhat = xv * r
        diff = xhat * gv - t_ref[...]
        part_loss = 0.5 * jnp.sum(jnp.mean(diff * diff, axis=-1, keepdims=True), axis=0, keepdims=True)
        dy = diff * (1.0 / d)
        part_dg = jnp.sum(dy * xhat, axis=0, keepdims=True)

        @pl.when(i == 0)
        def _():
            dg_ref[...] = part_dg
            loss_ref[...] = jnp.broadcast_to(part_loss, loss_ref.shape)

        @pl.when(i > 0)
        def _():
            dg_ref[...] += part_dg
            loss_ref[...] += jnp.broadcast_to(part_loss, loss_ref.shape)

        dxh = dy * gv
        dx = r * (dxh - xhat * jnp.mean(dxh * xhat, axis=-1, keepdims=True))
        dx_ref[...] = dx
        dxb_ref[...] = dx.astype(BF16)

    row = pl.BlockSpec((tm, d), lambda i: (i, 0))
    vec = pl.BlockSpec((1, d), lambda i: (0, 0))
    return pl.pallas_call(
        body, name=name, grid=(s // tm,),
        in_specs=[row, row, vec], out_specs=[row, row, vec, pl.BlockSpec((1, LANES), lambda i: (0, 0))],
        out_shape=[jax.ShapeDtypeStruct((s, d), F32), jax.ShapeDtypeStruct((s, d), BF16),
                   jax.ShapeDtypeStruct((1, d), F32), jax.ShapeDtypeStruct((1, LANES), F32)],
        compiler_params=_params("arbitrary"),
    )(x, tgt, g.reshape(1, d))


def matmul(pairs, mode, out_dtype, name, res=None, tm=512, tn=512, roles=(None, None), count=1):
    a0, b0 = pairs[0]
    m = a0.shape[-1] if mode == "tn" else a0.shape[-2]
    n = b0.shape[-2] if mode == "nt" else b0.shape[-1]
    tm, tn = _rows_tile(m, tm), _rows_tile(n, tn)
    dims = {"nn": _NN, "nt": _NT, "tn": _TN}[mode]
    npairs = len(pairs)
    batched, reducing = "batch" in roles, "reduce" in roles
    assert not (batched and reducing) and (res is None or not batched)
    n_in = 2 * npairs + (res is not None)

    def body(*refs):
        o_ref = refs[n_in]
        acc = None
        for p in range(npairs):
            a_ref, b_ref = refs[2 * p], refs[2 * p + 1]
            for kk in range(count if reducing else 1):
                a = a_ref[kk] if roles[0] == "reduce" else a_ref[...]
                b = b_ref[kk] if roles[1] == "reduce" else b_ref[...]
                d = _dot(a.astype(BF16), b.astype(BF16), dims)
                acc = d if acc is None else acc + d
        if res is not None:
            acc = acc + refs[2 * npairs][...]
        o_ref[...] = acc.astype(o_ref.dtype)

    def spec(arr, role, block, index):
        if arr.ndim == 2:
            return pl.BlockSpec(block, lambda bb, i, j: index(i, j))
        assert role in ("batch", "reduce") and arr.shape[0] == count, (name, role, arr.shape)
        if role == "batch":
            return pl.BlockSpec((None,) + block, lambda bb, i, j: (bb,) + index(i, j))
        return pl.BlockSpec((count,) + block, lambda bb, i, j: (0,) + index(i, j))

    in_specs, args = [], []
    for a, b in pairs:
        if mode == "tn":
            in_specs.append(spec(a, roles[0], (a.shape[-2], tm), lambda i, j: (0, i)))
        else:
            in_specs.append(spec(a, roles[0], (tm, a.shape[-1]), lambda i, j: (i, 0)))
        if mode == "nt":
            in_specs.append(spec(b, roles[1], (tn, b.shape[-1]), lambda i, j: (j, 0)))
        else:
            in_specs.append(spec(b, roles[1], (b.shape[-2], tn), lambda i, j: (0, j)))
        args += [a, b]
    if res is not None:
        in_specs.append(pl.BlockSpec((tm, tn), lambda bb, i, j: (i, j)))
        args.append(res)
    if batched:
        out_spec = pl.BlockSpec((None, tm, tn), lambda bb, i, j: (bb, i, j))
        out_shape = jax.ShapeDtypeStruct((count, m, n), out_dtype)
    else:
        out_spec = pl.BlockSpec((tm, tn), lambda bb, i, j: (i, j))
        out_shape = jax.ShapeDtypeStruct((m, n), out_dtype)
    return pl.pallas_call(
        body, name=name, grid=(count if batched else 1, m // tm, n // tn),
        in_specs=in_specs, out_specs=out_spec, out_shape=out_shape,
        compiler_params=_params("parallel", "parallel", "parallel"),
    )(*args)


def _sigmoid(v):
    return 1.0 / (1.0 + jnp.exp(-v))


def ffn_up(h, wgu, name):
    s, d = h.shape
    _, n_sh, _, fs = wgu.shape
    tm = _rows_tile(s, 512)

    def body(h_ref, w_ref, gu_ref, act_ref):
        hv = h_ref[...]
        gate = _dot(hv, w_ref[0])
        up = _dot(hv, w_ref[1])
        gu_ref[0] = gate.astype(BF16)
        gu_ref[1] = up.astype(BF16)
        act_ref[...] = ((gate * _sigmoid(gate)) * up).astype(BF16)

    return pl.pallas_call(
        body, name=name, grid=(n_sh, s // tm),
        in_specs=[pl.BlockSpec((tm, d), lambda j, i: (i, 0)), pl.BlockSpec((2, None, d, fs), lambda j, i: (0, j, 0, 0))],
        out_specs=[pl.BlockSpec((2, None, tm, fs), lambda j, i: (0, j, i, 0)),
                   pl.BlockSpec((None, tm, fs), lambda j, i: (j, i, 0))],
        out_shape=[jax.ShapeDtypeStruct((2, n_sh, s, fs), BF16), jax.ShapeDtypeStruct((n_sh, s, fs), BF16)],
        compiler_params=_params("parallel", "parallel"),
    )(h, wgu)


def ffn_bwd_act(dxo, wdown, gu, name):
    s, d = dxo.shape
    n_sh, fs, _ = wdown.shape
    tm = _rows_tile(s, 512)

    def body(dx_ref, w_ref, gu_ref, dgu_ref):
        dact = _dot(dx_ref[...], w_ref[...], _NT)
        gate = gu_ref[0].astype(F32)
        sig = _sigmoid(gate)
        dgu_ref[0] = ((dact * gu_ref[1].astype(F32)) * (sig * (1.0 + gate * (1.0 - sig)))).astype(BF16)
        dgu_ref[1] = (dact * (gate * sig)).astype(BF16)

    tile = pl.BlockSpec((2, None, tm, fs), lambda j, i: (0, j, i, 0))
    return pl.pallas_call(
        body, name=name, grid=(n_sh, s // tm),
        in_specs=[pl.BlockSpec((tm, d), lambda j, i: (i, 0)), pl.BlockSpec((None, fs, d), lambda j, i: (j, 0, 0)), tile],
        out_specs=tile,
        out_shape=jax.ShapeDtypeStruct((2, n_sh, s, fs), BF16),
        compiler_params=_params("parallel", "parallel"),
    )(dxo, wdown, gu)


def _pool_select(col, by_window):
    out = by_window[3]
    for g in (2, 1, 0):
        out = jnp.where(col < (g + 1) * POOL_GROUP, by_window[g], out)
    return out


def _pool_counts(pos):
    return [jnp.minimum(pos + 1, w).astype(F32) for w in POOL_WINDOWS]


def pool_fwd(proj, wbd, scale, name):
    s = proj.shape[0]
    tm = _rows_tile(s, 256)
    nt = s // tm
    width = SB_WIDTH
    per = tm // POOL_HALO

    def body(u_ref, halo_ref, w_ref, sc_ref, pooled_ref, grouped_ref):
        i = pl.program_id(0)
        u = u_ref[...]
        halo = jnp.where(i > 0, halo_ref[...], 0.0)
        ext = jnp.concatenate([halo, u], axis=0)
        sums = []
        acc = ext
        for k in (1, 2, 4, 8):
            acc = acc + pltpu.roll(acc, k, 0)
            sums.append(acc[POOL_HALO:])
        pos = i * tm + lax.broadcasted_iota(jnp.int32, (tm, width), 0)
        col = lax.broadcasted_iota(jnp.int32, (tm, width), 1)
        means = [sm / cnt for sm, cnt in zip(sums, _pool_counts(pos))]
        pooled = (_pool_select(col, means) - u).astype(BF16)
        pooled_ref[...] = pooled
        grouped_ref[...] = (_dot(pooled, w_ref[...]) * sc_ref[...]).astype(BF16)

    row = pl.BlockSpec((tm, width), lambda i: (i, 0))
    return pl.pallas_call(
        body, name=name, grid=(nt,),
        in_specs=[row, pl.BlockSpec((POOL_HALO, width), lambda i: (jnp.maximum(i * per - 1, 0), 0)),
                  pl.BlockSpec((width, width), lambda i: (0, 0)), pl.BlockSpec((1, width), lambda i: (0, 0))],
        out_specs=[row, row],
        out_shape=[jax.ShapeDtypeStruct((s, width), BF16), jax.ShapeDtypeStruct((s, width + MEM_WIDTH), BF16)],
        compiler_params=_params("parallel"),
    )(proj, proj, wbd, scale.reshape(1, width))


def pool_bwd(dcat, pooled, wbd, scale, name):
    s = dcat.shape[0]
    tm = _rows_tile(s, 256)
    nt = s // tm
    width = SB_WIDTH
    per = tm // POOL_HALO
    n_ext = tm + POOL_HALO

    def body(dg_ref, halo_ref, pooled_ref, w_ref, sc_ref, du_ref, dw_ref, dsc_ref):
        i = pl.program_id(0)
        w = w_ref[...]
        sc = sc_ref[...]
        dg = dg_ref[...]
        pooled = pooled_ref[...]
        pg = _dot(pooled, w)
        dpg = (dg * sc).astype(BF16)
        part_sc = jnp.sum(dg * pg, axis=0, keepdims=True)
        part_w = _dot(pooled, dpg, _TN)

        @pl.when(i == 0)
        def _():
            dsc_ref[...] = part_sc
            dw_ref[...] = part_w

        @pl.when(i > 0)
        def _():
            dsc_ref[...] += part_sc
            dw_ref[...] += part_w

        dpooled = _dot(dpg, w, _NT)
        halo_dpg = (jnp.where(i < nt - 1, halo_ref[...], 0.0) * sc).astype(BF16)
        ext = jnp.concatenate([dpooled, _dot(halo_dpg, w, _NT)], axis=0)
        pos = i * tm + lax.broadcasted_iota(jnp.int32, (n_ext, width), 0)
        col = lax.broadcasted_iota(jnp.int32, (tm, width), 1)
        outs = []
        for k_idx, cnt in enumerate(_pool_counts(pos)):
            acc = ext / cnt
            for k in (1, 2, 4, 8)[:k_idx + 1]:
                acc = acc + pltpu.roll(acc, n_ext - k, 0)
            outs.append(acc[:tm])
        du_ref[...] = (_pool_select(col, outs) - dpooled).astype(BF16)

    row = pl.BlockSpec((tm, width), lambda i: (i, 0))
    return pl.pallas_call(
        body, name=name, grid=(nt,),
        in_specs=[row, pl.BlockSpec((POOL_HALO, width), lambda i: (jnp.minimum((i + 1) * per, s // POOL_HALO - 1), 0)),
                  row, pl.BlockSpec((width, width), lambda i: (0, 0)), pl.BlockSpec((1, width), lambda i: (0, 0))],
        out_specs=[row, pl.BlockSpec((width, width), lambda i: (0, 0)), pl.BlockSpec((1, width), lambda i: (0, 0))],
        out_shape=[jax.ShapeDtypeStruct((s, width + MEM_WIDTH), BF16), jax.ShapeDtypeStruct((width, width), F32),
                   jax.ShapeDtypeStruct((1, width), F32)],
        compiler_params=_params("arbitrary"),
    )(dcat, dcat, pooled, wbd, scale.reshape(1, width))


def _head_masks(shape):
    lane = lax.broadcasted_iota(jnp.int32, shape, 1)
    return lane < HEAD_DIM, lane >= HEAD_DIM


def _mem_probs(qh, kp):
    logits = _dot(qh, kp, _NT)
    logits = logits - jnp.max(logits, axis=-1, keepdims=True)
    p = jnp.exp(logits)
    return p / jnp.sum(p, axis=-1, keepdims=True)


def mem_attn_fwd(proj, mkv, cat, name):
    s = proj.shape[0]
    m = mkv.shape[0]
    tm = _rows_tile(s, 256)
    q_blk = SB_WIDTH // MEM_WIDTH

    def body(q_ref, kv_ref, _, o_ref):
        masks = _head_masks((tm, LANES))
        for pr in range(MEM_WIDTH // LANES):
            q = q_ref[:, pr * LANES:(pr + 1) * LANES] * QK_SCALE
            kp = kv_ref[:, pr * LANES:(pr + 1) * LANES]
            vp = kv_ref[:, MEM_WIDTH + pr * LANES:MEM_WIDTH + (pr + 1) * LANES]
            outs = []
            for hm in masks:
                p = _mem_probs(jnp.where(hm, q, 0.0).astype(BF16), kp)
                outs.append(_dot(p.astype(BF16), vp))
            o_ref[:, pr * LANES:(pr + 1) * LANES] = jnp.where(masks[0], outs[0], outs[1]).astype(BF16)

    return pl.pallas_call(
        body, name=name, grid=(s // tm,),
        in_specs=[pl.BlockSpec((tm, MEM_WIDTH), lambda i: (i, q_blk)), pl.BlockSpec((m, 2 * MEM_WIDTH), lambda i: (0, 0)),
                  _ANY],
        out_specs=pl.BlockSpec((tm, MEM_WIDTH), lambda i: (i, q_blk)),
        out_shape=jax.ShapeDtypeStruct(cat.shape, BF16),
        input_output_aliases={2: 0},
        compiler_params=_params("parallel"),
    )(proj, mkv, cat)


def mem_attn_bwd(proj, mkv, dcat, dproj, name):
    s = proj.shape[0]
    m = mkv.shape[0]
    tm = _rows_tile(s, 256)
    q_blk = SB_WIDTH // MEM_WIDTH

    def body(q_ref, kv_ref, do_ref, _, dq_ref, dkv_ref):
        i = pl.program_id(0)

        @pl.when(i == 0)
        def _():
            dkv_ref[...] = jnp.zeros_like(dkv_ref)

        masks = _head_masks((tm, LANES))
        for pr in range(MEM_WIDTH // LANES):
            q = q_ref[:, pr * LANES:(pr + 1) * LANES] * QK_SCALE
            do = do_ref[:, pr * LANES:(pr + 1) * LANES]
            kp = kv_ref[:, pr * LANES:(pr + 1) * LANES]
            vp = kv_ref[:, MEM_WIDTH + pr * LANES:MEM_WIDTH + (pr + 1) * LANES]
            dqs = []
            dk = jnp.zeros((m, LANES), F32)
            dv = jnp.zeros((m, LANES), F32)
            for hm in masks:
                qh = jnp.where(hm, q, 0.0).astype(BF16)
                doh = jnp.where(hm, do, 0.0).astype(BF16)
                p = _mem_probs(qh, kp)
                dp = _dot(doh, vp, _NT)
                ds = (p * (dp - jnp.sum(dp * p, axis=-1, keepdims=True))).astype(BF16)
                dqs.append(_dot(ds, kp))
                dk = dk + _dot(ds, qh, _TN)
                dv = dv + _dot(p.astype(BF16), doh, _TN)
            dq_ref[:, pr * LANES:(pr + 1) * LANES] = (jnp.where(masks[0], dqs[0], dqs[1]) * QK_SCALE).astype(BF16)
            dkv_ref[:, pr * LANES:(pr + 1) * LANES] += dk
            dkv_ref[:, MEM_WIDTH + pr * LANES:MEM_WIDTH + (pr + 1) * LANES] += dv

    return pl.pallas_call(
        body, name=name, grid=(s // tm,),
        in_specs=[pl.BlockSpec((tm, MEM_WIDTH), lambda i: (i, q_blk)), pl.BlockSpec((m, 2 * MEM_WIDTH), lambda i: (0, 0)),
                  pl.BlockSpec((tm, MEM_WIDTH), lambda i: (i, q_blk)), _ANY],
        out_specs=[pl.BlockSpec((tm, MEM_WIDTH), lambda i: (i, q_blk)), pl.BlockSpec((m, 2 * MEM_WIDTH), lambda i: (0, 0))],
        out_shape=[jax.ShapeDtypeStruct(dproj.shape, BF16), jax.ShapeDtypeStruct((m, 2 * MEM_WIDTH), F32)],
        input_output_aliases={3: 0},
        compiler_params=_params("arbitrary"),
    )(proj, mkv, dcat, dproj)


SB_ROWS = 256
SB_KEYS = 512
SB_SUB = LANES


def _sb_tri(later):
    r = lax.broadcasted_iota(jnp.int32, (2 * SB_SUB, 2 * SB_SUB), 0) % SB_SUB
    c = lax.broadcasted_iota(jnp.int32, (2 * SB_SUB, 2 * SB_SUB), 1)
    tri = (r > c) if later else (r < c)
    return (tri | (c >= SB_SUB)).astype(BF16)


def _sb_sums(v, tri, n_sub):
    out = []
    for m in range(n_sub):
        vm = v[:, m * SB_SUB:(m + 1) * SB_SUB]
        hi = vm.astype(BF16)
        lo = (vm - hi.astype(F32)).astype(BF16)
        r = _dot(jnp.concatenate([hi, lo], axis=1), tri)
        out.append((r[:, :SB_SUB], r[:, SB_SUB:]))
    return out


def _sb_logs(nqh, k_blk, valid):
    nz = _dot(nqh, k_blk, _NT)
    log_not = jnp.minimum(nz, 0.0) - jnp.log(1.0 + jnp.exp(-jnp.abs(nz)))
    log_beta = log_not - nz
    if valid is not None:
        log_not = jnp.where(valid, log_not, 0.0)
    return log_not, log_beta


def _sb_weights(log_not, log_beta, tri_later, carry_in, valid, n_sub):
    sums = _sb_sums(log_not, tri_later, n_sub)
    run = carry_in
    later = [None] * n_sub
    for m in reversed(range(n_sub)):
        later[m] = sums[m][0] + run
        run = run + sums[m][1]
    w = jnp.exp(log_beta + jnp.concatenate(later, axis=1))
    if valid is not None:
        w = jnp.where(valid, w, 0.0)
    return w, run


def _sb_valid(i, j, tq, tk):
    qpos = i * tq + lax.broadcasted_iota(jnp.int32, (tq, tk), 0)
    kpos = j * tk + lax.broadcasted_iota(jnp.int32, (tq, tk), 1)
    return qpos > kpos


def sb_fwd(projb, kv, name, gather=()):
    s = projb.shape[0]
    tq, tk = _rows_tile(s, SB_ROWS), _rows_tile(s, SB_KEYS)
    n_sub = tk // SB_SUB
    n_pairs = SB_WIDTH // LANES
    assert s // tk <= LANES and tk % tq == 0

    n_g = len(gather)
    n_i = s // tq

    def body(*refs):
        q_ref, k_ref, v_ref = refs[:3]
        o_ref, c_ref = refs[3 + n_g:5 + n_g]
        i = pl.program_id(1)
        if n_g:
            start, wait = _exchange_ops(refs[3:3 + n_g], refs[5 + n_g:5 + 2 * n_g], refs[5 + 2 * n_g:], 0)
            pl.when((pl.program_id(0) == 0) & (i == 0))(start)
        j_diag = (i * tq) // tk
        masks = _head_masks((tq, LANES))
        nq = q_ref[...] * (-QK_SCALE)
        tri = _sb_tri(later=True)
        lane = lax.broadcasted_iota(jnp.int32, (tq, LANES), 1)
        nqh = [jnp.where(hm, nq, 0.0).astype(BF16) for hm in masks]

        def block(j, carry, valid):
            off = pl.multiple_of(j * tk, tk)
            k_blk, v_blk = k_ref[pl.ds(off, tk), :], v_ref[pl.ds(off, tk), :]
            out = []
            for hh in range(2):
                c, acc, cmat = carry[3 * hh:3 * hh + 3]
                log_not, log_beta = _sb_logs(nqh[hh], k_blk, valid)
                w, c_next = _sb_weights(log_not, log_beta, tri, c, valid, n_sub)
                out += [c_next, acc + _dot(w.astype(BF16), v_blk), jnp.where(lane == j, c, cmat)]
            return tuple(out)

        zero = jnp.zeros((tq, LANES), F32)
        carry = block(j_diag, (zero,) * 6, _sb_valid(i, j_diag, tq, tk))
        carry = lax.fori_loop(0, j_diag, lambda jj, cr: block(j_diag - 1 - jj, cr, None), carry)
        c_ref[0] = carry[2]
        c_ref[1] = carry[5]
        o_ref[...] = jnp.where(masks[0], carry[1], carry[4]).astype(BF16)
        if n_g:
            pl.when((pl.program_id(0) == n_pairs - 1) & (i == n_i - 1))(wait)

    outs = pl.pallas_call(
        body, name=name, grid=(n_pairs, n_i),
        in_specs=[pl.BlockSpec((tq, LANES), lambda p, i: (i, p)),
                  pl.BlockSpec((s, LANES), lambda p, i: (0, p)),
                  pl.BlockSpec((s, LANES), lambda p, i: (0, n_pairs + p))] + [_ANY] * n_g,
        out_specs=[pl.BlockSpec((tq, LANES), lambda p, i: (i, p)),
                   pl.BlockSpec((2, tq, LANES), lambda p, i: (p, i, 0))] + [_ANY] * n_g,
        out_shape=[jax.ShapeDtypeStruct((s, SB_WIDTH + MEM_WIDTH), BF16),
                   jax.ShapeDtypeStruct((N_SB_HEADS, s, LANES), F32)] + _exchange_out_shape([], gather),
        scratch_shapes=_exchange_sems(n_g) if n_g else [],
        compiler_params=_params("arbitrary", "arbitrary"),
    )(projb, kv, kv, *gather)
    return outs[0], outs[1], outs[2:]


def sb_bwd(projb, kv, dcat, csave, name, scatter=()):
    s = projb.shape[0]
    tq, tk = _rows_tile(s, SB_ROWS), _rows_tile(s, SB_KEYS)
    n_sub = tk // SB_SUB
    n_pairs = SB_WIDTH // LANES
    n_x = len(scatter)
    n_i = s // tq

    def body(*refs):
        q_ref, k_ref, v_ref, do_ref, c_ref = refs[:5]
        dq_ref, dkb_ref, dvb_ref = refs[5 + n_x:8 + n_x]
        dk_ref, dv_ref = refs[8 + 2 * n_x:10 + 2 * n_x]
        i = pl.program_id(1)
        if n_x:
            start, wait = _exchange_ops(refs[5:5 + n_x], refs[8 + n_x:8 + 2 * n_x], refs[10 + 2 * n_x:], n_x)
            pl.when((pl.program_id(0) == 0) & (i == 0))(start)
        j_diag = (i * tq) // tk

        @pl.when(i == 0)
        def _():
            dk_ref[...] = jnp.zeros_like(dk_ref)
            dv_ref[...] = jnp.zeros_like(dv_ref)

        masks = _head_masks((tq, LANES))
        qs = q_ref[...] * QK_SCALE
        do = do_ref[...]
        tri_later = _sb_tri(later=True)
        tri_before = _sb_tri(later=False)
        lane = lax.broadcasted_iota(jnp.int32, (tq, LANES), 1)
        qh = [jnp.where(hm, qs, 0.0).astype(BF16) for hm in masks]
        nqh = [jnp.where(hm, -qs, 0.0).astype(BF16) for hm in masks]
        doh = [jnp.where(hm, do, 0.0).astype(BF16) for hm in masks]

        def block(j, carry, valid):
            off = pl.multiple_of(j * tk, tk)
            k_blk, v_blk = k_ref[pl.ds(off, tk), :], v_ref[pl.ds(off, tk), :]
            out, dk, dv = [], None, None
            for hh in range(2):
                e_before, dq = carry[2 * hh:2 * hh + 2]
                log_not, log_beta = _sb_logs(nqh[hh], k_blk, valid)
                c = jnp.sum(jnp.where(lane == j, c_ref[hh], 0.0), axis=1, keepdims=True)
                w, _ = _sb_weights(log_not, log_beta, tri_later, jnp.broadcast_to(c, (tq, LANES)), valid, n_sub)
                dv_h = _dot(w.astype(BF16), doh[hh], _TN)
                e = w * _dot(doh[hh], v_blk, _NT)
                sums = _sb_sums(e, tri_before, n_sub)
                run = e_before
                e_pre = []
                for m in range(n_sub):
                    e_pre.append(sums[m][0] + run)
                    run = run + sums[m][1]
                dz = e - jnp.exp(log_beta) * (e + jnp.concatenate(e_pre, axis=1))
                if valid is not None:
                    dz = jnp.where(valid, dz, 0.0)
                dz = dz.astype(BF16)
                dk_h = _dot(dz, qh[hh], _TN)
                dk, dv = (dk_h, dv_h) if hh == 0 else (dk + dk_h, dv + dv_h)
                out += [run, dq + _dot(dz, k_blk)]
            dk_ref[pl.ds(off, tk), :] += dk
            dv_ref[pl.ds(off, tk), :] += dv
            return tuple(out)

        zero = jnp.zeros((tq, LANES), F32)
        carry = lax.fori_loop(0, j_diag, lambda j, cr: block(j, cr, None), (zero,) * 4)
        carry = block(j_diag, carry, _sb_valid(i, j_diag, tq, tk))
        dq_ref[...] = (jnp.where(masks[0], carry[1], carry[3]) * QK_SCALE).astype(BF16)

        @pl.when(i == n_i - 1)
        def _():
            dkb_ref[...] = dk_ref[...].astype(BF16)
            dvb_ref[...] = dv_ref[...].astype(BF16)

        if n_x:
            pl.when((pl.program_id(0) == n_pairs - 1) & (i == n_i - 1))(wait)

    tile = pl.BlockSpec((tq, LANES), lambda p, i: (i, p))
    full = pl.BlockSpec((s, LANES), lambda p, i: (0, p))
    outs = pl.pallas_call(
        body, name=name, grid=(n_pairs, n_i),
        in_specs=[tile, full, pl.BlockSpec((s, LANES), lambda p, i: (0, n_pairs + p)), tile,
                  pl.BlockSpec((2, tq, LANES), lambda p, i: (p, i, 0))] + [_ANY] * n_x,
        out_specs=[tile, full, full] + [_ANY] * n_x,
        out_shape=[jax.ShapeDtypeStruct((s, SB_WIDTH + MEM_WIDTH), BF16), jax.ShapeDtypeStruct((s, SB_WIDTH), BF16),
                   jax.ShapeDtypeStruct((s, SB_WIDTH), BF16)] + _exchange_out_shape(scatter, []),
        scratch_shapes=[pltpu.VMEM((s, LANES), F32), pltpu.VMEM((s, LANES), F32)]
        + (_exchange_sems(n_x) if n_x else []),
        compiler_params=_params("arbitrary", "arbitrary"),
    )(projb, kv, kv, dcat, csave, *scatter)
    return outs[0], outs[1], outs[2], outs[3:]


def reduce_adamw(parts, w, m, v, name, after=None):
    n, r, c = parts.shape
    tr = next((t for t in (128, 176) if r % t == 0), r)
    bias1 = 1.0 - ADAM_B1 ** ADAM_STEP
    bias2 = 1.0 - ADAM_B2 ** ADAM_STEP

    def body(p_ref, w_ref, m_ref, v_ref, *rest):
        g_ref, d_ref, nm_ref, nv_ref = rest[-4:]
        g = p_ref[0].astype(F32)
        for k in range(1, n):
            g = g + p_ref[k].astype(F32)
        new_m = ADAM_B1 * m_ref[...] + (1.0 - ADAM_B1) * g
        new_v = ADAM_B2 * v_ref[...] + (1.0 - ADAM_B2) * (g * g)
        m_hat = new_m / bias1
        v_hat = new_v / bias2
        g_ref[...] = g
        d_ref[...] = -ADAM_LR * (m_hat / (jnp.sqrt(v_hat) + ADAM_EPS) + ADAM_WD * w_ref[...])
        nm_ref[...] = new_m
        nv_ref[...] = new_v

    row = pl.BlockSpec((tr, c), lambda i: (i, 0))
    return pl.pallas_call(
        body, name=name, grid=(r // tr,),
        in_specs=[pl.BlockSpec((n, tr, c), lambda i: (0, i, 0)), row, row, row] + ([] if after is None else [_ANY]),
        out_specs=[row, row, row, row],
        out_shape=[jax.ShapeDtypeStruct((r, c), F32)] * 4,
        compiler_params=_params("parallel"),
    )(parts, w, m, v, *(() if after is None else (after,)))


EARLY_1 = ("a_w_in", "a_w_mem_kv", "a_w_out")
EARLY_2 = ("a_w_gu", "a_w_down")
EARLY_3 = ("w_kv", "b_w_q", "b_w_mem_kv")
EARLY = EARLY_1 + EARLY_2 + EARLY_3
LATE = ("b_w_out", "b_w_gu", "b_w_down")
BIG = EARLY + LATE
SMALL_SHARDED = ("a_norm_mix", "a_scale", "a_norm_ffn")
SMALL_REPL = ("mem_norm", "kv_norm", "b_norm_mix", "b_norm_ffn", "final_norm")
WEIGHTS = ("mem_norm", "a_norm_mix", "a_w_in", "a_w_group", "a_scale", "a_w_mem_kv", "a_w_out", "a_norm_ffn", "a_w_gu",
           "a_w_down", "kv_norm", "w_kv", "b_norm_mix", "b_w_q", "b_w_mem_kv", "b_w_out", "b_norm_ffn", "b_w_gu",
           "b_w_down", "final_norm")


def _pad_row(v):
    v = v.reshape(1, -1)
    return jnp.pad(v, ((0, 0), (0, PACK_COLS - v.shape[1])))


def kernel(x, mem, mem_norm, a_norm_mix, a_w_in, a_w_group, a_scale, a_w_mem_kv, a_w_out, a_norm_ffn, a_w_gu, a_w_down, kv_norm, w_kv, b_norm_mix, b_w_q, b_w_mem_kv, b_w_out, b_norm_ffn, b_w_gu, b_w_down, final_norm, loss_target, m_mem_norm, m_a_norm_mix, m_a_w_in, m_a_w_group, m_a_scale, m_a_w_mem_kv, m_a_w_out, m_a_norm_ffn, m_a_w_gu, m_a_w_down, m_kv_norm, m_w_kv, m_b_norm_mix, m_b_w_q, m_b_w_mem_kv, m_b_w_out, m_b_norm_ffn, m_b_w_gu, m_b_w_down, m_final_norm, v_mem_norm, v_a_norm_mix, v_a_w_in, v_a_w_group, v_a_scale, v_a_w_mem_kv, v_a_w_out, v_a_norm_ffn, v_a_w_gu, v_a_w_down, v_kv_norm, v_w_kv, v_b_norm_mix, v_b_w_q, v_b_w_mem_kv, v_b_w_out, v_b_norm_ffn, v_b_w_gu, v_b_w_down, v_final_norm):
    w = dict(mem_norm=mem_norm, a_norm_mix=a_norm_mix, a_w_in=a_w_in, a_w_group=a_w_group, a_scale=a_scale,
             a_w_mem_kv=a_w_mem_kv, a_w_out=a_w_out, a_norm_ffn=a_norm_ffn, a_w_gu=a_w_gu, a_w_down=a_w_down,
             kv_norm=kv_norm, w_kv=w_kv, b_norm_mix=b_norm_mix, b_w_q=b_w_q, b_w_mem_kv=b_w_mem_kv, b_w_out=b_w_out,
             b_norm_ffn=b_norm_ffn, b_w_gu=b_w_gu, b_w_down=b_w_down, final_norm=final_norm)
    mom = dict(mem_norm=m_mem_norm, a_norm_mix=m_a_norm_mix, a_w_in=m_a_w_in, a_w_group=m_a_w_group, a_scale=m_a_scale,
               a_w_mem_kv=m_a_w_mem_kv, a_w_out=m_a_w_out, a_norm_ffn=m_a_norm_ffn, a_w_gu=m_a_w_gu,
               a_w_down=m_a_w_down, kv_norm=m_kv_norm, w_kv=m_w_kv, b_norm_mix=m_b_norm_mix, b_w_q=m_b_w_q,
               b_w_mem_kv=m_b_w_mem_kv, b_w_out=m_b_w_out, b_norm_ffn=m_b_norm_ffn, b_w_gu=m_b_w_gu,
               b_w_down=m_b_w_down, final_norm=m_final_norm)
    var = dict(mem_norm=v_mem_norm, a_norm_mix=v_a_norm_mix, a_w_in=v_a_w_in, a_w_group=v_a_w_group, a_scale=v_a_scale,
               a_w_mem_kv=v_a_w_mem_kv, a_w_out=v_a_w_out, a_norm_ffn=v_a_norm_ffn, a_w_gu=v_a_w_gu,
               a_w_down=v_a_w_down, kv_norm=v_kv_norm, w_kv=v_w_kv, b_norm_mix=v_b_norm_mix, b_w_q=v_b_w_q,
               b_w_mem_kv=v_b_w_mem_kv, b_w_out=v_b_w_out, b_norm_ffn=v_b_norm_ffn, b_w_gu=v_b_w_gu,
               b_w_down=v_b_w_down, final_norm=v_final_norm)

    me = 4 * lax.axis_index("x") + 2 * lax.axis_index("y") + lax.axis_index("c")
    shard2d = {n: w[n].shape[-2:] for n in BIG}
    shard = {n: w[n].reshape(shard2d[n]).astype(BF16) for n in BIG}
    n_sh = N_DEV // 2
    s_len = x.shape[1]

    small_send = jnp.concatenate([_pad_row(w[n]) for n in SMALL_SHARDED]
                                 + [jnp.zeros((8 - len(SMALL_SHARDED), PACK_COLS), F32)], axis=0)
    def own_slot(land, own):
        return lax.dynamic_update_slice(land, own[None], (me, 0, 0))

    def gather_done(handle, names, after, name):
        got = exchange_wait(handle, after, name)
        gathered.update({n: own_slot(g, shard[n]) for n, g in zip(names, got)})
        return got

    gathered = {}
    gather_1 = exchange_start([], [shard[n] for n in EARLY_1] + [small_send], x, "gather_1_start")
    gather_2 = exchange_start([], [shard[n] for n in EARLY_2], gather_1["token"], "gather_2_start")
    gather_3 = exchange_start([], [shard[n] for n in EARLY_3], gather_2["token"], "gather_3_start")
    small_all = own_slot(gather_done(gather_1, EARLY_1, gather_3["token"], "gather_1_wait")[-1], small_send)
    a_norm_mix_f = small_all[:, 0, :a_norm_mix.shape[1]].reshape(-1)
    a_scale_f = small_all[:, 1, :a_scale.shape[1]].reshape(-1)
    a_norm_ffn_f = small_all[:, 2, :a_norm_ffn.shape[1]].reshape(-1)

    def rows_full(n):
        return gathered[n].reshape(-1, shard2d[n][1])

    def ffn_views(tag):
        gu8, down = gathered[tag + "_w_gu"], gathered[tag + "_w_down"]
        return gu8.reshape((2, n_sh) + gu8.shape[1:]), gu8, down.reshape(n_sh, -1, down.shape[2])

    wbd = jnp.zeros((SB_WIDTH, SB_WIDTH), BF16)
    for g in range(4):
        sl = slice(g * POOL_GROUP, (g + 1) * POOL_GROUP)
        wbd = wbd.at[sl, sl].set(a_w_group[0, g].astype(BF16))

    xs, mems, tgt = x[0], mem[0], loss_target[0]

    memn = rmsnorm_fwd(mems, mem_norm, "memn")
    h0 = rmsnorm_fwd(xs, a_norm_mix_f, "h0")
    proj = matmul([(h0, rows_full("a_w_in"))], "nn", F32, "proj_a")
    mkv_a = matmul([(memn, rows_full("a_w_mem_kv"))], "nn", BF16, "mkv_a")
    pooled, cat_a = pool_fwd(proj, wbd, a_scale_f, "pool_fwd")
    cat_a = mem_attn_fwd(proj, mkv_a, cat_a, "mem_fwd_a")
    x1 = matmul([(cat_a, rows_full("a_w_out"))], "nn", F32, "out_a", res=xs)
    h1 = rmsnorm_fwd(x1, a_norm_ffn_f, "h1")
    gather_done(gather_2, EARLY_2, h1, "gather_2_wait")
    wgu_a, wgu8_a, wdown_a = ffn_views("a")
    gu_a, act_a = ffn_up(h1, wgu_a, "ffn_up_a")
    x2 = matmul([(act_a, wdown_a)], "nn", F32, "down_a", res=x1, roles=("reduce", "reduce"), count=n_sh)

    hk = rmsnorm_fwd(x2, kv_norm, "hk")
    gather_done(gather_3, EARLY_3, hk, "gather_3_wait")
    wkv = jnp.transpose(gathered["w_kv"], (1, 0, 2)).reshape(shard2d["w_kv"][0], -1)
    kv = matmul([(hk, wkv)], "nn", BF16, "kv")
    h3 = rmsnorm_fwd(x2, b_norm_mix[0], "h3")
    projb = matmul([(h3, rows_full("b_w_q"))], "nn", F32, "proj_b")
    mkv_b = matmul([(memn, rows_full("b_w_mem_kv"))], "nn", BF16, "mkv_b")
    cat_b, csave, got = sb_fwd(projb, kv, "sb_fwd", gather=[shard[n] for n in LATE])
    gathered.update(zip(LATE, got))
    cat_b = mem_attn_fwd(projb, mkv_b, cat_b, "mem_fwd_b")
    x3 = matmul([(cat_b, rows_full("b_w_out"))], "nn", F32, "out_b", res=x2)
    h4 = rmsnorm_fwd(x3, b_norm_ffn[0], "h4")
    wgu_b, wgu8_b, wdown_b = ffn_views("b")
    gu_b, act_b = ffn_up(h4, wgu_b, "ffn_up_b")
    x4 = matmul([(act_b, wdown_b)], "nn", F32, "down_b", res=x3, roles=("reduce", "reduce"), count=n_sh)

    grads = {}
    dx4, dx4_bf, grads["final_norm"], loss_part = loss_head(x4, tgt, final_norm, "loss_head")

    def blocks(n, g):
        return g.reshape((N_DEV,) + tuple(shard2d[n]))

    def ffn_weight_grads(tag, dx_out_bf, h, gu, act, wgu8, wdown):
        fs = wgu8.shape[2]
        dgu8 = ffn_bwd_act(dx_out_bf, wdown, gu, "ffn_bwd_act_" + tag).reshape(N_DEV, s_len, fs)
        d_wdown = matmul([(act, dx_out_bf)], "tn", BF16, "dw_down_" + tag, tm=fs, roles=("batch", None), count=n_sh)
        grads[tag + "_w_gu"] = matmul([(h, dgu8)], "tn", BF16, "dw_gu_" + tag, tn=fs, roles=(None, "batch"), count=N_DEV)
        grads[tag + "_w_down"] = blocks(tag + "_w_down", d_wdown)
        return dgu8

    def ffn_input_grads(tag, x_in, dx_out, dgu8, wgu8, g_ffn):
        dh = matmul([(dgu8, wgu8)], "nt", F32, "dh_ffn_" + tag, roles=("reduce", "reduce"), count=N_DEV)
        dx_in, dx_in_bf, dgs = rmsnorm_bwd(x_in, dx_out, [(g_ffn, dh)], "norm_bwd_ffn_" + tag)
        return dx_in, dx_in_bf, dgs[0]

    def grads_start(names, small, after, name):
        return exchange_start([grads[n] for n in names], small, after, name)

    def grads_done(handle, names, after, name):
        got = exchange_wait(handle, after, name)
        for n, g in zip(names, got):
            recv[n] = lax.dynamic_update_slice(g, lax.dynamic_slice(grads[n], (me, 0, 0), (1,) + g.shape[1:]), (me, 0, 0))
        return got

    recv = {}
    dgu8_b = ffn_weight_grads("b", dx4_bf, h4, gu_b, act_b, wgu8_b, wdown_b)
    dx3, dx3_bf, grads["b_norm_ffn"] = ffn_input_grads("b", x3, dx4, dgu8_b, wgu8_b, b_norm_ffn[0])

    dcat_b = matmul([(dx3_bf, rows_full("b_w_out"))], "nt", F32, "dcat_b")
    grads["b_w_out"] = blocks("b_w_out", matmul([(cat_b, dx3_bf)], "tn", BF16, "dw_out_b"))
    dprojb, dk, dv, recv_late = sb_bwd(projb, kv, dcat_b, csave, "sb_bwd", scatter=[grads[n] for n in LATE])
    dprojb, dmkv_b = mem_attn_bwd(projb, mkv_b, dcat_b, dprojb, "mem_bwd_b")
    dh3 = matmul([(dprojb, rows_full("b_w_q"))], "nt", F32, "dh3")
    grads["b_w_q"] = blocks("b_w_q", matmul([(h3, dprojb)], "tn", BF16, "dw_q_b"))
    dhk = matmul([(dk, wkv[:, :SB_WIDTH]), (dv, wkv[:, SB_WIDTH:])], "nt", F32, "dhk")
    d_wkv = jnp.concatenate([matmul([(hk, dk)], "tn", BF16, "dw_k", tn=SB_WIDTH),
                             matmul([(hk, dv)], "tn", BF16, "dw_v", tn=SB_WIDTH)], axis=1)
    grads["w_kv"] = jnp.transpose(d_wkv.reshape(d_wkv.shape[0], N_DEV, -1), (1, 0, 2))
    dx2, dx2_bf, dgs = rmsnorm_bwd(x2, dx3, [(b_norm_mix[0], dh3), (kv_norm, dhk)], "norm_bwd_x2")
    grads["b_norm_mix"], grads["kv_norm"] = dgs
    dmkv_b_bf = dmkv_b.astype(BF16)
    grads["b_w_mem_kv"] = blocks("b_w_mem_kv", matmul([(memn, dmkv_b_bf)], "tn", BF16, "dw_mkv_b"))

    grads_3 = grads_start(EARLY_3, [], grads["b_w_mem_kv"], "grads_3_start")
    dgu8_a = ffn_weight_grads("a", dx2_bf, h1, gu_a, act_a, wgu8_a, wdown_a)
    grads_2 = grads_start(EARLY_2, [], grads_3["token"], "grads_2_start")
    dx1, dx1_bf, grads["a_norm_ffn"] = ffn_input_grads(
        "a", x1, dx2, dgu8_a, wgu8_a, a_norm_ffn_f + grads_2["token"][0, 0])

    dcat_a = matmul([(dx1_bf, rows_full("a_w_out"))], "nt", F32, "dcat_a")
    grads["a_w_out"] = blocks("a_w_out", matmul([(cat_a, dx1_bf)], "tn", BF16, "dw_out_a"))
    dproj, d_wbd, grads["a_scale"] = pool_bwd(dcat_a, pooled, wbd, a_scale_f, "pool_bwd")
    grads["a_w_group"] = jnp.stack(
        [d_wbd[g * POOL_GROUP:(g + 1) * POOL_GROUP, g * POOL_GROUP:(g + 1) * POOL_GROUP] for g in range(4)])
    dproj, dmkv_a = mem_attn_bwd(proj, mkv_a, dcat_a, dproj, "mem_bwd_a")
    dh0 = matmul([(dproj, rows_full("a_w_in"))], "nt", F32, "dh0")
    grads["a_w_in"] = blocks("a_w_in", matmul([(h0, dproj)], "tn", BF16, "dw_in_a"))
    dx, _, dgs = rmsnorm_bwd(xs, dx1, [(a_norm_mix_f, dh0)], "norm_bwd_x")
    grads["a_norm_mix"] = dgs[0]
    dmkv_a_bf = dmkv_a.astype(BF16)
    grads["a_w_mem_kv"] = blocks("a_w_mem_kv", matmul([(memn, dmkv_a_bf)], "tn", BF16, "dw_mkv_a"))
    dmemn = matmul([(dmkv_a_bf, rows_full("a_w_mem_kv")), (dmkv_b_bf, rows_full("b_w_mem_kv"))], "nt", F32, "dmemn")
    _, _, dgs = rmsnorm_bwd(mems, jnp.zeros_like(mems), [(mem_norm, dmemn)], "norm_bwd_mem")
    grads["mem_norm"] = dgs[0]

    small_names = SMALL_REPL + SMALL_SHARDED
    tile_rows = 8
    head_rows = tile_rows * (len(small_names) + 1)

    def small_pack(rows_of, group):
        tiles = []
        for v, col in rows_of:
            v = v.reshape(1, -1)
            if col is None:
                tiles.append(jnp.pad(v, ((0, tile_rows - 1), (0, PACK_COLS - v.shape[1]))))
            else:
                tiles.append(lax.dynamic_update_slice(jnp.zeros((tile_rows, PACK_COLS), F32), v, (0, col)))
        return jnp.concatenate(tiles + [group.reshape(-1, PACK_COLS)], axis=0)

    small_grads = small_pack([(grads[n], None) for n in small_names] + [(loss_part[:, :1], None)], grads["a_w_group"])
    grads_1 = grads_start(EARLY_1, [small_grads], dx, "grads_1_start")

    def small_state(src):
        return small_pack([(src[n], None) for n in SMALL_REPL]
                          + [(src[n], me * src[n].shape[-1]) for n in SMALL_SHARDED]
                          + [(jnp.zeros((1, 1), F32), None)], src["a_w_group"])

    def adamw(names, after):
        for n in names:
            big_out[n] = reduce_adamw(recv[n], w[n].reshape(shard2d[n]), mom[n].reshape(shard2d[n]),
                                      var[n].reshape(shard2d[n]), "adamw_" + n, after=after)
        return big_out[names[-1]][0]

    big_out = {}
    recv.update(zip(LATE, recv_late))
    done = adamw(LATE, grads_1["token"])
    grads_done(grads_3, EARLY_3, done, "grads_3_wait")
    done = adamw(EARLY_3, None)
    grads_done(grads_2, EARLY_2, done, "grads_2_wait")
    done = adamw(EARLY_2, None)
    small_recv = own_slot(grads_done(grads_1, EARLY_1, done, "grads_1_wait")[-1], small_grads)
    adamw(EARLY_1, None)
    small_out = reduce_adamw(small_recv, small_state(w), small_state(mom), small_state(var), "adamw_small")

    def unpack(kind):
        out = {n: big_out[n][kind].reshape(w[n].shape) for n in BIG}
        so = small_out[kind]
        for r, n in enumerate(SMALL_REPL):
            out[n] = so[tile_rows * r, :w[n].shape[-1]].reshape(w[n].shape)
        for r, n in enumerate(SMALL_SHARDED):
            width, row = w[n].shape[-1], tile_rows * (len(SMALL_REPL) + r)
            out[n] = lax.dynamic_slice(so, (row, me * width), (1, width)).reshape(w[n].shape)
        out["a_w_group"] = so[head_rows:head_rows + a_w_group.size // PACK_COLS].reshape(a_w_group.shape)
        return out

    loss = small_out[0][tile_rows * len(small_names), 0]
    results = [loss, dx.reshape(x.shape)]
    for kind in range(4):
        out = unpack(kind)
        results += [out[n] for n in WEIGHTS]
    return tuple(results)
```

```python
import functools

import jax
import jax.numpy as jnp
from jax import lax
from jax.experimental import pallas as pl
from jax.experimental.pallas import tpu as pltpu

F32 = jnp.float32
BF16 = jnp.bfloat16

N_DEV = 8
HEAD_DIM = 64
N_SB_HEADS = 12
SB_WIDTH = N_SB_HEADS * HEAD_DIM
MEM_WIDTH = 4 * HEAD_DIM
POOL_WINDOWS = (2, 4, 8, 16)
POOL_GROUP = SB_WIDTH // 4
POOL_HALO = 16
EPS = 1e-6
QK_SCALE = HEAD_DIM ** -0.5
LANES = 128
PACK_COLS = 1024

ADAM_LR = 0.001
ADAM_B1 = 0.9
ADAM_B2 = 0.999
ADAM_EPS = 1e-08
ADAM_WD = 0.01
ADAM_STEP = 10

VMEM_LIMIT = 48 * 1024 * 1024

_NT = (((1,), (1,)), ((), ()))
_TN = (((0,), (0,)), ((), ()))
_NN = (((1,), (0,)), ((), ()))


def _params(*sem):
    return pltpu.CompilerParams(dimension_semantics=sem, vmem_limit_bytes=VMEM_LIMIT)


def _dot(a, b, dims=_NN):
    return lax.dot_general(a, b, dims, preferred_element_type=F32)


def exchange(a2a, ag, name):
    n_a2a, n_arr = len(a2a), len(a2a) + len(ag)

    def body(*refs):
        start, wait = _exchange_ops(refs[:n_arr], refs[n_arr:2 * n_arr], refs[2 * n_arr:], n_a2a)
        start()
        wait()

    return pl.pallas_call(
        body, name=name, out_shape=_exchange_out_shape(a2a, ag),
        in_specs=[_ANY] * n_arr, out_specs=[_ANY] * n_arr,
        scratch_shapes=_exchange_sems(n_arr),
    )(*a2a, *ag)


_ANY = pl.BlockSpec(memory_space=pl.ANY)
_HBM = pl.BlockSpec(memory_space=pltpu.HBM)
_SEM = pl.BlockSpec(memory_space=pltpu.SEMAPHORE)
_DATAFLOW = pltpu.SideEffectType.DATAFLOW_SIDE_EFFECTING


def exchange_start(a2a, ag, after, name):
    n_a2a, n_arr = len(a2a), len(a2a) + len(ag)
    arrays = [pltpu.with_memory_space_constraint(v, pltpu.HBM) for v in list(a2a) + list(ag)]
    shapes = _exchange_out_shape(a2a, ag)
    lands = [pltpu.with_memory_space_constraint(lax.empty(sh.shape, sh.dtype), pltpu.HBM) for sh in shapes]

    def body(*refs):
        ins, outs = refs[:n_arr], refs[n_arr:2 * n_arr]
        send_sems, recv_sems, token = refs[4 * n_arr + 1:4 * n_arr + 4]
        start, _ = _exchange_ops(ins, outs, (send_sems, recv_sems, None), n_a2a, local_copies=False)
        start()
        token[...] = jnp.zeros_like(token)

    outs = pl.pallas_call(
        body, name=name,
        out_shape=[pltpu.HBM(v.shape, v.dtype) for v in arrays] + [pltpu.HBM(sh.shape, sh.dtype) for sh in shapes]
        + [pltpu.SemaphoreType.DMA((n_arr * (N_DEV - 1),)), pltpu.SemaphoreType.DMA((n_arr * (N_DEV - 1),)),
           jax.ShapeDtypeStruct((8, LANES), F32)],
        in_specs=[_HBM] * (2 * n_arr) + [_ANY],
        out_specs=[_HBM] * (2 * n_arr) + [_SEM, _SEM, pl.BlockSpec(memory_space=pltpu.VMEM)],
        input_output_aliases={i: i for i in range(2 * n_arr)},
        compiler_params=pltpu.CompilerParams(has_side_effects=_DATAFLOW),
    )(*arrays, *lands, after)
    return dict(ins=outs[:n_arr], lands=outs[n_arr:2 * n_arr], sems=outs[2 * n_arr:2 * n_arr + 2], n_a2a=n_a2a,
                token=outs[2 * n_arr + 2])


def exchange_wait(handle, after, name):
    ins, lands, sems, n_a2a = handle["ins"], handle["lands"], handle["sems"], handle["n_a2a"]
    n_arr = len(ins)

    def body(*refs):
        in_refs, land_refs = refs[:n_arr], refs[n_arr:2 * n_arr]
        send_sems, recv_sems = refs[2 * n_arr:2 * n_arr + 2]
        _, wait = _exchange_ops(in_refs, land_refs, (send_sems, recv_sems, None), n_a2a, local_copies=False)
        wait()

    outs = pl.pallas_call(
        body, name=name,
        out_shape=[pltpu.HBM(v.shape, v.dtype) for v in list(ins) + list(lands)],
        in_specs=[_HBM] * (2 * n_arr) + [_SEM, _SEM, _ANY], out_specs=[_HBM] * (2 * n_arr),
        input_output_aliases={i: i for i in range(2 * n_arr)},
        compiler_params=pltpu.CompilerParams(has_side_effects=_DATAFLOW),
    )(*ins, *lands, *sems, after)
    return outs[n_arr:]


def _exchange_out_shape(a2a, ag):
    out_shape = [jax.ShapeDtypeStruct(v.shape, v.dtype) for v in a2a]
    return out_shape + [jax.ShapeDtypeStruct((N_DEV,) + v.shape, v.dtype) for v in ag]


def _exchange_sems(n_arr):
    return [pltpu.SemaphoreType.DMA((n_arr * (N_DEV - 1),)), pltpu.SemaphoreType.DMA((n_arr * (N_DEV - 1),)),
            pltpu.SemaphoreType.DMA((n_arr,))]


def _exchange_ops(ins, outs, sems, n_a2a, local_copies=True):
    send_sems, recv_sems, local_sems = sems
    n_arr = len(ins)
    x, y, c = lax.axis_index("x"), lax.axis_index("y"), lax.axis_index("c")
    me = 4 * x + 2 * y + c

    def peer_of(k):
        px = 1 - x if k & 4 else x
        py = 1 - y if k & 2 else y
        pc = 1 - c if k & 1 else c
        return (px, py, pc), 4 * px + 2 * py + pc

    def copy(a, k, landing):
        peer, pid = peer_of(k)
        src = ins[a].at[pid] if a < n_a2a else ins[a]
        return pltpu.make_async_remote_copy(
            src_ref=src, dst_ref=outs[a].at[pid if landing else me], send_sem=send_sems.at[a * (N_DEV - 1) + k - 1],
            recv_sem=recv_sems.at[a * (N_DEV - 1) + k - 1], device_id=peer, device_id_type=pl.DeviceIdType.MESH)

    def local(a):
        src = ins[a].at[me] if a < n_a2a else ins[a]
        return pltpu.make_async_copy(src, outs[a].at[me], local_sems.at[a])

    order = (6, 7, 4, 5, 2, 3, 1)

    def start():
        if local_copies:
            for a in range(n_arr):
                local(a).start()
        for a in range(n_arr):
            for k in order:
                copy(a, k, landing=False).start()

    def wait():
        for a in range(n_arr):
            for k in order:
                copy(a, k, landing=True).wait_recv()
        for a in range(n_arr):
            for k in order:
                copy(a, k, landing=False).wait_send()
        if local_copies:
            for a in range(n_arr):
                local(a).wait()

    return start, wait


def _rows_tile(n, want):
    t = min(want, n)
    assert n % t == 0, (n, t)
    return t


def rmsnorm_fwd(x, g, name):
    s, d = x.shape
    tm = _rows_tile(s, 512)

    def body(x_ref, g_ref, h_ref):
        xv = x_ref[...]
        r = lax.rsqrt(jnp.mean(xv * xv, axis=-1, keepdims=True) + EPS)
        h_ref[...] = ((xv * r) * g_ref[...]).astype(h_ref.dtype)

    return pl.pallas_call(
        body, name=name, grid=(s // tm,),
        in_specs=[pl.BlockSpec((tm, d), lambda i: (i, 0)), pl.BlockSpec((1, d), lambda i: (0, 0))],
        out_specs=pl.BlockSpec((tm, d), lambda i: (i, 0)),
        out_shape=jax.ShapeDtypeStruct((s, d), BF16),
        compiler_params=_params("parallel"),
    )(x, g.reshape(1, d))


def rmsnorm_bwd(x, dres, pairs, name):
    s, d = x.shape
    tm = _rows_tile(s, 256)
    n = len(pairs)

    def body(*refs):
        x_ref, dres_ref = refs[0], refs[1]
        g_refs, dh_refs = refs[2:2 + n], refs[2 + n:2 + 2 * n]
        dx_ref, dxb_ref = refs[2 + 2 * n], refs[3 + 2 * n]
        dg_refs = refs[4 + 2 * n:]
        i = pl.program_id(0)
        xv = x_ref[...]
        r = lax.rsqrt(jnp.mean(xv * xv, axis=-1, keepdims=True) + EPS)
        xhat = xv * r
        acc = dres_ref[...]
        for k in range(n):
            dh = dh_refs[k][...].astype(F32)
            part = jnp.sum(dh * xhat, axis=0, keepdims=True)

            @pl.when(i == 0)
            def _(k=k, part=part):
                dg_refs[k][...] = part

            @pl.when(i > 0)
            def _(k=k, part=part):
                dg_refs[k][...] += part

            dxh = dh * g_refs[k][...]
            acc = acc + r * (dxh - xhat * jnp.mean(dxh * xhat, axis=-1, keepdims=True))
        dx_ref[...] = acc
        dxb_ref[...] = acc.astype(BF16)

    row = pl.BlockSpec((tm, d), lambda i: (i, 0))
    vec = pl.BlockSpec((1, d), lambda i: (0, 0))
    outs = pl.pallas_call(
        body, name=name, grid=(s // tm,),
        in_specs=[row, row] + [vec] * n + [row] * n,
        out_specs=[row, row] + [vec] * n,
        out_shape=[jax.ShapeDtypeStruct((s, d), F32), jax.ShapeDtypeStruct((s, d), BF16)]
        + [jax.ShapeDtypeStruct((1, d), F32)] * n,
        compiler_params=_params("arbitrary"),
    )(x, dres, *[g.reshape(1, d) for g, _ in pairs], *[dh for _, dh in pairs])
    return outs[0], outs[1], outs[2:]


def loss_head(x, tgt, g, name):
    s, d = x.shape
    tm = _rows_tile(s, 256)

    def body(x_ref, t_ref, g_ref, dx_ref, dxb_ref, dg_ref, loss_ref):
        i = pl.program_id(0)
        xv = x_ref[...]
        gv = g_ref[...]
        r = lax.rsqrt(jnp.mean(xv * xv, axis=-1, keepdims=True) + EPS)
        xhat = xv * r
        diff = xhat * gv - t_ref[...]
        part_loss = 0.5 * jnp.sum(jnp.mean(diff * diff, axis=-1, keepdims=True), axis=0, keepdims=True)
        dy = diff * (1.0 / d)
        part_dg = jnp.sum(dy * xhat, axis=0, keepdims=True)

        @pl.when(i == 0)
        def _():
            dg_ref[...] = part_dg
            loss_ref[...] = jnp.broadcast_to(part_loss, loss_ref.shape)

        @pl.when(i > 0)
        def _():
            dg_ref[...] += part_dg
            loss_ref[...] += jnp.broadcast_to(part_loss, loss_ref.shape)

        dxh = dy * gv
        dx = r * (dxh - xhat * jnp.mean(dxh * xhat, axis=-1, keepdims=True))
        dx_ref[...] = dx
        dxb_ref[...] = dx.astype(BF16)

    row = pl.BlockSpec((tm, d), lambda i: (i, 0))
    vec = pl.BlockSpec((1, d), lambda i: (0, 0))
    return pl.pallas_call(
        body, name=name, grid=(s // tm,),
        in_specs=[row, row, vec], out_specs=[row, row, vec, pl.BlockSpec((1, LANES), lambda i: (0, 0))],
        out_shape=[jax.ShapeDtypeStruct((s, d), F32), jax.ShapeDtypeStruct((s, d), BF16),
                   jax.ShapeDtypeStruct((1, d), F32), jax.ShapeDtypeStruct((1, LANES), F32)],
        compiler_params=_params("arbitrary"),
    )(x, tgt, g.reshape(1, d))


def matmul(pairs, mode, out_dtype, name, res=None, tm=512, tn=512, roles=(None, None), count=1):
    a0, b0 = pairs[0]
    m = a0.shape[-1] if mode == "tn" else a0.shape[-2]
    n = b0.shape[-2] if mode == "nt" else b0.shape[-1]
    tm, tn = _rows_tile(m, tm), _rows_tile(n, tn)
    dims = {"nn": _NN, "nt": _NT, "tn": _TN}[mode]
    npairs = len(pairs)
    batched, reducing = "batch" in roles, "reduce" in roles
    assert not (batched and reducing) and (res is None or not batched)
    n_in = 2 * npairs + (res is not None)

    def body(*refs):
        o_ref = refs[n_in]
        acc = None
        for p in range(npairs):
            a_ref, b_ref = refs[2 * p], refs[2 * p + 1]
            for kk in range(count if reducing else 1):
                a = a_ref[kk] if roles[0] == "reduce" else a_ref[...]
                b = b_ref[kk] if roles[1] == "reduce" else b_ref[...]
                d = _dot(a.astype(BF16), b.astype(BF16), dims)
                acc = d if acc is None else acc + d
        if res is not None:
            acc = acc + refs[2 * npairs][...]
        o_ref[...] = acc.astype(o_ref.dtype)

    def spec(arr, role, block, index):
        if arr.ndim == 2:
            return pl.BlockSpec(block, lambda bb, i, j: index(i, j))
        assert role in ("batch", "reduce") and arr.shape[0] == count, (name, role, arr.shape)
        if role == "batch":
            return pl.BlockSpec((None,) + block, lambda bb, i, j: (bb,) + index(i, j))
        return pl.BlockSpec((count,) + block, lambda bb, i, j: (0,) + index(i, j))

    in_specs, args = [], []
    for a, b in pairs:
        if mode == "tn":
            in_specs.append(spec(a, roles[0], (a.shape[-2], tm), lambda i, j: (0, i)))
        else:
            in_specs.append(spec(a, roles[0], (tm, a.shape[-1]), lambda i, j: (i, 0)))
        if mode == "nt":
            in_specs.append(spec(b, roles[1], (tn, b.shape[-1]), lambda i, j: (j, 0)))
        else:
            in_specs.append(spec(b, roles[1], (b.shape[-2], tn), lambda i, j: (0, j)))
        args += [a, b]
    if res is not None:
        in_specs.append(pl.BlockSpec((tm, tn), lambda bb, i, j: (i, j)))
        args.append(res)
    if batched:
        out_spec = pl.BlockSpec((None, tm, tn), lambda bb, i, j: (bb, i, j))
        out_shape = jax.ShapeDtypeStruct((count, m, n), out_dtype)
    else:
        out_spec = pl.BlockSpec((tm, tn), lambda bb, i, j: (i, j))
        out_shape = jax.ShapeDtypeStruct((m, n), out_dtype)
    return pl.pallas_call(
        body, name=name, grid=(count if batched else 1, m // tm, n // tn),
        in_specs=in_specs, out_specs=out_spec, out_shape=out_shape,
        compiler_params=_params("parallel", "parallel", "parallel"),
    )(*args)


def _sigmoid(v):
    return 1.0 / (1.0 + jnp.exp(-v))


def ffn_up(h, wgu, name):
    s, d = h.shape
    _, n_sh, _, fs = wgu.shape
    tm = _rows_tile(s, 512)

    def body(h_ref, w_ref, gu_ref, act_ref):
        hv = h_ref[...]
        gate = _dot(hv, w_ref[0])
        up = _dot(hv, w_ref[1])
        gu_ref[0] = gate.astype(BF16)
        gu_ref[1] = up.astype(BF16)
        act_ref[...] = ((gate * _sigmoid(gate)) * up).astype(BF16)

    return pl.pallas_call(
        body, name=name, grid=(n_sh, s // tm),
        in_specs=[pl.BlockSpec((tm, d), lambda j, i: (i, 0)), pl.BlockSpec((2, None, d, fs), lambda j, i: (0, j, 0, 0))],
        out_specs=[pl.BlockSpec((2, None, tm, fs), lambda j, i: (0, j, i, 0)),
                   pl.BlockSpec((None, tm, fs), lambda j, i: (j, i, 0))],
        out_shape=[jax.ShapeDtypeStruct((2, n_sh, s, fs), BF16), jax.ShapeDtypeStruct((n_sh, s, fs), BF16)],
        compiler_params=_params("parallel", "parallel"),
    )(h, wgu)


def ffn_bwd_act(dxo, wdown, gu, name):
    s, d = dxo.shape
    n_sh, fs, _ = wdown.shape
    tm = _rows_tile(s, 512)

    def body(dx_ref, w_ref, gu_ref, dgu_ref):
        dact = _dot(dx_ref[...], w_ref[...], _NT)
        gate = gu_ref[0].astype(F32)
        sig = _sigmoid(gate)
        dgu_ref[0] = ((dact * gu_ref[1].astype(F32)) * (sig * (1.0 + gate * (1.0 - sig)))).astype(BF16)
        dgu_ref[1] = (dact * (gate * sig)).astype(BF16)

    tile = pl.BlockSpec((2, None, tm, fs), lambda j, i: (0, j, i, 0))
    return pl.pallas_call(
        body, name=name, grid=(n_sh, s // tm),
        in_specs=[pl.BlockSpec((tm, d), lambda j, i: (i, 0)), pl.BlockSpec((None, fs, d), lambda j, i: (j, 0, 0)), tile],
        out_specs=tile,
        out_shape=jax.ShapeDtypeStruct((2, n_sh, s, fs), BF16),
        compiler_params=_params("parallel", "parallel"),
    )(dxo, wdown, gu)


def _pool_select(col, by_window):
    out = by_window[3]
    for g in (2, 1, 0):
        out = jnp.where(col < (g + 1) * POOL_GROUP, by_window[g], out)
    return out


def _pool_counts(pos):
    return [jnp.minimum(pos + 1, w).astype(F32) for w in POOL_WINDOWS]


def pool_fwd(proj, wbd, scale, name):
    s = proj.shape[0]
    tm = _rows_tile(s, 256)
    nt = s // tm
    width = SB_WIDTH
    per = tm // POOL_HALO

    def body(u_ref, halo_ref, w_ref, sc_ref, pooled_ref, grouped_ref):
        i = pl.program_id(0)
        u = u_ref[...]
        halo = jnp.where(i > 0, halo_ref[...], 0.0)
        ext = jnp.concatenate([halo, u], axis=0)
        sums = []
        acc = ext
        for k in (1, 2, 4, 8):
            acc = acc + pltpu.roll(acc, k, 0)
            sums.append(acc[POOL_HALO:])
        pos = i * tm + lax.broadcasted_iota(jnp.int32, (tm, width), 0)
        col = lax.broadcasted_iota(jnp.int32, (tm, width), 1)
        means = [sm / cnt for sm, cnt in zip(sums, _pool_counts(pos))]
        pooled = (_pool_select(col, means) - u).astype(BF16)
        pooled_ref[...] = pooled
        grouped_ref[...] = (_dot(pooled, w_ref[...]) * sc_ref[...]).astype(BF16)

    row = pl.BlockSpec((tm, width), lambda i: (i, 0))
    return pl.pallas_call(
        body, name=name, grid=(nt,),
        in_specs=[row, pl.BlockSpec((POOL_HALO, width), lambda i: (jnp.maximum(i * per - 1, 0), 0)),
                  pl.BlockSpec((width, width), lambda i: (0, 0)), pl.BlockSpec((1, width), lambda i: (0, 0))],
        out_specs=[row, row],
        out_shape=[jax.ShapeDtypeStruct((s, width), BF16), jax.ShapeDtypeStruct((s, width + MEM_WIDTH), BF16)],
        compiler_params=_params("parallel"),
    )(proj, proj, wbd, scale.reshape(1, width))


def pool_bwd(dcat, pooled, wbd, scale, name):
    s = dcat.shape[0]
    tm = _rows_tile(s, 256)
    nt = s // tm
    width = SB_WIDTH
    per = tm // POOL_HALO
    n_ext = tm + POOL_HALO

    def body(dg_ref, halo_ref, pooled_ref, w_ref, sc_ref, du_ref, dw_ref, dsc_ref):
        i = pl.program_id(0)
        w = w_ref[...]
        sc = sc_ref[...]
        dg = dg_ref[...]
        pooled = pooled_ref[...]
        pg = _dot(pooled, w)
        dpg = (dg * sc).astype(BF16)
        part_sc = jnp.sum(dg * pg, axis=0, keepdims=True)
        part_w = _dot(pooled, dpg, _TN)

        @pl.when(i == 0)
        def _():
            dsc_ref[...] = part_sc
            dw_ref[...] = part_w

        @pl.when(i > 0)
        def _():
            dsc_ref[...] += part_sc
            dw_ref[...] += part_w

        dpooled = _dot(dpg, w, _NT)
        halo_dpg = (jnp.where(i < nt - 1, halo_ref[...], 0.0) * sc).astype(BF16)
        ext = jnp.concatenate([dpooled, _dot(halo_dpg, w, _NT)], axis=0)
        pos = i * tm + lax.broadcasted_iota(jnp.int32, (n_ext, width), 0)
        col = lax.broadcasted_iota(jnp.int32, (tm, width), 1)
        outs = []
        for k_idx, cnt in enumerate(_pool_counts(pos)):
            acc = ext / cnt
            for k in (1, 2, 4, 8)[:k_idx + 1]:
                acc = acc + pltpu.roll(acc, n_ext - k, 0)
            outs.append(acc[:tm])
        du_ref[...] = (_pool_select(col, outs) - dpooled).astype(BF16)

    row = pl.BlockSpec((tm, width), lambda i: (i, 0))
    return pl.pallas_call(
        body, name=name, grid=(nt,),
        in_specs=[row, pl.BlockSpec((POOL_HALO, width), lambda i: (jnp.minimum((i + 1) * per, s // POOL_HALO - 1), 0)),
                  row, pl.BlockSpec((width, width), lambda i: (0, 0)), pl.BlockSpec((1, width), lambda i: (0, 0))],
        out_specs=[row, pl.BlockSpec((width, width), lambda i: (0, 0)), pl.BlockSpec((1, width), lambda i: (0, 0))],
        out_shape=[jax.ShapeDtypeStruct((s, width + MEM_WIDTH), BF16), jax.ShapeDtypeStruct((width, width), F32),
                   jax.ShapeDtypeStruct((1, width), F32)],
        compiler_params=_params("arbitrary"),
    )(dcat, dcat, pooled, wbd, scale.reshape(1, width))


def _head_masks(shape):
    lane = lax.broadcasted_iota(jnp.int32, shape, 1)
    return lane < HEAD_DIM, lane >= HEAD_DIM


def _mem_probs(qh, kp):
    logits = _dot(qh, kp, _NT)
    logits = logits - jnp.max(logits, axis=-1, keepdims=True)
    p = jnp.exp(logits)
    return p / jnp.sum(p, axis=-1, keepdims=True)


def mem_attn_fwd(proj, mkv, cat, name):
    s = proj.shape[0]
    m = mkv.shape[0]
    tm = _rows_tile(s, 256)
    q_blk = SB_WIDTH // MEM_WIDTH

    def body(q_ref, kv_ref, _, o_ref):
        masks = _head_masks((tm, LANES))
        for pr in range(MEM_WIDTH // LANES):
            q = q_ref[:, pr * LANES:(pr + 1) * LANES] * QK_SCALE
            kp = kv_ref[:, pr * LANES:(pr + 1) * LANES]
            vp = kv_ref[:, MEM_WIDTH + pr * LANES:MEM_WIDTH + (pr + 1) * LANES]
            outs = []
            for hm in masks:
                p = _mem_probs(jnp.where(hm, q, 0.0).astype(BF16), kp)
                outs.append(_dot(p.astype(BF16), vp))
            o_ref[:, pr * LANES:(pr + 1) * LANES] = jnp.where(masks[0], outs[0], outs[1]).astype(BF16)

    return pl.pallas_call(
        body, name=name, grid=(s // tm,),
        in_specs=[pl.BlockSpec((tm, MEM_WIDTH), lambda i: (i, q_blk)), pl.BlockSpec((m, 2 * MEM_WIDTH), lambda i: (0, 0)),
                  _ANY],
        out_specs=pl.BlockSpec((tm, MEM_WIDTH), lambda i: (i, q_blk)),
        out_shape=jax.ShapeDtypeStruct(cat.shape, BF16),
        input_output_aliases={2: 0},
        compiler_params=_params("parallel"),
    )(proj, mkv, cat)


def mem_attn_bwd(proj, mkv, dcat, dproj, name):
    s = proj.shape[0]
    m = mkv.shape[0]
    tm = _rows_tile(s, 256)
    q_blk = SB_WIDTH // MEM_WIDTH

    def body(q_ref, kv_ref, do_ref, _, dq_ref, dkv_ref):
        i = pl.program_id(0)

        @pl.when(i == 0)
        def _():
            dkv_ref[...] = jnp.zeros_like(dkv_ref)

        masks = _head_masks((tm, LANES))
        for pr in range(MEM_WIDTH // LANES):
            q = q_ref[:, pr * LANES:(pr + 1) * LANES] * QK_SCALE
            do = do_ref[:, pr * LANES:(pr + 1) * LANES]
            kp = kv_ref[:, pr * LANES:(pr + 1) * LANES]
            vp = kv_ref[:, MEM_WIDTH + pr * LANES:MEM_WIDTH + (pr + 1) * LANES]
            dqs = []
            dk = jnp.zeros((m, LANES), F32)
            dv = jnp.zeros((m, LANES), F32)
            for hm in masks:
                qh = jnp.where(hm, q, 0.0).astype(BF16)
                doh = jnp.where(hm, do, 0.0).astype(BF16)
                p = _mem_probs(qh, kp)
                dp = _dot(doh, vp, _NT)
                ds = (p * (dp - jnp.sum(dp * p, axis=-1, keepdims=True))).astype(BF16)
                dqs.append(_dot(ds, kp))
                dk = dk + _dot(ds, qh, _TN)
                dv = dv + _dot(p.astype(BF16), doh, _TN)
            dq_ref[:, pr * LANES:(pr + 1) * LANES] = (jnp.where(masks[0], dqs[0], dqs[1]) * QK_SCALE).astype(BF16)
            dkv_ref[:, pr * LANES:(pr + 1) * LANES] += dk
            dkv_ref[:, MEM_WIDTH + pr * LANES:MEM_WIDTH + (pr + 1) * LANES] += dv

    return pl.pallas_call(
        body, name=name, grid=(s // tm,),
        in_specs=[pl.BlockSpec((tm, MEM_WIDTH), lambda i: (i, q_blk)), pl.BlockSpec((m, 2 * MEM_WIDTH), lambda i: (0, 0)),
                  pl.BlockSpec((tm, MEM_WIDTH), lambda i: (i, q_blk)), _ANY],
        out_specs=[pl.BlockSpec((tm, MEM_WIDTH), lambda i: (i, q_blk)), pl.BlockSpec((m, 2 * MEM_WIDTH), lambda i: (0, 0))],
        out_shape=[jax.ShapeDtypeStruct(dproj.shape, BF16), jax.ShapeDtypeStruct((m, 2 * MEM_WIDTH), F32)],
        input_output_aliases={3: 0},
        compiler_params=_params("arbitrary"),
    )(proj, mkv, dcat, dproj)


SB_ROWS = 512
SB_KEYS = 512
SB_SUB = LANES


def _sb_tri(later):
    r = lax.broadcasted_iota(jnp.int32, (2 * SB_SUB, 2 * SB_SUB), 0) % SB_SUB
    c = lax.broadcasted_iota(jnp.int32, (2 * SB_SUB, 2 * SB_SUB), 1)
    tri = (r > c) if later else (r < c)
    return (tri | (c >= SB_SUB)).astype(BF16)


def _sb_sums(v, tri, n_sub):
    out = []
    for m in range(n_sub):
        vm = v[:, m * SB_SUB:(m + 1) * SB_SUB]
        hi = vm.astype(BF16)
        lo = (vm - hi.astype(F32)).astype(BF16)
        r = _dot(jnp.concatenate([hi, lo], axis=1), tri)
        out.append((r[:, :SB_SUB], r[:, SB_SUB:]))
    return out


def _sb_logs(nqh, k_blk, valid):
    nz = _dot(nqh, k_blk, _NT)
    log_not = jnp.minimum(nz, 0.0) - jnp.log(1.0 + jnp.exp(-jnp.abs(nz)))
    log_beta = log_not - nz
    if valid is not None:
        log_not = jnp.where(valid, log_not, 0.0)
    return log_not, log_beta


def _sb_weights(log_not, log_beta, tri_later, carry_in, valid, n_sub):
    sums = _sb_sums(log_not, tri_later, n_sub)
    run = carry_in
    later = [None] * n_sub
    for m in reversed(range(n_sub)):
        later[m] = sums[m][0] + run
        run = run + sums[m][1]
    w = jnp.exp(log_beta + jnp.concatenate(later, axis=1))
    if valid is not None:
        w = jnp.where(valid, w, 0.0)
    return w, run


def _sb_valid(i, j, tq, tk):
    qpos = i * tq + lax.broadcasted_iota(jnp.int32, (tq, tk), 0)
    kpos = j * tk + lax.broadcasted_iota(jnp.int32, (tq, tk), 1)
    return qpos > kpos


def sb_fwd(projb, kv, name, gather=()):
    s = projb.shape[0]
    tq, tk = _rows_tile(s, SB_ROWS), _rows_tile(s, SB_KEYS)
    n_sub = tk // SB_SUB
    n_pairs = SB_WIDTH // LANES
    assert s // tk <= LANES and tk % tq == 0

    n_g = len(gather)
    n_i = s // tq

    def body(*refs):
        q_ref, k_ref, v_ref = refs[:3]
        o_ref, c_ref = refs[3 + n_g:5 + n_g]
        i = pl.program_id(1)
        if n_g:
            start, wait = _exchange_ops(refs[3:3 + n_g], refs[5 + n_g:5 + 2 * n_g], refs[5 + 2 * n_g:], 0)
            pl.when((pl.program_id(0) == 0) & (i == 0))(start)
        j_diag = (i * tq) // tk
        masks = _head_masks((tq, LANES))
        nq = q_ref[...] * (-QK_SCALE)
        tri = _sb_tri(later=True)
        lane = lax.broadcasted_iota(jnp.int32, (tq, LANES), 1)
        nqh = [jnp.where(hm, nq, 0.0).astype(BF16) for hm in masks]

        def block(j, carry, valid):
            off = pl.multiple_of(j * tk, tk)
            k_blk, v_blk = k_ref[pl.ds(off, tk), :], v_ref[pl.ds(off, tk), :]
            out = []
            for hh in range(2):
                c, acc, cmat = carry[3 * hh:3 * hh + 3]
                log_not, log_beta = _sb_logs(nqh[hh], k_blk, valid)
                w, c_next = _sb_weights(log_not, log_beta, tri, c, valid, n_sub)
                out += [c_next, acc + _dot(w.astype(BF16), v_blk), jnp.where(lane == j, c, cmat)]
            return tuple(out)

        zero = jnp.zeros((tq, LANES), F32)
        carry = block(j_diag, (zero,) * 6, _sb_valid(i, j_diag, tq, tk))
        carry = lax.fori_loop(0, j_diag, lambda jj, cr: block(j_diag - 1 - jj, cr, None), carry)
        c_ref[0] = carry[2]
        c_ref[1] = carry[5]
        o_ref[...] = jnp.where(masks[0], carry[1], carry[4]).astype(BF16)
        if n_g:
            pl.when((pl.program_id(0) == n_pairs - 1) & (i == n_i - 1))(wait)

    outs = pl.pallas_call(
        body, name=name, grid=(n_pairs, n_i),
        in_specs=[pl.BlockSpec((tq, LANES), lambda p, i: (i, p)),
                  pl.BlockSpec((s, LANES), lambda p, i: (0, p)),
                  pl.BlockSpec((s, LANES), lambda p, i: (0, n_pairs + p))] + [_ANY] * n_g,
        out_specs=[pl.BlockSpec((tq, LANES), lambda p, i: (i, p)),
                   pl.BlockSpec((2, tq, LANES), lambda p, i: (p, i, 0))] + [_ANY] * n_g,
        out_shape=[jax.ShapeDtypeStruct((s, SB_WIDTH + MEM_WIDTH), BF16),
                   jax.ShapeDtypeStruct((N_SB_HEADS, s, LANES), F32)] + _exchange_out_shape([], gather),
        scratch_shapes=_exchange_sems(n_g) if n_g else [],
        compiler_params=_params("arbitrary", "arbitrary"),
    )(projb, kv, kv, *gather)
    return outs[0], outs[1], outs[2:]


def sb_bwd(projb, kv, dcat, csave, name, scatter=()):
    s = projb.shape[0]
    tq, tk = _rows_tile(s, SB_ROWS), _rows_tile(s, SB_KEYS)
    n_sub = tk // SB_SUB
    n_pairs = SB_WIDTH // LANES
    n_x = len(scatter)
    n_i = s // tq

    def body(*refs):
        q_ref, k_ref, v_ref, do_ref, c_ref = refs[:5]
        dq_ref, dkb_ref, dvb_ref = refs[5 + n_x:8 + n_x]
        dk_ref, dv_ref = refs[8 + 2 * n_x:10 + 2 * n_x]
        i = pl.program_id(1)
        if n_x:
            start, wait = _exchange_ops(refs[5:5 + n_x], refs[8 + n_x:8 + 2 * n_x], refs[10 + 2 * n_x:], n_x)
            pl.when((pl.program_id(0) == 0) & (i == 0))(start)
        j_diag = (i * tq) // tk

        @pl.when(i == 0)
        def _():
            dk_ref[...] = jnp.zeros_like(dk_ref)
            dv_ref[...] = jnp.zeros_like(dv_ref)

        masks = _head_masks((tq, LANES))
        qs = q_ref[...] * QK_SCALE
        do = do_ref[...]
        tri_later = _sb_tri(later=True)
        tri_before = _sb_tri(later=False)
        lane = lax.broadcasted_iota(jnp.int32, (tq, LANES), 1)
        qh = [jnp.where(hm, qs, 0.0).astype(BF16) for hm in masks]
        nqh = [jnp.where(hm, -qs, 0.0).astype(BF16) for hm in masks]
        doh = [jnp.where(hm, do, 0.0).astype(BF16) for hm in masks]

        def block(j, carry, valid):
            off = pl.multiple_of(j * tk, tk)
            k_blk, v_blk = k_ref[pl.ds(off, tk), :], v_ref[pl.ds(off, tk), :]
            out, dk, dv = [], None, None
            for hh in range(2):
                e_before, dq = carry[2 * hh:2 * hh + 2]
                log_not, log_beta = _sb_logs(nqh[hh], k_blk, valid)
                c = jnp.sum(jnp.where(lane == j, c_ref[hh], 0.0), axis=1, keepdims=True)
                w, _ = _sb_weights(log_not, log_beta, tri_later, jnp.broadcast_to(c, (tq, LANES)), valid, n_sub)
                dv_h = _dot(w.astype(BF16), doh[hh], _TN)
                e = w * _dot(doh[hh], v_blk, _NT)
                sums = _sb_sums(e, tri_before, n_sub)
                run = e_before
                e_pre = []
                for m in range(n_sub):
                    e_pre.append(sums[m][0] + run)
                    run = run + sums[m][1]
                dz = e - jnp.exp(log_beta) * (e + jnp.concatenate(e_pre, axis=1))
                if valid is not None:
                    dz = jnp.where(valid, dz, 0.0)
                dz = dz.astype(BF16)
                dk_h = _dot(dz, qh[hh], _TN)
                dk, dv = (dk_h, dv_h) if hh == 0 else (dk + dk_h, dv + dv_h)
                out += [run, dq + _dot(dz, k_blk)]
            dk_ref[pl.ds(off, tk), :] += dk
            dv_ref[pl.ds(off, tk), :] += dv
            return tuple(out)

        zero = jnp.zeros((tq, LANES), F32)
        carry = lax.fori_loop(0, j_diag, lambda j, cr: block(j, cr, None), (zero,) * 4)
        carry = block(j_diag, carry, _sb_valid(i, j_diag, tq, tk))
        dq_ref[...] = (jnp.where(masks[0], carry[1], carry[3]) * QK_SCALE).astype(BF16)

        @pl.when(i == n_i - 1)
        def _():
            dkb_ref[...] = dk_ref[...].astype(BF16)
            dvb_ref[...] = dv_ref[...].astype(BF16)

        if n_x:
            pl.when((pl.program_id(0) == n_pairs - 1) & (i == n_i - 1))(wait)

    tile = pl.BlockSpec((tq, LANES), lambda p, i: (i, p))
    full = pl.BlockSpec((s, LANES), lambda p, i: (0, p))
    outs = pl.pallas_call(
        body, name=name, grid=(n_pairs, n_i),
        in_specs=[tile, full, pl.BlockSpec((s, LANES), lambda p, i: (0, n_pairs + p)), tile,
                  pl.BlockSpec((2, tq, LANES), lambda p, i: (p, i, 0))] + [_ANY] * n_x,
        out_specs=[tile, full, full] + [_ANY] * n_x,
        out_shape=[jax.ShapeDtypeStruct((s, SB_WIDTH + MEM_WIDTH), BF16), jax.ShapeDtypeStruct((s, SB_WIDTH), BF16),
                   jax.ShapeDtypeStruct((s, SB_WIDTH), BF16)] + _exchange_out_shape(scatter, []),
        scratch_shapes=[pltpu.VMEM((s, LANES), F32), pltpu.VMEM((s, LANES), F32)]
        + (_exchange_sems(n_x) if n_x else []),
        compiler_params=_params("arbitrary", "arbitrary"),
    )(projb, kv, kv, dcat, csave, *scatter)
    return outs[0], outs[1], outs[2], outs[3:]


def reduce_adamw(parts, w, m, v, name, after=None):
    n, r, c = parts.shape
    tr = next((t for t in (128, 176) if r % t == 0), r)
    bias1 = 1.0 - ADAM_B1 ** ADAM_STEP
    bias2 = 1.0 - ADAM_B2 ** ADAM_STEP

    def body(p_ref, w_ref, m_ref, v_ref, *rest):
        g_ref, d_ref, nm_ref, nv_ref = rest[-4:]
        g = p_ref[0].astype(F32)
        for k in range(1, n):
            g = g + p_ref[k].astype(F32)
        new_m = ADAM_B1 * m_ref[...] + (1.0 - ADAM_B1) * g
        new_v = ADAM_B2 * v_ref[...] + (1.0 - ADAM_B2) * (g * g)
        m_hat = new_m / bias1
        v_hat = new_v / bias2
        g_ref[...] = g
        d_ref[...] = -ADAM_LR * (m_hat / (jnp.sqrt(v_hat) + ADAM_EPS) + ADAM_WD * w_ref[...])
        nm_ref[...] = new_m
        nv_ref[...] = new_v

    row = pl.BlockSpec((tr, c), lambda i: (i, 0))
    return pl.pallas_call(
        body, name=name, grid=(r // tr,),
        in_specs=[pl.BlockSpec((n, tr, c), lambda i: (0, i, 0)), row, row, row] + ([] if after is None else [_ANY]),
        out_specs=[row, row, row, row],
        out_shape=[jax.ShapeDtypeStruct((r, c), F32)] * 4,
        compiler_params=_params("parallel"),
    )(parts, w, m, v, *(() if after is None else (after,)))


EARLY_1 = ("a_w_in", "a_w_mem_kv", "a_w_out")
EARLY_2 = ("a_w_gu", "a_w_down")
EARLY_3 = ("w_kv", "b_w_q", "b_w_mem_kv")
EARLY = EARLY_1 + EARLY_2 + EARLY_3
LATE = ("b_w_out", "b_w_gu", "b_w_down")
BIG = EARLY + LATE
COL_SHARDED = ("a_w_gu", "w_kv", "b_w_gu")
SMALL_SHARDED = ("a_norm_mix", "a_scale", "a_norm_ffn")
SMALL_REPL = ("mem_norm", "kv_norm", "b_norm_mix", "b_norm_ffn", "final_norm")
WEIGHTS = ("mem_norm", "a_norm_mix", "a_w_in", "a_w_group", "a_scale", "a_w_mem_kv", "a_w_out", "a_norm_ffn", "a_w_gu",
           "a_w_down", "kv_norm", "w_kv", "b_norm_mix", "b_w_q", "b_w_mem_kv", "b_w_out", "b_norm_ffn", "b_w_gu",
           "b_w_down", "final_norm")


def _pad_row(v):
    v = v.reshape(1, -1)
    return jnp.pad(v, ((0, 0), (0, PACK_COLS - v.shape[1])))


def kernel(x, mem, mem_norm, a_norm_mix, a_w_in, a_w_group, a_scale, a_w_mem_kv, a_w_out, a_norm_ffn, a_w_gu, a_w_down, kv_norm, w_kv, b_norm_mix, b_w_q, b_w_mem_kv, b_w_out, b_norm_ffn, b_w_gu, b_w_down, final_norm, loss_target, m_mem_norm, m_a_norm_mix, m_a_w_in, m_a_w_group, m_a_scale, m_a_w_mem_kv, m_a_w_out, m_a_norm_ffn, m_a_w_gu, m_a_w_down, m_kv_norm, m_w_kv, m_b_norm_mix, m_b_w_q, m_b_w_mem_kv, m_b_w_out, m_b_norm_ffn, m_b_w_gu, m_b_w_down, m_final_norm, v_mem_norm, v_a_norm_mix, v_a_w_in, v_a_w_group, v_a_scale, v_a_w_mem_kv, v_a_w_out, v_a_norm_ffn, v_a_w_gu, v_a_w_down, v_kv_norm, v_w_kv, v_b_norm_mix, v_b_w_q, v_b_w_mem_kv, v_b_w_out, v_b_norm_ffn, v_b_w_gu, v_b_w_down, v_final_norm):
    w = dict(mem_norm=mem_norm, a_norm_mix=a_norm_mix, a_w_in=a_w_in, a_w_group=a_w_group, a_scale=a_scale,
             a_w_mem_kv=a_w_mem_kv, a_w_out=a_w_out, a_norm_ffn=a_norm_ffn, a_w_gu=a_w_gu, a_w_down=a_w_down,
             kv_norm=kv_norm, w_kv=w_kv, b_norm_mix=b_norm_mix, b_w_q=b_w_q, b_w_mem_kv=b_w_mem_kv, b_w_out=b_w_out,
             b_norm_ffn=b_norm_ffn, b_w_gu=b_w_gu, b_w_down=b_w_down, final_norm=final_norm)
    mom = dict(mem_norm=m_mem_norm, a_norm_mix=m_a_norm_mix, a_w_in=m_a_w_in, a_w_group=m_a_w_group, a_scale=m_a_scale,
               a_w_mem_kv=m_a_w_mem_kv, a_w_out=m_a_w_out, a_norm_ffn=m_a_norm_ffn, a_w_gu=m_a_w_gu,
               a_w_down=m_a_w_down, kv_norm=m_kv_norm, w_kv=m_w_kv, b_norm_mix=m_b_norm_mix, b_w_q=m_b_w_q,
               b_w_mem_kv=m_b_w_mem_kv, b_w_out=m_b_w_out, b_norm_ffn=m_b_norm_ffn, b_w_gu=m_b_w_gu,
               b_w_down=m_b_w_down, final_norm=m_final_norm)
    var = dict(mem_norm=v_mem_norm, a_norm_mix=v_a_norm_mix, a_w_in=v_a_w_in, a_w_group=v_a_w_group, a_scale=v_a_scale,
               a_w_mem_kv=v_a_w_mem_kv, a_w_out=v_a_w_out, a_norm_ffn=v_a_norm_ffn, a_w_gu=v_a_w_gu,
               a_w_down=v_a_w_down, kv_norm=v_kv_norm, w_kv=v_w_kv, b_norm_mix=v_b_norm_mix, b_w_q=v_b_w_q,
               b_w_mem_kv=v_b_w_mem_kv, b_w_out=v_b_w_out, b_norm_ffn=v_b_norm_ffn, b_w_gu=v_b_w_gu,
               b_w_down=v_b_w_down, final_norm=v_final_norm)

    me = 4 * lax.axis_index("x") + 2 * lax.axis_index("y") + lax.axis_index("c")
    shard2d = {n: w[n].shape[-2:] for n in BIG}
    shard = {n: w[n].reshape(shard2d[n]).astype(BF16) for n in BIG}
    n_sh = N_DEV // 2
    s_len = x.shape[1]

    small_send = jnp.concatenate([_pad_row(w[n]) for n in SMALL_SHARDED]
                                 + [jnp.zeros((8 - len(SMALL_SHARDED), PACK_COLS), F32)], axis=0)
    def own_slot(land, own):
        return lax.dynamic_update_slice(land, own[None], (me, 0, 0))

    def gather_done(handle, names, after, name):
        got = exchange_wait(handle, after, name)
        gathered.update({n: own_slot(g, shard[n]) for n, g in zip(names, got)})
        return got

    gathered = {}
    gather_1 = exchange_start([], [shard[n] for n in EARLY_1] + [small_send], x, "gather_1_start")
    gather_2 = exchange_start([], [shard[n] for n in EARLY_2], gather_1["token"], "gather_2_start")
    gather_3 = exchange_start([], [shard[n] for n in EARLY_3], gather_2["token"], "gather_3_start")
    small_all = own_slot(gather_done(gather_1, EARLY_1, gather_3["token"], "gather_1_wait")[-1], small_send)
    a_norm_mix_f = small_all[:, 0, :a_norm_mix.shape[1]].reshape(-1)
    a_scale_f = small_all[:, 1, :a_scale.shape[1]].reshape(-1)
    a_norm_ffn_f = small_all[:, 2, :a_norm_ffn.shape[1]].reshape(-1)

    def rows_full(n):
        return gathered[n].reshape(-1, shard2d[n][1])

    def ffn_views(tag):
        gu8, down = gathered[tag + "_w_gu"], gathered[tag + "_w_down"]
        return gu8.reshape((2, n_sh) + gu8.shape[1:]), gu8, down.reshape(n_sh, -1, down.shape[2])

    wbd = jnp.zeros((SB_WIDTH, SB_WIDTH), BF16)
    for g in range(4):
        sl = slice(g * POOL_GROUP, (g + 1) * POOL_GROUP)
        wbd = wbd.at[sl, sl].set(a_w_group[0, g].astype(BF16))

    xs, mems, tgt = x[0], mem[0], loss_target[0]

    memn = rmsnorm_fwd(mems, mem_norm, "memn")
    h0 = rmsnorm_fwd(xs, a_norm_mix_f, "h0")
    proj = matmul([(h0, rows_full("a_w_in"))], "nn", F32, "proj_a")
    mkv_a = matmul([(memn, rows_full("a_w_mem_kv"))], "nn", BF16, "mkv_a")
    pooled, cat_a = pool_fwd(proj, wbd, a_scale_f, "pool_fwd")
    cat_a = mem_attn_fwd(proj, mkv_a, cat_a, "mem_fwd_a")
    x1 = matmul([(cat_a, rows_full("a_w_out"))], "nn", F32, "out_a", res=xs)
    h1 = rmsnorm_fwd(x1, a_norm_ffn_f, "h1")
    gather_done(gather_2, EARLY_2, h1, "gather_2_wait")
    wgu_a, wgu8_a, wdown_a = ffn_views("a")
    gu_a, act_a = ffn_up(h1, wgu_a, "ffn_up_a")
    x2 = matmul([(act_a, wdown_a)], "nn", F32, "down_a", res=x1, roles=("reduce", "reduce"), count=n_sh)

    hk = rmsnorm_fwd(x2, kv_norm, "hk")
    gather_done(gather_3, EARLY_3, hk, "gather_3_wait")
    wkv = jnp.transpose(gathered["w_kv"], (1, 0, 2)).reshape(shard2d["w_kv"][0], -1)
    kv = matmul([(hk, wkv)], "nn", BF16, "kv")
    h3 = rmsnorm_fwd(x2, b_norm_mix[0], "h3")
    projb = matmul([(h3, rows_full("b_w_q"))], "nn", F32, "proj_b")
    mkv_b = matmul([(memn, rows_full("b_w_mem_kv"))], "nn", BF16, "mkv_b")
    cat_b, csave, got = sb_fwd(projb, kv, "sb_fwd", gather=[shard[n] for n in LATE])
    gathered.update(zip(LATE, got))
    cat_b = mem_attn_fwd(projb, mkv_b, cat_b, "mem_fwd_b")
    x3 = matmul([(cat_b, rows_full("b_w_out"))], "nn", F32, "out_b", res=x2)
    h4 = rmsnorm_fwd(x3, b_norm_ffn[0], "h4")
    wgu_b, wgu8_b, wdown_b = ffn_views("b")
    gu_b, act_b = ffn_up(h4, wgu_b, "ffn_up_b")
    x4 = matmul([(act_b, wdown_b)], "nn", F32, "down_b", res=x3, roles=("reduce", "reduce"), count=n_sh)

    grads = {}
    dx4, dx4_bf, grads["final_norm"], loss_part = loss_head(x4, tgt, final_norm, "loss_head")

    def blocks(n, g):
        return g.reshape((N_DEV,) + tuple(shard2d[n]))

    def ffn_weight_grads(tag, dx_out_bf, h, gu, act, wgu8, wdown):
        fs = wgu8.shape[2]
        dgu8 = ffn_bwd_act(dx_out_bf, wdown, gu, "ffn_bwd_act_" + tag).reshape(N_DEV, s_len, fs)
        d_wdown = matmul([(act, dx_out_bf)], "tn", BF16, "dw_down_" + tag, tm=fs, roles=("batch", None), count=n_sh)
        grads[tag + "_w_gu"] = matmul([(dgu8, h)], "tn", BF16, "dw_gu_" + tag, tm=fs, roles=("batch", None), count=N_DEV)
        grads[tag + "_w_down"] = blocks(tag + "_w_down", d_wdown)
        return dgu8

    def ffn_input_grads(tag, x_in, dx_out, dgu8, wgu8, g_ffn):
        dh = matmul([(dgu8, wgu8)], "nt", F32, "dh_ffn_" + tag, roles=("reduce", "reduce"), count=N_DEV)
        dx_in, dx_in_bf, dgs = rmsnorm_bwd(x_in, dx_out, [(g_ffn, dh)], "norm_bwd_ffn_" + tag)
        return dx_in, dx_in_bf, dgs[0]

    def grads_start(names, small, after, name):
        return exchange_start([grads[n] for n in names], small, after, name)

    def grads_done(handle, names, after, name):
        got = exchange_wait(handle, after, name)
        for n, g in zip(names, got):
            recv[n] = lax.dynamic_update_slice(g, lax.dynamic_slice(grads[n], (me, 0, 0), (1,) + g.shape[1:]), (me, 0, 0))
        return got

    recv = {}
    dgu8_b = ffn_weight_grads("b", dx4_bf, h4, gu_b, act_b, wgu8_b, wdown_b)
    dx3, dx3_bf, grads["b_norm_ffn"] = ffn_input_grads("b", x3, dx4, dgu8_b, wgu8_b, b_norm_ffn[0])

    dcat_b = matmul([(dx3_bf, rows_full("b_w_out"))], "nt", F32, "dcat_b")
    grads["b_w_out"] = blocks("b_w_out", matmul([(cat_b, dx3_bf)], "tn", BF16, "dw_out_b"))
    dprojb, dk, dv, recv_late = sb_bwd(projb, kv, dcat_b, csave, "sb_bwd", scatter=[grads[n] for n in LATE])
    dprojb, dmkv_b = mem_attn_bwd(projb, mkv_b, dcat_b, dprojb, "mem_bwd_b")
    dh3 = matmul([(dprojb, rows_full("b_w_q"))], "nt", F32, "dh3")
    grads["b_w_q"] = blocks("b_w_q", matmul([(h3, dprojb)], "tn", BF16, "dw_q_b"))
    dhk = matmul([(dk, wkv[:, :SB_WIDTH]), (dv, wkv[:, SB_WIDTH:])], "nt", F32, "dhk")
    d_wkv_t = jnp.concatenate([matmul([(dk, hk)], "tn", BF16, "dw_k", tm=SB_WIDTH),
                               matmul([(dv, hk)], "tn", BF16, "dw_v", tm=SB_WIDTH)], axis=0)
    grads["w_kv"] = d_wkv_t.reshape(N_DEV, -1, d_wkv_t.shape[1])
    dx2, dx2_bf, dgs = rmsnorm_bwd(x2, dx3, [(b_norm_mix[0], dh3), (kv_norm, dhk)], "norm_bwd_x2")
    grads["b_norm_mix"], grads["kv_norm"] = dgs
    dmkv_b_bf = dmkv_b.astype(BF16)
    grads["b_w_mem_kv"] = blocks("b_w_mem_kv", matmul([(memn, dmkv_b_bf)], "tn", BF16, "dw_mkv_b"))

    grads_3 = grads_start(EARLY_3, [], grads["b_w_mem_kv"], "grads_3_start")
    dgu8_a = ffn_weight_grads("a", dx2_bf, h1, gu_a, act_a, wgu8_a, wdown_a)
    grads_2 = grads_start(EARLY_2, [], grads_3["token"], "grads_2_start")
    dx1, dx1_bf, grads["a_norm_ffn"] = ffn_input_grads(
        "a", x1, dx2, dgu8_a, wgu8_a, a_norm_ffn_f + grads_2["token"][0, 0])

    dcat_a = matmul([(dx1_bf, rows_full("a_w_out"))], "nt", F32, "dcat_a")
    grads["a_w_out"] = blocks("a_w_out", matmul([(cat_a, dx1_bf)], "tn", BF16, "dw_out_a"))
    dproj, d_wbd, grads["a_scale"] = pool_bwd(dcat_a, pooled, wbd, a_scale_f, "pool_bwd")
    grads["a_w_group"] = jnp.stack(
        [d_wbd[g * POOL_GROUP:(g + 1) * POOL_GROUP, g * POOL_GROUP:(g + 1) * POOL_GROUP] for g in range(4)])
    dproj, dmkv_a = mem_attn_bwd(proj, mkv_a, dcat_a, dproj, "mem_bwd_a")
    dh0 = matmul([(dproj, rows_full("a_w_in"))], "nt", F32, "dh0")
    grads["a_w_in"] = blocks("a_w_in", matmul([(h0, dproj)], "tn", BF16, "dw_in_a"))
    dx, _, dgs = rmsnorm_bwd(xs, dx1, [(a_norm_mix_f, dh0)], "norm_bwd_x")
    grads["a_norm_mix"] = dgs[0]
    dmkv_a_bf = dmkv_a.astype(BF16)
    grads["a_w_mem_kv"] = blocks("a_w_mem_kv", matmul([(memn, dmkv_a_bf)], "tn", BF16, "dw_mkv_a"))
    dmemn = matmul([(dmkv_a_bf, rows_full("a_w_mem_kv")), (dmkv_b_bf, rows_full("b_w_mem_kv"))], "nt", F32, "dmemn")
    _, _, dgs = rmsnorm_bwd(mems, jnp.zeros_like(mems), [(mem_norm, dmemn)], "norm_bwd_mem")
    grads["mem_norm"] = dgs[0]

    small_names = SMALL_REPL + SMALL_SHARDED
    tile_rows = 8
    head_rows = tile_rows * (len(small_names) + 1)

    def small_pack(rows_of, group):
        tiles = []
        for v, col in rows_of:
            v = v.reshape(1, -1)
            if col is None:
                tiles.append(jnp.pad(v, ((0, tile_rows - 1), (0, PACK_COLS - v.shape[1]))))
            else:
                tiles.append(lax.dynamic_update_slice(jnp.zeros((tile_rows, PACK_COLS), F32), v, (0, col)))
        return jnp.concatenate(tiles + [group.reshape(-1, PACK_COLS)], axis=0)

    small_grads = small_pack([(grads[n], None) for n in small_names] + [(loss_part[:, :1], None)], grads["a_w_group"])
    grads_1 = grads_start(EARLY_1, [small_grads], dx, "grads_1_start")

    def small_state(src):
        return small_pack([(src[n], None) for n in SMALL_REPL]
                          + [(src[n], me * src[n].shape[-1]) for n in SMALL_SHARDED]
                          + [(jnp.zeros((1, 1), F32), None)], src["a_w_group"])

    def state2d(src, n):
        a = src[n].reshape(shard2d[n])
        return a.T if n in COL_SHARDED else a

    def adamw(names, after):
        for n in names:
            big_out[n] = reduce_adamw(recv[n], state2d(w, n), state2d(mom, n), state2d(var, n), "adamw_" + n, after=after)
        return big_out[names[-1]][0]

    big_out = {}
    recv.update(zip(LATE, recv_late))
    done = adamw(LATE, grads_1["token"])
    grads_done(grads_3, EARLY_3, done, "grads_3_wait")
    done = adamw(EARLY_3, None)
    grads_done(grads_2, EARLY_2, done, "grads_2_wait")
    done = adamw(EARLY_2, None)
    small_recv = own_slot(grads_done(grads_1, EARLY_1, done, "grads_1_wait")[-1], small_grads)
    adamw(EARLY_1, None)
    small_out = reduce_adamw(small_recv, small_state(w), small_state(mom), small_state(var), "adamw_small")

    def unpack(kind):
        out = {n: (big_out[n][kind].T if n in COL_SHARDED else big_out[n][kind]).reshape(w[n].shape) for n in BIG}
        so = small_out[kind]
        for r, n in enumerate(SMALL_REPL):
            out[n] = so[tile_rows * r, :w[n].shape[-1]].reshape(w[n].shape)
        for r, n in enumerate(SMALL_SHARDED):
            width, row = w[n].shape[-1], tile_rows * (len(SMALL_REPL) + r)
            out[n] = lax.dynamic_slice(so, (row, me * width), (1, width)).reshape(w[n].shape)
        out["a_w_group"] = so[head_rows:head_rows + a_w_group.size // PACK_COLS].reshape(a_w_group.shape)
        return out

    loss = small_out[0][tile_rows * len(small_names), 0]
    results = [loss, dx.reshape(x.shape)]
    for kind in range(4):
        out = unpack(kind)
        results += [out[n] for n in WEIGHTS]
    return tuple(results)
```

```python
import functools

import jax
import jax.numpy as jnp
from jax import lax
from jax.experimental import pallas as pl
from jax.experimental.pallas import tpu as pltpu

F32 = jnp.float32
BF16 = jnp.bfloat16

N_DEV = 8
HEAD_DIM = 64
N_SB_HEADS = 12
SB_WIDTH = N_SB_HEADS * HEAD_DIM
MEM_WIDTH = 4 * HEAD_DIM
POOL_WINDOWS = (2, 4, 8, 16)
POOL_GROUP = SB_WIDTH // 4
POOL_HALO = 16
EPS = 1e-6
QK_SCALE = HEAD_DIM ** -0.5
LOG2_E = 1.4426950408889634
LANES = 128
PACK_COLS = 1024

ADAM_LR = 0.001
ADAM_B1 = 0.9
ADAM_B2 = 0.999
ADAM_EPS = 1e-08
ADAM_WD = 0.01
ADAM_STEP = 10

VMEM_LIMIT = 48 * 1024 * 1024

_NT = (((1,), (1,)), ((), ()))
_TN = (((0,), (0,)), ((), ()))
_NN = (((1,), (0,)), ((), ()))


def _params(*sem):
    return pltpu.CompilerParams(dimension_semantics=sem, vmem_limit_bytes=VMEM_LIMIT)


def _dot(a, b, dims=_NN):
    return lax.dot_general(a, b, dims, preferred_element_type=F32)


def exchange(a2a, ag, name):
    n_a2a, n_arr = len(a2a), len(a2a) + len(ag)

    def body(*refs):
        start, wait = _exchange_ops(refs[:n_arr], refs[n_arr:2 * n_arr], refs[2 * n_arr:], n_a2a)
        start()
        wait()

    return pl.pallas_call(
        body, name=name, out_shape=_exchange_out_shape(a2a, ag),
        in_specs=[_ANY] * n_arr, out_specs=[_ANY] * n_arr,
        scratch_shapes=_exchange_sems(n_arr),
    )(*a2a, *ag)


_ANY = pl.BlockSpec(memory_space=pl.ANY)
_HBM = pl.BlockSpec(memory_space=pltpu.HBM)
_SEM = pl.BlockSpec(memory_space=pltpu.SEMAPHORE)
_DATAFLOW = pltpu.SideEffectType.DATAFLOW_SIDE_EFFECTING


def exchange_start(a2a, ag, after, name):
    n_a2a, n_arr = len(a2a), len(a2a) + len(ag)
    arrays = [pltpu.with_memory_space_constraint(v, pltpu.HBM) for v in list(a2a) + list(ag)]
    shapes = _exchange_out_shape(a2a, ag)
    lands = [pltpu.with_memory_space_constraint(lax.empty(sh.shape, sh.dtype), pltpu.HBM) for sh in shapes]

    def body(*refs):
        ins, outs = refs[:n_arr], refs[n_arr:2 * n_arr]
        send_sems, recv_sems, token = refs[4 * n_arr + 1:4 * n_arr + 4]
        start, _ = _exchange_ops(ins, outs, (send_sems, recv_sems, None), n_a2a, local_copies=False)
        start()
        token[...] = jnp.zeros_like(token)

    outs = pl.pallas_call(
        body, name=name,
        out_shape=[pltpu.HBM(v.shape, v.dtype) for v in arrays] + [pltpu.HBM(sh.shape, sh.dtype) for sh in shapes]
        + [pltpu.SemaphoreType.DMA((n_arr * (N_DEV - 1),)), pltpu.SemaphoreType.DMA((n_arr * (N_DEV - 1),)),
           jax.ShapeDtypeStruct((8, LANES), F32)],
        in_specs=[_HBM] * (2 * n_arr) + [_ANY],
        out_specs=[_HBM] * (2 * n_arr) + [_SEM, _SEM, pl.BlockSpec(memory_space=pltpu.VMEM)],
        input_output_aliases={i: i for i in range(2 * n_arr)},
        compiler_params=pltpu.CompilerParams(has_side_effects=_DATAFLOW),
    )(*arrays, *lands, after)
    return dict(ins=outs[:n_arr], lands=outs[n_arr:2 * n_arr], sems=outs[2 * n_arr:2 * n_arr + 2], n_a2a=n_a2a,
                token=outs[2 * n_arr + 2])


def exchange_wait(handle, after, name):
    ins, lands, sems, n_a2a = handle["ins"], handle["lands"], handle["sems"], handle["n_a2a"]
    n_arr = len(ins)

    def body(*refs):
        in_refs, land_refs = refs[:n_arr], refs[n_arr:2 * n_arr]
        send_sems, recv_sems = refs[2 * n_arr:2 * n_arr + 2]
        _, wait = _exchange_ops(in_refs, land_refs, (send_sems, recv_sems, None), n_a2a, local_copies=False)
        wait()

    outs = pl.pallas_call(
        body, name=name,
        out_shape=[pltpu.HBM(v.shape, v.dtype) for v in list(ins) + list(lands)],
        in_specs=[_HBM] * (2 * n_arr) + [_SEM, _SEM, _ANY], out_specs=[_HBM] * (2 * n_arr),
        input_output_aliases={i: i for i in range(2 * n_arr)},
        compiler_params=pltpu.CompilerParams(has_side_effects=_DATAFLOW),
    )(*ins, *lands, *sems, after)
    return outs[n_arr:]


def _exchange_out_shape(a2a, ag):
    out_shape = [jax.ShapeDtypeStruct(v.shape, v.dtype) for v in a2a]
    return out_shape + [jax.ShapeDtypeStruct((N_DEV,) + v.shape, v.dtype) for v in ag]


def _exchange_sems(n_arr):
    return [pltpu.SemaphoreType.DMA((n_arr * (N_DEV - 1),)), pltpu.SemaphoreType.DMA((n_arr * (N_DEV - 1),)),
            pltpu.SemaphoreType.DMA((n_arr,))]


def _exchange_ops(ins, outs, sems, n_a2a, local_copies=True):
    send_sems, recv_sems, local_sems = sems
    n_arr = len(ins)
    x, y, c = lax.axis_index("x"), lax.axis_index("y"), lax.axis_index("c")
    me = 4 * x + 2 * y + c

    def peer_of(k):
        px = 1 - x if k & 4 else x
        py = 1 - y if k & 2 else y
        pc = 1 - c if k & 1 else c
        return (px, py, pc), 4 * px + 2 * py + pc

    def copy(a, k, landing):
        peer, pid = peer_of(k)
        src = ins[a].at[pid] if a < n_a2a else ins[a]
        return pltpu.make_async_remote_copy(
            src_ref=src, dst_ref=outs[a].at[pid if landing else me], send_sem=send_sems.at[a * (N_DEV - 1) + k - 1],
            recv_sem=recv_sems.at[a * (N_DEV - 1) + k - 1], device_id=peer, device_id_type=pl.DeviceIdType.MESH)

    def local(a):
        src = ins[a].at[me] if a < n_a2a else ins[a]
        return pltpu.make_async_copy(src, outs[a].at[me], local_sems.at[a])

    order = (6, 7, 4, 5, 2, 3, 1)

    def start():
        if local_copies:
            for a in range(n_arr):
                local(a).start()
        for a in range(n_arr):
            for k in order:
                copy(a, k, landing=False).start()

    def wait():
        for a in range(n_arr):
            for k in order:
                copy(a, k, landing=True).wait_recv()
        for a in range(n_arr):
            for k in order:
                copy(a, k, landing=False).wait_send()
        if local_copies:
            for a in range(n_arr):
                local(a).wait()

    return start, wait


def _rows_tile(n, want):
    t = min(want, n)
    assert n % t == 0, (n, t)
    return t


def rmsnorm_fwd(x, g, name):
    s, d = x.shape
    tm = _rows_tile(s, 512)

    def body(x_ref, g_ref, h_ref):
        xv = x_ref[...]
        r = lax.rsqrt(jnp.mean(xv * xv, axis=-1, keepdims=True) + EPS)
        h_ref[...] = ((xv * r) * g_ref[...]).astype(h_ref.dtype)

    return pl.pallas_call(
        body, name=name, grid=(s // tm,),
        in_specs=[pl.BlockSpec((tm, d), lambda i: (i, 0)), pl.BlockSpec((1, d), lambda i: (0, 0))],
        out_specs=pl.BlockSpec((tm, d), lambda i: (i, 0)),
        out_shape=jax.ShapeDtypeStruct((s, d), BF16),
        compiler_params=_params("parallel"),
    )(x, g.reshape(1, d))


def rmsnorm_bwd(x, dres, pairs, name):
    s, d = x.shape
    tm = _rows_tile(s, 256)
    n = len(pairs)

    def body(*refs):
        x_ref, dres_ref = refs[0], refs[1]
        g_refs, dh_refs = refs[2:2 + n], refs[2 + n:2 + 2 * n]
        dx_ref, dxb_ref = refs[2 + 2 * n], refs[3 + 2 * n]
        dg_refs = refs[4 + 2 * n:]
        i = pl.program_id(0)
        xv = x_ref[...]
        r = lax.rsqrt(jnp.mean(xv * xv, axis=-1, keepdims=True) + EPS)
        xhat = xv * r
        acc = dres_ref[...]
        for k in range(n):
            dh = dh_refs[k][...].astype(F32)
            part = jnp.sum(dh * xhat, axis=0, keepdims=True)

            @pl.when(i == 0)
            def _(k=k, part=part):
                dg_refs[k][...] = part

            @pl.when(i > 0)
            def _(k=k, part=part):
                dg_refs[k][...] += part

            dxh = dh * g_refs[k][...]
            acc = acc + r * (dxh - xhat * jnp.mean(dxh * xhat, axis=-1, keepdims=True))
        dx_ref[...] = acc
        dxb_ref[...] = acc.astype(BF16)

    row = pl.BlockSpec((tm, d), lambda i: (i, 0))
    vec = pl.BlockSpec((1, d), lambda i: (0, 0))
    outs = pl.pallas_call(
        body, name=name, grid=(s // tm,),
        in_specs=[row, row] + [vec] * n + [row] * n,
        out_specs=[row, row] + [vec] * n,
        out_shape=[jax.ShapeDtypeStruct((s, d), F32), jax.ShapeDtypeStruct((s, d), BF16)]
        + [jax.ShapeDtypeStruct((1, d), F32)] * n,
        compiler_params=_params("arbitrary"),
    )(x, dres, *[g.reshape(1, d) for g, _ in pairs], *[dh for _, dh in pairs])
    return outs[0], outs[1], outs[2:]


def loss_head(x, tgt, g, name):
    s, d = x.shape
    tm = _rows_tile(s, 256)

    def body(x_ref, t_ref, g_ref, dx_ref, dxb_ref, dg_ref, loss_ref):
        i = pl.program_id(0)
        xv = x_ref[...]
        gv = g_ref[...]
        r = lax.rsqrt(jnp.mean(xv * xv, axis=-1, keepdims=True) + EPS)
        xhat = xv * r
        diff = xhat * gv - t_ref[...]
        part_loss = 0.5 * jnp.sum(jnp.mean(diff * diff, axis=-1, keepdims=True), axis=0, keepdims=True)
        dy = diff * (1.0 / d)
        part_dg = jnp.sum(dy * xhat, axis=0, keepdims=True)

        @pl.when(i == 0)
        def _():
            dg_ref[...] = part_dg
            loss_ref[...] = jnp.broadcast_to(part_loss, loss_ref.shape)

        @pl.when(i > 0)
        def _():
            dg_ref[...] += part_dg
            loss_ref[...] += jnp.broadcast_to(part_loss, loss_ref.shape)

        dxh = dy * gv
        dx = r * (dxh - xhat * jnp.mean(dxh * xhat, axis=-1, keepdims=True))
        dx_ref[...] = dx
        dxb_ref[...] = dx.astype(BF16)

    row = pl.BlockSpec((tm, d), lambda i: (i, 0))
    vec = pl.BlockSpec((1, d), lambda i: (0, 0))
    return pl.pallas_call(
        body, name=name, grid=(s // tm,),
        in_specs=[row, row, vec], out_specs=[row, row, vec, pl.BlockSpec((1, LANES), lambda i: (0, 0))],
        out_shape=[jax.ShapeDtypeStruct((s, d), F32), jax.ShapeDtypeStruct((s, d), BF16),
                   jax.ShapeDtypeStruct((1, d), F32), jax.ShapeDtypeStruct((1, LANES), F32)],
        compiler_params=_params("arbitrary"),
    )(x, tgt, g.reshape(1, d))


def matmul(pairs, mode, out_dtype, name, res=None, tm=512, tn=512, roles=(None, None), count=1):
    a0, b0 = pairs[0]
    m = a0.shape[-1] if mode == "tn" else a0.shape[-2]
    n = b0.shape[-2] if mode == "nt" else b0.shape[-1]
    tm, tn = _rows_tile(m, tm), _rows_tile(n, tn)
    dims = {"nn": _NN, "nt": _NT, "tn": _TN}[mode]
    npairs = len(pairs)
    batched, reducing = "batch" in roles, "reduce" in roles
    assert not (batched and reducing) and (res is None or not batched)
    n_in = 2 * npairs + (res is not None)

    def body(*refs):
        o_ref = refs[n_in]
        acc = None
        for p in range(npairs):
            a_ref, b_ref = refs[2 * p], refs[2 * p + 1]
            for kk in range(count if reducing else 1):
                a = a_ref[kk] if roles[0] == "reduce" else a_ref[...]
                b = b_ref[kk] if roles[1] == "reduce" else b_ref[...]
                d = _dot(a.astype(BF16), b.astype(BF16), dims)
                acc = d if acc is None else acc + d
        if res is not None:
            acc = acc + refs[2 * npairs][...]
        o_ref[...] = acc.astype(o_ref.dtype)

    def spec(arr, role, block, index):
        if arr.ndim == 2:
            return pl.BlockSpec(block, lambda bb, i, j: index(i, j))
        assert role in ("batch", "reduce") and arr.shape[0] == count, (name, role, arr.shape)
        if role == "batch":
            return pl.BlockSpec((None,) + block, lambda bb, i, j: (bb,) + index(i, j))
        return pl.BlockSpec((count,) + block, lambda bb, i, j: (0,) + index(i, j))

    in_specs, args = [], []
    for a, b in pairs:
        if mode == "tn":
            in_specs.append(spec(a, roles[0], (a.shape[-2], tm), lambda i, j: (0, i)))
        else:
            in_specs.append(spec(a, roles[0], (tm, a.shape[-1]), lambda i, j: (i, 0)))
        if mode == "nt":
            in_specs.append(spec(b, roles[1], (tn, b.shape[-1]), lambda i, j: (j, 0)))
        else:
            in_specs.append(spec(b, roles[1], (b.shape[-2], tn), lambda i, j: (0, j)))
        args += [a, b]
    if res is not None:
        in_specs.append(pl.BlockSpec((tm, tn), lambda bb, i, j: (i, j)))
        args.append(res)
    if batched:
        out_spec = pl.BlockSpec((None, tm, tn), lambda bb, i, j: (bb, i, j))
        out_shape = jax.ShapeDtypeStruct((count, m, n), out_dtype)
    else:
        out_spec = pl.BlockSpec((tm, tn), lambda bb, i, j: (i, j))
        out_shape = jax.ShapeDtypeStruct((m, n), out_dtype)
    return pl.pallas_call(
        body, name=name, grid=(count if batched else 1, m // tm, n // tn),
        in_specs=in_specs, out_specs=out_spec, out_shape=out_shape,
        compiler_params=_params("parallel", "parallel", "parallel"),
    )(*args)


def _sigmoid(v):
    return 0.5 * jnp.tanh(0.5 * v) + 0.5


def ffn_up(h, wgu, name):
    s, d = h.shape
    _, n_sh, _, fs = wgu.shape
    tm = _rows_tile(s, 512)

    def body(h_ref, w_ref, gu_ref, act_ref):
        hv = h_ref[...]
        gate = _dot(hv, w_ref[0])
        up = _dot(hv, w_ref[1])
        gu_ref[0] = gate.astype(BF16)
        gu_ref[1] = up.astype(BF16)
        act_ref[...] = ((gate * _sigmoid(gate)) * up).astype(BF16)

    return pl.pallas_call(
        body, name=name, grid=(n_sh, s // tm),
        in_specs=[pl.BlockSpec((tm, d), lambda j, i: (i, 0)), pl.BlockSpec((2, None, d, fs), lambda j, i: (0, j, 0, 0))],
        out_specs=[pl.BlockSpec((2, None, tm, fs), lambda j, i: (0, j, i, 0)),
                   pl.BlockSpec((None, tm, fs), lambda j, i: (j, i, 0))],
        out_shape=[jax.ShapeDtypeStruct((2, n_sh, s, fs), BF16), jax.ShapeDtypeStruct((n_sh, s, fs), BF16)],
        compiler_params=_params("parallel", "parallel"),
    )(h, wgu)


def ffn_bwd_act(dxo, wdown, gu, name):
    s, d = dxo.shape
    n_sh, fs, _ = wdown.shape
    tm = _rows_tile(s, 512)

    def body(dx_ref, w_ref, gu_ref, dgu_ref):
        dact = _dot(dx_ref[...], w_ref[...], _NT)
        gate = gu_ref[0].astype(F32)
        sig = _sigmoid(gate)
        dgu_ref[0] = ((dact * gu_ref[1].astype(F32)) * (sig * (1.0 + gate * (1.0 - sig)))).astype(BF16)
        dgu_ref[1] = (dact * (gate * sig)).astype(BF16)

    tile = pl.BlockSpec((2, None, tm, fs), lambda j, i: (0, j, i, 0))
    return pl.pallas_call(
        body, name=name, grid=(n_sh, s // tm),
        in_specs=[pl.BlockSpec((tm, d), lambda j, i: (i, 0)), pl.BlockSpec((None, fs, d), lambda j, i: (j, 0, 0)), tile],
        out_specs=tile,
        out_shape=jax.ShapeDtypeStruct((2, n_sh, s, fs), BF16),
        compiler_params=_params("parallel", "parallel"),
    )(dxo, wdown, gu)


def _pool_select(col, by_window):
    out = by_window[3]
    for g in (2, 1, 0):
        out = jnp.where(col < (g + 1) * POOL_GROUP, by_window[g], out)
    return out


def _pool_counts(pos):
    return [jnp.minimum(pos + 1, w).astype(F32) for w in POOL_WINDOWS]


def pool_fwd(proj, wbd, scale, name):
    s = proj.shape[0]
    tm = _rows_tile(s, 256)
    nt = s // tm
    width = SB_WIDTH
    per = tm // POOL_HALO

    def body(u_ref, halo_ref, w_ref, sc_ref, pooled_ref, grouped_ref):
        i = pl.program_id(0)
        u = u_ref[...]
        halo = jnp.where(i > 0, halo_ref[...], 0.0)
        ext = jnp.concatenate([halo, u], axis=0)
        sums = []
        acc = ext
        for k in (1, 2, 4, 8):
            acc = acc + pltpu.roll(acc, k, 0)
            sums.append(acc[POOL_HALO:])
        pos = i * tm + lax.broadcasted_iota(jnp.int32, (tm, width), 0)
        col = lax.broadcasted_iota(jnp.int32, (tm, width), 1)
        means = [sm / cnt for sm, cnt in zip(sums, _pool_counts(pos))]
        pooled = (_pool_select(col, means) - u).astype(BF16)
        pooled_ref[...] = pooled
        grouped_ref[...] = (_dot(pooled, w_ref[...]) * sc_ref[...]).astype(BF16)

    row = pl.BlockSpec((tm, width), lambda i: (i, 0))
    return pl.pallas_call(
        body, name=name, grid=(nt,),
        in_specs=[row, pl.BlockSpec((POOL_HALO, width), lambda i: (jnp.maximum(i * per - 1, 0), 0)),
                  pl.BlockSpec((width, width), lambda i: (0, 0)), pl.BlockSpec((1, width), lambda i: (0, 0))],
        out_specs=[row, row],
        out_shape=[jax.ShapeDtypeStruct((s, width), BF16), jax.ShapeDtypeStruct((s, width + MEM_WIDTH), BF16)],
        compiler_params=_params("parallel"),
    )(proj, proj, wbd, scale.reshape(1, width))


def pool_bwd(dcat, pooled, wbd, scale, name):
    s = dcat.shape[0]
    tm = _rows_tile(s, 256)
    nt = s // tm
    width = SB_WIDTH
    per = tm // POOL_HALO
    n_ext = tm + POOL_HALO

    def body(dg_ref, halo_ref, pooled_ref, w_ref, sc_ref, du_ref, dw_ref, dsc_ref):
        i = pl.program_id(0)
        w = w_ref[...]
        sc = sc_ref[...]
        dg = dg_ref[...]
        pooled = pooled_ref[...]
        pg = _dot(pooled, w)
        dpg = (dg * sc).astype(BF16)
        part_sc = jnp.sum(dg * pg, axis=0, keepdims=True)
        part_w = _dot(pooled, dpg, _TN)

        @pl.when(i == 0)
        def _():
            dsc_ref[...] = part_sc
            dw_ref[...] = part_w

        @pl.when(i > 0)
        def _():
            dsc_ref[...] += part_sc
            dw_ref[...] += part_w

        dpooled = _dot(dpg, w, _NT)
        halo_dpg = (jnp.where(i < nt - 1, halo_ref[...], 0.0) * sc).astype(BF16)
        ext = jnp.concatenate([dpooled, _dot(halo_dpg, w, _NT)], axis=0)
        pos = i * tm + lax.broadcasted_iota(jnp.int32, (n_ext, width), 0)
        col = lax.broadcasted_iota(jnp.int32, (tm, width), 1)
        outs = []
        for k_idx, cnt in enumerate(_pool_counts(pos)):
            acc = ext / cnt
            for k in (1, 2, 4, 8)[:k_idx + 1]:
                acc = acc + pltpu.roll(acc, n_ext - k, 0)
            outs.append(acc[:tm])
        du_ref[...] = (_pool_select(col, outs) - dpooled).astype(BF16)

    row = pl.BlockSpec((tm, width), lambda i: (i, 0))
    return pl.pallas_call(
        body, name=name, grid=(nt,),
        in_specs=[row, pl.BlockSpec((POOL_HALO, width), lambda i: (jnp.minimum((i + 1) * per, s // POOL_HALO - 1), 0)),
                  row, pl.BlockSpec((width, width), lambda i: (0, 0)), pl.BlockSpec((1, width), lambda i: (0, 0))],
        out_specs=[row, pl.BlockSpec((width, width), lambda i: (0, 0)), pl.BlockSpec((1, width), lambda i: (0, 0))],
        out_shape=[jax.ShapeDtypeStruct((s, width + MEM_WIDTH), BF16), jax.ShapeDtypeStruct((width, width), F32),
                   jax.ShapeDtypeStruct((1, width), F32)],
        compiler_params=_params("arbitrary"),
    )(dcat, dcat, pooled, wbd, scale.reshape(1, width))


def _head_masks(shape):
    lane = lax.broadcasted_iota(jnp.int32, shape, 1)
    return lane < HEAD_DIM, lane >= HEAD_DIM


def _mem_probs(qh, kp):
    logits = _dot(qh, kp, _NT)
    logits = logits - jnp.max(logits, axis=-1, keepdims=True)
    p = jnp.exp(logits)
    return p / jnp.sum(p, axis=-1, keepdims=True)


def mem_attn_fwd(proj, mkv, cat, name):
    s = proj.shape[0]
    m = mkv.shape[0]
    tm = _rows_tile(s, 256)
    q_blk = SB_WIDTH // MEM_WIDTH

    def body(q_ref, kv_ref, _, o_ref):
        masks = _head_masks((tm, LANES))
        for pr in range(MEM_WIDTH // LANES):
            q = q_ref[:, pr * LANES:(pr + 1) * LANES] * QK_SCALE
            kp = kv_ref[:, pr * LANES:(pr + 1) * LANES]
            vp = kv_ref[:, MEM_WIDTH + pr * LANES:MEM_WIDTH + (pr + 1) * LANES]
            outs = []
            for hm in masks:
                p = _mem_probs(jnp.where(hm, q, 0.0).astype(BF16), kp)
                outs.append(_dot(p.astype(BF16), vp))
            o_ref[:, pr * LANES:(pr + 1) * LANES] = jnp.where(masks[0], outs[0], outs[1]).astype(BF16)

    return pl.pallas_call(
        body, name=name, grid=(s // tm,),
        in_specs=[pl.BlockSpec((tm, MEM_WIDTH), lambda i: (i, q_blk)), pl.BlockSpec((m, 2 * MEM_WIDTH), lambda i: (0, 0)),
                  _ANY],
        out_specs=pl.BlockSpec((tm, MEM_WIDTH), lambda i: (i, q_blk)),
        out_shape=jax.ShapeDtypeStruct(cat.shape, BF16),
        input_output_aliases={2: 0},
        compiler_params=_params("parallel"),
    )(proj, mkv, cat)


def mem_attn_bwd(proj, mkv, dcat, dproj, name):
    s = proj.shape[0]
    m = mkv.shape[0]
    tm = _rows_tile(s, 256)
    q_blk = SB_WIDTH // MEM_WIDTH

    def body(q_ref, kv_ref, do_ref, _, dq_ref, dkv_ref):
        i = pl.program_id(0)

        @pl.when(i == 0)
        def _():
            dkv_ref[...] = jnp.zeros_like(dkv_ref)

        masks = _head_masks((tm, LANES))
        for pr in range(MEM_WIDTH // LANES):
            q = q_ref[:, pr * LANES:(pr + 1) * LANES] * QK_SCALE
            do = do_ref[:, pr * LANES:(pr + 1) * LANES]
            kp = kv_ref[:, pr * LANES:(pr + 1) * LANES]
            vp = kv_ref[:, MEM_WIDTH + pr * LANES:MEM_WIDTH + (pr + 1) * LANES]
            dqs = []
            dk = jnp.zeros((m, LANES), F32)
            dv = jnp.zeros((m, LANES), F32)
            for hm in masks:
                qh = jnp.where(hm, q, 0.0).astype(BF16)
                doh = jnp.where(hm, do, 0.0).astype(BF16)
                p = _mem_probs(qh, kp)
                dp = _dot(doh, vp, _NT)
                ds = (p * (dp - jnp.sum(dp * p, axis=-1, keepdims=True))).astype(BF16)
                dqs.append(_dot(ds, kp))
                dk = dk + _dot(ds, qh, _TN)
                dv = dv + _dot(p.astype(BF16), doh, _TN)
            dq_ref[:, pr * LANES:(pr + 1) * LANES] = (jnp.where(masks[0], dqs[0], dqs[1]) * QK_SCALE).astype(BF16)
            dkv_ref[:, pr * LANES:(pr + 1) * LANES] += dk
            dkv_ref[:, MEM_WIDTH + pr * LANES:MEM_WIDTH + (pr + 1) * LANES] += dv

    return pl.pallas_call(
        body, name=name, grid=(s // tm,),
        in_specs=[pl.BlockSpec((tm, MEM_WIDTH), lambda i: (i, q_blk)), pl.BlockSpec((m, 2 * MEM_WIDTH), lambda i: (0, 0)),
                  pl.BlockSpec((tm, MEM_WIDTH), lambda i: (i, q_blk)), _ANY],
        out_specs=[pl.BlockSpec((tm, MEM_WIDTH), lambda i: (i, q_blk)), pl.BlockSpec((m, 2 * MEM_WIDTH), lambda i: (0, 0))],
        out_shape=[jax.ShapeDtypeStruct(dproj.shape, BF16), jax.ShapeDtypeStruct((m, 2 * MEM_WIDTH), F32)],
        input_output_aliases={3: 0},
        compiler_params=_params("arbitrary"),
    )(proj, mkv, dcat, dproj)


SB_ROWS = 512
SB_KEYS = 512
SB_SUB = LANES


def _sb_tri(later):
    r = lax.broadcasted_iota(jnp.int32, (2 * SB_SUB, 2 * SB_SUB), 0) % SB_SUB
    c = lax.broadcasted_iota(jnp.int32, (2 * SB_SUB, 2 * SB_SUB), 1)
    tri = (r > c) if later else (r < c)
    return (tri | (c >= SB_SUB)).astype(BF16)


def _sb_sums(v, tri, n_sub):
    out = []
    for m in range(n_sub):
        vm = v[:, m * SB_SUB:(m + 1) * SB_SUB]
        hi = vm.astype(BF16)
        lo = (vm - hi.astype(F32)).astype(BF16)
        r = _dot(jnp.concatenate([hi, lo], axis=1), tri)
        out.append((r[:, :SB_SUB], r[:, SB_SUB:]))
    return out


def _sb_logs(nqh, k_blk, valid):
    nz = _dot(nqh, k_blk, _NT)
    log_not = jnp.minimum(nz, 0.0) - jnp.log(1.0 + jnp.exp2(jnp.abs(nz) * -LOG2_E))
    log_beta = log_not - nz
    if valid is not None:
        log_not = jnp.where(valid, log_not, 0.0)
    return log_not, log_beta


def _sb_weights(log_not, log_beta, tri_later, carry_in, valid, n_sub):
    sums = _sb_sums(log_not, tri_later, n_sub)
    run = carry_in
    later = [None] * n_sub
    for m in reversed(range(n_sub)):
        later[m] = sums[m][0] + run
        run = run + sums[m][1]
    w = jnp.exp(log_beta + jnp.concatenate(later, axis=1))
    if valid is not None:
        w = jnp.where(valid, w, 0.0)
    return w, run


def _pad_rows(a, r0):
    return a if r0 == 0 else jnp.concatenate([jnp.zeros((r0,) + a.shape[1:], a.dtype), a], axis=0)


def _sb_diag_parts(nqh, k_blk, tri_later, n_sub):
    tq = nqh.shape[0]
    parts = []
    for m in range(n_sub):
        r0 = m * SB_SUB
        row = lax.broadcasted_iota(jnp.int32, (tq - r0, SB_SUB), 0)
        col = lax.broadcasted_iota(jnp.int32, (tq - r0, SB_SUB), 1)
        valid = (row >= SB_SUB) | (row > col)
        log_not, log_beta = _sb_logs(nqh[r0:], k_blk[r0:r0 + SB_SUB], valid)
        (inside, row_sum), = _sb_sums(log_not, tri_later, 1)
        parts.append([r0, valid, log_beta, inside, row_sum])
    run = jnp.zeros((tq, LANES), F32)
    for part in reversed(parts):
        r0, row_sum = part[0], part[4]
        part[4] = run[r0:]
        run = run + _pad_rows(row_sum, r0)
    return parts, run


def _sb_diag_fwd(nqh, k_blk, v_blk, tri_later, n_sub):
    parts, run = _sb_diag_parts(nqh, k_blk, tri_later, n_sub)
    acc = jnp.zeros((nqh.shape[0], LANES), F32)
    for r0, valid, log_beta, inside, later_blocks in parts:
        w = jnp.where(valid, jnp.exp(log_beta + inside + later_blocks), 0.0)
        acc = acc + _pad_rows(_dot(w.astype(BF16), v_blk[r0:r0 + SB_SUB]), r0)
    return run, acc


def _sb_valid(i, j, tq, tk):
    qpos = i * tq + lax.broadcasted_iota(jnp.int32, (tq, tk), 0)
    kpos = j * tk + lax.broadcasted_iota(jnp.int32, (tq, tk), 1)
    return qpos > kpos


def sb_fwd(projb, kv, name, gather=()):
    s = projb.shape[0]
    tq, tk = _rows_tile(s, SB_ROWS), _rows_tile(s, SB_KEYS)
    n_sub = tk // SB_SUB
    n_pairs = SB_WIDTH // LANES
    assert s // tk <= LANES and tk % tq == 0

    n_g = len(gather)
    n_i = s // tq

    def body(*refs):
        q_ref, k_ref, v_ref = refs[:3]
        o_ref, c_ref = refs[3 + n_g:5 + n_g]
        i = pl.program_id(1)
        if n_g:
            start, wait = _exchange_ops(refs[3:3 + n_g], refs[5 + n_g:5 + 2 * n_g], refs[5 + 2 * n_g:], 0)
            pl.when((pl.program_id(0) == 0) & (i == 0))(start)
        j_diag = (i * tq) // tk
        masks = _head_masks((tq, LANES))
        nq = q_ref[...] * (-QK_SCALE)
        tri = _sb_tri(later=True)
        lane = lax.broadcasted_iota(jnp.int32, (tq, LANES), 1)
        nqh = [jnp.where(hm, nq, 0.0).astype(BF16) for hm in masks]

        def block(j, carry, valid):
            off = pl.multiple_of(j * tk, tk)
            k_blk, v_blk = k_ref[pl.ds(off, tk), :], v_ref[pl.ds(off, tk), :]
            out = []
            for hh in range(2):
                c, acc, cmat = carry[3 * hh:3 * hh + 3]
                log_not, log_beta = _sb_logs(nqh[hh], k_blk, valid)
                w, c_next = _sb_weights(log_not, log_beta, tri, c, valid, n_sub)
                out += [c_next, acc + _dot(w.astype(BF16), v_blk), jnp.where(lane == j, c, cmat)]
            return tuple(out)

        zero = jnp.zeros((tq, LANES), F32)
        if tq == tk:
            off = pl.multiple_of(i * tk, tk)
            k_blk, v_blk = k_ref[pl.ds(off, tk), :], v_ref[pl.ds(off, tk), :]
            carry = ()
            for hh in range(2):
                carry += _sb_diag_fwd(nqh[hh], k_blk, v_blk, tri, n_sub) + (zero,)
        else:
            carry = block(j_diag, (zero,) * 6, _sb_valid(i, j_diag, tq, tk))
        carry = lax.fori_loop(0, j_diag, lambda jj, cr: block(j_diag - 1 - jj, cr, None), carry)
        c_ref[0] = carry[2]
        c_ref[1] = carry[5]
        o_ref[...] = jnp.where(masks[0], carry[1], carry[4]).astype(BF16)
        if n_g:
            pl.when((pl.program_id(0) == n_pairs - 1) & (i == n_i - 1))(wait)

    outs = pl.pallas_call(
        body, name=name, grid=(n_pairs, n_i),
        in_specs=[pl.BlockSpec((tq, LANES), lambda p, i: (i, p)),
                  pl.BlockSpec((s, LANES), lambda p, i: (0, p)),
                  pl.BlockSpec((s, LANES), lambda p, i: (0, n_pairs + p))] + [_ANY] * n_g,
        out_specs=[pl.BlockSpec((tq, LANES), lambda p, i: (i, p)),
                   pl.BlockSpec((2, tq, LANES), lambda p, i: (p, i, 0))] + [_ANY] * n_g,
        out_shape=[jax.ShapeDtypeStruct((s, SB_WIDTH + MEM_WIDTH), BF16),
                   jax.ShapeDtypeStruct((N_SB_HEADS, s, LANES), F32)] + _exchange_out_shape([], gather),
        scratch_shapes=_exchange_sems(n_g) if n_g else [],
        compiler_params=_params("arbitrary", "arbitrary"),
    )(projb, kv, kv, *gather)
    return outs[0], outs[1], outs[2:]


def sb_bwd(projb, kv, dcat, csave, name, scatter=()):
    s = projb.shape[0]
    tq, tk = _rows_tile(s, SB_ROWS), _rows_tile(s, SB_KEYS)
    n_sub = tk // SB_SUB
    n_pairs = SB_WIDTH // LANES
    n_x = len(scatter)
    n_i = s // tq

    def body(*refs):
        q_ref, k_ref, v_ref, do_ref, c_ref = refs[:5]
        dq_ref, dkb_ref, dvb_ref = refs[5 + n_x:8 + n_x]
        dk_ref, dv_ref = refs[8 + 2 * n_x:10 + 2 * n_x]
        i = pl.program_id(1)
        if n_x:
            start, wait = _exchange_ops(refs[5:5 + n_x], refs[8 + n_x:8 + 2 * n_x], refs[10 + 2 * n_x:], n_x)
            pl.when((pl.program_id(0) == 0) & (i == 0))(start)
        j_diag = (i * tq) // tk

        @pl.when(i == 0)
        def _():
            dk_ref[...] = jnp.zeros_like(dk_ref)
            dv_ref[...] = jnp.zeros_like(dv_ref)

        masks = _head_masks((tq, LANES))
        qs = q_ref[...] * QK_SCALE
        do = do_ref[...]
        tri_later = _sb_tri(later=True)
        tri_before = _sb_tri(later=False)
        lane = lax.broadcasted_iota(jnp.int32, (tq, LANES), 1)
        qh = [jnp.where(hm, qs, 0.0).astype(BF16) for hm in masks]
        nqh = [jnp.where(hm, -qs, 0.0).astype(BF16) for hm in masks]
        doh = [jnp.where(hm, do, 0.0).astype(BF16) for hm in masks]

        def block(j, carry, valid):
            off = pl.multiple_of(j * tk, tk)
            k_blk, v_blk = k_ref[pl.ds(off, tk), :], v_ref[pl.ds(off, tk), :]
            out, dk, dv = [], None, None
            for hh in range(2):
                e_before, dq = carry[2 * hh:2 * hh + 2]
                log_not, log_beta = _sb_logs(nqh[hh], k_blk, valid)
                c = jnp.sum(jnp.where(lane == j, c_ref[hh], 0.0), axis=1, keepdims=True)
                w, _ = _sb_weights(log_not, log_beta, tri_later, jnp.broadcast_to(c, (tq, LANES)), valid, n_sub)
                dv_h = _dot(w.astype(BF16), doh[hh], _TN)
                e = w * _dot(doh[hh], v_blk, _NT)
                sums = _sb_sums(e, tri_before, n_sub)
                run = e_before
                e_pre = []
                for m in range(n_sub):
                    e_pre.append(sums[m][0] + run)
                    run = run + sums[m][1]
                dz = e - jnp.exp(log_beta) * (e + jnp.concatenate(e_pre, axis=1))
                if valid is not None:
                    dz = jnp.where(valid, dz, 0.0)
                dz = dz.astype(BF16)
                dk_h = _dot(dz, qh[hh], _TN)
                dk, dv = (dk_h, dv_h) if hh == 0 else (dk + dk_h, dv + dv_h)
                out += [run, dq + _dot(dz, k_blk)]
            dk_ref[pl.ds(off, tk), :] += dk
            dv_ref[pl.ds(off, tk), :] += dv
            return tuple(out)

        zero = jnp.zeros((tq, LANES), F32)
        carry = lax.fori_loop(0, j_diag, lambda j, cr: block(j, cr, None), (zero,) * 4)
        carry = block(j_diag, carry, _sb_valid(i, j_diag, tq, tk))
        dq_ref[...] = (jnp.where(masks[0], carry[1], carry[3]) * QK_SCALE).astype(BF16)

        @pl.when(i == n_i - 1)
        def _():
            dkb_ref[...] = dk_ref[...].astype(BF16)
            dvb_ref[...] = dv_ref[...].astype(BF16)

        if n_x:
            pl.when((pl.program_id(0) == n_pairs - 1) & (i == n_i - 1))(wait)

    tile = pl.BlockSpec((tq, LANES), lambda p, i: (i, p))
    full = pl.BlockSpec((s, LANES), lambda p, i: (0, p))
    outs = pl.pallas_call(
        body, name=name, grid=(n_pairs, n_i),
        in_specs=[tile, full, pl.BlockSpec((s, LANES), lambda p, i: (0, n_pairs + p)), tile,
                  pl.BlockSpec((2, tq, LANES), lambda p, i: (p, i, 0))] + [_ANY] * n_x,
        out_specs=[tile, full, full] + [_ANY] * n_x,
        out_shape=[jax.ShapeDtypeStruct((s, SB_WIDTH + MEM_WIDTH), BF16), jax.ShapeDtypeStruct((s, SB_WIDTH), BF16),
                   jax.ShapeDtypeStruct((s, SB_WIDTH), BF16)] + _exchange_out_shape(scatter, []),
        scratch_shapes=[pltpu.VMEM((s, LANES), F32), pltpu.VMEM((s, LANES), F32)]
        + (_exchange_sems(n_x) if n_x else []),
        compiler_params=_params("arbitrary", "arbitrary"),
    )(projb, kv, kv, dcat, csave, *scatter)
    return outs[0], outs[1], outs[2], outs[3:]


def reduce_adamw(parts, w, m, v, name, after=None):
    n, r, c = parts.shape
    tr = next((t for t in (128, 176) if r % t == 0), r)
    bias1 = 1.0 - ADAM_B1 ** ADAM_STEP
    bias2 = 1.0 - ADAM_B2 ** ADAM_STEP

    def body(p_ref, w_ref, m_ref, v_ref, *rest):
        g_ref, d_ref, nm_ref, nv_ref = rest[-4:]
        g = p_ref[0].astype(F32)
        for k in range(1, n):
            g = g + p_ref[k].astype(F32)
        new_m = ADAM_B1 * m_ref[...] + (1.0 - ADAM_B1) * g
        new_v = ADAM_B2 * v_ref[...] + (1.0 - ADAM_B2) * (g * g)
        m_hat = new_m / bias1
        v_hat = new_v / bias2
        g_ref[...] = g
        d_ref[...] = -ADAM_LR * (m_hat / (jnp.sqrt(v_hat) + ADAM_EPS) + ADAM_WD * w_ref[...])
        nm_ref[...] = new_m
        nv_ref[...] = new_v

    row = pl.BlockSpec((tr, c), lambda i: (i, 0))
    return pl.pallas_call(
        body, name=name, grid=(r // tr,),
        in_specs=[pl.BlockSpec((n, tr, c), lambda i: (0, i, 0)), row, row, row] + ([] if after is None else [_ANY]),
        out_specs=[row, row, row, row],
        out_shape=[jax.ShapeDtypeStruct((r, c), F32)] * 4,
        compiler_params=_params("parallel"),
    )(parts, w, m, v, *(() if after is None else (after,)))


EARLY_0 = ("a_w_in", "a_w_mem_kv")
EARLY_1 = ("a_w_out",)
EARLY_2 = ("a_w_gu", "a_w_down")
EARLY_3 = ("w_kv", "b_w_q", "b_w_mem_kv")
EARLY = EARLY_0 + EARLY_1 + EARLY_2 + EARLY_3
LATE = ("b_w_out", "b_w_gu", "b_w_down")
BIG = EARLY + LATE
COL_SHARDED = ("a_w_gu", "w_kv", "b_w_gu")
SMALL_SHARDED = ("a_norm_mix", "a_scale", "a_norm_ffn")
SMALL_REPL = ("mem_norm", "kv_norm", "b_norm_mix", "b_norm_ffn", "final_norm")
WEIGHTS = ("mem_norm", "a_norm_mix", "a_w_in", "a_w_group", "a_scale", "a_w_mem_kv", "a_w_out", "a_norm_ffn", "a_w_gu",
           "a_w_down", "kv_norm", "w_kv", "b_norm_mix", "b_w_q", "b_w_mem_kv", "b_w_out", "b_norm_ffn", "b_w_gu",
           "b_w_down", "final_norm")


def _pad_row(v):
    v = v.reshape(1, -1)
    return jnp.pad(v, ((0, 0), (0, PACK_COLS - v.shape[1])))


def kernel(x, mem, mem_norm, a_norm_mix, a_w_in, a_w_group, a_scale, a_w_mem_kv, a_w_out, a_norm_ffn, a_w_gu, a_w_down, kv_norm, w_kv, b_norm_mix, b_w_q, b_w_mem_kv, b_w_out, b_norm_ffn, b_w_gu, b_w_down, final_norm, loss_target, m_mem_norm, m_a_norm_mix, m_a_w_in, m_a_w_group, m_a_scale, m_a_w_mem_kv, m_a_w_out, m_a_norm_ffn, m_a_w_gu, m_a_w_down, m_kv_norm, m_w_kv, m_b_norm_mix, m_b_w_q, m_b_w_mem_kv, m_b_w_out, m_b_norm_ffn, m_b_w_gu, m_b_w_down, m_final_norm, v_mem_norm, v_a_norm_mix, v_a_w_in, v_a_w_group, v_a_scale, v_a_w_mem_kv, v_a_w_out, v_a_norm_ffn, v_a_w_gu, v_a_w_down, v_kv_norm, v_w_kv, v_b_norm_mix, v_b_w_q, v_b_w_mem_kv, v_b_w_out, v_b_norm_ffn, v_b_w_gu, v_b_w_down, v_final_norm):
    w = dict(mem_norm=mem_norm, a_norm_mix=a_norm_mix, a_w_in=a_w_in, a_w_group=a_w_group, a_scale=a_scale,
             a_w_mem_kv=a_w_mem_kv, a_w_out=a_w_out, a_norm_ffn=a_norm_ffn, a_w_gu=a_w_gu, a_w_down=a_w_down,
             kv_norm=kv_norm, w_kv=w_kv, b_norm_mix=b_norm_mix, b_w_q=b_w_q, b_w_mem_kv=b_w_mem_kv, b_w_out=b_w_out,
             b_norm_ffn=b_norm_ffn, b_w_gu=b_w_gu, b_w_down=b_w_down, final_norm=final_norm)
    mom = dict(mem_norm=m_mem_norm, a_norm_mix=m_a_norm_mix, a_w_in=m_a_w_in, a_w_group=m_a_w_group, a_scale=m_a_scale,
               a_w_mem_kv=m_a_w_mem_kv, a_w_out=m_a_w_out, a_norm_ffn=m_a_norm_ffn, a_w_gu=m_a_w_gu,
               a_w_down=m_a_w_down, kv_norm=m_kv_norm, w_kv=m_w_kv, b_norm_mix=m_b_norm_mix, b_w_q=m_b_w_q,
               b_w_mem_kv=m_b_w_mem_kv, b_w_out=m_b_w_out, b_norm_ffn=m_b_norm_ffn, b_w_gu=m_b_w_gu,
               b_w_down=m_b_w_down, final_norm=m_final_norm)
    var = dict(mem_norm=v_mem_norm, a_norm_mix=v_a_norm_mix, a_w_in=v_a_w_in, a_w_group=v_a_w_group, a_scale=v_a_scale,
               a_w_mem_kv=v_a_w_mem_kv, a_w_out=v_a_w_out, a_norm_ffn=v_a_norm_ffn, a_w_gu=v_a_w_gu,
               a_w_down=v_a_w_down, kv_norm=v_kv_norm, w_kv=v_w_kv, b_norm_mix=v_b_norm_mix, b_w_q=v_b_w_q,
               b_w_mem_kv=v_b_w_mem_kv, b_w_out=v_b_w_out, b_norm_ffn=v_b_norm_ffn, b_w_gu=v_b_w_gu,
               b_w_down=v_b_w_down, final_norm=v_final_norm)

    me = 4 * lax.axis_index("x") + 2 * lax.axis_index("y") + lax.axis_index("c")
    shard2d = {n: w[n].shape[-2:] for n in BIG}
    shard = {n: w[n].reshape(shard2d[n]).astype(BF16) for n in BIG}
    n_sh = N_DEV // 2
    s_len = x.shape[1]

    small_send = jnp.concatenate([_pad_row(w[n]) for n in SMALL_SHARDED]
                                 + [jnp.zeros((8 - len(SMALL_SHARDED), PACK_COLS), F32)], axis=0)
    def own_slot(land, own):
        return lax.dynamic_update_slice(land, own[None], (me, 0, 0))

    def gather_done(handle, names, after, name):
        got = exchange_wait(handle, after, name)
        gathered.update({n: own_slot(g, shard[n]) for n, g in zip(names, got)})
        return got

    gathered = {}
    gather_0 = exchange_start([], [shard[n] for n in EARLY_0] + [small_send], x, "gather_0_start")
    gather_1 = exchange_start([], [shard[n] for n in EARLY_1], gather_0["token"], "gather_1_start")
    gather_2 = exchange_start([], [shard[n] for n in EARLY_2], gather_1["token"], "gather_2_start")
    gather_3 = exchange_start([], [shard[n] for n in EARLY_3], gather_2["token"], "gather_3_start")
    small_all = own_slot(gather_done(gather_0, EARLY_0, gather_3["token"], "gather_0_wait")[-1], small_send)
    a_norm_mix_f = small_all[:, 0, :a_norm_mix.shape[1]].reshape(-1)
    a_scale_f = small_all[:, 1, :a_scale.shape[1]].reshape(-1)
    a_norm_ffn_f = small_all[:, 2, :a_norm_ffn.shape[1]].reshape(-1)

    def rows_full(n):
        return gathered[n].reshape(-1, shard2d[n][1])

    def ffn_views(tag):
        gu8, down = gathered[tag + "_w_gu"], gathered[tag + "_w_down"]
        return gu8.reshape((2, n_sh) + gu8.shape[1:]), gu8, down.reshape(n_sh, -1, down.shape[2])

    wbd = jnp.zeros((SB_WIDTH, SB_WIDTH), BF16)
    for g in range(4):
        sl = slice(g * POOL_GROUP, (g + 1) * POOL_GROUP)
        wbd = wbd.at[sl, sl].set(a_w_group[0, g].astype(BF16))

    xs, mems, tgt = x[0], mem[0], loss_target[0]

    memn = rmsnorm_fwd(mems, mem_norm, "memn")
    h0 = rmsnorm_fwd(xs, a_norm_mix_f, "h0")
    proj = matmul([(h0, rows_full("a_w_in"))], "nn", F32, "proj_a")
    mkv_a = matmul([(memn, rows_full("a_w_mem_kv"))], "nn", BF16, "mkv_a")
    pooled, cat_a = pool_fwd(proj, wbd, a_scale_f, "pool_fwd")
    cat_a = mem_attn_fwd(proj, mkv_a, cat_a, "mem_fwd_a")
    gather_done(gather_1, EARLY_1, cat_a, "gather_1_wait")
    x1 = matmul([(cat_a, rows_full("a_w_out"))], "nn", F32, "out_a", res=xs)
    h1 = rmsnorm_fwd(x1, a_norm_ffn_f, "h1")
    gather_done(gather_2, EARLY_2, h1, "gather_2_wait")
    wgu_a, wgu8_a, wdown_a = ffn_views("a")
    gu_a, act_a = ffn_up(h1, wgu_a, "ffn_up_a")
    x2 = matmul([(act_a, wdown_a)], "nn", F32, "down_a", res=x1, roles=("reduce", "reduce"), count=n_sh)

    hk = rmsnorm_fwd(x2, kv_norm, "hk")
    gather_done(gather_3, EARLY_3, hk, "gather_3_wait")
    wkv = jnp.transpose(gathered["w_kv"], (1, 0, 2)).reshape(shard2d["w_kv"][0], -1)
    kv = matmul([(hk, wkv)], "nn", BF16, "kv")
    h3 = rmsnorm_fwd(x2, b_norm_mix[0], "h3")
    projb = matmul([(h3, rows_full("b_w_q"))], "nn", F32, "proj_b")
    mkv_b = matmul([(memn, rows_full("b_w_mem_kv"))], "nn", BF16, "mkv_b")
    cat_b, csave, got = sb_fwd(projb, kv, "sb_fwd", gather=[shard[n] for n in LATE])
    gathered.update(zip(LATE, got))
    cat_b = mem_attn_fwd(projb, mkv_b, cat_b, "mem_fwd_b")
    x3 = matmul([(cat_b, rows_full("b_w_out"))], "nn", F32, "out_b", res=x2)
    h4 = rmsnorm_fwd(x3, b_norm_ffn[0], "h4")
    wgu_b, wgu8_b, wdown_b = ffn_views("b")
    gu_b, act_b = ffn_up(h4, wgu_b, "ffn_up_b")
    x4 = matmul([(act_b, wdown_b)], "nn", F32, "down_b", res=x3, roles=("reduce", "reduce"), count=n_sh)

    grads = {}
    dx4, dx4_bf, grads["final_norm"], loss_part = loss_head(x4, tgt, final_norm, "loss_head")

    def blocks(n, g):
        return g.reshape((N_DEV,) + tuple(shard2d[n]))

    def ffn_weight_grads(tag, dx_out_bf, h, gu, act, wgu8, wdown):
        fs = wgu8.shape[2]
        dgu8 = ffn_bwd_act(dx_out_bf, wdown, gu, "ffn_bwd_act_" + tag).reshape(N_DEV, s_len, fs)
        d_wdown = matmul([(act, dx_out_bf)], "tn", BF16, "dw_down_" + tag, tm=fs, roles=("batch", None), count=n_sh)
        grads[tag + "_w_gu"] = matmul([(dgu8, h)], "tn", BF16, "dw_gu_" + tag, tm=fs, roles=("batch", None), count=N_DEV)
        grads[tag + "_w_down"] = blocks(tag + "_w_down", d_wdown)
        return dgu8

    def ffn_input_grads(tag, x_in, dx_out, dgu8, wgu8, g_ffn):
        dh = matmul([(dgu8, wgu8)], "nt", F32, "dh_ffn_" + tag, roles=("reduce", "reduce"), count=N_DEV)
        dx_in, dx_in_bf, dgs = rmsnorm_bwd(x_in, dx_out, [(g_ffn, dh)], "norm_bwd_ffn_" + tag)
        return dx_in, dx_in_bf, dgs[0]

    def grads_start(names, small, after, name):
        return exchange_start([grads[n] for n in names], small, after, name)

    def grads_done(handle, names, after, name):
        got = exchange_wait(handle, after, name)
        for n, g in zip(names, got):
            recv[n] = lax.dynamic_update_slice(g, lax.dynamic_slice(grads[n], (me, 0, 0), (1,) + g.shape[1:]), (me, 0, 0))
        return got

    recv = {}
    dgu8_b = ffn_weight_grads("b", dx4_bf, h4, gu_b, act_b, wgu8_b, wdown_b)
    dx3, dx3_bf, grads["b_norm_ffn"] = ffn_input_grads("b", x3, dx4, dgu8_b, wgu8_b, b_norm_ffn[0])

    dcat_b = matmul([(dx3_bf, rows_full("b_w_out"))], "nt", F32, "dcat_b")
    grads["b_w_out"] = blocks("b_w_out", matmul([(cat_b, dx3_bf)], "tn", BF16, "dw_out_b"))
    dprojb, dk, dv, recv_late = sb_bwd(projb, kv, dcat_b, csave, "sb_bwd", scatter=[grads[n] for n in LATE])
    dprojb, dmkv_b = mem_attn_bwd(projb, mkv_b, dcat_b, dprojb, "mem_bwd_b")
    dh3 = matmul([(dprojb, rows_full("b_w_q"))], "nt", F32, "dh3")
    grads["b_w_q"] = blocks("b_w_q", matmul([(h3, dprojb)], "tn", BF16, "dw_q_b"))
    dhk = matmul([(dk, wkv[:, :SB_WIDTH]), (dv, wkv[:, SB_WIDTH:])], "nt", F32, "dhk")
    d_wkv_t = jnp.concatenate([matmul([(dk, hk)], "tn", BF16, "dw_k", tm=SB_WIDTH),
                               matmul([(dv, hk)], "tn", BF16, "dw_v", tm=SB_WIDTH)], axis=0)
    grads["w_kv"] = d_wkv_t.reshape(N_DEV, -1, d_wkv_t.shape[1])
    dx2, dx2_bf, dgs = rmsnorm_bwd(x2, dx3, [(b_norm_mix[0], dh3), (kv_norm, dhk)], "norm_bwd_x2")
    grads["b_norm_mix"], grads["kv_norm"] = dgs
    dmkv_b_bf = dmkv_b.astype(BF16)
    grads["b_w_mem_kv"] = blocks("b_w_mem_kv", matmul([(memn, dmkv_b_bf)], "tn", BF16, "dw_mkv_b"))

    grads_3 = grads_start(EARLY_3, [], grads["b_w_mem_kv"], "grads_3_start")
    dgu8_a = ffn_weight_grads("a", dx2_bf, h1, gu_a, act_a, wgu8_a, wdown_a)
    grads_2 = grads_start(EARLY_2, [], grads_3["token"], "grads_2_start")
    dx1, dx1_bf, grads["a_norm_ffn"] = ffn_input_grads(
        "a", x1, dx2, dgu8_a, wgu8_a, a_norm_ffn_f + grads_2["token"][0, 0])

    dcat_a = matmul([(dx1_bf, rows_full("a_w_out"))], "nt", F32, "dcat_a")
    grads["a_w_out"] = blocks("a_w_out", matmul([(cat_a, dx1_bf)], "tn", BF16, "dw_out_a"))
    dproj, d_wbd, grads["a_scale"] = pool_bwd(dcat_a, pooled, wbd, a_scale_f, "pool_bwd")
    grads["a_w_group"] = jnp.stack(
        [d_wbd[g * POOL_GROUP:(g + 1) * POOL_GROUP, g * POOL_GROUP:(g + 1) * POOL_GROUP] for g in range(4)])
    group_grads = grads["a_w_group"].reshape(-1, PACK_COLS).astype(BF16)
    grads_1 = exchange_start([grads[n] for n in EARLY_1], [group_grads], grads_2["token"], "grads_1_start")
    dproj, dmkv_a = mem_attn_bwd(proj, mkv_a, dcat_a, dproj, "mem_bwd_a")
    dh0 = matmul([(dproj, rows_full("a_w_in"))], "nt", F32, "dh0")
    grads["a_w_in"] = blocks("a_w_in", matmul([(h0, dproj)], "tn", BF16, "dw_in_a"))
    dx, _, dgs = rmsnorm_bwd(xs, dx1, [(a_norm_mix_f + grads_1["token"][0, 0], dh0)], "norm_bwd_x")
    grads["a_norm_mix"] = dgs[0]
    dmkv_a_bf = dmkv_a.astype(BF16)
    grads["a_w_mem_kv"] = blocks("a_w_mem_kv", matmul([(memn, dmkv_a_bf)], "tn", BF16, "dw_mkv_a"))
    dmemn = matmul([(dmkv_a_bf, rows_full("a_w_mem_kv")), (dmkv_b_bf, rows_full("b_w_mem_kv"))], "nt", F32, "dmemn")
    _, _, dgs = rmsnorm_bwd(mems, jnp.zeros_like(mems), [(mem_norm, dmemn)], "norm_bwd_mem")
    grads["mem_norm"] = dgs[0]

    small_names = SMALL_REPL + SMALL_SHARDED
    head_rows = 16

    def small_pack(rows_of):
        head = jnp.zeros((head_rows, PACK_COLS), F32)
        for r, (v, col) in enumerate(rows_of):
            v = v.reshape(1, -1)
            if col is None:
                head = head + jnp.pad(v, ((r, head_rows - 1 - r), (0, PACK_COLS - v.shape[1])))
            else:
                head = lax.dynamic_update_slice(head, v, (r, col))
        return head

    small_grads = small_pack([(grads[n], None) for n in small_names] + [(loss_part[:, :1], None)])
    grads_0 = grads_start(EARLY_0, [small_grads], dx, "grads_0_start")

    def small_state(src):
        return small_pack([(src[n], None) for n in SMALL_REPL]
                          + [(src[n], me * src[n].shape[-1]) for n in SMALL_SHARDED])

    def state2d(src, n):
        a = src[n].reshape(shard2d[n])
        return a.T if n in COL_SHARDED else a

    def adamw(names, after):
        for n in names:
            big_out[n] = reduce_adamw(recv[n], state2d(w, n), state2d(mom, n), state2d(var, n), "adamw_" + n, after=after)
        return big_out[names[-1]][0]

    big_out = {}
    recv.update(zip(LATE, recv_late))
    done = adamw(LATE, grads_0["token"])
    grads_done(grads_3, EARLY_3, done, "grads_3_wait")
    done = adamw(EARLY_3, None)
    grads_done(grads_2, EARLY_2, done, "grads_2_wait")
    done = adamw(EARLY_2, None)
    group_recv = own_slot(grads_done(grads_1, EARLY_1, done, "grads_1_wait")[-1], group_grads)
    done = adamw(EARLY_1, None)
    group_out = reduce_adamw(group_recv, *[src["a_w_group"].reshape(-1, PACK_COLS) for src in (w, mom, var)], "adamw_group")
    small_recv = own_slot(grads_done(grads_0, EARLY_0, group_out[0], "grads_0_wait")[-1], small_grads)
    adamw(EARLY_0, None)
    small_out = reduce_adamw(small_recv, small_state(w), small_state(mom), small_state(var), "adamw_small")

    def unpack(kind):
        out = {n: (big_out[n][kind].T if n in COL_SHARDED else big_out[n][kind]).reshape(w[n].shape) for n in BIG}
        so = small_out[kind]
        for r, n in enumerate(SMALL_REPL):
            out[n] = so[r, :w[n].shape[-1]].reshape(w[n].shape)
        for r, n in enumerate(SMALL_SHARDED):
            width = w[n].shape[-1]
            out[n] = lax.dynamic_slice(so, (len(SMALL_REPL) + r, me * width), (1, width)).reshape(w[n].shape)
        out["a_w_group"] = group_out[kind].reshape(a_w_group.shape)
        return out

    loss = small_out[0][len(small_names), 0]
    results = [loss, dx.reshape(x.shape)]
    for kind in range(4):
        out = unpack(kind)
        results += [out[n] for n in WEIGHTS]
    return tuple(results)
```

```python
import functools

import jax
import jax.numpy as jnp
from jax import lax
from jax.experimental import pallas as pl
from jax.experimental.pallas import tpu as pltpu

F32 = jnp.float32
BF16 = jnp.bfloat16

N_DEV = 8
HEAD_DIM = 64
N_SB_HEADS = 12
SB_WIDTH = N_SB_HEADS * HEAD_DIM
MEM_WIDTH = 4 * HEAD_DIM
POOL_WINDOWS = (2, 4, 8, 16)
POOL_GROUP = SB_WIDTH // 4
POOL_HALO = 16
EPS = 1e-6
QK_SCALE = HEAD_DIM ** -0.5
LOG2_E = 1.4426950408889634
LANES = 128
PACK_COLS = 1024

ADAM_LR = 0.001
ADAM_B1 = 0.9
ADAM_B2 = 0.999
ADAM_EPS = 1e-08
ADAM_WD = 0.01
ADAM_STEP = 10

VMEM_LIMIT = 48 * 1024 * 1024

_NT = (((1,), (1,)), ((), ()))
_TN = (((0,), (0,)), ((), ()))
_NN = (((1,), (0,)), ((), ()))


def _params(*sem):
    return pltpu.CompilerParams(dimension_semantics=sem, vmem_limit_bytes=VMEM_LIMIT)


def _dot(a, b, dims=_NN):
    return lax.dot_general(a, b, dims, preferred_element_type=F32)


def exchange(a2a, ag, name):
    n_a2a, n_arr = len(a2a), len(a2a) + len(ag)

    def body(*refs):
        start, wait = _exchange_ops(refs[:n_arr], refs[n_arr:2 * n_arr], refs[2 * n_arr:], n_a2a)
        start()
        wait()

    return pl.pallas_call(
        body, name=name, out_shape=_exchange_out_shape(a2a, ag),
        in_specs=[_ANY] * n_arr, out_specs=[_ANY] * n_arr,
        scratch_shapes=_exchange_sems(n_arr),
    )(*a2a, *ag)


_ANY = pl.BlockSpec(memory_space=pl.ANY)
_HBM = pl.BlockSpec(memory_space=pltpu.HBM)
_SEM = pl.BlockSpec(memory_space=pltpu.SEMAPHORE)
_DATAFLOW = pltpu.SideEffectType.DATAFLOW_SIDE_EFFECTING


def exchange_start(a2a, ag, after, name):
    n_a2a, n_arr = len(a2a), len(a2a) + len(ag)
    arrays = [pltpu.with_memory_space_constraint(v, pltpu.HBM) for v in list(a2a) + list(ag)]
    shapes = _exchange_out_shape(a2a, ag)
    lands = [pltpu.with_memory_space_constraint(lax.empty(sh.shape, sh.dtype), pltpu.HBM) for sh in shapes]

    def body(*refs):
        ins, outs = refs[:n_arr], refs[n_arr:2 * n_arr]
        send_sems, recv_sems, token = refs[4 * n_arr + 1:4 * n_arr + 4]
        start, _ = _exchange_ops(ins, outs, (send_sems, recv_sems, None), n_a2a, local_copies=False)
        start()
        token[...] = jnp.zeros_like(token)

    outs = pl.pallas_call(
        body, name=name,
        out_shape=[pltpu.HBM(v.shape, v.dtype) for v in arrays] + [pltpu.HBM(sh.shape, sh.dtype) for sh in shapes]
        + [pltpu.SemaphoreType.DMA((n_arr * (N_DEV - 1),)), pltpu.SemaphoreType.DMA((n_arr * (N_DEV - 1),)),
           jax.ShapeDtypeStruct((8, LANES), F32)],
        in_specs=[_HBM] * (2 * n_arr) + [_ANY],
        out_specs=[_HBM] * (2 * n_arr) + [_SEM, _SEM, pl.BlockSpec(memory_space=pltpu.VMEM)],
        input_output_aliases={i: i for i in range(2 * n_arr)},
        compiler_params=pltpu.CompilerParams(has_side_effects=_DATAFLOW),
    )(*arrays, *lands, after)
    return dict(ins=outs[:n_arr], lands=outs[n_arr:2 * n_arr], sems=outs[2 * n_arr:2 * n_arr + 2], n_a2a=n_a2a,
                token=outs[2 * n_arr + 2])


def exchange_wait(handle, after, name):
    ins, lands, sems, n_a2a = handle["ins"], handle["lands"], handle["sems"], handle["n_a2a"]
    n_arr = len(ins)

    def body(*refs):
        in_refs, land_refs = refs[:n_arr], refs[n_arr:2 * n_arr]
        send_sems, recv_sems = refs[2 * n_arr:2 * n_arr + 2]
        _, wait = _exchange_ops(in_refs, land_refs, (send_sems, recv_sems, None), n_a2a, local_copies=False)
        wait()

    outs = pl.pallas_call(
        body, name=name,
        out_shape=[pltpu.HBM(v.shape, v.dtype) for v in list(ins) + list(lands)],
        in_specs=[_HBM] * (2 * n_arr) + [_SEM, _SEM, _ANY], out_specs=[_HBM] * (2 * n_arr),
        input_output_aliases={i: i for i in range(2 * n_arr)},
        compiler_params=pltpu.CompilerParams(has_side_effects=_DATAFLOW),
    )(*ins, *lands, *sems, after)
    return outs[n_arr:]


def _exchange_out_shape(a2a, ag):
    out_shape = [jax.ShapeDtypeStruct(v.shape, v.dtype) for v in a2a]
    return out_shape + [jax.ShapeDtypeStruct((N_DEV,) + v.shape, v.dtype) for v in ag]


def _exchange_sems(n_arr):
    return [pltpu.SemaphoreType.DMA((n_arr * (N_DEV - 1),)), pltpu.SemaphoreType.DMA((n_arr * (N_DEV - 1),)),
            pltpu.SemaphoreType.DMA((n_arr,))]


def _exchange_ops(ins, outs, sems, n_a2a, local_copies=True):
    send_sems, recv_sems, local_sems = sems
    n_arr = len(ins)
    x, y, c = lax.axis_index("x"), lax.axis_index("y"), lax.axis_index("c")
    me = 4 * x + 2 * y + c

    def peer_of(k):
        px = 1 - x if k & 4 else x
        py = 1 - y if k & 2 else y
        pc = 1 - c if k & 1 else c
        return (px, py, pc), 4 * px + 2 * py + pc

    def copy(a, k, landing):
        peer, pid = peer_of(k)
        src = ins[a].at[pid] if a < n_a2a else ins[a]
        return pltpu.make_async_remote_copy(
            src_ref=src, dst_ref=outs[a].at[pid if landing else me], send_sem=send_sems.at[a * (N_DEV - 1) + k - 1],
            recv_sem=recv_sems.at[a * (N_DEV - 1) + k - 1], device_id=peer, device_id_type=pl.DeviceIdType.MESH)

    def local(a):
        src = ins[a].at[me] if a < n_a2a else ins[a]
        return pltpu.make_async_copy(src, outs[a].at[me], local_sems.at[a])

    order = (6, 7, 4, 5, 2, 3, 1)

    def start():
        if local_copies:
            for a in range(n_arr):
                local(a).start()
        for a in range(n_arr):
            for k in order:
                copy(a, k, landing=False).start()

    def wait():
        for a in range(n_arr):
            for k in order:
                copy(a, k, landing=True).wait_recv()
        for a in range(n_arr):
            for k in order:
                copy(a, k, landing=False).wait_send()
        if local_copies:
            for a in range(n_arr):
                local(a).wait()

    return start, wait


def _rows_tile(n, want):
    t = min(want, n)
    assert n % t == 0, (n, t)
    return t


def rmsnorm_fwd(x, g, name):
    s, d = x.shape
    tm = _rows_tile(s, 512)

    def body(x_ref, g_ref, h_ref):
        xv = x_ref[...]
        r = lax.rsqrt(jnp.mean(xv * xv, axis=-1, keepdims=True) + EPS)
        h_ref[...] = ((xv * r) * g_ref[...]).astype(h_ref.dtype)

    return pl.pallas_call(
        body, name=name, grid=(s // tm,),
        in_specs=[pl.BlockSpec((tm, d), lambda i: (i, 0)), pl.BlockSpec((1, d), lambda i: (0, 0))],
        out_specs=pl.BlockSpec((tm, d), lambda i: (i, 0)),
        out_shape=jax.ShapeDtypeStruct((s, d), BF16),
        compiler_params=_params("parallel"),
    )(x, g.reshape(1, d))


def norm_matmul(x, heads, name):
    s, d = x.shape
    tm = _rows_tile(s, 512)
    n = len(heads)

    def body(*refs):
        x_ref = refs[0]
        xv = x_ref[...]
        xhat = xv * lax.rsqrt(jnp.mean(xv * xv, axis=-1, keepdims=True) + EPS)
        for k in range(n):
            g_ref, w_ref = refs[1 + 2 * k], refs[2 + 2 * k]
            h_ref, y_ref = refs[1 + 2 * n + 2 * k], refs[2 + 2 * n + 2 * k]
            h = (xhat * g_ref[...]).astype(BF16)
            h_ref[...] = h
            y_ref[...] = _dot(h, w_ref[...]).astype(y_ref.dtype)

    row = pl.BlockSpec((tm, d), lambda i: (i, 0))
    in_specs, out_specs, out_shape, args = [row], [], [], [x]
    for g, w_k, dtype in heads:
        in_specs += [pl.BlockSpec((1, d), lambda i: (0, 0)), pl.BlockSpec(w_k.shape, lambda i: (0, 0))]
        args += [g.reshape(1, d), w_k]
        out_specs += [row, pl.BlockSpec((tm, w_k.shape[1]), lambda i: (i, 0))]
        out_shape += [jax.ShapeDtypeStruct((s, d), BF16), jax.ShapeDtypeStruct((s, w_k.shape[1]), dtype)]
    outs = pl.pallas_call(
        body, name=name, grid=(s // tm,), in_specs=in_specs, out_specs=out_specs, out_shape=out_shape,
        compiler_params=_params("parallel"),
    )(*args)
    return [(outs[2 * k], outs[2 * k + 1]) for k in range(n)]


def rmsnorm_bwd(x, dres, pairs, name):
    s, d = x.shape
    tm = _rows_tile(s, 256)
    n = len(pairs)

    def body(*refs):
        x_ref, dres_ref = refs[0], refs[1]
        g_refs, dh_refs = refs[2:2 + n], refs[2 + n:2 + 2 * n]
        dx_ref, dxb_ref = refs[2 + 2 * n], refs[3 + 2 * n]
        dg_refs = refs[4 + 2 * n:]
        i = pl.program_id(0)
        xv = x_ref[...]
        r = lax.rsqrt(jnp.mean(xv * xv, axis=-1, keepdims=True) + EPS)
        xhat = xv * r
        acc = dres_ref[...]
        for k in range(n):
            dh = dh_refs[k][...].astype(F32)
            part = jnp.sum(dh * xhat, axis=0, keepdims=True)

            @pl.when(i == 0)
            def _(k=k, part=part):
                dg_refs[k][...] = part

            @pl.when(i > 0)
            def _(k=k, part=part):
                dg_refs[k][...] += part

            dxh = dh * g_refs[k][...]
            acc = acc + r * (dxh - xhat * jnp.mean(dxh * xhat, axis=-1, keepdims=True))
        dx_ref[...] = acc
        dxb_ref[...] = acc.astype(BF16)

    row = pl.BlockSpec((tm, d), lambda i: (i, 0))
    vec = pl.BlockSpec((1, d), lambda i: (0, 0))
    outs = pl.pallas_call(
        body, name=name, grid=(s // tm,),
        in_specs=[row, row] + [vec] * n + [row] * n,
        out_specs=[row, row] + [vec] * n,
        out_shape=[jax.ShapeDtypeStruct((s, d), F32), jax.ShapeDtypeStruct((s, d), BF16)]
        + [jax.ShapeDtypeStruct((1, d), F32)] * n,
        compiler_params=_params("arbitrary"),
    )(x, dres, *[g.reshape(1, d) for g, _ in pairs], *[dh for _, dh in pairs])
    return outs[0], outs[1], outs[2:]


def loss_head(x, tgt, g, name):
    s, d = x.shape
    tm = _rows_tile(s, 256)

    def body(x_ref, t_ref, g_ref, dx_ref, dxb_ref, dg_ref, loss_ref):
        i = pl.program_id(0)
        xv = x_ref[...]
        gv = g_ref[...]
        r = lax.rsqrt(jnp.mean(xv * xv, axis=-1, keepdims=True) + EPS)
        xhat = xv * r
        diff = xhat * gv - t_ref[...]
        part_loss = 0.5 * jnp.sum(jnp.mean(diff * diff, axis=-1, keepdims=True), axis=0, keepdims=True)
        dy = diff * (1.0 / d)
        part_dg = jnp.sum(dy * xhat, axis=0, keepdims=True)

        @pl.when(i == 0)
        def _():
            dg_ref[...] = part_dg
            loss_ref[...] = jnp.broadcast_to(part_loss, loss_ref.shape)

        @pl.when(i > 0)
        def _():
            dg_ref[...] += part_dg
            loss_ref[...] += jnp.broadcast_to(part_loss, loss_ref.shape)

        dxh = dy * gv
        dx = r * (dxh - xhat * jnp.mean(dxh * xhat, axis=-1, keepdims=True))
        dx_ref[...] = dx
        dxb_ref[...] = dx.astype(BF16)

    row = pl.BlockSpec((tm, d), lambda i: (i, 0))
    vec = pl.BlockSpec((1, d), lambda i: (0, 0))
    return pl.pallas_call(
        body, name=name, grid=(s // tm,),
        in_specs=[row, row, vec], out_specs=[row, row, vec, pl.BlockSpec((1, LANES), lambda i: (0, 0))],
        out_shape=[jax.ShapeDtypeStruct((s, d), F32), jax.ShapeDtypeStruct((s, d), BF16),
                   jax.ShapeDtypeStruct((1, d), F32), jax.ShapeDtypeStruct((1, LANES), F32)],
        compiler_params=_params("arbitrary"),
    )(x, tgt, g.reshape(1, d))


def matmul(pairs, mode, out_dtype, name, res=None, tm=512, tn=None, roles=(None, None), count=1):
    a0, b0 = pairs[0]
    m = a0.shape[-1] if mode == "tn" else a0.shape[-2]
    n = b0.shape[-2] if mode == "nt" else b0.shape[-1]
    if tn is None:
        tn = n if (mode != "tn" and n <= 1024) else 512
    tm, tn = _rows_tile(m, tm), _rows_tile(n, tn)
    dims = {"nn": _NN, "nt": _NT, "tn": _TN}[mode]
    npairs = len(pairs)
    batched, reducing = "batch" in roles, "reduce" in roles
    assert not (batched and reducing) and (res is None or not batched)
    n_in = 2 * npairs + (res is not None)

    def body(*refs):
        o_ref = refs[n_in]
        acc = None
        for p in range(npairs):
            a_ref, b_ref = refs[2 * p], refs[2 * p + 1]
            for kk in range(count if reducing else 1):
                a = a_ref[kk] if roles[0] == "reduce" else a_ref[...]
                b = b_ref[kk] if roles[1] == "reduce" else b_ref[...]
                d = _dot(a.astype(BF16), b.astype(BF16), dims)
                acc = d if acc is None else acc + d
        if res is not None:
            acc = acc + refs[2 * npairs][...]
        o_ref[...] = acc.astype(o_ref.dtype)

    def spec(arr, role, block, index):
        if arr.ndim == 2:
            return pl.BlockSpec(block, lambda bb, i, j: index(i, j))
        assert role in ("batch", "reduce") and arr.shape[0] == count, (name, role, arr.shape)
        if role == "batch":
            return pl.BlockSpec((None,) + block, lambda bb, i, j: (bb,) + index(i, j))
        return pl.BlockSpec((count,) + block, lambda bb, i, j: (0,) + index(i, j))

    in_specs, args = [], []
    for a, b in pairs:
        if mode == "tn":
            in_specs.append(spec(a, roles[0], (a.shape[-2], tm), lambda i, j: (0, i)))
        else:
            in_specs.append(spec(a, roles[0], (tm, a.shape[-1]), lambda i, j: (i, 0)))
        if mode == "nt":
            in_specs.append(spec(b, roles[1], (tn, b.shape[-1]), lambda i, j: (j, 0)))
        else:
            in_specs.append(spec(b, roles[1], (b.shape[-2], tn), lambda i, j: (0, j)))
        args += [a, b]
    if res is not None:
        in_specs.append(pl.BlockSpec((tm, tn), lambda bb, i, j: (i, j)))
        args.append(res)
    if batched:
        out_spec = pl.BlockSpec((None, tm, tn), lambda bb, i, j: (bb, i, j))
        out_shape = jax.ShapeDtypeStruct((count, m, n), out_dtype)
    else:
        out_spec = pl.BlockSpec((tm, tn), lambda bb, i, j: (i, j))
        out_shape = jax.ShapeDtypeStruct((m, n), out_dtype)
    return pl.pallas_call(
        body, name=name, grid=(count if batched else 1, m // tm, n // tn),
        in_specs=in_specs, out_specs=out_spec, out_shape=out_shape,
        compiler_params=_params("parallel", "parallel", "parallel"),
    )(*args)


def _sigmoid(v):
    return 0.5 * jnp.tanh(0.5 * v) + 0.5


def ffn_up(h, wgu, name):
    s, d = h.shape
    _, n_sh, _, fs = wgu.shape
    tm = _rows_tile(s, 512)

    def body(h_ref, w_ref, gu_ref, act_ref):
        hv = h_ref[...]
        gate = _dot(hv, w_ref[0])
        up = _dot(hv, w_ref[1])
        gu_ref[0] = gate.astype(BF16)
        gu_ref[1] = up.astype(BF16)
        act_ref[...] = ((gate * _sigmoid(gate)) * up).astype(BF16)

    return pl.pallas_call(
        body, name=name, grid=(n_sh, s // tm),
        in_specs=[pl.BlockSpec((tm, d), lambda j, i: (i, 0)), pl.BlockSpec((2, None, d, fs), lambda j, i: (0, j, 0, 0))],
        out_specs=[pl.BlockSpec((2, None, tm, fs), lambda j, i: (0, j, i, 0)),
                   pl.BlockSpec((None, tm, fs), lambda j, i: (j, i, 0))],
        out_shape=[jax.ShapeDtypeStruct((2, n_sh, s, fs), BF16), jax.ShapeDtypeStruct((n_sh, s, fs), BF16)],
        compiler_params=_params("parallel", "parallel"),
    )(h, wgu)


def ffn_bwd_act(dxo, wdown, gu, name):
    s, d = dxo.shape
    n_sh, fs, _ = wdown.shape
    tm = _rows_tile(s, 512)

    def body(dx_ref, w_ref, gu_ref, dgu_ref):
        dact = _dot(dx_ref[...], w_ref[...], _NT)
        gate = gu_ref[0].astype(F32)
        sig = _sigmoid(gate)
        dgu_ref[0] = ((dact * gu_ref[1].astype(F32)) * (sig * (1.0 + gate * (1.0 - sig)))).astype(BF16)
        dgu_ref[1] = (dact * (gate * sig)).astype(BF16)

    tile = pl.BlockSpec((2, None, tm, fs), lambda j, i: (0, j, i, 0))
    return pl.pallas_call(
        body, name=name, grid=(n_sh, s // tm),
        in_specs=[pl.BlockSpec((tm, d), lambda j, i: (i, 0)), pl.BlockSpec((None, fs, d), lambda j, i: (j, 0, 0)), tile],
        out_specs=tile,
        out_shape=jax.ShapeDtypeStruct((2, n_sh, s, fs), BF16),
        compiler_params=_params("parallel", "parallel"),
    )(dxo, wdown, gu)


def _pool_select(col, by_window):
    out = by_window[3]
    for g in (2, 1, 0):
        out = jnp.where(col < (g + 1) * POOL_GROUP, by_window[g], out)
    return out


def _pool_counts(pos):
    return [jnp.minimum(pos + 1, w).astype(F32) for w in POOL_WINDOWS]


def pool_fwd(proj, wbd, scale, name):
    s = proj.shape[0]
    tm = _rows_tile(s, 256)
    nt = s // tm
    width = SB_WIDTH
    per = tm // POOL_HALO

    def body(u_ref, halo_ref, w_ref, sc_ref, pooled_ref, grouped_ref):
        i = pl.program_id(0)
        u = u_ref[...]
        halo = jnp.where(i > 0, halo_ref[...], 0.0)
        ext = jnp.concatenate([halo, u], axis=0)
        sums = []
        acc = ext
        for k in (1, 2, 4, 8):
            acc = acc + pltpu.roll(acc, k, 0)
            sums.append(acc[POOL_HALO:])
        pos = i * tm + lax.broadcasted_iota(jnp.int32, (tm, width), 0)
        col = lax.broadcasted_iota(jnp.int32, (tm, width), 1)
        means = [sm / cnt for sm, cnt in zip(sums, _pool_counts(pos))]
        pooled = (_pool_select(col, means) - u).astype(BF16)
        pooled_ref[...] = pooled
        grouped_ref[...] = (_dot(pooled, w_ref[...]) * sc_ref[...]).astype(BF16)

    row = pl.BlockSpec((tm, width), lambda i: (i, 0))
    return pl.pallas_call(
        body, name=name, grid=(nt,),
        in_specs=[row, pl.BlockSpec((POOL_HALO, width), lambda i: (jnp.maximum(i * per - 1, 0), 0)),
                  pl.BlockSpec((width, width), lambda i: (0, 0)), pl.BlockSpec((1, width), lambda i: (0, 0))],
        out_specs=[row, row],
        out_shape=[jax.ShapeDtypeStruct((s, width), BF16), jax.ShapeDtypeStruct((s, width + MEM_WIDTH), BF16)],
        compiler_params=_params("parallel"),
    )(proj, proj, wbd, scale.reshape(1, width))


def pool_bwd(dcat, pooled, wbd, scale, name):
    s = dcat.shape[0]
    tm = _rows_tile(s, 256)
    nt = s // tm
    width = SB_WIDTH
    per = tm // POOL_HALO
    n_ext = tm + POOL_HALO

    def body(dg_ref, halo_ref, pooled_ref, w_ref, sc_ref, du_ref, dw_ref, dsc_ref):
        i = pl.program_id(0)
        w = w_ref[...]
        sc = sc_ref[...]
        dg = dg_ref[...]
        pooled = pooled_ref[...]
        pg = _dot(pooled, w)
        dpg = (dg * sc).astype(BF16)
        part_sc = jnp.sum(dg * pg, axis=0, keepdims=True)
        part_w = _dot(pooled, dpg, _TN)

        @pl.when(i == 0)
        def _():
            dsc_ref[...] = part_sc
            dw_ref[...] = part_w

        @pl.when(i > 0)
        def _():
            dsc_ref[...] += part_sc
            dw_ref[...] += part_w

        dpooled = _dot(dpg, w, _NT)
        halo_dpg = (jnp.where(i < nt - 1, halo_ref[...], 0.0) * sc).astype(BF16)
        ext = jnp.concatenate([dpooled, _dot(halo_dpg, w, _NT)], axis=0)
        pos = i * tm + lax.broadcasted_iota(jnp.int32, (n_ext, width), 0)
        col = lax.broadcasted_iota(jnp.int32, (tm, width), 1)
        outs = []
        for k_idx, cnt in enumerate(_pool_counts(pos)):
            acc = ext / cnt
            for k in (1, 2, 4, 8)[:k_idx + 1]:
                acc = acc + pltpu.roll(acc, n_ext - k, 0)
            outs.append(acc[:tm])
        du_ref[...] = (_pool_select(col, outs) - dpooled).astype(BF16)

    row = pl.BlockSpec((tm, width), lambda i: (i, 0))
    return pl.pallas_call(
        body, name=name, grid=(nt,),
        in_specs=[row, pl.BlockSpec((POOL_HALO, width), lambda i: (jnp.minimum((i + 1) * per, s // POOL_HALO - 1), 0)),
                  row, pl.BlockSpec((width, width), lambda i: (0, 0)), pl.BlockSpec((1, width), lambda i: (0, 0))],
        out_specs=[row, pl.BlockSpec((width, width), lambda i: (0, 0)), pl.BlockSpec((1, width), lambda i: (0, 0))],
        out_shape=[jax.ShapeDtypeStruct((s, width + MEM_WIDTH), BF16), jax.ShapeDtypeStruct((width, width), F32),
                   jax.ShapeDtypeStruct((1, width), F32)],
        compiler_params=_params("arbitrary"),
    )(dcat, dcat, pooled, wbd, scale.reshape(1, width))


def _head_masks(shape):
    lane = lax.broadcasted_iota(jnp.int32, shape, 1)
    return lane < HEAD_DIM, lane >= HEAD_DIM


def _mem_probs(qh, kp):
    logits = _dot(qh, kp, _NT)
    logits = logits - jnp.max(logits, axis=-1, keepdims=True)
    p = jnp.exp(logits)
    return p / jnp.sum(p, axis=-1, keepdims=True)


def mem_attn_fwd(proj, mkv, cat, name):
    s = proj.shape[0]
    m = mkv.shape[0]
    tm = _rows_tile(s, 256)
    q_blk = SB_WIDTH // MEM_WIDTH

    def body(q_ref, kv_ref, _, o_ref):
        masks = _head_masks((tm, LANES))
        for pr in range(MEM_WIDTH // LANES):
            q = q_ref[:, pr * LANES:(pr + 1) * LANES] * QK_SCALE
            kp = kv_ref[:, pr * LANES:(pr + 1) * LANES]
            vp = kv_ref[:, MEM_WIDTH + pr * LANES:MEM_WIDTH + (pr + 1) * LANES]
            outs = []
            for hm in masks:
                p = _mem_probs(jnp.where(hm, q, 0.0).astype(BF16), kp)
                outs.append(_dot(p.astype(BF16), vp))
            o_ref[:, pr * LANES:(pr + 1) * LANES] = jnp.where(masks[0], outs[0], outs[1]).astype(BF16)

    return pl.pallas_call(
        body, name=name, grid=(s // tm,),
        in_specs=[pl.BlockSpec((tm, MEM_WIDTH), lambda i: (i, q_blk)), pl.BlockSpec((m, 2 * MEM_WIDTH), lambda i: (0, 0)),
                  _ANY],
        out_specs=pl.BlockSpec((tm, MEM_WIDTH), lambda i: (i, q_blk)),
        out_shape=jax.ShapeDtypeStruct(cat.shape, BF16),
        input_output_aliases={2: 0},
        compiler_params=_params("parallel"),
    )(proj, mkv, cat)


def mem_attn_bwd(proj, mkv, dcat, dproj, name):
    s = proj.shape[0]
    m = mkv.shape[0]
    tm = _rows_tile(s, 256)
    q_blk = SB_WIDTH // MEM_WIDTH

    def body(q_ref, kv_ref, do_ref, _, dq_ref, dkv_ref):
        i = pl.program_id(0)

        @pl.when(i == 0)
        def _():
            dkv_ref[...] = jnp.zeros_like(dkv_ref)

        masks = _head_masks((tm, LANES))
        for pr in range(MEM_WIDTH // LANES):
            q = q_ref[:, pr * LANES:(pr + 1) * LANES] * QK_SCALE
            do = do_ref[:, pr * LANES:(pr + 1) * LANES]
            kp = kv_ref[:, pr * LANES:(pr + 1) * LANES]
            vp = kv_ref[:, MEM_WIDTH + pr * LANES:MEM_WIDTH + (pr + 1) * LANES]
            dqs = []
            dk = jnp.zeros((m, LANES), F32)
            dv = jnp.zeros((m, LANES), F32)
            for hm in masks:
                qh = jnp.where(hm, q, 0.0).astype(BF16)
                doh = jnp.where(hm, do, 0.0).astype(BF16)
                p = _mem_probs(qh, kp)
                dp = _dot(doh, vp, _NT)
                ds = (p * (dp - jnp.sum(dp * p, axis=-1, keepdims=True))).astype(BF16)
                dqs.append(_dot(ds, kp))
                dk = dk + _dot(ds, qh, _TN)
                dv = dv + _dot(p.astype(BF16), doh, _TN)
            dq_ref[:, pr * LANES:(pr + 1) * LANES] = (jnp.where(masks[0], dqs[0], dqs[1]) * QK_SCALE).astype(BF16)
            dkv_ref[:, pr * LANES:(pr + 1) * LANES] += dk
            dkv_ref[:, MEM_WIDTH + pr * LANES:MEM_WIDTH + (pr + 1) * LANES] += dv

    return pl.pallas_call(
        body, name=name, grid=(s // tm,),
        in_specs=[pl.BlockSpec((tm, MEM_WIDTH), lambda i: (i, q_blk)), pl.BlockSpec((m, 2 * MEM_WIDTH), lambda i: (0, 0)),
                  pl.BlockSpec((tm, MEM_WIDTH), lambda i: (i, q_blk)), _ANY],
        out_specs=[pl.BlockSpec((tm, MEM_WIDTH), lambda i: (i, q_blk)), pl.BlockSpec((m, 2 * MEM_WIDTH), lambda i: (0, 0))],
        out_shape=[jax.ShapeDtypeStruct(dproj.shape, BF16), jax.ShapeDtypeStruct((m, 2 * MEM_WIDTH), F32)],
        input_output_aliases={3: 0},
        compiler_params=_params("arbitrary"),
    )(proj, mkv, dcat, dproj)


SB_ROWS = 512
SB_KEYS = 512
SB_SUB = LANES


def _sb_tri(later):
    r = lax.broadcasted_iota(jnp.int32, (2 * SB_SUB, 2 * SB_SUB), 0) % SB_SUB
    c = lax.broadcasted_iota(jnp.int32, (2 * SB_SUB, 2 * SB_SUB), 1)
    tri = (r > c) if later else (r < c)
    return (tri | (c >= SB_SUB)).astype(BF16)


def _sb_sums(v, tri, n_sub):
    out = []
    for m in range(n_sub):
        vm = v[:, m * SB_SUB:(m + 1) * SB_SUB]
        hi = vm.astype(BF16)
        lo = (vm - hi.astype(F32)).astype(BF16)
        r = _dot(jnp.concatenate([hi, lo], axis=1), tri)
        out.append((r[:, :SB_SUB], r[:, SB_SUB:]))
    return out


def _sb_tri_pair_before():
    r = lax.broadcasted_iota(jnp.int32, (2 * SB_SUB, 2 * SB_SUB), 0)
    c = lax.broadcasted_iota(jnp.int32, (2 * SB_SUB, 2 * SB_SUB), 1)
    return ((r // SB_SUB == c // SB_SUB) & (r % SB_SUB < c % SB_SUB)).astype(BF16)


def _sb_prefix_pairs(e, tri_pair, n_sub):
    out = []
    for m in range(0, n_sub, 2):
        pair = e[:, m * SB_SUB:(m + 2) * SB_SUB]
        r = _dot(pair.astype(BF16), tri_pair)
        for h in range(2):
            em = pair[:, h * SB_SUB:(h + 1) * SB_SUB]
            out.append((r[:, h * SB_SUB:(h + 1) * SB_SUB], jnp.sum(em, axis=1, keepdims=True)))
    return out


def _sb_logs(nqh, k_blk, valid):
    nz = _dot(nqh, k_blk, _NT)
    log_not = jnp.minimum(nz, 0.0) - jnp.log(1.0 + jnp.exp2(jnp.abs(nz) * -LOG2_E))
    log_beta = log_not - nz
    if valid is not None:
        log_not = jnp.where(valid, log_not, 0.0)
    return log_not, log_beta


def _sb_weights(log_not, log_beta, tri_later, carry_in, valid, n_sub):
    sums = _sb_sums(log_not, tri_later, n_sub)
    run = carry_in
    later = [None] * n_sub
    for m in reversed(range(n_sub)):
        later[m] = sums[m][0] + run
        run = run + sums[m][1]
    w = jnp.exp(log_beta + jnp.concatenate(later, axis=1))
    if valid is not None:
        w = jnp.where(valid, w, 0.0)
    return w, run


def _pad_rows(a, r0):
    return a if r0 == 0 else jnp.concatenate([jnp.zeros((r0,) + a.shape[1:], a.dtype), a], axis=0)


def _sb_diag_parts(nqh, k_blk, tri_later, n_sub):
    tq = nqh.shape[0]
    parts = []
    for m in range(n_sub):
        r0 = m * SB_SUB
        row = lax.broadcasted_iota(jnp.int32, (tq - r0, SB_SUB), 0)
        col = lax.broadcasted_iota(jnp.int32, (tq - r0, SB_SUB), 1)
        valid = (row >= SB_SUB) | (row > col)
        log_not, log_beta = _sb_logs(nqh[r0:], k_blk[r0:r0 + SB_SUB], valid)
        (inside, row_sum), = _sb_sums(log_not, tri_later, 1)
        parts.append([r0, valid, log_beta, inside, row_sum])
    run = jnp.zeros((tq, LANES), F32)
    for part in reversed(parts):
        r0, row_sum = part[0], part[4]
        part[4] = run[r0:]
        run = run + _pad_rows(row_sum, r0)
    return parts, run


def _sb_diag_fwd(nqh, k_blk, v_blk, tri_later, n_sub):
    parts, run = _sb_diag_parts(nqh, k_blk, tri_later, n_sub)
    acc = jnp.zeros((nqh.shape[0], LANES), F32)
    for r0, valid, log_beta, inside, later_blocks in parts:
        w = jnp.where(valid, jnp.exp(log_beta + inside + later_blocks), 0.0)
        acc = acc + _pad_rows(_dot(w.astype(BF16), v_blk[r0:r0 + SB_SUB]), r0)
    return run, acc


def _sb_valid(i, j, tq, tk):
    qpos = i * tq + lax.broadcasted_iota(jnp.int32, (tq, tk), 0)
    kpos = j * tk + lax.broadcasted_iota(jnp.int32, (tq, tk), 1)
    return qpos > kpos


def sb_fwd(projb, kv, name, gather=()):
    s = projb.shape[0]
    tq, tk = _rows_tile(s, SB_ROWS), _rows_tile(s, SB_KEYS)
    n_sub = tk // SB_SUB
    n_pairs = SB_WIDTH // LANES
    assert s // tk <= LANES and tk % tq == 0

    n_g = len(gather)
    n_i = s // tq

    def body(*refs):
        q_ref, k_ref, v_ref = refs[:3]
        o_ref, c_ref = refs[3 + n_g:5 + n_g]
        i = pl.program_id(1)
        if n_g:
            start, wait = _exchange_ops(refs[3:3 + n_g], refs[5 + n_g:5 + 2 * n_g], refs[5 + 2 * n_g:], 0)
            pl.when((pl.program_id(0) == 0) & (i == 0))(start)
        j_diag = (i * tq) // tk
        masks = _head_masks((tq, LANES))
        nq = q_ref[...] * (-QK_SCALE)
        tri = _sb_tri(later=True)
        lane = lax.broadcasted_iota(jnp.int32, (tq, LANES), 1)
        nqh = [jnp.where(hm, nq, 0.0).astype(BF16) for hm in masks]

        def block(j, carry, valid):
            off = pl.multiple_of(j * tk, tk)
            k_blk, v_blk = k_ref[pl.ds(off, tk), :], v_ref[pl.ds(off, tk), :]
            out = []
            for hh in range(2):
                c, acc, cmat = carry[3 * hh:3 * hh + 3]
                log_not, log_beta = _sb_logs(nqh[hh], k_blk, valid)
                w, c_next = _sb_weights(log_not, log_beta, tri, c, valid, n_sub)
                out += [c_next, acc + _dot(w.astype(BF16), v_blk), jnp.where(lane == j, c, cmat)]
            return tuple(out)

        zero = jnp.zeros((tq, LANES), F32)
        if tq == tk:
            off = pl.multiple_of(i * tk, tk)
            k_blk, v_blk = k_ref[pl.ds(off, tk), :], v_ref[pl.ds(off, tk), :]
            carry = ()
            for hh in range(2):
                carry += _sb_diag_fwd(nqh[hh], k_blk, v_blk, tri, n_sub) + (zero,)
        else:
            carry = block(j_diag, (zero,) * 6, _sb_valid(i, j_diag, tq, tk))
        carry = lax.fori_loop(0, j_diag, lambda jj, cr: block(j_diag - 1 - jj, cr, None), carry)
        c_ref[0] = carry[2]
        c_ref[1] = carry[5]
        o_ref[...] = jnp.where(masks[0], carry[1], carry[4]).astype(BF16)
        if n_g:
            pl.when((pl.program_id(0) == n_pairs - 1) & (i == n_i - 1))(wait)

    outs = pl.pallas_call(
        body, name=name, grid=(n_pairs, n_i),
        in_specs=[pl.BlockSpec((tq, LANES), lambda p, i: (i, p)),
                  pl.BlockSpec((s, LANES), lambda p, i: (0, p)),
                  pl.BlockSpec((s, LANES), lambda p, i: (0, n_pairs + p))] + [_ANY] * n_g,
        out_specs=[pl.BlockSpec((tq, LANES), lambda p, i: (i, p)),
                   pl.BlockSpec((2, tq, LANES), lambda p, i: (p, i, 0))] + [_ANY] * n_g,
        out_shape=[jax.ShapeDtypeStruct((s, SB_WIDTH + MEM_WIDTH), BF16),
                   jax.ShapeDtypeStruct((N_SB_HEADS, s, LANES), F32)] + _exchange_out_shape([], gather),
        scratch_shapes=_exchange_sems(n_g) if n_g else [],
        compiler_params=_params("arbitrary", "arbitrary"),
    )(projb, kv, kv, *gather)
    return outs[0], outs[1], outs[2:]


def sb_bwd(projb, kv, dcat, csave, name, scatter=()):
    s = projb.shape[0]
    tq, tk = _rows_tile(s, SB_ROWS), _rows_tile(s, SB_KEYS)
    n_sub = tk // SB_SUB
    n_pairs = SB_WIDTH // LANES
    n_x = len(scatter)
    n_i = s // tq

    def body(*refs):
        q_ref, k_ref, v_ref, do_ref, c_ref = refs[:5]
        dq_ref, dkb_ref, dvb_ref = refs[5 + n_x:8 + n_x]
        dk_ref, dv_ref = refs[8 + 2 * n_x:10 + 2 * n_x]
        i = pl.program_id(1)
        if n_x:
            start, wait = _exchange_ops(refs[5:5 + n_x], refs[8 + n_x:8 + 2 * n_x], refs[10 + 2 * n_x:], n_x)
            pl.when((pl.program_id(0) == 0) & (i == 0))(start)
        j_diag = (i * tq) // tk

        @pl.when(i == 0)
        def _():
            dk_ref[...] = jnp.zeros_like(dk_ref)
            dv_ref[...] = jnp.zeros_like(dv_ref)

        masks = _head_masks((tq, LANES))
        qs = q_ref[...] * QK_SCALE
        do = do_ref[...]
        tri_later = _sb_tri(later=True)
        tri_before = _sb_tri_pair_before()
        lane = lax.broadcasted_iota(jnp.int32, (tq, LANES), 1)
        qh = [jnp.where(hm, qs, 0.0).astype(BF16) for hm in masks]
        nqh = [jnp.where(hm, -qs, 0.0).astype(BF16) for hm in masks]
        doh = [jnp.where(hm, do, 0.0).astype(BF16) for hm in masks]

        def block(j, carry, valid):
            off = pl.multiple_of(j * tk, tk)
            k_blk, v_blk = k_ref[pl.ds(off, tk), :], v_ref[pl.ds(off, tk), :]
            out, dk, dv = [], None, None
            for hh in range(2):
                e_before, dq = carry[2 * hh:2 * hh + 2]
                log_not, log_beta = _sb_logs(nqh[hh], k_blk, valid)
                c = jnp.sum(jnp.where(lane == j, c_ref[hh], 0.0), axis=1, keepdims=True)
                w, _ = _sb_weights(log_not, log_beta, tri_later, jnp.broadcast_to(c, (tq, LANES)), valid, n_sub)
                dv_h = _dot(w.astype(BF16), doh[hh], _TN)
                e = w * _dot(doh[hh], v_blk, _NT)
                sums = _sb_prefix_pairs(e, tri_before, n_sub)
                run = e_before
                e_pre = []
                for m in range(n_sub):
                    e_pre.append(sums[m][0] + run)
                    run = run + sums[m][1]
                dz = e - jnp.exp(log_beta) * (e + jnp.concatenate(e_pre, axis=1))
                if valid is not None:
                    dz = jnp.where(valid, dz, 0.0)
                dz = dz.astype(BF16)
                dk_h = _dot(dz, qh[hh], _TN)
                dk, dv = (dk_h, dv_h) if hh == 0 else (dk + dk_h, dv + dv_h)
                out += [run, dq + _dot(dz, k_blk)]
            dk_ref[pl.ds(off, tk), :] += dk
            dv_ref[pl.ds(off, tk), :] += dv
            return tuple(out)

        zero = jnp.zeros((tq, LANES), F32)
        carry = lax.fori_loop(0, j_diag, lambda j, cr: block(j, cr, None), (zero,) * 4)
        carry = block(j_diag, carry, _sb_valid(i, j_diag, tq, tk))
        dq_ref[...] = (jnp.where(masks[0], carry[1], carry[3]) * QK_SCALE).astype(BF16)

        @pl.when(i == n_i - 1)
        def _():
            dkb_ref[...] = dk_ref[...].astype(BF16)
            dvb_ref[...] = dv_ref[...].astype(BF16)

        if n_x:
            pl.when((pl.program_id(0) == n_pairs - 1) & (i == n_i - 1))(wait)

    tile = pl.BlockSpec((tq, LANES), lambda p, i: (i, p))
    full = pl.BlockSpec((s, LANES), lambda p, i: (0, p))
    outs = pl.pallas_call(
        body, name=name, grid=(n_pairs, n_i),
        in_specs=[tile, full, pl.BlockSpec((s, LANES), lambda p, i: (0, n_pairs + p)), tile,
                  pl.BlockSpec((2, tq, LANES), lambda p, i: (p, i, 0))] + [_ANY] * n_x,
        out_specs=[tile, full, full] + [_ANY] * n_x,
        out_shape=[jax.ShapeDtypeStruct((s, SB_WIDTH + MEM_WIDTH), BF16), jax.ShapeDtypeStruct((s, SB_WIDTH), BF16),
                   jax.ShapeDtypeStruct((s, SB_WIDTH), BF16)] + _exchange_out_shape(scatter, []),
        scratch_shapes=[pltpu.VMEM((s, LANES), F32), pltpu.VMEM((s, LANES), F32)]
        + (_exchange_sems(n_x) if n_x else []),
        compiler_params=_params("arbitrary", "arbitrary"),
    )(projb, kv, kv, dcat, csave, *scatter)
    return outs[0], outs[1], outs[2], outs[3:]


def reduce_adamw(parts, w, m, v, name, after=None):
    n, r, c = parts.shape
    tr = next((t for t in (128, 176) if r % t == 0), r)
    bias1 = 1.0 - ADAM_B1 ** ADAM_STEP
    bias2 = 1.0 - ADAM_B2 ** ADAM_STEP

    def body(p_ref, w_ref, m_ref, v_ref, *rest):
        g_ref, d_ref, nm_ref, nv_ref = rest[-4:]
        g = p_ref[0].astype(F32)
        for k in range(1, n):
            g = g + p_ref[k].astype(F32)
        new_m = ADAM_B1 * m_ref[...] + (1.0 - ADAM_B1) * g
        new_v = ADAM_B2 * v_ref[...] + (1.0 - ADAM_B2) * (g * g)
        m_hat = new_m / bias1
        v_hat = new_v / bias2
        g_ref[...] = g
        d_ref[...] = -ADAM_LR * (m_hat / (jnp.sqrt(v_hat) + ADAM_EPS) + ADAM_WD * w_ref[...])
        nm_ref[...] = new_m
        nv_ref[...] = new_v

    row = pl.BlockSpec((tr, c), lambda i: (i, 0))
    return pl.pallas_call(
        body, name=name, grid=(r // tr,),
        in_specs=[pl.BlockSpec((n, tr, c), lambda i: (0, i, 0)), row, row, row] + ([] if after is None else [_ANY]),
        out_specs=[row, row, row, row],
        out_shape=[jax.ShapeDtypeStruct((r, c), F32)] * 4,
        compiler_params=_params("parallel"),
    )(parts, w, m, v, *(() if after is None else (after,)))


EARLY_0 = ("a_w_in", "a_w_mem_kv")
EARLY_1 = ("a_w_out",)
EARLY_2 = ("a_w_gu", "a_w_down")
EARLY_3 = ("w_kv", "b_w_q", "b_w_mem_kv")
EARLY = EARLY_0 + EARLY_1 + EARLY_2 + EARLY_3
LATE = ("b_w_out", "b_w_gu", "b_w_down")
BIG = EARLY + LATE
COL_SHARDED = ("a_w_gu", "w_kv", "b_w_gu")
SMALL_SHARDED = ("a_norm_mix", "a_scale", "a_norm_ffn")
SMALL_REPL = ("mem_norm", "kv_norm", "b_norm_mix", "b_norm_ffn", "final_norm")
WEIGHTS = ("mem_norm", "a_norm_mix", "a_w_in", "a_w_group", "a_scale", "a_w_mem_kv", "a_w_out", "a_norm_ffn", "a_w_gu",
           "a_w_down", "kv_norm", "w_kv", "b_norm_mix", "b_w_q", "b_w_mem_kv", "b_w_out", "b_norm_ffn", "b_w_gu",
           "b_w_down", "final_norm")


def _pad_row(v):
    v = v.reshape(1, -1)
    return jnp.pad(v, ((0, 0), (0, PACK_COLS - v.shape[1])))


def kernel(x, mem, mem_norm, a_norm_mix, a_w_in, a_w_group, a_scale, a_w_mem_kv, a_w_out, a_norm_ffn, a_w_gu, a_w_down, kv_norm, w_kv, b_norm_mix, b_w_q, b_w_mem_kv, b_w_out, b_norm_ffn, b_w_gu, b_w_down, final_norm, loss_target, m_mem_norm, m_a_norm_mix, m_a_w_in, m_a_w_group, m_a_scale, m_a_w_mem_kv, m_a_w_out, m_a_norm_ffn, m_a_w_gu, m_a_w_down, m_kv_norm, m_w_kv, m_b_norm_mix, m_b_w_q, m_b_w_mem_kv, m_b_w_out, m_b_norm_ffn, m_b_w_gu, m_b_w_down, m_final_norm, v_mem_norm, v_a_norm_mix, v_a_w_in, v_a_w_group, v_a_scale, v_a_w_mem_kv, v_a_w_out, v_a_norm_ffn, v_a_w_gu, v_a_w_down, v_kv_norm, v_w_kv, v_b_norm_mix, v_b_w_q, v_b_w_mem_kv, v_b_w_out, v_b_norm_ffn, v_b_w_gu, v_b_w_down, v_final_norm):
    w = dict(mem_norm=mem_norm, a_norm_mix=a_norm_mix, a_w_in=a_w_in, a_w_group=a_w_group, a_scale=a_scale,
             a_w_mem_kv=a_w_mem_kv, a_w_out=a_w_out, a_norm_ffn=a_norm_ffn, a_w_gu=a_w_gu, a_w_down=a_w_down,
             kv_norm=kv_norm, w_kv=w_kv, b_norm_mix=b_norm_mix, b_w_q=b_w_q, b_w_mem_kv=b_w_mem_kv, b_w_out=b_w_out,
             b_norm_ffn=b_norm_ffn, b_w_gu=b_w_gu, b_w_down=b_w_down, final_norm=final_norm)
    mom = dict(mem_norm=m_mem_norm, a_norm_mix=m_a_norm_mix, a_w_in=m_a_w_in, a_w_group=m_a_w_group, a_scale=m_a_scale,
               a_w_mem_kv=m_a_w_mem_kv, a_w_out=m_a_w_out, a_norm_ffn=m_a_norm_ffn, a_w_gu=m_a_w_gu,
               a_w_down=m_a_w_down, kv_norm=m_kv_norm, w_kv=m_w_kv, b_norm_mix=m_b_norm_mix, b_w_q=m_b_w_q,
               b_w_mem_kv=m_b_w_mem_kv, b_w_out=m_b_w_out, b_norm_ffn=m_b_norm_ffn, b_w_gu=m_b_w_gu,
               b_w_down=m_b_w_down, final_norm=m_final_norm)
    var = dict(mem_norm=v_mem_norm, a_norm_mix=v_a_norm_mix, a_w_in=v_a_w_in, a_w_group=v_a_w_group, a_scale=v_a_scale,
               a_w_mem_kv=v_a_w_mem_kv, a_w_out=v_a_w_out, a_norm_ffn=v_a_norm_ffn, a_w_gu=v_a_w_gu,
               a_w_down=v_a_w_down, kv_norm=v_kv_norm, w_kv=v_w_kv, b_norm_mix=v_b_norm_mix, b_w_q=v_b_w_q,
               b_w_mem_kv=v_b_w_mem_kv, b_w_out=v_b_w_out, b_norm_ffn=v_b_norm_ffn, b_w_gu=v_b_w_gu,
               b_w_down=v_b_w_down, final_norm=v_final_norm)

    me = 4 * lax.axis_index("x") + 2 * lax.axis_index("y") + lax.axis_index("c")
    shard2d = {n: w[n].shape[-2:] for n in BIG}
    shard = {n: w[n].reshape(shard2d[n]).astype(BF16) for n in BIG}
    n_sh = N_DEV // 2
    s_len = x.shape[1]

    small_send = jnp.concatenate([_pad_row(w[n]) for n in SMALL_SHARDED]
                                 + [jnp.zeros((8 - len(SMALL_SHARDED), PACK_COLS), F32)], axis=0)
    def own_slot(land, own):
        return lax.dynamic_update_slice(land, own[None], (me, 0, 0))

    def gather_done(handle, names, after, name):
        got = exchange_wait(handle, after, name)
        gathered.update({n: own_slot(g, shard[n]) for n, g in zip(names, got)})
        return got

    gathered = {}
    gather_0 = exchange_start([], [shard[n] for n in EARLY_0] + [small_send], x, "gather_0_start")
    gather_1 = exchange_start([], [shard[n] for n in EARLY_1], gather_0["token"], "gather_1_start")
    gather_2 = exchange_start([], [shard["a_w_gu"]], gather_1["token"], "gather_2_start")
    gather_2b = exchange_start([], [shard["a_w_down"]], gather_2["token"], "gather_2b_start")
    gather_3 = exchange_start([], [shard[n] for n in EARLY_3], gather_2b["token"], "gather_3_start")
    small_all = own_slot(gather_done(gather_0, EARLY_0, gather_3["token"], "gather_0_wait")[-1], small_send)
    a_norm_mix_f = small_all[:, 0, :a_norm_mix.shape[1]].reshape(-1)
    a_scale_f = small_all[:, 1, :a_scale.shape[1]].reshape(-1)
    a_norm_ffn_f = small_all[:, 2, :a_norm_ffn.shape[1]].reshape(-1)

    def rows_full(n):
        return gathered[n].reshape(-1, shard2d[n][1])

    def gu_views(tag):
        gu8 = gathered[tag + "_w_gu"]
        return gu8.reshape((2, n_sh) + gu8.shape[1:]), gu8

    def down_view(tag):
        down = gathered[tag + "_w_down"]
        return down.reshape(n_sh, -1, down.shape[2])

    wbd = jnp.zeros((SB_WIDTH, SB_WIDTH), BF16)
    for g in range(4):
        sl = slice(g * POOL_GROUP, (g + 1) * POOL_GROUP)
        wbd = wbd.at[sl, sl].set(a_w_group[0, g].astype(BF16))

    xs, mems, tgt = x[0], mem[0], loss_target[0]

    memn = rmsnorm_fwd(mems, mem_norm, "memn")
    (h0, proj), = norm_matmul(xs, [(a_norm_mix_f, rows_full("a_w_in"), F32)], "proj_a")
    mkv_a = matmul([(memn, rows_full("a_w_mem_kv"))], "nn", BF16, "mkv_a")
    pooled, cat_a = pool_fwd(proj, wbd, a_scale_f, "pool_fwd")
    cat_a = mem_attn_fwd(proj, mkv_a, cat_a, "mem_fwd_a")
    gather_done(gather_1, EARLY_1, cat_a, "gather_1_wait")
    x1 = matmul([(cat_a, rows_full("a_w_out"))], "nn", F32, "out_a", res=xs)
    h1 = rmsnorm_fwd(x1, a_norm_ffn_f, "h1")
    gather_done(gather_2, ("a_w_gu",), h1, "gather_2_wait")
    wgu_a, wgu8_a = gu_views("a")
    gu_a, act_a = ffn_up(h1, wgu_a, "ffn_up_a")
    gather_done(gather_2b, ("a_w_down",), act_a, "gather_2b_wait")
    wdown_a = down_view("a")
    x2 = matmul([(act_a, wdown_a)], "nn", F32, "down_a", res=x1, roles=("reduce", "reduce"), count=n_sh)

    gather_done(gather_3, EARLY_3, x2, "gather_3_wait")
    wkv = jnp.transpose(gathered["w_kv"], (1, 0, 2)).reshape(shard2d["w_kv"][0], -1)
    (hk, kv), (h3, projb) = norm_matmul(
        x2, [(kv_norm, wkv, BF16), (b_norm_mix[0], rows_full("b_w_q"), F32)], "kv_proj_b")
    mkv_b = matmul([(memn, rows_full("b_w_mem_kv"))], "nn", BF16, "mkv_b")
    cat_b, csave, got = sb_fwd(projb, kv, "sb_fwd", gather=[shard[n] for n in LATE])
    gathered.update(zip(LATE, got))
    cat_b = mem_attn_fwd(projb, mkv_b, cat_b, "mem_fwd_b")
    x3 = matmul([(cat_b, rows_full("b_w_out"))], "nn", F32, "out_b", res=x2)
    h4 = rmsnorm_fwd(x3, b_norm_ffn[0], "h4")
    (wgu_b, wgu8_b), wdown_b = gu_views("b"), down_view("b")
    gu_b, act_b = ffn_up(h4, wgu_b, "ffn_up_b")
    x4 = matmul([(act_b, wdown_b)], "nn", F32, "down_b", res=x3, roles=("reduce", "reduce"), count=n_sh)

    grads = {}
    dx4, dx4_bf, grads["final_norm"], loss_part = loss_head(x4, tgt, final_norm, "loss_head")

    def blocks(n, g):
        return g.reshape((N_DEV,) + tuple(shard2d[n]))

    def ffn_weight_grads(tag, dx_out_bf, h, gu, act, wgu8, wdown):
        fs = wgu8.shape[2]
        dgu8 = ffn_bwd_act(dx_out_bf, wdown, gu, "ffn_bwd_act_" + tag).reshape(N_DEV, s_len, fs)
        d_wdown = matmul([(act, dx_out_bf)], "tn", BF16, "dw_down_" + tag, tm=fs, roles=("batch", None), count=n_sh)
        grads[tag + "_w_gu"] = matmul([(dgu8, h)], "tn", BF16, "dw_gu_" + tag, tm=fs, roles=("batch", None), count=N_DEV)
        grads[tag + "_w_down"] = blocks(tag + "_w_down", d_wdown)
        return dgu8

    def ffn_input_grads(tag, x_in, dx_out, dgu8, wgu8, g_ffn):
        dh = matmul([(dgu8, wgu8)], "nt", F32, "dh_ffn_" + tag, tn=512, roles=("reduce", "reduce"), count=N_DEV)
        dx_in, dx_in_bf, dgs = rmsnorm_bwd(x_in, dx_out, [(g_ffn, dh)], "norm_bwd_ffn_" + tag)
        return dx_in, dx_in_bf, dgs[0]

    def grads_start(names, small, after, name):
        return exchange_start([grads[n] for n in names], small, after, name)

    def grads_done(handle, names, after, name):
        got = exchange_wait(handle, after, name)
        for n, g in zip(names, got):
            recv[n] = lax.dynamic_update_slice(g, lax.dynamic_slice(grads[n], (me, 0, 0), (1,) + g.shape[1:]), (me, 0, 0))
        return got

    recv = {}
    dgu8_b = ffn_weight_grads("b", dx4_bf, h4, gu_b, act_b, wgu8_b, wdown_b)
    dx3, dx3_bf, grads["b_norm_ffn"] = ffn_input_grads("b", x3, dx4, dgu8_b, wgu8_b, b_norm_ffn[0])

    dcat_b = matmul([(dx3_bf, rows_full("b_w_out"))], "nt", F32, "dcat_b")
    grads["b_w_out"] = blocks("b_w_out", matmul([(cat_b, dx3_bf)], "tn", BF16, "dw_out_b"))
    dprojb, dk, dv, recv_late = sb_bwd(projb, kv, dcat_b, csave, "sb_bwd", scatter=[grads[n] for n in LATE])
    dprojb, dmkv_b = mem_attn_bwd(projb, mkv_b, dcat_b, dprojb, "mem_bwd_b")
    dh3 = matmul([(dprojb, rows_full("b_w_q"))], "nt", F32, "dh3")
    grads["b_w_q"] = blocks("b_w_q", matmul([(h3, dprojb)], "tn", BF16, "dw_q_b"))
    dhk = matmul([(dk, wkv[:, :SB_WIDTH]), (dv, wkv[:, SB_WIDTH:])], "nt", F32, "dhk")
    d_wkv_t = jnp.concatenate([matmul([(dk, hk)], "tn", BF16, "dw_k", tm=SB_WIDTH),
                               matmul([(dv, hk)], "tn", BF16, "dw_v", tm=SB_WIDTH)], axis=0)
    grads["w_kv"] = d_wkv_t.reshape(N_DEV, -1, d_wkv_t.shape[1])
    dx2, dx2_bf, dgs = rmsnorm_bwd(x2, dx3, [(b_norm_mix[0], dh3), (kv_norm, dhk)], "norm_bwd_x2")
    grads["b_norm_mix"], grads["kv_norm"] = dgs
    dmkv_b_bf = dmkv_b.astype(BF16)
    grads["b_w_mem_kv"] = blocks("b_w_mem_kv", matmul([(memn, dmkv_b_bf)], "tn", BF16, "dw_mkv_b"))

    grads_3 = grads_start(EARLY_3, [], grads["b_w_mem_kv"], "grads_3_start")
    dgu8_a = ffn_weight_grads("a", dx2_bf, h1, gu_a, act_a, wgu8_a, wdown_a)
    grads_2 = grads_start(EARLY_2, [], grads_3["token"], "grads_2_start")
    dx1, dx1_bf, grads["a_norm_ffn"] = ffn_input_grads(
        "a", x1, dx2, dgu8_a, wgu8_a, a_norm_ffn_f + grads_2["token"][0, 0])

    dcat_a = matmul([(dx1_bf, rows_full("a_w_out"))], "nt", F32, "dcat_a")
    grads["a_w_out"] = blocks("a_w_out", matmul([(cat_a, dx1_bf)], "tn", BF16, "dw_out_a"))
    dproj, d_wbd, grads["a_scale"] = pool_bwd(dcat_a, pooled, wbd, a_scale_f, "pool_bwd")
    grads["a_w_group"] = jnp.stack(
        [d_wbd[g * POOL_GROUP:(g + 1) * POOL_GROUP, g * POOL_GROUP:(g + 1) * POOL_GROUP] for g in range(4)])
    group_grads = grads["a_w_group"].reshape(-1, PACK_COLS).astype(BF16)
    grads_1 = exchange_start([grads[n] for n in EARLY_1], [group_grads], grads_2["token"], "grads_1_start")
    dproj, dmkv_a = mem_attn_bwd(proj, mkv_a, dcat_a, dproj, "mem_bwd_a")
    dh0 = matmul([(dproj, rows_full("a_w_in"))], "nt", F32, "dh0")
    grads["a_w_in"] = blocks("a_w_in", matmul([(h0, dproj)], "tn", BF16, "dw_in_a"))
    dx, _, dgs = rmsnorm_bwd(xs, dx1, [(a_norm_mix_f + grads_1["token"][0, 0], dh0)], "norm_bwd_x")
    grads["a_norm_mix"] = dgs[0]
    dmkv_a_bf = dmkv_a.astype(BF16)
    grads["a_w_mem_kv"] = blocks("a_w_mem_kv", matmul([(memn, dmkv_a_bf)], "tn", BF16, "dw_mkv_a"))
    dmemn = matmul([(dmkv_a_bf, rows_full("a_w_mem_kv")), (dmkv_b_bf, rows_full("b_w_mem_kv"))], "nt", F32, "dmemn")
    _, _, dgs = rmsnorm_bwd(mems, jnp.zeros_like(mems), [(mem_norm, dmemn)], "norm_bwd_mem")
    grads["mem_norm"] = dgs[0]

    small_names = SMALL_REPL + SMALL_SHARDED
    head_rows = 16

    def small_pack(rows_of):
        head = jnp.zeros((head_rows, PACK_COLS), F32)
        for r, (v, col) in enumerate(rows_of):
            v = v.reshape(1, -1)
            if col is None:
                head = head + jnp.pad(v, ((r, head_rows - 1 - r), (0, PACK_COLS - v.shape[1])))
            else:
                head = lax.dynamic_update_slice(head, v, (r, col))
        return head

    small_grads = small_pack([(grads[n], None) for n in small_names] + [(loss_part[:, :1], None)])
    grads_0 = grads_start(EARLY_0, [small_grads], dx, "grads_0_start")

    def small_state(src):
        return small_pack([(src[n], None) for n in SMALL_REPL]
                          + [(src[n], me * src[n].shape[-1]) for n in SMALL_SHARDED])

    def state2d(src, n):
        a = src[n].reshape(shard2d[n])
        return a.T if n in COL_SHARDED else a

    def adamw(names, after):
        for n in names:
            big_out[n] = reduce_adamw(recv[n], state2d(w, n), state2d(mom, n), state2d(var, n), "adamw_" + n, after=after)
        return big_out[names[-1]][0]

    big_out = {}
    recv.update(zip(LATE, recv_late))
    done = adamw(LATE, grads_0["token"])
    grads_done(grads_3, EARLY_3, done, "grads_3_wait")
    done = adamw(EARLY_3, None)
    grads_done(grads_2, EARLY_2, done, "grads_2_wait")
    done = adamw(EARLY_2, None)
    group_recv = own_slot(grads_done(grads_1, EARLY_1, done, "grads_1_wait")[-1], group_grads)
    done = adamw(EARLY_1, None)
    group_out = reduce_adamw(group_recv, *[src["a_w_group"].reshape(-1, PACK_COLS) for src in (w, mom, var)], "adamw_group")
    small_recv = own_slot(grads_done(grads_0, EARLY_0, group_out[0], "grads_0_wait")[-1], small_grads)
    adamw(EARLY_0, None)
    small_out = reduce_adamw(small_recv, small_state(w), small_state(mom), small_state(var), "adamw_small")

    def unpack(kind):
        out = {n: (big_out[n][kind].T if n in COL_SHARDED else big_out[n][kind]).reshape(w[n].shape) for n in BIG}
        so = small_out[kind]
        for r, n in enumerate(SMALL_REPL):
            out[n] = so[r, :w[n].shape[-1]].reshape(w[n].shape)
        for r, n in enumerate(SMALL_SHARDED):
            width = w[n].shape[-1]
            out[n] = lax.dynamic_slice(so, (len(SMALL_REPL) + r, me * width), (1, width)).reshape(w[n].shape)
        out["a_w_group"] = group_out[kind].reshape(a_w_group.shape)
        return out

    loss = small_out[0][len(small_names), 0]
    results = [loss, dx.reshape(x.shape)]
    for kind in range(4):
        out = unpack(kind)
        results += [out[n] for n in WEIGHTS]
    return tuple(results)
```

```python
import functools

import jax
import jax.numpy as jnp
from jax import lax
from jax.experimental import pallas as pl
from jax.experimental.pallas import tpu as pltpu

F32 = jnp.float32
BF16 = jnp.bfloat16

N_DEV = 8
HEAD_DIM = 64
N_SB_HEADS = 12
SB_WIDTH = N_SB_HEADS * HEAD_DIM
MEM_WIDTH = 4 * HEAD_DIM
POOL_WINDOWS = (2, 4, 8, 16)
POOL_GROUP = SB_WIDTH // 4
POOL_HALO = 16
EPS = 1e-6
QK_SCALE = HEAD_DIM ** -0.5
LOG2_E = 1.4426950408889634
LANES = 128
PACK_COLS = 1024

ADAM_LR = 0.001
ADAM_B1 = 0.9
ADAM_B2 = 0.999
ADAM_EPS = 1e-08
ADAM_WD = 0.01
ADAM_STEP = 10

VMEM_LIMIT = 48 * 1024 * 1024

_NT = (((1,), (1,)), ((), ()))
_TN = (((0,), (0,)), ((), ()))
_NN = (((1,), (0,)), ((), ()))


def _params(*sem):
    return pltpu.CompilerParams(dimension_semantics=sem, vmem_limit_bytes=VMEM_LIMIT)


def _dot(a, b, dims=_NN):
    return lax.dot_general(a, b, dims, preferred_element_type=F32)


def exchange(a2a, ag, name):
    n_a2a, n_arr = len(a2a), len(a2a) + len(ag)

    def body(*refs):
        start, wait = _exchange_ops(refs[:n_arr], refs[n_arr:2 * n_arr], refs[2 * n_arr:], n_a2a)
        start()
        wait()

    return pl.pallas_call(
        body, name=name, out_shape=_exchange_out_shape(a2a, ag),
        in_specs=[_ANY] * n_arr, out_specs=[_ANY] * n_arr,
        scratch_shapes=_exchange_sems(n_arr),
    )(*a2a, *ag)


_ANY = pl.BlockSpec(memory_space=pl.ANY)
_HBM = pl.BlockSpec(memory_space=pltpu.HBM)
_SEM = pl.BlockSpec(memory_space=pltpu.SEMAPHORE)
_DATAFLOW = pltpu.SideEffectType.DATAFLOW_SIDE_EFFECTING


def exchange_start(a2a, ag, after, name):
    n_a2a, n_arr = len(a2a), len(a2a) + len(ag)
    arrays = [pltpu.with_memory_space_constraint(v, pltpu.HBM) for v in list(a2a) + list(ag)]
    shapes = _exchange_out_shape(a2a, ag)
    lands = [pltpu.with_memory_space_constraint(lax.empty(sh.shape, sh.dtype), pltpu.HBM) for sh in shapes]

    def body(*refs):
        ins, outs = refs[:n_arr], refs[n_arr:2 * n_arr]
        send_sems, recv_sems, token = refs[4 * n_arr + 1:4 * n_arr + 4]
        start, _ = _exchange_ops(ins, outs, (send_sems, recv_sems, None), n_a2a, local_copies=False)
        start()
        token[...] = jnp.zeros_like(token)

    outs = pl.pallas_call(
        body, name=name,
        out_shape=[pltpu.HBM(v.shape, v.dtype) for v in arrays] + [pltpu.HBM(sh.shape, sh.dtype) for sh in shapes]
        + [pltpu.SemaphoreType.DMA((n_arr * (N_DEV - 1),)), pltpu.SemaphoreType.DMA((n_arr * (N_DEV - 1),)),
           jax.ShapeDtypeStruct((8, LANES), F32)],
        in_specs=[_HBM] * (2 * n_arr) + [_ANY],
        out_specs=[_HBM] * (2 * n_arr) + [_SEM, _SEM, pl.BlockSpec(memory_space=pltpu.VMEM)],
        input_output_aliases={i: i for i in range(2 * n_arr)},
        compiler_params=pltpu.CompilerParams(has_side_effects=_DATAFLOW),
    )(*arrays, *lands, after)
    return dict(ins=outs[:n_arr], lands=outs[n_arr:2 * n_arr], sems=outs[2 * n_arr:2 * n_arr + 2], n_a2a=n_a2a,
                token=outs[2 * n_arr + 2])


def exchange_wait(handle, after, name):
    ins, lands, sems, n_a2a = handle["ins"], handle["lands"], handle["sems"], handle["n_a2a"]
    n_arr = len(ins)

    def body(*refs):
        in_refs, land_refs = refs[:n_arr], refs[n_arr:2 * n_arr]
        send_sems, recv_sems = refs[2 * n_arr:2 * n_arr + 2]
        _, wait = _exchange_ops(in_refs, land_refs, (send_sems, recv_sems, None), n_a2a, local_copies=False)
        wait()

    outs = pl.pallas_call(
        body, name=name,
        out_shape=[pltpu.HBM(v.shape, v.dtype) for v in list(ins) + list(lands)],
        in_specs=[_HBM] * (2 * n_arr) + [_SEM, _SEM, _ANY], out_specs=[_HBM] * (2 * n_arr),
        input_output_aliases={i: i for i in range(2 * n_arr)},
        compiler_params=pltpu.CompilerParams(has_side_effects=_DATAFLOW),
    )(*ins, *lands, *sems, after)
    return outs[n_arr:]


def _exchange_out_shape(a2a, ag):
    out_shape = [jax.ShapeDtypeStruct(v.shape, v.dtype) for v in a2a]
    return out_shape + [jax.ShapeDtypeStruct((N_DEV,) + v.shape, v.dtype) for v in ag]


def _exchange_sems(n_arr):
    return [pltpu.SemaphoreType.DMA((n_arr * (N_DEV - 1),)), pltpu.SemaphoreType.DMA((n_arr * (N_DEV - 1),)),
            pltpu.SemaphoreType.DMA((n_arr,))]


def _exchange_ops(ins, outs, sems, n_a2a, local_copies=True):
    send_sems, recv_sems, local_sems = sems
    n_arr = len(ins)
    x, y, c = lax.axis_index("x"), lax.axis_index("y"), lax.axis_index("c")
    me = 4 * x + 2 * y + c

    def peer_of(k):
        px = 1 - x if k & 4 else x
        py = 1 - y if k & 2 else y
        pc = 1 - c if k & 1 else c
        return (px, py, pc), 4 * px + 2 * py + pc

    def copy(a, k, landing):
        peer, pid = peer_of(k)
        src = ins[a].at[pid] if a < n_a2a else ins[a]
        return pltpu.make_async_remote_copy(
            src_ref=src, dst_ref=outs[a].at[pid if landing else me], send_sem=send_sems.at[a * (N_DEV - 1) + k - 1],
            recv_sem=recv_sems.at[a * (N_DEV - 1) + k - 1], device_id=peer, device_id_type=pl.DeviceIdType.MESH)

    def local(a):
        src = ins[a].at[me] if a < n_a2a else ins[a]
        return pltpu.make_async_copy(src, outs[a].at[me], local_sems.at[a])

    order = (6, 7, 4, 5, 2, 3, 1)

    def start():
        if local_copies:
            for a in range(n_arr):
                local(a).start()
        for a in range(n_arr):
            for k in order:
                copy(a, k, landing=False).start()

    def wait():
        for a in range(n_arr):
            for k in order:
                copy(a, k, landing=True).wait_recv()
        for a in range(n_arr):
            for k in order:
                copy(a, k, landing=False).wait_send()
        if local_copies:
            for a in range(n_arr):
                local(a).wait()

    return start, wait


def _rows_tile(n, want):
    t = min(want, n)
    assert n % t == 0, (n, t)
    return t


def rmsnorm_fwd(x, g, name):
    s, d = x.shape
    tm = _rows_tile(s, 512)

    def body(x_ref, g_ref, h_ref):
        xv = x_ref[...]
        r = lax.rsqrt(jnp.mean(xv * xv, axis=-1, keepdims=True) + EPS)
        h_ref[...] = ((xv * r) * g_ref[...]).astype(h_ref.dtype)

    return pl.pallas_call(
        body, name=name, grid=(s // tm,),
        in_specs=[pl.BlockSpec((tm, d), lambda i: (i, 0)), pl.BlockSpec((1, d), lambda i: (0, 0))],
        out_specs=pl.BlockSpec((tm, d), lambda i: (i, 0)),
        out_shape=jax.ShapeDtypeStruct((s, d), BF16),
        compiler_params=_params("parallel"),
    )(x, g.reshape(1, d))


def norm_matmul(x, heads, name):
    s, d = x.shape
    tm = _rows_tile(s, 512)
    n = len(heads)

    def body(*refs):
        x_ref = refs[0]
        xv = x_ref[...]
        xhat = xv * lax.rsqrt(jnp.mean(xv * xv, axis=-1, keepdims=True) + EPS)
        for k in range(n):
            g_ref, w_ref = refs[1 + 2 * k], refs[2 + 2 * k]
            h_ref, y_ref = refs[1 + 2 * n + 2 * k], refs[2 + 2 * n + 2 * k]
            h = (xhat * g_ref[...]).astype(BF16)
            h_ref[...] = h
            y_ref[...] = _dot(h, w_ref[...]).astype(y_ref.dtype)

    row = pl.BlockSpec((tm, d), lambda i: (i, 0))
    in_specs, out_specs, out_shape, args = [row], [], [], [x]
    for g, w_k, dtype in heads:
        in_specs += [pl.BlockSpec((1, d), lambda i: (0, 0)), pl.BlockSpec(w_k.shape, lambda i: (0, 0))]
        args += [g.reshape(1, d), w_k]
        out_specs += [row, pl.BlockSpec((tm, w_k.shape[1]), lambda i: (i, 0))]
        out_shape += [jax.ShapeDtypeStruct((s, d), BF16), jax.ShapeDtypeStruct((s, w_k.shape[1]), dtype)]
    outs = pl.pallas_call(
        body, name=name, grid=(s // tm,), in_specs=in_specs, out_specs=out_specs, out_shape=out_shape,
        compiler_params=_params("parallel"),
    )(*args)
    return [(outs[2 * k], outs[2 * k + 1]) for k in range(n)]


def rmsnorm_bwd(x, dres, pairs, name):
    s, d = x.shape
    tm = _rows_tile(s, 256)
    n = len(pairs)

    def body(*refs):
        x_ref, dres_ref = refs[0], refs[1]
        g_refs, dh_refs = refs[2:2 + n], refs[2 + n:2 + 2 * n]
        dx_ref, dxb_ref = refs[2 + 2 * n], refs[3 + 2 * n]
        dg_refs = refs[4 + 2 * n:]
        i = pl.program_id(0)
        xv = x_ref[...]
        r = lax.rsqrt(jnp.mean(xv * xv, axis=-1, keepdims=True) + EPS)
        xhat = xv * r
        acc = dres_ref[...]
        for k in range(n):
            dh = dh_refs[k][...].astype(F32)
            part = jnp.sum(dh * xhat, axis=0, keepdims=True)

            @pl.when(i == 0)
            def _(k=k, part=part):
                dg_refs[k][...] = part

            @pl.when(i > 0)
            def _(k=k, part=part):
                dg_refs[k][...] += part

            dxh = dh * g_refs[k][...]
            acc = acc + r * (dxh - xhat * jnp.mean(dxh * xhat, axis=-1, keepdims=True))
        dx_ref[...] = acc
        dxb_ref[...] = acc.astype(BF16)

    row = pl.BlockSpec((tm, d), lambda i: (i, 0))
    vec = pl.BlockSpec((1, d), lambda i: (0, 0))
    outs = pl.pallas_call(
        body, name=name, grid=(s // tm,),
        in_specs=[row, row] + [vec] * n + [row] * n,
        out_specs=[row, row] + [vec] * n,
        out_shape=[jax.ShapeDtypeStruct((s, d), F32), jax.ShapeDtypeStruct((s, d), BF16)]
        + [jax.ShapeDtypeStruct((1, d), F32)] * n,
        compiler_params=_params("arbitrary"),
    )(x, dres, *[g.reshape(1, d) for g, _ in pairs], *[dh for _, dh in pairs])
    return outs[0], outs[1], outs[2:]


def matmul_norm_bwd(x, dres, heads, name):
    s, d = x.shape
    tm = _rows_tile(s, 512)
    n = len(heads)
    n_ops = [len(pairs) for _, pairs in heads]

    def body(*refs):
        x_ref, dres_ref = refs[0], refs[1]
        n_in = 2 + n + 2 * sum(n_ops)
        dx_ref, dxb_ref = refs[n_in], refs[n_in + 1]
        dg_refs = refs[n_in + 2:]
        i = pl.program_id(0)
        xv = x_ref[...]
        r = lax.rsqrt(jnp.mean(xv * xv, axis=-1, keepdims=True) + EPS)
        xhat = xv * r
        acc = dres_ref[...]
        pos = 2 + n
        for k in range(n):
            dh = None
            for _ in range(n_ops[k]):
                prod = _dot(refs[pos][...], refs[pos + 1][...], _NT)
                dh = prod if dh is None else dh + prod
                pos += 2
            part = jnp.sum(dh * xhat, axis=0, keepdims=True)

            @pl.when(i == 0)
            def _(k=k, part=part):
                dg_refs[k][...] = part

            @pl.when(i > 0)
            def _(k=k, part=part):
                dg_refs[k][...] += part

            dxh = dh * refs[2 + k][...]
            acc = acc + r * (dxh - xhat * jnp.mean(dxh * xhat, axis=-1, keepdims=True))
        dx_ref[...] = acc
        dxb_ref[...] = acc.astype(BF16)

    row = pl.BlockSpec((tm, d), lambda i: (i, 0))
    vec = pl.BlockSpec((1, d), lambda i: (0, 0))
    in_specs, args = [row, row] + [vec] * n, [x, dres] + [g.reshape(1, d) for g, _ in heads]
    for _, pairs in heads:
        for a, w_k in pairs:
            in_specs += [pl.BlockSpec((tm, a.shape[1]), lambda i: (i, 0)), pl.BlockSpec(w_k.shape, lambda i: (0, 0))]
            args += [a, w_k]
    outs = pl.pallas_call(
        body, name=name, grid=(s // tm,), in_specs=in_specs, out_specs=[row, row] + [vec] * n,
        out_shape=[jax.ShapeDtypeStruct((s, d), F32), jax.ShapeDtypeStruct((s, d), BF16)]
        + [jax.ShapeDtypeStruct((1, d), F32)] * n,
        compiler_params=_params("arbitrary"),
    )(*args)
    return outs[0], outs[1], outs[2:]


def loss_head(x, tgt, g, name):
    s, d = x.shape
    tm = _rows_tile(s, 256)

    def body(x_ref, t_ref, g_ref, dx_ref, dxb_ref, dg_ref, loss_ref):
        i = pl.program_id(0)
        xv = x_ref[...]
        gv = g_ref[...]
        r = lax.rsqrt(jnp.mean(xv * xv, axis=-1, keepdims=True) + EPS)
        xhat = xv * r
        diff = xhat * gv - t_ref[...]
        part_loss = 0.5 * jnp.sum(jnp.mean(diff * diff, axis=-1, keepdims=True), axis=0, keepdims=True)
        dy = diff * (1.0 / d)
        part_dg = jnp.sum(dy * xhat, axis=0, keepdims=True)

        @pl.when(i == 0)
        def _():
            dg_ref[...] = part_dg
            loss_ref[...] = jnp.broadcast_to(part_loss, loss_ref.shape)

        @pl.when(i > 0)
        def _():
            dg_ref[...] += part_dg
            loss_ref[...] += jnp.broadcast_to(part_loss, loss_ref.shape)

        dxh = dy * gv
        dx = r * (dxh - xhat * jnp.mean(dxh * xhat, axis=-1, keepdims=True))
        dx_ref[...] = dx
        dxb_ref[...] = dx.astype(BF16)

    row = pl.BlockSpec((tm, d), lambda i: (i, 0))
    vec = pl.BlockSpec((1, d), lambda i: (0, 0))
    return pl.pallas_call(
        body, name=name, grid=(s // tm,),
        in_specs=[row, row, vec], out_specs=[row, row, vec, pl.BlockSpec((1, LANES), lambda i: (0, 0))],
        out_shape=[jax.ShapeDtypeStruct((s, d), F32), jax.ShapeDtypeStruct((s, d), BF16),
                   jax.ShapeDtypeStruct((1, d), F32), jax.ShapeDtypeStruct((1, LANES), F32)],
        compiler_params=_params("arbitrary"),
    )(x, tgt, g.reshape(1, d))


def matmul(pairs, mode, out_dtype, name, res=None, tm=512, tn=None, roles=(None, None), count=1):
    a0, b0 = pairs[0]
    m = a0.shape[-1] if mode == "tn" else a0.shape[-2]
    n = b0.shape[-2] if mode == "nt" else b0.shape[-1]
    if tn is None:
        tn = n if (mode != "tn" and n <= 1024) else 512
    tm, tn = _rows_tile(m, tm), _rows_tile(n, tn)
    dims = {"nn": _NN, "nt": _NT, "tn": _TN}[mode]
    npairs = len(pairs)
    batched, reducing = "batch" in roles, "reduce" in roles
    assert not (batched and reducing) and (res is None or not batched)
    n_in = 2 * npairs + (res is not None)

    def body(*refs):
        o_ref = refs[n_in]
        acc = None
        for p in range(npairs):
            a_ref, b_ref = refs[2 * p], refs[2 * p + 1]
            for kk in range(count if reducing else 1):
                a = a_ref[kk] if roles[0] == "reduce" else a_ref[...]
                b = b_ref[kk] if roles[1] == "reduce" else b_ref[...]
                d = _dot(a.astype(BF16), b.astype(BF16), dims)
                acc = d if acc is None else acc + d
        if res is not None:
            acc = acc + refs[2 * npairs][...]
        o_ref[...] = acc.astype(o_ref.dtype)

    def spec(arr, role, block, index):
        if arr.ndim == 2:
            return pl.BlockSpec(block, lambda bb, i, j: index(i, j))
        assert role in ("batch", "reduce") and arr.shape[0] == count, (name, role, arr.shape)
        if role == "batch":
            return pl.BlockSpec((None,) + block, lambda bb, i, j: (bb,) + index(i, j))
        return pl.BlockSpec((count,) + block, lambda bb, i, j: (0,) + index(i, j))

    in_specs, args = [], []
    for a, b in pairs:
        if mode == "tn":
            in_specs.append(spec(a, roles[0], (a.shape[-2], tm), lambda i, j: (0, i)))
        else:
            in_specs.append(spec(a, roles[0], (tm, a.shape[-1]), lambda i, j: (i, 0)))
        if mode == "nt":
            in_specs.append(spec(b, roles[1], (tn, b.shape[-1]), lambda i, j: (j, 0)))
        else:
            in_specs.append(spec(b, roles[1], (b.shape[-2], tn), lambda i, j: (0, j)))
        args += [a, b]
    if res is not None:
        in_specs.append(pl.BlockSpec((tm, tn), lambda bb, i, j: (i, j)))
        args.append(res)
    if batched:
        out_spec = pl.BlockSpec((None, tm, tn), lambda bb, i, j: (bb, i, j))
        out_shape = jax.ShapeDtypeStruct((count, m, n), out_dtype)
    else:
        out_spec = pl.BlockSpec((tm, tn), lambda bb, i, j: (i, j))
        out_shape = jax.ShapeDtypeStruct((m, n), out_dtype)
    return pl.pallas_call(
        body, name=name, grid=(count if batched else 1, m // tm, n // tn),
        in_specs=in_specs, out_specs=out_spec, out_shape=out_shape,
        compiler_params=_params("parallel", "parallel", "parallel"),
    )(*args)


def _sigmoid(v):
    return 0.5 * jnp.tanh(0.5 * v) + 0.5


def ffn_up(h, wgu, name):
    s, d = h.shape
    _, n_sh, _, fs = wgu.shape
    tm = _rows_tile(s, 512)

    def body(h_ref, w_ref, gu_ref, act_ref):
        hv = h_ref[...]
        gate = _dot(hv, w_ref[0])
        up = _dot(hv, w_ref[1])
        gu_ref[0] = gate.astype(BF16)
        gu_ref[1] = up.astype(BF16)
        act_ref[...] = ((gate * _sigmoid(gate)) * up).astype(BF16)

    return pl.pallas_call(
        body, name=name, grid=(n_sh, s // tm),
        in_specs=[pl.BlockSpec((tm, d), lambda j, i: (i, 0)), pl.BlockSpec((2, None, d, fs), lambda j, i: (0, j, 0, 0))],
        out_specs=[pl.BlockSpec((2, None, tm, fs), lambda j, i: (0, j, i, 0)),
                   pl.BlockSpec((None, tm, fs), lambda j, i: (j, i, 0))],
        out_shape=[jax.ShapeDtypeStruct((2, n_sh, s, fs), BF16), jax.ShapeDtypeStruct((n_sh, s, fs), BF16)],
        compiler_params=_params("parallel", "parallel"),
    )(h, wgu)


def ffn_bwd_act(dxo, wdown, gu, name):
    s, d = dxo.shape
    n_sh, fs, _ = wdown.shape
    tm = _rows_tile(s, 512)

    def body(dx_ref, w_ref, gu_ref, dgu_ref):
        dact = _dot(dx_ref[...], w_ref[...], _NT)
        gate = gu_ref[0].astype(F32)
        sig = _sigmoid(gate)
        dgu_ref[0] = ((dact * gu_ref[1].astype(F32)) * (sig * (1.0 + gate * (1.0 - sig)))).astype(BF16)
        dgu_ref[1] = (dact * (gate * sig)).astype(BF16)

    tile = pl.BlockSpec((2, None, tm, fs), lambda j, i: (0, j, i, 0))
    return pl.pallas_call(
        body, name=name, grid=(n_sh, s // tm),
        in_specs=[pl.BlockSpec((tm, d), lambda j, i: (i, 0)), pl.BlockSpec((None, fs, d), lambda j, i: (j, 0, 0)), tile],
        out_specs=tile,
        out_shape=jax.ShapeDtypeStruct((2, n_sh, s, fs), BF16),
        compiler_params=_params("parallel", "parallel"),
    )(dxo, wdown, gu)


def _pool_select(col, by_window):
    out = by_window[3]
    for g in (2, 1, 0):
        out = jnp.where(col < (g + 1) * POOL_GROUP, by_window[g], out)
    return out


def _pool_counts(pos):
    return [jnp.minimum(pos + 1, w).astype(F32) for w in POOL_WINDOWS]


def pool_fwd(proj, wbd, scale, name):
    s = proj.shape[0]
    tm = _rows_tile(s, 256)
    nt = s // tm
    width = SB_WIDTH
    per = tm // POOL_HALO

    def body(u_ref, halo_ref, w_ref, sc_ref, pooled_ref, grouped_ref):
        i = pl.program_id(0)
        u = u_ref[...]
        halo = jnp.where(i > 0, halo_ref[...], 0.0)
        ext = jnp.concatenate([halo, u], axis=0)
        sums = []
        acc = ext
        for k in (1, 2, 4, 8):
            acc = acc + pltpu.roll(acc, k, 0)
            sums.append(acc[POOL_HALO:])
        pos = i * tm + lax.broadcasted_iota(jnp.int32, (tm, width), 0)
        col = lax.broadcasted_iota(jnp.int32, (tm, width), 1)
        means = [sm / cnt for sm, cnt in zip(sums, _pool_counts(pos))]
        pooled = (_pool_select(col, means) - u).astype(BF16)
        pooled_ref[...] = pooled
        grouped_ref[...] = (_dot(pooled, w_ref[...]) * sc_ref[...]).astype(BF16)

    row = pl.BlockSpec((tm, width), lambda i: (i, 0))
    return pl.pallas_call(
        body, name=name, grid=(nt,),
        in_specs=[row, pl.BlockSpec((POOL_HALO, width), lambda i: (jnp.maximum(i * per - 1, 0), 0)),
                  pl.BlockSpec((width, width), lambda i: (0, 0)), pl.BlockSpec((1, width), lambda i: (0, 0))],
        out_specs=[row, row],
        out_shape=[jax.ShapeDtypeStruct((s, width), BF16), jax.ShapeDtypeStruct((s, width + MEM_WIDTH), BF16)],
        compiler_params=_params("parallel"),
    )(proj, proj, wbd, scale.reshape(1, width))


def pool_bwd(dcat, pooled, wbd, scale, name):
    s = dcat.shape[0]
    tm = _rows_tile(s, 256)
    nt = s // tm
    width = SB_WIDTH
    per = tm // POOL_HALO
    n_ext = tm + POOL_HALO

    def body(dg_ref, halo_ref, pooled_ref, w_ref, sc_ref, du_ref, dw_ref, dsc_ref):
        i = pl.program_id(0)
        w = w_ref[...]
        sc = sc_ref[...]
        dg = dg_ref[...]
        pooled = pooled_ref[...]
        pg = _dot(pooled, w)
        dpg = (dg * sc).astype(BF16)
        part_sc = jnp.sum(dg * pg, axis=0, keepdims=True)
        part_w = _dot(pooled, dpg, _TN)

        @pl.when(i == 0)
        def _():
            dsc_ref[...] = part_sc
            dw_ref[...] = part_w

        @pl.when(i > 0)
        def _():
            dsc_ref[...] += part_sc
            dw_ref[...] += part_w

        dpooled = _dot(dpg, w, _NT)
        halo_dpg = (jnp.where(i < nt - 1, halo_ref[...], 0.0) * sc).astype(BF16)
        ext = jnp.concatenate([dpooled, _dot(halo_dpg, w, _NT)], axis=0)
        pos = i * tm + lax.broadcasted_iota(jnp.int32, (n_ext, width), 0)
        col = lax.broadcasted_iota(jnp.int32, (tm, width), 1)
        outs = []
        for k_idx, cnt in enumerate(_pool_counts(pos)):
            acc = ext / cnt
            for k in (1, 2, 4, 8)[:k_idx + 1]:
                acc = acc + pltpu.roll(acc, n_ext - k, 0)
            outs.append(acc[:tm])
        du_ref[...] = (_pool_select(col, outs) - dpooled).astype(BF16)

    row = pl.BlockSpec((tm, width), lambda i: (i, 0))
    return pl.pallas_call(
        body, name=name, grid=(nt,),
        in_specs=[row, pl.BlockSpec((POOL_HALO, width), lambda i: (jnp.minimum((i + 1) * per, s // POOL_HALO - 1), 0)),
                  row, pl.BlockSpec((width, width), lambda i: (0, 0)), pl.BlockSpec((1, width), lambda i: (0, 0))],
        out_specs=[row, pl.BlockSpec((width, width), lambda i: (0, 0)), pl.BlockSpec((1, width), lambda i: (0, 0))],
        out_shape=[jax.ShapeDtypeStruct((s, width + MEM_WIDTH), BF16), jax.ShapeDtypeStruct((width, width), F32),
                   jax.ShapeDtypeStruct((1, width), F32)],
        compiler_params=_params("arbitrary"),
    )(dcat, dcat, pooled, wbd, scale.reshape(1, width))


def _head_masks(shape):
    lane = lax.broadcasted_iota(jnp.int32, shape, 1)
    return lane < HEAD_DIM, lane >= HEAD_DIM


def _mem_probs(qh, kp):
    logits = _dot(qh, kp, _NT)
    logits = logits - jnp.max(logits, axis=-1, keepdims=True)
    p = jnp.exp(logits)
    return p / jnp.sum(p, axis=-1, keepdims=True)


def mem_attn_fwd(proj, mkv, cat, name):
    s = proj.shape[0]
    m = mkv.shape[0]
    tm = _rows_tile(s, 512)
    q_blk = SB_WIDTH // MEM_WIDTH

    def body(q_ref, kv_ref, _, o_ref):
        masks = _head_masks((tm, LANES))
        for pr in range(MEM_WIDTH // LANES):
            q = q_ref[:, pr * LANES:(pr + 1) * LANES] * QK_SCALE
            kp = kv_ref[:, pr * LANES:(pr + 1) * LANES]
            vp = kv_ref[:, MEM_WIDTH + pr * LANES:MEM_WIDTH + (pr + 1) * LANES]
            outs = []
            for hm in masks:
                p = _mem_probs(jnp.where(hm, q, 0.0).astype(BF16), kp)
                outs.append(_dot(p.astype(BF16), vp))
            o_ref[:, pr * LANES:(pr + 1) * LANES] = jnp.where(masks[0], outs[0], outs[1]).astype(BF16)

    return pl.pallas_call(
        body, name=name, grid=(s // tm,),
        in_specs=[pl.BlockSpec((tm, MEM_WIDTH), lambda i: (i, q_blk)), pl.BlockSpec((m, 2 * MEM_WIDTH), lambda i: (0, 0)),
                  _ANY],
        out_specs=pl.BlockSpec((tm, MEM_WIDTH), lambda i: (i, q_blk)),
        out_shape=jax.ShapeDtypeStruct(cat.shape, BF16),
        input_output_aliases={2: 0},
        compiler_params=_params("parallel"),
    )(proj, mkv, cat)


def mem_attn_bwd(proj, mkv, dcat, dproj, name):
    s = proj.shape[0]
    m = mkv.shape[0]
    tm = _rows_tile(s, 512)
    q_blk = SB_WIDTH // MEM_WIDTH

    def body(q_ref, kv_ref, do_ref, _, dq_ref, dkv_ref):
        i = pl.program_id(0)

        @pl.when(i == 0)
        def _():
            dkv_ref[...] = jnp.zeros_like(dkv_ref)

        masks = _head_masks((tm, LANES))
        for pr in range(MEM_WIDTH // LANES):
            q = q_ref[:, pr * LANES:(pr + 1) * LANES] * QK_SCALE
            do = do_ref[:, pr * LANES:(pr + 1) * LANES]
            kp = kv_ref[:, pr * LANES:(pr + 1) * LANES]
            vp = kv_ref[:, MEM_WIDTH + pr * LANES:MEM_WIDTH + (pr + 1) * LANES]
            dqs = []
            dk = jnp.zeros((m, LANES), F32)
            dv = jnp.zeros((m, LANES), F32)
            for hm in masks:
                qh = jnp.where(hm, q, 0.0).astype(BF16)
                doh = jnp.where(hm, do, 0.0).astype(BF16)
                p = _mem_probs(qh, kp)
                dp = _dot(doh, vp, _NT)
                ds = (p * (dp - jnp.sum(dp * p, axis=-1, keepdims=True))).astype(BF16)
                dqs.append(_dot(ds, kp))
                dk = dk + _dot(ds, qh, _TN)
                dv = dv + _dot(p.astype(BF16), doh, _TN)
            dq_ref[:, pr * LANES:(pr + 1) * LANES] = (jnp.where(masks[0], dqs[0], dqs[1]) * QK_SCALE).astype(BF16)
            dkv_ref[:, pr * LANES:(pr + 1) * LANES] += dk
            dkv_ref[:, MEM_WIDTH + pr * LANES:MEM_WIDTH + (pr + 1) * LANES] += dv

    return pl.pallas_call(
        body, name=name, grid=(s // tm,),
        in_specs=[pl.BlockSpec((tm, MEM_WIDTH), lambda i: (i, q_blk)), pl.BlockSpec((m, 2 * MEM_WIDTH), lambda i: (0, 0)),
                  pl.BlockSpec((tm, MEM_WIDTH), lambda i: (i, q_blk)), _ANY],
        out_specs=[pl.BlockSpec((tm, MEM_WIDTH), lambda i: (i, q_blk)), pl.BlockSpec((m, 2 * MEM_WIDTH), lambda i: (0, 0))],
        out_shape=[jax.ShapeDtypeStruct(dproj.shape, BF16), jax.ShapeDtypeStruct((m, 2 * MEM_WIDTH), F32)],
        input_output_aliases={3: 0},
        compiler_params=_params("arbitrary"),
    )(proj, mkv, dcat, dproj)


SB_ROWS = 512
SB_KEYS = 512
SB_SUB = LANES


def _sb_tri(later):
    r = lax.broadcasted_iota(jnp.int32, (2 * SB_SUB, 2 * SB_SUB), 0) % SB_SUB
    c = lax.broadcasted_iota(jnp.int32, (2 * SB_SUB, 2 * SB_SUB), 1)
    tri = (r > c) if later else (r < c)
    return (tri | (c >= SB_SUB)).astype(BF16)


def _sb_sums(v, tri, n_sub):
    out = []
    for m in range(n_sub):
        vm = v[:, m * SB_SUB:(m + 1) * SB_SUB]
        hi = vm.astype(BF16)
        lo = (vm - hi.astype(F32)).astype(BF16)
        r = _dot(jnp.concatenate([hi, lo], axis=1), tri)
        out.append((r[:, :SB_SUB], r[:, SB_SUB:]))
    return out


def _sb_tri_pair_before():
    r = lax.broadcasted_iota(jnp.int32, (2 * SB_SUB, 2 * SB_SUB), 0)
    c = lax.broadcasted_iota(jnp.int32, (2 * SB_SUB, 2 * SB_SUB), 1)
    return ((r // SB_SUB == c // SB_SUB) & (r % SB_SUB < c % SB_SUB)).astype(BF16)


def _sb_prefix_pairs(e, tri_pair, n_sub):
    out = []
    for m in range(0, n_sub, 2):
        pair = e[:, m * SB_SUB:(m + 2) * SB_SUB]
        r = _dot(pair.astype(BF16), tri_pair)
        for h in range(2):
            em = pair[:, h * SB_SUB:(h + 1) * SB_SUB]
            out.append((r[:, h * SB_SUB:(h + 1) * SB_SUB], jnp.sum(em, axis=1, keepdims=True)))
    return out


def _sb_logs(nqh, k_blk, valid):
    nz = _dot(nqh, k_blk, _NT)
    log_not = jnp.minimum(nz, 0.0) - jnp.log(1.0 + jnp.exp2(jnp.abs(nz) * -LOG2_E))
    log_beta = log_not - nz
    if valid is not None:
        log_not = jnp.where(valid, log_not, 0.0)
    return log_not, log_beta


def _sb_weights(log_not, log_beta, tri_later, carry_in, valid, n_sub):
    sums = _sb_sums(log_not, tri_later, n_sub)
    run = carry_in
    later = [None] * n_sub
    for m in reversed(range(n_sub)):
        later[m] = sums[m][0] + run
        run = run + sums[m][1]
    w = jnp.exp(log_beta + jnp.concatenate(later, axis=1))
    if valid is not None:
        w = jnp.where(valid, w, 0.0)
    return w, run


def _pad_rows(a, r0):
    return a if r0 == 0 else jnp.concatenate([jnp.zeros((r0,) + a.shape[1:], a.dtype), a], axis=0)


def _sb_diag_parts(nqh, k_blk, tri_later, n_sub):
    tq = nqh.shape[0]
    parts = []
    for m in range(n_sub):
        r0 = m * SB_SUB
        row = lax.broadcasted_iota(jnp.int32, (tq - r0, SB_SUB), 0)
        col = lax.broadcasted_iota(jnp.int32, (tq - r0, SB_SUB), 1)
        valid = (row >= SB_SUB) | (row > col)
        log_not, log_beta = _sb_logs(nqh[r0:], k_blk[r0:r0 + SB_SUB], valid)
        (inside, row_sum), = _sb_sums(log_not, tri_later, 1)
        parts.append([r0, valid, log_beta, inside, row_sum])
    run = jnp.zeros((tq, LANES), F32)
    for part in reversed(parts):
        r0, row_sum = part[0], part[4]
        part[4] = run[r0:]
        run = run + _pad_rows(row_sum, r0)
    return parts, run


def _sb_diag_fwd(nqh, k_blk, v_blk, tri_later, n_sub):
    parts, run = _sb_diag_parts(nqh, k_blk, tri_later, n_sub)
    acc = jnp.zeros((nqh.shape[0], LANES), F32)
    for r0, valid, log_beta, inside, later_blocks in parts:
        w = jnp.where(valid, jnp.exp(log_beta + inside + later_blocks), 0.0)
        acc = acc + _pad_rows(_dot(w.astype(BF16), v_blk[r0:r0 + SB_SUB]), r0)
    return run, acc


def _sb_valid(i, j, tq, tk):
    qpos = i * tq + lax.broadcasted_iota(jnp.int32, (tq, tk), 0)
    kpos = j * tk + lax.broadcasted_iota(jnp.int32, (tq, tk), 1)
    return qpos > kpos


def sb_fwd(projb, kv, name, gather=()):
    s = projb.shape[0]
    tq, tk = _rows_tile(s, SB_ROWS), _rows_tile(s, SB_KEYS)
    n_sub = tk // SB_SUB
    n_pairs = SB_WIDTH // LANES
    assert s // tk <= LANES and tk % tq == 0

    n_g = len(gather)
    n_i = s // tq

    def body(*refs):
        q_ref, k_ref, v_ref = refs[:3]
        o_ref, c_ref = refs[3 + n_g:5 + n_g]
        i = pl.program_id(1)
        if n_g:
            start, wait = _exchange_ops(refs[3:3 + n_g], refs[5 + n_g:5 + 2 * n_g], refs[5 + 2 * n_g:], 0)
            pl.when((pl.program_id(0) == 0) & (i == 0))(start)
        j_diag = (i * tq) // tk
        masks = _head_masks((tq, LANES))
        nq = q_ref[...] * (-QK_SCALE)
        tri = _sb_tri(later=True)
        lane = lax.broadcasted_iota(jnp.int32, (tq, LANES), 1)
        nqh = [jnp.where(hm, nq, 0.0).astype(BF16) for hm in masks]

        def block(j, carry, valid):
            off = pl.multiple_of(j * tk, tk)
            k_blk, v_blk = k_ref[pl.ds(off, tk), :], v_ref[pl.ds(off, tk), :]
            out = []
            for hh in range(2):
                c, acc, cmat = carry[3 * hh:3 * hh + 3]
                log_not, log_beta = _sb_logs(nqh[hh], k_blk, valid)
                w, c_next = _sb_weights(log_not, log_beta, tri, c, valid, n_sub)
                out += [c_next, acc + _dot(w.astype(BF16), v_blk), jnp.where(lane == j, c, cmat)]
            return tuple(out)

        zero = jnp.zeros((tq, LANES), F32)
        if tq == tk:
            off = pl.multiple_of(i * tk, tk)
            k_blk, v_blk = k_ref[pl.ds(off, tk), :], v_ref[pl.ds(off, tk), :]
            carry = ()
            for hh in range(2):
                carry += _sb_diag_fwd(nqh[hh], k_blk, v_blk, tri, n_sub) + (zero,)
        else:
            carry = block(j_diag, (zero,) * 6, _sb_valid(i, j_diag, tq, tk))
        carry = lax.fori_loop(0, j_diag, lambda jj, cr: block(j_diag - 1 - jj, cr, None), carry)
        c_ref[0] = carry[2]
        c_ref[1] = carry[5]
        o_ref[...] = jnp.where(masks[0], carry[1], carry[4]).astype(BF16)
        if n_g:
            pl.when((pl.program_id(0) == n_pairs - 1) & (i == n_i - 1))(wait)

    outs = pl.pallas_call(
        body, name=name, grid=(n_pairs, n_i),
        in_specs=[pl.BlockSpec((tq, LANES), lambda p, i: (i, p)),
                  pl.BlockSpec((s, LANES), lambda p, i: (0, p)),
                  pl.BlockSpec((s, LANES), lambda p, i: (0, n_pairs + p))] + [_ANY] * n_g,
        out_specs=[pl.BlockSpec((tq, LANES), lambda p, i: (i, p)),
                   pl.BlockSpec((2, tq, LANES), lambda p, i: (p, i, 0))] + [_ANY] * n_g,
        out_shape=[jax.ShapeDtypeStruct((s, SB_WIDTH + MEM_WIDTH), BF16),
                   jax.ShapeDtypeStruct((N_SB_HEADS, s, LANES), F32)] + _exchange_out_shape([], gather),
        scratch_shapes=_exchange_sems(n_g) if n_g else [],
        compiler_params=_params("arbitrary", "arbitrary"),
    )(projb, kv, kv, *gather)
    return outs[0], outs[1], outs[2:]


def sb_bwd(projb, kv, dcat, csave, name, scatter=()):
    s = projb.shape[0]
    tq, tk = _rows_tile(s, SB_ROWS), _rows_tile(s, SB_KEYS)
    n_sub = tk // SB_SUB
    n_pairs = SB_WIDTH // LANES
    n_x = len(scatter)
    n_i = s // tq

    def body(*refs):
        q_ref, k_ref, v_ref, do_ref, c_ref = refs[:5]
        dq_ref, dkb_ref, dvb_ref = refs[5 + n_x:8 + n_x]
        dk_ref, dv_ref = refs[8 + 2 * n_x:10 + 2 * n_x]
        i = pl.program_id(1)
        if n_x:
            start, wait = _exchange_ops(refs[5:5 + n_x], refs[8 + n_x:8 + 2 * n_x], refs[10 + 2 * n_x:], n_x)
            pl.when((pl.program_id(0) == 0) & (i == 0))(start)
        j_diag = (i * tq) // tk

        @pl.when(i == 0)
        def _():
            dk_ref[...] = jnp.zeros_like(dk_ref)
            dv_ref[...] = jnp.zeros_like(dv_ref)

        masks = _head_masks((tq, LANES))
        qs = q_ref[...] * QK_SCALE
        do = do_ref[...]
        tri_later = _sb_tri(later=True)
        tri_before = _sb_tri_pair_before()
        lane = lax.broadcasted_iota(jnp.int32, (tq, LANES), 1)
        qh = [jnp.where(hm, qs, 0.0).astype(BF16) for hm in masks]
        nqh = [jnp.where(hm, -qs, 0.0).astype(BF16) for hm in masks]
        doh = [jnp.where(hm, do, 0.0).astype(BF16) for hm in masks]

        def block(j, carry, valid):
            off = pl.multiple_of(j * tk, tk)
            k_blk, v_blk = k_ref[pl.ds(off, tk), :], v_ref[pl.ds(off, tk), :]
            out, dk, dv = [], None, None
            for hh in range(2):
                e_before, dq = carry[2 * hh:2 * hh + 2]
                log_not, log_beta = _sb_logs(nqh[hh], k_blk, valid)
                c = jnp.sum(jnp.where(lane == j, c_ref[hh], 0.0), axis=1, keepdims=True)
                w, _ = _sb_weights(log_not, log_beta, tri_later, jnp.broadcast_to(c, (tq, LANES)), valid, n_sub)
                dv_h = _dot(w.astype(BF16), doh[hh], _TN)
                e = w * _dot(doh[hh], v_blk, _NT)
                sums = _sb_prefix_pairs(e, tri_before, n_sub)
                run = e_before
                e_pre = []
                for m in range(n_sub):
                    e_pre.append(sums[m][0] + run)
                    run = run + sums[m][1]
                dz = e - jnp.exp(log_beta) * (e + jnp.concatenate(e_pre, axis=1))
                if valid is not None:
                    dz = jnp.where(valid, dz, 0.0)
                dz = dz.astype(BF16)
                dk_h = _dot(dz, qh[hh], _TN)
                dk, dv = (dk_h, dv_h) if hh == 0 else (dk + dk_h, dv + dv_h)
                out += [run, dq + _dot(dz, k_blk)]
            dk_ref[pl.ds(off, tk), :] += dk
            dv_ref[pl.ds(off, tk), :] += dv
            return tuple(out)

        zero = jnp.zeros((tq, LANES), F32)
        carry = lax.fori_loop(0, j_diag, lambda j, cr: block(j, cr, None), (zero,) * 4)
        carry = block(j_diag, carry, _sb_valid(i, j_diag, tq, tk))
        dq_ref[...] = (jnp.where(masks[0], carry[1], carry[3]) * QK_SCALE).astype(BF16)

        @pl.when(i == n_i - 1)
        def _():
            dkb_ref[...] = dk_ref[...].astype(BF16)
            dvb_ref[...] = dv_ref[...].astype(BF16)

        if n_x:
            pl.when((pl.program_id(0) == n_pairs - 1) & (i == n_i - 1))(wait)

    tile = pl.BlockSpec((tq, LANES), lambda p, i: (i, p))
    full = pl.BlockSpec((s, LANES), lambda p, i: (0, p))
    outs = pl.pallas_call(
        body, name=name, grid=(n_pairs, n_i),
        in_specs=[tile, full, pl.BlockSpec((s, LANES), lambda p, i: (0, n_pairs + p)), tile,
                  pl.BlockSpec((2, tq, LANES), lambda p, i: (p, i, 0))] + [_ANY] * n_x,
        out_specs=[tile, full, full] + [_ANY] * n_x,
        out_shape=[jax.ShapeDtypeStruct((s, SB_WIDTH + MEM_WIDTH), BF16), jax.ShapeDtypeStruct((s, SB_WIDTH), BF16),
                   jax.ShapeDtypeStruct((s, SB_WIDTH), BF16)] + _exchange_out_shape(scatter, []),
        scratch_shapes=[pltpu.VMEM((s, LANES), F32), pltpu.VMEM((s, LANES), F32)]
        + (_exchange_sems(n_x) if n_x else []),
        compiler_params=_params("arbitrary", "arbitrary"),
    )(projb, kv, kv, dcat, csave, *scatter)
    return outs[0], outs[1], outs[2], outs[3:]


def reduce_adamw(parts, w, m, v, name, after=None):
    n, r, c = parts.shape
    tr = next((t for t in (128, 176) if r % t == 0), r)
    bias1 = 1.0 - ADAM_B1 ** ADAM_STEP
    bias2 = 1.0 - ADAM_B2 ** ADAM_STEP

    def body(p_ref, w_ref, m_ref, v_ref, *rest):
        g_ref, d_ref, nm_ref, nv_ref = rest[-4:]
        g = p_ref[0].astype(F32)
        for k in range(1, n):
            g = g + p_ref[k].astype(F32)
        new_m = ADAM_B1 * m_ref[...] + (1.0 - ADAM_B1) * g
        new_v = ADAM_B2 * v_ref[...] + (1.0 - ADAM_B2) * (g * g)
        m_hat = new_m / bias1
        v_hat = new_v / bias2
        g_ref[...] = g
        d_ref[...] = -ADAM_LR * (m_hat / (jnp.sqrt(v_hat) + ADAM_EPS) + ADAM_WD * w_ref[...])
        nm_ref[...] = new_m
        nv_ref[...] = new_v

    row = pl.BlockSpec((tr, c), lambda i: (i, 0))
    return pl.pallas_call(
        body, name=name, grid=(r // tr,),
        in_specs=[pl.BlockSpec((n, tr, c), lambda i: (0, i, 0)), row, row, row] + ([] if after is None else [_ANY]),
        out_specs=[row, row, row, row],
        out_shape=[jax.ShapeDtypeStruct((r, c), F32)] * 4,
        compiler_params=_params("parallel"),
    )(parts, w, m, v, *(() if after is None else (after,)))


EARLY_0 = ("a_w_in", "a_w_mem_kv")
EARLY_1 = ("a_w_out",)
EARLY_2 = ("a_w_gu", "a_w_down")
EARLY_3 = ("w_kv", "b_w_q", "b_w_mem_kv")
EARLY = EARLY_0 + EARLY_1 + EARLY_2 + EARLY_3
LATE = ("b_w_out", "b_w_gu", "b_w_down")
BIG = EARLY + LATE
GATHER_3 = ("w_kv", "b_w_q")
GATHER_LATE = LATE + ("b_w_mem_kv",)
COL_SHARDED = ("a_w_gu", "w_kv", "b_w_gu")
SMALL_SHARDED = ("a_norm_mix", "a_scale", "a_norm_ffn")
SMALL_REPL = ("mem_norm", "kv_norm", "b_norm_mix", "b_norm_ffn", "final_norm")
WEIGHTS = ("mem_norm", "a_norm_mix", "a_w_in", "a_w_group", "a_scale", "a_w_mem_kv", "a_w_out", "a_norm_ffn", "a_w_gu",
           "a_w_down", "kv_norm", "w_kv", "b_norm_mix", "b_w_q", "b_w_mem_kv", "b_w_out", "b_norm_ffn", "b_w_gu",
           "b_w_down", "final_norm")


def _pad_row(v):
    v = v.reshape(1, -1)
    return jnp.pad(v, ((0, 0), (0, PACK_COLS - v.shape[1])))


def kernel(x, mem, mem_norm, a_norm_mix, a_w_in, a_w_group, a_scale, a_w_mem_kv, a_w_out, a_norm_ffn, a_w_gu, a_w_down, kv_norm, w_kv, b_norm_mix, b_w_q, b_w_mem_kv, b_w_out, b_norm_ffn, b_w_gu, b_w_down, final_norm, loss_target, m_mem_norm, m_a_norm_mix, m_a_w_in, m_a_w_group, m_a_scale, m_a_w_mem_kv, m_a_w_out, m_a_norm_ffn, m_a_w_gu, m_a_w_down, m_kv_norm, m_w_kv, m_b_norm_mix, m_b_w_q, m_b_w_mem_kv, m_b_w_out, m_b_norm_ffn, m_b_w_gu, m_b_w_down, m_final_norm, v_mem_norm, v_a_norm_mix, v_a_w_in, v_a_w_group, v_a_scale, v_a_w_mem_kv, v_a_w_out, v_a_norm_ffn, v_a_w_gu, v_a_w_down, v_kv_norm, v_w_kv, v_b_norm_mix, v_b_w_q, v_b_w_mem_kv, v_b_w_out, v_b_norm_ffn, v_b_w_gu, v_b_w_down, v_final_norm):
    w = dict(mem_norm=mem_norm, a_norm_mix=a_norm_mix, a_w_in=a_w_in, a_w_group=a_w_group, a_scale=a_scale,
             a_w_mem_kv=a_w_mem_kv, a_w_out=a_w_out, a_norm_ffn=a_norm_ffn, a_w_gu=a_w_gu, a_w_down=a_w_down,
             kv_norm=kv_norm, w_kv=w_kv, b_norm_mix=b_norm_mix, b_w_q=b_w_q, b_w_mem_kv=b_w_mem_kv, b_w_out=b_w_out,
             b_norm_ffn=b_norm_ffn, b_w_gu=b_w_gu, b_w_down=b_w_down, final_norm=final_norm)
    mom = dict(mem_norm=m_mem_norm, a_norm_mix=m_a_norm_mix, a_w_in=m_a_w_in, a_w_group=m_a_w_group, a_scale=m_a_scale,
               a_w_mem_kv=m_a_w_mem_kv, a_w_out=m_a_w_out, a_norm_ffn=m_a_norm_ffn, a_w_gu=m_a_w_gu,
               a_w_down=m_a_w_down, kv_norm=m_kv_norm, w_kv=m_w_kv, b_norm_mix=m_b_norm_mix, b_w_q=m_b_w_q,
               b_w_mem_kv=m_b_w_mem_kv, b_w_out=m_b_w_out, b_norm_ffn=m_b_norm_ffn, b_w_gu=m_b_w_gu,
               b_w_down=m_b_w_down, final_norm=m_final_norm)
    var = dict(mem_norm=v_mem_norm, a_norm_mix=v_a_norm_mix, a_w_in=v_a_w_in, a_w_group=v_a_w_group, a_scale=v_a_scale,
               a_w_mem_kv=v_a_w_mem_kv, a_w_out=v_a_w_out, a_norm_ffn=v_a_norm_ffn, a_w_gu=v_a_w_gu,
               a_w_down=v_a_w_down, kv_norm=v_kv_norm, w_kv=v_w_kv, b_norm_mix=v_b_norm_mix, b_w_q=v_b_w_q,
               b_w_mem_kv=v_b_w_mem_kv, b_w_out=v_b_w_out, b_norm_ffn=v_b_norm_ffn, b_w_gu=v_b_w_gu,
               b_w_down=v_b_w_down, final_norm=v_final_norm)

    me = 4 * lax.axis_index("x") + 2 * lax.axis_index("y") + lax.axis_index("c")
    shard2d = {n: w[n].shape[-2:] for n in BIG}
    shard = {n: w[n].reshape(shard2d[n]).astype(BF16) for n in BIG}
    n_sh = N_DEV // 2
    s_len = x.shape[1]

    small_send = jnp.concatenate([_pad_row(w[n]) for n in SMALL_SHARDED]
                                 + [jnp.zeros((8 - len(SMALL_SHARDED), PACK_COLS), F32)], axis=0)
    def own_slot(land, own):
        return lax.dynamic_update_slice(land, own[None], (me, 0, 0))

    def gather_done(handle, names, after, name):
        got = exchange_wait(handle, after, name)
        gathered.update({n: own_slot(g, shard[n]) for n, g in zip(names, got)})
        return got

    gathered = {}
    gather_0 = exchange_start([], [shard[n] for n in EARLY_0] + [small_send], x, "gather_0_start")
    gather_1 = exchange_start([], [shard[n] for n in EARLY_1], gather_0["token"], "gather_1_start")
    gather_2 = exchange_start([], [shard["a_w_gu"]], gather_1["token"], "gather_2_start")
    gather_2b = exchange_start([], [shard["a_w_down"]], gather_2["token"], "gather_2b_start")
    gather_3 = exchange_start([], [shard[n] for n in GATHER_3], gather_2b["token"], "gather_3_start")
    small_all = own_slot(gather_done(gather_0, EARLY_0, gather_3["token"], "gather_0_wait")[-1], small_send)
    a_norm_mix_f = small_all[:, 0, :a_norm_mix.shape[1]].reshape(-1)
    a_scale_f = small_all[:, 1, :a_scale.shape[1]].reshape(-1)
    a_norm_ffn_f = small_all[:, 2, :a_norm_ffn.shape[1]].reshape(-1)

    def rows_full(n):
        return gathered[n].reshape(-1, shard2d[n][1])

    def gu_views(tag):
        gu8 = gathered[tag + "_w_gu"]
        return gu8.reshape((2, n_sh) + gu8.shape[1:]), gu8

    def down_view(tag):
        down = gathered[tag + "_w_down"]
        return down.reshape(n_sh, -1, down.shape[2])

    wbd = jnp.zeros((SB_WIDTH, SB_WIDTH), BF16)
    for g in range(4):
        sl = slice(g * POOL_GROUP, (g + 1) * POOL_GROUP)
        wbd = wbd.at[sl, sl].set(a_w_group[0, g].astype(BF16))

    xs, mems, tgt = x[0], mem[0], loss_target[0]

    memn = rmsnorm_fwd(mems, mem_norm, "memn")
    (h0, proj), = norm_matmul(xs, [(a_norm_mix_f, rows_full("a_w_in"), F32)], "proj_a")
    mkv_a = matmul([(memn, rows_full("a_w_mem_kv"))], "nn", BF16, "mkv_a")
    pooled, cat_a = pool_fwd(proj, wbd, a_scale_f, "pool_fwd")
    cat_a = mem_attn_fwd(proj, mkv_a, cat_a, "mem_fwd_a")
    gather_done(gather_1, EARLY_1, cat_a, "gather_1_wait")
    x1 = matmul([(cat_a, rows_full("a_w_out"))], "nn", F32, "out_a", res=xs)
    h1 = rmsnorm_fwd(x1, a_norm_ffn_f, "h1")
    gather_done(gather_2, ("a_w_gu",), h1, "gather_2_wait")
    wgu_a, wgu8_a = gu_views("a")
    gu_a, act_a = ffn_up(h1, wgu_a, "ffn_up_a")
    gather_done(gather_2b, ("a_w_down",), act_a, "gather_2b_wait")
    wdown_a = down_view("a")
    x2 = matmul([(act_a, wdown_a)], "nn", F32, "down_a", res=x1, roles=("reduce", "reduce"), count=n_sh)

    gather_done(gather_3, GATHER_3, x2, "gather_3_wait")
    wkv = jnp.transpose(gathered["w_kv"], (1, 0, 2)).reshape(shard2d["w_kv"][0], -1)
    (hk, kv), (h3, projb) = norm_matmul(
        x2, [(kv_norm, wkv, BF16), (b_norm_mix[0], rows_full("b_w_q"), F32)], "kv_proj_b")
    cat_b, csave, got = sb_fwd(projb, kv, "sb_fwd", gather=[shard[n] for n in GATHER_LATE])
    gathered.update(zip(GATHER_LATE, got))
    mkv_b = matmul([(memn, rows_full("b_w_mem_kv"))], "nn", BF16, "mkv_b")
    cat_b = mem_attn_fwd(projb, mkv_b, cat_b, "mem_fwd_b")
    x3 = matmul([(cat_b, rows_full("b_w_out"))], "nn", F32, "out_b", res=x2)
    h4 = rmsnorm_fwd(x3, b_norm_ffn[0], "h4")
    (wgu_b, wgu8_b), wdown_b = gu_views("b"), down_view("b")
    gu_b, act_b = ffn_up(h4, wgu_b, "ffn_up_b")
    x4 = matmul([(act_b, wdown_b)], "nn", F32, "down_b", res=x3, roles=("reduce", "reduce"), count=n_sh)

    grads = {}
    dx4, dx4_bf, grads["final_norm"], loss_part = loss_head(x4, tgt, final_norm, "loss_head")

    def blocks(n, g):
        return g.reshape((N_DEV,) + tuple(shard2d[n]))

    def ffn_weight_grads(tag, dx_out_bf, h, gu, act, wgu8, wdown):
        fs = wgu8.shape[2]
        dgu8 = ffn_bwd_act(dx_out_bf, wdown, gu, "ffn_bwd_act_" + tag).reshape(N_DEV, s_len, fs)
        d_wdown = matmul([(act, dx_out_bf)], "tn", BF16, "dw_down_" + tag, tm=fs, roles=("batch", None), count=n_sh)
        grads[tag + "_w_gu"] = matmul([(dgu8, h)], "tn", BF16, "dw_gu_" + tag, tm=fs, roles=("batch", None), count=N_DEV)
        grads[tag + "_w_down"] = blocks(tag + "_w_down", d_wdown)
        return dgu8

    def ffn_input_grads(tag, x_in, dx_out, dgu8, wgu8, g_ffn):
        dh = matmul([(dgu8, wgu8)], "nt", F32, "dh_ffn_" + tag, tn=512, roles=("reduce", "reduce"), count=N_DEV)
        dx_in, dx_in_bf, dgs = rmsnorm_bwd(x_in, dx_out, [(g_ffn, dh)], "norm_bwd_ffn_" + tag)
        return dx_in, dx_in_bf, dgs[0]

    def grads_start(names, small, after, name):
        return exchange_start([grads[n] for n in names], small, after, name)

    def grads_done(handle, names, after, name):
        got = exchange_wait(handle, after, name)
        for n, g in zip(names, got):
            recv[n] = lax.dynamic_update_slice(g, lax.dynamic_slice(grads[n], (me, 0, 0), (1,) + g.shape[1:]), (me, 0, 0))
        return got

    recv = {}
    dgu8_b = ffn_weight_grads("b", dx4_bf, h4, gu_b, act_b, wgu8_b, wdown_b)
    dx3, dx3_bf, grads["b_norm_ffn"] = ffn_input_grads("b", x3, dx4, dgu8_b, wgu8_b, b_norm_ffn[0])

    dcat_b = matmul([(dx3_bf, rows_full("b_w_out"))], "nt", F32, "dcat_b")
    grads["b_w_out"] = blocks("b_w_out", matmul([(cat_b, dx3_bf)], "tn", BF16, "dw_out_b"))
    dprojb, dk, dv, recv_late = sb_bwd(projb, kv, dcat_b, csave, "sb_bwd", scatter=[grads[n] for n in LATE])
    dprojb, dmkv_b = mem_attn_bwd(projb, mkv_b, dcat_b, dprojb, "mem_bwd_b")
    grads["b_w_q"] = blocks("b_w_q", matmul([(h3, dprojb)], "tn", BF16, "dw_q_b"))
    d_wkv_t = jnp.concatenate([matmul([(dk, hk)], "tn", BF16, "dw_k", tm=SB_WIDTH),
                               matmul([(dv, hk)], "tn", BF16, "dw_v", tm=SB_WIDTH)], axis=0)
    grads["w_kv"] = d_wkv_t.reshape(N_DEV, -1, d_wkv_t.shape[1])
    dx2, dx2_bf, dgs = matmul_norm_bwd(
        x2, dx3, [(b_norm_mix[0], [(dprojb, rows_full("b_w_q"))]),
                  (kv_norm, [(dk, wkv[:, :SB_WIDTH]), (dv, wkv[:, SB_WIDTH:])])], "norm_bwd_x2")
    grads["b_norm_mix"], grads["kv_norm"] = dgs
    dmkv_b_bf = dmkv_b.astype(BF16)
    grads["b_w_mem_kv"] = blocks("b_w_mem_kv", matmul([(memn, dmkv_b_bf)], "tn", BF16, "dw_mkv_b"))

    grads_3 = grads_start(EARLY_3, [], grads["b_w_mem_kv"], "grads_3_start")
    dgu8_a = ffn_weight_grads("a", dx2_bf, h1, gu_a, act_a, wgu8_a, wdown_a)
    grads_2 = grads_start(EARLY_2, [], grads_3["token"], "grads_2_start")
    dx1, dx1_bf, grads["a_norm_ffn"] = ffn_input_grads(
        "a", x1, dx2, dgu8_a, wgu8_a, a_norm_ffn_f + grads_2["token"][0, 0])

    dcat_a = matmul([(dx1_bf, rows_full("a_w_out"))], "nt", F32, "dcat_a")
    grads["a_w_out"] = blocks("a_w_out", matmul([(cat_a, dx1_bf)], "tn", BF16, "dw_out_a"))
    dproj, d_wbd, grads["a_scale"] = pool_bwd(dcat_a, pooled, wbd, a_scale_f, "pool_bwd")
    grads["a_w_group"] = jnp.stack(
        [d_wbd[g * POOL_GROUP:(g + 1) * POOL_GROUP, g * POOL_GROUP:(g + 1) * POOL_GROUP] for g in range(4)])
    group_grads = grads["a_w_group"].reshape(-1, PACK_COLS).astype(BF16)
    grads_1 = exchange_start([grads[n] for n in EARLY_1], [group_grads], grads_2["token"], "grads_1_start")
    dproj, dmkv_a = mem_attn_bwd(proj, mkv_a, dcat_a, dproj, "mem_bwd_a")
    grads["a_w_in"] = blocks("a_w_in", matmul([(h0, dproj)], "tn", BF16, "dw_in_a"))
    dx, _, dgs = matmul_norm_bwd(
        xs, dx1, [(a_norm_mix_f + grads_1["token"][0, 0], [(dproj, rows_full("a_w_in"))])], "norm_bwd_x")
    grads["a_norm_mix"] = dgs[0]
    dmkv_a_bf = dmkv_a.astype(BF16)
    grads["a_w_mem_kv"] = blocks("a_w_mem_kv", matmul([(memn, dmkv_a_bf)], "tn", BF16, "dw_mkv_a"))
    dmemn = matmul([(dmkv_a_bf, rows_full("a_w_mem_kv")), (dmkv_b_bf, rows_full("b_w_mem_kv"))], "nt", F32, "dmemn")
    _, _, dgs = rmsnorm_bwd(mems, jnp.zeros_like(mems), [(mem_norm, dmemn)], "norm_bwd_mem")
    grads["mem_norm"] = dgs[0]

    small_names = SMALL_REPL + SMALL_SHARDED
    head_rows = 16

    def small_pack(rows_of):
        head = jnp.zeros((head_rows, PACK_COLS), F32)
        for r, (v, col) in enumerate(rows_of):
            v = v.reshape(1, -1)
            if col is None:
                head = head + jnp.pad(v, ((r, head_rows - 1 - r), (0, PACK_COLS - v.shape[1])))
            else:
                head = lax.dynamic_update_slice(head, v, (r, col))
        return head

    small_grads = small_pack([(grads[n], None) for n in small_names] + [(loss_part[:, :1], None)])
    grads_0 = grads_start(EARLY_0, [small_grads], dx, "grads_0_start")

    def small_state(src):
        return small_pack([(src[n], None) for n in SMALL_REPL]
                          + [(src[n], me * src[n].shape[-1]) for n in SMALL_SHARDED])

    def state2d(src, n):
        a = src[n].reshape(shard2d[n])
        return a.T if n in COL_SHARDED else a

    def adamw(names, after):
        for n in names:
            big_out[n] = reduce_adamw(recv[n], state2d(w, n), state2d(mom, n), state2d(var, n), "adamw_" + n, after=after)
        return big_out[names[-1]][0]

    big_out = {}
    recv.update(zip(LATE, recv_late))
    done = adamw(LATE, grads_0["token"])
    grads_done(grads_3, EARLY_3, done, "grads_3_wait")
    done = adamw(EARLY_3, None)
    grads_done(grads_2, EARLY_2, done, "grads_2_wait")
    done = adamw(EARLY_2, None)
    group_recv = own_slot(grads_done(grads_1, EARLY_1, done, "grads_1_wait")[-1], group_grads)
    done = adamw(EARLY_1, None)
    group_out = reduce_adamw(group_recv, *[src["a_w_group"].reshape(-1, PACK_COLS) for src in (w, mom, var)], "adamw_group")
    small_recv = own_slot(grads_done(grads_0, EARLY_0, group_out[0], "grads_0_wait")[-1], small_grads)
    adamw(EARLY_0, None)
    small_out = reduce_adamw(small_recv, small_state(w), small_state(mom), small_state(var), "adamw_small")

    def unpack(kind):
        out = {n: (big_out[n][kind].T if n in COL_SHARDED else big_out[n][kind]).reshape(w[n].shape) for n in BIG}
        so = small_out[kind]
        for r, n in enumerate(SMALL_REPL):
            out[n] = so[r, :w[n].shape[-1]].reshape(w[n].shape)
        for r, n in enumerate(SMALL_SHARDED):
            width = w[n].shape[-1]
            out[n] = lax.dynamic_slice(so, (len(SMALL_REPL) + r, me * width), (1, width)).reshape(w[n].shape)
        out["a_w_group"] = group_out[kind].reshape(a_w_group.shape)
        return out

    loss = small_out[0][len(small_names), 0]
    results = [loss, dx.reshape(x.shape)]
    for kind in range(4):
        out = unpack(kind)
        results += [out[n] for n in WEIGHTS]
    return tuple(results)
```

```python
import functools

import jax
import jax.numpy as jnp
from jax import lax
from jax.experimental import pallas as pl
from jax.experimental.pallas import tpu as pltpu

F32 = jnp.float32
BF16 = jnp.bfloat16

N_DEV = 8
HEAD_DIM = 64
N_SB_HEADS = 12
SB_WIDTH = N_SB_HEADS * HEAD_DIM
MEM_WIDTH = 4 * HEAD_DIM
POOL_WINDOWS = (2, 4, 8, 16)
POOL_GROUP = SB_WIDTH // 4
POOL_HALO = 16
EPS = 1e-6
QK_SCALE = HEAD_DIM ** -0.5
LOG2_E = 1.4426950408889634
LANES = 128
PACK_COLS = 1024

ADAM_LR = 0.001
ADAM_B1 = 0.9
ADAM_B2 = 0.999
ADAM_EPS = 1e-08
ADAM_WD = 0.01
ADAM_STEP = 10

VMEM_LIMIT = 48 * 1024 * 1024

_NT = (((1,), (1,)), ((), ()))
_TN = (((0,), (0,)), ((), ()))
_NN = (((1,), (0,)), ((), ()))


def _params(*sem):
    return pltpu.CompilerParams(dimension_semantics=sem, vmem_limit_bytes=VMEM_LIMIT)


def _dot(a, b, dims=_NN):
    return lax.dot_general(a, b, dims, preferred_element_type=F32)


def exchange(a2a, ag, name):
    n_a2a, n_arr = len(a2a), len(a2a) + len(ag)

    def body(*refs):
        start, wait = _exchange_ops(refs[:n_arr], refs[n_arr:2 * n_arr], refs[2 * n_arr:], n_a2a)
        start()
        wait()

    return pl.pallas_call(
        body, name=name, out_shape=_exchange_out_shape(a2a, ag),
        in_specs=[_ANY] * n_arr, out_specs=[_ANY] * n_arr,
        scratch_shapes=_exchange_sems(n_arr),
    )(*a2a, *ag)


_ANY = pl.BlockSpec(memory_space=pl.ANY)
_HBM = pl.BlockSpec(memory_space=pltpu.HBM)
_SEM = pl.BlockSpec(memory_space=pltpu.SEMAPHORE)
_DATAFLOW = pltpu.SideEffectType.DATAFLOW_SIDE_EFFECTING


def exchange_start(a2a, ag, after, name):
    n_a2a, n_arr = len(a2a), len(a2a) + len(ag)
    arrays = [pltpu.with_memory_space_constraint(v, pltpu.HBM) for v in list(a2a) + list(ag)]
    shapes = _exchange_out_shape(a2a, ag)
    lands = [pltpu.with_memory_space_constraint(lax.empty(sh.shape, sh.dtype), pltpu.HBM) for sh in shapes]

    def body(*refs):
        ins, outs = refs[:n_arr], refs[n_arr:2 * n_arr]
        send_sems, recv_sems, token = refs[4 * n_arr + 1:4 * n_arr + 4]
        start, _ = _exchange_ops(ins, outs, (send_sems, recv_sems, None), n_a2a, local_copies=False)
        start()
        token[...] = jnp.zeros_like(token)

    outs = pl.pallas_call(
        body, name=name,
        out_shape=[pltpu.HBM(v.shape, v.dtype) for v in arrays] + [pltpu.HBM(sh.shape, sh.dtype) for sh in shapes]
        + [pltpu.SemaphoreType.DMA((n_arr * (N_DEV - 1),)), pltpu.SemaphoreType.DMA((n_arr * (N_DEV - 1),)),
           jax.ShapeDtypeStruct((8, LANES), F32)],
        in_specs=[_HBM] * (2 * n_arr) + [_ANY],
        out_specs=[_HBM] * (2 * n_arr) + [_SEM, _SEM, pl.BlockSpec(memory_space=pltpu.VMEM)],
        input_output_aliases={i: i for i in range(2 * n_arr)},
        compiler_params=pltpu.CompilerParams(has_side_effects=_DATAFLOW),
    )(*arrays, *lands, after)
    return dict(ins=outs[:n_arr], lands=outs[n_arr:2 * n_arr], sems=outs[2 * n_arr:2 * n_arr + 2], n_a2a=n_a2a,
                token=outs[2 * n_arr + 2])


def exchange_wait(handle, after, name):
    ins, lands, sems, n_a2a = handle["ins"], handle["lands"], handle["sems"], handle["n_a2a"]
    n_arr = len(ins)

    def body(*refs):
        in_refs, land_refs = refs[:n_arr], refs[n_arr:2 * n_arr]
        send_sems, recv_sems = refs[2 * n_arr:2 * n_arr + 2]
        _, wait = _exchange_ops(in_refs, land_refs, (send_sems, recv_sems, None), n_a2a, local_copies=False)
        wait()

    outs = pl.pallas_call(
        body, name=name,
        out_shape=[pltpu.HBM(v.shape, v.dtype) for v in list(ins) + list(lands)],
        in_specs=[_HBM] * (2 * n_arr) + [_SEM, _SEM, _ANY], out_specs=[_HBM] * (2 * n_arr),
        input_output_aliases={i: i for i in range(2 * n_arr)},
        compiler_params=pltpu.CompilerParams(has_side_effects=_DATAFLOW),
    )(*ins, *lands, *sems, after)
    return outs[n_arr:]


def _exchange_out_shape(a2a, ag):
    out_shape = [jax.ShapeDtypeStruct(v.shape, v.dtype) for v in a2a]
    return out_shape + [jax.ShapeDtypeStruct((N_DEV,) + v.shape, v.dtype) for v in ag]


def _exchange_sems(n_arr):
    return [pltpu.SemaphoreType.DMA((n_arr * (N_DEV - 1),)), pltpu.SemaphoreType.DMA((n_arr * (N_DEV - 1),)),
            pltpu.SemaphoreType.DMA((n_arr,))]


def _exchange_ops(ins, outs, sems, n_a2a, local_copies=True):
    send_sems, recv_sems, local_sems = sems
    n_arr = len(ins)
    x, y, c = lax.axis_index("x"), lax.axis_index("y"), lax.axis_index("c")
    me = 4 * x + 2 * y + c

    def peer_of(k):
        px = 1 - x if k & 4 else x
        py = 1 - y if k & 2 else y
        pc = 1 - c if k & 1 else c
        return (px, py, pc), 4 * px + 2 * py + pc

    def copy(a, k, landing):
        peer, pid = peer_of(k)
        src = ins[a].at[pid] if a < n_a2a else ins[a]
        return pltpu.make_async_remote_copy(
            src_ref=src, dst_ref=outs[a].at[pid if landing else me], send_sem=send_sems.at[a * (N_DEV - 1) + k - 1],
            recv_sem=recv_sems.at[a * (N_DEV - 1) + k - 1], device_id=peer, device_id_type=pl.DeviceIdType.MESH)

    def local(a):
        src = ins[a].at[me] if a < n_a2a else ins[a]
        return pltpu.make_async_copy(src, outs[a].at[me], local_sems.at[a])

    order = (6, 7, 4, 5, 2, 3, 1)

    def start():
        if local_copies:
            for a in range(n_arr):
                local(a).start()
        for a in range(n_arr):
            for k in order:
                copy(a, k, landing=False).start()

    def wait():
        for a in range(n_arr):
            for k in order:
                copy(a, k, landing=True).wait_recv()
        for a in range(n_arr):
            for k in order:
                copy(a, k, landing=False).wait_send()
        if local_copies:
            for a in range(n_arr):
                local(a).wait()

    return start, wait


def _rows_tile(n, want):
    t = min(want, n)
    assert n % t == 0, (n, t)
    return t


def rmsnorm_fwd(x, g, name):
    s, d = x.shape
    tm = _rows_tile(s, 512)

    def body(x_ref, g_ref, h_ref):
        xv = x_ref[...]
        r = lax.rsqrt(jnp.mean(xv * xv, axis=-1, keepdims=True) + EPS)
        h_ref[...] = ((xv * r) * g_ref[...]).astype(h_ref.dtype)

    return pl.pallas_call(
        body, name=name, grid=(s // tm,),
        in_specs=[pl.BlockSpec((tm, d), lambda i: (i, 0)), pl.BlockSpec((1, d), lambda i: (0, 0))],
        out_specs=pl.BlockSpec((tm, d), lambda i: (i, 0)),
        out_shape=jax.ShapeDtypeStruct((s, d), BF16),
        compiler_params=_params("parallel"),
    )(x, g.reshape(1, d))


def norm_matmul(x, heads, name):
    s, d = x.shape
    tm = _rows_tile(s, 512)
    n = len(heads)

    def body(*refs):
        x_ref = refs[0]
        xv = x_ref[...]
        xhat = xv * lax.rsqrt(jnp.mean(xv * xv, axis=-1, keepdims=True) + EPS)
        for k in range(n):
            g_ref, w_ref = refs[1 + 2 * k], refs[2 + 2 * k]
            h_ref, y_ref = refs[1 + 2 * n + 2 * k], refs[2 + 2 * n + 2 * k]
            h = (xhat * g_ref[...]).astype(BF16)
            h_ref[...] = h
            y_ref[...] = _dot(h, w_ref[...]).astype(y_ref.dtype)

    row = pl.BlockSpec((tm, d), lambda i: (i, 0))
    in_specs, out_specs, out_shape, args = [row], [], [], [x]
    for g, w_k, dtype in heads:
        in_specs += [pl.BlockSpec((1, d), lambda i: (0, 0)), pl.BlockSpec(w_k.shape, lambda i: (0, 0))]
        args += [g.reshape(1, d), w_k]
        out_specs += [row, pl.BlockSpec((tm, w_k.shape[1]), lambda i: (i, 0))]
        out_shape += [jax.ShapeDtypeStruct((s, d), BF16), jax.ShapeDtypeStruct((s, w_k.shape[1]), dtype)]
    outs = pl.pallas_call(
        body, name=name, grid=(s // tm,), in_specs=in_specs, out_specs=out_specs, out_shape=out_shape,
        compiler_params=_params("parallel"),
    )(*args)
    return [(outs[2 * k], outs[2 * k + 1]) for k in range(n)]


def rmsnorm_bwd(x, dres, pairs, name):
    s, d = x.shape
    tm = _rows_tile(s, 256)
    n = len(pairs)

    def body(*refs):
        x_ref, dres_ref = refs[0], refs[1]
        g_refs, dh_refs = refs[2:2 + n], refs[2 + n:2 + 2 * n]
        dx_ref, dxb_ref = refs[2 + 2 * n], refs[3 + 2 * n]
        dg_refs = refs[4 + 2 * n:]
        i = pl.program_id(0)
        xv = x_ref[...]
        r = lax.rsqrt(jnp.mean(xv * xv, axis=-1, keepdims=True) + EPS)
        xhat = xv * r
        acc = dres_ref[...]
        for k in range(n):
            dh = dh_refs[k][...].astype(F32)
            part = jnp.sum(dh * xhat, axis=0, keepdims=True)

            @pl.when(i == 0)
            def _(k=k, part=part):
                dg_refs[k][...] = part

            @pl.when(i > 0)
            def _(k=k, part=part):
                dg_refs[k][...] += part

            dxh = dh * g_refs[k][...]
            acc = acc + r * (dxh - xhat * jnp.mean(dxh * xhat, axis=-1, keepdims=True))
        dx_ref[...] = acc
        dxb_ref[...] = acc.astype(BF16)

    row = pl.BlockSpec((tm, d), lambda i: (i, 0))
    vec = pl.BlockSpec((1, d), lambda i: (0, 0))
    outs = pl.pallas_call(
        body, name=name, grid=(s // tm,),
        in_specs=[row, row] + [vec] * n + [row] * n,
        out_specs=[row, row] + [vec] * n,
        out_shape=[jax.ShapeDtypeStruct((s, d), F32), jax.ShapeDtypeStruct((s, d), BF16)]
        + [jax.ShapeDtypeStruct((1, d), F32)] * n,
        compiler_params=_params("arbitrary"),
    )(x, dres, *[g.reshape(1, d) for g, _ in pairs], *[dh for _, dh in pairs])
    return outs[0], outs[1], outs[2:]


def matmul_norm_bwd(x, dres, heads, name):
    s, d = x.shape
    tm = _rows_tile(s, 512)
    n = len(heads)
    n_ops = [len(pairs) for _, pairs in heads]

    def body(*refs):
        x_ref, dres_ref = refs[0], refs[1]
        n_in = 2 + n + 2 * sum(n_ops)
        dx_ref, dxb_ref = refs[n_in], refs[n_in + 1]
        dg_refs = refs[n_in + 2:]
        i = pl.program_id(0)
        xv = x_ref[...]
        r = lax.rsqrt(jnp.mean(xv * xv, axis=-1, keepdims=True) + EPS)
        xhat = xv * r
        acc = dres_ref[...]
        pos = 2 + n
        for k in range(n):
            dh = None
            for a, _ in heads[k][1]:
                a_ref, w_ref = refs[pos], refs[pos + 1]
                for kk in range(a.shape[0] if a.ndim == 3 else 1):
                    prod = _dot(a_ref[kk], w_ref[kk], _NT) if a.ndim == 3 else _dot(a_ref[...], w_ref[...], _NT)
                    dh = prod if dh is None else dh + prod
                pos += 2
            part = jnp.sum(dh * xhat, axis=0, keepdims=True)

            @pl.when(i == 0)
            def _(k=k, part=part):
                dg_refs[k][...] = part

            @pl.when(i > 0)
            def _(k=k, part=part):
                dg_refs[k][...] += part

            dxh = dh * refs[2 + k][...]
            acc = acc + r * (dxh - xhat * jnp.mean(dxh * xhat, axis=-1, keepdims=True))
        dx_ref[...] = acc
        dxb_ref[...] = acc.astype(BF16)

    row = pl.BlockSpec((tm, d), lambda i: (i, 0))
    vec = pl.BlockSpec((1, d), lambda i: (0, 0))
    in_specs, args = [row, row] + [vec] * n, [x, dres] + [g.reshape(1, d) for g, _ in heads]
    for _, pairs in heads:
        for a, w_k in pairs:
            if a.ndim == 3:
                in_specs += [pl.BlockSpec((a.shape[0], tm, a.shape[2]), lambda i: (0, i, 0)),
                             pl.BlockSpec(w_k.shape, lambda i: (0, 0, 0), pipeline_mode=pl.Buffered(1))]
            else:
                in_specs += [pl.BlockSpec((tm, a.shape[1]), lambda i: (i, 0)),
                             pl.BlockSpec(w_k.shape, lambda i: (0, 0), pipeline_mode=pl.Buffered(1))]
            args += [a, w_k]
    outs = pl.pallas_call(
        body, name=name, grid=(s // tm,), in_specs=in_specs, out_specs=[row, row] + [vec] * n,
        out_shape=[jax.ShapeDtypeStruct((s, d), F32), jax.ShapeDtypeStruct((s, d), BF16)]
        + [jax.ShapeDtypeStruct((1, d), F32)] * n,
        compiler_params=_params("arbitrary"),
    )(*args)
    return outs[0], outs[1], outs[2:]


def loss_head(x, tgt, g, name):
    s, d = x.shape
    tm = _rows_tile(s, 256)

    def body(x_ref, t_ref, g_ref, dx_ref, dxb_ref, dg_ref, loss_ref):
        i = pl.program_id(0)
        xv = x_ref[...]
        gv = g_ref[...]
        r = lax.rsqrt(jnp.mean(xv * xv, axis=-1, keepdims=True) + EPS)
        xhat = xv * r
        diff = xhat * gv - t_ref[...]
        part_loss = 0.5 * jnp.sum(jnp.mean(diff * diff, axis=-1, keepdims=True), axis=0, keepdims=True)
        dy = diff * (1.0 / d)
        part_dg = jnp.sum(dy * xhat, axis=0, keepdims=True)

        @pl.when(i == 0)
        def _():
            dg_ref[...] = part_dg
            loss_ref[...] = jnp.broadcast_to(part_loss, loss_ref.shape)

        @pl.when(i > 0)
        def _():
            dg_ref[...] += part_dg
            loss_ref[...] += jnp.broadcast_to(part_loss, loss_ref.shape)

        dxh = dy * gv
        dx = r * (dxh - xhat * jnp.mean(dxh * xhat, axis=-1, keepdims=True))
        dx_ref[...] = dx
        dxb_ref[...] = dx.astype(BF16)

    row = pl.BlockSpec((tm, d), lambda i: (i, 0))
    vec = pl.BlockSpec((1, d), lambda i: (0, 0))
    return pl.pallas_call(
        body, name=name, grid=(s // tm,),
        in_specs=[row, row, vec], out_specs=[row, row, vec, pl.BlockSpec((1, LANES), lambda i: (0, 0))],
        out_shape=[jax.ShapeDtypeStruct((s, d), F32), jax.ShapeDtypeStruct((s, d), BF16),
                   jax.ShapeDtypeStruct((1, d), F32), jax.ShapeDtypeStruct((1, LANES), F32)],
        compiler_params=_params("arbitrary"),
    )(x, tgt, g.reshape(1, d))


def matmul(pairs, mode, out_dtype, name, res=None, tm=512, tn=None, roles=(None, None), count=1):
    a0, b0 = pairs[0]
    m = a0.shape[-1] if mode == "tn" else a0.shape[-2]
    n = b0.shape[-2] if mode == "nt" else b0.shape[-1]
    if tn is None:
        tn = n if (mode != "tn" and n <= 1024) else 512
    tm, tn = _rows_tile(m, tm), _rows_tile(n, tn)
    dims = {"nn": _NN, "nt": _NT, "tn": _TN}[mode]
    npairs = len(pairs)
    batched, reducing = "batch" in roles, "reduce" in roles
    assert not (batched and reducing) and (res is None or not batched)
    n_in = 2 * npairs + (res is not None)

    def body(*refs):
        o_ref = refs[n_in]
        acc = None
        for p in range(npairs):
            a_ref, b_ref = refs[2 * p], refs[2 * p + 1]
            for kk in range(count if reducing else 1):
                a = a_ref[kk] if roles[0] == "reduce" else a_ref[...]
                b = b_ref[kk] if roles[1] == "reduce" else b_ref[...]
                d = _dot(a.astype(BF16), b.astype(BF16), dims)
                acc = d if acc is None else acc + d
        if res is not None:
            acc = acc + refs[2 * npairs][...]
        o_ref[...] = acc.astype(o_ref.dtype)

    def spec(arr, role, block, index):
        if arr.ndim == 2:
            return pl.BlockSpec(block, lambda bb, i, j: index(i, j))
        assert role in ("batch", "reduce") and arr.shape[0] == count, (name, role, arr.shape)
        if role == "batch":
            return pl.BlockSpec((None,) + block, lambda bb, i, j: (bb,) + index(i, j))
        return pl.BlockSpec((count,) + block, lambda bb, i, j: (0,) + index(i, j))

    in_specs, args = [], []
    for a, b in pairs:
        if mode == "tn":
            in_specs.append(spec(a, roles[0], (a.shape[-2], tm), lambda i, j: (0, i)))
        else:
            in_specs.append(spec(a, roles[0], (tm, a.shape[-1]), lambda i, j: (i, 0)))
        if mode == "nt":
            in_specs.append(spec(b, roles[1], (tn, b.shape[-1]), lambda i, j: (j, 0)))
        else:
            in_specs.append(spec(b, roles[1], (b.shape[-2], tn), lambda i, j: (0, j)))
        args += [a, b]
    if res is not None:
        in_specs.append(pl.BlockSpec((tm, tn), lambda bb, i, j: (i, j)))
        args.append(res)
    if batched:
        out_spec = pl.BlockSpec((None, tm, tn), lambda bb, i, j: (bb, i, j))
        out_shape = jax.ShapeDtypeStruct((count, m, n), out_dtype)
    else:
        out_spec = pl.BlockSpec((tm, tn), lambda bb, i, j: (i, j))
        out_shape = jax.ShapeDtypeStruct((m, n), out_dtype)
    return pl.pallas_call(
        body, name=name, grid=(count if batched else 1, m // tm, n // tn),
        in_specs=in_specs, out_specs=out_spec, out_shape=out_shape,
        compiler_params=_params("parallel", "parallel", "parallel"),
    )(*args)


def _sigmoid(v):
    return 0.5 * jnp.tanh(0.5 * v) + 0.5


def ffn_up(h, wgu, name):
    s, d = h.shape
    _, n_sh, _, fs = wgu.shape
    tm = _rows_tile(s, 512)

    def body(h_ref, w_ref, gu_ref, act_ref):
        hv = h_ref[...]
        gate = _dot(hv, w_ref[0])
        up = _dot(hv, w_ref[1])
        gu_ref[0] = gate.astype(BF16)
        gu_ref[1] = up.astype(BF16)
        act_ref[...] = ((gate * _sigmoid(gate)) * up).astype(BF16)

    return pl.pallas_call(
        body, name=name, grid=(n_sh, s // tm),
        in_specs=[pl.BlockSpec((tm, d), lambda j, i: (i, 0)), pl.BlockSpec((2, None, d, fs), lambda j, i: (0, j, 0, 0))],
        out_specs=[pl.BlockSpec((2, None, tm, fs), lambda j, i: (0, j, i, 0)),
                   pl.BlockSpec((None, tm, fs), lambda j, i: (j, i, 0))],
        out_shape=[jax.ShapeDtypeStruct((2, n_sh, s, fs), BF16), jax.ShapeDtypeStruct((n_sh, s, fs), BF16)],
        compiler_params=_params("parallel", "parallel"),
    )(h, wgu)


def ffn_bwd_act(dxo, wdown, gu, name):
    s, d = dxo.shape
    n_sh, fs, _ = wdown.shape
    tm = _rows_tile(s, 512)

    def body(dx_ref, w_ref, gu_ref, dgu_ref):
        dact = _dot(dx_ref[...], w_ref[...], _NT)
        gate = gu_ref[0].astype(F32)
        sig = _sigmoid(gate)
        dgu_ref[0] = ((dact * gu_ref[1].astype(F32)) * (sig * (1.0 + gate * (1.0 - sig)))).astype(BF16)
        dgu_ref[1] = (dact * (gate * sig)).astype(BF16)

    tile = pl.BlockSpec((2, None, tm, fs), lambda j, i: (0, j, i, 0))
    return pl.pallas_call(
        body, name=name, grid=(n_sh, s // tm),
        in_specs=[pl.BlockSpec((tm, d), lambda j, i: (i, 0)), pl.BlockSpec((None, fs, d), lambda j, i: (j, 0, 0)), tile],
        out_specs=tile,
        out_shape=jax.ShapeDtypeStruct((2, n_sh, s, fs), BF16),
        compiler_params=_params("parallel", "parallel"),
    )(dxo, wdown, gu)


def _pool_select(col, by_window):
    out = by_window[3]
    for g in (2, 1, 0):
        out = jnp.where(col < (g + 1) * POOL_GROUP, by_window[g], out)
    return out


def _pool_counts(pos):
    return [jnp.minimum(pos + 1, w).astype(F32) for w in POOL_WINDOWS]


def pool_fwd(proj, wbd, scale, name):
    s = proj.shape[0]
    tm = _rows_tile(s, 512)
    nt = s // tm
    width = SB_WIDTH
    per = tm // POOL_HALO

    def body(u_ref, halo_ref, w_ref, sc_ref, pooled_ref, grouped_ref):
        i = pl.program_id(0)
        u = u_ref[...]
        halo = jnp.where(i > 0, halo_ref[...], 0.0)
        ext = jnp.concatenate([halo, u], axis=0)
        sums = []
        acc = ext
        for k in (1, 2, 4, 8):
            acc = acc + pltpu.roll(acc, k, 0)
            sums.append(acc[POOL_HALO:])
        pos = i * tm + lax.broadcasted_iota(jnp.int32, (tm, width), 0)
        col = lax.broadcasted_iota(jnp.int32, (tm, width), 1)
        means = [sm / cnt for sm, cnt in zip(sums, _pool_counts(pos))]
        pooled = (_pool_select(col, means) - u).astype(BF16)
        pooled_ref[...] = pooled
        grouped_ref[...] = (_dot(pooled, w_ref[...]) * sc_ref[...]).astype(BF16)

    row = pl.BlockSpec((tm, width), lambda i: (i, 0))
    return pl.pallas_call(
        body, name=name, grid=(nt,),
        in_specs=[row, pl.BlockSpec((POOL_HALO, width), lambda i: (jnp.maximum(i * per - 1, 0), 0)),
                  pl.BlockSpec((width, width), lambda i: (0, 0)), pl.BlockSpec((1, width), lambda i: (0, 0))],
        out_specs=[row, row],
        out_shape=[jax.ShapeDtypeStruct((s, width), BF16), jax.ShapeDtypeStruct((s, width + MEM_WIDTH), BF16)],
        compiler_params=_params("parallel"),
    )(proj, proj, wbd, scale.reshape(1, width))


def pool_bwd(dcat, pooled, wbd, scale, name):
    s = dcat.shape[0]
    tm = _rows_tile(s, 512)
    nt = s // tm
    width = SB_WIDTH
    per = tm // POOL_HALO
    n_ext = tm + POOL_HALO

    def body(dg_ref, halo_ref, pooled_ref, w_ref, sc_ref, du_ref, dw_ref, dsc_ref):
        i = pl.program_id(0)
        w = w_ref[...]
        sc = sc_ref[...]
        dg = dg_ref[...]
        pooled = pooled_ref[...]
        pg = _dot(pooled, w)
        dpg = (dg * sc).astype(BF16)
        part_sc = jnp.sum(dg * pg, axis=0, keepdims=True)
        part_w = _dot(pooled, dpg, _TN)

        @pl.when(i == 0)
        def _():
            dsc_ref[...] = part_sc
            dw_ref[...] = part_w

        @pl.when(i > 0)
        def _():
            dsc_ref[...] += part_sc
            dw_ref[...] += part_w

        dpooled = _dot(dpg, w, _NT)
        halo_dpg = (jnp.where(i < nt - 1, halo_ref[...], 0.0) * sc).astype(BF16)
        ext = jnp.concatenate([dpooled, _dot(halo_dpg, w, _NT)], axis=0)
        pos = i * tm + lax.broadcasted_iota(jnp.int32, (n_ext, width), 0)
        col = lax.broadcasted_iota(jnp.int32, (tm, width), 1)
        outs = []
        for k_idx, cnt in enumerate(_pool_counts(pos)):
            acc = ext / cnt
            for k in (1, 2, 4, 8)[:k_idx + 1]:
                acc = acc + pltpu.roll(acc, n_ext - k, 0)
            outs.append(acc[:tm])
        du_ref[...] = (_pool_select(col, outs) - dpooled).astype(BF16)

    row = pl.BlockSpec((tm, width), lambda i: (i, 0))
    return pl.pallas_call(
        body, name=name, grid=(nt,),
        in_specs=[row, pl.BlockSpec((POOL_HALO, width), lambda i: (jnp.minimum((i + 1) * per, s // POOL_HALO - 1), 0)),
                  row, pl.BlockSpec((width, width), lambda i: (0, 0)), pl.BlockSpec((1, width), lambda i: (0, 0))],
        out_specs=[row, pl.BlockSpec((width, width), lambda i: (0, 0)), pl.BlockSpec((1, width), lambda i: (0, 0))],
        out_shape=[jax.ShapeDtypeStruct((s, width + MEM_WIDTH), BF16), jax.ShapeDtypeStruct((width, width), F32),
                   jax.ShapeDtypeStruct((1, width), F32)],
        compiler_params=_params("arbitrary"),
    )(dcat, dcat, pooled, wbd, scale.reshape(1, width))


def _head_masks(shape):
    lane = lax.broadcasted_iota(jnp.int32, shape, 1)
    return lane < HEAD_DIM, lane >= HEAD_DIM


def _mem_probs(qh, kp):
    logits = _dot(qh, kp, _NT)
    logits = logits - jnp.max(logits, axis=-1, keepdims=True)
    p = jnp.exp(logits)
    return p / jnp.sum(p, axis=-1, keepdims=True)


def mem_attn_fwd(proj, mkv, cat, name):
    s = proj.shape[0]
    m = mkv.shape[0]
    tm = _rows_tile(s, 512)
    q_blk = SB_WIDTH // MEM_WIDTH

    def body(q_ref, kv_ref, _, o_ref):
        masks = _head_masks((tm, LANES))
        for pr in range(MEM_WIDTH // LANES):
            q = q_ref[:, pr * LANES:(pr + 1) * LANES] * QK_SCALE
            kp = kv_ref[:, pr * LANES:(pr + 1) * LANES]
            vp = kv_ref[:, MEM_WIDTH + pr * LANES:MEM_WIDTH + (pr + 1) * LANES]
            outs = []
            for hm in masks:
                p = _mem_probs(jnp.where(hm, q, 0.0).astype(BF16), kp)
                outs.append(_dot(p.astype(BF16), vp))
            o_ref[:, pr * LANES:(pr + 1) * LANES] = jnp.where(masks[0], outs[0], outs[1]).astype(BF16)

    return pl.pallas_call(
        body, name=name, grid=(s // tm,),
        in_specs=[pl.BlockSpec((tm, MEM_WIDTH), lambda i: (i, q_blk)), pl.BlockSpec((m, 2 * MEM_WIDTH), lambda i: (0, 0)),
                  _ANY],
        out_specs=pl.BlockSpec((tm, MEM_WIDTH), lambda i: (i, q_blk)),
        out_shape=jax.ShapeDtypeStruct(cat.shape, BF16),
        input_output_aliases={2: 0},
        compiler_params=_params("parallel"),
    )(proj, mkv, cat)


def mem_attn_bwd(proj, mkv, dcat, dproj, name):
    s = proj.shape[0]
    m = mkv.shape[0]
    tm = _rows_tile(s, 512)
    q_blk = SB_WIDTH // MEM_WIDTH

    def body(q_ref, kv_ref, do_ref, _, dq_ref, dkv_ref):
        i = pl.program_id(0)

        @pl.when(i == 0)
        def _():
            dkv_ref[...] = jnp.zeros_like(dkv_ref)

        masks = _head_masks((tm, LANES))
        for pr in range(MEM_WIDTH // LANES):
            q = q_ref[:, pr * LANES:(pr + 1) * LANES] * QK_SCALE
            do = do_ref[:, pr * LANES:(pr + 1) * LANES]
            kp = kv_ref[:, pr * LANES:(pr + 1) * LANES]
            vp = kv_ref[:, MEM_WIDTH + pr * LANES:MEM_WIDTH + (pr + 1) * LANES]
            dqs = []
            dk = jnp.zeros((m, LANES), F32)
            dv = jnp.zeros((m, LANES), F32)
            for hm in masks:
                qh = jnp.where(hm, q, 0.0).astype(BF16)
                doh = jnp.where(hm, do, 0.0).astype(BF16)
                p = _mem_probs(qh, kp)
                dp = _dot(doh, vp, _NT)
                ds = (p * (dp - jnp.sum(dp * p, axis=-1, keepdims=True))).astype(BF16)
                dqs.append(_dot(ds, kp))
                dk = dk + _dot(ds, qh, _TN)
                dv = dv + _dot(p.astype(BF16), doh, _TN)
            dq_ref[:, pr * LANES:(pr + 1) * LANES] = (jnp.where(masks[0], dqs[0], dqs[1]) * QK_SCALE).astype(BF16)
            dkv_ref[:, pr * LANES:(pr + 1) * LANES] += dk
            dkv_ref[:, MEM_WIDTH + pr * LANES:MEM_WIDTH + (pr + 1) * LANES] += dv

    return pl.pallas_call(
        body, name=name, grid=(s // tm,),
        in_specs=[pl.BlockSpec((tm, MEM_WIDTH), lambda i: (i, q_blk)), pl.BlockSpec((m, 2 * MEM_WIDTH), lambda i: (0, 0)),
                  pl.BlockSpec((tm, MEM_WIDTH), lambda i: (i, q_blk)), _ANY],
        out_specs=[pl.BlockSpec((tm, MEM_WIDTH), lambda i: (i, q_blk)), pl.BlockSpec((m, 2 * MEM_WIDTH), lambda i: (0, 0))],
        out_shape=[jax.ShapeDtypeStruct(dproj.shape, BF16), jax.ShapeDtypeStruct((m, 2 * MEM_WIDTH), F32)],
        input_output_aliases={3: 0},
        compiler_params=_params("arbitrary"),
    )(proj, mkv, dcat, dproj)


SB_ROWS = 512
SB_KEYS = 512
SB_SUB = LANES


def _sb_tri(later):
    r = lax.broadcasted_iota(jnp.int32, (2 * SB_SUB, 2 * SB_SUB), 0) % SB_SUB
    c = lax.broadcasted_iota(jnp.int32, (2 * SB_SUB, 2 * SB_SUB), 1)
    tri = (r > c) if later else (r < c)
    return (tri | (c >= SB_SUB)).astype(BF16)


def _sb_sums(v, tri, n_sub):
    out = []
    for m in range(n_sub):
        vm = v[:, m * SB_SUB:(m + 1) * SB_SUB]
        hi = vm.astype(BF16)
        lo = (vm - hi.astype(F32)).astype(BF16)
        r = _dot(jnp.concatenate([hi, lo], axis=1), tri)
        out.append((r[:, :SB_SUB], r[:, SB_SUB:]))
    return out


def _sb_tri_pair_before():
    r = lax.broadcasted_iota(jnp.int32, (2 * SB_SUB, 2 * SB_SUB), 0)
    c = lax.broadcasted_iota(jnp.int32, (2 * SB_SUB, 2 * SB_SUB), 1)
    return ((r // SB_SUB == c // SB_SUB) & (r % SB_SUB < c % SB_SUB)).astype(BF16)


def _sb_prefix_pairs(e, tri_pair, n_sub):
    out = []
    for m in range(0, n_sub, 2):
        pair = e[:, m * SB_SUB:(m + 2) * SB_SUB]
        r = _dot(pair.astype(BF16), tri_pair)
        for h in range(2):
            em = pair[:, h * SB_SUB:(h + 1) * SB_SUB]
            out.append((r[:, h * SB_SUB:(h + 1) * SB_SUB], jnp.sum(em, axis=1, keepdims=True)))
    return out


def _sb_logs(nqh, k_blk, valid):
    nz = _dot(nqh, k_blk, _NT)
    log_not = jnp.minimum(nz, 0.0) - jnp.log(1.0 + jnp.exp2(jnp.abs(nz) * -LOG2_E))
    log_beta = log_not - nz
    if valid is not None:
        log_not = jnp.where(valid, log_not, 0.0)
    return log_not, log_beta


def _sb_weights(log_not, log_beta, tri_later, carry_in, valid, n_sub):
    sums = _sb_sums(log_not, tri_later, n_sub)
    run = carry_in
    later = [None] * n_sub
    for m in reversed(range(n_sub)):
        later[m] = sums[m][0] + run
        run = run + sums[m][1]
    w = jnp.exp(log_beta + jnp.concatenate(later, axis=1))
    if valid is not None:
        w = jnp.where(valid, w, 0.0)
    return w, run


def _pad_rows(a, r0):
    return a if r0 == 0 else jnp.concatenate([jnp.zeros((r0,) + a.shape[1:], a.dtype), a], axis=0)


def _sb_diag_parts(nqh, k_blk, tri_later, n_sub):
    tq = nqh.shape[0]
    parts = []
    for m in range(n_sub):
        r0 = m * SB_SUB
        row = lax.broadcasted_iota(jnp.int32, (tq - r0, SB_SUB), 0)
        col = lax.broadcasted_iota(jnp.int32, (tq - r0, SB_SUB), 1)
        valid = (row >= SB_SUB) | (row > col)
        log_not, log_beta = _sb_logs(nqh[r0:], k_blk[r0:r0 + SB_SUB], valid)
        (inside, row_sum), = _sb_sums(log_not, tri_later, 1)
        parts.append([r0, valid, log_beta, inside, row_sum])
    run = jnp.zeros((tq, LANES), F32)
    for part in reversed(parts):
        r0, row_sum = part[0], part[4]
        part[4] = run[r0:]
        run = run + _pad_rows(row_sum, r0)
    return parts, run


def _sb_diag_fwd(nqh, k_blk, v_blk, tri_later, n_sub):
    parts, run = _sb_diag_parts(nqh, k_blk, tri_later, n_sub)
    acc = jnp.zeros((nqh.shape[0], LANES), F32)
    for r0, valid, log_beta, inside, later_blocks in parts:
        w = jnp.where(valid, jnp.exp(log_beta + inside + later_blocks), 0.0)
        acc = acc + _pad_rows(_dot(w.astype(BF16), v_blk[r0:r0 + SB_SUB]), r0)
    return run, acc


def _sb_valid(i, j, tq, tk):
    qpos = i * tq + lax.broadcasted_iota(jnp.int32, (tq, tk), 0)
    kpos = j * tk + lax.broadcasted_iota(jnp.int32, (tq, tk), 1)
    return qpos > kpos


def sb_fwd(projb, kv, name, gather=()):
    s = projb.shape[0]
    tq, tk = _rows_tile(s, SB_ROWS), _rows_tile(s, SB_KEYS)
    n_sub = tk // SB_SUB
    n_pairs = SB_WIDTH // LANES
    assert s // tk <= LANES and tk % tq == 0

    n_g = len(gather)
    n_i = s // tq

    def body(*refs):
        q_ref, k_ref, v_ref = refs[:3]
        o_ref, c_ref = refs[3 + n_g:5 + n_g]
        i = pl.program_id(1)
        if n_g:
            start, wait = _exchange_ops(refs[3:3 + n_g], refs[5 + n_g:5 + 2 * n_g], refs[5 + 2 * n_g:], 0)
            pl.when((pl.program_id(0) == 0) & (i == 0))(start)
        j_diag = (i * tq) // tk
        masks = _head_masks((tq, LANES))
        nq = q_ref[...] * (-QK_SCALE)
        tri = _sb_tri(later=True)
        lane = lax.broadcasted_iota(jnp.int32, (tq, LANES), 1)
        nqh = [jnp.where(hm, nq, 0.0).astype(BF16) for hm in masks]

        def block(j, carry, valid):
            off = pl.multiple_of(j * tk, tk)
            k_blk, v_blk = k_ref[pl.ds(off, tk), :], v_ref[pl.ds(off, tk), :]
            out = []
            for hh in range(2):
                c, acc, cmat = carry[3 * hh:3 * hh + 3]
                log_not, log_beta = _sb_logs(nqh[hh], k_blk, valid)
                w, c_next = _sb_weights(log_not, log_beta, tri, c, valid, n_sub)
                out += [c_next, acc + _dot(w.astype(BF16), v_blk), jnp.where(lane == j, c, cmat)]
            return tuple(out)

        zero = jnp.zeros((tq, LANES), F32)
        if tq == tk:
            off = pl.multiple_of(i * tk, tk)
            k_blk, v_blk = k_ref[pl.ds(off, tk), :], v_ref[pl.ds(off, tk), :]
            carry = ()
            for hh in range(2):
                carry += _sb_diag_fwd(nqh[hh], k_blk, v_blk, tri, n_sub) + (zero,)
        else:
            carry = block(j_diag, (zero,) * 6, _sb_valid(i, j_diag, tq, tk))
        carry = lax.fori_loop(0, j_diag, lambda jj, cr: block(j_diag - 1 - jj, cr, None), carry)
        c_ref[0] = carry[2]
        c_ref[1] = carry[5]
        o_ref[...] = jnp.where(masks[0], carry[1], carry[4]).astype(BF16)
        if n_g:
            pl.when((pl.program_id(0) == n_pairs - 1) & (i == n_i - 1))(wait)

    outs = pl.pallas_call(
        body, name=name, grid=(n_pairs, n_i),
        in_specs=[pl.BlockSpec((tq, LANES), lambda p, i: (i, p)),
                  pl.BlockSpec((s, LANES), lambda p, i: (0, p)),
                  pl.BlockSpec((s, LANES), lambda p, i: (0, n_pairs + p))] + [_ANY] * n_g,
        out_specs=[pl.BlockSpec((tq, LANES), lambda p, i: (i, p)),
                   pl.BlockSpec((2, tq, LANES), lambda p, i: (p, i, 0))] + [_ANY] * n_g,
        out_shape=[jax.ShapeDtypeStruct((s, SB_WIDTH + MEM_WIDTH), BF16),
                   jax.ShapeDtypeStruct((N_SB_HEADS, s, LANES), F32)] + _exchange_out_shape([], gather),
        scratch_shapes=_exchange_sems(n_g) if n_g else [],
        compiler_params=_params("arbitrary", "arbitrary"),
    )(projb, kv, kv, *gather)
    return outs[0], outs[1], outs[2:]


def sb_bwd(projb, kv, dcat, csave, name, scatter=()):
    s = projb.shape[0]
    tq, tk = _rows_tile(s, SB_ROWS), _rows_tile(s, SB_KEYS)
    n_sub = tk // SB_SUB
    n_pairs = SB_WIDTH // LANES
    n_x = len(scatter)
    n_i = s // tq

    def body(*refs):
        q_ref, k_ref, v_ref, do_ref, c_ref = refs[:5]
        dq_ref, dkb_ref, dvb_ref = refs[5 + n_x:8 + n_x]
        dk_ref, dv_ref = refs[8 + 2 * n_x:10 + 2 * n_x]
        i = pl.program_id(1)
        if n_x:
            start, wait = _exchange_ops(refs[5:5 + n_x], refs[8 + n_x:8 + 2 * n_x], refs[10 + 2 * n_x:], n_x)
            pl.when((pl.program_id(0) == 0) & (i == 0))(start)
        j_diag = (i * tq) // tk

        @pl.when(i == 0)
        def _():
            dk_ref[...] = jnp.zeros_like(dk_ref)
            dv_ref[...] = jnp.zeros_like(dv_ref)

        masks = _head_masks((tq, LANES))
        qs = q_ref[...] * QK_SCALE
        do = do_ref[...]
        tri_later = _sb_tri(later=True)
        tri_before = _sb_tri_pair_before()
        lane = lax.broadcasted_iota(jnp.int32, (tq, LANES), 1)
        qh = [jnp.where(hm, qs, 0.0).astype(BF16) for hm in masks]
        nqh = [jnp.where(hm, -qs, 0.0).astype(BF16) for hm in masks]
        doh = [jnp.where(hm, do, 0.0).astype(BF16) for hm in masks]

        def block(j, carry, valid):
            off = pl.multiple_of(j * tk, tk)
            k_blk, v_blk = k_ref[pl.ds(off, tk), :], v_ref[pl.ds(off, tk), :]
            out, dk, dv = [], None, None
            for hh in range(2):
                e_before, dq = carry[2 * hh:2 * hh + 2]
                log_not, log_beta = _sb_logs(nqh[hh], k_blk, valid)
                c = jnp.sum(jnp.where(lane == j, c_ref[hh], 0.0), axis=1, keepdims=True)
                w, _ = _sb_weights(log_not, log_beta, tri_later, jnp.broadcast_to(c, (tq, LANES)), valid, n_sub)
                dv_h = _dot(w.astype(BF16), doh[hh], _TN)
                e = w * _dot(doh[hh], v_blk, _NT)
                sums = _sb_prefix_pairs(e, tri_before, n_sub)
                run = e_before
                e_pre = []
                for m in range(n_sub):
                    e_pre.append(sums[m][0] + run)
                    run = run + sums[m][1]
                dz = e - jnp.exp(log_beta) * (e + jnp.concatenate(e_pre, axis=1))
                if valid is not None:
                    dz = jnp.where(valid, dz, 0.0)
                dz = dz.astype(BF16)
                dk_h = _dot(dz, qh[hh], _TN)
                dk, dv = (dk_h, dv_h) if hh == 0 else (dk + dk_h, dv + dv_h)
                out += [run, dq + _dot(dz, k_blk)]
            dk_ref[pl.ds(off, tk), :] += dk
            dv_ref[pl.ds(off, tk), :] += dv
            return tuple(out)

        zero = jnp.zeros((tq, LANES), F32)
        carry = lax.fori_loop(0, j_diag, lambda j, cr: block(j, cr, None), (zero,) * 4)
        carry = block(j_diag, carry, _sb_valid(i, j_diag, tq, tk))
        dq_ref[...] = (jnp.where(masks[0], carry[1], carry[3]) * QK_SCALE).astype(BF16)

        @pl.when(i == n_i - 1)
        def _():
            dkb_ref[...] = dk_ref[...].astype(BF16)
            dvb_ref[...] = dv_ref[...].astype(BF16)

        if n_x:
            pl.when((pl.program_id(0) == n_pairs - 1) & (i == n_i - 1))(wait)

    tile = pl.BlockSpec((tq, LANES), lambda p, i: (i, p))
    full = pl.BlockSpec((s, LANES), lambda p, i: (0, p))
    outs = pl.pallas_call(
        body, name=name, grid=(n_pairs, n_i),
        in_specs=[tile, full, pl.BlockSpec((s, LANES), lambda p, i: (0, n_pairs + p)), tile,
                  pl.BlockSpec((2, tq, LANES), lambda p, i: (p, i, 0))] + [_ANY] * n_x,
        out_specs=[tile, full, full] + [_ANY] * n_x,
        out_shape=[jax.ShapeDtypeStruct((s, SB_WIDTH + MEM_WIDTH), BF16), jax.ShapeDtypeStruct((s, SB_WIDTH), BF16),
                   jax.ShapeDtypeStruct((s, SB_WIDTH), BF16)] + _exchange_out_shape(scatter, []),
        scratch_shapes=[pltpu.VMEM((s, LANES), F32), pltpu.VMEM((s, LANES), F32)]
        + (_exchange_sems(n_x) if n_x else []),
        compiler_params=_params("arbitrary", "arbitrary"),
    )(projb, kv, kv, dcat, csave, *scatter)
    return outs[0], outs[1], outs[2], outs[3:]


def reduce_adamw(parts, w, m, v, name, after=None):
    n, r, c = parts.shape
    tr = next((t for t in (128, 176) if r % t == 0), r)
    bias1 = 1.0 - ADAM_B1 ** ADAM_STEP
    bias2 = 1.0 - ADAM_B2 ** ADAM_STEP

    def body(p_ref, w_ref, m_ref, v_ref, *rest):
        g_ref, d_ref, nm_ref, nv_ref = rest[-4:]
        g = p_ref[0].astype(F32)
        for k in range(1, n):
            g = g + p_ref[k].astype(F32)
        new_m = ADAM_B1 * m_ref[...] + (1.0 - ADAM_B1) * g
        new_v = ADAM_B2 * v_ref[...] + (1.0 - ADAM_B2) * (g * g)
        m_hat = new_m / bias1
        v_hat = new_v / bias2
        g_ref[...] = g
        d_ref[...] = -ADAM_LR * (m_hat / (jnp.sqrt(v_hat) + ADAM_EPS) + ADAM_WD * w_ref[...])
        nm_ref[...] = new_m
        nv_ref[...] = new_v

    row = pl.BlockSpec((tr, c), lambda i: (i, 0))
    return pl.pallas_call(
        body, name=name, grid=(r // tr,),
        in_specs=[pl.BlockSpec((n, tr, c), lambda i: (0, i, 0)), row, row, row] + ([] if after is None else [_ANY]),
        out_specs=[row, row, row, row],
        out_shape=[jax.ShapeDtypeStruct((r, c), F32)] * 4,
        compiler_params=_params("parallel"),
    )(parts, w, m, v, *(() if after is None else (after,)))


EARLY_0 = ("a_w_in", "a_w_mem_kv")
EARLY_1 = ("a_w_out",)
EARLY_2 = ("a_w_gu", "a_w_down")
EARLY_3 = ("w_kv", "b_w_q", "b_w_mem_kv")
EARLY = EARLY_0 + EARLY_1 + EARLY_2 + EARLY_3
LATE = ("b_w_out", "b_w_gu", "b_w_down")
BIG = EARLY + LATE
GATHER_3 = ("w_kv", "b_w_q")
GATHER_LATE = LATE + ("b_w_mem_kv",)
COL_SHARDED = ("a_w_gu", "w_kv", "b_w_gu")
SMALL_SHARDED = ("a_norm_mix", "a_scale", "a_norm_ffn")
SMALL_REPL = ("mem_norm", "kv_norm", "b_norm_mix", "b_norm_ffn", "final_norm")
WEIGHTS = ("mem_norm", "a_norm_mix", "a_w_in", "a_w_group", "a_scale", "a_w_mem_kv", "a_w_out", "a_norm_ffn", "a_w_gu",
           "a_w_down", "kv_norm", "w_kv", "b_norm_mix", "b_w_q", "b_w_mem_kv", "b_w_out", "b_norm_ffn", "b_w_gu",
           "b_w_down", "final_norm")


def _pad_row(v):
    v = v.reshape(1, -1)
    return jnp.pad(v, ((0, 0), (0, PACK_COLS - v.shape[1])))


def kernel(x, mem, mem_norm, a_norm_mix, a_w_in, a_w_group, a_scale, a_w_mem_kv, a_w_out, a_norm_ffn, a_w_gu, a_w_down, kv_norm, w_kv, b_norm_mix, b_w_q, b_w_mem_kv, b_w_out, b_norm_ffn, b_w_gu, b_w_down, final_norm, loss_target, m_mem_norm, m_a_norm_mix, m_a_w_in, m_a_w_group, m_a_scale, m_a_w_mem_kv, m_a_w_out, m_a_norm_ffn, m_a_w_gu, m_a_w_down, m_kv_norm, m_w_kv, m_b_norm_mix, m_b_w_q, m_b_w_mem_kv, m_b_w_out, m_b_norm_ffn, m_b_w_gu, m_b_w_down, m_final_norm, v_mem_norm, v_a_norm_mix, v_a_w_in, v_a_w_group, v_a_scale, v_a_w_mem_kv, v_a_w_out, v_a_norm_ffn, v_a_w_gu, v_a_w_down, v_kv_norm, v_w_kv, v_b_norm_mix, v_b_w_q, v_b_w_mem_kv, v_b_w_out, v_b_norm_ffn, v_b_w_gu, v_b_w_down, v_final_norm):
    w = dict(mem_norm=mem_norm, a_norm_mix=a_norm_mix, a_w_in=a_w_in, a_w_group=a_w_group, a_scale=a_scale,
             a_w_mem_kv=a_w_mem_kv, a_w_out=a_w_out, a_norm_ffn=a_norm_ffn, a_w_gu=a_w_gu, a_w_down=a_w_down,
             kv_norm=kv_norm, w_kv=w_kv, b_norm_mix=b_norm_mix, b_w_q=b_w_q, b_w_mem_kv=b_w_mem_kv, b_w_out=b_w_out,
             b_norm_ffn=b_norm_ffn, b_w_gu=b_w_gu, b_w_down=b_w_down, final_norm=final_norm)
    mom = dict(mem_norm=m_mem_norm, a_norm_mix=m_a_norm_mix, a_w_in=m_a_w_in, a_w_group=m_a_w_group, a_scale=m_a_scale,
               a_w_mem_kv=m_a_w_mem_kv, a_w_out=m_a_w_out, a_norm_ffn=m_a_norm_ffn, a_w_gu=m_a_w_gu,
               a_w_down=m_a_w_down, kv_norm=m_kv_norm, w_kv=m_w_kv, b_norm_mix=m_b_norm_mix, b_w_q=m_b_w_q,
               b_w_mem_kv=m_b_w_mem_kv, b_w_out=m_b_w_out, b_norm_ffn=m_b_norm_ffn, b_w_gu=m_b_w_gu,
               b_w_down=m_b_w_down, final_norm=m_final_norm)
    var = dict(mem_norm=v_mem_norm, a_norm_mix=v_a_norm_mix, a_w_in=v_a_w_in, a_w_group=v_a_w_group, a_scale=v_a_scale,
               a_w_mem_kv=v_a_w_mem_kv, a_w_out=v_a_w_out, a_norm_ffn=v_a_norm_ffn, a_w_gu=v_a_w_gu,
               a_w_down=v_a_w_down, kv_norm=v_kv_norm, w_kv=v_w_kv, b_norm_mix=v_b_norm_mix, b_w_q=v_b_w_q,
               b_w_mem_kv=v_b_w_mem_kv, b_w_out=v_b_w_out, b_norm_ffn=v_b_norm_ffn, b_w_gu=v_b_w_gu,
               b_w_down=v_b_w_down, final_norm=v_final_norm)

    me = 4 * lax.axis_index("x") + 2 * lax.axis_index("y") + lax.axis_index("c")
    shard2d = {n: w[n].shape[-2:] for n in BIG}
    shard = {n: w[n].reshape(shard2d[n]).astype(BF16) for n in BIG}
    n_sh = N_DEV // 2
    s_len = x.shape[1]

    small_send = jnp.concatenate([_pad_row(w[n]) for n in SMALL_SHARDED]
                                 + [jnp.zeros((8 - len(SMALL_SHARDED), PACK_COLS), F32)], axis=0)
    def own_slot(land, own):
        return lax.dynamic_update_slice(land, own[None], (me, 0, 0))

    def gather_done(handle, names, after, name):
        got = exchange_wait(handle, after, name)
        gathered.update({n: own_slot(g, shard[n]) for n, g in zip(names, got)})
        return got

    gathered = {}
    gather_0 = exchange_start([], [shard[n] for n in EARLY_0] + [small_send], x, "gather_0_start")
    gather_1 = exchange_start([], [shard[n] for n in EARLY_1], gather_0["token"], "gather_1_start")
    gather_2 = exchange_start([], [shard["a_w_gu"]], gather_1["token"], "gather_2_start")
    gather_2b = exchange_start([], [shard["a_w_down"]], gather_2["token"], "gather_2b_start")
    gather_3 = exchange_start([], [shard[n] for n in GATHER_3], gather_2b["token"], "gather_3_start")
    small_all = own_slot(gather_done(gather_0, EARLY_0, gather_3["token"], "gather_0_wait")[-1], small_send)
    a_norm_mix_f = small_all[:, 0, :a_norm_mix.shape[1]].reshape(-1)
    a_scale_f = small_all[:, 1, :a_scale.shape[1]].reshape(-1)
    a_norm_ffn_f = small_all[:, 2, :a_norm_ffn.shape[1]].reshape(-1)

    def rows_full(n):
        return gathered[n].reshape(-1, shard2d[n][1])

    def gu_views(tag):
        gu8 = gathered[tag + "_w_gu"]
        return gu8.reshape((2, n_sh) + gu8.shape[1:]), gu8

    def down_view(tag):
        down = gathered[tag + "_w_down"]
        return down.reshape(n_sh, -1, down.shape[2])

    wbd = jnp.zeros((SB_WIDTH, SB_WIDTH), BF16)
    for g in range(4):
        sl = slice(g * POOL_GROUP, (g + 1) * POOL_GROUP)
        wbd = wbd.at[sl, sl].set(a_w_group[0, g].astype(BF16))

    xs, mems, tgt = x[0], mem[0], loss_target[0]

    memn = rmsnorm_fwd(mems, mem_norm, "memn")
    (h0, proj), = norm_matmul(xs, [(a_norm_mix_f, rows_full("a_w_in"), F32)], "proj_a")
    mkv_a = matmul([(memn, rows_full("a_w_mem_kv"))], "nn", BF16, "mkv_a")
    pooled, cat_a = pool_fwd(proj, wbd, a_scale_f, "pool_fwd")
    cat_a = mem_attn_fwd(proj, mkv_a, cat_a, "mem_fwd_a")
    gather_done(gather_1, EARLY_1, cat_a, "gather_1_wait")
    x1 = matmul([(cat_a, rows_full("a_w_out"))], "nn", F32, "out_a", res=xs)
    h1 = rmsnorm_fwd(x1, a_norm_ffn_f, "h1")
    gather_done(gather_2, ("a_w_gu",), h1, "gather_2_wait")
    wgu_a, wgu8_a = gu_views("a")
    gu_a, act_a = ffn_up(h1, wgu_a, "ffn_up_a")
    gather_done(gather_2b, ("a_w_down",), act_a, "gather_2b_wait")
    wdown_a = down_view("a")
    x2 = matmul([(act_a, wdown_a)], "nn", F32, "down_a", res=x1, roles=("reduce", "reduce"), count=n_sh)

    gather_done(gather_3, GATHER_3, x2, "gather_3_wait")
    wkv = jnp.transpose(gathered["w_kv"], (1, 0, 2)).reshape(shard2d["w_kv"][0], -1)
    (hk, kv), (h3, projb) = norm_matmul(
        x2, [(kv_norm, wkv, BF16), (b_norm_mix[0], rows_full("b_w_q"), F32)], "kv_proj_b")
    cat_b, csave, got = sb_fwd(projb, kv, "sb_fwd", gather=[shard[n] for n in GATHER_LATE])
    gathered.update(zip(GATHER_LATE, got))
    mkv_b = matmul([(memn, rows_full("b_w_mem_kv"))], "nn", BF16, "mkv_b")
    cat_b = mem_attn_fwd(projb, mkv_b, cat_b, "mem_fwd_b")
    x3 = matmul([(cat_b, rows_full("b_w_out"))], "nn", F32, "out_b", res=x2)
    h4 = rmsnorm_fwd(x3, b_norm_ffn[0], "h4")
    (wgu_b, wgu8_b), wdown_b = gu_views("b"), down_view("b")
    gu_b, act_b = ffn_up(h4, wgu_b, "ffn_up_b")
    x4 = matmul([(act_b, wdown_b)], "nn", F32, "down_b", res=x3, roles=("reduce", "reduce"), count=n_sh)

    grads = {}
    dx4, dx4_bf, grads["final_norm"], loss_part = loss_head(x4, tgt, final_norm, "loss_head")

    def blocks(n, g):
        return g.reshape((N_DEV,) + tuple(shard2d[n]))

    def ffn_weight_grads(tag, dx_out_bf, h, gu, act, wgu8, wdown):
        fs = wgu8.shape[2]
        dgu8 = ffn_bwd_act(dx_out_bf, wdown, gu, "ffn_bwd_act_" + tag).reshape(N_DEV, s_len, fs)
        d_wdown = matmul([(act, dx_out_bf)], "tn", BF16, "dw_down_" + tag, tm=fs, roles=("batch", None), count=n_sh)
        grads[tag + "_w_gu"] = matmul([(dgu8, h)], "tn", BF16, "dw_gu_" + tag, tm=fs, roles=("batch", None), count=N_DEV)
        grads[tag + "_w_down"] = blocks(tag + "_w_down", d_wdown)
        return dgu8

    def ffn_input_grads(tag, x_in, dx_out, dgu8, wgu8, g_ffn):
        dx_in, dx_in_bf, dgs = matmul_norm_bwd(x_in, dx_out, [(g_ffn, [(dgu8, wgu8)])], "norm_bwd_ffn_" + tag)
        return dx_in, dx_in_bf, dgs[0]

    def grads_start(names, small, after, name):
        return exchange_start([grads[n] for n in names], small, after, name)

    def grads_done(handle, names, after, name):
        got = exchange_wait(handle, after, name)
        for n, g in zip(names, got):
            recv[n] = lax.dynamic_update_slice(g, lax.dynamic_slice(grads[n], (me, 0, 0), (1,) + g.shape[1:]), (me, 0, 0))
        return got

    recv = {}
    dgu8_b = ffn_weight_grads("b", dx4_bf, h4, gu_b, act_b, wgu8_b, wdown_b)
    dx3, dx3_bf, grads["b_norm_ffn"] = ffn_input_grads("b", x3, dx4, dgu8_b, wgu8_b, b_norm_ffn[0])

    dcat_b = matmul([(dx3_bf, rows_full("b_w_out"))], "nt", F32, "dcat_b")
    grads["b_w_out"] = blocks("b_w_out", matmul([(cat_b, dx3_bf)], "tn", BF16, "dw_out_b"))
    dprojb, dk, dv, recv_late = sb_bwd(projb, kv, dcat_b, csave, "sb_bwd", scatter=[grads[n] for n in LATE])
    dprojb, dmkv_b = mem_attn_bwd(projb, mkv_b, dcat_b, dprojb, "mem_bwd_b")
    grads["b_w_q"] = blocks("b_w_q", matmul([(h3, dprojb)], "tn", BF16, "dw_q_b"))
    d_wkv_t = jnp.concatenate([matmul([(dk, hk)], "tn", BF16, "dw_k", tm=SB_WIDTH),
                               matmul([(dv, hk)], "tn", BF16, "dw_v", tm=SB_WIDTH)], axis=0)
    grads["w_kv"] = d_wkv_t.reshape(N_DEV, -1, d_wkv_t.shape[1])
    dx2, dx2_bf, dgs = matmul_norm_bwd(
        x2, dx3, [(b_norm_mix[0], [(dprojb, rows_full("b_w_q"))]),
                  (kv_norm, [(dk, wkv[:, :SB_WIDTH]), (dv, wkv[:, SB_WIDTH:])])], "norm_bwd_x2")
    grads["b_norm_mix"], grads["kv_norm"] = dgs
    dmkv_b_bf = dmkv_b.astype(BF16)
    grads["b_w_mem_kv"] = blocks("b_w_mem_kv", matmul([(memn, dmkv_b_bf)], "tn", BF16, "dw_mkv_b"))

    grads_3 = grads_start(EARLY_3, [], grads["b_w_mem_kv"], "grads_3_start")
    dgu8_a = ffn_weight_grads("a", dx2_bf, h1, gu_a, act_a, wgu8_a, wdown_a)
    grads_2 = grads_start(EARLY_2, [], grads_3["token"], "grads_2_start")
    dx1, dx1_bf, grads["a_norm_ffn"] = ffn_input_grads(
        "a", x1, dx2, dgu8_a, wgu8_a, a_norm_ffn_f + grads_2["token"][0, 0])

    dcat_a = matmul([(dx1_bf, rows_full("a_w_out"))], "nt", F32, "dcat_a")
    grads["a_w_out"] = blocks("a_w_out", matmul([(cat_a, dx1_bf)], "tn", BF16, "dw_out_a"))
    dproj, d_wbd, grads["a_scale"] = pool_bwd(dcat_a, pooled, wbd, a_scale_f, "pool_bwd")
    grads["a_w_group"] = jnp.stack(
        [d_wbd[g * POOL_GROUP:(g + 1) * POOL_GROUP, g * POOL_GROUP:(g + 1) * POOL_GROUP] for g in range(4)])
    group_grads = grads["a_w_group"].reshape(-1, PACK_COLS).astype(BF16)
    grads_1 = exchange_start([grads[n] for n in EARLY_1], [group_grads], grads_2["token"], "grads_1_start")
    dproj, dmkv_a = mem_attn_bwd(proj, mkv_a, dcat_a, dproj, "mem_bwd_a")
    grads["a_w_in"] = blocks("a_w_in", matmul([(h0, dproj)], "tn", BF16, "dw_in_a"))
    dmkv_a_bf = dmkv_a.astype(BF16)
    grads["a_w_mem_kv"] = blocks("a_w_mem_kv", matmul([(memn, dmkv_a_bf)], "tn", BF16, "dw_mkv_a"))
    grads_0 = grads_start(EARLY_0, [], grads_1["token"], "grads_0_start")
    dx, _, dgs = matmul_norm_bwd(
        xs, dx1, [(a_norm_mix_f + grads_0["token"][0, 0], [(dproj, rows_full("a_w_in"))])], "norm_bwd_x")
    grads["a_norm_mix"] = dgs[0]
    dmemn = matmul([(dmkv_a_bf, rows_full("a_w_mem_kv")), (dmkv_b_bf, rows_full("b_w_mem_kv"))], "nt", F32, "dmemn")
    _, _, dgs = rmsnorm_bwd(mems, jnp.zeros_like(mems), [(mem_norm, dmemn)], "norm_bwd_mem")
    grads["mem_norm"] = dgs[0]

    small_names = SMALL_REPL + SMALL_SHARDED
    head_rows = 16

    def small_pack(rows_of):
        head = jnp.zeros((head_rows, PACK_COLS), F32)
        for r, (v, col) in enumerate(rows_of):
            v = v.reshape(1, -1)
            if col is None:
                head = head + jnp.pad(v, ((r, head_rows - 1 - r), (0, PACK_COLS - v.shape[1])))
            else:
                head = lax.dynamic_update_slice(head, v, (r, col))
        return head

    small_grads = small_pack([(grads[n], None) for n in small_names] + [(loss_part[:, :1], None)])
    grads_s = exchange_start([], [small_grads], dx, "grads_s_start")

    def small_state(src):
        return small_pack([(src[n], None) for n in SMALL_REPL]
                          + [(src[n], me * src[n].shape[-1]) for n in SMALL_SHARDED])

    def state2d(src, n):
        a = src[n].reshape(shard2d[n])
        return a.T if n in COL_SHARDED else a

    def adamw(names, after):
        for n in names:
            big_out[n] = reduce_adamw(recv[n], state2d(w, n), state2d(mom, n), state2d(var, n), "adamw_" + n, after=after)
        return big_out[names[-1]][0]

    big_out = {}
    recv.update(zip(LATE, recv_late))
    done = adamw(LATE, grads_s["token"])
    grads_done(grads_3, EARLY_3, done, "grads_3_wait")
    done = adamw(EARLY_3, None)
    grads_done(grads_2, EARLY_2, done, "grads_2_wait")
    done = adamw(EARLY_2, None)
    group_recv = own_slot(grads_done(grads_1, EARLY_1, done, "grads_1_wait")[-1], group_grads)
    done = adamw(EARLY_1, None)
    group_out = reduce_adamw(group_recv, *[src["a_w_group"].reshape(-1, PACK_COLS) for src in (w, mom, var)], "adamw_group")
    grads_done(grads_0, EARLY_0, group_out[0], "grads_0_wait")
    done = adamw(EARLY_0, None)
    small_recv = own_slot(exchange_wait(grads_s, done, "grads_s_wait")[0], small_grads)
    small_out = reduce_adamw(small_recv, small_state(w), small_state(mom), small_state(var), "adamw_small")

    def unpack(kind):
        out = {n: (big_out[n][kind].T if n in COL_SHARDED else big_out[n][kind]).reshape(w[n].shape) for n in BIG}
        so = small_out[kind]
        for r, n in enumerate(SMALL_REPL):
            out[n] = so[r, :w[n].shape[-1]].reshape(w[n].shape)
        for r, n in enumerate(SMALL_SHARDED):
            width = w[n].shape[-1]
            out[n] = lax.dynamic_slice(so, (len(SMALL_REPL) + r, me * width), (1, width)).reshape(w[n].shape)
        out["a_w_group"] = group_out[kind].reshape(a_w_group.shape)
        return out

    loss = small_out[0][len(small_names), 0]
    results = [loss, dx.reshape(x.shape)]
    for kind in range(4):
        out = unpack(kind)
        results += [out[n] for n in WEIGHTS]
    return tuple(results)
```

```python
import functools

import jax
import jax.numpy as jnp
from jax import lax
from jax.experimental import pallas as pl
from jax.experimental.pallas import tpu as pltpu

F32 = jnp.float32
BF16 = jnp.bfloat16

N_DEV = 8
HEAD_DIM = 64
N_SB_HEADS = 12
SB_WIDTH = N_SB_HEADS * HEAD_DIM
MEM_WIDTH = 4 * HEAD_DIM
POOL_WINDOWS = (2, 4, 8, 16)
POOL_GROUP = SB_WIDTH // 4
POOL_HALO = 16
EPS = 1e-6
QK_SCALE = HEAD_DIM ** -0.5
LOG2_E = 1.4426950408889634
LANES = 128
PACK_COLS = 1024

ADAM_LR = 0.001
ADAM_B1 = 0.9
ADAM_B2 = 0.999
ADAM_EPS = 1e-08
ADAM_WD = 0.01
ADAM_STEP = 10

VMEM_LIMIT = 48 * 1024 * 1024

_NT = (((1,), (1,)), ((), ()))
_TN = (((0,), (0,)), ((), ()))
_NN = (((1,), (0,)), ((), ()))


def _params(*sem):
    return pltpu.CompilerParams(dimension_semantics=sem, vmem_limit_bytes=VMEM_LIMIT)


def _dot(a, b, dims=_NN):
    return lax.dot_general(a, b, dims, preferred_element_type=F32)


def exchange(a2a, ag, name):
    n_a2a, n_arr = len(a2a), len(a2a) + len(ag)

    def body(*refs):
        start, wait = _exchange_ops(refs[:n_arr], refs[n_arr:2 * n_arr], refs[2 * n_arr:], n_a2a)
        start()
        wait()

    return pl.pallas_call(
        body, name=name, out_shape=_exchange_out_shape(a2a, ag),
        in_specs=[_ANY] * n_arr, out_specs=[_ANY] * n_arr,
        scratch_shapes=_exchange_sems(n_arr),
    )(*a2a, *ag)


_ANY = pl.BlockSpec(memory_space=pl.ANY)
_HBM = pl.BlockSpec(memory_space=pltpu.HBM)
_SEM = pl.BlockSpec(memory_space=pltpu.SEMAPHORE)
_DATAFLOW = pltpu.SideEffectType.DATAFLOW_SIDE_EFFECTING


def exchange_start(a2a, ag, after, name):
    n_a2a, n_arr = len(a2a), len(a2a) + len(ag)
    arrays = [pltpu.with_memory_space_constraint(v, pltpu.HBM) for v in list(a2a) + list(ag)]
    shapes = _exchange_out_shape(a2a, ag)
    lands = [pltpu.with_memory_space_constraint(lax.empty(sh.shape, sh.dtype), pltpu.HBM) for sh in shapes]

    def body(*refs):
        ins, outs = refs[:n_arr], refs[n_arr:2 * n_arr]
        send_sems, recv_sems, token = refs[4 * n_arr + 1:4 * n_arr + 4]
        start, _ = _exchange_ops(ins, outs, (send_sems, recv_sems, None), n_a2a, local_copies=False)
        start()
        token[...] = jnp.zeros_like(token)

    outs = pl.pallas_call(
        body, name=name,
        out_shape=[pltpu.HBM(v.shape, v.dtype) for v in arrays] + [pltpu.HBM(sh.shape, sh.dtype) for sh in shapes]
        + [pltpu.SemaphoreType.DMA((n_arr * (N_DEV - 1),)), pltpu.SemaphoreType.DMA((n_arr * (N_DEV - 1),)),
           jax.ShapeDtypeStruct((8, LANES), F32)],
        in_specs=[_HBM] * (2 * n_arr) + [_ANY],
        out_specs=[_HBM] * (2 * n_arr) + [_SEM, _SEM, pl.BlockSpec(memory_space=pltpu.VMEM)],
        input_output_aliases={i: i for i in range(2 * n_arr)},
        compiler_params=pltpu.CompilerParams(has_side_effects=_DATAFLOW),
    )(*arrays, *lands, after)
    return dict(ins=outs[:n_arr], lands=outs[n_arr:2 * n_arr], sems=outs[2 * n_arr:2 * n_arr + 2], n_a2a=n_a2a,
                token=outs[2 * n_arr + 2])


def exchange_wait(handle, after, name):
    ins, lands, sems, n_a2a = handle["ins"], handle["lands"], handle["sems"], handle["n_a2a"]
    n_arr = len(ins)

    def body(*refs):
        in_refs, land_refs = refs[:n_arr], refs[n_arr:2 * n_arr]
        send_sems, recv_sems = refs[2 * n_arr:2 * n_arr + 2]
        _, wait = _exchange_ops(in_refs, land_refs, (send_sems, recv_sems, None), n_a2a, local_copies=False)
        wait()

    outs = pl.pallas_call(
        body, name=name,
        out_shape=[pltpu.HBM(v.shape, v.dtype) for v in list(ins) + list(lands)],
        in_specs=[_HBM] * (2 * n_arr) + [_SEM, _SEM, _ANY], out_specs=[_HBM] * (2 * n_arr),
        input_output_aliases={i: i for i in range(2 * n_arr)},
        compiler_params=pltpu.CompilerParams(has_side_effects=_DATAFLOW),
    )(*ins, *lands, *sems, after)
    return outs[n_arr:]


def _exchange_out_shape(a2a, ag):
    out_shape = [jax.ShapeDtypeStruct(v.shape, v.dtype) for v in a2a]
    return out_shape + [jax.ShapeDtypeStruct((N_DEV,) + v.shape, v.dtype) for v in ag]


def _exchange_sems(n_arr):
    return [pltpu.SemaphoreType.DMA((n_arr * (N_DEV - 1),)), pltpu.SemaphoreType.DMA((n_arr * (N_DEV - 1),)),
            pltpu.SemaphoreType.DMA((n_arr,))]


def _exchange_ops(ins, outs, sems, n_a2a, local_copies=True):
    send_sems, recv_sems, local_sems = sems
    n_arr = len(ins)
    x, y, c = lax.axis_index("x"), lax.axis_index("y"), lax.axis_index("c")
    me = 4 * x + 2 * y + c

    def peer_of(k):
        px = 1 - x if k & 4 else x
        py = 1 - y if k & 2 else y
        pc = 1 - c if k & 1 else c
        return (px, py, pc), 4 * px + 2 * py + pc

    def copy(a, k, landing):
        peer, pid = peer_of(k)
        src = ins[a].at[pid] if a < n_a2a else ins[a]
        return pltpu.make_async_remote_copy(
            src_ref=src, dst_ref=outs[a].at[pid if landing else me], send_sem=send_sems.at[a * (N_DEV - 1) + k - 1],
            recv_sem=recv_sems.at[a * (N_DEV - 1) + k - 1], device_id=peer, device_id_type=pl.DeviceIdType.MESH)

    def local(a):
        src = ins[a].at[me] if a < n_a2a else ins[a]
        return pltpu.make_async_copy(src, outs[a].at[me], local_sems.at[a])

    order = (6, 7, 4, 5, 2, 3, 1)

    def start():
        if local_copies:
            for a in range(n_arr):
                local(a).start()
        for a in range(n_arr):
            for k in order:
                copy(a, k, landing=False).start()

    def wait():
        for a in range(n_arr):
            for k in order:
                copy(a, k, landing=True).wait_recv()
        for a in range(n_arr):
            for k in order:
                copy(a, k, landing=False).wait_send()
        if local_copies:
            for a in range(n_arr):
                local(a).wait()

    return start, wait


def _rows_tile(n, want):
    t = min(want, n)
    assert n % t == 0, (n, t)
    return t


def rmsnorm_fwd(x, g, name):
    s, d = x.shape
    tm = _rows_tile(s, 512)

    def body(x_ref, g_ref, h_ref):
        xv = x_ref[...]
        r = lax.rsqrt(jnp.mean(xv * xv, axis=-1, keepdims=True) + EPS)
        h_ref[...] = ((xv * r) * g_ref[...]).astype(h_ref.dtype)

    return pl.pallas_call(
        body, name=name, grid=(s // tm,),
        in_specs=[pl.BlockSpec((tm, d), lambda i: (i, 0)), pl.BlockSpec((1, d), lambda i: (0, 0))],
        out_specs=pl.BlockSpec((tm, d), lambda i: (i, 0)),
        out_shape=jax.ShapeDtypeStruct((s, d), BF16),
        compiler_params=_params("parallel"),
    )(x, g.reshape(1, d))


def norm_matmul(x, heads, name):
    s, d = x.shape
    tm = _rows_tile(s, 512)
    n = len(heads)

    def body(*refs):
        x_ref = refs[0]
        xv = x_ref[...]
        xhat = xv * lax.rsqrt(jnp.mean(xv * xv, axis=-1, keepdims=True) + EPS)
        for k in range(n):
            g_ref, w_ref = refs[1 + 2 * k], refs[2 + 2 * k]
            h_ref, y_ref = refs[1 + 2 * n + 2 * k], refs[2 + 2 * n + 2 * k]
            h = (xhat * g_ref[...]).astype(BF16)
            h_ref[...] = h
            y_ref[...] = _dot(h, w_ref[...]).astype(y_ref.dtype)

    row = pl.BlockSpec((tm, d), lambda i: (i, 0))
    in_specs, out_specs, out_shape, args = [row], [], [], [x]
    for g, w_k, dtype in heads:
        in_specs += [pl.BlockSpec((1, d), lambda i: (0, 0)), pl.BlockSpec(w_k.shape, lambda i: (0, 0))]
        args += [g.reshape(1, d), w_k]
        out_specs += [row, pl.BlockSpec((tm, w_k.shape[1]), lambda i: (i, 0))]
        out_shape += [jax.ShapeDtypeStruct((s, d), BF16), jax.ShapeDtypeStruct((s, w_k.shape[1]), dtype)]
    outs = pl.pallas_call(
        body, name=name, grid=(s // tm,), in_specs=in_specs, out_specs=out_specs, out_shape=out_shape,
        compiler_params=_params("parallel"),
    )(*args)
    return [(outs[2 * k], outs[2 * k + 1]) for k in range(n)]


def rmsnorm_bwd(x, dres, pairs, name):
    s, d = x.shape
    tm = _rows_tile(s, 256)
    n = len(pairs)

    def body(*refs):
        x_ref, dres_ref = refs[0], refs[1]
        g_refs, dh_refs = refs[2:2 + n], refs[2 + n:2 + 2 * n]
        dx_ref, dxb_ref = refs[2 + 2 * n], refs[3 + 2 * n]
        dg_refs = refs[4 + 2 * n:]
        i = pl.program_id(0)
        xv = x_ref[...]
        r = lax.rsqrt(jnp.mean(xv * xv, axis=-1, keepdims=True) + EPS)
        xhat = xv * r
        acc = dres_ref[...]
        for k in range(n):
            dh = dh_refs[k][...].astype(F32)
            part = jnp.sum(dh * xhat, axis=0, keepdims=True)

            @pl.when(i == 0)
            def _(k=k, part=part):
                dg_refs[k][...] = part

            @pl.when(i > 0)
            def _(k=k, part=part):
                dg_refs[k][...] += part

            dxh = dh * g_refs[k][...]
            acc = acc + r * (dxh - xhat * jnp.mean(dxh * xhat, axis=-1, keepdims=True))
        dx_ref[...] = acc
        dxb_ref[...] = acc.astype(BF16)

    row = pl.BlockSpec((tm, d), lambda i: (i, 0))
    vec = pl.BlockSpec((1, d), lambda i: (0, 0))
    outs = pl.pallas_call(
        body, name=name, grid=(s // tm,),
        in_specs=[row, row] + [vec] * n + [row] * n,
        out_specs=[row, row] + [vec] * n,
        out_shape=[jax.ShapeDtypeStruct((s, d), F32), jax.ShapeDtypeStruct((s, d), BF16)]
        + [jax.ShapeDtypeStruct((1, d), F32)] * n,
        compiler_params=_params("arbitrary"),
    )(x, dres, *[g.reshape(1, d) for g, _ in pairs], *[dh for _, dh in pairs])
    return outs[0], outs[1], outs[2:]


def matmul_norm_bwd(x, dres, heads, name):
    s, d = x.shape
    tm = _rows_tile(s, 512)
    n = len(heads)
    n_ops = [len(pairs) for _, pairs in heads]

    def body(*refs):
        x_ref, dres_ref = refs[0], refs[1]
        n_in = 2 + n + 2 * sum(n_ops)
        dx_ref, dxb_ref = refs[n_in], refs[n_in + 1]
        dg_refs = refs[n_in + 2:]
        i = pl.program_id(0)
        xv = x_ref[...]
        r = lax.rsqrt(jnp.mean(xv * xv, axis=-1, keepdims=True) + EPS)
        xhat = xv * r
        acc = dres_ref[...]
        pos = 2 + n
        for k in range(n):
            dh = None
            for a, _ in heads[k][1]:
                a_ref, w_ref = refs[pos], refs[pos + 1]
                for kk in range(a.shape[0] if a.ndim == 3 else 1):
                    prod = _dot(a_ref[kk], w_ref[kk], _NT) if a.ndim == 3 else _dot(a_ref[...], w_ref[...], _NT)
                    dh = prod if dh is None else dh + prod
                pos += 2
            part = jnp.sum(dh * xhat, axis=0, keepdims=True)

            @pl.when(i == 0)
            def _(k=k, part=part):
                dg_refs[k][...] = part

            @pl.when(i > 0)
            def _(k=k, part=part):
                dg_refs[k][...] += part

            dxh = dh * refs[2 + k][...]
            acc = acc + r * (dxh - xhat * jnp.mean(dxh * xhat, axis=-1, keepdims=True))
        dx_ref[...] = acc
        dxb_ref[...] = acc.astype(BF16)

    row = pl.BlockSpec((tm, d), lambda i: (i, 0))
    vec = pl.BlockSpec((1, d), lambda i: (0, 0))
    in_specs, args = [row, row] + [vec] * n, [x, dres] + [g.reshape(1, d) for g, _ in heads]
    for _, pairs in heads:
        for a, w_k in pairs:
            if a.ndim == 3:
                in_specs += [pl.BlockSpec((a.shape[0], tm, a.shape[2]), lambda i: (0, i, 0)),
                             pl.BlockSpec(w_k.shape, lambda i: (0, 0, 0), pipeline_mode=pl.Buffered(1))]
            else:
                in_specs += [pl.BlockSpec((tm, a.shape[1]), lambda i: (i, 0)),
                             pl.BlockSpec(w_k.shape, lambda i: (0, 0), pipeline_mode=pl.Buffered(1))]
            args += [a, w_k]
    outs = pl.pallas_call(
        body, name=name, grid=(s // tm,), in_specs=in_specs, out_specs=[row, row] + [vec] * n,
        out_shape=[jax.ShapeDtypeStruct((s, d), F32), jax.ShapeDtypeStruct((s, d), BF16)]
        + [jax.ShapeDtypeStruct((1, d), F32)] * n,
        compiler_params=_params("arbitrary"),
    )(*args)
    return outs[0], outs[1], outs[2:]


def loss_head(x, tgt, g, name):
    s, d = x.shape
    tm = _rows_tile(s, 512)

    def body(x_ref, t_ref, g_ref, dx_ref, dxb_ref, dg_ref, loss_ref):
        i = pl.program_id(0)
        xv = x_ref[...]
        gv = g_ref[...]
        r = lax.rsqrt(jnp.mean(xv * xv, axis=-1, keepdims=True) + EPS)
        xhat = xv * r
        diff = xhat * gv - t_ref[...]
        part_loss = 0.5 * jnp.sum(jnp.mean(diff * diff, axis=-1, keepdims=True), axis=0, keepdims=True)
        dy = diff * (1.0 / d)
        part_dg = jnp.sum(dy * xhat, axis=0, keepdims=True)

        @pl.when(i == 0)
        def _():
            dg_ref[...] = part_dg
            loss_ref[...] = jnp.broadcast_to(part_loss, loss_ref.shape)

        @pl.when(i > 0)
        def _():
            dg_ref[...] += part_dg
            loss_ref[...] += jnp.broadcast_to(part_loss, loss_ref.shape)

        dxh = dy * gv
        dx = r * (dxh - xhat * jnp.mean(dxh * xhat, axis=-1, keepdims=True))
        dx_ref[...] = dx
        dxb_ref[...] = dx.astype(BF16)

    row = pl.BlockSpec((tm, d), lambda i: (i, 0))
    vec = pl.BlockSpec((1, d), lambda i: (0, 0))
    return pl.pallas_call(
        body, name=name, grid=(s // tm,),
        in_specs=[row, row, vec], out_specs=[row, row, vec, pl.BlockSpec((1, LANES), lambda i: (0, 0))],
        out_shape=[jax.ShapeDtypeStruct((s, d), F32), jax.ShapeDtypeStruct((s, d), BF16),
                   jax.ShapeDtypeStruct((1, d), F32), jax.ShapeDtypeStruct((1, LANES), F32)],
        compiler_params=_params("arbitrary"),
    )(x, tgt, g.reshape(1, d))


def matmul(pairs, mode, out_dtype, name, res=None, tm=512, tn=None, roles=(None, None), count=1):
    a0, b0 = pairs[0]
    m = a0.shape[-1] if mode == "tn" else a0.shape[-2]
    n = b0.shape[-2] if mode == "nt" else b0.shape[-1]
    if tn is None:
        tn = n if (mode != "tn" and n <= 1024) else 512
    tm, tn = _rows_tile(m, tm), _rows_tile(n, tn)
    dims = {"nn": _NN, "nt": _NT, "tn": _TN}[mode]
    npairs = len(pairs)
    batched, reducing = "batch" in roles, "reduce" in roles
    assert not (batched and reducing) and (res is None or not batched)
    n_in = 2 * npairs + (res is not None)

    def body(*refs):
        o_ref = refs[n_in]
        acc = None
        for p in range(npairs):
            a_ref, b_ref = refs[2 * p], refs[2 * p + 1]
            for kk in range(count if reducing else 1):
                a = a_ref[kk] if roles[0] == "reduce" else a_ref[...]
                b = b_ref[kk] if roles[1] == "reduce" else b_ref[...]
                d = _dot(a.astype(BF16), b.astype(BF16), dims)
                acc = d if acc is None else acc + d
        if res is not None:
            acc = acc + refs[2 * npairs][...]
        o_ref[...] = acc.astype(o_ref.dtype)

    def spec(arr, role, block, index):
        if arr.ndim == 2:
            return pl.BlockSpec(block, lambda bb, i, j: index(i, j))
        assert role in ("batch", "reduce") and arr.shape[0] == count, (name, role, arr.shape)
        if role == "batch":
            return pl.BlockSpec((None,) + block, lambda bb, i, j: (bb,) + index(i, j))
        return pl.BlockSpec((count,) + block, lambda bb, i, j: (0,) + index(i, j))

    in_specs, args = [], []
    for a, b in pairs:
        if mode == "tn":
            in_specs.append(spec(a, roles[0], (a.shape[-2], tm), lambda i, j: (0, i)))
        else:
            in_specs.append(spec(a, roles[0], (tm, a.shape[-1]), lambda i, j: (i, 0)))
        if mode == "nt":
            in_specs.append(spec(b, roles[1], (tn, b.shape[-1]), lambda i, j: (j, 0)))
        else:
            in_specs.append(spec(b, roles[1], (b.shape[-2], tn), lambda i, j: (0, j)))
        args += [a, b]
    if res is not None:
        in_specs.append(pl.BlockSpec((tm, tn), lambda bb, i, j: (i, j)))
        args.append(res)
    if batched:
        out_spec = pl.BlockSpec((None, tm, tn), lambda bb, i, j: (bb, i, j))
        out_shape = jax.ShapeDtypeStruct((count, m, n), out_dtype)
    else:
        out_spec = pl.BlockSpec((tm, tn), lambda bb, i, j: (i, j))
        out_shape = jax.ShapeDtypeStruct((m, n), out_dtype)
    return pl.pallas_call(
        body, name=name, grid=(count if batched else 1, m // tm, n // tn),
        in_specs=in_specs, out_specs=out_spec, out_shape=out_shape,
        compiler_params=_params("parallel", "parallel", "parallel"),
    )(*args)


def _sigmoid(v):
    return 0.5 * jnp.tanh(0.5 * v) + 0.5


def ffn_up(h, wgu, name):
    s, d = h.shape
    _, n_sh, _, fs = wgu.shape
    tm = _rows_tile(s, 512)

    def body(h_ref, w_ref, gu_ref, act_ref):
        hv = h_ref[...]
        gate = _dot(hv, w_ref[0])
        up = _dot(hv, w_ref[1])
        gu_ref[0] = gate.astype(BF16)
        gu_ref[1] = up.astype(BF16)
        act_ref[...] = ((gate * _sigmoid(gate)) * up).astype(BF16)

    return pl.pallas_call(
        body, name=name, grid=(n_sh, s // tm),
        in_specs=[pl.BlockSpec((tm, d), lambda j, i: (i, 0)), pl.BlockSpec((2, None, d, fs), lambda j, i: (0, j, 0, 0))],
        out_specs=[pl.BlockSpec((2, None, tm, fs), lambda j, i: (0, j, i, 0)),
                   pl.BlockSpec((None, tm, fs), lambda j, i: (j, i, 0))],
        out_shape=[jax.ShapeDtypeStruct((2, n_sh, s, fs), BF16), jax.ShapeDtypeStruct((n_sh, s, fs), BF16)],
        compiler_params=_params("parallel", "parallel"),
    )(h, wgu)


def ffn_bwd_act(dxo, wdown, gu, name):
    s, d = dxo.shape
    n_sh, fs, _ = wdown.shape
    tm = _rows_tile(s, 512)

    def body(dx_ref, w_ref, gu_ref, dgu_ref):
        dact = _dot(dx_ref[...], w_ref[...], _NT)
        gate = gu_ref[0].astype(F32)
        sig = _sigmoid(gate)
        dgu_ref[0] = ((dact * gu_ref[1].astype(F32)) * (sig * (1.0 + gate * (1.0 - sig)))).astype(BF16)
        dgu_ref[1] = (dact * (gate * sig)).astype(BF16)

    tile = pl.BlockSpec((2, None, tm, fs), lambda j, i: (0, j, i, 0))
    return pl.pallas_call(
        body, name=name, grid=(n_sh, s // tm),
        in_specs=[pl.BlockSpec((tm, d), lambda j, i: (i, 0)), pl.BlockSpec((None, fs, d), lambda j, i: (j, 0, 0)), tile],
        out_specs=tile,
        out_shape=jax.ShapeDtypeStruct((2, n_sh, s, fs), BF16),
        compiler_params=_params("parallel", "parallel"),
    )(dxo, wdown, gu)


def _pool_select(col, by_window):
    out = by_window[3]
    for g in (2, 1, 0):
        out = jnp.where(col < (g + 1) * POOL_GROUP, by_window[g], out)
    return out


def _pool_counts(pos):
    return [jnp.minimum(pos + 1, w).astype(F32) for w in POOL_WINDOWS]


def pool_fwd(proj, wbd, scale, name):
    s = proj.shape[0]
    tm = _rows_tile(s, 512)
    nt = s // tm
    width = SB_WIDTH
    per = tm // POOL_HALO

    def body(u_ref, halo_ref, w_ref, sc_ref, pooled_ref, grouped_ref):
        i = pl.program_id(0)
        u = u_ref[...]
        halo = jnp.where(i > 0, halo_ref[...], 0.0)
        ext = jnp.concatenate([halo, u], axis=0)
        sums = []
        acc = ext
        for k in (1, 2, 4, 8):
            acc = acc + pltpu.roll(acc, k, 0)
            sums.append(acc[POOL_HALO:])
        pos = i * tm + lax.broadcasted_iota(jnp.int32, (tm, width), 0)
        col = lax.broadcasted_iota(jnp.int32, (tm, width), 1)
        means = [sm / cnt for sm, cnt in zip(sums, _pool_counts(pos))]
        pooled = (_pool_select(col, means) - u).astype(BF16)
        pooled_ref[...] = pooled
        grouped_ref[...] = (_dot(pooled, w_ref[...]) * sc_ref[...]).astype(BF16)

    row = pl.BlockSpec((tm, width), lambda i: (i, 0))
    return pl.pallas_call(
        body, name=name, grid=(nt,),
        in_specs=[row, pl.BlockSpec((POOL_HALO, width), lambda i: (jnp.maximum(i * per - 1, 0), 0)),
                  pl.BlockSpec((width, width), lambda i: (0, 0)), pl.BlockSpec((1, width), lambda i: (0, 0))],
        out_specs=[row, row],
        out_shape=[jax.ShapeDtypeStruct((s, width), BF16), jax.ShapeDtypeStruct((s, width + MEM_WIDTH), BF16)],
        compiler_params=_params("parallel"),
    )(proj, proj, wbd, scale.reshape(1, width))


def pool_bwd(dcat, pooled, wbd, scale, name):
    s = dcat.shape[0]
    tm = _rows_tile(s, 512)
    nt = s // tm
    width = SB_WIDTH
    per = tm // POOL_HALO
    n_ext = tm + POOL_HALO

    def body(dg_ref, halo_ref, pooled_ref, w_ref, sc_ref, du_ref, dw_ref, dsc_ref):
        i = pl.program_id(0)
        w = w_ref[...]
        sc = sc_ref[...]
        dg = dg_ref[...]
        pooled = pooled_ref[...]
        pg = _dot(pooled, w)
        dpg = (dg * sc).astype(BF16)
        part_sc = jnp.sum(dg * pg, axis=0, keepdims=True)
        part_w = _dot(pooled, dpg, _TN)

        @pl.when(i == 0)
        def _():
            dsc_ref[...] = part_sc
            dw_ref[...] = part_w

        @pl.when(i > 0)
        def _():
            dsc_ref[...] += part_sc
            dw_ref[...] += part_w

        dpooled = _dot(dpg, w, _NT)
        halo_dpg = (jnp.where(i < nt - 1, halo_ref[...], 0.0) * sc).astype(BF16)
        ext = jnp.concatenate([dpooled, _dot(halo_dpg, w, _NT)], axis=0)
        pos = i * tm + lax.broadcasted_iota(jnp.int32, (n_ext, width), 0)
        col = lax.broadcasted_iota(jnp.int32, (tm, width), 1)
        outs = []
        for k_idx, cnt in enumerate(_pool_counts(pos)):
            acc = ext / cnt
            for k in (1, 2, 4, 8)[:k_idx + 1]:
                acc = acc + pltpu.roll(acc, n_ext - k, 0)
            outs.append(acc[:tm])
        du_ref[...] = (_pool_select(col, outs) - dpooled).astype(BF16)

    row = pl.BlockSpec((tm, width), lambda i: (i, 0))
    return pl.pallas_call(
        body, name=name, grid=(nt,),
        in_specs=[row, pl.BlockSpec((POOL_HALO, width), lambda i: (jnp.minimum((i + 1) * per, s // POOL_HALO - 1), 0)),
                  row, pl.BlockSpec((width, width), lambda i: (0, 0)), pl.BlockSpec((1, width), lambda i: (0, 0))],
        out_specs=[row, pl.BlockSpec((width, width), lambda i: (0, 0)), pl.BlockSpec((1, width), lambda i: (0, 0))],
        out_shape=[jax.ShapeDtypeStruct((s, width + MEM_WIDTH), BF16), jax.ShapeDtypeStruct((width, width), F32),
                   jax.ShapeDtypeStruct((1, width), F32)],
        compiler_params=_params("arbitrary"),
    )(dcat, dcat, pooled, wbd, scale.reshape(1, width))


def _head_masks(shape):
    lane = lax.broadcasted_iota(jnp.int32, shape, 1)
    return lane < HEAD_DIM, lane >= HEAD_DIM


def _mem_probs(qh, kp):
    logits = _dot(qh, kp, _NT)
    logits = logits - jnp.max(logits, axis=-1, keepdims=True)
    p = jnp.exp(logits)
    return p / jnp.sum(p, axis=-1, keepdims=True)


def mem_attn_fwd(proj, mkv, cat, name):
    s = proj.shape[0]
    m = mkv.shape[0]
    tm = _rows_tile(s, 512)
    q_blk = SB_WIDTH // MEM_WIDTH

    def body(q_ref, kv_ref, _, o_ref):
        masks = _head_masks((tm, LANES))
        for pr in range(MEM_WIDTH // LANES):
            q = q_ref[:, pr * LANES:(pr + 1) * LANES] * QK_SCALE
            kp = kv_ref[:, pr * LANES:(pr + 1) * LANES]
            vp = kv_ref[:, MEM_WIDTH + pr * LANES:MEM_WIDTH + (pr + 1) * LANES]
            outs = []
            for hm in masks:
                p = _mem_probs(jnp.where(hm, q, 0.0).astype(BF16), kp)
                outs.append(_dot(p.astype(BF16), vp))
            o_ref[:, pr * LANES:(pr + 1) * LANES] = jnp.where(masks[0], outs[0], outs[1]).astype(BF16)

    return pl.pallas_call(
        body, name=name, grid=(s // tm,),
        in_specs=[pl.BlockSpec((tm, MEM_WIDTH), lambda i: (i, q_blk)), pl.BlockSpec((m, 2 * MEM_WIDTH), lambda i: (0, 0)),
                  _ANY],
        out_specs=pl.BlockSpec((tm, MEM_WIDTH), lambda i: (i, q_blk)),
        out_shape=jax.ShapeDtypeStruct(cat.shape, BF16),
        input_output_aliases={2: 0},
        compiler_params=_params("parallel"),
    )(proj, mkv, cat)


def mem_attn_bwd(proj, mkv, dcat, dproj, name):
    s = proj.shape[0]
    m = mkv.shape[0]
    tm = _rows_tile(s, 512)
    q_blk = SB_WIDTH // MEM_WIDTH

    def body(q_ref, kv_ref, do_ref, _, dq_ref, dkv_ref):
        i = pl.program_id(0)

        @pl.when(i == 0)
        def _():
            dkv_ref[...] = jnp.zeros_like(dkv_ref)

        masks = _head_masks((tm, LANES))
        for pr in range(MEM_WIDTH // LANES):
            q = q_ref[:, pr * LANES:(pr + 1) * LANES] * QK_SCALE
            do = do_ref[:, pr * LANES:(pr + 1) * LANES]
            kp = kv_ref[:, pr * LANES:(pr + 1) * LANES]
            vp = kv_ref[:, MEM_WIDTH + pr * LANES:MEM_WIDTH + (pr + 1) * LANES]
            dqs = []
            dk = jnp.zeros((m, LANES), F32)
            dv = jnp.zeros((m, LANES), F32)
            for hm in masks:
                qh = jnp.where(hm, q, 0.0).astype(BF16)
                doh = jnp.where(hm, do, 0.0).astype(BF16)
                p = _mem_probs(qh, kp)
                dp = _dot(doh, vp, _NT)
                ds = (p * (dp - jnp.sum(dp * p, axis=-1, keepdims=True))).astype(BF16)
                dqs.append(_dot(ds, kp))
                dk = dk + _dot(ds, qh, _TN)
                dv = dv + _dot(p.astype(BF16), doh, _TN)
            dq_ref[:, pr * LANES:(pr + 1) * LANES] = (jnp.where(masks[0], dqs[0], dqs[1]) * QK_SCALE).astype(BF16)
            dkv_ref[:, pr * LANES:(pr + 1) * LANES] += dk
            dkv_ref[:, MEM_WIDTH + pr * LANES:MEM_WIDTH + (pr + 1) * LANES] += dv

    return pl.pallas_call(
        body, name=name, grid=(s // tm,),
        in_specs=[pl.BlockSpec((tm, MEM_WIDTH), lambda i: (i, q_blk)), pl.BlockSpec((m, 2 * MEM_WIDTH), lambda i: (0, 0)),
                  pl.BlockSpec((tm, MEM_WIDTH), lambda i: (i, q_blk)), _ANY],
        out_specs=[pl.BlockSpec((tm, MEM_WIDTH), lambda i: (i, q_blk)), pl.BlockSpec((m, 2 * MEM_WIDTH), lambda i: (0, 0))],
        out_shape=[jax.ShapeDtypeStruct(dproj.shape, BF16), jax.ShapeDtypeStruct((m, 2 * MEM_WIDTH), F32)],
        input_output_aliases={3: 0},
        compiler_params=_params("arbitrary"),
    )(proj, mkv, dcat, dproj)


SB_ROWS = 512
SB_KEYS = 512
SB_SUB = LANES


def _sb_tri(later):
    r = lax.broadcasted_iota(jnp.int32, (2 * SB_SUB, 2 * SB_SUB), 0) % SB_SUB
    c = lax.broadcasted_iota(jnp.int32, (2 * SB_SUB, 2 * SB_SUB), 1)
    tri = (r > c) if later else (r < c)
    return (tri | (c >= SB_SUB)).astype(BF16)


def _sb_sums(v, tri, n_sub):
    out = []
    for m in range(n_sub):
        vm = v[:, m * SB_SUB:(m + 1) * SB_SUB]
        hi = vm.astype(BF16)
        lo = (vm - hi.astype(F32)).astype(BF16)
        r = _dot(jnp.concatenate([hi, lo], axis=1), tri)
        out.append((r[:, :SB_SUB], r[:, SB_SUB:]))
    return out


def _sb_tri_pair_before():
    r = lax.broadcasted_iota(jnp.int32, (2 * SB_SUB, 2 * SB_SUB), 0)
    c = lax.broadcasted_iota(jnp.int32, (2 * SB_SUB, 2 * SB_SUB), 1)
    return ((r // SB_SUB == c // SB_SUB) & (r % SB_SUB < c % SB_SUB)).astype(BF16)


def _sb_prefix_pairs(e, tri_pair, n_sub):
    out = []
    for m in range(0, n_sub, 2):
        pair = e[:, m * SB_SUB:(m + 2) * SB_SUB]
        r = _dot(pair.astype(BF16), tri_pair)
        for h in range(2):
            em = pair[:, h * SB_SUB:(h + 1) * SB_SUB]
            out.append((r[:, h * SB_SUB:(h + 1) * SB_SUB], jnp.sum(em, axis=1, keepdims=True)))
    return out


def _sb_logs(nqh, k_blk, valid):
    nz = _dot(nqh, k_blk, _NT)
    log_not = jnp.minimum(nz, 0.0) - jnp.log(1.0 + jnp.exp2(jnp.abs(nz) * -LOG2_E))
    log_beta = log_not - nz
    if valid is not None:
        log_not = jnp.where(valid, log_not, 0.0)
    return log_not, log_beta


def _sb_weights(log_not, log_beta, tri_later, carry_in, valid, n_sub):
    sums = _sb_sums(log_not, tri_later, n_sub)
    run = carry_in
    later = [None] * n_sub
    for m in reversed(range(n_sub)):
        later[m] = sums[m][0] + run
        run = run + sums[m][1]
    w = jnp.exp(log_beta + jnp.concatenate(later, axis=1))
    if valid is not None:
        w = jnp.where(valid, w, 0.0)
    return w, run


def _pad_rows(a, r0):
    return a if r0 == 0 else jnp.concatenate([jnp.zeros((r0,) + a.shape[1:], a.dtype), a], axis=0)


def _sb_diag_parts(nqh, k_blk, tri_later, n_sub):
    tq = nqh.shape[0]
    parts = []
    for m in range(n_sub):
        r0 = m * SB_SUB
        row = lax.broadcasted_iota(jnp.int32, (tq - r0, SB_SUB), 0)
        col = lax.broadcasted_iota(jnp.int32, (tq - r0, SB_SUB), 1)
        valid = (row >= SB_SUB) | (row > col)
        log_not, log_beta = _sb_logs(nqh[r0:], k_blk[r0:r0 + SB_SUB], valid)
        (inside, row_sum), = _sb_sums(log_not, tri_later, 1)
        parts.append([r0, valid, log_beta, inside, row_sum])
    run = jnp.zeros((tq, LANES), F32)
    for part in reversed(parts):
        r0, row_sum = part[0], part[4]
        part[4] = run[r0:]
        run = run + _pad_rows(row_sum, r0)
    return parts, run


def _sb_diag_fwd(nqh, k_blk, v_blk, tri_later, n_sub):
    parts, run = _sb_diag_parts(nqh, k_blk, tri_later, n_sub)
    acc = jnp.zeros((nqh.shape[0], LANES), F32)
    for r0, valid, log_beta, inside, later_blocks in parts:
        w = jnp.where(valid, jnp.exp(log_beta + inside + later_blocks), 0.0)
        acc = acc + _pad_rows(_dot(w.astype(BF16), v_blk[r0:r0 + SB_SUB]), r0)
    return run, acc


def _sb_valid(i, j, tq, tk):
    qpos = i * tq + lax.broadcasted_iota(jnp.int32, (tq, tk), 0)
    kpos = j * tk + lax.broadcasted_iota(jnp.int32, (tq, tk), 1)
    return qpos > kpos


def sb_fwd(projb, kv, name, gather=()):
    s = projb.shape[0]
    tq, tk = _rows_tile(s, SB_ROWS), _rows_tile(s, SB_KEYS)
    n_sub = tk // SB_SUB
    n_pairs = SB_WIDTH // LANES
    assert s // tk <= LANES and tk % tq == 0

    n_g = len(gather)
    n_i = s // tq

    def body(*refs):
        q_ref, k_ref, v_ref = refs[:3]
        o_ref, c_ref = refs[3 + n_g:5 + n_g]
        i = pl.program_id(1)
        if n_g:
            start, wait = _exchange_ops(refs[3:3 + n_g], refs[5 + n_g:5 + 2 * n_g], refs[5 + 2 * n_g:], 0)
            pl.when((pl.program_id(0) == 0) & (i == 0))(start)
        j_diag = (i * tq) // tk
        masks = _head_masks((tq, LANES))
        nq = q_ref[...] * (-QK_SCALE)
        tri = _sb_tri(later=True)
        lane = lax.broadcasted_iota(jnp.int32, (tq, LANES), 1)
        nqh = [jnp.where(hm, nq, 0.0).astype(BF16) for hm in masks]

        def block(j, carry, valid):
            off = pl.multiple_of(j * tk, tk)
            k_blk, v_blk = k_ref[pl.ds(off, tk), :], v_ref[pl.ds(off, tk), :]
            out = []
            for hh in range(2):
                c, acc, cmat = carry[3 * hh:3 * hh + 3]
                log_not, log_beta = _sb_logs(nqh[hh], k_blk, valid)
                w, c_next = _sb_weights(log_not, log_beta, tri, c, valid, n_sub)
                out += [c_next, acc + _dot(w.astype(BF16), v_blk), jnp.where(lane == j, c, cmat)]
            return tuple(out)

        zero = jnp.zeros((tq, LANES), F32)
        if tq == tk:
            off = pl.multiple_of(i * tk, tk)
            k_blk, v_blk = k_ref[pl.ds(off, tk), :], v_ref[pl.ds(off, tk), :]
            carry = ()
            for hh in range(2):
                carry += _sb_diag_fwd(nqh[hh], k_blk, v_blk, tri, n_sub) + (zero,)
        else:
            carry = block(j_diag, (zero,) * 6, _sb_valid(i, j_diag, tq, tk))
        carry = lax.fori_loop(
            0, j_diag // 2,
            lambda t, cr: block(j_diag - 2 - 2 * t, block(j_diag - 1 - 2 * t, cr, None), None), carry)
        carry = lax.fori_loop(0, j_diag % 2, lambda jj, cr: block(jj, cr, None), carry)
        c_ref[0] = carry[2]
        c_ref[1] = carry[5]
        o_ref[...] = jnp.where(masks[0], carry[1], carry[4]).astype(BF16)
        if n_g:
            pl.when((pl.program_id(0) == n_pairs - 1) & (i == n_i - 1))(wait)

    outs = pl.pallas_call(
        body, name=name, grid=(n_pairs, n_i),
        in_specs=[pl.BlockSpec((tq, LANES), lambda p, i: (i, p)),
                  pl.BlockSpec((s, LANES), lambda p, i: (0, p)),
                  pl.BlockSpec((s, LANES), lambda p, i: (0, n_pairs + p))] + [_ANY] * n_g,
        out_specs=[pl.BlockSpec((tq, LANES), lambda p, i: (i, p)),
                   pl.BlockSpec((2, tq, LANES), lambda p, i: (p, i, 0))] + [_ANY] * n_g,
        out_shape=[jax.ShapeDtypeStruct((s, SB_WIDTH + MEM_WIDTH), BF16),
                   jax.ShapeDtypeStruct((N_SB_HEADS, s, LANES), F32)] + _exchange_out_shape([], gather),
        scratch_shapes=_exchange_sems(n_g) if n_g else [],
        compiler_params=_params("arbitrary", "arbitrary"),
    )(projb, kv, kv, *gather)
    return outs[0], outs[1], outs[2:]


def sb_bwd(projb, kv, dcat, csave, name, scatter=()):
    s = projb.shape[0]
    tq, tk = _rows_tile(s, SB_ROWS), _rows_tile(s, SB_KEYS)
    n_sub = tk // SB_SUB
    n_pairs = SB_WIDTH // LANES
    n_x = len(scatter)
    n_i = s // tq

    def body(*refs):
        q_ref, k_ref, v_ref, do_ref, c_ref = refs[:5]
        dq_ref, dkb_ref, dvb_ref = refs[5 + n_x:8 + n_x]
        dk_ref, dv_ref = refs[8 + 2 * n_x:10 + 2 * n_x]
        i = pl.program_id(1)
        if n_x:
            start, wait = _exchange_ops(refs[5:5 + n_x], refs[8 + n_x:8 + 2 * n_x], refs[10 + 2 * n_x:], n_x)
            pl.when((pl.program_id(0) == 0) & (i == 0))(start)
        j_diag = (i * tq) // tk

        @pl.when(i == 0)
        def _():
            dk_ref[...] = jnp.zeros_like(dk_ref)
            dv_ref[...] = jnp.zeros_like(dv_ref)

        masks = _head_masks((tq, LANES))
        qs = q_ref[...] * QK_SCALE
        do = do_ref[...]
        tri_later = _sb_tri(later=True)
        tri_before = _sb_tri_pair_before()
        lane = lax.broadcasted_iota(jnp.int32, (tq, LANES), 1)
        qh = [jnp.where(hm, qs, 0.0).astype(BF16) for hm in masks]
        nqh = [jnp.where(hm, -qs, 0.0).astype(BF16) for hm in masks]
        doh = [jnp.where(hm, do, 0.0).astype(BF16) for hm in masks]

        def block(j, carry, valid):
            off = pl.multiple_of(j * tk, tk)
            k_blk, v_blk = k_ref[pl.ds(off, tk), :], v_ref[pl.ds(off, tk), :]
            out, dk, dv = [], None, None
            for hh in range(2):
                e_before, dq = carry[2 * hh:2 * hh + 2]
                log_not, log_beta = _sb_logs(nqh[hh], k_blk, valid)
                c = jnp.sum(jnp.where(lane == j, c_ref[hh], 0.0), axis=1, keepdims=True)
                w, _ = _sb_weights(log_not, log_beta, tri_later, jnp.broadcast_to(c, (tq, LANES)), valid, n_sub)
                dv_h = _dot(w.astype(BF16), doh[hh], _TN)
                e = w * _dot(doh[hh], v_blk, _NT)
                sums = _sb_prefix_pairs(e, tri_before, n_sub)
                run = e_before
                e_pre = []
                for m in range(n_sub):
                    e_pre.append(sums[m][0] + run)
                    run = run + sums[m][1]
                dz = e - jnp.exp(log_beta) * (e + jnp.concatenate(e_pre, axis=1))
                if valid is not None:
                    dz = jnp.where(valid, dz, 0.0)
                dz = dz.astype(BF16)
                dk_h = _dot(dz, qh[hh], _TN)
                dk, dv = (dk_h, dv_h) if hh == 0 else (dk + dk_h, dv + dv_h)
                out += [run, dq + _dot(dz, k_blk)]
            dk_ref[pl.ds(off, tk), :] += dk
            dv_ref[pl.ds(off, tk), :] += dv
            return tuple(out)

        zero = jnp.zeros((tq, LANES), F32)
        carry = lax.fori_loop(0, j_diag // 2, lambda t, cr: block(2 * t + 1, block(2 * t, cr, None), None), (zero,) * 4)
        carry = lax.fori_loop(2 * (j_diag // 2), j_diag, lambda j, cr: block(j, cr, None), carry)
        carry = block(j_diag, carry, _sb_valid(i, j_diag, tq, tk))
        dq_ref[...] = (jnp.where(masks[0], carry[1], carry[3]) * QK_SCALE).astype(BF16)

        @pl.when(i == n_i - 1)
        def _():
            dkb_ref[...] = dk_ref[...].astype(BF16)
            dvb_ref[...] = dv_ref[...].astype(BF16)

        if n_x:
            pl.when((pl.program_id(0) == n_pairs - 1) & (i == n_i - 1))(wait)

    tile = pl.BlockSpec((tq, LANES), lambda p, i: (i, p))
    full = pl.BlockSpec((s, LANES), lambda p, i: (0, p))
    outs = pl.pallas_call(
        body, name=name, grid=(n_pairs, n_i),
        in_specs=[tile, full, pl.BlockSpec((s, LANES), lambda p, i: (0, n_pairs + p)), tile,
                  pl.BlockSpec((2, tq, LANES), lambda p, i: (p, i, 0))] + [_ANY] * n_x,
        out_specs=[tile, full, full] + [_ANY] * n_x,
        out_shape=[jax.ShapeDtypeStruct((s, SB_WIDTH + MEM_WIDTH), BF16), jax.ShapeDtypeStruct((s, SB_WIDTH), BF16),
                   jax.ShapeDtypeStruct((s, SB_WIDTH), BF16)] + _exchange_out_shape(scatter, []),
        scratch_shapes=[pltpu.VMEM((s, LANES), F32), pltpu.VMEM((s, LANES), F32)]
        + (_exchange_sems(n_x) if n_x else []),
        compiler_params=_params("arbitrary", "arbitrary"),
    )(projb, kv, kv, dcat, csave, *scatter)
    return outs[0], outs[1], outs[2], outs[3:]


def reduce_adamw(parts, w, m, v, name, after=None):
    n, r, c = parts.shape
    tr = next((t for t in (128, 176) if r % t == 0), r)
    bias1 = 1.0 - ADAM_B1 ** ADAM_STEP
    bias2 = 1.0 - ADAM_B2 ** ADAM_STEP

    def body(p_ref, w_ref, m_ref, v_ref, *rest):
        g_ref, d_ref, nm_ref, nv_ref = rest[-4:]
        g = p_ref[0].astype(F32)
        for k in range(1, n):
            g = g + p_ref[k].astype(F32)
        new_m = ADAM_B1 * m_ref[...] + (1.0 - ADAM_B1) * g
        new_v = ADAM_B2 * v_ref[...] + (1.0 - ADAM_B2) * (g * g)
        m_hat = new_m / bias1
        v_hat = new_v / bias2
        g_ref[...] = g
        d_ref[...] = -ADAM_LR * (m_hat / (jnp.sqrt(v_hat) + ADAM_EPS) + ADAM_WD * w_ref[...])
        nm_ref[...] = new_m
        nv_ref[...] = new_v

    row = pl.BlockSpec((tr, c), lambda i: (i, 0))
    return pl.pallas_call(
        body, name=name, grid=(r // tr,),
        in_specs=[pl.BlockSpec((n, tr, c), lambda i: (0, i, 0)), row, row, row] + ([] if after is None else [_ANY]),
        out_specs=[row, row, row, row],
        out_shape=[jax.ShapeDtypeStruct((r, c), F32)] * 4,
        compiler_params=_params("parallel"),
    )(parts, w, m, v, *(() if after is None else (after,)))


EARLY_0 = ("a_w_in", "a_w_mem_kv")
EARLY_1 = ("a_w_out",)
EARLY_2 = ("a_w_gu", "a_w_down")
EARLY_3 = ("w_kv", "b_w_q", "b_w_mem_kv")
EARLY = EARLY_0 + EARLY_1 + EARLY_2 + EARLY_3
LATE = ("b_w_out", "b_w_gu", "b_w_down")
BIG = EARLY + LATE
GATHER_3 = ("w_kv", "b_w_q")
GATHER_LATE = LATE + ("b_w_mem_kv",)
COL_SHARDED = ("a_w_gu", "w_kv", "b_w_gu")
SMALL_SHARDED = ("a_norm_mix", "a_scale", "a_norm_ffn")
SMALL_REPL = ("mem_norm", "kv_norm", "b_norm_mix", "b_norm_ffn", "final_norm")
WEIGHTS = ("mem_norm", "a_norm_mix", "a_w_in", "a_w_group", "a_scale", "a_w_mem_kv", "a_w_out", "a_norm_ffn", "a_w_gu",
           "a_w_down", "kv_norm", "w_kv", "b_norm_mix", "b_w_q", "b_w_mem_kv", "b_w_out", "b_norm_ffn", "b_w_gu",
           "b_w_down", "final_norm")


def _pad_row(v):
    v = v.reshape(1, -1)
    return jnp.pad(v, ((0, 0), (0, PACK_COLS - v.shape[1])))


def kernel(x, mem, mem_norm, a_norm_mix, a_w_in, a_w_group, a_scale, a_w_mem_kv, a_w_out, a_norm_ffn, a_w_gu, a_w_down, kv_norm, w_kv, b_norm_mix, b_w_q, b_w_mem_kv, b_w_out, b_norm_ffn, b_w_gu, b_w_down, final_norm, loss_target, m_mem_norm, m_a_norm_mix, m_a_w_in, m_a_w_group, m_a_scale, m_a_w_mem_kv, m_a_w_out, m_a_norm_ffn, m_a_w_gu, m_a_w_down, m_kv_norm, m_w_kv, m_b_norm_mix, m_b_w_q, m_b_w_mem_kv, m_b_w_out, m_b_norm_ffn, m_b_w_gu, m_b_w_down, m_final_norm, v_mem_norm, v_a_norm_mix, v_a_w_in, v_a_w_group, v_a_scale, v_a_w_mem_kv, v_a_w_out, v_a_norm_ffn, v_a_w_gu, v_a_w_down, v_kv_norm, v_w_kv, v_b_norm_mix, v_b_w_q, v_b_w_mem_kv, v_b_w_out, v_b_norm_ffn, v_b_w_gu, v_b_w_down, v_final_norm):
    w = dict(mem_norm=mem_norm, a_norm_mix=a_norm_mix, a_w_in=a_w_in, a_w_group=a_w_group, a_scale=a_scale,
             a_w_mem_kv=a_w_mem_kv, a_w_out=a_w_out, a_norm_ffn=a_norm_ffn, a_w_gu=a_w_gu, a_w_down=a_w_down,
             kv_norm=kv_norm, w_kv=w_kv, b_norm_mix=b_norm_mix, b_w_q=b_w_q, b_w_mem_kv=b_w_mem_kv, b_w_out=b_w_out,
             b_norm_ffn=b_norm_ffn, b_w_gu=b_w_gu, b_w_down=b_w_down, final_norm=final_norm)
    mom = dict(mem_norm=m_mem_norm, a_norm_mix=m_a_norm_mix, a_w_in=m_a_w_in, a_w_group=m_a_w_group, a_scale=m_a_scale,
               a_w_mem_kv=m_a_w_mem_kv, a_w_out=m_a_w_out, a_norm_ffn=m_a_norm_ffn, a_w_gu=m_a_w_gu,
               a_w_down=m_a_w_down, kv_norm=m_kv_norm, w_kv=m_w_kv, b_norm_mix=m_b_norm_mix, b_w_q=m_b_w_q,
               b_w_mem_kv=m_b_w_mem_kv, b_w_out=m_b_w_out, b_norm_ffn=m_b_norm_ffn, b_w_gu=m_b_w_gu,
               b_w_down=m_b_w_down, final_norm=m_final_norm)
    var = dict(mem_norm=v_mem_norm, a_norm_mix=v_a_norm_mix, a_w_in=v_a_w_in, a_w_group=v_a_w_group, a_scale=v_a_scale,
               a_w_mem_kv=v_a_w_mem_kv, a_w_out=v_a_w_out, a_norm_ffn=v_a_norm_ffn, a_w_gu=v_a_w_gu,
               a_w_down=v_a_w_down, kv_norm=v_kv_norm, w_kv=v_w_kv, b_norm_mix=v_b_norm_mix, b_w_q=v_b_w_q,
               b_w_mem_kv=v_b_w_mem_kv, b_w_out=v_b_w_out, b_norm_ffn=v_b_norm_ffn, b_w_gu=v_b_w_gu,
               b_w_down=v_b_w_down, final_norm=v_final_norm)

    me = 4 * lax.axis_index("x") + 2 * lax.axis_index("y") + lax.axis_index("c")
    shard2d = {n: w[n].shape[-2:] for n in BIG}
    shard = {n: w[n].reshape(shard2d[n]).astype(BF16) for n in BIG}
    n_sh = N_DEV // 2
    s_len = x.shape[1]

    small_send = jnp.concatenate([_pad_row(w[n]) for n in SMALL_SHARDED]
                                 + [jnp.zeros((8 - len(SMALL_SHARDED), PACK_COLS), F32)], axis=0)
    def own_slot(land, own):
        return lax.dynamic_update_slice(land, own[None], (me, 0, 0))

    def gather_done(handle, names, after, name):
        got = exchange_wait(handle, after, name)
        gathered.update({n: own_slot(g, shard[n]) for n, g in zip(names, got)})
        return got

    gathered = {}
    gather_0 = exchange_start([], [shard[n] for n in EARLY_0] + [small_send], x, "gather_0_start")
    gather_1 = exchange_start([], [shard[n] for n in EARLY_1], gather_0["token"], "gather_1_start")
    gather_2 = exchange_start([], [shard["a_w_gu"]], gather_1["token"], "gather_2_start")
    gather_2b = exchange_start([], [shard["a_w_down"]], gather_2["token"], "gather_2b_start")
    gather_3 = exchange_start([], [shard[n] for n in GATHER_3], gather_2b["token"], "gather_3_start")
    small_all = own_slot(gather_done(gather_0, EARLY_0, gather_3["token"], "gather_0_wait")[-1], small_send)
    a_norm_mix_f = small_all[:, 0, :a_norm_mix.shape[1]].reshape(-1)
    a_scale_f = small_all[:, 1, :a_scale.shape[1]].reshape(-1)
    a_norm_ffn_f = small_all[:, 2, :a_norm_ffn.shape[1]].reshape(-1)

    def rows_full(n):
        return gathered[n].reshape(-1, shard2d[n][1])

    def gu_views(tag):
        gu8 = gathered[tag + "_w_gu"]
        return gu8.reshape((2, n_sh) + gu8.shape[1:]), gu8

    def down_view(tag):
        down = gathered[tag + "_w_down"]
        return down.reshape(n_sh, -1, down.shape[2])

    wbd = jnp.zeros((SB_WIDTH, SB_WIDTH), BF16)
    for g in range(4):
        sl = slice(g * POOL_GROUP, (g + 1) * POOL_GROUP)
        wbd = wbd.at[sl, sl].set(a_w_group[0, g].astype(BF16))

    xs, mems, tgt = x[0], mem[0], loss_target[0]

    memn = rmsnorm_fwd(mems, mem_norm, "memn")
    (h0, proj), = norm_matmul(xs, [(a_norm_mix_f, rows_full("a_w_in"), F32)], "proj_a")
    mkv_a = matmul([(memn, rows_full("a_w_mem_kv"))], "nn", BF16, "mkv_a")
    pooled, cat_a = pool_fwd(proj, wbd, a_scale_f, "pool_fwd")
    cat_a = mem_attn_fwd(proj, mkv_a, cat_a, "mem_fwd_a")
    gather_done(gather_1, EARLY_1, cat_a, "gather_1_wait")
    x1 = matmul([(cat_a, rows_full("a_w_out"))], "nn", F32, "out_a", res=xs)
    h1 = rmsnorm_fwd(x1, a_norm_ffn_f, "h1")
    gather_done(gather_2, ("a_w_gu",), h1, "gather_2_wait")
    wgu_a, wgu8_a = gu_views("a")
    gu_a, act_a = ffn_up(h1, wgu_a, "ffn_up_a")
    gather_done(gather_2b, ("a_w_down",), act_a, "gather_2b_wait")
    wdown_a = down_view("a")
    x2 = matmul([(act_a, wdown_a)], "nn", F32, "down_a", res=x1, roles=("reduce", "reduce"), count=n_sh)

    gather_done(gather_3, GATHER_3, x2, "gather_3_wait")
    wkv = jnp.transpose(gathered["w_kv"], (1, 0, 2)).reshape(shard2d["w_kv"][0], -1)
    (hk, kv), (h3, projb) = norm_matmul(
        x2, [(kv_norm, wkv, BF16), (b_norm_mix[0], rows_full("b_w_q"), F32)], "kv_proj_b")
    cat_b, csave, got = sb_fwd(projb, kv, "sb_fwd", gather=[shard[n] for n in GATHER_LATE])
    gathered.update(zip(GATHER_LATE, got))
    mkv_b = matmul([(memn, rows_full("b_w_mem_kv"))], "nn", BF16, "mkv_b")
    cat_b = mem_attn_fwd(projb, mkv_b, cat_b, "mem_fwd_b")
    x3 = matmul([(cat_b, rows_full("b_w_out"))], "nn", F32, "out_b", res=x2)
    h4 = rmsnorm_fwd(x3, b_norm_ffn[0], "h4")
    (wgu_b, wgu8_b), wdown_b = gu_views("b"), down_view("b")
    gu_b, act_b = ffn_up(h4, wgu_b, "ffn_up_b")
    x4 = matmul([(act_b, wdown_b)], "nn", F32, "down_b", res=x3, roles=("reduce", "reduce"), count=n_sh)

    grads = {}
    dx4, dx4_bf, grads["final_norm"], loss_part = loss_head(x4, tgt, final_norm, "loss_head")

    def blocks(n, g):
        return g.reshape((N_DEV,) + tuple(shard2d[n]))

    def ffn_weight_grads(tag, dx_out_bf, h, gu, act, wgu8, wdown):
        fs = wgu8.shape[2]
        dgu8 = ffn_bwd_act(dx_out_bf, wdown, gu, "ffn_bwd_act_" + tag).reshape(N_DEV, s_len, fs)
        d_wdown = matmul([(act, dx_out_bf)], "tn", BF16, "dw_down_" + tag, tm=fs, roles=("batch", None), count=n_sh)
        grads[tag + "_w_gu"] = matmul([(dgu8, h)], "tn", BF16, "dw_gu_" + tag, tm=fs, roles=("batch", None), count=N_DEV)
        grads[tag + "_w_down"] = blocks(tag + "_w_down", d_wdown)
        return dgu8

    def ffn_input_grads(tag, x_in, dx_out, dgu8, wgu8, g_ffn):
        dx_in, dx_in_bf, dgs = matmul_norm_bwd(x_in, dx_out, [(g_ffn, [(dgu8, wgu8)])], "norm_bwd_ffn_" + tag)
        return dx_in, dx_in_bf, dgs[0]

    def grads_start(names, small, after, name):
        return exchange_start([grads[n] for n in names], small, after, name)

    def grads_done(handle, names, after, name):
        got = exchange_wait(handle, after, name)
        for n, g in zip(names, got):
            recv[n] = lax.dynamic_update_slice(g, lax.dynamic_slice(grads[n], (me, 0, 0), (1,) + g.shape[1:]), (me, 0, 0))
        return got

    recv = {}
    dgu8_b = ffn_weight_grads("b", dx4_bf, h4, gu_b, act_b, wgu8_b, wdown_b)
    dx3, dx3_bf, grads["b_norm_ffn"] = ffn_input_grads("b", x3, dx4, dgu8_b, wgu8_b, b_norm_ffn[0])

    dcat_b = matmul([(dx3_bf, rows_full("b_w_out"))], "nt", F32, "dcat_b")
    grads["b_w_out"] = blocks("b_w_out", matmul([(cat_b, dx3_bf)], "tn", BF16, "dw_out_b"))
    dprojb, dk, dv, recv_late = sb_bwd(projb, kv, dcat_b, csave, "sb_bwd", scatter=[grads[n] for n in LATE])
    dprojb, dmkv_b = mem_attn_bwd(projb, mkv_b, dcat_b, dprojb, "mem_bwd_b")
    grads["b_w_q"] = blocks("b_w_q", matmul([(h3, dprojb)], "tn", BF16, "dw_q_b"))
    d_wkv_t = jnp.concatenate([matmul([(dk, hk)], "tn", BF16, "dw_k", tm=SB_WIDTH),
                               matmul([(dv, hk)], "tn", BF16, "dw_v", tm=SB_WIDTH)], axis=0)
    grads["w_kv"] = d_wkv_t.reshape(N_DEV, -1, d_wkv_t.shape[1])
    dx2, dx2_bf, dgs = matmul_norm_bwd(
        x2, dx3, [(b_norm_mix[0], [(dprojb, rows_full("b_w_q"))]),
                  (kv_norm, [(dk, wkv[:, :SB_WIDTH]), (dv, wkv[:, SB_WIDTH:])])], "norm_bwd_x2")
    grads["b_norm_mix"], grads["kv_norm"] = dgs
    dmkv_b_bf = dmkv_b.astype(BF16)
    grads["b_w_mem_kv"] = blocks("b_w_mem_kv", matmul([(memn, dmkv_b_bf)], "tn", BF16, "dw_mkv_b"))

    grads_3 = grads_start(EARLY_3, [], grads["b_w_mem_kv"], "grads_3_start")
    dgu8_a = ffn_weight_grads("a", dx2_bf, h1, gu_a, act_a, wgu8_a, wdown_a)
    grads_2 = grads_start(EARLY_2, [], grads_3["token"], "grads_2_start")
    dx1, dx1_bf, grads["a_norm_ffn"] = ffn_input_grads(
        "a", x1, dx2, dgu8_a, wgu8_a, a_norm_ffn_f + grads_2["token"][0, 0])

    dcat_a = matmul([(dx1_bf, rows_full("a_w_out"))], "nt", F32, "dcat_a")
    grads["a_w_out"] = blocks("a_w_out", matmul([(cat_a, dx1_bf)], "tn", BF16, "dw_out_a"))
    dproj, d_wbd, grads["a_scale"] = pool_bwd(dcat_a, pooled, wbd, a_scale_f, "pool_bwd")
    grads["a_w_group"] = jnp.stack(
        [d_wbd[g * POOL_GROUP:(g + 1) * POOL_GROUP, g * POOL_GROUP:(g + 1) * POOL_GROUP] for g in range(4)])
    group_grads = grads["a_w_group"].reshape(-1, PACK_COLS).astype(BF16)
    grads_1 = exchange_start([grads[n] for n in EARLY_1], [group_grads], grads_2["token"], "grads_1_start")
    dproj, dmkv_a = mem_attn_bwd(proj, mkv_a, dcat_a, dproj, "mem_bwd_a")
    grads["a_w_in"] = blocks("a_w_in", matmul([(h0, dproj)], "tn", BF16, "dw_in_a"))
    dmkv_a_bf = dmkv_a.astype(BF16)
    grads["a_w_mem_kv"] = blocks("a_w_mem_kv", matmul([(memn, dmkv_a_bf)], "tn", BF16, "dw_mkv_a"))
    grads_0 = grads_start(EARLY_0, [], grads_1["token"], "grads_0_start")
    dx, _, dgs = matmul_norm_bwd(
        xs, dx1, [(a_norm_mix_f + grads_0["token"][0, 0], [(dproj, rows_full("a_w_in"))])], "norm_bwd_x")
    grads["a_norm_mix"] = dgs[0]
    dmemn = matmul([(dmkv_a_bf, rows_full("a_w_mem_kv")), (dmkv_b_bf, rows_full("b_w_mem_kv"))], "nt", F32, "dmemn")
    _, _, dgs = rmsnorm_bwd(mems, jnp.zeros_like(mems), [(mem_norm, dmemn)], "norm_bwd_mem")
    grads["mem_norm"] = dgs[0]

    small_names = SMALL_REPL + SMALL_SHARDED
    head_rows = 16

    def small_pack(rows_of):
        head = jnp.zeros((head_rows, PACK_COLS), F32)
        for r, (v, col) in enumerate(rows_of):
            v = v.reshape(1, -1)
            if col is None:
                head = head + jnp.pad(v, ((r, head_rows - 1 - r), (0, PACK_COLS - v.shape[1])))
            else:
                head = lax.dynamic_update_slice(head, v, (r, col))
        return head

    small_grads = small_pack([(grads[n], None) for n in small_names] + [(loss_part[:, :1], None)])
    grads_s = exchange_start([], [small_grads], dx, "grads_s_start")

    def small_state(src):
        return small_pack([(src[n], None) for n in SMALL_REPL]
                          + [(src[n], me * src[n].shape[-1]) for n in SMALL_SHARDED])

    def state2d(src, n):
        a = src[n].reshape(shard2d[n])
        return a.T if n in COL_SHARDED else a

    def adamw(names, after):
        for n in names:
            big_out[n] = reduce_adamw(recv[n], state2d(w, n), state2d(mom, n), state2d(var, n), "adamw_" + n, after=after)
        return big_out[names[-1]][0]

    big_out = {}
    recv.update(zip(LATE, recv_late))
    done = adamw(LATE, grads_s["token"])
    grads_done(grads_3, EARLY_3, done, "grads_3_wait")
    done = adamw(EARLY_3, None)
    grads_done(grads_2, EARLY_2, done, "grads_2_wait")
    done = adamw(EARLY_2, None)
    group_recv = own_slot(grads_done(grads_1, EARLY_1, done, "grads_1_wait")[-1], group_grads)
    done = adamw(EARLY_1, None)
    group_out = reduce_adamw(group_recv, *[src["a_w_group"].reshape(-1, PACK_COLS) for src in (w, mom, var)], "adamw_group")
    grads_done(grads_0, EARLY_0, group_out[0], "grads_0_wait")
    done = adamw(EARLY_0, None)
    small_recv = own_slot(exchange_wait(grads_s, done, "grads_s_wait")[0], small_grads)
    small_out = reduce_adamw(small_recv, small_state(w), small_state(mom), small_state(var), "adamw_small")

    def unpack(kind):
        out = {n: (big_out[n][kind].T if n in COL_SHARDED else big_out[n][kind]).reshape(w[n].shape) for n in BIG}
        so = small_out[kind]
        for r, n in enumerate(SMALL_REPL):
            out[n] = so[r, :w[n].shape[-1]].reshape(w[n].shape)
        for r, n in enumerate(SMALL_SHARDED):
            width = w[n].shape[-1]
            out[n] = lax.dynamic_slice(so, (len(SMALL_REPL) + r, me * width), (1, width)).reshape(w[n].shape)
        out["a_w_group"] = group_out[kind].reshape(a_w_group.shape)
        return out

    loss = small_out[0][len(small_names), 0]
    results = [loss, dx.reshape(x.shape)]
    for kind in range(4):
        out = unpack(kind)
        results += [out[n] for n in WEIGHTS]
    return tuple(results)
```

```python
import functools

import jax
import jax.numpy as jnp
from jax import lax
from jax.experimental import pallas as pl
from jax.experimental.pallas import tpu as pltpu

F32 = jnp.float32
BF16 = jnp.bfloat16

N_DEV = 8
HEAD_DIM = 64
N_SB_HEADS = 12
SB_WIDTH = N_SB_HEADS * HEAD_DIM
MEM_WIDTH = 4 * HEAD_DIM
POOL_WINDOWS = (2, 4, 8, 16)
POOL_GROUP = SB_WIDTH // 4
POOL_HALO = 16
EPS = 1e-6
QK_SCALE = HEAD_DIM ** -0.5
LOG2_E = 1.4426950408889634
LANES = 128
PACK_COLS = 1024

ADAM_LR = 0.001
ADAM_B1 = 0.9
ADAM_B2 = 0.999
ADAM_EPS = 1e-08
ADAM_WD = 0.01
ADAM_STEP = 10

VMEM_LIMIT = 48 * 1024 * 1024

_NT = (((1,), (1,)), ((), ()))
_TN = (((0,), (0,)), ((), ()))
_NN = (((1,), (0,)), ((), ()))


def _params(*sem):
    return pltpu.CompilerParams(dimension_semantics=sem, vmem_limit_bytes=VMEM_LIMIT)


def _dot(a, b, dims=_NN):
    return lax.dot_general(a, b, dims, preferred_element_type=F32)


def exchange(a2a, ag, name):
    n_a2a, n_arr = len(a2a), len(a2a) + len(ag)

    def body(*refs):
        start, wait = _exchange_ops(refs[:n_arr], refs[n_arr:2 * n_arr], refs[2 * n_arr:], n_a2a)
        start()
        wait()

    return pl.pallas_call(
        body, name=name, out_shape=_exchange_out_shape(a2a, ag),
        in_specs=[_ANY] * n_arr, out_specs=[_ANY] * n_arr,
        scratch_shapes=_exchange_sems(n_arr),
    )(*a2a, *ag)


_ANY = pl.BlockSpec(memory_space=pl.ANY)
_HBM = pl.BlockSpec(memory_space=pltpu.HBM)
_SEM = pl.BlockSpec(memory_space=pltpu.SEMAPHORE)
_DATAFLOW = pltpu.SideEffectType.DATAFLOW_SIDE_EFFECTING


def exchange_start(a2a, ag, after, name):
    n_a2a, n_arr = len(a2a), len(a2a) + len(ag)
    arrays = [pltpu.with_memory_space_constraint(v, pltpu.HBM) for v in list(a2a) + list(ag)]
    shapes = _exchange_out_shape(a2a, ag)
    lands = [pltpu.with_memory_space_constraint(lax.empty(sh.shape, sh.dtype), pltpu.HBM) for sh in shapes]

    def body(*refs):
        ins, outs = refs[:n_arr], refs[n_arr:2 * n_arr]
        send_sems, recv_sems, token = refs[4 * n_arr + 1:4 * n_arr + 4]
        start, _ = _exchange_ops(ins, outs, (send_sems, recv_sems, None), n_a2a, local_copies=False)
        start()
        token[...] = jnp.zeros_like(token)

    outs = pl.pallas_call(
        body, name=name,
        out_shape=[pltpu.HBM(v.shape, v.dtype) for v in arrays] + [pltpu.HBM(sh.shape, sh.dtype) for sh in shapes]
        + [pltpu.SemaphoreType.DMA((n_arr * (N_DEV - 1),)), pltpu.SemaphoreType.DMA((n_arr * (N_DEV - 1),)),
           jax.ShapeDtypeStruct((8, LANES), F32)],
        in_specs=[_HBM] * (2 * n_arr) + [_ANY],
        out_specs=[_HBM] * (2 * n_arr) + [_SEM, _SEM, pl.BlockSpec(memory_space=pltpu.VMEM)],
        input_output_aliases={i: i for i in range(2 * n_arr)},
        compiler_params=pltpu.CompilerParams(has_side_effects=_DATAFLOW),
    )(*arrays, *lands, after)
    return dict(ins=outs[:n_arr], lands=outs[n_arr:2 * n_arr], sems=outs[2 * n_arr:2 * n_arr + 2], n_a2a=n_a2a,
                token=outs[2 * n_arr + 2])


def exchange_wait(handle, after, name):
    ins, lands, sems, n_a2a = handle["ins"], handle["lands"], handle["sems"], handle["n_a2a"]
    n_arr = len(ins)

    def body(*refs):
        in_refs, land_refs = refs[:n_arr], refs[n_arr:2 * n_arr]
        send_sems, recv_sems = refs[2 * n_arr:2 * n_arr + 2]
        _, wait = _exchange_ops(in_refs, land_refs, (send_sems, recv_sems, None), n_a2a, local_copies=False)
        wait()

    outs = pl.pallas_call(
        body, name=name,
        out_shape=[pltpu.HBM(v.shape, v.dtype) for v in list(ins) + list(lands)],
        in_specs=[_HBM] * (2 * n_arr) + [_SEM, _SEM, _ANY], out_specs=[_HBM] * (2 * n_arr),
        input_output_aliases={i: i for i in range(2 * n_arr)},
        compiler_params=pltpu.CompilerParams(has_side_effects=_DATAFLOW),
    )(*ins, *lands, *sems, after)
    return outs[n_arr:]


def _exchange_out_shape(a2a, ag):
    out_shape = [jax.ShapeDtypeStruct(v.shape, v.dtype) for v in a2a]
    return out_shape + [jax.ShapeDtypeStruct((N_DEV,) + v.shape, v.dtype) for v in ag]


def _exchange_sems(n_arr):
    return [pltpu.SemaphoreType.DMA((n_arr * (N_DEV - 1),)), pltpu.SemaphoreType.DMA((n_arr * (N_DEV - 1),)),
            pltpu.SemaphoreType.DMA((n_arr,))]


def _exchange_ops(ins, outs, sems, n_a2a, local_copies=True):
    send_sems, recv_sems, local_sems = sems
    n_arr = len(ins)
    x, y, c = lax.axis_index("x"), lax.axis_index("y"), lax.axis_index("c")
    me = 4 * x + 2 * y + c

    def peer_of(k):
        px = 1 - x if k & 4 else x
        py = 1 - y if k & 2 else y
        pc = 1 - c if k & 1 else c
        return (px, py, pc), 4 * px + 2 * py + pc

    def copy(a, k, landing):
        peer, pid = peer_of(k)
        src = ins[a].at[pid] if a < n_a2a else ins[a]
        return pltpu.make_async_remote_copy(
            src_ref=src, dst_ref=outs[a].at[pid if landing else me], send_sem=send_sems.at[a * (N_DEV - 1) + k - 1],
            recv_sem=recv_sems.at[a * (N_DEV - 1) + k - 1], device_id=peer, device_id_type=pl.DeviceIdType.MESH)

    def local(a):
        src = ins[a].at[me] if a < n_a2a else ins[a]
        return pltpu.make_async_copy(src, outs[a].at[me], local_sems.at[a])

    order = (6, 7, 4, 5, 2, 3, 1)

    def start():
        if local_copies:
            for a in range(n_arr):
                local(a).start()
        for a in range(n_arr):
            for k in order:
                copy(a, k, landing=False).start()

    def wait():
        for a in range(n_arr):
            for k in order:
                copy(a, k, landing=True).wait_recv()
        for a in range(n_arr):
            for k in order:
                copy(a, k, landing=False).wait_send()
        if local_copies:
            for a in range(n_arr):
                local(a).wait()

    return start, wait


def _rows_tile(n, want):
    t = min(want, n)
    assert n % t == 0, (n, t)
    return t


def rmsnorm_fwd(x, g, name):
    s, d = x.shape
    tm = _rows_tile(s, 512)

    def body(x_ref, g_ref, h_ref):
        xv = x_ref[...]
        r = lax.rsqrt(jnp.mean(xv * xv, axis=-1, keepdims=True) + EPS)
        h_ref[...] = ((xv * r) * g_ref[...]).astype(h_ref.dtype)

    return pl.pallas_call(
        body, name=name, grid=(s // tm,),
        in_specs=[pl.BlockSpec((tm, d), lambda i: (i, 0)), pl.BlockSpec((1, d), lambda i: (0, 0))],
        out_specs=pl.BlockSpec((tm, d), lambda i: (i, 0)),
        out_shape=jax.ShapeDtypeStruct((s, d), BF16),
        compiler_params=_params("parallel"),
    )(x, g.reshape(1, d))


def norm_matmul(x, heads, name):
    s, d = x.shape
    tm = _rows_tile(s, 512)
    n = len(heads)

    def body(*refs):
        x_ref = refs[0]
        xv = x_ref[...]
        xhat = xv * lax.rsqrt(jnp.mean(xv * xv, axis=-1, keepdims=True) + EPS)
        for k in range(n):
            g_ref, w_ref = refs[1 + 2 * k], refs[2 + 2 * k]
            h_ref, y_ref = refs[1 + 2 * n + 2 * k], refs[2 + 2 * n + 2 * k]
            h = (xhat * g_ref[...]).astype(BF16)
            h_ref[...] = h
            y_ref[...] = _dot(h, w_ref[...]).astype(y_ref.dtype)

    row = pl.BlockSpec((tm, d), lambda i: (i, 0))
    in_specs, out_specs, out_shape, args = [row], [], [], [x]
    for g, w_k, dtype in heads:
        in_specs += [pl.BlockSpec((1, d), lambda i: (0, 0)), pl.BlockSpec(w_k.shape, lambda i: (0, 0))]
        args += [g.reshape(1, d), w_k]
        out_specs += [row, pl.BlockSpec((tm, w_k.shape[1]), lambda i: (i, 0))]
        out_shape += [jax.ShapeDtypeStruct((s, d), BF16), jax.ShapeDtypeStruct((s, w_k.shape[1]), dtype)]
    outs = pl.pallas_call(
        body, name=name, grid=(s // tm,), in_specs=in_specs, out_specs=out_specs, out_shape=out_shape,
        compiler_params=_params("parallel"),
    )(*args)
    return [(outs[2 * k], outs[2 * k + 1]) for k in range(n)]


def rmsnorm_bwd(x, dres, pairs, name):
    s, d = x.shape
    tm = _rows_tile(s, 256)
    n = len(pairs)

    def body(*refs):
        x_ref, dres_ref = refs[0], refs[1]
        g_refs, dh_refs = refs[2:2 + n], refs[2 + n:2 + 2 * n]
        dx_ref, dxb_ref = refs[2 + 2 * n], refs[3 + 2 * n]
        dg_refs = refs[4 + 2 * n:]
        i = pl.program_id(0)
        xv = x_ref[...]
        r = lax.rsqrt(jnp.mean(xv * xv, axis=-1, keepdims=True) + EPS)
        xhat = xv * r
        acc = dres_ref[...]
        for k in range(n):
            dh = dh_refs[k][...].astype(F32)
            part = jnp.sum(dh * xhat, axis=0, keepdims=True)

            @pl.when(i == 0)
            def _(k=k, part=part):
                dg_refs[k][...] = part

            @pl.when(i > 0)
            def _(k=k, part=part):
                dg_refs[k][...] += part

            dxh = dh * g_refs[k][...]
            acc = acc + r * (dxh - xhat * jnp.mean(dxh * xhat, axis=-1, keepdims=True))
        dx_ref[...] = acc
        dxb_ref[...] = acc.astype(BF16)

    row = pl.BlockSpec((tm, d), lambda i: (i, 0))
    vec = pl.BlockSpec((1, d), lambda i: (0, 0))
    outs = pl.pallas_call(
        body, name=name, grid=(s // tm,),
        in_specs=[row, row] + [vec] * n + [row] * n,
        out_specs=[row, row] + [vec] * n,
        out_shape=[jax.ShapeDtypeStruct((s, d), F32), jax.ShapeDtypeStruct((s, d), BF16)]
        + [jax.ShapeDtypeStruct((1, d), F32)] * n,
        compiler_params=_params("arbitrary"),
    )(x, dres, *[g.reshape(1, d) for g, _ in pairs], *[dh for _, dh in pairs])
    return outs[0], outs[1], outs[2:]


def matmul_norm_bwd(x, dres, heads, name):
    s, d = x.shape
    tm = _rows_tile(s, 512)
    n = len(heads)
    n_ops = [len(pairs) for _, pairs in heads]

    def body(*refs):
        x_ref, dres_ref = refs[0], refs[1]
        n_in = 2 + n + 2 * sum(n_ops)
        dx_ref, dxb_ref = refs[n_in], refs[n_in + 1]
        dg_refs = refs[n_in + 2:]
        i = pl.program_id(0)
        xv = x_ref[...]
        r = lax.rsqrt(jnp.mean(xv * xv, axis=-1, keepdims=True) + EPS)
        xhat = xv * r
        acc = dres_ref[...]
        pos = 2 + n
        for k in range(n):
            dh = None
            for a, _ in heads[k][1]:
                a_ref, w_ref = refs[pos], refs[pos + 1]
                for kk in range(a.shape[0] if a.ndim == 3 else 1):
                    prod = _dot(a_ref[kk], w_ref[kk], _NT) if a.ndim == 3 else _dot(a_ref[...], w_ref[...], _NT)
                    dh = prod if dh is None else dh + prod
                pos += 2
            part = jnp.sum(dh * xhat, axis=0, keepdims=True)

            @pl.when(i == 0)
            def _(k=k, part=part):
                dg_refs[k][...] = part

            @pl.when(i > 0)
            def _(k=k, part=part):
                dg_refs[k][...] += part

            dxh = dh * refs[2 + k][...]
            acc = acc + r * (dxh - xhat * jnp.mean(dxh * xhat, axis=-1, keepdims=True))
        dx_ref[...] = acc
        dxb_ref[...] = acc.astype(BF16)

    row = pl.BlockSpec((tm, d), lambda i: (i, 0))
    vec = pl.BlockSpec((1, d), lambda i: (0, 0))
    in_specs, args = [row, row] + [vec] * n, [x, dres] + [g.reshape(1, d) for g, _ in heads]
    for _, pairs in heads:
        for a, w_k in pairs:
            if a.ndim == 3:
                in_specs += [pl.BlockSpec((a.shape[0], tm, a.shape[2]), lambda i: (0, i, 0)),
                             pl.BlockSpec(w_k.shape, lambda i: (0, 0, 0), pipeline_mode=pl.Buffered(1))]
            else:
                in_specs += [pl.BlockSpec((tm, a.shape[1]), lambda i: (i, 0)),
                             pl.BlockSpec(w_k.shape, lambda i: (0, 0), pipeline_mode=pl.Buffered(1))]
            args += [a, w_k]
    outs = pl.pallas_call(
        body, name=name, grid=(s // tm,), in_specs=in_specs, out_specs=[row, row] + [vec] * n,
        out_shape=[jax.ShapeDtypeStruct((s, d), F32), jax.ShapeDtypeStruct((s, d), BF16)]
        + [jax.ShapeDtypeStruct((1, d), F32)] * n,
        compiler_params=_params("arbitrary"),
    )(*args)
    return outs[0], outs[1], outs[2:]


def loss_head(x, tgt, g, name):
    s, d = x.shape
    tm = _rows_tile(s, 512)

    def body(x_ref, t_ref, g_ref, dx_ref, dxb_ref, dg_ref, loss_ref):
        i = pl.program_id(0)
        xv = x_ref[...]
        gv = g_ref[...]
        r = lax.rsqrt(jnp.mean(xv * xv, axis=-1, keepdims=True) + EPS)
        xhat = xv * r
        diff = xhat * gv - t_ref[...]
        part_loss = 0.5 * jnp.sum(jnp.mean(diff * diff, axis=-1, keepdims=True), axis=0, keepdims=True)
        dy = diff * (1.0 / d)
        part_dg = jnp.sum(dy * xhat, axis=0, keepdims=True)

        @pl.when(i == 0)
        def _():
            dg_ref[...] = part_dg
            loss_ref[...] = jnp.broadcast_to(part_loss, loss_ref.shape)

        @pl.when(i > 0)
        def _():
            dg_ref[...] += part_dg
            loss_ref[...] += jnp.broadcast_to(part_loss, loss_ref.shape)

        dxh = dy * gv
        dx = r * (dxh - xhat * jnp.mean(dxh * xhat, axis=-1, keepdims=True))
        dx_ref[...] = dx
        dxb_ref[...] = dx.astype(BF16)

    row = pl.BlockSpec((tm, d), lambda i: (i, 0))
    vec = pl.BlockSpec((1, d), lambda i: (0, 0))
    return pl.pallas_call(
        body, name=name, grid=(s // tm,),
        in_specs=[row, row, vec], out_specs=[row, row, vec, pl.BlockSpec((1, LANES), lambda i: (0, 0))],
        out_shape=[jax.ShapeDtypeStruct((s, d), F32), jax.ShapeDtypeStruct((s, d), BF16),
                   jax.ShapeDtypeStruct((1, d), F32), jax.ShapeDtypeStruct((1, LANES), F32)],
        compiler_params=_params("arbitrary"),
    )(x, tgt, g.reshape(1, d))


def matmul(pairs, mode, out_dtype, name, res=None, tm=512, tn=None, roles=(None, None), count=1):
    a0, b0 = pairs[0]
    m = a0.shape[-1] if mode == "tn" else a0.shape[-2]
    n = b0.shape[-2] if mode == "nt" else b0.shape[-1]
    if tn is None:
        tn = n if (mode != "tn" and n <= 1024) else 512
    tm, tn = _rows_tile(m, tm), _rows_tile(n, tn)
    dims = {"nn": _NN, "nt": _NT, "tn": _TN}[mode]
    npairs = len(pairs)
    batched, reducing = "batch" in roles, "reduce" in roles
    assert not (batched and reducing) and (res is None or not batched)
    n_in = 2 * npairs + (res is not None)

    def body(*refs):
        o_ref = refs[n_in]
        acc = None
        for p in range(npairs):
            a_ref, b_ref = refs[2 * p], refs[2 * p + 1]
            for kk in range(count if reducing else 1):
                a = a_ref[kk] if roles[0] == "reduce" else a_ref[...]
                b = b_ref[kk] if roles[1] == "reduce" else b_ref[...]
                d = _dot(a.astype(BF16), b.astype(BF16), dims)
                acc = d if acc is None else acc + d
        if res is not None:
            acc = acc + refs[2 * npairs][...]
        o_ref[...] = acc.astype(o_ref.dtype)

    def spec(arr, role, block, index):
        if arr.ndim == 2:
            return pl.BlockSpec(block, lambda bb, i, j: index(i, j))
        assert role in ("batch", "reduce") and arr.shape[0] == count, (name, role, arr.shape)
        if role == "batch":
            return pl.BlockSpec((None,) + block, lambda bb, i, j: (bb,) + index(i, j))
        return pl.BlockSpec((count,) + block, lambda bb, i, j: (0,) + index(i, j))

    in_specs, args = [], []
    for a, b in pairs:
        if mode == "tn":
            in_specs.append(spec(a, roles[0], (a.shape[-2], tm), lambda i, j: (0, i)))
        else:
            in_specs.append(spec(a, roles[0], (tm, a.shape[-1]), lambda i, j: (i, 0)))
        if mode == "nt":
            in_specs.append(spec(b, roles[1], (tn, b.shape[-1]), lambda i, j: (j, 0)))
        else:
            in_specs.append(spec(b, roles[1], (b.shape[-2], tn), lambda i, j: (0, j)))
        args += [a, b]
    if res is not None:
        in_specs.append(pl.BlockSpec((tm, tn), lambda bb, i, j: (i, j)))
        args.append(res)
    if batched:
        out_spec = pl.BlockSpec((None, tm, tn), lambda bb, i, j: (bb, i, j))
        out_shape = jax.ShapeDtypeStruct((count, m, n), out_dtype)
    else:
        out_spec = pl.BlockSpec((tm, tn), lambda bb, i, j: (i, j))
        out_shape = jax.ShapeDtypeStruct((m, n), out_dtype)
    return pl.pallas_call(
        body, name=name, grid=(count if batched else 1, m // tm, n // tn),
        in_specs=in_specs, out_specs=out_spec, out_shape=out_shape,
        compiler_params=_params("parallel", "parallel", "parallel"),
    )(*args)


def _sigmoid(v):
    return 0.5 * jnp.tanh(0.5 * v) + 0.5


def ffn_up(h, wgu, name):
    s, d = h.shape
    _, n_sh, _, fs = wgu.shape
    tm = _rows_tile(s, 512)

    def body(h_ref, w_ref, gu_ref, act_ref):
        hv = h_ref[...]
        gate = _dot(hv, w_ref[0])
        up = _dot(hv, w_ref[1])
        gu_ref[0] = gate.astype(BF16)
        gu_ref[1] = up.astype(BF16)
        act_ref[...] = ((gate * _sigmoid(gate)) * up).astype(BF16)

    return pl.pallas_call(
        body, name=name, grid=(n_sh, s // tm),
        in_specs=[pl.BlockSpec((tm, d), lambda j, i: (i, 0)), pl.BlockSpec((2, None, d, fs), lambda j, i: (0, j, 0, 0))],
        out_specs=[pl.BlockSpec((2, None, tm, fs), lambda j, i: (0, j, i, 0)),
                   pl.BlockSpec((None, tm, fs), lambda j, i: (j, i, 0))],
        out_shape=[jax.ShapeDtypeStruct((2, n_sh, s, fs), BF16), jax.ShapeDtypeStruct((n_sh, s, fs), BF16)],
        compiler_params=_params("parallel", "parallel"),
    )(h, wgu)


def ffn_bwd_act(dxo, wdown, gu, name):
    s, d = dxo.shape
    n_sh, fs, _ = wdown.shape
    tm = _rows_tile(s, 512)

    def body(dx_ref, w_ref, gu_ref, dgu_ref):
        dact = _dot(dx_ref[...], w_ref[...], _NT)
        gate = gu_ref[0].astype(F32)
        sig = _sigmoid(gate)
        dgu_ref[0] = ((dact * gu_ref[1].astype(F32)) * (sig * (1.0 + gate * (1.0 - sig)))).astype(BF16)
        dgu_ref[1] = (dact * (gate * sig)).astype(BF16)

    tile = pl.BlockSpec((2, None, tm, fs), lambda j, i: (0, j, i, 0))
    return pl.pallas_call(
        body, name=name, grid=(n_sh, s // tm),
        in_specs=[pl.BlockSpec((tm, d), lambda j, i: (i, 0)), pl.BlockSpec((None, fs, d), lambda j, i: (j, 0, 0)), tile],
        out_specs=tile,
        out_shape=jax.ShapeDtypeStruct((2, n_sh, s, fs), BF16),
        compiler_params=_params("parallel", "parallel"),
    )(dxo, wdown, gu)


def _pool_select(col, by_window):
    out = by_window[3]
    for g in (2, 1, 0):
        out = jnp.where(col < (g + 1) * POOL_GROUP, by_window[g], out)
    return out


def _pool_counts(pos):
    return [jnp.minimum(pos + 1, w).astype(F32) for w in POOL_WINDOWS]


def pool_fwd(proj, wbd, scale, name):
    s = proj.shape[0]
    tm = _rows_tile(s, 512)
    nt = s // tm
    width = SB_WIDTH
    per = tm // POOL_HALO

    def body(u_ref, halo_ref, w_ref, sc_ref, pooled_ref, grouped_ref):
        i = pl.program_id(0)
        u = u_ref[...]
        halo = jnp.where(i > 0, halo_ref[...], 0.0)
        ext = jnp.concatenate([halo, u], axis=0)
        sums = []
        acc = ext
        for k in (1, 2, 4, 8):
            acc = acc + pltpu.roll(acc, k, 0)
            sums.append(acc[POOL_HALO:])
        pos = i * tm + lax.broadcasted_iota(jnp.int32, (tm, width), 0)
        col = lax.broadcasted_iota(jnp.int32, (tm, width), 1)
        means = [sm / cnt for sm, cnt in zip(sums, _pool_counts(pos))]
        pooled = (_pool_select(col, means) - u).astype(BF16)
        pooled_ref[...] = pooled
        grouped_ref[...] = (_dot(pooled, w_ref[...]) * sc_ref[...]).astype(BF16)

    row = pl.BlockSpec((tm, width), lambda i: (i, 0))
    return pl.pallas_call(
        body, name=name, grid=(nt,),
        in_specs=[row, pl.BlockSpec((POOL_HALO, width), lambda i: (jnp.maximum(i * per - 1, 0), 0)),
                  pl.BlockSpec((width, width), lambda i: (0, 0)), pl.BlockSpec((1, width), lambda i: (0, 0))],
        out_specs=[row, row],
        out_shape=[jax.ShapeDtypeStruct((s, width), BF16), jax.ShapeDtypeStruct((s, width + MEM_WIDTH), BF16)],
        compiler_params=_params("parallel"),
    )(proj, proj, wbd, scale.reshape(1, width))


def pool_bwd(dcat, pooled, wbd, scale, name):
    s = dcat.shape[0]
    tm = _rows_tile(s, 512)
    nt = s // tm
    width = SB_WIDTH
    per = tm // POOL_HALO
    n_ext = tm + POOL_HALO

    def body(dg_ref, halo_ref, pooled_ref, w_ref, sc_ref, du_ref, dw_ref, dsc_ref):
        i = pl.program_id(0)
        w = w_ref[...]
        sc = sc_ref[...]
        dg = dg_ref[...]
        pooled = pooled_ref[...]
        pg = _dot(pooled, w)
        dpg = (dg * sc).astype(BF16)
        part_sc = jnp.sum(dg * pg, axis=0, keepdims=True)
        part_w = _dot(pooled, dpg, _TN)

        @pl.when(i == 0)
        def _():
            dsc_ref[...] = part_sc
            dw_ref[...] = part_w

        @pl.when(i > 0)
        def _():
            dsc_ref[...] += part_sc
            dw_ref[...] += part_w

        dpooled = _dot(dpg, w, _NT)
        halo_dpg = (jnp.where(i < nt - 1, halo_ref[...], 0.0) * sc).astype(BF16)
        ext = jnp.concatenate([dpooled, _dot(halo_dpg, w, _NT)], axis=0)
        pos = i * tm + lax.broadcasted_iota(jnp.int32, (n_ext, width), 0)
        col = lax.broadcasted_iota(jnp.int32, (tm, width), 1)
        outs = []
        for k_idx, cnt in enumerate(_pool_counts(pos)):
            acc = ext / cnt
            for k in (1, 2, 4, 8)[:k_idx + 1]:
                acc = acc + pltpu.roll(acc, n_ext - k, 0)
            outs.append(acc[:tm])
        du_ref[...] = (_pool_select(col, outs) - dpooled).astype(BF16)

    row = pl.BlockSpec((tm, width), lambda i: (i, 0))
    return pl.pallas_call(
        body, name=name, grid=(nt,),
        in_specs=[row, pl.BlockSpec((POOL_HALO, width), lambda i: (jnp.minimum((i + 1) * per, s // POOL_HALO - 1), 0)),
                  row, pl.BlockSpec((width, width), lambda i: (0, 0)), pl.BlockSpec((1, width), lambda i: (0, 0))],
        out_specs=[row, pl.BlockSpec((width, width), lambda i: (0, 0)), pl.BlockSpec((1, width), lambda i: (0, 0))],
        out_shape=[jax.ShapeDtypeStruct((s, width + MEM_WIDTH), BF16), jax.ShapeDtypeStruct((width, width), F32),
                   jax.ShapeDtypeStruct((1, width), F32)],
        compiler_params=_params("arbitrary"),
    )(dcat, dcat, pooled, wbd, scale.reshape(1, width))


def _head_masks(shape):
    lane = lax.broadcasted_iota(jnp.int32, shape, 1)
    return lane < HEAD_DIM, lane >= HEAD_DIM


def _mem_probs(qh, kp):
    logits = _dot(qh, kp, _NT)
    logits = logits - jnp.max(logits, axis=-1, keepdims=True)
    p = jnp.exp(logits)
    return p / jnp.sum(p, axis=-1, keepdims=True)


def mem_attn_fwd(proj, mkv, cat, name):
    s = proj.shape[0]
    m = mkv.shape[0]
    tm = _rows_tile(s, 512)
    q_blk = SB_WIDTH // MEM_WIDTH

    def body(q_ref, kv_ref, _, o_ref):
        masks = _head_masks((tm, LANES))
        for pr in range(MEM_WIDTH // LANES):
            q = q_ref[:, pr * LANES:(pr + 1) * LANES] * QK_SCALE
            kp = kv_ref[:, pr * LANES:(pr + 1) * LANES]
            vp = kv_ref[:, MEM_WIDTH + pr * LANES:MEM_WIDTH + (pr + 1) * LANES]
            outs = []
            for hm in masks:
                p = _mem_probs(jnp.where(hm, q, 0.0).astype(BF16), kp)
                outs.append(_dot(p.astype(BF16), vp))
            o_ref[:, pr * LANES:(pr + 1) * LANES] = jnp.where(masks[0], outs[0], outs[1]).astype(BF16)

    return pl.pallas_call(
        body, name=name, grid=(s // tm,),
        in_specs=[pl.BlockSpec((tm, MEM_WIDTH), lambda i: (i, q_blk)), pl.BlockSpec((m, 2 * MEM_WIDTH), lambda i: (0, 0)),
                  _ANY],
        out_specs=pl.BlockSpec((tm, MEM_WIDTH), lambda i: (i, q_blk)),
        out_shape=jax.ShapeDtypeStruct(cat.shape, BF16),
        input_output_aliases={2: 0},
        compiler_params=_params("parallel"),
    )(proj, mkv, cat)


def mem_attn_bwd(proj, mkv, dcat, dproj, name):
    s = proj.shape[0]
    m = mkv.shape[0]
    tm = _rows_tile(s, 512)
    q_blk = SB_WIDTH // MEM_WIDTH

    def body(q_ref, kv_ref, do_ref, _, dq_ref, dkv_ref):
        i = pl.program_id(0)

        @pl.when(i == 0)
        def _():
            dkv_ref[...] = jnp.zeros_like(dkv_ref)

        masks = _head_masks((tm, LANES))
        for pr in range(MEM_WIDTH // LANES):
            q = q_ref[:, pr * LANES:(pr + 1) * LANES] * QK_SCALE
            do = do_ref[:, pr * LANES:(pr + 1) * LANES]
            kp = kv_ref[:, pr * LANES:(pr + 1) * LANES]
            vp = kv_ref[:, MEM_WIDTH + pr * LANES:MEM_WIDTH + (pr + 1) * LANES]
            dqs = []
            dk = jnp.zeros((m, LANES), F32)
            dv = jnp.zeros((m, LANES), F32)
            for hm in masks:
                qh = jnp.where(hm, q, 0.0).astype(BF16)
                doh = jnp.where(hm, do, 0.0).astype(BF16)
                p = _mem_probs(qh, kp)
                dp = _dot(doh, vp, _NT)
                ds = (p * (dp - jnp.sum(dp * p, axis=-1, keepdims=True))).astype(BF16)
                dqs.append(_dot(ds, kp))
                dk = dk + _dot(ds, qh, _TN)
                dv = dv + _dot(p.astype(BF16), doh, _TN)
            dq_ref[:, pr * LANES:(pr + 1) * LANES] = (jnp.where(masks[0], dqs[0], dqs[1]) * QK_SCALE).astype(BF16)
            dkv_ref[:, pr * LANES:(pr + 1) * LANES] += dk
            dkv_ref[:, MEM_WIDTH + pr * LANES:MEM_WIDTH + (pr + 1) * LANES] += dv

    return pl.pallas_call(
        body, name=name, grid=(s // tm,),
        in_specs=[pl.BlockSpec((tm, MEM_WIDTH), lambda i: (i, q_blk)), pl.BlockSpec((m, 2 * MEM_WIDTH), lambda i: (0, 0)),
                  pl.BlockSpec((tm, MEM_WIDTH), lambda i: (i, q_blk)), _ANY],
        out_specs=[pl.BlockSpec((tm, MEM_WIDTH), lambda i: (i, q_blk)), pl.BlockSpec((m, 2 * MEM_WIDTH), lambda i: (0, 0))],
        out_shape=[jax.ShapeDtypeStruct(dproj.shape, BF16), jax.ShapeDtypeStruct((m, 2 * MEM_WIDTH), F32)],
        input_output_aliases={3: 0},
        compiler_params=_params("arbitrary"),
    )(proj, mkv, dcat, dproj)


SB_ROWS = 512
SB_KEYS = 512
SB_SUB = LANES


def _sb_tri(later):
    r = lax.broadcasted_iota(jnp.int32, (2 * SB_SUB, 2 * SB_SUB), 0) % SB_SUB
    c = lax.broadcasted_iota(jnp.int32, (2 * SB_SUB, 2 * SB_SUB), 1)
    tri = (r > c) if later else (r < c)
    return (tri | (c >= SB_SUB)).astype(BF16)


def _sb_sums(v, tri, n_sub):
    out = []
    for m in range(n_sub):
        vm = v[:, m * SB_SUB:(m + 1) * SB_SUB]
        hi = vm.astype(BF16)
        lo = (vm - hi.astype(F32)).astype(BF16)
        r = _dot(jnp.concatenate([hi, lo], axis=1), tri)
        out.append((r[:, :SB_SUB], r[:, SB_SUB:]))
    return out


def _sb_tri_pair_before():
    r = lax.broadcasted_iota(jnp.int32, (2 * SB_SUB, 2 * SB_SUB), 0)
    c = lax.broadcasted_iota(jnp.int32, (2 * SB_SUB, 2 * SB_SUB), 1)
    return ((r // SB_SUB == c // SB_SUB) & (r % SB_SUB < c % SB_SUB)).astype(BF16)


def _sb_prefix_pairs(e, tri_pair, n_sub):
    out = []
    for m in range(0, n_sub, 2):
        pair = e[:, m * SB_SUB:(m + 2) * SB_SUB]
        r = _dot(pair.astype(BF16), tri_pair)
        for h in range(2):
            em = pair[:, h * SB_SUB:(h + 1) * SB_SUB]
            out.append((r[:, h * SB_SUB:(h + 1) * SB_SUB], jnp.sum(em, axis=1, keepdims=True)))
    return out


def _sb_logs(nqh, k_blk, valid):
    nz = _dot(nqh, k_blk, _NT)
    log_not = jnp.minimum(nz, 0.0) - jnp.log(1.0 + jnp.exp2(jnp.abs(nz) * -LOG2_E))
    log_beta = log_not - nz
    if valid is not None:
        log_not = jnp.where(valid, log_not, 0.0)
    return log_not, log_beta


def _sb_weights(log_not, log_beta, tri_later, carry_in, valid, n_sub):
    sums = _sb_sums(log_not, tri_later, n_sub)
    run = carry_in
    later = [None] * n_sub
    for m in reversed(range(n_sub)):
        later[m] = sums[m][0] + run
        run = run + sums[m][1]
    w = jnp.exp(log_beta + jnp.concatenate(later, axis=1))
    if valid is not None:
        w = jnp.where(valid, w, 0.0)
    return w, run


def _pad_rows(a, r0):
    return a if r0 == 0 else jnp.concatenate([jnp.zeros((r0,) + a.shape[1:], a.dtype), a], axis=0)


def _sb_diag_parts(nqh, k_blk, tri_later, n_sub):
    tq = nqh.shape[0]
    parts = []
    for m in range(n_sub):
        r0 = m * SB_SUB
        row = lax.broadcasted_iota(jnp.int32, (tq - r0, SB_SUB), 0)
        col = lax.broadcasted_iota(jnp.int32, (tq - r0, SB_SUB), 1)
        valid = (row >= SB_SUB) | (row > col)
        log_not, log_beta = _sb_logs(nqh[r0:], k_blk[r0:r0 + SB_SUB], valid)
        (inside, row_sum), = _sb_sums(log_not, tri_later, 1)
        parts.append([r0, valid, log_beta, inside, row_sum])
    run = jnp.zeros((tq, LANES), F32)
    for part in reversed(parts):
        r0, row_sum = part[0], part[4]
        part[4] = run[r0:]
        run = run + _pad_rows(row_sum, r0)
    return parts, run


def _sb_diag_fwd(nqh, k_blk, v_blk, tri_later, n_sub):
    parts, run = _sb_diag_parts(nqh, k_blk, tri_later, n_sub)
    acc = jnp.zeros((nqh.shape[0], LANES), F32)
    for r0, valid, log_beta, inside, later_blocks in parts:
        w = jnp.where(valid, jnp.exp(log_beta + inside + later_blocks), 0.0)
        acc = acc + _pad_rows(_dot(w.astype(BF16), v_blk[r0:r0 + SB_SUB]), r0)
    return run, acc


def _sb_valid(i, j, tq, tk):
    qpos = i * tq + lax.broadcasted_iota(jnp.int32, (tq, tk), 0)
    kpos = j * tk + lax.broadcasted_iota(jnp.int32, (tq, tk), 1)
    return qpos > kpos


def sb_fwd(projb, kv, name, gather=()):
    s = projb.shape[0]
    tq, tk = _rows_tile(s, SB_ROWS), _rows_tile(s, SB_KEYS)
    n_sub = tk // SB_SUB
    n_pairs = SB_WIDTH // LANES
    assert s // tk <= LANES and tk % tq == 0

    n_g = len(gather)
    n_i = s // tq

    def body(*refs):
        q_ref, k_ref, v_ref = refs[:3]
        o_ref, c_ref = refs[3 + n_g:5 + n_g]
        i = pl.program_id(1)
        if n_g:
            start, wait = _exchange_ops(refs[3:3 + n_g], refs[5 + n_g:5 + 2 * n_g], refs[5 + 2 * n_g:], 0)
            pl.when((pl.program_id(0) == 0) & (i == 0))(start)
        j_diag = (i * tq) // tk
        masks = _head_masks((tq, LANES))
        nq = q_ref[...] * (-QK_SCALE)
        tri = _sb_tri(later=True)
        lane = lax.broadcasted_iota(jnp.int32, (tq, LANES), 1)
        nqh = [jnp.where(hm, nq, 0.0).astype(BF16) for hm in masks]

        def block(j, carry, valid):
            off = pl.multiple_of(j * tk, tk)
            k_blk, v_blk = k_ref[pl.ds(off, tk), :], v_ref[pl.ds(off, tk), :]
            out = []
            for hh in range(2):
                c, acc, cmat = carry[3 * hh:3 * hh + 3]
                log_not, log_beta = _sb_logs(nqh[hh], k_blk, valid)
                w, c_next = _sb_weights(log_not, log_beta, tri, c, valid, n_sub)
                out += [c_next, acc + _dot(w.astype(BF16), v_blk), jnp.where(lane == j, c, cmat)]
            return tuple(out)

        zero = jnp.zeros((tq, LANES), F32)
        if tq == tk:
            off = pl.multiple_of(i * tk, tk)
            k_blk, v_blk = k_ref[pl.ds(off, tk), :], v_ref[pl.ds(off, tk), :]
            carry = ()
            for hh in range(2):
                carry += _sb_diag_fwd(nqh[hh], k_blk, v_blk, tri, n_sub) + (zero,)
        else:
            carry = block(j_diag, (zero,) * 6, _sb_valid(i, j_diag, tq, tk))
        carry = lax.fori_loop(
            0, j_diag // 2,
            lambda t, cr: block(j_diag - 2 - 2 * t, block(j_diag - 1 - 2 * t, cr, None), None), carry)
        carry = lax.fori_loop(0, j_diag % 2, lambda jj, cr: block(jj, cr, None), carry)
        c_ref[0] = carry[2]
        c_ref[1] = carry[5]
        o_ref[...] = jnp.where(masks[0], carry[1], carry[4]).astype(BF16)
        if n_g:
            pl.when((pl.program_id(0) == n_pairs - 1) & (i == n_i - 1))(wait)

    outs = pl.pallas_call(
        body, name=name, grid=(n_pairs, n_i),
        in_specs=[pl.BlockSpec((tq, LANES), lambda p, i: (i, p)),
                  pl.BlockSpec((s, LANES), lambda p, i: (0, p)),
                  pl.BlockSpec((s, LANES), lambda p, i: (0, n_pairs + p))] + [_ANY] * n_g,
        out_specs=[pl.BlockSpec((tq, LANES), lambda p, i: (i, p)),
                   pl.BlockSpec((2, tq, LANES), lambda p, i: (p, i, 0))] + [_ANY] * n_g,
        out_shape=[jax.ShapeDtypeStruct((s, SB_WIDTH + MEM_WIDTH), BF16),
                   jax.ShapeDtypeStruct((N_SB_HEADS, s, LANES), F32)] + _exchange_out_shape([], gather),
        scratch_shapes=_exchange_sems(n_g) if n_g else [],
        compiler_params=_params("arbitrary", "arbitrary"),
    )(projb, kv, kv, *gather)
    return outs[0], outs[1], outs[2:]


def sb_bwd(projb, kv, dcat, csave, name, scatter=()):
    s = projb.shape[0]
    tq, tk = _rows_tile(s, SB_ROWS), _rows_tile(s, SB_KEYS)
    n_sub = tk // SB_SUB
    n_pairs = SB_WIDTH // LANES
    n_x = len(scatter)
    n_i = s // tq

    def body(*refs):
        q_ref, k_ref, v_ref, do_ref, c_ref = refs[:5]
        dq_ref, dkb_ref, dvb_ref = refs[5 + n_x:8 + n_x]
        dk_ref, dv_ref = refs[8 + 2 * n_x:10 + 2 * n_x]
        i = pl.program_id(1)
        if n_x:
            start, wait = _exchange_ops(refs[5:5 + n_x], refs[8 + n_x:8 + 2 * n_x], refs[10 + 2 * n_x:], n_x)
            pl.when((pl.program_id(0) == 0) & (i == 0))(start)
        j_diag = (i * tq) // tk

        @pl.when(i == 0)
        def _():
            dk_ref[...] = jnp.zeros_like(dk_ref)
            dv_ref[...] = jnp.zeros_like(dv_ref)

        masks = _head_masks((tq, LANES))
        qs = q_ref[...] * QK_SCALE
        do = do_ref[...]
        tri_later = _sb_tri(later=True)
        tri_before = _sb_tri_pair_before()
        lane = lax.broadcasted_iota(jnp.int32, (tq, LANES), 1)
        qh = [jnp.where(hm, qs, 0.0).astype(BF16) for hm in masks]
        nqh = [jnp.where(hm, -qs, 0.0).astype(BF16) for hm in masks]
        doh = [jnp.where(hm, do, 0.0).astype(BF16) for hm in masks]

        def head_part(hh, rows, k_blk, v_blk, valid, c, e_before, dq):
            n_part = k_blk.shape[0] // SB_SUB
            log_not, log_beta = _sb_logs(nqh[hh][rows], k_blk, valid)
            w, _ = _sb_weights(log_not, log_beta, tri_later, c, valid, n_part)
            dv_h = _dot(w.astype(BF16), doh[hh][rows], _TN)
            e = w * _dot(doh[hh][rows], v_blk, _NT)
            sums = _sb_prefix_pairs(e, tri_before, n_part)
            run = e_before
            e_pre = []
            for m in range(n_part):
                e_pre.append(sums[m][0] + run)
                run = run + sums[m][1]
            dz = e - jnp.exp(log_beta) * (e + jnp.concatenate(e_pre, axis=1))
            if valid is not None:
                dz = jnp.where(valid, dz, 0.0)
            dz = dz.astype(BF16)
            return run, dq + _dot(dz, k_blk), _dot(dz, qh[hh][rows], _TN), dv_h

        def block(j, carry, valid):
            off = pl.multiple_of(j * tk, tk)
            k_blk, v_blk = k_ref[pl.ds(off, tk), :], v_ref[pl.ds(off, tk), :]
            out, dk, dv = [], None, None
            for hh in range(2):
                c = jnp.sum(jnp.where(lane == j, c_ref[hh], 0.0), axis=1, keepdims=True)
                run, dq, dk_h, dv_h = head_part(hh, slice(None), k_blk, v_blk, valid, jnp.broadcast_to(c, (tq, LANES)),
                                                carry[2 * hh], carry[2 * hh + 1])
                dk, dv = (dk_h, dv_h) if hh == 0 else (dk + dk_h, dv + dv_h)
                out += [run, dq]
            dk_ref[pl.ds(off, tk), :] += dk
            dv_ref[pl.ds(off, tk), :] += dv
            return tuple(out)

        def diagonal(carry):
            half = tq // 2
            off = pl.multiple_of(i * tk, tk)
            k_blk, v_blk = k_ref[pl.ds(off, tk), :], v_ref[pl.ds(off, tk), :]
            row_a = lax.broadcasted_iota(jnp.int32, (half, half), 0)
            col_a = lax.broadcasted_iota(jnp.int32, (half, half), 1)
            row_b = lax.broadcasted_iota(jnp.int32, (half, tk), 0) + half
            col_b = lax.broadcasted_iota(jnp.int32, (half, tk), 1)
            zero_c = jnp.zeros((half, LANES), F32)
            dqs, dk, dv = [], None, None
            for hh in range(2):
                e_before, dq = carry[2 * hh], carry[2 * hh + 1]
                _, dq_a, dk_a, dv_a = head_part(hh, slice(0, half), k_blk[:half], v_blk[:half], row_a > col_a,
                                                zero_c, e_before[:half], dq[:half])
                _, dq_b, dk_b, dv_b = head_part(hh, slice(half, tq), k_blk, v_blk, row_b > col_b,
                                                zero_c, e_before[half:], dq[half:])
                zeros_k = jnp.zeros((tk - half, LANES), F32)
                dk_h = dk_b + jnp.concatenate([dk_a, zeros_k], axis=0)
                dv_h = dv_b + jnp.concatenate([dv_a, zeros_k], axis=0)
                dk, dv = (dk_h, dv_h) if hh == 0 else (dk + dk_h, dv + dv_h)
                dqs.append(jnp.concatenate([dq_a, dq_b], axis=0))
            dk_ref[pl.ds(off, tk), :] += dk
            dv_ref[pl.ds(off, tk), :] += dv
            return dqs

        zero = jnp.zeros((tq, LANES), F32)
        carry = lax.fori_loop(0, j_diag // 2, lambda t, cr: block(2 * t + 1, block(2 * t, cr, None), None), (zero,) * 4)
        carry = lax.fori_loop(2 * (j_diag // 2), j_diag, lambda j, cr: block(j, cr, None), carry)
        if tq == tk:
            dq0, dq1 = diagonal(carry)
        else:
            carry = block(j_diag, carry, _sb_valid(i, j_diag, tq, tk))
            dq0, dq1 = carry[1], carry[3]
        dq_ref[...] = (jnp.where(masks[0], dq0, dq1) * QK_SCALE).astype(BF16)

        @pl.when(i == n_i - 1)
        def _():
            dkb_ref[...] = dk_ref[...].astype(BF16)
            dvb_ref[...] = dv_ref[...].astype(BF16)

        if n_x:
            pl.when((pl.program_id(0) == n_pairs - 1) & (i == n_i - 1))(wait)

    tile = pl.BlockSpec((tq, LANES), lambda p, i: (i, p))
    full = pl.BlockSpec((s, LANES), lambda p, i: (0, p))
    outs = pl.pallas_call(
        body, name=name, grid=(n_pairs, n_i),
        in_specs=[tile, full, pl.BlockSpec((s, LANES), lambda p, i: (0, n_pairs + p)), tile,
                  pl.BlockSpec((2, tq, LANES), lambda p, i: (p, i, 0))] + [_ANY] * n_x,
        out_specs=[tile, full, full] + [_ANY] * n_x,
        out_shape=[jax.ShapeDtypeStruct((s, SB_WIDTH + MEM_WIDTH), BF16), jax.ShapeDtypeStruct((s, SB_WIDTH), BF16),
                   jax.ShapeDtypeStruct((s, SB_WIDTH), BF16)] + _exchange_out_shape(scatter, []),
        scratch_shapes=[pltpu.VMEM((s, LANES), F32), pltpu.VMEM((s, LANES), F32)]
        + (_exchange_sems(n_x) if n_x else []),
        compiler_params=_params("arbitrary", "arbitrary"),
    )(projb, kv, kv, dcat, csave, *scatter)
    return outs[0], outs[1], outs[2], outs[3:]


def reduce_adamw(parts, w, m, v, name, after=None):
    n, r, c = parts.shape
    tr = next((t for t in (128, 176) if r % t == 0), r)
    bias1 = 1.0 - ADAM_B1 ** ADAM_STEP
    bias2 = 1.0 - ADAM_B2 ** ADAM_STEP

    def body(p_ref, w_ref, m_ref, v_ref, *rest):
        g_ref, d_ref, nm_ref, nv_ref = rest[-4:]
        g = p_ref[0].astype(F32)
        for k in range(1, n):
            g = g + p_ref[k].astype(F32)
        new_m = ADAM_B1 * m_ref[...] + (1.0 - ADAM_B1) * g
        new_v = ADAM_B2 * v_ref[...] + (1.0 - ADAM_B2) * (g * g)
        m_hat = new_m / bias1
        v_hat = new_v / bias2
        g_ref[...] = g
        d_ref[...] = -ADAM_LR * (m_hat / (jnp.sqrt(v_hat) + ADAM_EPS) + ADAM_WD * w_ref[...])
        nm_ref[...] = new_m
        nv_ref[...] = new_v

    row = pl.BlockSpec((tr, c), lambda i: (i, 0))
    return pl.pallas_call(
        body, name=name, grid=(r // tr,),
        in_specs=[pl.BlockSpec((n, tr, c), lambda i: (0, i, 0)), row, row, row] + ([] if after is None else [_ANY]),
        out_specs=[row, row, row, row],
        out_shape=[jax.ShapeDtypeStruct((r, c), F32)] * 4,
        compiler_params=_params("parallel"),
    )(parts, w, m, v, *(() if after is None else (after,)))


EARLY_0 = ("a_w_in", "a_w_mem_kv")
EARLY_1 = ("a_w_out",)
EARLY_2 = ("a_w_gu", "a_w_down")
EARLY_3 = ("w_kv", "b_w_q", "b_w_mem_kv")
EARLY = EARLY_0 + EARLY_1 + EARLY_2 + EARLY_3
LATE = ("b_w_out", "b_w_gu", "b_w_down")
BIG = EARLY + LATE
GATHER_3 = ("w_kv", "b_w_q")
GATHER_LATE = LATE + ("b_w_mem_kv",)
COL_SHARDED = ("a_w_gu", "w_kv", "b_w_gu")
SMALL_SHARDED = ("a_norm_mix", "a_scale", "a_norm_ffn")
SMALL_REPL = ("mem_norm", "kv_norm", "b_norm_mix", "b_norm_ffn", "final_norm")
WEIGHTS = ("mem_norm", "a_norm_mix", "a_w_in", "a_w_group", "a_scale", "a_w_mem_kv", "a_w_out", "a_norm_ffn", "a_w_gu",
           "a_w_down", "kv_norm", "w_kv", "b_norm_mix", "b_w_q", "b_w_mem_kv", "b_w_out", "b_norm_ffn", "b_w_gu",
           "b_w_down", "final_norm")


def _pad_row(v):
    v = v.reshape(1, -1)
    return jnp.pad(v, ((0, 0), (0, PACK_COLS - v.shape[1])))


def kernel(x, mem, mem_norm, a_norm_mix, a_w_in, a_w_group, a_scale, a_w_mem_kv, a_w_out, a_norm_ffn, a_w_gu, a_w_down, kv_norm, w_kv, b_norm_mix, b_w_q, b_w_mem_kv, b_w_out, b_norm_ffn, b_w_gu, b_w_down, final_norm, loss_target, m_mem_norm, m_a_norm_mix, m_a_w_in, m_a_w_group, m_a_scale, m_a_w_mem_kv, m_a_w_out, m_a_norm_ffn, m_a_w_gu, m_a_w_down, m_kv_norm, m_w_kv, m_b_norm_mix, m_b_w_q, m_b_w_mem_kv, m_b_w_out, m_b_norm_ffn, m_b_w_gu, m_b_w_down, m_final_norm, v_mem_norm, v_a_norm_mix, v_a_w_in, v_a_w_group, v_a_scale, v_a_w_mem_kv, v_a_w_out, v_a_norm_ffn, v_a_w_gu, v_a_w_down, v_kv_norm, v_w_kv, v_b_norm_mix, v_b_w_q, v_b_w_mem_kv, v_b_w_out, v_b_norm_ffn, v_b_w_gu, v_b_w_down, v_final_norm):
    w = dict(mem_norm=mem_norm, a_norm_mix=a_norm_mix, a_w_in=a_w_in, a_w_group=a_w_group, a_scale=a_scale,
             a_w_mem_kv=a_w_mem_kv, a_w_out=a_w_out, a_norm_ffn=a_norm_ffn, a_w_gu=a_w_gu, a_w_down=a_w_down,
             kv_norm=kv_norm, w_kv=w_kv, b_norm_mix=b_norm_mix, b_w_q=b_w_q, b_w_mem_kv=b_w_mem_kv, b_w_out=b_w_out,
             b_norm_ffn=b_norm_ffn, b_w_gu=b_w_gu, b_w_down=b_w_down, final_norm=final_norm)
    mom = dict(mem_norm=m_mem_norm, a_norm_mix=m_a_norm_mix, a_w_in=m_a_w_in, a_w_group=m_a_w_group, a_scale=m_a_scale,
               a_w_mem_kv=m_a_w_mem_kv, a_w_out=m_a_w_out, a_norm_ffn=m_a_norm_ffn, a_w_gu=m_a_w_gu,
               a_w_down=m_a_w_down, kv_norm=m_kv_norm, w_kv=m_w_kv, b_norm_mix=m_b_norm_mix, b_w_q=m_b_w_q,
               b_w_mem_kv=m_b_w_mem_kv, b_w_out=m_b_w_out, b_norm_ffn=m_b_norm_ffn, b_w_gu=m_b_w_gu,
               b_w_down=m_b_w_down, final_norm=m_final_norm)
    var = dict(mem_norm=v_mem_norm, a_norm_mix=v_a_norm_mix, a_w_in=v_a_w_in, a_w_group=v_a_w_group, a_scale=v_a_scale,
               a_w_mem_kv=v_a_w_mem_kv, a_w_out=v_a_w_out, a_norm_ffn=v_a_norm_ffn, a_w_gu=v_a_w_gu,
               a_w_down=v_a_w_down, kv_norm=v_kv_norm, w_kv=v_w_kv, b_norm_mix=v_b_norm_mix, b_w_q=v_b_w_q,
               b_w_mem_kv=v_b_w_mem_kv, b_w_out=v_b_w_out, b_norm_ffn=v_b_norm_ffn, b_w_gu=v_b_w_gu,
               b_w_down=v_b_w_down, final_norm=v_final_norm)

    me = 4 * lax.axis_index("x") + 2 * lax.axis_index("y") + lax.axis_index("c")
    shard2d = {n: w[n].shape[-2:] for n in BIG}
    shard = {n: w[n].reshape(shard2d[n]).astype(BF16) for n in BIG}
    n_sh = N_DEV // 2
    s_len = x.shape[1]

    small_send = jnp.concatenate([_pad_row(w[n]) for n in SMALL_SHARDED]
                                 + [jnp.zeros((8 - len(SMALL_SHARDED), PACK_COLS), F32)], axis=0)
    def own_slot(land, own):
        return lax.dynamic_update_slice(land, own[None], (me, 0, 0))

    def gather_done(handle, names, after, name):
        got = exchange_wait(handle, after, name)
        gathered.update({n: own_slot(g, shard[n]) for n, g in zip(names, got)})
        return got

    gathered = {}
    gather_0 = exchange_start([], [shard[n] for n in EARLY_0] + [small_send], x, "gather_0_start")
    gather_1 = exchange_start([], [shard[n] for n in EARLY_1], gather_0["token"], "gather_1_start")
    gather_2 = exchange_start([], [shard["a_w_gu"]], gather_1["token"], "gather_2_start")
    gather_2b = exchange_start([], [shard["a_w_down"]], gather_2["token"], "gather_2b_start")
    gather_3 = exchange_start([], [shard[n] for n in GATHER_3], gather_2b["token"], "gather_3_start")
    small_all = own_slot(gather_done(gather_0, EARLY_0, gather_3["token"], "gather_0_wait")[-1], small_send)
    a_norm_mix_f = small_all[:, 0, :a_norm_mix.shape[1]].reshape(-1)
    a_scale_f = small_all[:, 1, :a_scale.shape[1]].reshape(-1)
    a_norm_ffn_f = small_all[:, 2, :a_norm_ffn.shape[1]].reshape(-1)

    def rows_full(n):
        return gathered[n].reshape(-1, shard2d[n][1])

    def gu_views(tag):
        gu8 = gathered[tag + "_w_gu"]
        return gu8.reshape((2, n_sh) + gu8.shape[1:]), gu8

    def down_view(tag):
        down = gathered[tag + "_w_down"]
        return down.reshape(n_sh, -1, down.shape[2])

    wbd = jnp.zeros((SB_WIDTH, SB_WIDTH), BF16)
    for g in range(4):
        sl = slice(g * POOL_GROUP, (g + 1) * POOL_GROUP)
        wbd = wbd.at[sl, sl].set(a_w_group[0, g].astype(BF16))

    xs, mems, tgt = x[0], mem[0], loss_target[0]

    memn = rmsnorm_fwd(mems, mem_norm, "memn")
    (h0, proj), = norm_matmul(xs, [(a_norm_mix_f, rows_full("a_w_in"), F32)], "proj_a")
    mkv_a = matmul([(memn, rows_full("a_w_mem_kv"))], "nn", BF16, "mkv_a")
    pooled, cat_a = pool_fwd(proj, wbd, a_scale_f, "pool_fwd")
    cat_a = mem_attn_fwd(proj, mkv_a, cat_a, "mem_fwd_a")
    gather_done(gather_1, EARLY_1, cat_a, "gather_1_wait")
    x1 = matmul([(cat_a, rows_full("a_w_out"))], "nn", F32, "out_a", res=xs)
    h1 = rmsnorm_fwd(x1, a_norm_ffn_f, "h1")
    gather_done(gather_2, ("a_w_gu",), h1, "gather_2_wait")
    wgu_a, wgu8_a = gu_views("a")
    gu_a, act_a = ffn_up(h1, wgu_a, "ffn_up_a")
    gather_done(gather_2b, ("a_w_down",), act_a, "gather_2b_wait")
    wdown_a = down_view("a")
    x2 = matmul([(act_a, wdown_a)], "nn", F32, "down_a", res=x1, roles=("reduce", "reduce"), count=n_sh)

    gather_done(gather_3, GATHER_3, x2, "gather_3_wait")
    wkv = jnp.transpose(gathered["w_kv"], (1, 0, 2)).reshape(shard2d["w_kv"][0], -1)
    (hk, kv), (h3, projb) = norm_matmul(
        x2, [(kv_norm, wkv, BF16), (b_norm_mix[0], rows_full("b_w_q"), F32)], "kv_proj_b")
    cat_b, csave, got = sb_fwd(projb, kv, "sb_fwd", gather=[shard[n] for n in GATHER_LATE])
    gathered.update(zip(GATHER_LATE, got))
    mkv_b = matmul([(memn, rows_full("b_w_mem_kv"))], "nn", BF16, "mkv_b")
    cat_b = mem_attn_fwd(projb, mkv_b, cat_b, "mem_fwd_b")
    x3 = matmul([(cat_b, rows_full("b_w_out"))], "nn", F32, "out_b", res=x2)
    h4 = rmsnorm_fwd(x3, b_norm_ffn[0], "h4")
    (wgu_b, wgu8_b), wdown_b = gu_views("b"), down_view("b")
    gu_b, act_b = ffn_up(h4, wgu_b, "ffn_up_b")
    x4 = matmul([(act_b, wdown_b)], "nn", F32, "down_b", res=x3, roles=("reduce", "reduce"), count=n_sh)

    grads = {}
    dx4, dx4_bf, grads["final_norm"], loss_part = loss_head(x4, tgt, final_norm, "loss_head")

    def blocks(n, g):
        return g.reshape((N_DEV,) + tuple(shard2d[n]))

    def ffn_weight_grads(tag, dx_out_bf, h, gu, act, wgu8, wdown):
        fs = wgu8.shape[2]
        dgu8 = ffn_bwd_act(dx_out_bf, wdown, gu, "ffn_bwd_act_" + tag).reshape(N_DEV, s_len, fs)
        d_wdown = matmul([(act, dx_out_bf)], "tn", BF16, "dw_down_" + tag, tm=fs, roles=("batch", None), count=n_sh)
        grads[tag + "_w_gu"] = matmul([(dgu8, h)], "tn", BF16, "dw_gu_" + tag, tm=fs, roles=("batch", None), count=N_DEV)
        grads[tag + "_w_down"] = blocks(tag + "_w_down", d_wdown)
        return dgu8

    def ffn_input_grads(tag, x_in, dx_out, dgu8, wgu8, g_ffn):
        dx_in, dx_in_bf, dgs = matmul_norm_bwd(x_in, dx_out, [(g_ffn, [(dgu8, wgu8)])], "norm_bwd_ffn_" + tag)
        return dx_in, dx_in_bf, dgs[0]

    def grads_start(names, small, after, name):
        return exchange_start([grads[n] for n in names], small, after, name)

    def grads_done(handle, names, after, name):
        got = exchange_wait(handle, after, name)
        for n, g in zip(names, got):
            recv[n] = lax.dynamic_update_slice(g, lax.dynamic_slice(grads[n], (me, 0, 0), (1,) + g.shape[1:]), (me, 0, 0))
        return got

    recv = {}
    dgu8_b = ffn_weight_grads("b", dx4_bf, h4, gu_b, act_b, wgu8_b, wdown_b)
    dx3, dx3_bf, grads["b_norm_ffn"] = ffn_input_grads("b", x3, dx4, dgu8_b, wgu8_b, b_norm_ffn[0])

    dcat_b = matmul([(dx3_bf, rows_full("b_w_out"))], "nt", F32, "dcat_b")
    grads["b_w_out"] = blocks("b_w_out", matmul([(cat_b, dx3_bf)], "tn", BF16, "dw_out_b"))
    dprojb, dk, dv, recv_late = sb_bwd(projb, kv, dcat_b, csave, "sb_bwd", scatter=[grads[n] for n in LATE])
    dprojb, dmkv_b = mem_attn_bwd(projb, mkv_b, dcat_b, dprojb, "mem_bwd_b")
    grads["b_w_q"] = blocks("b_w_q", matmul([(h3, dprojb)], "tn", BF16, "dw_q_b"))
    d_wkv_t = jnp.concatenate([matmul([(dk, hk)], "tn", BF16, "dw_k", tm=SB_WIDTH),
                               matmul([(dv, hk)], "tn", BF16, "dw_v", tm=SB_WIDTH)], axis=0)
    grads["w_kv"] = d_wkv_t.reshape(N_DEV, -1, d_wkv_t.shape[1])
    dx2, dx2_bf, dgs = matmul_norm_bwd(
        x2, dx3, [(b_norm_mix[0], [(dprojb, rows_full("b_w_q"))]),
                  (kv_norm, [(dk, wkv[:, :SB_WIDTH]), (dv, wkv[:, SB_WIDTH:])])], "norm_bwd_x2")
    grads["b_norm_mix"], grads["kv_norm"] = dgs
    dmkv_b_bf = dmkv_b.astype(BF16)
    grads["b_w_mem_kv"] = blocks("b_w_mem_kv", matmul([(memn, dmkv_b_bf)], "tn", BF16, "dw_mkv_b"))

    grads_3 = grads_start(EARLY_3, [], grads["b_w_mem_kv"], "grads_3_start")
    dgu8_a = ffn_weight_grads("a", dx2_bf, h1, gu_a, act_a, wgu8_a, wdown_a)
    grads_2 = grads_start(EARLY_2, [], grads_3["token"], "grads_2_start")
    dx1, dx1_bf, grads["a_norm_ffn"] = ffn_input_grads(
        "a", x1, dx2, dgu8_a, wgu8_a, a_norm_ffn_f + grads_2["token"][0, 0])

    dcat_a = matmul([(dx1_bf, rows_full("a_w_out"))], "nt", F32, "dcat_a")
    grads["a_w_out"] = blocks("a_w_out", matmul([(cat_a, dx1_bf)], "tn", BF16, "dw_out_a"))
    dproj, d_wbd, grads["a_scale"] = pool_bwd(dcat_a, pooled, wbd, a_scale_f, "pool_bwd")
    grads["a_w_group"] = jnp.stack(
        [d_wbd[g * POOL_GROUP:(g + 1) * POOL_GROUP, g * POOL_GROUP:(g + 1) * POOL_GROUP] for g in range(4)])
    group_grads = grads["a_w_group"].reshape(-1, PACK_COLS).astype(BF16)
    grads_1 = exchange_start([grads[n] for n in EARLY_1], [group_grads], grads_2["token"], "grads_1_start")
    dproj, dmkv_a = mem_attn_bwd(proj, mkv_a, dcat_a, dproj, "mem_bwd_a")
    grads["a_w_in"] = blocks("a_w_in", matmul([(h0, dproj)], "tn", BF16, "dw_in_a"))
    dmkv_a_bf = dmkv_a.astype(BF16)
    grads["a_w_mem_kv"] = blocks("a_w_mem_kv", matmul([(memn, dmkv_a_bf)], "tn", BF16, "dw_mkv_a"))
    grads_0 = grads_start(EARLY_0, [], grads_1["token"], "grads_0_start")
    dx, _, dgs = matmul_norm_bwd(
        xs, dx1, [(a_norm_mix_f + grads_0["token"][0, 0], [(dproj, rows_full("a_w_in"))])], "norm_bwd_x")
    grads["a_norm_mix"] = dgs[0]
    dmemn = matmul([(dmkv_a_bf, rows_full("a_w_mem_kv")), (dmkv_b_bf, rows_full("b_w_mem_kv"))], "nt", F32, "dmemn")
    _, _, dgs = rmsnorm_bwd(mems, jnp.zeros_like(mems), [(mem_norm, dmemn)], "norm_bwd_mem")
    grads["mem_norm"] = dgs[0]

    small_names = SMALL_REPL + SMALL_SHARDED
    head_rows = 16

    def small_pack(rows_of):
        head = jnp.zeros((head_rows, PACK_COLS), F32)
        for r, (v, col) in enumerate(rows_of):
            v = v.reshape(1, -1)
            if col is None:
                head = head + jnp.pad(v, ((r, head_rows - 1 - r), (0, PACK_COLS - v.shape[1])))
            else:
                head = lax.dynamic_update_slice(head, v, (r, col))
        return head

    small_grads = small_pack([(grads[n], None) for n in small_names] + [(loss_part[:, :1], None)])
    grads_s = exchange_start([], [small_grads], dx, "grads_s_start")

    def small_state(src):
        return small_pack([(src[n], None) for n in SMALL_REPL]
                          + [(src[n], me * src[n].shape[-1]) for n in SMALL_SHARDED])

    def state2d(src, n):
        a = src[n].reshape(shard2d[n])
        return a.T if n in COL_SHARDED else a

    def adamw(names, after):
        for n in names:
            big_out[n] = reduce_adamw(recv[n], state2d(w, n), state2d(mom, n), state2d(var, n), "adamw_" + n, after=after)
        return big_out[names[-1]][0]

    big_out = {}
    recv.update(zip(LATE, recv_late))
    done = adamw(LATE, grads_s["token"])
    grads_done(grads_3, EARLY_3, done, "grads_3_wait")
    done = adamw(EARLY_3, None)
    grads_done(grads_2, EARLY_2, done, "grads_2_wait")
    done = adamw(EARLY_2, None)
    group_recv = own_slot(grads_done(grads_1, EARLY_1, done, "grads_1_wait")[-1], group_grads)
    done = adamw(EARLY_1, None)
    group_out = reduce_adamw(group_recv, *[src["a_w_group"].reshape(-1, PACK_COLS) for src in (w, mom, var)], "adamw_group")
    grads_done(grads_0, EARLY_0, group_out[0], "grads_0_wait")
    done = adamw(EARLY_0, None)
    small_recv = own_slot(exchange_wait(grads_s, done, "grads_s_wait")[0], small_grads)
    small_out = reduce_adamw(small_recv, small_state(w), small_state(mom), small_state(var), "adamw_small")

    def unpack(kind):
        out = {n: (big_out[n][kind].T if n in COL_SHARDED else big_out[n][kind]).reshape(w[n].shape) for n in BIG}
        so = small_out[kind]
        for r, n in enumerate(SMALL_REPL):
            out[n] = so[r, :w[n].shape[-1]].reshape(w[n].shape)
        for r, n in enumerate(SMALL_SHARDED):
            width = w[n].shape[-1]
            out[n] = lax.dynamic_slice(so, (len(SMALL_REPL) + r, me * width), (1, width)).reshape(w[n].shape)
        out["a_w_group"] = group_out[kind].reshape(a_w_group.shape)
        return out

    loss = small_out[0][len(small_names), 0]
    results = [loss, dx.reshape(x.shape)]
    for kind in range(4):
        out = unpack(kind)
        results += [out[n] for n in WEIGHTS]
    return tuple(results)
```

```python
import functools

import jax
import jax.numpy as jnp
from jax import lax
from jax.experimental import pallas as pl
from jax.experimental.pallas import tpu as pltpu

F32 = jnp.float32
BF16 = jnp.bfloat16

N_DEV = 8
HEAD_DIM = 64
N_SB_HEADS = 12
SB_WIDTH = N_SB_HEADS * HEAD_DIM
MEM_WIDTH = 4 * HEAD_DIM
POOL_WINDOWS = (2, 4, 8, 16)
POOL_GROUP = SB_WIDTH // 4
POOL_HALO = 16
EPS = 1e-6
QK_SCALE = HEAD_DIM ** -0.5
LOG2_E = 1.4426950408889634
LANES = 128
PACK_COLS = 1024

ADAM_LR = 0.001
ADAM_B1 = 0.9
ADAM_B2 = 0.999
ADAM_EPS = 1e-08
ADAM_WD = 0.01
ADAM_STEP = 10

VMEM_LIMIT = 48 * 1024 * 1024

_NT = (((1,), (1,)), ((), ()))
_TN = (((0,), (0,)), ((), ()))
_NN = (((1,), (0,)), ((), ()))


def _params(*sem):
    return pltpu.CompilerParams(dimension_semantics=sem, vmem_limit_bytes=VMEM_LIMIT)


def _dot(a, b, dims=_NN):
    return lax.dot_general(a, b, dims, preferred_element_type=F32)


def exchange(a2a, ag, name):
    n_a2a, n_arr = len(a2a), len(a2a) + len(ag)

    def body(*refs):
        start, wait = _exchange_ops(refs[:n_arr], refs[n_arr:2 * n_arr], refs[2 * n_arr:], n_a2a)
        start()
        wait()

    return pl.pallas_call(
        body, name=name, out_shape=_exchange_out_shape(a2a, ag),
        in_specs=[_ANY] * n_arr, out_specs=[_ANY] * n_arr,
        scratch_shapes=_exchange_sems(n_arr),
    )(*a2a, *ag)


_ANY = pl.BlockSpec(memory_space=pl.ANY)
_HBM = pl.BlockSpec(memory_space=pltpu.HBM)
_SEM = pl.BlockSpec(memory_space=pltpu.SEMAPHORE)
_DATAFLOW = pltpu.SideEffectType.DATAFLOW_SIDE_EFFECTING


def exchange_start(a2a, ag, after, name):
    n_a2a, n_arr = len(a2a), len(a2a) + len(ag)
    arrays = [pltpu.with_memory_space_constraint(v, pltpu.HBM) for v in list(a2a) + list(ag)]
    shapes = _exchange_out_shape(a2a, ag)
    lands = [pltpu.with_memory_space_constraint(lax.empty(sh.shape, sh.dtype), pltpu.HBM) for sh in shapes]

    def body(*refs):
        ins, outs = refs[:n_arr], refs[n_arr:2 * n_arr]
        send_sems, recv_sems, token = refs[4 * n_arr + 1:4 * n_arr + 4]
        start, _ = _exchange_ops(ins, outs, (send_sems, recv_sems, None), n_a2a, local_copies=False)
        start()
        token[...] = jnp.zeros_like(token)

    outs = pl.pallas_call(
        body, name=name,
        out_shape=[pltpu.HBM(v.shape, v.dtype) for v in arrays] + [pltpu.HBM(sh.shape, sh.dtype) for sh in shapes]
        + [pltpu.SemaphoreType.DMA((n_arr * (N_DEV - 1),)), pltpu.SemaphoreType.DMA((n_arr * (N_DEV - 1),)),
           jax.ShapeDtypeStruct((8, LANES), F32)],
        in_specs=[_HBM] * (2 * n_arr) + [_ANY],
        out_specs=[_HBM] * (2 * n_arr) + [_SEM, _SEM, pl.BlockSpec(memory_space=pltpu.VMEM)],
        input_output_aliases={i: i for i in range(2 * n_arr)},
        compiler_params=pltpu.CompilerParams(has_side_effects=_DATAFLOW),
    )(*arrays, *lands, after)
    return dict(ins=outs[:n_arr], lands=outs[n_arr:2 * n_arr], sems=outs[2 * n_arr:2 * n_arr + 2], n_a2a=n_a2a,
                token=outs[2 * n_arr + 2])


def exchange_wait(handle, after, name):
    ins, lands, sems, n_a2a = handle["ins"], handle["lands"], handle["sems"], handle["n_a2a"]
    n_arr = len(ins)

    def body(*refs):
        in_refs, land_refs = refs[:n_arr], refs[n_arr:2 * n_arr]
        send_sems, recv_sems = refs[2 * n_arr:2 * n_arr + 2]
        _, wait = _exchange_ops(in_refs, land_refs, (send_sems, recv_sems, None), n_a2a, local_copies=False)
        wait()

    outs = pl.pallas_call(
        body, name=name,
        out_shape=[pltpu.HBM(v.shape, v.dtype) for v in list(ins) + list(lands)],
        in_specs=[_HBM] * (2 * n_arr) + [_SEM, _SEM, _ANY], out_specs=[_HBM] * (2 * n_arr),
        input_output_aliases={i: i for i in range(2 * n_arr)},
        compiler_params=pltpu.CompilerParams(has_side_effects=_DATAFLOW),
    )(*ins, *lands, *sems, after)
    return outs[n_arr:]


def _exchange_out_shape(a2a, ag):
    out_shape = [jax.ShapeDtypeStruct(v.shape, v.dtype) for v in a2a]
    return out_shape + [jax.ShapeDtypeStruct((N_DEV,) + v.shape, v.dtype) for v in ag]


def _exchange_sems(n_arr):
    return [pltpu.SemaphoreType.DMA((n_arr * (N_DEV - 1),)), pltpu.SemaphoreType.DMA((n_arr * (N_DEV - 1),)),
            pltpu.SemaphoreType.DMA((n_arr,))]


def _exchange_ops(ins, outs, sems, n_a2a, local_copies=True):
    send_sems, recv_sems, local_sems = sems
    n_arr = len(ins)
    x, y, c = lax.axis_index("x"), lax.axis_index("y"), lax.axis_index("c")
    me = 4 * x + 2 * y + c

    def peer_of(k):
        px = 1 - x if k & 4 else x
        py = 1 - y if k & 2 else y
        pc = 1 - c if k & 1 else c
        return (px, py, pc), 4 * px + 2 * py + pc

    def copy(a, k, landing):
        peer, pid = peer_of(k)
        src = ins[a].at[pid] if a < n_a2a else ins[a]
        return pltpu.make_async_remote_copy(
            src_ref=src, dst_ref=outs[a].at[pid if landing else me], send_sem=send_sems.at[a * (N_DEV - 1) + k - 1],
            recv_sem=recv_sems.at[a * (N_DEV - 1) + k - 1], device_id=peer, device_id_type=pl.DeviceIdType.MESH)

    def local(a):
        src = ins[a].at[me] if a < n_a2a else ins[a]
        return pltpu.make_async_copy(src, outs[a].at[me], local_sems.at[a])

    order = (6, 7, 4, 5, 2, 3, 1)

    def start():
        if local_copies:
            for a in range(n_arr):
                local(a).start()
        for a in range(n_arr):
            for k in order:
                copy(a, k, landing=False).start()

    def wait():
        for a in range(n_arr):
            for k in order:
                copy(a, k, landing=True).wait_recv()
        for a in range(n_arr):
            for k in order:
                copy(a, k, landing=False).wait_send()
        if local_copies:
            for a in range(n_arr):
                local(a).wait()

    return start, wait


def _rows_tile(n, want):
    t = min(want, n)
    assert n % t == 0, (n, t)
    return t


def rmsnorm_fwd(x, g, name):
    s, d = x.shape
    tm = _rows_tile(s, 512)

    def body(x_ref, g_ref, h_ref):
        xv = x_ref[...]
        r = lax.rsqrt(jnp.mean(xv * xv, axis=-1, keepdims=True) + EPS)
        h_ref[...] = ((xv * r) * g_ref[...]).astype(h_ref.dtype)

    return pl.pallas_call(
        body, name=name, grid=(s // tm,),
        in_specs=[pl.BlockSpec((tm, d), lambda i: (i, 0)), pl.BlockSpec((1, d), lambda i: (0, 0))],
        out_specs=pl.BlockSpec((tm, d), lambda i: (i, 0)),
        out_shape=jax.ShapeDtypeStruct((s, d), BF16),
        compiler_params=_params("parallel"),
    )(x, g.reshape(1, d))


def norm_matmul(x, heads, name):
    s, d = x.shape
    tm = _rows_tile(s, 512)
    n = len(heads)

    def body(*refs):
        x_ref = refs[0]
        xv = x_ref[...]
        xhat = xv * lax.rsqrt(jnp.mean(xv * xv, axis=-1, keepdims=True) + EPS)
        for k in range(n):
            g_ref, w_ref = refs[1 + 2 * k], refs[2 + 2 * k]
            h_ref, y_ref = refs[1 + 2 * n + 2 * k], refs[2 + 2 * n + 2 * k]
            h = (xhat * g_ref[...]).astype(BF16)
            h_ref[...] = h
            y_ref[...] = _dot(h, w_ref[...]).astype(y_ref.dtype)

    row = pl.BlockSpec((tm, d), lambda i: (i, 0))
    in_specs, out_specs, out_shape, args = [row], [], [], [x]
    for g, w_k, dtype in heads:
        in_specs += [pl.BlockSpec((1, d), lambda i: (0, 0)), pl.BlockSpec(w_k.shape, lambda i: (0, 0))]
        args += [g.reshape(1, d), w_k]
        out_specs += [row, pl.BlockSpec((tm, w_k.shape[1]), lambda i: (i, 0))]
        out_shape += [jax.ShapeDtypeStruct((s, d), BF16), jax.ShapeDtypeStruct((s, w_k.shape[1]), dtype)]
    outs = pl.pallas_call(
        body, name=name, grid=(s // tm,), in_specs=in_specs, out_specs=out_specs, out_shape=out_shape,
        compiler_params=_params("parallel"),
    )(*args)
    return [(outs[2 * k], outs[2 * k + 1]) for k in range(n)]


def rmsnorm_bwd(x, dres, pairs, name):
    s, d = x.shape
    tm = _rows_tile(s, 256)
    n = len(pairs)

    def body(*refs):
        x_ref, dres_ref = refs[0], refs[1]
        g_refs, dh_refs = refs[2:2 + n], refs[2 + n:2 + 2 * n]
        dx_ref, dxb_ref = refs[2 + 2 * n], refs[3 + 2 * n]
        dg_refs = refs[4 + 2 * n:]
        i = pl.program_id(0)
        xv = x_ref[...]
        r = lax.rsqrt(jnp.mean(xv * xv, axis=-1, keepdims=True) + EPS)
        xhat = xv * r
        acc = dres_ref[...]
        for k in range(n):
            dh = dh_refs[k][...].astype(F32)
            part = jnp.sum(dh * xhat, axis=0, keepdims=True)

            @pl.when(i == 0)
            def _(k=k, part=part):
                dg_refs[k][...] = part

            @pl.when(i > 0)
            def _(k=k, part=part):
                dg_refs[k][...] += part

            dxh = dh * g_refs[k][...]
            acc = acc + r * (dxh - xhat * jnp.mean(dxh * xhat, axis=-1, keepdims=True))
        dx_ref[...] = acc
        dxb_ref[...] = acc.astype(BF16)

    row = pl.BlockSpec((tm, d), lambda i: (i, 0))
    vec = pl.BlockSpec((1, d), lambda i: (0, 0))
    outs = pl.pallas_call(
        body, name=name, grid=(s // tm,),
        in_specs=[row, row] + [vec] * n + [row] * n,
        out_specs=[row, row] + [vec] * n,
        out_shape=[jax.ShapeDtypeStruct((s, d), F32), jax.ShapeDtypeStruct((s, d), BF16)]
        + [jax.ShapeDtypeStruct((1, d), F32)] * n,
        compiler_params=_params("arbitrary"),
    )(x, dres, *[g.reshape(1, d) for g, _ in pairs], *[dh for _, dh in pairs])
    return outs[0], outs[1], outs[2:]


def matmul_norm_bwd(x, dres, heads, name):
    s, d = x.shape
    tm = _rows_tile(s, 512)
    n = len(heads)
    n_ops = [len(pairs) for _, pairs in heads]

    def body(*refs):
        x_ref, dres_ref = refs[0], refs[1]
        n_in = 2 + n + 2 * sum(n_ops)
        dx_ref, dxb_ref = refs[n_in], refs[n_in + 1]
        dg_refs = refs[n_in + 2:]
        i = pl.program_id(0)
        xv = x_ref[...]
        r = lax.rsqrt(jnp.mean(xv * xv, axis=-1, keepdims=True) + EPS)
        xhat = xv * r
        acc = dres_ref[...]
        pos = 2 + n
        for k in range(n):
            dh = None
            for a, _ in heads[k][1]:
                a_ref, w_ref = refs[pos], refs[pos + 1]
                for kk in range(a.shape[0] if a.ndim == 3 else 1):
                    prod = _dot(a_ref[kk], w_ref[kk], _NT) if a.ndim == 3 else _dot(a_ref[...], w_ref[...], _NT)
                    dh = prod if dh is None else dh + prod
                pos += 2
            part = jnp.sum(dh * xhat, axis=0, keepdims=True)

            @pl.when(i == 0)
            def _(k=k, part=part):
                dg_refs[k][...] = part

            @pl.when(i > 0)
            def _(k=k, part=part):
                dg_refs[k][...] += part

            dxh = dh * refs[2 + k][...]
            acc = acc + r * (dxh - xhat * jnp.mean(dxh * xhat, axis=-1, keepdims=True))
        dx_ref[...] = acc
        dxb_ref[...] = acc.astype(BF16)

    row = pl.BlockSpec((tm, d), lambda i: (i, 0))
    vec = pl.BlockSpec((1, d), lambda i: (0, 0))
    in_specs, args = [row, row] + [vec] * n, [x, dres] + [g.reshape(1, d) for g, _ in heads]
    for _, pairs in heads:
        for a, w_k in pairs:
            if a.ndim == 3:
                in_specs += [pl.BlockSpec((a.shape[0], tm, a.shape[2]), lambda i: (0, i, 0)),
                             pl.BlockSpec(w_k.shape, lambda i: (0, 0, 0), pipeline_mode=pl.Buffered(1))]
            else:
                in_specs += [pl.BlockSpec((tm, a.shape[1]), lambda i: (i, 0)),
                             pl.BlockSpec(w_k.shape, lambda i: (0, 0), pipeline_mode=pl.Buffered(1))]
            args += [a, w_k]
    outs = pl.pallas_call(
        body, name=name, grid=(s // tm,), in_specs=in_specs, out_specs=[row, row] + [vec] * n,
        out_shape=[jax.ShapeDtypeStruct((s, d), F32), jax.ShapeDtypeStruct((s, d), BF16)]
        + [jax.ShapeDtypeStruct((1, d), F32)] * n,
        compiler_params=_params("arbitrary"),
    )(*args)
    return outs[0], outs[1], outs[2:]


def loss_head(x, tgt, g, name):
    s, d = x.shape
    tm = _rows_tile(s, 512)

    def body(x_ref, t_ref, g_ref, dx_ref, dxb_ref, dg_ref, loss_ref):
        i = pl.program_id(0)
        xv = x_ref[...]
        gv = g_ref[...]
        r = lax.rsqrt(jnp.mean(xv * xv, axis=-1, keepdims=True) + EPS)
        xhat = xv * r
        diff = xhat * gv - t_ref[...]
        part_loss = 0.5 * jnp.sum(jnp.mean(diff * diff, axis=-1, keepdims=True), axis=0, keepdims=True)
        dy = diff * (1.0 / d)
        part_dg = jnp.sum(dy * xhat, axis=0, keepdims=True)

        @pl.when(i == 0)
        def _():
            dg_ref[...] = part_dg
            loss_ref[...] = jnp.broadcast_to(part_loss, loss_ref.shape)

        @pl.when(i > 0)
        def _():
            dg_ref[...] += part_dg
            loss_ref[...] += jnp.broadcast_to(part_loss, loss_ref.shape)

        dxh = dy * gv
        dx = r * (dxh - xhat * jnp.mean(dxh * xhat, axis=-1, keepdims=True))
        dx_ref[...] = dx
        dxb_ref[...] = dx.astype(BF16)

    row = pl.BlockSpec((tm, d), lambda i: (i, 0))
    vec = pl.BlockSpec((1, d), lambda i: (0, 0))
    return pl.pallas_call(
        body, name=name, grid=(s // tm,),
        in_specs=[row, row, vec], out_specs=[row, row, vec, pl.BlockSpec((1, LANES), lambda i: (0, 0))],
        out_shape=[jax.ShapeDtypeStruct((s, d), F32), jax.ShapeDtypeStruct((s, d), BF16),
                   jax.ShapeDtypeStruct((1, d), F32), jax.ShapeDtypeStruct((1, LANES), F32)],
        compiler_params=_params("arbitrary"),
    )(x, tgt, g.reshape(1, d))


def matmul(pairs, mode, out_dtype, name, res=None, tm=512, tn=None, roles=(None, None), count=1):
    a0, b0 = pairs[0]
    m = a0.shape[-1] if mode == "tn" else a0.shape[-2]
    n = b0.shape[-2] if mode == "nt" else b0.shape[-1]
    if tn is None:
        tn = n if (mode != "tn" and n <= 1024) else 512
    tm, tn = _rows_tile(m, tm), _rows_tile(n, tn)
    dims = {"nn": _NN, "nt": _NT, "tn": _TN}[mode]
    npairs = len(pairs)
    batched, reducing = "batch" in roles, "reduce" in roles
    assert not (batched and reducing) and (res is None or not batched)
    n_in = 2 * npairs + (res is not None)

    def body(*refs):
        o_ref = refs[n_in]
        acc = None
        for p in range(npairs):
            a_ref, b_ref = refs[2 * p], refs[2 * p + 1]
            for kk in range(count if reducing else 1):
                a = a_ref[kk] if roles[0] == "reduce" else a_ref[...]
                b = b_ref[kk] if roles[1] == "reduce" else b_ref[...]
                d = _dot(a.astype(BF16), b.astype(BF16), dims)
                acc = d if acc is None else acc + d
        if res is not None:
            acc = acc + refs[2 * npairs][...]
        o_ref[...] = acc.astype(o_ref.dtype)

    def spec(arr, role, block, index):
        if arr.ndim == 2:
            return pl.BlockSpec(block, lambda bb, i, j: index(i, j))
        assert role in ("batch", "reduce") and arr.shape[0] == count, (name, role, arr.shape)
        if role == "batch":
            return pl.BlockSpec((None,) + block, lambda bb, i, j: (bb,) + index(i, j))
        return pl.BlockSpec((count,) + block, lambda bb, i, j: (0,) + index(i, j))

    in_specs, args = [], []
    for a, b in pairs:
        if mode == "tn":
            in_specs.append(spec(a, roles[0], (a.shape[-2], tm), lambda i, j: (0, i)))
        else:
            in_specs.append(spec(a, roles[0], (tm, a.shape[-1]), lambda i, j: (i, 0)))
        if mode == "nt":
            in_specs.append(spec(b, roles[1], (tn, b.shape[-1]), lambda i, j: (j, 0)))
        else:
            in_specs.append(spec(b, roles[1], (b.shape[-2], tn), lambda i, j: (0, j)))
        args += [a, b]
    if res is not None:
        in_specs.append(pl.BlockSpec((tm, tn), lambda bb, i, j: (i, j)))
        args.append(res)
    if batched:
        out_spec = pl.BlockSpec((None, tm, tn), lambda bb, i, j: (bb, i, j))
        out_shape = jax.ShapeDtypeStruct((count, m, n), out_dtype)
    else:
        out_spec = pl.BlockSpec((tm, tn), lambda bb, i, j: (i, j))
        out_shape = jax.ShapeDtypeStruct((m, n), out_dtype)
    return pl.pallas_call(
        body, name=name, grid=(count if batched else 1, m // tm, n // tn),
        in_specs=in_specs, out_specs=out_spec, out_shape=out_shape,
        compiler_params=_params("parallel", "parallel", "parallel"),
    )(*args)


def _sigmoid(v):
    return 0.5 * jnp.tanh(0.5 * v) + 0.5


def ffn_up(h, wgu, name):
    s, d = h.shape
    _, n_sh, _, fs = wgu.shape
    tm = _rows_tile(s, 1024)

    def body(h_ref, w_ref, gu_ref, act_ref):
        hv = h_ref[...]
        gate = _dot(hv, w_ref[0])
        up = _dot(hv, w_ref[1])
        gu_ref[0] = gate.astype(BF16)
        gu_ref[1] = up.astype(BF16)
        act_ref[...] = ((gate * _sigmoid(gate)) * up).astype(BF16)

    return pl.pallas_call(
        body, name=name, grid=(n_sh, s // tm),
        in_specs=[pl.BlockSpec((tm, d), lambda j, i: (i, 0)), pl.BlockSpec((2, None, d, fs), lambda j, i: (0, j, 0, 0))],
        out_specs=[pl.BlockSpec((2, None, tm, fs), lambda j, i: (0, j, i, 0)),
                   pl.BlockSpec((None, tm, fs), lambda j, i: (j, i, 0))],
        out_shape=[jax.ShapeDtypeStruct((2, n_sh, s, fs), BF16), jax.ShapeDtypeStruct((n_sh, s, fs), BF16)],
        compiler_params=_params("parallel", "parallel"),
    )(h, wgu)


def ffn_bwd_act(dxo, wdown, gu, name):
    s, d = dxo.shape
    n_sh, fs, _ = wdown.shape
    tm = _rows_tile(s, 1024)

    def body(dx_ref, w_ref, gu_ref, dgu_ref):
        dact = _dot(dx_ref[...], w_ref[...], _NT)
        gate = gu_ref[0].astype(F32)
        sig = _sigmoid(gate)
        dgu_ref[0] = ((dact * gu_ref[1].astype(F32)) * (sig * (1.0 + gate * (1.0 - sig)))).astype(BF16)
        dgu_ref[1] = (dact * (gate * sig)).astype(BF16)

    tile = pl.BlockSpec((2, None, tm, fs), lambda j, i: (0, j, i, 0))
    return pl.pallas_call(
        body, name=name, grid=(n_sh, s // tm),
        in_specs=[pl.BlockSpec((tm, d), lambda j, i: (i, 0)), pl.BlockSpec((None, fs, d), lambda j, i: (j, 0, 0)), tile],
        out_specs=tile,
        out_shape=jax.ShapeDtypeStruct((2, n_sh, s, fs), BF16),
        compiler_params=_params("parallel", "parallel"),
    )(dxo, wdown, gu)


def _pool_select(col, by_window):
    out = by_window[3]
    for g in (2, 1, 0):
        out = jnp.where(col < (g + 1) * POOL_GROUP, by_window[g], out)
    return out


def _pool_counts(pos):
    return [jnp.minimum(pos + 1, w).astype(F32) for w in POOL_WINDOWS]


def pool_fwd(proj, wbd, scale, name):
    s = proj.shape[0]
    tm = _rows_tile(s, 512)
    nt = s // tm
    width = SB_WIDTH
    per = tm // POOL_HALO

    def body(u_ref, halo_ref, w_ref, sc_ref, pooled_ref, grouped_ref):
        i = pl.program_id(0)
        u = u_ref[...]
        halo = jnp.where(i > 0, halo_ref[...], 0.0)
        ext = jnp.concatenate([halo, u], axis=0)
        sums = []
        acc = ext
        for k in (1, 2, 4, 8):
            acc = acc + pltpu.roll(acc, k, 0)
            sums.append(acc[POOL_HALO:])
        pos = i * tm + lax.broadcasted_iota(jnp.int32, (tm, width), 0)
        col = lax.broadcasted_iota(jnp.int32, (tm, width), 1)
        means = [sm / cnt for sm, cnt in zip(sums, _pool_counts(pos))]
        pooled = (_pool_select(col, means) - u).astype(BF16)
        pooled_ref[...] = pooled
        grouped_ref[...] = (_dot(pooled, w_ref[...]) * sc_ref[...]).astype(BF16)

    row = pl.BlockSpec((tm, width), lambda i: (i, 0))
    return pl.pallas_call(
        body, name=name, grid=(nt,),
        in_specs=[row, pl.BlockSpec((POOL_HALO, width), lambda i: (jnp.maximum(i * per - 1, 0), 0)),
                  pl.BlockSpec((width, width), lambda i: (0, 0)), pl.BlockSpec((1, width), lambda i: (0, 0))],
        out_specs=[row, row],
        out_shape=[jax.ShapeDtypeStruct((s, width), BF16), jax.ShapeDtypeStruct((s, width + MEM_WIDTH), BF16)],
        compiler_params=_params("parallel"),
    )(proj, proj, wbd, scale.reshape(1, width))


def pool_bwd(dcat, pooled, wbd, scale, name):
    s = dcat.shape[0]
    tm = _rows_tile(s, 512)
    nt = s // tm
    width = SB_WIDTH
    per = tm // POOL_HALO
    n_ext = tm + POOL_HALO

    def body(dg_ref, halo_ref, pooled_ref, w_ref, sc_ref, du_ref, dw_ref, dsc_ref):
        i = pl.program_id(0)
        w = w_ref[...]
        sc = sc_ref[...]
        dg = dg_ref[...]
        pooled = pooled_ref[...]
        pg = _dot(pooled, w)
        dpg = (dg * sc).astype(BF16)
        part_sc = jnp.sum(dg * pg, axis=0, keepdims=True)
        part_w = _dot(pooled, dpg, _TN)

        @pl.when(i == 0)
        def _():
            dsc_ref[...] = part_sc
            dw_ref[...] = part_w

        @pl.when(i > 0)
        def _():
            dsc_ref[...] += part_sc
            dw_ref[...] += part_w

        dpooled = _dot(dpg, w, _NT)
        halo_dpg = (jnp.where(i < nt - 1, halo_ref[...], 0.0) * sc).astype(BF16)
        ext = jnp.concatenate([dpooled, _dot(halo_dpg, w, _NT)], axis=0)
        pos = i * tm + lax.broadcasted_iota(jnp.int32, (n_ext, width), 0)
        col = lax.broadcasted_iota(jnp.int32, (tm, width), 1)
        outs = []
        for k_idx, cnt in enumerate(_pool_counts(pos)):
            acc = ext / cnt
            for k in (1, 2, 4, 8)[:k_idx + 1]:
                acc = acc + pltpu.roll(acc, n_ext - k, 0)
            outs.append(acc[:tm])
        du_ref[...] = (_pool_select(col, outs) - dpooled).astype(BF16)

    row = pl.BlockSpec((tm, width), lambda i: (i, 0))
    return pl.pallas_call(
        body, name=name, grid=(nt,),
        in_specs=[row, pl.BlockSpec((POOL_HALO, width), lambda i: (jnp.minimum((i + 1) * per, s // POOL_HALO - 1), 0)),
                  row, pl.BlockSpec((width, width), lambda i: (0, 0)), pl.BlockSpec((1, width), lambda i: (0, 0))],
        out_specs=[row, pl.BlockSpec((width, width), lambda i: (0, 0)), pl.BlockSpec((1, width), lambda i: (0, 0))],
        out_shape=[jax.ShapeDtypeStruct((s, width + MEM_WIDTH), BF16), jax.ShapeDtypeStruct((width, width), F32),
                   jax.ShapeDtypeStruct((1, width), F32)],
        compiler_params=_params("arbitrary"),
    )(dcat, dcat, pooled, wbd, scale.reshape(1, width))


def _head_masks(shape):
    lane = lax.broadcasted_iota(jnp.int32, shape, 1)
    return lane < HEAD_DIM, lane >= HEAD_DIM


def _mem_probs(qh, kp):
    logits = _dot(qh, kp, _NT)
    logits = logits - jnp.max(logits, axis=-1, keepdims=True)
    p = jnp.exp(logits)
    return p / jnp.sum(p, axis=-1, keepdims=True)


def mem_attn_fwd(proj, mkv, cat, name):
    s = proj.shape[0]
    m = mkv.shape[0]
    tm = _rows_tile(s, 512)
    q_blk = SB_WIDTH // MEM_WIDTH

    def body(q_ref, kv_ref, _, o_ref):
        masks = _head_masks((tm, LANES))
        for pr in range(MEM_WIDTH // LANES):
            q = q_ref[:, pr * LANES:(pr + 1) * LANES] * QK_SCALE
            kp = kv_ref[:, pr * LANES:(pr + 1) * LANES]
            vp = kv_ref[:, MEM_WIDTH + pr * LANES:MEM_WIDTH + (pr + 1) * LANES]
            outs = []
            for hm in masks:
                p = _mem_probs(jnp.where(hm, q, 0.0).astype(BF16), kp)
                outs.append(_dot(p.astype(BF16), vp))
            o_ref[:, pr * LANES:(pr + 1) * LANES] = jnp.where(masks[0], outs[0], outs[1]).astype(BF16)

    return pl.pallas_call(
        body, name=name, grid=(s // tm,),
        in_specs=[pl.BlockSpec((tm, MEM_WIDTH), lambda i: (i, q_blk)), pl.BlockSpec((m, 2 * MEM_WIDTH), lambda i: (0, 0)),
                  _ANY],
        out_specs=pl.BlockSpec((tm, MEM_WIDTH), lambda i: (i, q_blk)),
        out_shape=jax.ShapeDtypeStruct(cat.shape, BF16),
        input_output_aliases={2: 0},
        compiler_params=_params("parallel"),
    )(proj, mkv, cat)


def mem_attn_bwd(proj, mkv, dcat, dproj, name):
    s = proj.shape[0]
    m = mkv.shape[0]
    tm = _rows_tile(s, 512)
    q_blk = SB_WIDTH // MEM_WIDTH

    def body(q_ref, kv_ref, do_ref, _, dq_ref, dkv_ref):
        i = pl.program_id(0)

        @pl.when(i == 0)
        def _():
            dkv_ref[...] = jnp.zeros_like(dkv_ref)

        masks = _head_masks((tm, LANES))
        for pr in range(MEM_WIDTH // LANES):
            q = q_ref[:, pr * LANES:(pr + 1) * LANES] * QK_SCALE
            do = do_ref[:, pr * LANES:(pr + 1) * LANES]
            kp = kv_ref[:, pr * LANES:(pr + 1) * LANES]
            vp = kv_ref[:, MEM_WIDTH + pr * LANES:MEM_WIDTH + (pr + 1) * LANES]
            dqs = []
            dk = jnp.zeros((m, LANES), F32)
            dv = jnp.zeros((m, LANES), F32)
            for hm in masks:
                qh = jnp.where(hm, q, 0.0).astype(BF16)
                doh = jnp.where(hm, do, 0.0).astype(BF16)
                p = _mem_probs(qh, kp)
                dp = _dot(doh, vp, _NT)
                ds = (p * (dp - jnp.sum(dp * p, axis=-1, keepdims=True))).astype(BF16)
                dqs.append(_dot(ds, kp))
                dk = dk + _dot(ds, qh, _TN)
                dv = dv + _dot(p.astype(BF16), doh, _TN)
            dq_ref[:, pr * LANES:(pr + 1) * LANES] = (jnp.where(masks[0], dqs[0], dqs[1]) * QK_SCALE).astype(BF16)
            dkv_ref[:, pr * LANES:(pr + 1) * LANES] += dk
            dkv_ref[:, MEM_WIDTH + pr * LANES:MEM_WIDTH + (pr + 1) * LANES] += dv

    return pl.pallas_call(
        body, name=name, grid=(s // tm,),
        in_specs=[pl.BlockSpec((tm, MEM_WIDTH), lambda i: (i, q_blk)), pl.BlockSpec((m, 2 * MEM_WIDTH), lambda i: (0, 0)),
                  pl.BlockSpec((tm, MEM_WIDTH), lambda i: (i, q_blk)), _ANY],
        out_specs=[pl.BlockSpec((tm, MEM_WIDTH), lambda i: (i, q_blk)), pl.BlockSpec((m, 2 * MEM_WIDTH), lambda i: (0, 0))],
        out_shape=[jax.ShapeDtypeStruct(dproj.shape, BF16), jax.ShapeDtypeStruct((m, 2 * MEM_WIDTH), F32)],
        input_output_aliases={3: 0},
        compiler_params=_params("arbitrary"),
    )(proj, mkv, dcat, dproj)


SB_ROWS = 512
SB_KEYS = 512
SB_SUB = LANES


def _sb_tri(later):
    r = lax.broadcasted_iota(jnp.int32, (2 * SB_SUB, 2 * SB_SUB), 0) % SB_SUB
    c = lax.broadcasted_iota(jnp.int32, (2 * SB_SUB, 2 * SB_SUB), 1)
    tri = (r > c) if later else (r < c)
    return (tri | (c >= SB_SUB)).astype(BF16)


def _sb_sums(v, tri, n_sub):
    out = []
    for m in range(n_sub):
        vm = v[:, m * SB_SUB:(m + 1) * SB_SUB]
        hi = vm.astype(BF16)
        lo = (vm - hi.astype(F32)).astype(BF16)
        r = _dot(jnp.concatenate([hi, lo], axis=1), tri)
        out.append((r[:, :SB_SUB], r[:, SB_SUB:]))
    return out


def _sb_tri_pair_before():
    r = lax.broadcasted_iota(jnp.int32, (2 * SB_SUB, 2 * SB_SUB), 0)
    c = lax.broadcasted_iota(jnp.int32, (2 * SB_SUB, 2 * SB_SUB), 1)
    return ((r // SB_SUB == c // SB_SUB) & (r % SB_SUB < c % SB_SUB)).astype(BF16)


def _sb_prefix_pairs(e, tri_pair, n_sub):
    out = []
    for m in range(0, n_sub, 2):
        pair = e[:, m * SB_SUB:(m + 2) * SB_SUB]
        r = _dot(pair.astype(BF16), tri_pair)
        for h in range(2):
            em = pair[:, h * SB_SUB:(h + 1) * SB_SUB]
            out.append((r[:, h * SB_SUB:(h + 1) * SB_SUB], jnp.sum(em, axis=1, keepdims=True)))
    return out


def _sb_logs(nqh, k_blk, valid):
    nz = _dot(nqh, k_blk, _NT)
    log_not = jnp.minimum(nz, 0.0) - jnp.log(1.0 + jnp.exp2(jnp.abs(nz) * -LOG2_E))
    log_beta = log_not - nz
    if valid is not None:
        log_not = jnp.where(valid, log_not, 0.0)
    return log_not, log_beta


def _sb_weights(log_not, log_beta, tri_later, carry_in, valid, n_sub):
    sums = _sb_sums(log_not, tri_later, n_sub)
    run = carry_in
    later = [None] * n_sub
    for m in reversed(range(n_sub)):
        later[m] = sums[m][0] + run
        run = run + sums[m][1]
    w = jnp.exp(log_beta + jnp.concatenate(later, axis=1))
    if valid is not None:
        w = jnp.where(valid, w, 0.0)
    return w, run


def _pad_rows(a, r0):
    return a if r0 == 0 else jnp.concatenate([jnp.zeros((r0,) + a.shape[1:], a.dtype), a], axis=0)


def _sb_diag_parts(nqh, k_blk, tri_later, n_sub):
    tq = nqh.shape[0]
    parts = []
    for m in range(n_sub):
        r0 = m * SB_SUB
        row = lax.broadcasted_iota(jnp.int32, (tq - r0, SB_SUB), 0)
        col = lax.broadcasted_iota(jnp.int32, (tq - r0, SB_SUB), 1)
        valid = (row >= SB_SUB) | (row > col)
        log_not, log_beta = _sb_logs(nqh[r0:], k_blk[r0:r0 + SB_SUB], valid)
        (inside, row_sum), = _sb_sums(log_not, tri_later, 1)
        parts.append([r0, valid, log_beta, inside, row_sum])
    run = jnp.zeros((tq, LANES), F32)
    for part in reversed(parts):
        r0, row_sum = part[0], part[4]
        part[4] = run[r0:]
        run = run + _pad_rows(row_sum, r0)
    return parts, run


def _sb_diag_fwd(nqh, k_blk, v_blk, tri_later, n_sub):
    parts, run = _sb_diag_parts(nqh, k_blk, tri_later, n_sub)
    acc = jnp.zeros((nqh.shape[0], LANES), F32)
    for r0, valid, log_beta, inside, later_blocks in parts:
        w = jnp.where(valid, jnp.exp(log_beta + inside + later_blocks), 0.0)
        acc = acc + _pad_rows(_dot(w.astype(BF16), v_blk[r0:r0 + SB_SUB]), r0)
    return run, acc


def _sb_valid(i, j, tq, tk):
    qpos = i * tq + lax.broadcasted_iota(jnp.int32, (tq, tk), 0)
    kpos = j * tk + lax.broadcasted_iota(jnp.int32, (tq, tk), 1)
    return qpos > kpos


def sb_fwd(projb, kv, name, gather=()):
    s = projb.shape[0]
    tq, tk = _rows_tile(s, SB_ROWS), _rows_tile(s, SB_KEYS)
    n_sub = tk // SB_SUB
    n_pairs = SB_WIDTH // LANES
    assert s // tk <= LANES and tk % tq == 0

    n_g = len(gather)
    n_i = s // tq

    def body(*refs):
        q_ref, k_ref, v_ref = refs[:3]
        o_ref, c_ref = refs[3 + n_g:5 + n_g]
        i = pl.program_id(1)
        if n_g:
            start, wait = _exchange_ops(refs[3:3 + n_g], refs[5 + n_g:5 + 2 * n_g], refs[5 + 2 * n_g:], 0)
            pl.when((pl.program_id(0) == 0) & (i == 0))(start)
        j_diag = (i * tq) // tk
        masks = _head_masks((tq, LANES))
        nq = q_ref[...] * (-QK_SCALE)
        tri = _sb_tri(later=True)
        lane = lax.broadcasted_iota(jnp.int32, (tq, LANES), 1)
        nqh = [jnp.where(hm, nq, 0.0).astype(BF16) for hm in masks]

        def block(j, carry, valid):
            off = pl.multiple_of(j * tk, tk)
            k_blk, v_blk = k_ref[pl.ds(off, tk), :], v_ref[pl.ds(off, tk), :]
            out = []
            for hh in range(2):
                c, acc, cmat = carry[3 * hh:3 * hh + 3]
                log_not, log_beta = _sb_logs(nqh[hh], k_blk, valid)
                w, c_next = _sb_weights(log_not, log_beta, tri, c, valid, n_sub)
                out += [c_next, acc + _dot(w.astype(BF16), v_blk), jnp.where(lane == j, c, cmat)]
            return tuple(out)

        zero = jnp.zeros((tq, LANES), F32)
        if tq == tk:
            off = pl.multiple_of(i * tk, tk)
            k_blk, v_blk = k_ref[pl.ds(off, tk), :], v_ref[pl.ds(off, tk), :]
            carry = ()
            for hh in range(2):
                carry += _sb_diag_fwd(nqh[hh], k_blk, v_blk, tri, n_sub) + (zero,)
        else:
            carry = block(j_diag, (zero,) * 6, _sb_valid(i, j_diag, tq, tk))
        def run_blocks(last, count):
            def trip(t, cr):
                for u in range(count):
                    cr = block(last - count * t - u, cr, None)
                return cr
            return trip

        n_quads, rest = j_diag // 4, j_diag % 4
        carry = lax.fori_loop(0, n_quads, run_blocks(j_diag - 1, 4), carry)
        carry = lax.fori_loop(0, rest // 2, run_blocks(rest - 1, 2), carry)
        carry = lax.fori_loop(0, rest % 2, lambda jj, cr: block(jj, cr, None), carry)
        c_ref[0] = carry[2]
        c_ref[1] = carry[5]
        o_ref[...] = jnp.where(masks[0], carry[1], carry[4]).astype(BF16)
        if n_g:
            pl.when((pl.program_id(0) == n_pairs - 1) & (i == n_i - 1))(wait)

    outs = pl.pallas_call(
        body, name=name, grid=(n_pairs, n_i),
        in_specs=[pl.BlockSpec((tq, LANES), lambda p, i: (i, p)),
                  pl.BlockSpec((s, LANES), lambda p, i: (0, p)),
                  pl.BlockSpec((s, LANES), lambda p, i: (0, n_pairs + p))] + [_ANY] * n_g,
        out_specs=[pl.BlockSpec((tq, LANES), lambda p, i: (i, p)),
                   pl.BlockSpec((2, tq, LANES), lambda p, i: (p, i, 0))] + [_ANY] * n_g,
        out_shape=[jax.ShapeDtypeStruct((s, SB_WIDTH + MEM_WIDTH), BF16),
                   jax.ShapeDtypeStruct((N_SB_HEADS, s, LANES), F32)] + _exchange_out_shape([], gather),
        scratch_shapes=_exchange_sems(n_g) if n_g else [],
        compiler_params=_params("arbitrary", "arbitrary"),
    )(projb, kv, kv, *gather)
    return outs[0], outs[1], outs[2:]


def sb_bwd(projb, kv, dcat, csave, name, scatter=()):
    s = projb.shape[0]
    tq, tk = _rows_tile(s, SB_ROWS), _rows_tile(s, SB_KEYS)
    n_sub = tk // SB_SUB
    n_pairs = SB_WIDTH // LANES
    n_x = len(scatter)
    n_i = s // tq

    def body(*refs):
        q_ref, k_ref, v_ref, do_ref, c_ref = refs[:5]
        dq_ref, dkb_ref, dvb_ref = refs[5 + n_x:8 + n_x]
        dk_ref, dv_ref = refs[8 + 2 * n_x:10 + 2 * n_x]
        i = pl.program_id(1)
        if n_x:
            start, wait = _exchange_ops(refs[5:5 + n_x], refs[8 + n_x:8 + 2 * n_x], refs[10 + 2 * n_x:], n_x)
            pl.when((pl.program_id(0) == 0) & (i == 0))(start)
        j_diag = (i * tq) // tk

        @pl.when(i == 0)
        def _():
            dk_ref[...] = jnp.zeros_like(dk_ref)
            dv_ref[...] = jnp.zeros_like(dv_ref)

        masks = _head_masks((tq, LANES))
        qs = q_ref[...] * QK_SCALE
        do = do_ref[...]
        tri_later = _sb_tri(later=True)
        tri_before = _sb_tri_pair_before()
        lane = lax.broadcasted_iota(jnp.int32, (tq, LANES), 1)
        qh = [jnp.where(hm, qs, 0.0).astype(BF16) for hm in masks]
        nqh = [jnp.where(hm, -qs, 0.0).astype(BF16) for hm in masks]
        doh = [jnp.where(hm, do, 0.0).astype(BF16) for hm in masks]

        def head_part(hh, rows, k_blk, v_blk, valid, c, e_before, dq):
            n_part = k_blk.shape[0] // SB_SUB
            log_not, log_beta = _sb_logs(nqh[hh][rows], k_blk, valid)
            w, _ = _sb_weights(log_not, log_beta, tri_later, c, valid, n_part)
            dv_h = _dot(w.astype(BF16), doh[hh][rows], _TN)
            e = w * _dot(doh[hh][rows], v_blk, _NT)
            sums = _sb_prefix_pairs(e, tri_before, n_part)
            run = e_before
            e_pre = []
            for m in range(n_part):
                e_pre.append(sums[m][0] + run)
                run = run + sums[m][1]
            dz = e - jnp.exp(log_beta) * (e + jnp.concatenate(e_pre, axis=1))
            if valid is not None:
                dz = jnp.where(valid, dz, 0.0)
            dz = dz.astype(BF16)
            return run, dq + _dot(dz, k_blk), _dot(dz, qh[hh][rows], _TN), dv_h

        def block(j, carry, valid):
            off = pl.multiple_of(j * tk, tk)
            k_blk, v_blk = k_ref[pl.ds(off, tk), :], v_ref[pl.ds(off, tk), :]
            out, dk, dv = [], None, None
            for hh in range(2):
                c = jnp.sum(jnp.where(lane == j, c_ref[hh], 0.0), axis=1, keepdims=True)
                run, dq, dk_h, dv_h = head_part(hh, slice(None), k_blk, v_blk, valid, jnp.broadcast_to(c, (tq, LANES)),
                                                carry[2 * hh], carry[2 * hh + 1])
                dk, dv = (dk_h, dv_h) if hh == 0 else (dk + dk_h, dv + dv_h)
                out += [run, dq]
            dk_ref[pl.ds(off, tk), :] += dk
            dv_ref[pl.ds(off, tk), :] += dv
            return tuple(out)

        def diagonal(carry):
            half = tq // 2
            off = pl.multiple_of(i * tk, tk)
            k_blk, v_blk = k_ref[pl.ds(off, tk), :], v_ref[pl.ds(off, tk), :]
            row_a = lax.broadcasted_iota(jnp.int32, (half, half), 0)
            col_a = lax.broadcasted_iota(jnp.int32, (half, half), 1)
            row_b = lax.broadcasted_iota(jnp.int32, (half, tk), 0) + half
            col_b = lax.broadcasted_iota(jnp.int32, (half, tk), 1)
            zero_c = jnp.zeros((half, LANES), F32)
            dqs, dk, dv = [], None, None
            for hh in range(2):
                e_before, dq = carry[2 * hh], carry[2 * hh + 1]
                _, dq_a, dk_a, dv_a = head_part(hh, slice(0, half), k_blk[:half], v_blk[:half], row_a > col_a,
                                                zero_c, e_before[:half], dq[:half])
                _, dq_b, dk_b, dv_b = head_part(hh, slice(half, tq), k_blk, v_blk, row_b > col_b,
                                                zero_c, e_before[half:], dq[half:])
                zeros_k = jnp.zeros((tk - half, LANES), F32)
                dk_h = dk_b + jnp.concatenate([dk_a, zeros_k], axis=0)
                dv_h = dv_b + jnp.concatenate([dv_a, zeros_k], axis=0)
                dk, dv = (dk_h, dv_h) if hh == 0 else (dk + dk_h, dv + dv_h)
                dqs.append(jnp.concatenate([dq_a, dq_b], axis=0))
            dk_ref[pl.ds(off, tk), :] += dk
            dv_ref[pl.ds(off, tk), :] += dv
            return dqs

        zero = jnp.zeros((tq, LANES), F32)
        def run_blocks(first, count):
            def trip(t, cr):
                for u in range(count):
                    cr = block(first + count * t + u, cr, None)
                return cr
            return trip

        n_quads, rest = j_diag // 4, j_diag % 4
        carry = lax.fori_loop(0, n_quads, run_blocks(0, 4), (zero,) * 4)
        carry = lax.fori_loop(0, rest // 2, run_blocks(4 * n_quads, 2), carry)
        carry = lax.fori_loop(0, rest % 2, run_blocks(j_diag - 1, 1), carry)
        if tq == tk:
            dq0, dq1 = diagonal(carry)
        else:
            carry = block(j_diag, carry, _sb_valid(i, j_diag, tq, tk))
            dq0, dq1 = carry[1], carry[3]
        dq_ref[...] = (jnp.where(masks[0], dq0, dq1) * QK_SCALE).astype(BF16)

        @pl.when(i == n_i - 1)
        def _():
            dkb_ref[...] = dk_ref[...].astype(BF16)
            dvb_ref[...] = dv_ref[...].astype(BF16)

        if n_x:
            pl.when((pl.program_id(0) == n_pairs - 1) & (i == n_i - 1))(wait)

    tile = pl.BlockSpec((tq, LANES), lambda p, i: (i, p))
    full = pl.BlockSpec((s, LANES), lambda p, i: (0, p))
    outs = pl.pallas_call(
        body, name=name, grid=(n_pairs, n_i),
        in_specs=[tile, full, pl.BlockSpec((s, LANES), lambda p, i: (0, n_pairs + p)), tile,
                  pl.BlockSpec((2, tq, LANES), lambda p, i: (p, i, 0))] + [_ANY] * n_x,
        out_specs=[tile, full, full] + [_ANY] * n_x,
        out_shape=[jax.ShapeDtypeStruct((s, SB_WIDTH + MEM_WIDTH), BF16), jax.ShapeDtypeStruct((s, SB_WIDTH), BF16),
                   jax.ShapeDtypeStruct((s, SB_WIDTH), BF16)] + _exchange_out_shape(scatter, []),
        scratch_shapes=[pltpu.VMEM((s, LANES), F32), pltpu.VMEM((s, LANES), F32)]
        + (_exchange_sems(n_x) if n_x else []),
        compiler_params=_params("arbitrary", "arbitrary"),
    )(projb, kv, kv, dcat, csave, *scatter)
    return outs[0], outs[1], outs[2], outs[3:]


def reduce_adamw(parts, w, m, v, name, after=None):
    n, r, c = parts.shape
    tr = next((t for t in (128, 176) if r % t == 0), r)
    bias1 = 1.0 - ADAM_B1 ** ADAM_STEP
    bias2 = 1.0 - ADAM_B2 ** ADAM_STEP

    def body(p_ref, w_ref, m_ref, v_ref, *rest):
        g_ref, d_ref, nm_ref, nv_ref = rest[-4:]
        g = p_ref[0].astype(F32)
        for k in range(1, n):
            g = g + p_ref[k].astype(F32)
        new_m = ADAM_B1 * m_ref[...] + (1.0 - ADAM_B1) * g
        new_v = ADAM_B2 * v_ref[...] + (1.0 - ADAM_B2) * (g * g)
        m_hat = new_m / bias1
        v_hat = new_v / bias2
        g_ref[...] = g
        d_ref[...] = -ADAM_LR * (m_hat / (jnp.sqrt(v_hat) + ADAM_EPS) + ADAM_WD * w_ref[...])
        nm_ref[...] = new_m
        nv_ref[...] = new_v

    row = pl.BlockSpec((tr, c), lambda i: (i, 0))
    return pl.pallas_call(
        body, name=name, grid=(r // tr,),
        in_specs=[pl.BlockSpec((n, tr, c), lambda i: (0, i, 0)), row, row, row] + ([] if after is None else [_ANY]),
        out_specs=[row, row, row, row],
        out_shape=[jax.ShapeDtypeStruct((r, c), F32)] * 4,
        compiler_params=_params("parallel"),
    )(parts, w, m, v, *(() if after is None else (after,)))


EARLY_0 = ("a_w_in", "a_w_mem_kv")
EARLY_1 = ("a_w_out",)
EARLY_2 = ("a_w_gu", "a_w_down")
EARLY_3 = ("w_kv", "b_w_q", "b_w_mem_kv")
EARLY = EARLY_0 + EARLY_1 + EARLY_2 + EARLY_3
LATE = ("b_w_out", "b_w_gu", "b_w_down")
BIG = EARLY + LATE
GATHER_3 = ("w_kv", "b_w_q")
GATHER_LATE = LATE + ("b_w_mem_kv",)
COL_SHARDED = ("a_w_gu", "w_kv", "b_w_gu")
SMALL_SHARDED = ("a_norm_mix", "a_scale", "a_norm_ffn")
SMALL_REPL = ("mem_norm", "kv_norm", "b_norm_mix", "b_norm_ffn", "final_norm")
WEIGHTS = ("mem_norm", "a_norm_mix", "a_w_in", "a_w_group", "a_scale", "a_w_mem_kv", "a_w_out", "a_norm_ffn", "a_w_gu",
           "a_w_down", "kv_norm", "w_kv", "b_norm_mix", "b_w_q", "b_w_mem_kv", "b_w_out", "b_norm_ffn", "b_w_gu",
           "b_w_down", "final_norm")


def _pad_row(v):
    v = v.reshape(1, -1)
    return jnp.pad(v, ((0, 0), (0, PACK_COLS - v.shape[1])))


def kernel(x, mem, mem_norm, a_norm_mix, a_w_in, a_w_group, a_scale, a_w_mem_kv, a_w_out, a_norm_ffn, a_w_gu, a_w_down, kv_norm, w_kv, b_norm_mix, b_w_q, b_w_mem_kv, b_w_out, b_norm_ffn, b_w_gu, b_w_down, final_norm, loss_target, m_mem_norm, m_a_norm_mix, m_a_w_in, m_a_w_group, m_a_scale, m_a_w_mem_kv, m_a_w_out, m_a_norm_ffn, m_a_w_gu, m_a_w_down, m_kv_norm, m_w_kv, m_b_norm_mix, m_b_w_q, m_b_w_mem_kv, m_b_w_out, m_b_norm_ffn, m_b_w_gu, m_b_w_down, m_final_norm, v_mem_norm, v_a_norm_mix, v_a_w_in, v_a_w_group, v_a_scale, v_a_w_mem_kv, v_a_w_out, v_a_norm_ffn, v_a_w_gu, v_a_w_down, v_kv_norm, v_w_kv, v_b_norm_mix, v_b_w_q, v_b_w_mem_kv, v_b_w_out, v_b_norm_ffn, v_b_w_gu, v_b_w_down, v_final_norm):
    w = dict(mem_norm=mem_norm, a_norm_mix=a_norm_mix, a_w_in=a_w_in, a_w_group=a_w_group, a_scale=a_scale,
             a_w_mem_kv=a_w_mem_kv, a_w_out=a_w_out, a_norm_ffn=a_norm_ffn, a_w_gu=a_w_gu, a_w_down=a_w_down,
             kv_norm=kv_norm, w_kv=w_kv, b_norm_mix=b_norm_mix, b_w_q=b_w_q, b_w_mem_kv=b_w_mem_kv, b_w_out=b_w_out,
             b_norm_ffn=b_norm_ffn, b_w_gu=b_w_gu, b_w_down=b_w_down, final_norm=final_norm)
    mom = dict(mem_norm=m_mem_norm, a_norm_mix=m_a_norm_mix, a_w_in=m_a_w_in, a_w_group=m_a_w_group, a_scale=m_a_scale,
               a_w_mem_kv=m_a_w_mem_kv, a_w_out=m_a_w_out, a_norm_ffn=m_a_norm_ffn, a_w_gu=m_a_w_gu,
               a_w_down=m_a_w_down, kv_norm=m_kv_norm, w_kv=m_w_kv, b_norm_mix=m_b_norm_mix, b_w_q=m_b_w_q,
               b_w_mem_kv=m_b_w_mem_kv, b_w_out=m_b_w_out, b_norm_ffn=m_b_norm_ffn, b_w_gu=m_b_w_gu,
               b_w_down=m_b_w_down, final_norm=m_final_norm)
    var = dict(mem_norm=v_mem_norm, a_norm_mix=v_a_norm_mix, a_w_in=v_a_w_in, a_w_group=v_a_w_group, a_scale=v_a_scale,
               a_w_mem_kv=v_a_w_mem_kv, a_w_out=v_a_w_out, a_norm_ffn=v_a_norm_ffn, a_w_gu=v_a_w_gu,
               a_w_down=v_a_w_down, kv_norm=v_kv_norm, w_kv=v_w_kv, b_norm_mix=v_b_norm_mix, b_w_q=v_b_w_q,
               b_w_mem_kv=v_b_w_mem_kv, b_w_out=v_b_w_out, b_norm_ffn=v_b_norm_ffn, b_w_gu=v_b_w_gu,
               b_w_down=v_b_w_down, final_norm=v_final_norm)

    me = 4 * lax.axis_index("x") + 2 * lax.axis_index("y") + lax.axis_index("c")
    shard2d = {n: w[n].shape[-2:] for n in BIG}
    shard = {n: w[n].reshape(shard2d[n]).astype(BF16) for n in BIG}
    n_sh = N_DEV // 2
    s_len = x.shape[1]

    small_send = jnp.concatenate([_pad_row(w[n]) for n in SMALL_SHARDED]
                                 + [jnp.zeros((8 - len(SMALL_SHARDED), PACK_COLS), F32)], axis=0)
    def own_slot(land, own):
        return lax.dynamic_update_slice(land, own[None], (me, 0, 0))

    def gather_done(handle, names, after, name):
        got = exchange_wait(handle, after, name)
        gathered.update({n: own_slot(g, shard[n]) for n, g in zip(names, got)})
        return got

    gathered = {}
    gather_0 = exchange_start([], [shard[n] for n in EARLY_0] + [small_send], x, "gather_0_start")
    gather_1 = exchange_start([], [shard[n] for n in EARLY_1], gather_0["token"], "gather_1_start")
    gather_2 = exchange_start([], [shard["a_w_gu"]], gather_1["token"], "gather_2_start")
    gather_2b = exchange_start([], [shard["a_w_down"]], gather_2["token"], "gather_2b_start")
    gather_3 = exchange_start([], [shard[n] for n in GATHER_3], gather_2b["token"], "gather_3_start")
    small_all = own_slot(gather_done(gather_0, EARLY_0, gather_3["token"], "gather_0_wait")[-1], small_send)
    a_norm_mix_f = small_all[:, 0, :a_norm_mix.shape[1]].reshape(-1)
    a_scale_f = small_all[:, 1, :a_scale.shape[1]].reshape(-1)
    a_norm_ffn_f = small_all[:, 2, :a_norm_ffn.shape[1]].reshape(-1)

    def rows_full(n):
        return gathered[n].reshape(-1, shard2d[n][1])

    def gu_views(tag):
        gu8 = gathered[tag + "_w_gu"]
        return gu8.reshape((2, n_sh) + gu8.shape[1:]), gu8

    def down_view(tag):
        down = gathered[tag + "_w_down"]
        return down.reshape(n_sh, -1, down.shape[2])

    wbd = jnp.zeros((SB_WIDTH, SB_WIDTH), BF16)
    for g in range(4):
        sl = slice(g * POOL_GROUP, (g + 1) * POOL_GROUP)
        wbd = wbd.at[sl, sl].set(a_w_group[0, g].astype(BF16))

    xs, mems, tgt = x[0], mem[0], loss_target[0]

    memn = rmsnorm_fwd(mems, mem_norm, "memn")
    (h0, proj), = norm_matmul(xs, [(a_norm_mix_f, rows_full("a_w_in"), F32)], "proj_a")
    mkv_a = matmul([(memn, rows_full("a_w_mem_kv"))], "nn", BF16, "mkv_a")
    pooled, cat_a = pool_fwd(proj, wbd, a_scale_f, "pool_fwd")
    cat_a = mem_attn_fwd(proj, mkv_a, cat_a, "mem_fwd_a")
    gather_done(gather_1, EARLY_1, cat_a, "gather_1_wait")
    x1 = matmul([(cat_a, rows_full("a_w_out"))], "nn", F32, "out_a", res=xs)
    h1 = rmsnorm_fwd(x1, a_norm_ffn_f, "h1")
    gather_done(gather_2, ("a_w_gu",), h1, "gather_2_wait")
    wgu_a, wgu8_a = gu_views("a")
    gu_a, act_a = ffn_up(h1, wgu_a, "ffn_up_a")
    gather_done(gather_2b, ("a_w_down",), act_a, "gather_2b_wait")
    wdown_a = down_view("a")
    x2 = matmul([(act_a, wdown_a)], "nn", F32, "down_a", res=x1, tm=1024, roles=("reduce", "reduce"), count=n_sh)

    gather_done(gather_3, GATHER_3, x2, "gather_3_wait")
    wkv = jnp.transpose(gathered["w_kv"], (1, 0, 2)).reshape(shard2d["w_kv"][0], -1)
    (hk, kv), (h3, projb) = norm_matmul(
        x2, [(kv_norm, wkv, BF16), (b_norm_mix[0], rows_full("b_w_q"), F32)], "kv_proj_b")
    cat_b, csave, got = sb_fwd(projb, kv, "sb_fwd", gather=[shard[n] for n in GATHER_LATE])
    gathered.update(zip(GATHER_LATE, got))
    mkv_b = matmul([(memn, rows_full("b_w_mem_kv"))], "nn", BF16, "mkv_b")
    cat_b = mem_attn_fwd(projb, mkv_b, cat_b, "mem_fwd_b")
    x3 = matmul([(cat_b, rows_full("b_w_out"))], "nn", F32, "out_b", res=x2)
    h4 = rmsnorm_fwd(x3, b_norm_ffn[0], "h4")
    (wgu_b, wgu8_b), wdown_b = gu_views("b"), down_view("b")
    gu_b, act_b = ffn_up(h4, wgu_b, "ffn_up_b")
    x4 = matmul([(act_b, wdown_b)], "nn", F32, "down_b", res=x3, tm=1024, roles=("reduce", "reduce"), count=n_sh)

    grads = {}
    dx4, dx4_bf, grads["final_norm"], loss_part = loss_head(x4, tgt, final_norm, "loss_head")

    def blocks(n, g):
        return g.reshape((N_DEV,) + tuple(shard2d[n]))

    def ffn_weight_grads(tag, dx_out_bf, h, gu, act, wgu8, wdown):
        fs = wgu8.shape[2]
        dgu8 = ffn_bwd_act(dx_out_bf, wdown, gu, "ffn_bwd_act_" + tag).reshape(N_DEV, s_len, fs)
        d_wdown = matmul([(act, dx_out_bf)], "tn", BF16, "dw_down_" + tag, tm=fs, roles=("batch", None), count=n_sh)
        grads[tag + "_w_gu"] = matmul([(dgu8, h)], "tn", BF16, "dw_gu_" + tag, tm=fs, roles=("batch", None), count=N_DEV)
        grads[tag + "_w_down"] = blocks(tag + "_w_down", d_wdown)
        return dgu8

    def ffn_input_grads(tag, x_in, dx_out, dgu8, wgu8, g_ffn):
        dx_in, dx_in_bf, dgs = matmul_norm_bwd(x_in, dx_out, [(g_ffn, [(dgu8, wgu8)])], "norm_bwd_ffn_" + tag)
        return dx_in, dx_in_bf, dgs[0]

    def grads_start(names, small, after, name):
        return exchange_start([grads[n] for n in names], small, after, name)

    def grads_done(handle, names, after, name):
        got = exchange_wait(handle, after, name)
        for n, g in zip(names, got):
            recv[n] = lax.dynamic_update_slice(g, lax.dynamic_slice(grads[n], (me, 0, 0), (1,) + g.shape[1:]), (me, 0, 0))
        return got

    recv = {}
    dgu8_b = ffn_weight_grads("b", dx4_bf, h4, gu_b, act_b, wgu8_b, wdown_b)
    dx3, dx3_bf, grads["b_norm_ffn"] = ffn_input_grads("b", x3, dx4, dgu8_b, wgu8_b, b_norm_ffn[0])

    dcat_b = matmul([(dx3_bf, rows_full("b_w_out"))], "nt", F32, "dcat_b")
    grads["b_w_out"] = blocks("b_w_out", matmul([(cat_b, dx3_bf)], "tn", BF16, "dw_out_b"))
    dprojb, dk, dv, recv_late = sb_bwd(projb, kv, dcat_b, csave, "sb_bwd", scatter=[grads[n] for n in LATE])
    dprojb, dmkv_b = mem_attn_bwd(projb, mkv_b, dcat_b, dprojb, "mem_bwd_b")
    grads["b_w_q"] = blocks("b_w_q", matmul([(h3, dprojb)], "tn", BF16, "dw_q_b"))
    d_wkv_t = jnp.concatenate([matmul([(dk, hk)], "tn", BF16, "dw_k", tm=SB_WIDTH),
                               matmul([(dv, hk)], "tn", BF16, "dw_v", tm=SB_WIDTH)], axis=0)
    grads["w_kv"] = d_wkv_t.reshape(N_DEV, -1, d_wkv_t.shape[1])
    dx2, dx2_bf, dgs = matmul_norm_bwd(
        x2, dx3, [(b_norm_mix[0], [(dprojb, rows_full("b_w_q"))]),
                  (kv_norm, [(dk, wkv[:, :SB_WIDTH]), (dv, wkv[:, SB_WIDTH:])])], "norm_bwd_x2")
    grads["b_norm_mix"], grads["kv_norm"] = dgs
    dmkv_b_bf = dmkv_b.astype(BF16)
    grads["b_w_mem_kv"] = blocks("b_w_mem_kv", matmul([(memn, dmkv_b_bf)], "tn", BF16, "dw_mkv_b"))

    grads_3 = grads_start(EARLY_3, [], grads["b_w_mem_kv"], "grads_3_start")
    dgu8_a = ffn_weight_grads("a", dx2_bf, h1, gu_a, act_a, wgu8_a, wdown_a)
    grads_2 = grads_start(EARLY_2, [], grads_3["token"], "grads_2_start")
    dx1, dx1_bf, grads["a_norm_ffn"] = ffn_input_grads(
        "a", x1, dx2, dgu8_a, wgu8_a, a_norm_ffn_f + grads_2["token"][0, 0])

    dcat_a = matmul([(dx1_bf, rows_full("a_w_out"))], "nt", F32, "dcat_a")
    grads["a_w_out"] = blocks("a_w_out", matmul([(cat_a, dx1_bf)], "tn", BF16, "dw_out_a"))
    dproj, d_wbd, grads["a_scale"] = pool_bwd(dcat_a, pooled, wbd, a_scale_f, "pool_bwd")
    grads["a_w_group"] = jnp.stack(
        [d_wbd[g * POOL_GROUP:(g + 1) * POOL_GROUP, g * POOL_GROUP:(g + 1) * POOL_GROUP] for g in range(4)])
    group_grads = grads["a_w_group"].reshape(-1, PACK_COLS).astype(BF16)
    grads_1 = exchange_start([grads[n] for n in EARLY_1], [group_grads], grads_2["token"], "grads_1_start")
    dproj, dmkv_a = mem_attn_bwd(proj, mkv_a, dcat_a, dproj, "mem_bwd_a")
    grads["a_w_in"] = blocks("a_w_in", matmul([(h0, dproj)], "tn", BF16, "dw_in_a"))
    dmkv_a_bf = dmkv_a.astype(BF16)
    grads["a_w_mem_kv"] = blocks("a_w_mem_kv", matmul([(memn, dmkv_a_bf)], "tn", BF16, "dw_mkv_a"))
    grads_0 = grads_start(EARLY_0, [], grads_1["token"], "grads_0_start")
    dx, _, dgs = matmul_norm_bwd(
        xs, dx1, [(a_norm_mix_f + grads_0["token"][0, 0], [(dproj, rows_full("a_w_in"))])], "norm_bwd_x")
    grads["a_norm_mix"] = dgs[0]
    dmemn = matmul([(dmkv_a_bf, rows_full("a_w_mem_kv")), (dmkv_b_bf, rows_full("b_w_mem_kv"))], "nt", F32, "dmemn")
    _, _, dgs = rmsnorm_bwd(mems, jnp.zeros_like(mems), [(mem_norm, dmemn)], "norm_bwd_mem")
    grads["mem_norm"] = dgs[0]

    small_names = SMALL_REPL + SMALL_SHARDED
    head_rows = 16

    def small_pack(rows_of):
        head = jnp.zeros((head_rows, PACK_COLS), F32)
        for r, (v, col) in enumerate(rows_of):
            v = v.reshape(1, -1)
            if col is None:
                head = head + jnp.pad(v, ((r, head_rows - 1 - r), (0, PACK_COLS - v.shape[1])))
            else:
                head = lax.dynamic_update_slice(head, v, (r, col))
        return head

    small_grads = small_pack([(grads[n], None) for n in small_names] + [(loss_part[:, :1], None)])
    grads_s = exchange_start([], [small_grads], dx, "grads_s_start")

    def small_state(src):
        return small_pack([(src[n], None) for n in SMALL_REPL]
                          + [(src[n], me * src[n].shape[-1]) for n in SMALL_SHARDED])

    def state2d(src, n):
        a = src[n].reshape(shard2d[n])
        return a.T if n in COL_SHARDED else a

    def adamw(names, after):
        for n in names:
            big_out[n] = reduce_adamw(recv[n], state2d(w, n), state2d(mom, n), state2d(var, n), "adamw_" + n, after=after)
        return big_out[names[-1]][0]

    big_out = {}
    recv.update(zip(LATE, recv_late))
    done = adamw(LATE, grads_s["token"])
    grads_done(grads_3, EARLY_3, done, "grads_3_wait")
    done = adamw(EARLY_3, None)
    grads_done(grads_2, EARLY_2, done, "grads_2_wait")
    done = adamw(EARLY_2, None)
    group_recv = own_slot(grads_done(grads_1, EARLY_1, done, "grads_1_wait")[-1], group_grads)
    done = adamw(EARLY_1, None)
    group_out = reduce_adamw(group_recv, *[src["a_w_group"].reshape(-1, PACK_COLS) for src in (w, mom, var)], "adamw_group")
    grads_done(grads_0, EARLY_0, group_out[0], "grads_0_wait")
    done = adamw(EARLY_0, None)
    small_recv = own_slot(exchange_wait(grads_s, done, "grads_s_wait")[0], small_grads)
    small_out = reduce_adamw(small_recv, small_state(w), small_state(mom), small_state(var), "adamw_small")

    def unpack(kind):
        out = {n: (big_out[n][kind].T if n in COL_SHARDED else big_out[n][kind]).reshape(w[n].shape) for n in BIG}
        so = small_out[kind]
        for r, n in enumerate(SMALL_REPL):
            out[n] = so[r, :w[n].shape[-1]].reshape(w[n].shape)
        for r, n in enumerate(SMALL_SHARDED):
            width = w[n].shape[-1]
            out[n] = lax.dynamic_slice(so, (len(SMALL_REPL) + r, me * width), (1, width)).reshape(w[n].shape)
        out["a_w_group"] = group_out[kind].reshape(a_w_group.shape)
        return out

    loss = small_out[0][len(small_names), 0]
    results = [loss, dx.reshape(x.shape)]
    for kind in range(4):
        out = unpack(kind)
        results += [out[n] for n in WEIGHTS]
    return tuple(results)
```

```python
import functools

import jax
import jax.numpy as jnp
from jax import lax
from jax.experimental import pallas as pl
from jax.experimental.pallas import tpu as pltpu

F32 = jnp.float32
BF16 = jnp.bfloat16

N_DEV = 8
HEAD_DIM = 64
N_SB_HEADS = 12
SB_WIDTH = N_SB_HEADS * HEAD_DIM
MEM_WIDTH = 4 * HEAD_DIM
POOL_WINDOWS = (2, 4, 8, 16)
POOL_GROUP = SB_WIDTH // 4
POOL_HALO = 16
EPS = 1e-6
QK_SCALE = HEAD_DIM ** -0.5
LOG2_E = 1.4426950408889634
LANES = 128
PACK_COLS = 1024

ADAM_LR = 0.001
ADAM_B1 = 0.9
ADAM_B2 = 0.999
ADAM_EPS = 1e-08
ADAM_WD = 0.01
ADAM_STEP = 10

VMEM_LIMIT = 48 * 1024 * 1024

_NT = (((1,), (1,)), ((), ()))
_TN = (((0,), (0,)), ((), ()))
_NN = (((1,), (0,)), ((), ()))


def _params(*sem):
    return pltpu.CompilerParams(dimension_semantics=sem, vmem_limit_bytes=VMEM_LIMIT)


def _dot(a, b, dims=_NN):
    return lax.dot_general(a, b, dims, preferred_element_type=F32)


def exchange(a2a, ag, name):
    n_a2a, n_arr = len(a2a), len(a2a) + len(ag)

    def body(*refs):
        start, wait = _exchange_ops(refs[:n_arr], refs[n_arr:2 * n_arr], refs[2 * n_arr:], n_a2a)
        start()
        wait()

    return pl.pallas_call(
        body, name=name, out_shape=_exchange_out_shape(a2a, ag),
        in_specs=[_ANY] * n_arr, out_specs=[_ANY] * n_arr,
        scratch_shapes=_exchange_sems(n_arr),
    )(*a2a, *ag)


_ANY = pl.BlockSpec(memory_space=pl.ANY)
_HBM = pl.BlockSpec(memory_space=pltpu.HBM)
_SEM = pl.BlockSpec(memory_space=pltpu.SEMAPHORE)
_DATAFLOW = pltpu.SideEffectType.DATAFLOW_SIDE_EFFECTING


def exchange_start(a2a, ag, after, name):
    n_a2a, n_arr = len(a2a), len(a2a) + len(ag)
    arrays = [pltpu.with_memory_space_constraint(v, pltpu.HBM) for v in list(a2a) + list(ag)]
    shapes = _exchange_out_shape(a2a, ag)
    lands = [pltpu.with_memory_space_constraint(lax.empty(sh.shape, sh.dtype), pltpu.HBM) for sh in shapes]

    def body(*refs):
        ins, outs = refs[:n_arr], refs[n_arr:2 * n_arr]
        send_sems, recv_sems, token = refs[4 * n_arr + 1:4 * n_arr + 4]
        start, _ = _exchange_ops(ins, outs, (send_sems, recv_sems, None), n_a2a, local_copies=False)
        start()
        token[...] = jnp.zeros_like(token)

    outs = pl.pallas_call(
        body, name=name,
        out_shape=[pltpu.HBM(v.shape, v.dtype) for v in arrays] + [pltpu.HBM(sh.shape, sh.dtype) for sh in shapes]
        + [pltpu.SemaphoreType.DMA((n_arr * (N_DEV - 1),)), pltpu.SemaphoreType.DMA((n_arr * (N_DEV - 1),)),
           jax.ShapeDtypeStruct((8, LANES), F32)],
        in_specs=[_HBM] * (2 * n_arr) + [_ANY],
        out_specs=[_HBM] * (2 * n_arr) + [_SEM, _SEM, pl.BlockSpec(memory_space=pltpu.VMEM)],
        input_output_aliases={i: i for i in range(2 * n_arr)},
        compiler_params=pltpu.CompilerParams(has_side_effects=_DATAFLOW),
    )(*arrays, *lands, after)
    return dict(ins=outs[:n_arr], lands=outs[n_arr:2 * n_arr], sems=outs[2 * n_arr:2 * n_arr + 2], n_a2a=n_a2a,
                token=outs[2 * n_arr + 2])


def exchange_wait(handle, after, name):
    ins, lands, sems, n_a2a = handle["ins"], handle["lands"], handle["sems"], handle["n_a2a"]
    n_arr = len(ins)

    def body(*refs):
        in_refs, land_refs = refs[:n_arr], refs[n_arr:2 * n_arr]
        send_sems, recv_sems = refs[2 * n_arr:2 * n_arr + 2]
        _, wait = _exchange_ops(in_refs, land_refs, (send_sems, recv_sems, None), n_a2a, local_copies=False)
        wait()

    outs = pl.pallas_call(
        body, name=name,
        out_shape=[pltpu.HBM(v.shape, v.dtype) for v in list(ins) + list(lands)],
        in_specs=[_HBM] * (2 * n_arr) + [_SEM, _SEM, _ANY], out_specs=[_HBM] * (2 * n_arr),
        input_output_aliases={i: i for i in range(2 * n_arr)},
        compiler_params=pltpu.CompilerParams(has_side_effects=_DATAFLOW),
    )(*ins, *lands, *sems, after)
    return outs[n_arr:]


def _exchange_out_shape(a2a, ag):
    out_shape = [jax.ShapeDtypeStruct(v.shape, v.dtype) for v in a2a]
    return out_shape + [jax.ShapeDtypeStruct((N_DEV,) + v.shape, v.dtype) for v in ag]


def _exchange_sems(n_arr):
    return [pltpu.SemaphoreType.DMA((n_arr * (N_DEV - 1),)), pltpu.SemaphoreType.DMA((n_arr * (N_DEV - 1),)),
            pltpu.SemaphoreType.DMA((n_arr,))]


def _exchange_ops(ins, outs, sems, n_a2a, local_copies=True):
    send_sems, recv_sems, local_sems = sems
    n_arr = len(ins)
    x, y, c = lax.axis_index("x"), lax.axis_index("y"), lax.axis_index("c")
    me = 4 * x + 2 * y + c

    def peer_of(k):
        px = 1 - x if k & 4 else x
        py = 1 - y if k & 2 else y
        pc = 1 - c if k & 1 else c
        return (px, py, pc), 4 * px + 2 * py + pc

    def copy(a, k, landing):
        peer, pid = peer_of(k)
        src = ins[a].at[pid] if a < n_a2a else ins[a]
        return pltpu.make_async_remote_copy(
            src_ref=src, dst_ref=outs[a].at[pid if landing else me], send_sem=send_sems.at[a * (N_DEV - 1) + k - 1],
            recv_sem=recv_sems.at[a * (N_DEV - 1) + k - 1], device_id=peer, device_id_type=pl.DeviceIdType.MESH)

    def local(a):
        src = ins[a].at[me] if a < n_a2a else ins[a]
        return pltpu.make_async_copy(src, outs[a].at[me], local_sems.at[a])

    order = (6, 7, 4, 5, 2, 3, 1)

    def start():
        if local_copies:
            for a in range(n_arr):
                local(a).start()
        for a in range(n_arr):
            for k in order:
                copy(a, k, landing=False).start()

    def wait():
        for a in range(n_arr):
            for k in order:
                copy(a, k, landing=True).wait_recv()
        for a in range(n_arr):
            for k in order:
                copy(a, k, landing=False).wait_send()
        if local_copies:
            for a in range(n_arr):
                local(a).wait()

    return start, wait


def _rows_tile(n, want):
    t = min(want, n)
    assert n % t == 0, (n, t)
    return t


def rmsnorm_fwd(x, g, name):
    s, d = x.shape
    tm = _rows_tile(s, 512)

    def body(x_ref, g_ref, h_ref):
        xv = x_ref[...]
        r = lax.rsqrt(jnp.mean(xv * xv, axis=-1, keepdims=True) + EPS)
        h_ref[...] = ((xv * r) * g_ref[...]).astype(h_ref.dtype)

    return pl.pallas_call(
        body, name=name, grid=(s // tm,),
        in_specs=[pl.BlockSpec((tm, d), lambda i: (i, 0)), pl.BlockSpec((1, d), lambda i: (0, 0))],
        out_specs=pl.BlockSpec((tm, d), lambda i: (i, 0)),
        out_shape=jax.ShapeDtypeStruct((s, d), BF16),
        compiler_params=_params("parallel"),
    )(x, g.reshape(1, d))


def norm_matmul(x, heads, name):
    s, d = x.shape
    tm = _rows_tile(s, 512)
    n = len(heads)

    def body(*refs):
        x_ref = refs[0]
        xv = x_ref[...]
        xhat = xv * lax.rsqrt(jnp.mean(xv * xv, axis=-1, keepdims=True) + EPS)
        for k in range(n):
            g_ref, w_ref = refs[1 + 2 * k], refs[2 + 2 * k]
            h_ref, y_ref = refs[1 + 2 * n + 2 * k], refs[2 + 2 * n + 2 * k]
            h = (xhat * g_ref[...]).astype(BF16)
            h_ref[...] = h
            y_ref[...] = _dot(h, w_ref[...]).astype(y_ref.dtype)

    row = pl.BlockSpec((tm, d), lambda i: (i, 0))
    in_specs, out_specs, out_shape, args = [row], [], [], [x]
    for g, w_k, dtype in heads:
        in_specs += [pl.BlockSpec((1, d), lambda i: (0, 0)), pl.BlockSpec(w_k.shape, lambda i: (0, 0))]
        args += [g.reshape(1, d), w_k]
        out_specs += [row, pl.BlockSpec((tm, w_k.shape[1]), lambda i: (i, 0))]
        out_shape += [jax.ShapeDtypeStruct((s, d), BF16), jax.ShapeDtypeStruct((s, w_k.shape[1]), dtype)]
    outs = pl.pallas_call(
        body, name=name, grid=(s // tm,), in_specs=in_specs, out_specs=out_specs, out_shape=out_shape,
        compiler_params=_params("parallel"),
    )(*args)
    return [(outs[2 * k], outs[2 * k + 1]) for k in range(n)]


def rmsnorm_bwd(x, dres, pairs, name):
    s, d = x.shape
    tm = _rows_tile(s, 256)
    n = len(pairs)

    def body(*refs):
        x_ref, dres_ref = refs[0], refs[1]
        g_refs, dh_refs = refs[2:2 + n], refs[2 + n:2 + 2 * n]
        dx_ref, dxb_ref = refs[2 + 2 * n], refs[3 + 2 * n]
        dg_refs = refs[4 + 2 * n:]
        i = pl.program_id(0)
        xv = x_ref[...]
        r = lax.rsqrt(jnp.mean(xv * xv, axis=-1, keepdims=True) + EPS)
        xhat = xv * r
        acc = dres_ref[...]
        for k in range(n):
            dh = dh_refs[k][...].astype(F32)
            part = jnp.sum(dh * xhat, axis=0, keepdims=True)

            @pl.when(i == 0)
            def _(k=k, part=part):
                dg_refs[k][...] = part

            @pl.when(i > 0)
            def _(k=k, part=part):
                dg_refs[k][...] += part

            dxh = dh * g_refs[k][...]
            acc = acc + r * (dxh - xhat * jnp.mean(dxh * xhat, axis=-1, keepdims=True))
        dx_ref[...] = acc
        dxb_ref[...] = acc.astype(BF16)

    row = pl.BlockSpec((tm, d), lambda i: (i, 0))
    vec = pl.BlockSpec((1, d), lambda i: (0, 0))
    outs = pl.pallas_call(
        body, name=name, grid=(s // tm,),
        in_specs=[row, row] + [vec] * n + [row] * n,
        out_specs=[row, row] + [vec] * n,
        out_shape=[jax.ShapeDtypeStruct((s, d), F32), jax.ShapeDtypeStruct((s, d), BF16)]
        + [jax.ShapeDtypeStruct((1, d), F32)] * n,
        compiler_params=_params("arbitrary"),
    )(x, dres, *[g.reshape(1, d) for g, _ in pairs], *[dh for _, dh in pairs])
    return outs[0], outs[1], outs[2:]


def matmul_norm_bwd(x, dres, heads, name):
    s, d = x.shape
    tm = _rows_tile(s, 512)
    n = len(heads)
    n_ops = [len(pairs) for _, pairs in heads]

    def body(*refs):
        x_ref, dres_ref = refs[0], refs[1]
        n_in = 2 + n + 2 * sum(n_ops)
        dx_ref, dxb_ref = refs[n_in], refs[n_in + 1]
        dg_refs = refs[n_in + 2:]
        i = pl.program_id(0)
        xv = x_ref[...]
        r = lax.rsqrt(jnp.mean(xv * xv, axis=-1, keepdims=True) + EPS)
        xhat = xv * r
        acc = dres_ref[...]
        pos = 2 + n
        for k in range(n):
            dh = None
            for a, _ in heads[k][1]:
                a_ref, w_ref = refs[pos], refs[pos + 1]
                for kk in range(a.shape[0] if a.ndim == 3 else 1):
                    prod = _dot(a_ref[kk], w_ref[kk], _NT) if a.ndim == 3 else _dot(a_ref[...], w_ref[...], _NT)
                    dh = prod if dh is None else dh + prod
                pos += 2
            part = jnp.sum(dh * xhat, axis=0, keepdims=True)

            @pl.when(i == 0)
            def _(k=k, part=part):
                dg_refs[k][...] = part

            @pl.when(i > 0)
            def _(k=k, part=part):
                dg_refs[k][...] += part

            dxh = dh * refs[2 + k][...]
            acc = acc + r * (dxh - xhat * jnp.mean(dxh * xhat, axis=-1, keepdims=True))
        dx_ref[...] = acc
        dxb_ref[...] = acc.astype(BF16)

    row = pl.BlockSpec((tm, d), lambda i: (i, 0))
    vec = pl.BlockSpec((1, d), lambda i: (0, 0))
    in_specs, args = [row, row] + [vec] * n, [x, dres] + [g.reshape(1, d) for g, _ in heads]
    for _, pairs in heads:
        for a, w_k in pairs:
            if a.ndim == 3:
                in_specs += [pl.BlockSpec((a.shape[0], tm, a.shape[2]), lambda i: (0, i, 0)),
                             pl.BlockSpec(w_k.shape, lambda i: (0, 0, 0), pipeline_mode=pl.Buffered(1))]
            else:
                in_specs += [pl.BlockSpec((tm, a.shape[1]), lambda i: (i, 0)),
                             pl.BlockSpec(w_k.shape, lambda i: (0, 0), pipeline_mode=pl.Buffered(1))]
            args += [a, w_k]
    outs = pl.pallas_call(
        body, name=name, grid=(s // tm,), in_specs=in_specs, out_specs=[row, row] + [vec] * n,
        out_shape=[jax.ShapeDtypeStruct((s, d), F32), jax.ShapeDtypeStruct((s, d), BF16)]
        + [jax.ShapeDtypeStruct((1, d), F32)] * n,
        compiler_params=_params("arbitrary"),
    )(*args)
    return outs[0], outs[1], outs[2:]


def loss_head(x, tgt, g, name):
    s, d = x.shape
    tm = _rows_tile(s, 512)

    def body(x_ref, t_ref, g_ref, dx_ref, dxb_ref, dg_ref, loss_ref):
        i = pl.program_id(0)
        xv = x_ref[...]
        gv = g_ref[...]
        r = lax.rsqrt(jnp.mean(xv * xv, axis=-1, keepdims=True) + EPS)
        xhat = xv * r
        diff = xhat * gv - t_ref[...]
        part_loss = 0.5 * jnp.sum(jnp.mean(diff * diff, axis=-1, keepdims=True), axis=0, keepdims=True)
        dy = diff * (1.0 / d)
        part_dg = jnp.sum(dy * xhat, axis=0, keepdims=True)

        @pl.when(i == 0)
        def _():
            dg_ref[...] = part_dg
            loss_ref[...] = jnp.broadcast_to(part_loss, loss_ref.shape)

        @pl.when(i > 0)
        def _():
            dg_ref[...] += part_dg
            loss_ref[...] += jnp.broadcast_to(part_loss, loss_ref.shape)

        dxh = dy * gv
        dx = r * (dxh - xhat * jnp.mean(dxh * xhat, axis=-1, keepdims=True))
        dx_ref[...] = dx
        dxb_ref[...] = dx.astype(BF16)

    row = pl.BlockSpec((tm, d), lambda i: (i, 0))
    vec = pl.BlockSpec((1, d), lambda i: (0, 0))
    return pl.pallas_call(
        body, name=name, grid=(s // tm,),
        in_specs=[row, row, vec], out_specs=[row, row, vec, pl.BlockSpec((1, LANES), lambda i: (0, 0))],
        out_shape=[jax.ShapeDtypeStruct((s, d), F32), jax.ShapeDtypeStruct((s, d), BF16),
                   jax.ShapeDtypeStruct((1, d), F32), jax.ShapeDtypeStruct((1, LANES), F32)],
        compiler_params=_params("arbitrary"),
    )(x, tgt, g.reshape(1, d))


def matmul(pairs, mode, out_dtype, name, res=None, tm=512, tn=None, roles=(None, None), count=1):
    a0, b0 = pairs[0]
    m = a0.shape[-1] if mode == "tn" else a0.shape[-2]
    n = b0.shape[-2] if mode == "nt" else b0.shape[-1]
    if tn is None:
        tn = n if (mode != "tn" and n <= 1024) else 512
    tm, tn = _rows_tile(m, tm), _rows_tile(n, tn)
    dims = {"nn": _NN, "nt": _NT, "tn": _TN}[mode]
    npairs = len(pairs)
    batched, reducing = "batch" in roles, "reduce" in roles
    assert not (batched and reducing) and (res is None or not batched)
    n_in = 2 * npairs + (res is not None)

    def body(*refs):
        o_ref = refs[n_in]
        acc = None
        for p in range(npairs):
            a_ref, b_ref = refs[2 * p], refs[2 * p + 1]
            for kk in range(count if reducing else 1):
                a = a_ref[kk] if roles[0] == "reduce" else a_ref[...]
                b = b_ref[kk] if roles[1] == "reduce" else b_ref[...]
                d = _dot(a.astype(BF16), b.astype(BF16), dims)
                acc = d if acc is None else acc + d
        if res is not None:
            acc = acc + refs[2 * npairs][...]
        o_ref[...] = acc.astype(o_ref.dtype)

    def spec(arr, role, block, index):
        if arr.ndim == 2:
            return pl.BlockSpec(block, lambda bb, i, j: index(i, j))
        assert role in ("batch", "reduce") and arr.shape[0] == count, (name, role, arr.shape)
        if role == "batch":
            return pl.BlockSpec((None,) + block, lambda bb, i, j: (bb,) + index(i, j))
        return pl.BlockSpec((count,) + block, lambda bb, i, j: (0,) + index(i, j))

    in_specs, args = [], []
    for a, b in pairs:
        if mode == "tn":
            in_specs.append(spec(a, roles[0], (a.shape[-2], tm), lambda i, j: (0, i)))
        else:
            in_specs.append(spec(a, roles[0], (tm, a.shape[-1]), lambda i, j: (i, 0)))
        if mode == "nt":
            in_specs.append(spec(b, roles[1], (tn, b.shape[-1]), lambda i, j: (j, 0)))
        else:
            in_specs.append(spec(b, roles[1], (b.shape[-2], tn), lambda i, j: (0, j)))
        args += [a, b]
    if res is not None:
        in_specs.append(pl.BlockSpec((tm, tn), lambda bb, i, j: (i, j)))
        args.append(res)
    if batched:
        out_spec = pl.BlockSpec((None, tm, tn), lambda bb, i, j: (bb, i, j))
        out_shape = jax.ShapeDtypeStruct((count, m, n), out_dtype)
    else:
        out_spec = pl.BlockSpec((tm, tn), lambda bb, i, j: (i, j))
        out_shape = jax.ShapeDtypeStruct((m, n), out_dtype)
    return pl.pallas_call(
        body, name=name, grid=(count if batched else 1, m // tm, n // tn),
        in_specs=in_specs, out_specs=out_spec, out_shape=out_shape,
        compiler_params=_params("parallel", "parallel", "parallel"),
    )(*args)


def _sigmoid(v):
    return 0.5 * jnp.tanh(0.5 * v) + 0.5


def ffn_up(h, wgu, name):
    s, d = h.shape
    _, n_sh, _, fs = wgu.shape
    tm = _rows_tile(s, 1024)

    def body(h_ref, w_ref, gu_ref, act_ref):
        hv = h_ref[...]
        gate = _dot(hv, w_ref[0])
        up = _dot(hv, w_ref[1])
        gu_ref[0] = gate.astype(BF16)
        gu_ref[1] = up.astype(BF16)
        act_ref[...] = ((gate * _sigmoid(gate)) * up).astype(BF16)

    return pl.pallas_call(
        body, name=name, grid=(n_sh, s // tm),
        in_specs=[pl.BlockSpec((tm, d), lambda j, i: (i, 0)), pl.BlockSpec((2, None, d, fs), lambda j, i: (0, j, 0, 0))],
        out_specs=[pl.BlockSpec((2, None, tm, fs), lambda j, i: (0, j, i, 0)),
                   pl.BlockSpec((None, tm, fs), lambda j, i: (j, i, 0))],
        out_shape=[jax.ShapeDtypeStruct((2, n_sh, s, fs), BF16), jax.ShapeDtypeStruct((n_sh, s, fs), BF16)],
        compiler_params=_params("parallel", "parallel"),
    )(h, wgu)


def ffn_bwd_act(dxo, wdown, gu, name):
    s, d = dxo.shape
    n_sh, fs, _ = wdown.shape
    tm = _rows_tile(s, 1024)

    def body(dx_ref, w_ref, gu_ref, dgu_ref):
        dact = _dot(dx_ref[...], w_ref[...], _NT)
        gate = gu_ref[0].astype(F32)
        sig = _sigmoid(gate)
        dgu_ref[0] = ((dact * gu_ref[1].astype(F32)) * (sig * (1.0 + gate * (1.0 - sig)))).astype(BF16)
        dgu_ref[1] = (dact * (gate * sig)).astype(BF16)

    tile = pl.BlockSpec((2, None, tm, fs), lambda j, i: (0, j, i, 0))
    return pl.pallas_call(
        body, name=name, grid=(n_sh, s // tm),
        in_specs=[pl.BlockSpec((tm, d), lambda j, i: (i, 0)), pl.BlockSpec((None, fs, d), lambda j, i: (j, 0, 0)), tile],
        out_specs=tile,
        out_shape=jax.ShapeDtypeStruct((2, n_sh, s, fs), BF16),
        compiler_params=_params("parallel", "parallel"),
    )(dxo, wdown, gu)


def _pool_select(col, by_window):
    out = by_window[3]
    for g in (2, 1, 0):
        out = jnp.where(col < (g + 1) * POOL_GROUP, by_window[g], out)
    return out


def _pool_counts(pos):
    return [jnp.minimum(pos + 1, w).astype(F32) for w in POOL_WINDOWS]


def pool_fwd(proj, wbd, scale, name):
    s = proj.shape[0]
    tm = _rows_tile(s, 512)
    nt = s // tm
    width = SB_WIDTH
    per = tm // POOL_HALO

    def body(u_ref, halo_ref, w_ref, sc_ref, pooled_ref, grouped_ref):
        i = pl.program_id(0)
        u = u_ref[...]
        halo = jnp.where(i > 0, halo_ref[...], 0.0)
        ext = jnp.concatenate([halo, u], axis=0)
        sums = []
        acc = ext
        for k in (1, 2, 4, 8):
            acc = acc + pltpu.roll(acc, k, 0)
            sums.append(acc[POOL_HALO:])
        pos = i * tm + lax.broadcasted_iota(jnp.int32, (tm, width), 0)
        col = lax.broadcasted_iota(jnp.int32, (tm, width), 1)
        means = [sm / cnt for sm, cnt in zip(sums, _pool_counts(pos))]
        pooled = (_pool_select(col, means) - u).astype(BF16)
        pooled_ref[...] = pooled
        grouped_ref[...] = (_dot(pooled, w_ref[...]) * sc_ref[...]).astype(BF16)

    row = pl.BlockSpec((tm, width), lambda i: (i, 0))
    return pl.pallas_call(
        body, name=name, grid=(nt,),
        in_specs=[row, pl.BlockSpec((POOL_HALO, width), lambda i: (jnp.maximum(i * per - 1, 0), 0)),
                  pl.BlockSpec((width, width), lambda i: (0, 0)), pl.BlockSpec((1, width), lambda i: (0, 0))],
        out_specs=[row, row],
        out_shape=[jax.ShapeDtypeStruct((s, width), BF16), jax.ShapeDtypeStruct((s, width + MEM_WIDTH), BF16)],
        compiler_params=_params("parallel"),
    )(proj, proj, wbd, scale.reshape(1, width))


def pool_bwd(dcat, pooled, wbd, scale, name):
    s = dcat.shape[0]
    tm = _rows_tile(s, 512)
    nt = s // tm
    width = SB_WIDTH
    per = tm // POOL_HALO
    n_ext = tm + POOL_HALO

    def body(dg_ref, halo_ref, pooled_ref, w_ref, sc_ref, du_ref, dw_ref, dsc_ref):
        i = pl.program_id(0)
        w = w_ref[...]
        sc = sc_ref[...]
        dg = dg_ref[...]
        pooled = pooled_ref[...]
        pg = _dot(pooled, w)
        dpg = (dg * sc).astype(BF16)
        part_sc = jnp.sum(dg * pg, axis=0, keepdims=True)
        part_w = _dot(pooled, dpg, _TN)

        @pl.when(i == 0)
        def _():
            dsc_ref[...] = part_sc
            dw_ref[...] = part_w

        @pl.when(i > 0)
        def _():
            dsc_ref[...] += part_sc
            dw_ref[...] += part_w

        dpooled = _dot(dpg, w, _NT)
        halo_dpg = (jnp.where(i < nt - 1, halo_ref[...], 0.0) * sc).astype(BF16)
        ext = jnp.concatenate([dpooled, _dot(halo_dpg, w, _NT)], axis=0)
        pos = i * tm + lax.broadcasted_iota(jnp.int32, (n_ext, width), 0)
        col = lax.broadcasted_iota(jnp.int32, (tm, width), 1)
        outs = []
        for k_idx, cnt in enumerate(_pool_counts(pos)):
            acc = ext / cnt
            for k in (1, 2, 4, 8)[:k_idx + 1]:
                acc = acc + pltpu.roll(acc, n_ext - k, 0)
            outs.append(acc[:tm])
        du_ref[...] = (_pool_select(col, outs) - dpooled).astype(BF16)

    row = pl.BlockSpec((tm, width), lambda i: (i, 0))
    return pl.pallas_call(
        body, name=name, grid=(nt,),
        in_specs=[row, pl.BlockSpec((POOL_HALO, width), lambda i: (jnp.minimum((i + 1) * per, s // POOL_HALO - 1), 0)),
                  row, pl.BlockSpec((width, width), lambda i: (0, 0)), pl.BlockSpec((1, width), lambda i: (0, 0))],
        out_specs=[row, pl.BlockSpec((width, width), lambda i: (0, 0)), pl.BlockSpec((1, width), lambda i: (0, 0))],
        out_shape=[jax.ShapeDtypeStruct((s, width + MEM_WIDTH), BF16), jax.ShapeDtypeStruct((width, width), F32),
                   jax.ShapeDtypeStruct((1, width), F32)],
        compiler_params=_params("arbitrary"),
    )(dcat, dcat, pooled, wbd, scale.reshape(1, width))


def _head_masks(shape):
    lane = lax.broadcasted_iota(jnp.int32, shape, 1)
    return lane < HEAD_DIM, lane >= HEAD_DIM


def _mem_probs(qh, kp):
    logits = _dot(qh, kp, _NT)
    logits = logits - jnp.max(logits, axis=-1, keepdims=True)
    p = jnp.exp(logits)
    return p / jnp.sum(p, axis=-1, keepdims=True)


def mem_attn_fwd(proj, mkv, cat, name):
    s = proj.shape[0]
    m = mkv.shape[0]
    tm = _rows_tile(s, 512)
    q_blk = SB_WIDTH // MEM_WIDTH

    def body(q_ref, kv_ref, _, o_ref):
        masks = _head_masks((tm, LANES))
        for pr in range(MEM_WIDTH // LANES):
            q = q_ref[:, pr * LANES:(pr + 1) * LANES] * QK_SCALE
            kp = kv_ref[:, pr * LANES:(pr + 1) * LANES]
            vp = kv_ref[:, MEM_WIDTH + pr * LANES:MEM_WIDTH + (pr + 1) * LANES]
            outs = []
            for hm in masks:
                p = _mem_probs(jnp.where(hm, q, 0.0).astype(BF16), kp)
                outs.append(_dot(p.astype(BF16), vp))
            o_ref[:, pr * LANES:(pr + 1) * LANES] = jnp.where(masks[0], outs[0], outs[1]).astype(BF16)

    return pl.pallas_call(
        body, name=name, grid=(s // tm,),
        in_specs=[pl.BlockSpec((tm, MEM_WIDTH), lambda i: (i, q_blk)), pl.BlockSpec((m, 2 * MEM_WIDTH), lambda i: (0, 0)),
                  _ANY],
        out_specs=pl.BlockSpec((tm, MEM_WIDTH), lambda i: (i, q_blk)),
        out_shape=jax.ShapeDtypeStruct(cat.shape, BF16),
        input_output_aliases={2: 0},
        compiler_params=_params("parallel"),
    )(proj, mkv, cat)


def mem_attn_bwd(proj, mkv, dcat, dproj, name):
    s = proj.shape[0]
    m = mkv.shape[0]
    tm = _rows_tile(s, 512)
    q_blk = SB_WIDTH // MEM_WIDTH

    def body(q_ref, kv_ref, do_ref, _, dq_ref, dkv_ref):
        i = pl.program_id(0)

        @pl.when(i == 0)
        def _():
            dkv_ref[...] = jnp.zeros_like(dkv_ref)

        masks = _head_masks((tm, LANES))
        for pr in range(MEM_WIDTH // LANES):
            q = q_ref[:, pr * LANES:(pr + 1) * LANES] * QK_SCALE
            do = do_ref[:, pr * LANES:(pr + 1) * LANES]
            kp = kv_ref[:, pr * LANES:(pr + 1) * LANES]
            vp = kv_ref[:, MEM_WIDTH + pr * LANES:MEM_WIDTH + (pr + 1) * LANES]
            dqs = []
            dk = jnp.zeros((m, LANES), F32)
            dv = jnp.zeros((m, LANES), F32)
            for hm in masks:
                qh = jnp.where(hm, q, 0.0).astype(BF16)
                doh = jnp.where(hm, do, 0.0).astype(BF16)
                p = _mem_probs(qh, kp)
                dp = _dot(doh, vp, _NT)
                ds = (p * (dp - jnp.sum(dp * p, axis=-1, keepdims=True))).astype(BF16)
                dqs.append(_dot(ds, kp))
                dk = dk + _dot(ds, qh, _TN)
                dv = dv + _dot(p.astype(BF16), doh, _TN)
            dq_ref[:, pr * LANES:(pr + 1) * LANES] = (jnp.where(masks[0], dqs[0], dqs[1]) * QK_SCALE).astype(BF16)
            dkv_ref[:, pr * LANES:(pr + 1) * LANES] += dk
            dkv_ref[:, MEM_WIDTH + pr * LANES:MEM_WIDTH + (pr + 1) * LANES] += dv

    return pl.pallas_call(
        body, name=name, grid=(s // tm,),
        in_specs=[pl.BlockSpec((tm, MEM_WIDTH), lambda i: (i, q_blk)), pl.BlockSpec((m, 2 * MEM_WIDTH), lambda i: (0, 0)),
                  pl.BlockSpec((tm, MEM_WIDTH), lambda i: (i, q_blk)), _ANY],
        out_specs=[pl.BlockSpec((tm, MEM_WIDTH), lambda i: (i, q_blk)), pl.BlockSpec((m, 2 * MEM_WIDTH), lambda i: (0, 0))],
        out_shape=[jax.ShapeDtypeStruct(dproj.shape, BF16), jax.ShapeDtypeStruct((m, 2 * MEM_WIDTH), F32)],
        input_output_aliases={3: 0},
        compiler_params=_params("arbitrary"),
    )(proj, mkv, dcat, dproj)


SB_ROWS = 512
SB_KEYS = 512
SB_SUB = LANES


def _sb_tri(later):
    r = lax.broadcasted_iota(jnp.int32, (2 * SB_SUB, 2 * SB_SUB), 0) % SB_SUB
    c = lax.broadcasted_iota(jnp.int32, (2 * SB_SUB, 2 * SB_SUB), 1)
    tri = (r > c) if later else (r < c)
    return (tri | (c >= SB_SUB)).astype(BF16)


def _sb_sums(v, tri, n_sub):
    out = []
    for m in range(n_sub):
        vm = v[:, m * SB_SUB:(m + 1) * SB_SUB]
        hi = vm.astype(BF16)
        lo = (vm - hi.astype(F32)).astype(BF16)
        r = _dot(jnp.concatenate([hi, lo], axis=1), tri)
        out.append((r[:, :SB_SUB], r[:, SB_SUB:]))
    return out


def _sb_tri_pair_before():
    r = lax.broadcasted_iota(jnp.int32, (2 * SB_SUB, 2 * SB_SUB), 0)
    c = lax.broadcasted_iota(jnp.int32, (2 * SB_SUB, 2 * SB_SUB), 1)
    return ((r // SB_SUB == c // SB_SUB) & (r % SB_SUB < c % SB_SUB)).astype(BF16)


def _sb_prefix_pairs(e, tri_pair, n_sub):
    out = []
    for m in range(0, n_sub, 2):
        pair = e[:, m * SB_SUB:(m + 2) * SB_SUB]
        r = _dot(pair.astype(BF16), tri_pair)
        for h in range(2):
            em = pair[:, h * SB_SUB:(h + 1) * SB_SUB]
            out.append((r[:, h * SB_SUB:(h + 1) * SB_SUB], jnp.sum(em, axis=1, keepdims=True)))
    return out


def _sb_logs(nqh, k_blk, valid):
    nz = _dot(nqh, k_blk, _NT)
    log_not = jnp.minimum(nz, 0.0) - jnp.log(1.0 + jnp.exp2(jnp.abs(nz) * -LOG2_E))
    log_beta = log_not - nz
    if valid is not None:
        log_not = jnp.where(valid, log_not, 0.0)
    return log_not, log_beta


def _sb_weights(log_not, log_beta, tri_later, carry_in, valid, n_sub):
    sums = _sb_sums(log_not, tri_later, n_sub)
    run = carry_in
    later = [None] * n_sub
    for m in reversed(range(n_sub)):
        later[m] = sums[m][0] + run
        run = run + sums[m][1]
    w = jnp.exp(log_beta + jnp.concatenate(later, axis=1))
    if valid is not None:
        w = jnp.where(valid, w, 0.0)
    return w, run


def _pad_rows(a, r0):
    return a if r0 == 0 else jnp.concatenate([jnp.zeros((r0,) + a.shape[1:], a.dtype), a], axis=0)


def _sb_diag_parts(nqh, k_blk, tri_later, n_sub):
    tq = nqh.shape[0]
    parts = []
    for m in range(n_sub):
        r0 = m * SB_SUB
        row = lax.broadcasted_iota(jnp.int32, (tq - r0, SB_SUB), 0)
        col = lax.broadcasted_iota(jnp.int32, (tq - r0, SB_SUB), 1)
        valid = (row >= SB_SUB) | (row > col)
        log_not, log_beta = _sb_logs(nqh[r0:], k_blk[r0:r0 + SB_SUB], valid)
        (inside, row_sum), = _sb_sums(log_not, tri_later, 1)
        parts.append([r0, valid, log_beta, inside, row_sum])
    run = jnp.zeros((tq, LANES), F32)
    for part in reversed(parts):
        r0, row_sum = part[0], part[4]
        part[4] = run[r0:]
        run = run + _pad_rows(row_sum, r0)
    return parts, run


def _sb_diag_fwd(nqh, k_blk, v_blk, tri_later, n_sub):
    parts, run = _sb_diag_parts(nqh, k_blk, tri_later, n_sub)
    acc = jnp.zeros((nqh.shape[0], LANES), F32)
    for r0, valid, log_beta, inside, later_blocks in parts:
        w = jnp.where(valid, jnp.exp(log_beta + inside + later_blocks), 0.0)
        acc = acc + _pad_rows(_dot(w.astype(BF16), v_blk[r0:r0 + SB_SUB]), r0)
    return run, acc


def _sb_valid(i, j, tq, tk):
    qpos = i * tq + lax.broadcasted_iota(jnp.int32, (tq, tk), 0)
    kpos = j * tk + lax.broadcasted_iota(jnp.int32, (tq, tk), 1)
    return qpos > kpos


def sb_fwd(projb, kv, name, gather=()):
    s = projb.shape[0]
    tq, tk = _rows_tile(s, SB_ROWS), _rows_tile(s, SB_KEYS)
    n_sub = tk // SB_SUB
    n_pairs = SB_WIDTH // LANES
    assert s // tk <= LANES and tk % tq == 0

    n_g = len(gather)
    n_i = s // tq

    def body(*refs):
        q_ref, k_ref, v_ref = refs[:3]
        o_ref, c_ref = refs[3 + n_g:5 + n_g]
        i = pl.program_id(1)
        if n_g:
            start, wait = _exchange_ops(refs[3:3 + n_g], refs[5 + n_g:5 + 2 * n_g], refs[5 + 2 * n_g:], 0)
            pl.when((pl.program_id(0) == 0) & (i == 0))(start)
        j_diag = (i * tq) // tk
        masks = _head_masks((tq, LANES))
        nq = q_ref[...] * (-QK_SCALE)
        tri = _sb_tri(later=True)
        lane = lax.broadcasted_iota(jnp.int32, (tq, LANES), 1)
        nqh = [jnp.where(hm, nq, 0.0).astype(BF16) for hm in masks]

        def block(j, carry, valid):
            off = pl.multiple_of(j * tk, tk)
            k_blk, v_blk = k_ref[pl.ds(off, tk), :], v_ref[pl.ds(off, tk), :]
            out = []
            for hh in range(2):
                c, acc, cmat = carry[3 * hh:3 * hh + 3]
                log_not, log_beta = _sb_logs(nqh[hh], k_blk, valid)
                w, c_next = _sb_weights(log_not, log_beta, tri, c, valid, n_sub)
                out += [c_next, acc + _dot(w.astype(BF16), v_blk), jnp.where(lane == j, c, cmat)]
            return tuple(out)

        zero = jnp.zeros((tq, LANES), F32)
        if tq == tk:
            off = pl.multiple_of(i * tk, tk)
            k_blk, v_blk = k_ref[pl.ds(off, tk), :], v_ref[pl.ds(off, tk), :]
            carry = ()
            for hh in range(2):
                carry += _sb_diag_fwd(nqh[hh], k_blk, v_blk, tri, n_sub) + (zero,)
        else:
            carry = block(j_diag, (zero,) * 6, _sb_valid(i, j_diag, tq, tk))
        def run_blocks(last, count):
            def trip(t, cr):
                for u in range(count):
                    cr = block(last - count * t - u, cr, None)
                return cr
            return trip

        n_quads, rest = j_diag // 4, j_diag % 4
        carry = lax.fori_loop(0, n_quads, run_blocks(j_diag - 1, 4), carry)
        carry = lax.fori_loop(0, rest // 2, run_blocks(rest - 1, 2), carry)
        carry = lax.fori_loop(0, rest % 2, lambda jj, cr: block(jj, cr, None), carry)
        c_ref[0] = carry[2]
        c_ref[1] = carry[5]
        o_ref[...] = jnp.where(masks[0], carry[1], carry[4]).astype(BF16)
        if n_g:
            pl.when((pl.program_id(0) == n_pairs - 1) & (i == n_i - 1))(wait)

    outs = pl.pallas_call(
        body, name=name, grid=(n_pairs, n_i),
        in_specs=[pl.BlockSpec((tq, LANES), lambda p, i: (i, p)),
                  pl.BlockSpec((s, LANES), lambda p, i: (0, p)),
                  pl.BlockSpec((s, LANES), lambda p, i: (0, n_pairs + p))] + [_ANY] * n_g,
        out_specs=[pl.BlockSpec((tq, LANES), lambda p, i: (i, p)),
                   pl.BlockSpec((2, tq, LANES), lambda p, i: (p, i, 0))] + [_ANY] * n_g,
        out_shape=[jax.ShapeDtypeStruct((s, SB_WIDTH + MEM_WIDTH), BF16),
                   jax.ShapeDtypeStruct((N_SB_HEADS, s, LANES), F32)] + _exchange_out_shape([], gather),
        scratch_shapes=_exchange_sems(n_g) if n_g else [],
        compiler_params=_params("arbitrary", "arbitrary"),
    )(projb, kv, kv, *gather)
    return outs[0], outs[1], outs[2:]


def sb_bwd(projb, kv, dcat, csave, name, scatter=()):
    s = projb.shape[0]
    tq, tk = _rows_tile(s, SB_ROWS), _rows_tile(s, SB_KEYS)
    n_sub = tk // SB_SUB
    n_pairs = SB_WIDTH // LANES
    n_x = len(scatter)
    n_i = s // tq

    def body(*refs):
        q_ref, k_ref, v_ref, do_ref, c_ref = refs[:5]
        dq_ref, dkb_ref, dvb_ref = refs[5 + n_x:8 + n_x]
        dk_ref, dv_ref = refs[8 + 2 * n_x:10 + 2 * n_x]
        i = pl.program_id(1)
        if n_x:
            start, wait = _exchange_ops(refs[5:5 + n_x], refs[8 + n_x:8 + 2 * n_x], refs[10 + 2 * n_x:], n_x)
            pl.when((pl.program_id(0) == 0) & (i == 0))(start)
        j_diag = (i * tq) // tk

        @pl.when(i == 0)
        def _():
            dk_ref[...] = jnp.zeros_like(dk_ref)
            dv_ref[...] = jnp.zeros_like(dv_ref)

        masks = _head_masks((tq, LANES))
        qs = q_ref[...] * QK_SCALE
        do = do_ref[...]
        tri_later = _sb_tri(later=True)
        tri_before = _sb_tri_pair_before()
        lane = lax.broadcasted_iota(jnp.int32, (tq, LANES), 1)
        qh = [jnp.where(hm, qs, 0.0).astype(BF16) for hm in masks]
        nqh = [jnp.where(hm, -qs, 0.0).astype(BF16) for hm in masks]
        doh = [jnp.where(hm, do, 0.0).astype(BF16) for hm in masks]

        def head_part(hh, rows, k_blk, v_blk, valid, c, e_before, dq):
            n_part = k_blk.shape[0] // SB_SUB
            log_not, log_beta = _sb_logs(nqh[hh][rows], k_blk, valid)
            w, _ = _sb_weights(log_not, log_beta, tri_later, c, valid, n_part)
            dv_h = _dot(w.astype(BF16), doh[hh][rows], _TN)
            e = w * _dot(doh[hh][rows], v_blk, _NT)
            sums = _sb_prefix_pairs(e, tri_before, n_part)
            run = e_before
            e_pre = []
            for m in range(n_part):
                e_pre.append(sums[m][0] + run)
                run = run + sums[m][1]
            dz = e - jnp.exp(log_beta) * (e + jnp.concatenate(e_pre, axis=1))
            if valid is not None:
                dz = jnp.where(valid, dz, 0.0)
            dz = dz.astype(BF16)
            return run, dq + _dot(dz, k_blk), _dot(dz, qh[hh][rows], _TN), dv_h

        def block(j, carry, valid):
            off = pl.multiple_of(j * tk, tk)
            k_blk, v_blk = k_ref[pl.ds(off, tk), :], v_ref[pl.ds(off, tk), :]
            out, dk, dv = [], None, None
            for hh in range(2):
                c = jnp.sum(jnp.where(lane == j, c_ref[hh], 0.0), axis=1, keepdims=True)
                run, dq, dk_h, dv_h = head_part(hh, slice(None), k_blk, v_blk, valid, jnp.broadcast_to(c, (tq, LANES)),
                                                carry[2 * hh], carry[2 * hh + 1])
                dk, dv = (dk_h, dv_h) if hh == 0 else (dk + dk_h, dv + dv_h)
                out += [run, dq]
            dk_ref[pl.ds(off, tk), :] += dk
            dv_ref[pl.ds(off, tk), :] += dv
            return tuple(out)

        def diagonal(carry):
            half = tq // 2
            off = pl.multiple_of(i * tk, tk)
            k_blk, v_blk = k_ref[pl.ds(off, tk), :], v_ref[pl.ds(off, tk), :]
            row_a = lax.broadcasted_iota(jnp.int32, (half, half), 0)
            col_a = lax.broadcasted_iota(jnp.int32, (half, half), 1)
            row_b = lax.broadcasted_iota(jnp.int32, (half, tk), 0) + half
            col_b = lax.broadcasted_iota(jnp.int32, (half, tk), 1)
            zero_c = jnp.zeros((half, LANES), F32)
            dqs, dk, dv = [], None, None
            for hh in range(2):
                e_before, dq = carry[2 * hh], carry[2 * hh + 1]
                _, dq_a, dk_a, dv_a = head_part(hh, slice(0, half), k_blk[:half], v_blk[:half], row_a > col_a,
                                                zero_c, e_before[:half], dq[:half])
                _, dq_b, dk_b, dv_b = head_part(hh, slice(half, tq), k_blk, v_blk, row_b > col_b,
                                                zero_c, e_before[half:], dq[half:])
                zeros_k = jnp.zeros((tk - half, LANES), F32)
                dk_h = dk_b + jnp.concatenate([dk_a, zeros_k], axis=0)
                dv_h = dv_b + jnp.concatenate([dv_a, zeros_k], axis=0)
                dk, dv = (dk_h, dv_h) if hh == 0 else (dk + dk_h, dv + dv_h)
                dqs.append(jnp.concatenate([dq_a, dq_b], axis=0))
            dk_ref[pl.ds(off, tk), :] += dk
            dv_ref[pl.ds(off, tk), :] += dv
            return dqs

        zero = jnp.zeros((tq, LANES), F32)
        def run_blocks(first, count):
            def trip(t, cr):
                for u in range(count):
                    cr = block(first + count * t + u, cr, None)
                return cr
            return trip

        carry = lax.fori_loop(0, j_diag // 2, run_blocks(0, 2), (zero,) * 4)
        carry = lax.fori_loop(0, j_diag % 2, run_blocks(j_diag - 1, 1), carry)
        if tq == tk:
            dq0, dq1 = diagonal(carry)
        else:
            carry = block(j_diag, carry, _sb_valid(i, j_diag, tq, tk))
            dq0, dq1 = carry[1], carry[3]
        dq_ref[...] = (jnp.where(masks[0], dq0, dq1) * QK_SCALE).astype(BF16)

        @pl.when(i == n_i - 1)
        def _():
            dkb_ref[...] = dk_ref[...].astype(BF16)
            dvb_ref[...] = dv_ref[...].astype(BF16)

        if n_x:
            pl.when((pl.program_id(0) == n_pairs - 1) & (i == n_i - 1))(wait)

    tile = pl.BlockSpec((tq, LANES), lambda p, i: (i, p))
    full = pl.BlockSpec((s, LANES), lambda p, i: (0, p))
    outs = pl.pallas_call(
        body, name=name, grid=(n_pairs, n_i),
        in_specs=[tile, full, pl.BlockSpec((s, LANES), lambda p, i: (0, n_pairs + p)), tile,
                  pl.BlockSpec((2, tq, LANES), lambda p, i: (p, i, 0))] + [_ANY] * n_x,
        out_specs=[tile, full, full] + [_ANY] * n_x,
        out_shape=[jax.ShapeDtypeStruct((s, SB_WIDTH + MEM_WIDTH), BF16), jax.ShapeDtypeStruct((s, SB_WIDTH), BF16),
                   jax.ShapeDtypeStruct((s, SB_WIDTH), BF16)] + _exchange_out_shape(scatter, []),
        scratch_shapes=[pltpu.VMEM((s, LANES), F32), pltpu.VMEM((s, LANES), F32)]
        + (_exchange_sems(n_x) if n_x else []),
        compiler_params=_params("arbitrary", "arbitrary"),
    )(projb, kv, kv, dcat, csave, *scatter)
    return outs[0], outs[1], outs[2], outs[3:]


def reduce_adamw(parts, w, m, v, name, after=None):
    n, r, c = parts.shape
    tr = next((t for t in (128, 176) if r % t == 0), r)
    bias1 = 1.0 - ADAM_B1 ** ADAM_STEP
    bias2 = 1.0 - ADAM_B2 ** ADAM_STEP

    def body(p_ref, w_ref, m_ref, v_ref, *rest):
        g_ref, d_ref, nm_ref, nv_ref = rest[-4:]
        g = p_ref[0].astype(F32)
        for k in range(1, n):
            g = g + p_ref[k].astype(F32)
        new_m = ADAM_B1 * m_ref[...] + (1.0 - ADAM_B1) * g
        new_v = ADAM_B2 * v_ref[...] + (1.0 - ADAM_B2) * (g * g)
        m_hat = new_m / bias1
        v_hat = new_v / bias2
        g_ref[...] = g
        d_ref[...] = -ADAM_LR * (m_hat / (jnp.sqrt(v_hat) + ADAM_EPS) + ADAM_WD * w_ref[...])
        nm_ref[...] = new_m
        nv_ref[...] = new_v

    row = pl.BlockSpec((tr, c), lambda i: (i, 0))
    return pl.pallas_call(
        body, name=name, grid=(r // tr,),
        in_specs=[pl.BlockSpec((n, tr, c), lambda i: (0, i, 0)), row, row, row] + ([] if after is None else [_ANY]),
        out_specs=[row, row, row, row],
        out_shape=[jax.ShapeDtypeStruct((r, c), F32)] * 4,
        compiler_params=_params("parallel"),
    )(parts, w, m, v, *(() if after is None else (after,)))


EARLY_0 = ("a_w_in", "a_w_mem_kv")
EARLY_1 = ("a_w_out",)
EARLY_2 = ("a_w_gu", "a_w_down")
EARLY_3 = ("w_kv", "b_w_q", "b_w_mem_kv")
EARLY = EARLY_0 + EARLY_1 + EARLY_2 + EARLY_3
LATE = ("b_w_out", "b_w_gu", "b_w_down")
BIG = EARLY + LATE
GATHER_3 = ("w_kv", "b_w_q")
GATHER_LATE = LATE + ("b_w_mem_kv",)
COL_SHARDED = ("a_w_gu", "w_kv", "b_w_gu")
SMALL_SHARDED = ("a_norm_mix", "a_scale", "a_norm_ffn")
SMALL_REPL = ("mem_norm", "kv_norm", "b_norm_mix", "b_norm_ffn", "final_norm")
WEIGHTS = ("mem_norm", "a_norm_mix", "a_w_in", "a_w_group", "a_scale", "a_w_mem_kv", "a_w_out", "a_norm_ffn", "a_w_gu",
           "a_w_down", "kv_norm", "w_kv", "b_norm_mix", "b_w_q", "b_w_mem_kv", "b_w_out", "b_norm_ffn", "b_w_gu",
           "b_w_down", "final_norm")


def _pad_row(v):
    v = v.reshape(1, -1)
    return jnp.pad(v, ((0, 0), (0, PACK_COLS - v.shape[1])))


def kernel(x, mem, mem_norm, a_norm_mix, a_w_in, a_w_group, a_scale, a_w_mem_kv, a_w_out, a_norm_ffn, a_w_gu, a_w_down, kv_norm, w_kv, b_norm_mix, b_w_q, b_w_mem_kv, b_w_out, b_norm_ffn, b_w_gu, b_w_down, final_norm, loss_target, m_mem_norm, m_a_norm_mix, m_a_w_in, m_a_w_group, m_a_scale, m_a_w_mem_kv, m_a_w_out, m_a_norm_ffn, m_a_w_gu, m_a_w_down, m_kv_norm, m_w_kv, m_b_norm_mix, m_b_w_q, m_b_w_mem_kv, m_b_w_out, m_b_norm_ffn, m_b_w_gu, m_b_w_down, m_final_norm, v_mem_norm, v_a_norm_mix, v_a_w_in, v_a_w_group, v_a_scale, v_a_w_mem_kv, v_a_w_out, v_a_norm_ffn, v_a_w_gu, v_a_w_down, v_kv_norm, v_w_kv, v_b_norm_mix, v_b_w_q, v_b_w_mem_kv, v_b_w_out, v_b_norm_ffn, v_b_w_gu, v_b_w_down, v_final_norm):
    w = dict(mem_norm=mem_norm, a_norm_mix=a_norm_mix, a_w_in=a_w_in, a_w_group=a_w_group, a_scale=a_scale,
             a_w_mem_kv=a_w_mem_kv, a_w_out=a_w_out, a_norm_ffn=a_norm_ffn, a_w_gu=a_w_gu, a_w_down=a_w_down,
             kv_norm=kv_norm, w_kv=w_kv, b_norm_mix=b_norm_mix, b_w_q=b_w_q, b_w_mem_kv=b_w_mem_kv, b_w_out=b_w_out,
             b_norm_ffn=b_norm_ffn, b_w_gu=b_w_gu, b_w_down=b_w_down, final_norm=final_norm)
    mom = dict(mem_norm=m_mem_norm, a_norm_mix=m_a_norm_mix, a_w_in=m_a_w_in, a_w_group=m_a_w_group, a_scale=m_a_scale,
               a_w_mem_kv=m_a_w_mem_kv, a_w_out=m_a_w_out, a_norm_ffn=m_a_norm_ffn, a_w_gu=m_a_w_gu,
               a_w_down=m_a_w_down, kv_norm=m_kv_norm, w_kv=m_w_kv, b_norm_mix=m_b_norm_mix, b_w_q=m_b_w_q,
               b_w_mem_kv=m_b_w_mem_kv, b_w_out=m_b_w_out, b_norm_ffn=m_b_norm_ffn, b_w_gu=m_b_w_gu,
               b_w_down=m_b_w_down, final_norm=m_final_norm)
    var = dict(mem_norm=v_mem_norm, a_norm_mix=v_a_norm_mix, a_w_in=v_a_w_in, a_w_group=v_a_w_group, a_scale=v_a_scale,
               a_w_mem_kv=v_a_w_mem_kv, a_w_out=v_a_w_out, a_norm_ffn=v_a_norm_ffn, a_w_gu=v_a_w_gu,
               a_w_down=v_a_w_down, kv_norm=v_kv_norm, w_kv=v_w_kv, b_norm_mix=v_b_norm_mix, b_w_q=v_b_w_q,
               b_w_mem_kv=v_b_w_mem_kv, b_w_out=v_b_w_out, b_norm_ffn=v_b_norm_ffn, b_w_gu=v_b_w_gu,
               b_w_down=v_b_w_down, final_norm=v_final_norm)

    me = 4 * lax.axis_index("x") + 2 * lax.axis_index("y") + lax.axis_index("c")
    shard2d = {n: w[n].shape[-2:] for n in BIG}
    shard = {n: w[n].reshape(shard2d[n]).astype(BF16) for n in BIG}
    n_sh = N_DEV // 2
    s_len = x.shape[1]

    small_send = jnp.concatenate([_pad_row(w[n]) for n in SMALL_SHARDED]
                                 + [jnp.zeros((8 - len(SMALL_SHARDED), PACK_COLS), F32)], axis=0)
    def own_slot(land, own):
        return lax.dynamic_update_slice(land, own[None], (me, 0, 0))

    def gather_done(handle, names, after, name):
        got = exchange_wait(handle, after, name)
        gathered.update({n: own_slot(g, shard[n]) for n, g in zip(names, got)})
        return got

    gathered = {}
    gather_0 = exchange_start([], [shard[n] for n in EARLY_0] + [small_send], x, "gather_0_start")
    gather_1 = exchange_start([], [shard[n] for n in EARLY_1], gather_0["token"], "gather_1_start")
    gather_2 = exchange_start([], [shard["a_w_gu"]], gather_1["token"], "gather_2_start")
    gather_2b = exchange_start([], [shard["a_w_down"]], gather_2["token"], "gather_2b_start")
    gather_3 = exchange_start([], [shard[n] for n in GATHER_3], gather_2b["token"], "gather_3_start")
    small_all = own_slot(gather_done(gather_0, EARLY_0, gather_3["token"], "gather_0_wait")[-1], small_send)
    a_norm_mix_f = small_all[:, 0, :a_norm_mix.shape[1]].reshape(-1)
    a_scale_f = small_all[:, 1, :a_scale.shape[1]].reshape(-1)
    a_norm_ffn_f = small_all[:, 2, :a_norm_ffn.shape[1]].reshape(-1)

    def rows_full(n):
        return gathered[n].reshape(-1, shard2d[n][1])

    def gu_views(tag):
        gu8 = gathered[tag + "_w_gu"]
        return gu8.reshape((2, n_sh) + gu8.shape[1:]), gu8

    def down_view(tag):
        down = gathered[tag + "_w_down"]
        return down.reshape(n_sh, -1, down.shape[2])

    wbd = jnp.zeros((SB_WIDTH, SB_WIDTH), BF16)
    for g in range(4):
        sl = slice(g * POOL_GROUP, (g + 1) * POOL_GROUP)
        wbd = wbd.at[sl, sl].set(a_w_group[0, g].astype(BF16))

    xs, mems, tgt = x[0], mem[0], loss_target[0]

    memn = rmsnorm_fwd(mems, mem_norm, "memn")
    (h0, proj), = norm_matmul(xs, [(a_norm_mix_f, rows_full("a_w_in"), F32)], "proj_a")
    mkv_a = matmul([(memn, rows_full("a_w_mem_kv"))], "nn", BF16, "mkv_a")
    pooled, cat_a = pool_fwd(proj, wbd, a_scale_f, "pool_fwd")
    cat_a = mem_attn_fwd(proj, mkv_a, cat_a, "mem_fwd_a")
    gather_done(gather_1, EARLY_1, cat_a, "gather_1_wait")
    x1 = matmul([(cat_a, rows_full("a_w_out"))], "nn", F32, "out_a", res=xs)
    h1 = rmsnorm_fwd(x1, a_norm_ffn_f, "h1")
    gather_done(gather_2, ("a_w_gu",), h1, "gather_2_wait")
    wgu_a, wgu8_a = gu_views("a")
    gu_a, act_a = ffn_up(h1, wgu_a, "ffn_up_a")
    gather_done(gather_2b, ("a_w_down",), act_a, "gather_2b_wait")
    wdown_a = down_view("a")
    x2 = matmul([(act_a, wdown_a)], "nn", F32, "down_a", res=x1, roles=("reduce", "reduce"), count=n_sh)

    gather_done(gather_3, GATHER_3, x2, "gather_3_wait")
    wkv = jnp.transpose(gathered["w_kv"], (1, 0, 2)).reshape(shard2d["w_kv"][0], -1)
    (hk, kv), (h3, projb) = norm_matmul(
        x2, [(kv_norm, wkv, BF16), (b_norm_mix[0], rows_full("b_w_q"), F32)], "kv_proj_b")
    cat_b, csave, got = sb_fwd(projb, kv, "sb_fwd", gather=[shard[n] for n in GATHER_LATE])
    gathered.update(zip(GATHER_LATE, got))
    mkv_b = matmul([(memn, rows_full("b_w_mem_kv"))], "nn", BF16, "mkv_b")
    cat_b = mem_attn_fwd(projb, mkv_b, cat_b, "mem_fwd_b")
    x3 = matmul([(cat_b, rows_full("b_w_out"))], "nn", F32, "out_b", res=x2)
    h4 = rmsnorm_fwd(x3, b_norm_ffn[0], "h4")
    (wgu_b, wgu8_b), wdown_b = gu_views("b"), down_view("b")
    gu_b, act_b = ffn_up(h4, wgu_b, "ffn_up_b")
    x4 = matmul([(act_b, wdown_b)], "nn", F32, "down_b", res=x3, roles=("reduce", "reduce"), count=n_sh)

    grads = {}
    dx4, dx4_bf, grads["final_norm"], loss_part = loss_head(x4, tgt, final_norm, "loss_head")

    def blocks(n, g):
        return g.reshape((N_DEV,) + tuple(shard2d[n]))

    def ffn_weight_grads(tag, dx_out_bf, h, gu, act, wgu8, wdown):
        fs = wgu8.shape[2]
        dgu8 = ffn_bwd_act(dx_out_bf, wdown, gu, "ffn_bwd_act_" + tag).reshape(N_DEV, s_len, fs)
        d_wdown = matmul([(act, dx_out_bf)], "tn", BF16, "dw_down_" + tag, tm=fs, tn=1024, roles=("batch", None),
                         count=n_sh)
        grads[tag + "_w_gu"] = matmul([(dgu8, h)], "tn", BF16, "dw_gu_" + tag, tm=fs, tn=1024, roles=("batch", None),
                                      count=N_DEV)
        grads[tag + "_w_down"] = blocks(tag + "_w_down", d_wdown)
        return dgu8

    def ffn_input_grads(tag, x_in, dx_out, dgu8, wgu8, g_ffn):
        dx_in, dx_in_bf, dgs = matmul_norm_bwd(x_in, dx_out, [(g_ffn, [(dgu8, wgu8)])], "norm_bwd_ffn_" + tag)
        return dx_in, dx_in_bf, dgs[0]

    def grads_start(names, small, after, name):
        return exchange_start([grads[n] for n in names], small, after, name)

    def grads_done(handle, names, after, name):
        got = exchange_wait(handle, after, name)
        for n, g in zip(names, got):
            recv[n] = lax.dynamic_update_slice(g, lax.dynamic_slice(grads[n], (me, 0, 0), (1,) + g.shape[1:]), (me, 0, 0))
        return got

    recv = {}
    dgu8_b = ffn_weight_grads("b", dx4_bf, h4, gu_b, act_b, wgu8_b, wdown_b)
    dx3, dx3_bf, grads["b_norm_ffn"] = ffn_input_grads("b", x3, dx4, dgu8_b, wgu8_b, b_norm_ffn[0])

    dcat_b = matmul([(dx3_bf, rows_full("b_w_out"))], "nt", F32, "dcat_b")
    grads["b_w_out"] = blocks("b_w_out", matmul([(cat_b, dx3_bf)], "tn", BF16, "dw_out_b"))
    dprojb, dk, dv, recv_late = sb_bwd(projb, kv, dcat_b, csave, "sb_bwd", scatter=[grads[n] for n in LATE])
    dprojb, dmkv_b = mem_attn_bwd(projb, mkv_b, dcat_b, dprojb, "mem_bwd_b")
    grads["b_w_q"] = blocks("b_w_q", matmul([(h3, dprojb)], "tn", BF16, "dw_q_b"))
    d_wkv_t = jnp.concatenate([matmul([(dk, hk)], "tn", BF16, "dw_k", tm=SB_WIDTH),
                               matmul([(dv, hk)], "tn", BF16, "dw_v", tm=SB_WIDTH)], axis=0)
    grads["w_kv"] = d_wkv_t.reshape(N_DEV, -1, d_wkv_t.shape[1])
    dx2, dx2_bf, dgs = matmul_norm_bwd(
        x2, dx3, [(b_norm_mix[0], [(dprojb, rows_full("b_w_q"))]),
                  (kv_norm, [(dk, wkv[:, :SB_WIDTH]), (dv, wkv[:, SB_WIDTH:])])], "norm_bwd_x2")
    grads["b_norm_mix"], grads["kv_norm"] = dgs
    dmkv_b_bf = dmkv_b.astype(BF16)
    grads["b_w_mem_kv"] = blocks("b_w_mem_kv", matmul([(memn, dmkv_b_bf)], "tn", BF16, "dw_mkv_b"))

    grads_3 = grads_start(EARLY_3, [], grads["b_w_mem_kv"], "grads_3_start")
    dgu8_a = ffn_weight_grads("a", dx2_bf, h1, gu_a, act_a, wgu8_a, wdown_a)
    grads_2 = grads_start(EARLY_2, [], grads_3["token"], "grads_2_start")
    dx1, dx1_bf, grads["a_norm_ffn"] = ffn_input_grads(
        "a", x1, dx2, dgu8_a, wgu8_a, a_norm_ffn_f + grads_2["token"][0, 0])

    dcat_a = matmul([(dx1_bf, rows_full("a_w_out"))], "nt", F32, "dcat_a")
    grads["a_w_out"] = blocks("a_w_out", matmul([(cat_a, dx1_bf)], "tn", BF16, "dw_out_a"))
    dproj, d_wbd, grads["a_scale"] = pool_bwd(dcat_a, pooled, wbd, a_scale_f, "pool_bwd")
    grads["a_w_group"] = jnp.stack(
        [d_wbd[g * POOL_GROUP:(g + 1) * POOL_GROUP, g * POOL_GROUP:(g + 1) * POOL_GROUP] for g in range(4)])
    group_grads = grads["a_w_group"].reshape(-1, PACK_COLS).astype(BF16)
    grads_1 = exchange_start([grads[n] for n in EARLY_1], [group_grads], grads_2["token"], "grads_1_start")
    dproj, dmkv_a = mem_attn_bwd(proj, mkv_a, dcat_a, dproj, "mem_bwd_a")
    grads["a_w_in"] = blocks("a_w_in", matmul([(h0, dproj)], "tn", BF16, "dw_in_a"))
    dmkv_a_bf = dmkv_a.astype(BF16)
    grads["a_w_mem_kv"] = blocks("a_w_mem_kv", matmul([(memn, dmkv_a_bf)], "tn", BF16, "dw_mkv_a"))
    grads_0 = grads_start(EARLY_0, [], grads_1["token"], "grads_0_start")
    dx, _, dgs = matmul_norm_bwd(
        xs, dx1, [(a_norm_mix_f + grads_0["token"][0, 0], [(dproj, rows_full("a_w_in"))])], "norm_bwd_x")
    grads["a_norm_mix"] = dgs[0]
    dmemn = matmul([(dmkv_a_bf, rows_full("a_w_mem_kv")), (dmkv_b_bf, rows_full("b_w_mem_kv"))], "nt", F32, "dmemn")
    _, _, dgs = rmsnorm_bwd(mems, jnp.zeros_like(mems), [(mem_norm, dmemn)], "norm_bwd_mem")
    grads["mem_norm"] = dgs[0]

    small_names = SMALL_REPL + SMALL_SHARDED
    head_rows = 16

    def small_pack(rows_of):
        head = jnp.zeros((head_rows, PACK_COLS), F32)
        for r, (v, col) in enumerate(rows_of):
            v = v.reshape(1, -1)
            if col is None:
                head = head + jnp.pad(v, ((r, head_rows - 1 - r), (0, PACK_COLS - v.shape[1])))
            else:
                head = lax.dynamic_update_slice(head, v, (r, col))
        return head

    small_grads = small_pack([(grads[n], None) for n in small_names] + [(loss_part[:, :1], None)])
    grads_s = exchange_start([], [small_grads], dx, "grads_s_start")

    def small_state(src):
        return small_pack([(src[n], None) for n in SMALL_REPL]
                          + [(src[n], me * src[n].shape[-1]) for n in SMALL_SHARDED])

    def state2d(src, n):
        a = src[n].reshape(shard2d[n])
        return a.T if n in COL_SHARDED else a

    def adamw(names, after):
        for n in names:
            big_out[n] = reduce_adamw(recv[n], state2d(w, n), state2d(mom, n), state2d(var, n), "adamw_" + n, after=after)
        return big_out[names[-1]][0]

    big_out = {}
    recv.update(zip(LATE, recv_late))
    done = adamw(LATE, grads_s["token"])
    grads_done(grads_3, EARLY_3, done, "grads_3_wait")
    done = adamw(EARLY_3, None)
    grads_done(grads_2, EARLY_2, done, "grads_2_wait")
    done = adamw(EARLY_2, None)
    group_recv = own_slot(grads_done(grads_1, EARLY_1, done, "grads_1_wait")[-1], group_grads)
    done = adamw(EARLY_1, None)
    group_out = reduce_adamw(group_recv, *[src["a_w_group"].reshape(-1, PACK_COLS) for src in (w, mom, var)], "adamw_group")
    grads_done(grads_0, EARLY_0, group_out[0], "grads_0_wait")
    done = adamw(EARLY_0, None)
    small_recv = own_slot(exchange_wait(grads_s, done, "grads_s_wait")[0], small_grads)
    small_out = reduce_adamw(small_recv, small_state(w), small_state(mom), small_state(var), "adamw_small")

    def unpack(kind):
        out = {n: (big_out[n][kind].T if n in COL_SHARDED else big_out[n][kind]).reshape(w[n].shape) for n in BIG}
        so = small_out[kind]
        for r, n in enumerate(SMALL_REPL):
            out[n] = so[r, :w[n].shape[-1]].reshape(w[n].shape)
        for r, n in enumerate(SMALL_SHARDED):
            width = w[n].shape[-1]
            out[n] = lax.dynamic_slice(so, (len(SMALL_REPL) + r, me * width), (1, width)).reshape(w[n].shape)
        out["a_w_group"] = group_out[kind].reshape(a_w_group.shape)
        return out

    loss = small_out[0][len(small_names), 0]
    results = [loss, dx.reshape(x.shape)]
    for kind in range(4):
        out = unpack(kind)
        results += [out[n] for n in WEIGHTS]
    return tuple(results)
```

```python
import functools

import jax
import jax.numpy as jnp
from jax import lax
from jax.experimental import pallas as pl
from jax.experimental.pallas import tpu as pltpu

F32 = jnp.float32
BF16 = jnp.bfloat16

N_DEV = 8
HEAD_DIM = 64
N_SB_HEADS = 12
SB_WIDTH = N_SB_HEADS * HEAD_DIM
MEM_WIDTH = 4 * HEAD_DIM
POOL_WINDOWS = (2, 4, 8, 16)
POOL_GROUP = SB_WIDTH // 4
POOL_HALO = 16
EPS = 1e-6
QK_SCALE = HEAD_DIM ** -0.5
LOG2_E = 1.4426950408889634
LANES = 128
PACK_COLS = 1024

ADAM_LR = 0.001
ADAM_B1 = 0.9
ADAM_B2 = 0.999
ADAM_EPS = 1e-08
ADAM_WD = 0.01
ADAM_STEP = 10

VMEM_LIMIT = 48 * 1024 * 1024

_NT = (((1,), (1,)), ((), ()))
_TN = (((0,), (0,)), ((), ()))
_NN = (((1,), (0,)), ((), ()))


def _params(*sem):
    return pltpu.CompilerParams(dimension_semantics=sem, vmem_limit_bytes=VMEM_LIMIT)


def _dot(a, b, dims=_NN):
    return lax.dot_general(a, b, dims, preferred_element_type=F32)


def exchange(a2a, ag, name):
    n_a2a, n_arr = len(a2a), len(a2a) + len(ag)

    def body(*refs):
        start, wait = _exchange_ops(refs[:n_arr], refs[n_arr:2 * n_arr], refs[2 * n_arr:], n_a2a)
        start()
        wait()

    return pl.pallas_call(
        body, name=name, out_shape=_exchange_out_shape(a2a, ag),
        in_specs=[_ANY] * n_arr, out_specs=[_ANY] * n_arr,
        scratch_shapes=_exchange_sems(n_arr),
    )(*a2a, *ag)


_ANY = pl.BlockSpec(memory_space=pl.ANY)
_HBM = pl.BlockSpec(memory_space=pltpu.HBM)
_SEM = pl.BlockSpec(memory_space=pltpu.SEMAPHORE)
_DATAFLOW = pltpu.SideEffectType.DATAFLOW_SIDE_EFFECTING


def exchange_start(a2a, ag, after, name):
    n_a2a, n_arr = len(a2a), len(a2a) + len(ag)
    arrays = [pltpu.with_memory_space_constraint(v, pltpu.HBM) for v in list(a2a) + list(ag)]
    shapes = _exchange_out_shape(a2a, ag)
    lands = [pltpu.with_memory_space_constraint(lax.empty(sh.shape, sh.dtype), pltpu.HBM) for sh in shapes]

    def body(*refs):
        ins, outs = refs[:n_arr], refs[n_arr:2 * n_arr]
        send_sems, recv_sems, token = refs[4 * n_arr + 1:4 * n_arr + 4]
        start, _ = _exchange_ops(ins, outs, (send_sems, recv_sems, None), n_a2a, local_copies=False)
        start()
        token[...] = jnp.zeros_like(token)

    outs = pl.pallas_call(
        body, name=name,
        out_shape=[pltpu.HBM(v.shape, v.dtype) for v in arrays] + [pltpu.HBM(sh.shape, sh.dtype) for sh in shapes]
        + [pltpu.SemaphoreType.DMA((n_arr * (N_DEV - 1),)), pltpu.SemaphoreType.DMA((n_arr * (N_DEV - 1),)),
           jax.ShapeDtypeStruct((8, LANES), F32)],
        in_specs=[_HBM] * (2 * n_arr) + [_ANY],
        out_specs=[_HBM] * (2 * n_arr) + [_SEM, _SEM, pl.BlockSpec(memory_space=pltpu.VMEM)],
        input_output_aliases={i: i for i in range(2 * n_arr)},
        compiler_params=pltpu.CompilerParams(has_side_effects=_DATAFLOW),
    )(*arrays, *lands, after)
    return dict(ins=outs[:n_arr], lands=outs[n_arr:2 * n_arr], sems=outs[2 * n_arr:2 * n_arr + 2], n_a2a=n_a2a,
                token=outs[2 * n_arr + 2])


def exchange_wait(handle, after, name):
    ins, lands, sems, n_a2a = handle["ins"], handle["lands"], handle["sems"], handle["n_a2a"]
    n_arr = len(ins)

    def body(*refs):
        in_refs, land_refs = refs[:n_arr], refs[n_arr:2 * n_arr]
        send_sems, recv_sems = refs[2 * n_arr:2 * n_arr + 2]
        _, wait = _exchange_ops(in_refs, land_refs, (send_sems, recv_sems, None), n_a2a, local_copies=False)
        wait()

    outs = pl.pallas_call(
        body, name=name,
        out_shape=[pltpu.HBM(v.shape, v.dtype) for v in list(ins) + list(lands)],
        in_specs=[_HBM] * (2 * n_arr) + [_SEM, _SEM, _ANY], out_specs=[_HBM] * (2 * n_arr),
        input_output_aliases={i: i for i in range(2 * n_arr)},
        compiler_params=pltpu.CompilerParams(has_side_effects=_DATAFLOW),
    )(*ins, *lands, *sems, after)
    return outs[n_arr:]


def _exchange_out_shape(a2a, ag):
    out_shape = [jax.ShapeDtypeStruct(v.shape, v.dtype) for v in a2a]
    return out_shape + [jax.ShapeDtypeStruct((N_DEV,) + v.shape, v.dtype) for v in ag]


def _exchange_sems(n_arr):
    return [pltpu.SemaphoreType.DMA((n_arr * (N_DEV - 1),)), pltpu.SemaphoreType.DMA((n_arr * (N_DEV - 1),)),
            pltpu.SemaphoreType.DMA((n_arr,))]


def _exchange_ops(ins, outs, sems, n_a2a, local_copies=True):
    send_sems, recv_sems, local_sems = sems
    n_arr = len(ins)
    x, y, c = lax.axis_index("x"), lax.axis_index("y"), lax.axis_index("c")
    me = 4 * x + 2 * y + c

    def peer_of(k):
        px = 1 - x if k & 4 else x
        py = 1 - y if k & 2 else y
        pc = 1 - c if k & 1 else c
        return (px, py, pc), 4 * px + 2 * py + pc

    def copy(a, k, landing):
        peer, pid = peer_of(k)
        src = ins[a].at[pid] if a < n_a2a else ins[a]
        return pltpu.make_async_remote_copy(
            src_ref=src, dst_ref=outs[a].at[pid if landing else me], send_sem=send_sems.at[a * (N_DEV - 1) + k - 1],
            recv_sem=recv_sems.at[a * (N_DEV - 1) + k - 1], device_id=peer, device_id_type=pl.DeviceIdType.MESH)

    def local(a):
        src = ins[a].at[me] if a < n_a2a else ins[a]
        return pltpu.make_async_copy(src, outs[a].at[me], local_sems.at[a])

    order = (6, 7, 4, 5, 2, 3, 1)

    def start():
        if local_copies:
            for a in range(n_arr):
                local(a).start()
        for a in range(n_arr):
            for k in order:
                copy(a, k, landing=False).start()

    def wait():
        for a in range(n_arr):
            for k in order:
                copy(a, k, landing=True).wait_recv()
        for a in range(n_arr):
            for k in order:
                copy(a, k, landing=False).wait_send()
        if local_copies:
            for a in range(n_arr):
                local(a).wait()

    return start, wait


def _rows_tile(n, want):
    t = min(want, n)
    assert n % t == 0, (n, t)
    return t


def rmsnorm_fwd(x, g, name):
    s, d = x.shape
    tm = _rows_tile(s, 512)

    def body(x_ref, g_ref, h_ref):
        xv = x_ref[...]
        r = lax.rsqrt(jnp.mean(xv * xv, axis=-1, keepdims=True) + EPS)
        h_ref[...] = ((xv * r) * g_ref[...]).astype(h_ref.dtype)

    return pl.pallas_call(
        body, name=name, grid=(s // tm,),
        in_specs=[pl.BlockSpec((tm, d), lambda i: (i, 0)), pl.BlockSpec((1, d), lambda i: (0, 0))],
        out_specs=pl.BlockSpec((tm, d), lambda i: (i, 0)),
        out_shape=jax.ShapeDtypeStruct((s, d), BF16),
        compiler_params=_params("parallel"),
    )(x, g.reshape(1, d))


def norm_matmul(x, heads, name):
    s, d = x.shape
    tm = _rows_tile(s, 512)
    n = len(heads)

    def body(*refs):
        x_ref = refs[0]
        xv = x_ref[...]
        xhat = xv * lax.rsqrt(jnp.mean(xv * xv, axis=-1, keepdims=True) + EPS)
        for k in range(n):
            g_ref, w_ref = refs[1 + 2 * k], refs[2 + 2 * k]
            h_ref, y_ref = refs[1 + 2 * n + 2 * k], refs[2 + 2 * n + 2 * k]
            h = (xhat * g_ref[...]).astype(BF16)
            h_ref[...] = h
            y_ref[...] = _dot(h, w_ref[...]).astype(y_ref.dtype)

    row = pl.BlockSpec((tm, d), lambda i: (i, 0))
    in_specs, out_specs, out_shape, args = [row], [], [], [x]
    for g, w_k, dtype in heads:
        in_specs += [pl.BlockSpec((1, d), lambda i: (0, 0)), pl.BlockSpec(w_k.shape, lambda i: (0, 0))]
        args += [g.reshape(1, d), w_k]
        out_specs += [row, pl.BlockSpec((tm, w_k.shape[1]), lambda i: (i, 0))]
        out_shape += [jax.ShapeDtypeStruct((s, d), BF16), jax.ShapeDtypeStruct((s, w_k.shape[1]), dtype)]
    outs = pl.pallas_call(
        body, name=name, grid=(s // tm,), in_specs=in_specs, out_specs=out_specs, out_shape=out_shape,
        compiler_params=_params("parallel"),
    )(*args)
    return [(outs[2 * k], outs[2 * k + 1]) for k in range(n)]


def rmsnorm_bwd(x, dres, pairs, name):
    s, d = x.shape
    tm = _rows_tile(s, 256)
    n = len(pairs)

    def body(*refs):
        x_ref, dres_ref = refs[0], refs[1]
        g_refs, dh_refs = refs[2:2 + n], refs[2 + n:2 + 2 * n]
        dx_ref, dxb_ref = refs[2 + 2 * n], refs[3 + 2 * n]
        dg_refs = refs[4 + 2 * n:]
        i = pl.program_id(0)
        xv = x_ref[...]
        r = lax.rsqrt(jnp.mean(xv * xv, axis=-1, keepdims=True) + EPS)
        xhat = xv * r
        acc = dres_ref[...]
        for k in range(n):
            dh = dh_refs[k][...].astype(F32)
            part = jnp.sum(dh * xhat, axis=0, keepdims=True)

            @pl.when(i == 0)
            def _(k=k, part=part):
                dg_refs[k][...] = part

            @pl.when(i > 0)
            def _(k=k, part=part):
                dg_refs[k][...] += part

            dxh = dh * g_refs[k][...]
            acc = acc + r * (dxh - xhat * jnp.mean(dxh * xhat, axis=-1, keepdims=True))
        dx_ref[...] = acc
        dxb_ref[...] = acc.astype(BF16)

    row = pl.BlockSpec((tm, d), lambda i: (i, 0))
    vec = pl.BlockSpec((1, d), lambda i: (0, 0))
    outs = pl.pallas_call(
        body, name=name, grid=(s // tm,),
        in_specs=[row, row] + [vec] * n + [row] * n,
        out_specs=[row, row] + [vec] * n,
        out_shape=[jax.ShapeDtypeStruct((s, d), F32), jax.ShapeDtypeStruct((s, d), BF16)]
        + [jax.ShapeDtypeStruct((1, d), F32)] * n,
        compiler_params=_params("arbitrary"),
    )(x, dres, *[g.reshape(1, d) for g, _ in pairs], *[dh for _, dh in pairs])
    return outs[0], outs[1], outs[2:]


def matmul_norm_bwd(x, dres, heads, name):
    s, d = x.shape
    tm = _rows_tile(s, 512)
    n = len(heads)
    n_ops = [len(pairs) for _, pairs in heads]

    def body(*refs):
        x_ref, dres_ref = refs[0], refs[1]
        n_in = 2 + n + 2 * sum(n_ops)
        dx_ref, dxb_ref = refs[n_in], refs[n_in + 1]
        dg_refs = refs[n_in + 2:]
        i = pl.program_id(0)
        xv = x_ref[...]
        r = lax.rsqrt(jnp.mean(xv * xv, axis=-1, keepdims=True) + EPS)
        xhat = xv * r
        acc = dres_ref[...]
        pos = 2 + n
        for k in range(n):
            dh = None
            for a, _ in heads[k][1]:
                a_ref, w_ref = refs[pos], refs[pos + 1]
                for kk in range(a.shape[0] if a.ndim == 3 else 1):
                    prod = _dot(a_ref[kk], w_ref[kk], _NT) if a.ndim == 3 else _dot(a_ref[...], w_ref[...], _NT)
                    dh = prod if dh is None else dh + prod
                pos += 2
            part = jnp.sum(dh * xhat, axis=0, keepdims=True)

            @pl.when(i == 0)
            def _(k=k, part=part):
                dg_refs[k][...] = part

            @pl.when(i > 0)
            def _(k=k, part=part):
                dg_refs[k][...] += part

            dxh = dh * refs[2 + k][...]
            acc = acc + r * (dxh - xhat * jnp.mean(dxh * xhat, axis=-1, keepdims=True))
        dx_ref[...] = acc
        dxb_ref[...] = acc.astype(BF16)

    row = pl.BlockSpec((tm, d), lambda i: (i, 0))
    vec = pl.BlockSpec((1, d), lambda i: (0, 0))
    in_specs, args = [row, row] + [vec] * n, [x, dres] + [g.reshape(1, d) for g, _ in heads]
    for _, pairs in heads:
        for a, w_k in pairs:
            if a.ndim == 3:
                in_specs += [pl.BlockSpec((a.shape[0], tm, a.shape[2]), lambda i: (0, i, 0)),
                             pl.BlockSpec(w_k.shape, lambda i: (0, 0, 0), pipeline_mode=pl.Buffered(1))]
            else:
                in_specs += [pl.BlockSpec((tm, a.shape[1]), lambda i: (i, 0)),
                             pl.BlockSpec(w_k.shape, lambda i: (0, 0), pipeline_mode=pl.Buffered(1))]
            args += [a, w_k]
    outs = pl.pallas_call(
        body, name=name, grid=(s // tm,), in_specs=in_specs, out_specs=[row, row] + [vec] * n,
        out_shape=[jax.ShapeDtypeStruct((s, d), F32), jax.ShapeDtypeStruct((s, d), BF16)]
        + [jax.ShapeDtypeStruct((1, d), F32)] * n,
        compiler_params=_params("arbitrary"),
    )(*args)
    return outs[0], outs[1], outs[2:]


def loss_head(x, tgt, g, name):
    s, d = x.shape
    tm = _rows_tile(s, 512)

    def body(x_ref, t_ref, g_ref, dx_ref, dxb_ref, dg_ref, loss_ref):
        i = pl.program_id(0)
        xv = x_ref[...]
        gv = g_ref[...]
        r = lax.rsqrt(jnp.mean(xv * xv, axis=-1, keepdims=True) + EPS)
        xhat = xv * r
        diff = xhat * gv - t_ref[...]
        part_loss = 0.5 * jnp.sum(jnp.mean(diff * diff, axis=-1, keepdims=True), axis=0, keepdims=True)
        dy = diff * (1.0 / d)
        part_dg = jnp.sum(dy * xhat, axis=0, keepdims=True)

        @pl.when(i == 0)
        def _():
            dg_ref[...] = part_dg
            loss_ref[...] = jnp.broadcast_to(part_loss, loss_ref.shape)

        @pl.when(i > 0)
        def _():
            dg_ref[...] += part_dg
            loss_ref[...] += jnp.broadcast_to(part_loss, loss_ref.shape)

        dxh = dy * gv
        dx = r * (dxh - xhat * jnp.mean(dxh * xhat, axis=-1, keepdims=True))
        dx_ref[...] = dx
        dxb_ref[...] = dx.astype(BF16)

    row = pl.BlockSpec((tm, d), lambda i: (i, 0))
    vec = pl.BlockSpec((1, d), lambda i: (0, 0))
    return pl.pallas_call(
        body, name=name, grid=(s // tm,),
        in_specs=[row, row, vec], out_specs=[row, row, vec, pl.BlockSpec((1, LANES), lambda i: (0, 0))],
        out_shape=[jax.ShapeDtypeStruct((s, d), F32), jax.ShapeDtypeStruct((s, d), BF16),
                   jax.ShapeDtypeStruct((1, d), F32), jax.ShapeDtypeStruct((1, LANES), F32)],
        compiler_params=_params("arbitrary"),
    )(x, tgt, g.reshape(1, d))


def matmul(pairs, mode, out_dtype, name, res=None, tm=512, tn=None, roles=(None, None), count=1):
    a0, b0 = pairs[0]
    m = a0.shape[-1] if mode == "tn" else a0.shape[-2]
    n = b0.shape[-2] if mode == "nt" else b0.shape[-1]
    if tn is None:
        tn = n if n <= 1024 else 512
    tm, tn = _rows_tile(m, tm), _rows_tile(n, tn)
    dims = {"nn": _NN, "nt": _NT, "tn": _TN}[mode]
    npairs = len(pairs)
    batched, reducing = "batch" in roles, "reduce" in roles
    assert not (batched and reducing) and (res is None or not batched)
    n_in = 2 * npairs + (res is not None)

    def body(*refs):
        o_ref = refs[n_in]
        acc = None
        for p in range(npairs):
            a_ref, b_ref = refs[2 * p], refs[2 * p + 1]
            for kk in range(count if reducing else 1):
                a = a_ref[kk] if roles[0] == "reduce" else a_ref[...]
                b = b_ref[kk] if roles[1] == "reduce" else b_ref[...]
                d = _dot(a.astype(BF16), b.astype(BF16), dims)
                acc = d if acc is None else acc + d
        if res is not None:
            acc = acc + refs[2 * npairs][...]
        o_ref[...] = acc.astype(o_ref.dtype)

    def spec(arr, role, block, index):
        if arr.ndim == 2:
            return pl.BlockSpec(block, lambda bb, i, j: index(i, j))
        assert role in ("batch", "reduce") and arr.shape[0] == count, (name, role, arr.shape)
        if role == "batch":
            return pl.BlockSpec((None,) + block, lambda bb, i, j: (bb,) + index(i, j))
        return pl.BlockSpec((count,) + block, lambda bb, i, j: (0,) + index(i, j))

    in_specs, args = [], []
    for a, b in pairs:
        if mode == "tn":
            in_specs.append(spec(a, roles[0], (a.shape[-2], tm), lambda i, j: (0, i)))
        else:
            in_specs.append(spec(a, roles[0], (tm, a.shape[-1]), lambda i, j: (i, 0)))
        if mode == "nt":
            in_specs.append(spec(b, roles[1], (tn, b.shape[-1]), lambda i, j: (j, 0)))
        else:
            in_specs.append(spec(b, roles[1], (b.shape[-2], tn), lambda i, j: (0, j)))
        args += [a, b]
    if res is not None:
        in_specs.append(pl.BlockSpec((tm, tn), lambda bb, i, j: (i, j)))
        args.append(res)
    if batched:
        out_spec = pl.BlockSpec((None, tm, tn), lambda bb, i, j: (bb, i, j))
        out_shape = jax.ShapeDtypeStruct((count, m, n), out_dtype)
    else:
        out_spec = pl.BlockSpec((tm, tn), lambda bb, i, j: (i, j))
        out_shape = jax.ShapeDtypeStruct((m, n), out_dtype)
    return pl.pallas_call(
        body, name=name, grid=(count if batched else 1, m // tm, n // tn),
        in_specs=in_specs, out_specs=out_spec, out_shape=out_shape,
        compiler_params=_params("parallel", "parallel", "parallel"),
    )(*args)


def _sigmoid(v):
    return 0.5 * jnp.tanh(0.5 * v) + 0.5


def ffn_up(h, wgu, name):
    s, d = h.shape
    _, n_sh, _, fs = wgu.shape
    tm = _rows_tile(s, 1024)

    def body(h_ref, w_ref, gu_ref, act_ref):
        hv = h_ref[...]
        gate = _dot(hv, w_ref[0])
        up = _dot(hv, w_ref[1])
        gu_ref[0] = gate.astype(BF16)
        gu_ref[1] = up.astype(BF16)
        act_ref[...] = ((gate * _sigmoid(gate)) * up).astype(BF16)

    return pl.pallas_call(
        body, name=name, grid=(n_sh, s // tm),
        in_specs=[pl.BlockSpec((tm, d), lambda j, i: (i, 0)), pl.BlockSpec((2, None, d, fs), lambda j, i: (0, j, 0, 0))],
        out_specs=[pl.BlockSpec((2, None, tm, fs), lambda j, i: (0, j, i, 0)),
                   pl.BlockSpec((None, tm, fs), lambda j, i: (j, i, 0))],
        out_shape=[jax.ShapeDtypeStruct((2, n_sh, s, fs), BF16), jax.ShapeDtypeStruct((n_sh, s, fs), BF16)],
        compiler_params=_params("parallel", "parallel"),
    )(h, wgu)


def ffn_bwd_act(dxo, wdown, gu, name):
    s, d = dxo.shape
    n_sh, fs, _ = wdown.shape
    tm = _rows_tile(s, 1024)

    def body(dx_ref, w_ref, gu_ref, dgu_ref):
        dact = _dot(dx_ref[...], w_ref[...], _NT)
        gate = gu_ref[0].astype(F32)
        sig = _sigmoid(gate)
        dgu_ref[0] = ((dact * gu_ref[1].astype(F32)) * (sig * (1.0 + gate * (1.0 - sig)))).astype(BF16)
        dgu_ref[1] = (dact * (gate * sig)).astype(BF16)

    tile = pl.BlockSpec((2, None, tm, fs), lambda j, i: (0, j, i, 0))
    return pl.pallas_call(
        body, name=name, grid=(n_sh, s // tm),
        in_specs=[pl.BlockSpec((tm, d), lambda j, i: (i, 0)), pl.BlockSpec((None, fs, d), lambda j, i: (j, 0, 0)), tile],
        out_specs=tile,
        out_shape=jax.ShapeDtypeStruct((2, n_sh, s, fs), BF16),
        compiler_params=_params("parallel", "parallel"),
    )(dxo, wdown, gu)


def _pool_select(col, by_window):
    out = by_window[3]
    for g in (2, 1, 0):
        out = jnp.where(col < (g + 1) * POOL_GROUP, by_window[g], out)
    return out


def _pool_counts(pos):
    return [jnp.minimum(pos + 1, w).astype(F32) for w in POOL_WINDOWS]


def pool_fwd(proj, wbd, scale, name):
    s = proj.shape[0]
    tm = _rows_tile(s, 512)
    nt = s // tm
    width = SB_WIDTH
    per = tm // POOL_HALO

    def body(u_ref, halo_ref, w_ref, sc_ref, pooled_ref, grouped_ref):
        i = pl.program_id(0)
        u = u_ref[...]
        halo = jnp.where(i > 0, halo_ref[...], 0.0)
        ext = jnp.concatenate([halo, u], axis=0)
        sums = []
        acc = ext
        for k in (1, 2, 4, 8):
            acc = acc + pltpu.roll(acc, k, 0)
            sums.append(acc[POOL_HALO:])
        pos = i * tm + lax.broadcasted_iota(jnp.int32, (tm, width), 0)
        col = lax.broadcasted_iota(jnp.int32, (tm, width), 1)
        means = [sm / cnt for sm, cnt in zip(sums, _pool_counts(pos))]
        pooled = (_pool_select(col, means) - u).astype(BF16)
        pooled_ref[...] = pooled
        grouped_ref[...] = (_dot(pooled, w_ref[...]) * sc_ref[...]).astype(BF16)

    row = pl.BlockSpec((tm, width), lambda i: (i, 0))
    return pl.pallas_call(
        body, name=name, grid=(nt,),
        in_specs=[row, pl.BlockSpec((POOL_HALO, width), lambda i: (jnp.maximum(i * per - 1, 0), 0)),
                  pl.BlockSpec((width, width), lambda i: (0, 0)), pl.BlockSpec((1, width), lambda i: (0, 0))],
        out_specs=[row, row],
        out_shape=[jax.ShapeDtypeStruct((s, width), BF16), jax.ShapeDtypeStruct((s, width + MEM_WIDTH), BF16)],
        compiler_params=_params("parallel"),
    )(proj, proj, wbd, scale.reshape(1, width))


def pool_bwd(dcat, pooled, wbd, scale, name):
    s = dcat.shape[0]
    tm = _rows_tile(s, 512)
    nt = s // tm
    width = SB_WIDTH
    per = tm // POOL_HALO
    n_ext = tm + POOL_HALO

    def body(dg_ref, halo_ref, pooled_ref, w_ref, sc_ref, du_ref, dw_ref, dsc_ref):
        i = pl.program_id(0)
        w = w_ref[...]
        sc = sc_ref[...]
        dg = dg_ref[...]
        pooled = pooled_ref[...]
        pg = _dot(pooled, w)
        dpg = (dg * sc).astype(BF16)
        part_sc = jnp.sum(dg * pg, axis=0, keepdims=True)
        part_w = _dot(pooled, dpg, _TN)

        @pl.when(i == 0)
        def _():
            dsc_ref[...] = part_sc
            dw_ref[...] = part_w

        @pl.when(i > 0)
        def _():
            dsc_ref[...] += part_sc
            dw_ref[...] += part_w

        dpooled = _dot(dpg, w, _NT)
        halo_dpg = (jnp.where(i < nt - 1, halo_ref[...], 0.0) * sc).astype(BF16)
        ext = jnp.concatenate([dpooled, _dot(halo_dpg, w, _NT)], axis=0)
        pos = i * tm + lax.broadcasted_iota(jnp.int32, (n_ext, width), 0)
        col = lax.broadcasted_iota(jnp.int32, (tm, width), 1)
        outs = []
        for k_idx, cnt in enumerate(_pool_counts(pos)):
            acc = ext / cnt
            for k in (1, 2, 4, 8)[:k_idx + 1]:
                acc = acc + pltpu.roll(acc, n_ext - k, 0)
            outs.append(acc[:tm])
        du_ref[...] = (_pool_select(col, outs) - dpooled).astype(BF16)

    row = pl.BlockSpec((tm, width), lambda i: (i, 0))
    return pl.pallas_call(
        body, name=name, grid=(nt,),
        in_specs=[row, pl.BlockSpec((POOL_HALO, width), lambda i: (jnp.minimum((i + 1) * per, s // POOL_HALO - 1), 0)),
                  row, pl.BlockSpec((width, width), lambda i: (0, 0)), pl.BlockSpec((1, width), lambda i: (0, 0))],
        out_specs=[row, pl.BlockSpec((width, width), lambda i: (0, 0)), pl.BlockSpec((1, width), lambda i: (0, 0))],
        out_shape=[jax.ShapeDtypeStruct((s, width + MEM_WIDTH), BF16), jax.ShapeDtypeStruct((width, width), F32),
                   jax.ShapeDtypeStruct((1, width), F32)],
        compiler_params=_params("arbitrary"),
    )(dcat, dcat, pooled, wbd, scale.reshape(1, width))


def _head_masks(shape):
    lane = lax.broadcasted_iota(jnp.int32, shape, 1)
    return lane < HEAD_DIM, lane >= HEAD_DIM


def _mem_probs(qh, kp):
    logits = _dot(qh, kp, _NT)
    logits = logits - jnp.max(logits, axis=-1, keepdims=True)
    p = jnp.exp(logits)
    return p / jnp.sum(p, axis=-1, keepdims=True)


def mem_attn_fwd(proj, mkv, cat, name):
    s = proj.shape[0]
    m = mkv.shape[0]
    tm = _rows_tile(s, 512)
    q_blk = SB_WIDTH // MEM_WIDTH

    def body(q_ref, kv_ref, _, o_ref):
        masks = _head_masks((tm, LANES))
        for pr in range(MEM_WIDTH // LANES):
            q = q_ref[:, pr * LANES:(pr + 1) * LANES] * QK_SCALE
            kp = kv_ref[:, pr * LANES:(pr + 1) * LANES]
            vp = kv_ref[:, MEM_WIDTH + pr * LANES:MEM_WIDTH + (pr + 1) * LANES]
            outs = []
            for hm in masks:
                p = _mem_probs(jnp.where(hm, q, 0.0).astype(BF16), kp)
                outs.append(_dot(p.astype(BF16), vp))
            o_ref[:, pr * LANES:(pr + 1) * LANES] = jnp.where(masks[0], outs[0], outs[1]).astype(BF16)

    return pl.pallas_call(
        body, name=name, grid=(s // tm,),
        in_specs=[pl.BlockSpec((tm, MEM_WIDTH), lambda i: (i, q_blk)), pl.BlockSpec((m, 2 * MEM_WIDTH), lambda i: (0, 0)),
                  _ANY],
        out_specs=pl.BlockSpec((tm, MEM_WIDTH), lambda i: (i, q_blk)),
        out_shape=jax.ShapeDtypeStruct(cat.shape, BF16),
        input_output_aliases={2: 0},
        compiler_params=_params("parallel"),
    )(proj, mkv, cat)


def mem_attn_bwd(proj, mkv, dcat, dproj, name):
    s = proj.shape[0]
    m = mkv.shape[0]
    tm = _rows_tile(s, 512)
    q_blk = SB_WIDTH // MEM_WIDTH

    def body(q_ref, kv_ref, do_ref, _, dq_ref, dkv_ref):
        i = pl.program_id(0)

        @pl.when(i == 0)
        def _():
            dkv_ref[...] = jnp.zeros_like(dkv_ref)

        masks = _head_masks((tm, LANES))
        for pr in range(MEM_WIDTH // LANES):
            q = q_ref[:, pr * LANES:(pr + 1) * LANES] * QK_SCALE
            do = do_ref[:, pr * LANES:(pr + 1) * LANES]
            kp = kv_ref[:, pr * LANES:(pr + 1) * LANES]
            vp = kv_ref[:, MEM_WIDTH + pr * LANES:MEM_WIDTH + (pr + 1) * LANES]
            dqs = []
            dk = jnp.zeros((m, LANES), F32)
            dv = jnp.zeros((m, LANES), F32)
            for hm in masks:
                qh = jnp.where(hm, q, 0.0).astype(BF16)
                doh = jnp.where(hm, do, 0.0).astype(BF16)
                p = _mem_probs(qh, kp)
                dp = _dot(doh, vp, _NT)
                ds = (p * (dp - jnp.sum(dp * p, axis=-1, keepdims=True))).astype(BF16)
                dqs.append(_dot(ds, kp))
                dk = dk + _dot(ds, qh, _TN)
                dv = dv + _dot(p.astype(BF16), doh, _TN)
            dq_ref[:, pr * LANES:(pr + 1) * LANES] = (jnp.where(masks[0], dqs[0], dqs[1]) * QK_SCALE).astype(BF16)
            dkv_ref[:, pr * LANES:(pr + 1) * LANES] += dk
            dkv_ref[:, MEM_WIDTH + pr * LANES:MEM_WIDTH + (pr + 1) * LANES] += dv

    return pl.pallas_call(
        body, name=name, grid=(s // tm,),
        in_specs=[pl.BlockSpec((tm, MEM_WIDTH), lambda i: (i, q_blk)), pl.BlockSpec((m, 2 * MEM_WIDTH), lambda i: (0, 0)),
                  pl.BlockSpec((tm, MEM_WIDTH), lambda i: (i, q_blk)), _ANY],
        out_specs=[pl.BlockSpec((tm, MEM_WIDTH), lambda i: (i, q_blk)), pl.BlockSpec((m, 2 * MEM_WIDTH), lambda i: (0, 0))],
        out_shape=[jax.ShapeDtypeStruct(dproj.shape, BF16), jax.ShapeDtypeStruct((m, 2 * MEM_WIDTH), F32)],
        input_output_aliases={3: 0},
        compiler_params=_params("arbitrary"),
    )(proj, mkv, dcat, dproj)


SB_ROWS = 512
SB_KEYS = 512
SB_SUB = LANES


def _sb_tri(later):
    r = lax.broadcasted_iota(jnp.int32, (2 * SB_SUB, 2 * SB_SUB), 0) % SB_SUB
    c = lax.broadcasted_iota(jnp.int32, (2 * SB_SUB, 2 * SB_SUB), 1)
    tri = (r > c) if later else (r < c)
    return (tri | (c >= SB_SUB)).astype(BF16)


def _sb_sums(v, tri, n_sub):
    out = []
    for m in range(n_sub):
        vm = v[:, m * SB_SUB:(m + 1) * SB_SUB]
        hi = vm.astype(BF16)
        lo = (vm - hi.astype(F32)).astype(BF16)
        r = _dot(jnp.concatenate([hi, lo], axis=1), tri)
        out.append((r[:, :SB_SUB], r[:, SB_SUB:]))
    return out


def _sb_tri_pair_before():
    r = lax.broadcasted_iota(jnp.int32, (2 * SB_SUB, 2 * SB_SUB), 0)
    c = lax.broadcasted_iota(jnp.int32, (2 * SB_SUB, 2 * SB_SUB), 1)
    return ((r // SB_SUB == c // SB_SUB) & (r % SB_SUB < c % SB_SUB)).astype(BF16)


def _sb_prefix_pairs(e, tri_pair, n_sub):
    out = []
    for m in range(0, n_sub, 2):
        pair = e[:, m * SB_SUB:(m + 2) * SB_SUB]
        r = _dot(pair.astype(BF16), tri_pair)
        for h in range(2):
            em = pair[:, h * SB_SUB:(h + 1) * SB_SUB]
            out.append((r[:, h * SB_SUB:(h + 1) * SB_SUB], jnp.sum(em, axis=1, keepdims=True)))
    return out


def _sb_logs(nqh, k_blk, valid):
    nz = _dot(nqh, k_blk, _NT)
    log_not = jnp.minimum(nz, 0.0) - jnp.log(1.0 + jnp.exp2(jnp.abs(nz) * -LOG2_E))
    log_beta = log_not - nz
    if valid is not None:
        log_not = jnp.where(valid, log_not, 0.0)
    return log_not, log_beta


def _sb_weights(log_not, log_beta, tri_later, carry_in, valid, n_sub):
    sums = _sb_sums(log_not, tri_later, n_sub)
    run = carry_in
    later = [None] * n_sub
    for m in reversed(range(n_sub)):
        later[m] = sums[m][0] + run
        run = run + sums[m][1]
    w = jnp.exp(log_beta + jnp.concatenate(later, axis=1))
    if valid is not None:
        w = jnp.where(valid, w, 0.0)
    return w, run


def _pad_rows(a, r0):
    return a if r0 == 0 else jnp.concatenate([jnp.zeros((r0,) + a.shape[1:], a.dtype), a], axis=0)


def _sb_diag_parts(nqh, k_blk, tri_later, n_sub):
    tq = nqh.shape[0]
    parts = []
    for m in range(n_sub):
        r0 = m * SB_SUB
        row = lax.broadcasted_iota(jnp.int32, (tq - r0, SB_SUB), 0)
        col = lax.broadcasted_iota(jnp.int32, (tq - r0, SB_SUB), 1)
        valid = (row >= SB_SUB) | (row > col)
        log_not, log_beta = _sb_logs(nqh[r0:], k_blk[r0:r0 + SB_SUB], valid)
        (inside, row_sum), = _sb_sums(log_not, tri_later, 1)
        parts.append([r0, valid, log_beta, inside, row_sum])
    run = jnp.zeros((tq, LANES), F32)
    for part in reversed(parts):
        r0, row_sum = part[0], part[4]
        part[4] = run[r0:]
        run = run + _pad_rows(row_sum, r0)
    return parts, run


def _sb_diag_fwd(nqh, k_blk, v_blk, tri_later, n_sub):
    parts, run = _sb_diag_parts(nqh, k_blk, tri_later, n_sub)
    acc = jnp.zeros((nqh.shape[0], LANES), F32)
    for r0, valid, log_beta, inside, later_blocks in parts:
        w = jnp.where(valid, jnp.exp(log_beta + inside + later_blocks), 0.0)
        acc = acc + _pad_rows(_dot(w.astype(BF16), v_blk[r0:r0 + SB_SUB]), r0)
    return run, acc


def _sb_valid(i, j, tq, tk):
    qpos = i * tq + lax.broadcasted_iota(jnp.int32, (tq, tk), 0)
    kpos = j * tk + lax.broadcasted_iota(jnp.int32, (tq, tk), 1)
    return qpos > kpos


def sb_fwd(projb, kv, name, gather=()):
    s = projb.shape[0]
    tq, tk = _rows_tile(s, SB_ROWS), _rows_tile(s, SB_KEYS)
    n_sub = tk // SB_SUB
    n_pairs = SB_WIDTH // LANES
    assert s // tk <= LANES and tk % tq == 0

    n_g = len(gather)
    n_i = s // tq

    def body(*refs):
        q_ref, k_ref, v_ref = refs[:3]
        o_ref, c_ref = refs[3 + n_g:5 + n_g]
        i = pl.program_id(1)
        if n_g:
            start, wait = _exchange_ops(refs[3:3 + n_g], refs[5 + n_g:5 + 2 * n_g], refs[5 + 2 * n_g:], 0)
            pl.when((pl.program_id(0) == 0) & (i == 0))(start)
        j_diag = (i * tq) // tk
        masks = _head_masks((tq, LANES))
        nq = q_ref[...] * (-QK_SCALE)
        tri = _sb_tri(later=True)
        lane = lax.broadcasted_iota(jnp.int32, (tq, LANES), 1)
        nqh = [jnp.where(hm, nq, 0.0).astype(BF16) for hm in masks]

        def block(j, carry, valid):
            off = pl.multiple_of(j * tk, tk)
            k_blk, v_blk = k_ref[pl.ds(off, tk), :], v_ref[pl.ds(off, tk), :]
            out = []
            for hh in range(2):
                c, acc, cmat = carry[3 * hh:3 * hh + 3]
                log_not, log_beta = _sb_logs(nqh[hh], k_blk, valid)
                w, c_next = _sb_weights(log_not, log_beta, tri, c, valid, n_sub)
                out += [c_next, acc + _dot(w.astype(BF16), v_blk), jnp.where(lane == j, c, cmat)]
            return tuple(out)

        zero = jnp.zeros((tq, LANES), F32)
        if tq == tk:
            off = pl.multiple_of(i * tk, tk)
            k_blk, v_blk = k_ref[pl.ds(off, tk), :], v_ref[pl.ds(off, tk), :]
            carry = ()
            for hh in range(2):
                carry += _sb_diag_fwd(nqh[hh], k_blk, v_blk, tri, n_sub) + (zero,)
        else:
            carry = block(j_diag, (zero,) * 6, _sb_valid(i, j_diag, tq, tk))
        def run_blocks(last, count):
            def trip(t, cr):
                for u in range(count):
                    cr = block(last - count * t - u, cr, None)
                return cr
            return trip

        n_quads, rest = j_diag // 4, j_diag % 4
        carry = lax.fori_loop(0, n_quads, run_blocks(j_diag - 1, 4), carry)
        carry = lax.fori_loop(0, rest // 2, run_blocks(rest - 1, 2), carry)
        carry = lax.fori_loop(0, rest % 2, lambda jj, cr: block(jj, cr, None), carry)
        c_ref[0] = carry[2]
        c_ref[1] = carry[5]
        o_ref[...] = jnp.where(masks[0], carry[1], carry[4]).astype(BF16)
        if n_g:
            pl.when((pl.program_id(0) == n_pairs - 1) & (i == n_i - 1))(wait)

    outs = pl.pallas_call(
        body, name=name, grid=(n_pairs, n_i),
        in_specs=[pl.BlockSpec((tq, LANES), lambda p, i: (i, p)),
                  pl.BlockSpec((s, LANES), lambda p, i: (0, p)),
                  pl.BlockSpec((s, LANES), lambda p, i: (0, n_pairs + p))] + [_ANY] * n_g,
        out_specs=[pl.BlockSpec((tq, LANES), lambda p, i: (i, p)),
                   pl.BlockSpec((2, tq, LANES), lambda p, i: (p, i, 0))] + [_ANY] * n_g,
        out_shape=[jax.ShapeDtypeStruct((s, SB_WIDTH + MEM_WIDTH), BF16),
                   jax.ShapeDtypeStruct((N_SB_HEADS, s, LANES), F32)] + _exchange_out_shape([], gather),
        scratch_shapes=_exchange_sems(n_g) if n_g else [],
        compiler_params=_params("arbitrary", "arbitrary"),
    )(projb, kv, kv, *gather)
    return outs[0], outs[1], outs[2:]


def sb_bwd(projb, kv, dcat, csave, name, scatter=()):
    s = projb.shape[0]
    tq, tk = _rows_tile(s, SB_ROWS), _rows_tile(s, SB_KEYS)
    n_sub = tk // SB_SUB
    n_pairs = SB_WIDTH // LANES
    n_x = len(scatter)
    n_i = s // tq

    def body(*refs):
        q_ref, k_ref, v_ref, do_ref, c_ref = refs[:5]
        dq_ref, dkb_ref, dvb_ref = refs[5 + n_x:8 + n_x]
        dk_ref, dv_ref = refs[8 + 2 * n_x:10 + 2 * n_x]
        i = pl.program_id(1)
        if n_x:
            start, wait = _exchange_ops(refs[5:5 + n_x], refs[8 + n_x:8 + 2 * n_x], refs[10 + 2 * n_x:], n_x)
            pl.when((pl.program_id(0) == 0) & (i == 0))(start)
        j_diag = (i * tq) // tk

        @pl.when(i == 0)
        def _():
            dk_ref[...] = jnp.zeros_like(dk_ref)
            dv_ref[...] = jnp.zeros_like(dv_ref)

        masks = _head_masks((tq, LANES))
        qs = q_ref[...] * QK_SCALE
        do = do_ref[...]
        tri_later = _sb_tri(later=True)
        tri_before = _sb_tri_pair_before()
        lane = lax.broadcasted_iota(jnp.int32, (tq, LANES), 1)
        qh = [jnp.where(hm, qs, 0.0).astype(BF16) for hm in masks]
        nqh = [jnp.where(hm, -qs, 0.0).astype(BF16) for hm in masks]
        doh = [jnp.where(hm, do, 0.0).astype(BF16) for hm in masks]

        def head_part(hh, rows, k_blk, v_blk, valid, c, e_before, dq):
            n_part = k_blk.shape[0] // SB_SUB
            log_not, log_beta = _sb_logs(nqh[hh][rows], k_blk, valid)
            w, _ = _sb_weights(log_not, log_beta, tri_later, c, valid, n_part)
            dv_h = _dot(w.astype(BF16), doh[hh][rows], _TN)
            e = w * _dot(doh[hh][rows], v_blk, _NT)
            sums = _sb_prefix_pairs(e, tri_before, n_part)
            run = e_before
            e_pre = []
            for m in range(n_part):
                e_pre.append(sums[m][0] + run)
                run = run + sums[m][1]
            dz = e - jnp.exp(log_beta) * (e + jnp.concatenate(e_pre, axis=1))
            if valid is not None:
                dz = jnp.where(valid, dz, 0.0)
            dz = dz.astype(BF16)
            return run, dq + _dot(dz, k_blk), _dot(dz, qh[hh][rows], _TN), dv_h

        def block(j, carry, valid):
            off = pl.multiple_of(j * tk, tk)
            k_blk, v_blk = k_ref[pl.ds(off, tk), :], v_ref[pl.ds(off, tk), :]
            out, dk, dv = [], None, None
            for hh in range(2):
                c = jnp.sum(jnp.where(lane == j, c_ref[hh], 0.0), axis=1, keepdims=True)
                run, dq, dk_h, dv_h = head_part(hh, slice(None), k_blk, v_blk, valid, jnp.broadcast_to(c, (tq, LANES)),
                                                carry[2 * hh], carry[2 * hh + 1])
                dk, dv = (dk_h, dv_h) if hh == 0 else (dk + dk_h, dv + dv_h)
                out += [run, dq]
            dk_ref[pl.ds(off, tk), :] += dk
            dv_ref[pl.ds(off, tk), :] += dv
            return tuple(out)

        def diagonal(carry):
            half = tq // 2
            off = pl.multiple_of(i * tk, tk)
            k_blk, v_blk = k_ref[pl.ds(off, tk), :], v_ref[pl.ds(off, tk), :]
            row_a = lax.broadcasted_iota(jnp.int32, (half, half), 0)
            col_a = lax.broadcasted_iota(jnp.int32, (half, half), 1)
            row_b = lax.broadcasted_iota(jnp.int32, (half, tk), 0) + half
            col_b = lax.broadcasted_iota(jnp.int32, (half, tk), 1)
            zero_c = jnp.zeros((half, LANES), F32)
            dqs, dk, dv = [], None, None
            for hh in range(2):
                e_before, dq = carry[2 * hh], carry[2 * hh + 1]
                _, dq_a, dk_a, dv_a = head_part(hh, slice(0, half), k_blk[:half], v_blk[:half], row_a > col_a,
                                                zero_c, e_before[:half], dq[:half])
                _, dq_b, dk_b, dv_b = head_part(hh, slice(half, tq), k_blk, v_blk, row_b > col_b,
                                                zero_c, e_before[half:], dq[half:])
                zeros_k = jnp.zeros((tk - half, LANES), F32)
                dk_h = dk_b + jnp.concatenate([dk_a, zeros_k], axis=0)
                dv_h = dv_b + jnp.concatenate([dv_a, zeros_k], axis=0)
                dk, dv = (dk_h, dv_h) if hh == 0 else (dk + dk_h, dv + dv_h)
                dqs.append(jnp.concatenate([dq_a, dq_b], axis=0))
            dk_ref[pl.ds(off, tk), :] += dk
            dv_ref[pl.ds(off, tk), :] += dv
            return dqs

        zero = jnp.zeros((tq, LANES), F32)
        def run_blocks(first, count):
            def trip(t, cr):
                for u in range(count):
                    cr = block(first + count * t + u, cr, None)
                return cr
            return trip

        carry = lax.fori_loop(0, j_diag // 2, run_blocks(0, 2), (zero,) * 4)
        carry = lax.fori_loop(0, j_diag % 2, run_blocks(j_diag - 1, 1), carry)
        if tq == tk:
            dq0, dq1 = diagonal(carry)
        else:
            carry = block(j_diag, carry, _sb_valid(i, j_diag, tq, tk))
            dq0, dq1 = carry[1], carry[3]
        dq_ref[...] = (jnp.where(masks[0], dq0, dq1) * QK_SCALE).astype(BF16)

        @pl.when(i == n_i - 1)
        def _():
            dkb_ref[...] = dk_ref[...].astype(BF16)
            dvb_ref[...] = dv_ref[...].astype(BF16)

        if n_x:
            pl.when((pl.program_id(0) == n_pairs - 1) & (i == n_i - 1))(wait)

    tile = pl.BlockSpec((tq, LANES), lambda p, i: (i, p))
    full = pl.BlockSpec((s, LANES), lambda p, i: (0, p))
    outs = pl.pallas_call(
        body, name=name, grid=(n_pairs, n_i),
        in_specs=[tile, full, pl.BlockSpec((s, LANES), lambda p, i: (0, n_pairs + p)), tile,
                  pl.BlockSpec((2, tq, LANES), lambda p, i: (p, i, 0))] + [_ANY] * n_x,
        out_specs=[tile, full, full] + [_ANY] * n_x,
        out_shape=[jax.ShapeDtypeStruct((s, SB_WIDTH + MEM_WIDTH), BF16), jax.ShapeDtypeStruct((s, SB_WIDTH), BF16),
                   jax.ShapeDtypeStruct((s, SB_WIDTH), BF16)] + _exchange_out_shape(scatter, []),
        scratch_shapes=[pltpu.VMEM((s, LANES), F32), pltpu.VMEM((s, LANES), F32)]
        + (_exchange_sems(n_x) if n_x else []),
        compiler_params=_params("arbitrary", "arbitrary"),
    )(projb, kv, kv, dcat, csave, *scatter)
    return outs[0], outs[1], outs[2], outs[3:]


def reduce_adamw(parts, w, m, v, name, after=None):
    n, r, c = parts.shape
    tr = next((t for t in (128, 176) if r % t == 0), r)
    bias1 = 1.0 - ADAM_B1 ** ADAM_STEP
    bias2 = 1.0 - ADAM_B2 ** ADAM_STEP

    def body(p_ref, w_ref, m_ref, v_ref, *rest):
        g_ref, d_ref, nm_ref, nv_ref = rest[-4:]
        g = p_ref[0].astype(F32)
        for k in range(1, n):
            g = g + p_ref[k].astype(F32)
        new_m = ADAM_B1 * m_ref[...] + (1.0 - ADAM_B1) * g
        new_v = ADAM_B2 * v_ref[...] + (1.0 - ADAM_B2) * (g * g)
        m_hat = new_m / bias1
        v_hat = new_v / bias2
        g_ref[...] = g
        d_ref[...] = -ADAM_LR * (m_hat / (jnp.sqrt(v_hat) + ADAM_EPS) + ADAM_WD * w_ref[...])
        nm_ref[...] = new_m
        nv_ref[...] = new_v

    row = pl.BlockSpec((tr, c), lambda i: (i, 0))
    return pl.pallas_call(
        body, name=name, grid=(r // tr,),
        in_specs=[pl.BlockSpec((n, tr, c), lambda i: (0, i, 0)), row, row, row] + ([] if after is None else [_ANY]),
        out_specs=[row, row, row, row],
        out_shape=[jax.ShapeDtypeStruct((r, c), F32)] * 4,
        compiler_params=_params("parallel"),
    )(parts, w, m, v, *(() if after is None else (after,)))


EARLY_0 = ("a_w_in", "a_w_mem_kv")
EARLY_1 = ("a_w_out",)
EARLY_2 = ("a_w_gu", "a_w_down")
EARLY_3 = ("w_kv", "b_w_q", "b_w_mem_kv")
EARLY = EARLY_0 + EARLY_1 + EARLY_2 + EARLY_3
LATE = ("b_w_out", "b_w_gu", "b_w_down")
BIG = EARLY + LATE
GATHER_3 = ("w_kv", "b_w_q")
GATHER_LATE = LATE + ("b_w_mem_kv",)
COL_SHARDED = ("a_w_gu", "w_kv", "b_w_gu")
SMALL_SHARDED = ("a_norm_mix", "a_scale", "a_norm_ffn")
SMALL_REPL = ("mem_norm", "kv_norm", "b_norm_mix", "b_norm_ffn", "final_norm")
WEIGHTS = ("mem_norm", "a_norm_mix", "a_w_in", "a_w_group", "a_scale", "a_w_mem_kv", "a_w_out", "a_norm_ffn", "a_w_gu",
           "a_w_down", "kv_norm", "w_kv", "b_norm_mix", "b_w_q", "b_w_mem_kv", "b_w_out", "b_norm_ffn", "b_w_gu",
           "b_w_down", "final_norm")


def _pad_row(v):
    v = v.reshape(1, -1)
    return jnp.pad(v, ((0, 0), (0, PACK_COLS - v.shape[1])))


def kernel(x, mem, mem_norm, a_norm_mix, a_w_in, a_w_group, a_scale, a_w_mem_kv, a_w_out, a_norm_ffn, a_w_gu, a_w_down, kv_norm, w_kv, b_norm_mix, b_w_q, b_w_mem_kv, b_w_out, b_norm_ffn, b_w_gu, b_w_down, final_norm, loss_target, m_mem_norm, m_a_norm_mix, m_a_w_in, m_a_w_group, m_a_scale, m_a_w_mem_kv, m_a_w_out, m_a_norm_ffn, m_a_w_gu, m_a_w_down, m_kv_norm, m_w_kv, m_b_norm_mix, m_b_w_q, m_b_w_mem_kv, m_b_w_out, m_b_norm_ffn, m_b_w_gu, m_b_w_down, m_final_norm, v_mem_norm, v_a_norm_mix, v_a_w_in, v_a_w_group, v_a_scale, v_a_w_mem_kv, v_a_w_out, v_a_norm_ffn, v_a_w_gu, v_a_w_down, v_kv_norm, v_w_kv, v_b_norm_mix, v_b_w_q, v_b_w_mem_kv, v_b_w_out, v_b_norm_ffn, v_b_w_gu, v_b_w_down, v_final_norm):
    w = dict(mem_norm=mem_norm, a_norm_mix=a_norm_mix, a_w_in=a_w_in, a_w_group=a_w_group, a_scale=a_scale,
             a_w_mem_kv=a_w_mem_kv, a_w_out=a_w_out, a_norm_ffn=a_norm_ffn, a_w_gu=a_w_gu, a_w_down=a_w_down,
             kv_norm=kv_norm, w_kv=w_kv, b_norm_mix=b_norm_mix, b_w_q=b_w_q, b_w_mem_kv=b_w_mem_kv, b_w_out=b_w_out,
             b_norm_ffn=b_norm_ffn, b_w_gu=b_w_gu, b_w_down=b_w_down, final_norm=final_norm)
    mom = dict(mem_norm=m_mem_norm, a_norm_mix=m_a_norm_mix, a_w_in=m_a_w_in, a_w_group=m_a_w_group, a_scale=m_a_scale,
               a_w_mem_kv=m_a_w_mem_kv, a_w_out=m_a_w_out, a_norm_ffn=m_a_norm_ffn, a_w_gu=m_a_w_gu,
               a_w_down=m_a_w_down, kv_norm=m_kv_norm, w_kv=m_w_kv, b_norm_mix=m_b_norm_mix, b_w_q=m_b_w_q,
               b_w_mem_kv=m_b_w_mem_kv, b_w_out=m_b_w_out, b_norm_ffn=m_b_norm_ffn, b_w_gu=m_b_w_gu,
               b_w_down=m_b_w_down, final_norm=m_final_norm)
    var = dict(mem_norm=v_mem_norm, a_norm_mix=v_a_norm_mix, a_w_in=v_a_w_in, a_w_group=v_a_w_group, a_scale=v_a_scale,
               a_w_mem_kv=v_a_w_mem_kv, a_w_out=v_a_w_out, a_norm_ffn=v_a_norm_ffn, a_w_gu=v_a_w_gu,
               a_w_down=v_a_w_down, kv_norm=v_kv_norm, w_kv=v_w_kv, b_norm_mix=v_b_norm_mix, b_w_q=v_b_w_q,
               b_w_mem_kv=v_b_w_mem_kv, b_w_out=v_b_w_out, b_norm_ffn=v_b_norm_ffn, b_w_gu=v_b_w_gu,
               b_w_down=v_b_w_down, final_norm=v_final_norm)

    me = 4 * lax.axis_index("x") + 2 * lax.axis_index("y") + lax.axis_index("c")
    shard2d = {n: w[n].shape[-2:] for n in BIG}
    shard = {n: w[n].reshape(shard2d[n]).astype(BF16) for n in BIG}
    n_sh = N_DEV // 2
    s_len = x.shape[1]

    small_send = jnp.concatenate([_pad_row(w[n]) for n in SMALL_SHARDED]
                                 + [jnp.zeros((8 - len(SMALL_SHARDED), PACK_COLS), F32)], axis=0)
    def own_slot(land, own):
        return lax.dynamic_update_slice(land, own[None], (me, 0, 0))

    def gather_done(handle, names, after, name):
        got = exchange_wait(handle, after, name)
        gathered.update({n: own_slot(g, shard[n]) for n, g in zip(names, got)})
        return got

    gathered = {}
    gather_0 = exchange_start([], [shard[n] for n in EARLY_0] + [small_send], x, "gather_0_start")
    gather_1 = exchange_start([], [shard[n] for n in EARLY_1], gather_0["token"], "gather_1_start")
    gather_2 = exchange_start([], [shard["a_w_gu"]], gather_1["token"], "gather_2_start")
    gather_2b = exchange_start([], [shard["a_w_down"]], gather_2["token"], "gather_2b_start")
    gather_3 = exchange_start([], [shard[n] for n in GATHER_3], gather_2b["token"], "gather_3_start")
    small_all = own_slot(gather_done(gather_0, EARLY_0, gather_3["token"], "gather_0_wait")[-1], small_send)
    a_norm_mix_f = small_all[:, 0, :a_norm_mix.shape[1]].reshape(-1)
    a_scale_f = small_all[:, 1, :a_scale.shape[1]].reshape(-1)
    a_norm_ffn_f = small_all[:, 2, :a_norm_ffn.shape[1]].reshape(-1)

    def rows_full(n):
        return gathered[n].reshape(-1, shard2d[n][1])

    def gu_views(tag):
        gu8 = gathered[tag + "_w_gu"]
        return gu8.reshape((2, n_sh) + gu8.shape[1:]), gu8

    def down_view(tag):
        down = gathered[tag + "_w_down"]
        return down.reshape(n_sh, -1, down.shape[2])

    wbd = jnp.zeros((SB_WIDTH, SB_WIDTH), BF16)
    for g in range(4):
        sl = slice(g * POOL_GROUP, (g + 1) * POOL_GROUP)
        wbd = wbd.at[sl, sl].set(a_w_group[0, g].astype(BF16))

    xs, mems, tgt = x[0], mem[0], loss_target[0]

    memn = rmsnorm_fwd(mems, mem_norm, "memn")
    (h0, proj), = norm_matmul(xs, [(a_norm_mix_f, rows_full("a_w_in"), F32)], "proj_a")
    mkv_a = matmul([(memn, rows_full("a_w_mem_kv"))], "nn", BF16, "mkv_a")
    pooled, cat_a = pool_fwd(proj, wbd, a_scale_f, "pool_fwd")
    cat_a = mem_attn_fwd(proj, mkv_a, cat_a, "mem_fwd_a")
    gather_done(gather_1, EARLY_1, cat_a, "gather_1_wait")
    x1 = matmul([(cat_a, rows_full("a_w_out"))], "nn", F32, "out_a", res=xs)
    h1 = rmsnorm_fwd(x1, a_norm_ffn_f, "h1")
    gather_done(gather_2, ("a_w_gu",), h1, "gather_2_wait")
    wgu_a, wgu8_a = gu_views("a")
    gu_a, act_a = ffn_up(h1, wgu_a, "ffn_up_a")
    gather_done(gather_2b, ("a_w_down",), act_a, "gather_2b_wait")
    wdown_a = down_view("a")
    x2 = matmul([(act_a, wdown_a)], "nn", F32, "down_a", res=x1, roles=("reduce", "reduce"), count=n_sh)

    gather_done(gather_3, GATHER_3, x2, "gather_3_wait")
    wkv = jnp.transpose(gathered["w_kv"], (1, 0, 2)).reshape(shard2d["w_kv"][0], -1)
    (hk, kv), (h3, projb) = norm_matmul(
        x2, [(kv_norm, wkv, BF16), (b_norm_mix[0], rows_full("b_w_q"), F32)], "kv_proj_b")
    cat_b, csave, got = sb_fwd(projb, kv, "sb_fwd", gather=[shard[n] for n in GATHER_LATE])
    gathered.update(zip(GATHER_LATE, got))
    mkv_b = matmul([(memn, rows_full("b_w_mem_kv"))], "nn", BF16, "mkv_b")
    cat_b = mem_attn_fwd(projb, mkv_b, cat_b, "mem_fwd_b")
    x3 = matmul([(cat_b, rows_full("b_w_out"))], "nn", F32, "out_b", res=x2)
    h4 = rmsnorm_fwd(x3, b_norm_ffn[0], "h4")
    (wgu_b, wgu8_b), wdown_b = gu_views("b"), down_view("b")
    gu_b, act_b = ffn_up(h4, wgu_b, "ffn_up_b")
    x4 = matmul([(act_b, wdown_b)], "nn", F32, "down_b", res=x3, roles=("reduce", "reduce"), count=n_sh)

    grads = {}
    dx4, dx4_bf, grads["final_norm"], loss_part = loss_head(x4, tgt, final_norm, "loss_head")

    def blocks(n, g):
        return g.reshape((N_DEV,) + tuple(shard2d[n]))

    def ffn_weight_grads(tag, dx_out_bf, h, gu, act, wgu8, wdown):
        fs = wgu8.shape[2]
        dgu8 = ffn_bwd_act(dx_out_bf, wdown, gu, "ffn_bwd_act_" + tag).reshape(N_DEV, s_len, fs)
        d_wdown = matmul([(act, dx_out_bf)], "tn", BF16, "dw_down_" + tag, tm=fs, tn=1024, roles=("batch", None),
                         count=n_sh)
        grads[tag + "_w_gu"] = matmul([(dgu8, h)], "tn", BF16, "dw_gu_" + tag, tm=fs, tn=1024, roles=("batch", None),
                                      count=N_DEV)
        grads[tag + "_w_down"] = blocks(tag + "_w_down", d_wdown)
        return dgu8

    def ffn_input_grads(tag, x_in, dx_out, dgu8, wgu8, g_ffn):
        dx_in, dx_in_bf, dgs = matmul_norm_bwd(x_in, dx_out, [(g_ffn, [(dgu8, wgu8)])], "norm_bwd_ffn_" + tag)
        return dx_in, dx_in_bf, dgs[0]

    def grads_start(names, small, after, name):
        return exchange_start([grads[n] for n in names], small, after, name)

    def grads_done(handle, names, after, name):
        got = exchange_wait(handle, after, name)
        for n, g in zip(names, got):
            recv[n] = lax.dynamic_update_slice(g, lax.dynamic_slice(grads[n], (me, 0, 0), (1,) + g.shape[1:]), (me, 0, 0))
        return got

    recv = {}
    dgu8_b = ffn_weight_grads("b", dx4_bf, h4, gu_b, act_b, wgu8_b, wdown_b)
    dx3, dx3_bf, grads["b_norm_ffn"] = ffn_input_grads("b", x3, dx4, dgu8_b, wgu8_b, b_norm_ffn[0])

    dcat_b = matmul([(dx3_bf, rows_full("b_w_out"))], "nt", F32, "dcat_b")
    grads["b_w_out"] = blocks("b_w_out", matmul([(cat_b, dx3_bf)], "tn", BF16, "dw_out_b"))
    dprojb, dk, dv, recv_late = sb_bwd(projb, kv, dcat_b, csave, "sb_bwd", scatter=[grads[n] for n in LATE])
    dprojb, dmkv_b = mem_attn_bwd(projb, mkv_b, dcat_b, dprojb, "mem_bwd_b")
    grads["b_w_q"] = blocks("b_w_q", matmul([(h3, dprojb)], "tn", BF16, "dw_q_b"))
    d_wkv_t = jnp.concatenate([matmul([(dk, hk)], "tn", BF16, "dw_k", tm=SB_WIDTH),
                               matmul([(dv, hk)], "tn", BF16, "dw_v", tm=SB_WIDTH)], axis=0)
    grads["w_kv"] = d_wkv_t.reshape(N_DEV, -1, d_wkv_t.shape[1])
    dx2, dx2_bf, dgs = matmul_norm_bwd(
        x2, dx3, [(b_norm_mix[0], [(dprojb, rows_full("b_w_q"))]),
                  (kv_norm, [(dk, wkv[:, :SB_WIDTH]), (dv, wkv[:, SB_WIDTH:])])], "norm_bwd_x2")
    grads["b_norm_mix"], grads["kv_norm"] = dgs
    dmkv_b_bf = dmkv_b.astype(BF16)
    grads["b_w_mem_kv"] = blocks("b_w_mem_kv", matmul([(memn, dmkv_b_bf)], "tn", BF16, "dw_mkv_b"))

    grads_3 = grads_start(EARLY_3, [], grads["b_w_mem_kv"], "grads_3_start")
    dgu8_a = ffn_weight_grads("a", dx2_bf, h1, gu_a, act_a, wgu8_a, wdown_a)
    grads_2 = grads_start(EARLY_2, [], grads_3["token"], "grads_2_start")
    dx1, dx1_bf, grads["a_norm_ffn"] = ffn_input_grads(
        "a", x1, dx2, dgu8_a, wgu8_a, a_norm_ffn_f + grads_2["token"][0, 0])

    dcat_a = matmul([(dx1_bf, rows_full("a_w_out"))], "nt", F32, "dcat_a")
    grads["a_w_out"] = blocks("a_w_out", matmul([(cat_a, dx1_bf)], "tn", BF16, "dw_out_a"))
    dproj, d_wbd, grads["a_scale"] = pool_bwd(dcat_a, pooled, wbd, a_scale_f, "pool_bwd")
    grads["a_w_group"] = jnp.stack(
        [d_wbd[g * POOL_GROUP:(g + 1) * POOL_GROUP, g * POOL_GROUP:(g + 1) * POOL_GROUP] for g in range(4)])
    group_grads = grads["a_w_group"].reshape(-1, PACK_COLS).astype(BF16)
    grads_1 = exchange_start([grads[n] for n in EARLY_1], [group_grads], grads_2["token"], "grads_1_start")
    dproj, dmkv_a = mem_attn_bwd(proj, mkv_a, dcat_a, dproj, "mem_bwd_a")
    grads["a_w_in"] = blocks("a_w_in", matmul([(h0, dproj)], "tn", BF16, "dw_in_a"))
    dmkv_a_bf = dmkv_a.astype(BF16)
    grads["a_w_mem_kv"] = blocks("a_w_mem_kv", matmul([(memn, dmkv_a_bf)], "tn", BF16, "dw_mkv_a"))
    grads_0 = grads_start(EARLY_0, [], grads_1["token"], "grads_0_start")
    dx, _, dgs = matmul_norm_bwd(
        xs, dx1, [(a_norm_mix_f + grads_0["token"][0, 0], [(dproj, rows_full("a_w_in"))])], "norm_bwd_x")
    grads["a_norm_mix"] = dgs[0]
    dmemn = matmul([(dmkv_a_bf, rows_full("a_w_mem_kv")), (dmkv_b_bf, rows_full("b_w_mem_kv"))], "nt", F32, "dmemn")
    _, _, dgs = rmsnorm_bwd(mems, jnp.zeros_like(mems), [(mem_norm, dmemn)], "norm_bwd_mem")
    grads["mem_norm"] = dgs[0]

    small_names = SMALL_REPL + SMALL_SHARDED
    head_rows = 16

    def small_pack(rows_of):
        head = jnp.zeros((head_rows, PACK_COLS), F32)
        for r, (v, col) in enumerate(rows_of):
            v = v.reshape(1, -1)
            if col is None:
                head = head + jnp.pad(v, ((r, head_rows - 1 - r), (0, PACK_COLS - v.shape[1])))
            else:
                head = lax.dynamic_update_slice(head, v, (r, col))
        return head

    small_grads = small_pack([(grads[n], None) for n in small_names] + [(loss_part[:, :1], None)])
    grads_s = exchange_start([], [small_grads], dx, "grads_s_start")

    def small_state(src):
        return small_pack([(src[n], None) for n in SMALL_REPL]
                          + [(src[n], me * src[n].shape[-1]) for n in SMALL_SHARDED])

    def state2d(src, n):
        a = src[n].reshape(shard2d[n])
        return a.T if n in COL_SHARDED else a

    def adamw(names, after):
        for n in names:
            big_out[n] = reduce_adamw(recv[n], state2d(w, n), state2d(mom, n), state2d(var, n), "adamw_" + n, after=after)
        return big_out[names[-1]][0]

    big_out = {}
    recv.update(zip(LATE, recv_late))
    done = adamw(LATE, grads_s["token"])
    grads_done(grads_3, EARLY_3, done, "grads_3_wait")
    done = adamw(EARLY_3, None)
    grads_done(grads_2, EARLY_2, done, "grads_2_wait")
    done = adamw(EARLY_2, None)
    group_recv = own_slot(grads_done(grads_1, EARLY_1, done, "grads_1_wait")[-1], group_grads)
    done = adamw(EARLY_1, None)
    group_out = reduce_adamw(group_recv, *[src["a_w_group"].reshape(-1, PACK_COLS) for src in (w, mom, var)], "adamw_group")
    grads_done(grads_0, EARLY_0, group_out[0], "grads_0_wait")
    done = adamw(EARLY_0, None)
    small_recv = own_slot(exchange_wait(grads_s, done, "grads_s_wait")[0], small_grads)
    small_out = reduce_adamw(small_recv, small_state(w), small_state(mom), small_state(var), "adamw_small")

    def unpack(kind):
        out = {n: (big_out[n][kind].T if n in COL_SHARDED else big_out[n][kind]).reshape(w[n].shape) for n in BIG}
        so = small_out[kind]
        for r, n in enumerate(SMALL_REPL):
            out[n] = so[r, :w[n].shape[-1]].reshape(w[n].shape)
        for r, n in enumerate(SMALL_SHARDED):
            width = w[n].shape[-1]
            out[n] = lax.dynamic_slice(so, (len(SMALL_REPL) + r, me * width), (1, width)).reshape(w[n].shape)
        out["a_w_group"] = group_out[kind].reshape(a_w_group.shape)
        return out

    loss = small_out[0][len(small_names), 0]
    results = [loss, dx.reshape(x.shape)]
    for kind in range(4):
        out = unpack(kind)
        results += [out[n] for n in WEIGHTS]
    return tuple(results)
```

```python
import functools

import jax
import jax.numpy as jnp
from jax import lax
from jax.experimental import pallas as pl
from jax.experimental.pallas import tpu as pltpu

F32 = jnp.float32
BF16 = jnp.bfloat16

N_DEV = 8
HEAD_DIM = 64
N_SB_HEADS = 12
SB_WIDTH = N_SB_HEADS * HEAD_DIM
MEM_WIDTH = 4 * HEAD_DIM
POOL_WINDOWS = (2, 4, 8, 16)
POOL_GROUP = SB_WIDTH // 4
POOL_HALO = 16
EPS = 1e-6
QK_SCALE = HEAD_DIM ** -0.5
LOG2_E = 1.4426950408889634
LANES = 128
PACK_COLS = 1024

ADAM_LR = 0.001
ADAM_B1 = 0.9
ADAM_B2 = 0.999
ADAM_EPS = 1e-08
ADAM_WD = 0.01
ADAM_STEP = 10

VMEM_LIMIT = 48 * 1024 * 1024

_NT = (((1,), (1,)), ((), ()))
_TN = (((0,), (0,)), ((), ()))
_NN = (((1,), (0,)), ((), ()))


def _params(*sem):
    return pltpu.CompilerParams(dimension_semantics=sem, vmem_limit_bytes=VMEM_LIMIT)


def _dot(a, b, dims=_NN):
    return lax.dot_general(a, b, dims, preferred_element_type=F32)


def exchange(a2a, ag, name):
    n_a2a, n_arr = len(a2a), len(a2a) + len(ag)

    def body(*refs):
        start, wait = _exchange_ops(refs[:n_arr], refs[n_arr:2 * n_arr], refs[2 * n_arr:], n_a2a)
        start()
        wait()

    return pl.pallas_call(
        body, name=name, out_shape=_exchange_out_shape(a2a, ag),
        in_specs=[_ANY] * n_arr, out_specs=[_ANY] * n_arr,
        scratch_shapes=_exchange_sems(n_arr),
    )(*a2a, *ag)


_ANY = pl.BlockSpec(memory_space=pl.ANY)
_HBM = pl.BlockSpec(memory_space=pltpu.HBM)
_SEM = pl.BlockSpec(memory_space=pltpu.SEMAPHORE)
_DATAFLOW = pltpu.SideEffectType.DATAFLOW_SIDE_EFFECTING


def exchange_start(a2a, ag, after, name):
    n_a2a, n_arr = len(a2a), len(a2a) + len(ag)
    arrays = [pltpu.with_memory_space_constraint(v, pltpu.HBM) for v in list(a2a) + list(ag)]
    shapes = _exchange_out_shape(a2a, ag)
    lands = [pltpu.with_memory_space_constraint(lax.empty(sh.shape, sh.dtype), pltpu.HBM) for sh in shapes]

    def body(*refs):
        ins, outs = refs[:n_arr], refs[n_arr:2 * n_arr]
        send_sems, recv_sems, token = refs[4 * n_arr + 1:4 * n_arr + 4]
        start, _ = _exchange_ops(ins, outs, (send_sems, recv_sems, None), n_a2a, local_copies=False)
        start()
        token[...] = jnp.zeros_like(token)

    outs = pl.pallas_call(
        body, name=name,
        out_shape=[pltpu.HBM(v.shape, v.dtype) for v in arrays] + [pltpu.HBM(sh.shape, sh.dtype) for sh in shapes]
        + [pltpu.SemaphoreType.DMA((n_arr * (N_DEV - 1),)), pltpu.SemaphoreType.DMA((n_arr * (N_DEV - 1),)),
           jax.ShapeDtypeStruct((8, LANES), F32)],
        in_specs=[_HBM] * (2 * n_arr) + [_ANY],
        out_specs=[_HBM] * (2 * n_arr) + [_SEM, _SEM, pl.BlockSpec(memory_space=pltpu.VMEM)],
        input_output_aliases={i: i for i in range(2 * n_arr)},
        compiler_params=pltpu.CompilerParams(has_side_effects=_DATAFLOW),
    )(*arrays, *lands, after)
    return dict(ins=outs[:n_arr], lands=outs[n_arr:2 * n_arr], sems=outs[2 * n_arr:2 * n_arr + 2], n_a2a=n_a2a,
                token=outs[2 * n_arr + 2])


def exchange_wait(handle, after, name):
    ins, lands, sems, n_a2a = handle["ins"], handle["lands"], handle["sems"], handle["n_a2a"]
    n_arr = len(ins)

    def body(*refs):
        in_refs, land_refs = refs[:n_arr], refs[n_arr:2 * n_arr]
        send_sems, recv_sems = refs[2 * n_arr:2 * n_arr + 2]
        _, wait = _exchange_ops(in_refs, land_refs, (send_sems, recv_sems, None), n_a2a, local_copies=False)
        wait()

    outs = pl.pallas_call(
        body, name=name,
        out_shape=[pltpu.HBM(v.shape, v.dtype) for v in list(ins) + list(lands)],
        in_specs=[_HBM] * (2 * n_arr) + [_SEM, _SEM, _ANY], out_specs=[_HBM] * (2 * n_arr),
        input_output_aliases={i: i for i in range(2 * n_arr)},
        compiler_params=pltpu.CompilerParams(has_side_effects=_DATAFLOW),
    )(*ins, *lands, *sems, after)
    return outs[n_arr:]


def _exchange_out_shape(a2a, ag):
    out_shape = [jax.ShapeDtypeStruct(v.shape, v.dtype) for v in a2a]
    return out_shape + [jax.ShapeDtypeStruct((N_DEV,) + v.shape, v.dtype) for v in ag]


def _exchange_sems(n_arr):
    return [pltpu.SemaphoreType.DMA((n_arr * (N_DEV - 1),)), pltpu.SemaphoreType.DMA((n_arr * (N_DEV - 1),)),
            pltpu.SemaphoreType.DMA((n_arr,))]


def _exchange_ops(ins, outs, sems, n_a2a, local_copies=True):
    send_sems, recv_sems, local_sems = sems
    n_arr = len(ins)
    x, y, c = lax.axis_index("x"), lax.axis_index("y"), lax.axis_index("c")
    me = 4 * x + 2 * y + c

    def peer_of(k):
        px = 1 - x if k & 4 else x
        py = 1 - y if k & 2 else y
        pc = 1 - c if k & 1 else c
        return (px, py, pc), 4 * px + 2 * py + pc

    def copy(a, k, landing):
        peer, pid = peer_of(k)
        src = ins[a].at[pid] if a < n_a2a else ins[a]
        return pltpu.make_async_remote_copy(
            src_ref=src, dst_ref=outs[a].at[pid if landing else me], send_sem=send_sems.at[a * (N_DEV - 1) + k - 1],
            recv_sem=recv_sems.at[a * (N_DEV - 1) + k - 1], device_id=peer, device_id_type=pl.DeviceIdType.MESH)

    def local(a):
        src = ins[a].at[me] if a < n_a2a else ins[a]
        return pltpu.make_async_copy(src, outs[a].at[me], local_sems.at[a])

    order = (6, 7, 4, 5, 2, 3, 1)

    def start():
        if local_copies:
            for a in range(n_arr):
                local(a).start()
        for a in range(n_arr):
            for k in order:
                copy(a, k, landing=False).start()

    def wait():
        for a in range(n_arr):
            for k in order:
                copy(a, k, landing=True).wait_recv()
        for a in range(n_arr):
            for k in order:
                copy(a, k, landing=False).wait_send()
        if local_copies:
            for a in range(n_arr):
                local(a).wait()

    return start, wait


def _rows_tile(n, want):
    t = min(want, n)
    assert n % t == 0, (n, t)
    return t


def rmsnorm_fwd(x, g, name):
    s, d = x.shape
    tm = _rows_tile(s, 512)

    def body(x_ref, g_ref, h_ref):
        xv = x_ref[...]
        r = lax.rsqrt(jnp.mean(xv * xv, axis=-1, keepdims=True) + EPS)
        h_ref[...] = ((xv * r) * g_ref[...]).astype(h_ref.dtype)

    return pl.pallas_call(
        body, name=name, grid=(s // tm,),
        in_specs=[pl.BlockSpec((tm, d), lambda i: (i, 0)), pl.BlockSpec((1, d), lambda i: (0, 0))],
        out_specs=pl.BlockSpec((tm, d), lambda i: (i, 0)),
        out_shape=jax.ShapeDtypeStruct((s, d), BF16),
        compiler_params=_params("parallel"),
    )(x, g.reshape(1, d))


def norm_matmul(x, heads, name):
    s, d = x.shape
    tm = _rows_tile(s, 512)
    n = len(heads)

    def body(*refs):
        x_ref = refs[0]
        xv = x_ref[...]
        xhat = xv * lax.rsqrt(jnp.mean(xv * xv, axis=-1, keepdims=True) + EPS)
        for k in range(n):
            g_ref, w_ref = refs[1 + 2 * k], refs[2 + 2 * k]
            h_ref, y_ref = refs[1 + 2 * n + 2 * k], refs[2 + 2 * n + 2 * k]
            h = (xhat * g_ref[...]).astype(BF16)
            h_ref[...] = h
            y_ref[...] = _dot(h, w_ref[...]).astype(y_ref.dtype)

    row = pl.BlockSpec((tm, d), lambda i: (i, 0))
    in_specs, out_specs, out_shape, args = [row], [], [], [x]
    for g, w_k, dtype in heads:
        in_specs += [pl.BlockSpec((1, d), lambda i: (0, 0)), pl.BlockSpec(w_k.shape, lambda i: (0, 0))]
        args += [g.reshape(1, d), w_k]
        out_specs += [row, pl.BlockSpec((tm, w_k.shape[1]), lambda i: (i, 0))]
        out_shape += [jax.ShapeDtypeStruct((s, d), BF16), jax.ShapeDtypeStruct((s, w_k.shape[1]), dtype)]
    outs = pl.pallas_call(
        body, name=name, grid=(s // tm,), in_specs=in_specs, out_specs=out_specs, out_shape=out_shape,
        compiler_params=_params("parallel"),
    )(*args)
    return [(outs[2 * k], outs[2 * k + 1]) for k in range(n)]


def rmsnorm_bwd(x, dres, pairs, name):
    s, d = x.shape
    tm = _rows_tile(s, 256)
    n = len(pairs)

    def body(*refs):
        x_ref, dres_ref = refs[0], refs[1]
        g_refs, dh_refs = refs[2:2 + n], refs[2 + n:2 + 2 * n]
        dx_ref, dxb_ref = refs[2 + 2 * n], refs[3 + 2 * n]
        dg_refs = refs[4 + 2 * n:]
        i = pl.program_id(0)
        xv = x_ref[...]
        r = lax.rsqrt(jnp.mean(xv * xv, axis=-1, keepdims=True) + EPS)
        xhat = xv * r
        acc = dres_ref[...]
        for k in range(n):
            dh = dh_refs[k][...].astype(F32)
            part = jnp.sum(dh * xhat, axis=0, keepdims=True)

            @pl.when(i == 0)
            def _(k=k, part=part):
                dg_refs[k][...] = part

            @pl.when(i > 0)
            def _(k=k, part=part):
                dg_refs[k][...] += part

            dxh = dh * g_refs[k][...]
            acc = acc + r * (dxh - xhat * jnp.mean(dxh * xhat, axis=-1, keepdims=True))
        dx_ref[...] = acc
        dxb_ref[...] = acc.astype(BF16)

    row = pl.BlockSpec((tm, d), lambda i: (i, 0))
    vec = pl.BlockSpec((1, d), lambda i: (0, 0))
    outs = pl.pallas_call(
        body, name=name, grid=(s // tm,),
        in_specs=[row, row] + [vec] * n + [row] * n,
        out_specs=[row, row] + [vec] * n,
        out_shape=[jax.ShapeDtypeStruct((s, d), F32), jax.ShapeDtypeStruct((s, d), BF16)]
        + [jax.ShapeDtypeStruct((1, d), F32)] * n,
        compiler_params=_params("arbitrary"),
    )(x, dres, *[g.reshape(1, d) for g, _ in pairs], *[dh for _, dh in pairs])
    return outs[0], outs[1], outs[2:]


def matmul_norm_bwd(x, dres, heads, name):
    s, d = x.shape
    tm = _rows_tile(s, 512)
    n = len(heads)
    n_ops = [len(pairs) for _, pairs in heads]

    def body(*refs):
        x_ref, dres_ref = refs[0], refs[1]
        n_in = 2 + n + 2 * sum(n_ops)
        dx_ref, dxb_ref = refs[n_in], refs[n_in + 1]
        dg_refs = refs[n_in + 2:]
        i = pl.program_id(0)
        xv = x_ref[...]
        r = lax.rsqrt(jnp.mean(xv * xv, axis=-1, keepdims=True) + EPS)
        xhat = xv * r
        acc = dres_ref[...]
        pos = 2 + n
        for k in range(n):
            dh = None
            for a, _ in heads[k][1]:
                a_ref, w_ref = refs[pos], refs[pos + 1]
                for kk in range(a.shape[0] if a.ndim == 3 else 1):
                    prod = _dot(a_ref[kk], w_ref[kk], _NT) if a.ndim == 3 else _dot(a_ref[...], w_ref[...], _NT)
                    dh = prod if dh is None else dh + prod
                pos += 2
            part = jnp.sum(dh * xhat, axis=0, keepdims=True)

            @pl.when(i == 0)
            def _(k=k, part=part):
                dg_refs[k][...] = part

            @pl.when(i > 0)
            def _(k=k, part=part):
                dg_refs[k][...] += part

            dxh = dh * refs[2 + k][...]
            acc = acc + r * (dxh - xhat * jnp.mean(dxh * xhat, axis=-1, keepdims=True))
        dx_ref[...] = acc
        dxb_ref[...] = acc.astype(BF16)

    row = pl.BlockSpec((tm, d), lambda i: (i, 0))
    vec = pl.BlockSpec((1, d), lambda i: (0, 0))
    in_specs, args = [row, row] + [vec] * n, [x, dres] + [g.reshape(1, d) for g, _ in heads]
    for _, pairs in heads:
        for a, w_k in pairs:
            if a.ndim == 3:
                in_specs += [pl.BlockSpec((a.shape[0], tm, a.shape[2]), lambda i: (0, i, 0)),
                             pl.BlockSpec(w_k.shape, lambda i: (0, 0, 0), pipeline_mode=pl.Buffered(1))]
            else:
                in_specs += [pl.BlockSpec((tm, a.shape[1]), lambda i: (i, 0)),
                             pl.BlockSpec(w_k.shape, lambda i: (0, 0), pipeline_mode=pl.Buffered(1))]
            args += [a, w_k]
    outs = pl.pallas_call(
        body, name=name, grid=(s // tm,), in_specs=in_specs, out_specs=[row, row] + [vec] * n,
        out_shape=[jax.ShapeDtypeStruct((s, d), F32), jax.ShapeDtypeStruct((s, d), BF16)]
        + [jax.ShapeDtypeStruct((1, d), F32)] * n,
        compiler_params=_params("arbitrary"),
    )(*args)
    return outs[0], outs[1], outs[2:]


def loss_head(x, tgt, g, name):
    s, d = x.shape
    tm = _rows_tile(s, 512)

    def body(x_ref, t_ref, g_ref, dx_ref, dxb_ref, dg_ref, loss_ref):
        i = pl.program_id(0)
        xv = x_ref[...]
        gv = g_ref[...]
        r = lax.rsqrt(jnp.mean(xv * xv, axis=-1, keepdims=True) + EPS)
        xhat = xv * r
        diff = xhat * gv - t_ref[...]
        part_loss = 0.5 * jnp.sum(jnp.mean(diff * diff, axis=-1, keepdims=True), axis=0, keepdims=True)
        dy = diff * (1.0 / d)
        part_dg = jnp.sum(dy * xhat, axis=0, keepdims=True)

        @pl.when(i == 0)
        def _():
            dg_ref[...] = part_dg
            loss_ref[...] = jnp.broadcast_to(part_loss, loss_ref.shape)

        @pl.when(i > 0)
        def _():
            dg_ref[...] += part_dg
            loss_ref[...] += jnp.broadcast_to(part_loss, loss_ref.shape)

        dxh = dy * gv
        dx = r * (dxh - xhat * jnp.mean(dxh * xhat, axis=-1, keepdims=True))
        dx_ref[...] = dx
        dxb_ref[...] = dx.astype(BF16)

    row = pl.BlockSpec((tm, d), lambda i: (i, 0))
    vec = pl.BlockSpec((1, d), lambda i: (0, 0))
    return pl.pallas_call(
        body, name=name, grid=(s // tm,),
        in_specs=[row, row, vec], out_specs=[row, row, vec, pl.BlockSpec((1, LANES), lambda i: (0, 0))],
        out_shape=[jax.ShapeDtypeStruct((s, d), F32), jax.ShapeDtypeStruct((s, d), BF16),
                   jax.ShapeDtypeStruct((1, d), F32), jax.ShapeDtypeStruct((1, LANES), F32)],
        compiler_params=_params("arbitrary"),
    )(x, tgt, g.reshape(1, d))


def matmul(pairs, mode, out_dtype, name, res=None, tm=512, tn=None, roles=(None, None), count=1):
    a0, b0 = pairs[0]
    m = a0.shape[-1] if mode == "tn" else a0.shape[-2]
    n = b0.shape[-2] if mode == "nt" else b0.shape[-1]
    if tn is None:
        tn = n if (mode != "tn" and n <= 1024) else 512
    tm, tn = _rows_tile(m, tm), _rows_tile(n, tn)
    dims = {"nn": _NN, "nt": _NT, "tn": _TN}[mode]
    npairs = len(pairs)
    batched, reducing = "batch" in roles, "reduce" in roles
    assert not (batched and reducing) and (res is None or not batched)
    n_in = 2 * npairs + (res is not None)

    def body(*refs):
        o_ref = refs[n_in]
        acc = None
        for p in range(npairs):
            a_ref, b_ref = refs[2 * p], refs[2 * p + 1]
            for kk in range(count if reducing else 1):
                a = a_ref[kk] if roles[0] == "reduce" else a_ref[...]
                b = b_ref[kk] if roles[1] == "reduce" else b_ref[...]
                d = _dot(a.astype(BF16), b.astype(BF16), dims)
                acc = d if acc is None else acc + d
        if res is not None:
            acc = acc + refs[2 * npairs][...]
        o_ref[...] = acc.astype(o_ref.dtype)

    def spec(arr, role, block, index):
        if arr.ndim == 2:
            return pl.BlockSpec(block, lambda bb, i, j: index(i, j))
        assert role in ("batch", "reduce") and arr.shape[0] == count, (name, role, arr.shape)
        if role == "batch":
            return pl.BlockSpec((None,) + block, lambda bb, i, j: (bb,) + index(i, j))
        return pl.BlockSpec((count,) + block, lambda bb, i, j: (0,) + index(i, j))

    in_specs, args = [], []
    for a, b in pairs:
        if mode == "tn":
            in_specs.append(spec(a, roles[0], (a.shape[-2], tm), lambda i, j: (0, i)))
        else:
            in_specs.append(spec(a, roles[0], (tm, a.shape[-1]), lambda i, j: (i, 0)))
        if mode == "nt":
            in_specs.append(spec(b, roles[1], (tn, b.shape[-1]), lambda i, j: (j, 0)))
        else:
            in_specs.append(spec(b, roles[1], (b.shape[-2], tn), lambda i, j: (0, j)))
        args += [a, b]
    if res is not None:
        in_specs.append(pl.BlockSpec((tm, tn), lambda bb, i, j: (i, j)))
        args.append(res)
    if batched:
        out_spec = pl.BlockSpec((None, tm, tn), lambda bb, i, j: (bb, i, j))
        out_shape = jax.ShapeDtypeStruct((count, m, n), out_dtype)
    else:
        out_spec = pl.BlockSpec((tm, tn), lambda bb, i, j: (i, j))
        out_shape = jax.ShapeDtypeStruct((m, n), out_dtype)
    return pl.pallas_call(
        body, name=name, grid=(count if batched else 1, m // tm, n // tn),
        in_specs=in_specs, out_specs=out_spec, out_shape=out_shape,
        compiler_params=_params("parallel", "parallel", "parallel"),
    )(*args)


def _sigmoid(v):
    return 0.5 * jnp.tanh(0.5 * v) + 0.5


def ffn_up(h, wgu, name):
    s, d = h.shape
    _, n_sh, _, fs = wgu.shape
    tm = _rows_tile(s, 2048)

    def body(h_ref, w_ref, gu_ref, act_ref):
        hv = h_ref[...]
        gate = _dot(hv, w_ref[0])
        up = _dot(hv, w_ref[1])
        gu_ref[0] = gate.astype(BF16)
        gu_ref[1] = up.astype(BF16)
        act_ref[...] = ((gate * _sigmoid(gate)) * up).astype(BF16)

    return pl.pallas_call(
        body, name=name, grid=(n_sh, s // tm),
        in_specs=[pl.BlockSpec((tm, d), lambda j, i: (i, 0)), pl.BlockSpec((2, None, d, fs), lambda j, i: (0, j, 0, 0))],
        out_specs=[pl.BlockSpec((2, None, tm, fs), lambda j, i: (0, j, i, 0)),
                   pl.BlockSpec((None, tm, fs), lambda j, i: (j, i, 0))],
        out_shape=[jax.ShapeDtypeStruct((2, n_sh, s, fs), BF16), jax.ShapeDtypeStruct((n_sh, s, fs), BF16)],
        compiler_params=_params("parallel", "parallel"),
    )(h, wgu)


def ffn_bwd_act(dxo, wdown, gu, name):
    s, d = dxo.shape
    n_sh, fs, _ = wdown.shape
    tm = _rows_tile(s, 2048)

    def body(dx_ref, w_ref, gu_ref, dgu_ref):
        dact = _dot(dx_ref[...], w_ref[...], _NT)
        gate = gu_ref[0].astype(F32)
        sig = _sigmoid(gate)
        dgu_ref[0] = ((dact * gu_ref[1].astype(F32)) * (sig * (1.0 + gate * (1.0 - sig)))).astype(BF16)
        dgu_ref[1] = (dact * (gate * sig)).astype(BF16)

    tile = pl.BlockSpec((2, None, tm, fs), lambda j, i: (0, j, i, 0))
    return pl.pallas_call(
        body, name=name, grid=(n_sh, s // tm),
        in_specs=[pl.BlockSpec((tm, d), lambda j, i: (i, 0)), pl.BlockSpec((None, fs, d), lambda j, i: (j, 0, 0)), tile],
        out_specs=tile,
        out_shape=jax.ShapeDtypeStruct((2, n_sh, s, fs), BF16),
        compiler_params=_params("parallel", "parallel"),
    )(dxo, wdown, gu)


def _pool_select(col, by_window):
    out = by_window[3]
    for g in (2, 1, 0):
        out = jnp.where(col < (g + 1) * POOL_GROUP, by_window[g], out)
    return out


def _pool_counts(pos):
    return [jnp.minimum(pos + 1, w).astype(F32) for w in POOL_WINDOWS]


def pool_fwd(proj, wbd, scale, name):
    s = proj.shape[0]
    tm = _rows_tile(s, 512)
    nt = s // tm
    width = SB_WIDTH
    per = tm // POOL_HALO

    def body(u_ref, halo_ref, w_ref, sc_ref, pooled_ref, grouped_ref):
        i = pl.program_id(0)
        u = u_ref[...]
        halo = jnp.where(i > 0, halo_ref[...], 0.0)
        ext = jnp.concatenate([halo, u], axis=0)
        sums = []
        acc = ext
        for k in (1, 2, 4, 8):
            acc = acc + pltpu.roll(acc, k, 0)
            sums.append(acc[POOL_HALO:])
        pos = i * tm + lax.broadcasted_iota(jnp.int32, (tm, width), 0)
        col = lax.broadcasted_iota(jnp.int32, (tm, width), 1)
        means = [sm / cnt for sm, cnt in zip(sums, _pool_counts(pos))]
        pooled = (_pool_select(col, means) - u).astype(BF16)
        pooled_ref[...] = pooled
        grouped_ref[...] = (_dot(pooled, w_ref[...]) * sc_ref[...]).astype(BF16)

    row = pl.BlockSpec((tm, width), lambda i: (i, 0))
    return pl.pallas_call(
        body, name=name, grid=(nt,),
        in_specs=[row, pl.BlockSpec((POOL_HALO, width), lambda i: (jnp.maximum(i * per - 1, 0), 0)),
                  pl.BlockSpec((width, width), lambda i: (0, 0)), pl.BlockSpec((1, width), lambda i: (0, 0))],
        out_specs=[row, row],
        out_shape=[jax.ShapeDtypeStruct((s, width), BF16), jax.ShapeDtypeStruct((s, width + MEM_WIDTH), BF16)],
        compiler_params=_params("parallel"),
    )(proj, proj, wbd, scale.reshape(1, width))


def pool_bwd(dcat, pooled, wbd, scale, name):
    s = dcat.shape[0]
    tm = _rows_tile(s, 512)
    nt = s // tm
    width = SB_WIDTH
    per = tm // POOL_HALO
    n_ext = tm + POOL_HALO

    def body(dg_ref, halo_ref, pooled_ref, w_ref, sc_ref, du_ref, dw_ref, dsc_ref):
        i = pl.program_id(0)
        w = w_ref[...]
        sc = sc_ref[...]
        dg = dg_ref[...]
        pooled = pooled_ref[...]
        pg = _dot(pooled, w)
        dpg = (dg * sc).astype(BF16)
        part_sc = jnp.sum(dg * pg, axis=0, keepdims=True)
        part_w = _dot(pooled, dpg, _TN)

        @pl.when(i == 0)
        def _():
            dsc_ref[...] = part_sc
            dw_ref[...] = part_w

        @pl.when(i > 0)
        def _():
            dsc_ref[...] += part_sc
            dw_ref[...] += part_w

        dpooled = _dot(dpg, w, _NT)
        halo_dpg = (jnp.where(i < nt - 1, halo_ref[...], 0.0) * sc).astype(BF16)
        ext = jnp.concatenate([dpooled, _dot(halo_dpg, w, _NT)], axis=0)
        pos = i * tm + lax.broadcasted_iota(jnp.int32, (n_ext, width), 0)
        col = lax.broadcasted_iota(jnp.int32, (tm, width), 1)
        outs = []
        for k_idx, cnt in enumerate(_pool_counts(pos)):
            acc = ext / cnt
            for k in (1, 2, 4, 8)[:k_idx + 1]:
                acc = acc + pltpu.roll(acc, n_ext - k, 0)
            outs.append(acc[:tm])
        du_ref[...] = (_pool_select(col, outs) - dpooled).astype(BF16)

    row = pl.BlockSpec((tm, width), lambda i: (i, 0))
    return pl.pallas_call(
        body, name=name, grid=(nt,),
        in_specs=[row, pl.BlockSpec((POOL_HALO, width), lambda i: (jnp.minimum((i + 1) * per, s // POOL_HALO - 1), 0)),
                  row, pl.BlockSpec((width, width), lambda i: (0, 0)), pl.BlockSpec((1, width), lambda i: (0, 0))],
        out_specs=[row, pl.BlockSpec((width, width), lambda i: (0, 0)), pl.BlockSpec((1, width), lambda i: (0, 0))],
        out_shape=[jax.ShapeDtypeStruct((s, width + MEM_WIDTH), BF16), jax.ShapeDtypeStruct((width, width), F32),
                   jax.ShapeDtypeStruct((1, width), F32)],
        compiler_params=_params("arbitrary"),
    )(dcat, dcat, pooled, wbd, scale.reshape(1, width))


def _head_masks(shape):
    lane = lax.broadcasted_iota(jnp.int32, shape, 1)
    return lane < HEAD_DIM, lane >= HEAD_DIM


def _mem_probs(qh, kp):
    logits = _dot(qh, kp, _NT)
    logits = logits - jnp.max(logits, axis=-1, keepdims=True)
    p = jnp.exp(logits)
    return p / jnp.sum(p, axis=-1, keepdims=True)


def mem_attn_fwd(proj, mkv, cat, name):
    s = proj.shape[0]
    m = mkv.shape[0]
    tm = _rows_tile(s, 512)
    q_blk = SB_WIDTH // MEM_WIDTH

    def body(q_ref, kv_ref, _, o_ref):
        masks = _head_masks((tm, LANES))
        for pr in range(MEM_WIDTH // LANES):
            q = q_ref[:, pr * LANES:(pr + 1) * LANES] * QK_SCALE
            kp = kv_ref[:, pr * LANES:(pr + 1) * LANES]
            vp = kv_ref[:, MEM_WIDTH + pr * LANES:MEM_WIDTH + (pr + 1) * LANES]
            outs = []
            for hm in masks:
                p = _mem_probs(jnp.where(hm, q, 0.0).astype(BF16), kp)
                outs.append(_dot(p.astype(BF16), vp))
            o_ref[:, pr * LANES:(pr + 1) * LANES] = jnp.where(masks[0], outs[0], outs[1]).astype(BF16)

    return pl.pallas_call(
        body, name=name, grid=(s // tm,),
        in_specs=[pl.BlockSpec((tm, MEM_WIDTH), lambda i: (i, q_blk)), pl.BlockSpec((m, 2 * MEM_WIDTH), lambda i: (0, 0)),
                  _ANY],
        out_specs=pl.BlockSpec((tm, MEM_WIDTH), lambda i: (i, q_blk)),
        out_shape=jax.ShapeDtypeStruct(cat.shape, BF16),
        input_output_aliases={2: 0},
        compiler_params=_params("parallel"),
    )(proj, mkv, cat)


def mem_attn_bwd(proj, mkv, dcat, dproj, name):
    s = proj.shape[0]
    m = mkv.shape[0]
    tm = _rows_tile(s, 512)
    q_blk = SB_WIDTH // MEM_WIDTH

    def body(q_ref, kv_ref, do_ref, _, dq_ref, dkv_ref):
        i = pl.program_id(0)

        @pl.when(i == 0)
        def _():
            dkv_ref[...] = jnp.zeros_like(dkv_ref)

        masks = _head_masks((tm, LANES))
        for pr in range(MEM_WIDTH // LANES):
            q = q_ref[:, pr * LANES:(pr + 1) * LANES] * QK_SCALE
            do = do_ref[:, pr * LANES:(pr + 1) * LANES]
            kp = kv_ref[:, pr * LANES:(pr + 1) * LANES]
            vp = kv_ref[:, MEM_WIDTH + pr * LANES:MEM_WIDTH + (pr + 1) * LANES]
            dqs = []
            dk = jnp.zeros((m, LANES), F32)
            dv = jnp.zeros((m, LANES), F32)
            for hm in masks:
                qh = jnp.where(hm, q, 0.0).astype(BF16)
                doh = jnp.where(hm, do, 0.0).astype(BF16)
                p = _mem_probs(qh, kp)
                dp = _dot(doh, vp, _NT)
                ds = (p * (dp - jnp.sum(dp * p, axis=-1, keepdims=True))).astype(BF16)
                dqs.append(_dot(ds, kp))
                dk = dk + _dot(ds, qh, _TN)
                dv = dv + _dot(p.astype(BF16), doh, _TN)
            dq_ref[:, pr * LANES:(pr + 1) * LANES] = (jnp.where(masks[0], dqs[0], dqs[1]) * QK_SCALE).astype(BF16)
            dkv_ref[:, pr * LANES:(pr + 1) * LANES] += dk
            dkv_ref[:, MEM_WIDTH + pr * LANES:MEM_WIDTH + (pr + 1) * LANES] += dv

    return pl.pallas_call(
        body, name=name, grid=(s // tm,),
        in_specs=[pl.BlockSpec((tm, MEM_WIDTH), lambda i: (i, q_blk)), pl.BlockSpec((m, 2 * MEM_WIDTH), lambda i: (0, 0)),
                  pl.BlockSpec((tm, MEM_WIDTH), lambda i: (i, q_blk)), _ANY],
        out_specs=[pl.BlockSpec((tm, MEM_WIDTH), lambda i: (i, q_blk)), pl.BlockSpec((m, 2 * MEM_WIDTH), lambda i: (0, 0))],
        out_shape=[jax.ShapeDtypeStruct(dproj.shape, BF16), jax.ShapeDtypeStruct((m, 2 * MEM_WIDTH), F32)],
        input_output_aliases={3: 0},
        compiler_params=_params("arbitrary"),
    )(proj, mkv, dcat, dproj)


SB_ROWS = 512
SB_KEYS = 512
SB_SUB = LANES


def _sb_tri(later):
    r = lax.broadcasted_iota(jnp.int32, (2 * SB_SUB, 2 * SB_SUB), 0) % SB_SUB
    c = lax.broadcasted_iota(jnp.int32, (2 * SB_SUB, 2 * SB_SUB), 1)
    tri = (r > c) if later else (r < c)
    return (tri | (c >= SB_SUB)).astype(BF16)


def _sb_sums(v, tri, n_sub):
    out = []
    for m in range(n_sub):
        vm = v[:, m * SB_SUB:(m + 1) * SB_SUB]
        hi = vm.astype(BF16)
        lo = (vm - hi.astype(F32)).astype(BF16)
        r = _dot(jnp.concatenate([hi, lo], axis=1), tri)
        out.append((r[:, :SB_SUB], r[:, SB_SUB:]))
    return out


def _sb_tri_pair_before():
    r = lax.broadcasted_iota(jnp.int32, (2 * SB_SUB, 2 * SB_SUB), 0)
    c = lax.broadcasted_iota(jnp.int32, (2 * SB_SUB, 2 * SB_SUB), 1)
    return ((r // SB_SUB == c // SB_SUB) & (r % SB_SUB < c % SB_SUB)).astype(BF16)


def _sb_prefix_pairs(e, tri_pair, n_sub):
    out = []
    for m in range(0, n_sub, 2):
        pair = e[:, m * SB_SUB:(m + 2) * SB_SUB]
        r = _dot(pair.astype(BF16), tri_pair)
        for h in range(2):
            em = pair[:, h * SB_SUB:(h + 1) * SB_SUB]
            out.append((r[:, h * SB_SUB:(h + 1) * SB_SUB], jnp.sum(em, axis=1, keepdims=True)))
    return out


def _sb_logs(nqh, k_blk, valid):
    nz = _dot(nqh, k_blk, _NT)
    log_not = jnp.minimum(nz, 0.0) - jnp.log(1.0 + jnp.exp2(jnp.abs(nz) * -LOG2_E))
    log_beta = log_not - nz
    if valid is not None:
        log_not = jnp.where(valid, log_not, 0.0)
    return log_not, log_beta


def _sb_weights(log_not, log_beta, tri_later, carry_in, valid, n_sub):
    sums = _sb_sums(log_not, tri_later, n_sub)
    run = carry_in
    later = [None] * n_sub
    for m in reversed(range(n_sub)):
        later[m] = sums[m][0] + run
        run = run + sums[m][1]
    w = jnp.exp(log_beta + jnp.concatenate(later, axis=1))
    if valid is not None:
        w = jnp.where(valid, w, 0.0)
    return w, run


def _pad_rows(a, r0):
    return a if r0 == 0 else jnp.concatenate([jnp.zeros((r0,) + a.shape[1:], a.dtype), a], axis=0)


def _sb_diag_parts(nqh, k_blk, tri_later, n_sub):
    tq = nqh.shape[0]
    parts = []
    for m in range(n_sub):
        r0 = m * SB_SUB
        row = lax.broadcasted_iota(jnp.int32, (tq - r0, SB_SUB), 0)
        col = lax.broadcasted_iota(jnp.int32, (tq - r0, SB_SUB), 1)
        valid = (row >= SB_SUB) | (row > col)
        log_not, log_beta = _sb_logs(nqh[r0:], k_blk[r0:r0 + SB_SUB], valid)
        (inside, row_sum), = _sb_sums(log_not, tri_later, 1)
        parts.append([r0, valid, log_beta, inside, row_sum])
    run = jnp.zeros((tq, LANES), F32)
    for part in reversed(parts):
        r0, row_sum = part[0], part[4]
        part[4] = run[r0:]
        run = run + _pad_rows(row_sum, r0)
    return parts, run


def _sb_diag_fwd(nqh, k_blk, v_blk, tri_later, n_sub):
    parts, run = _sb_diag_parts(nqh, k_blk, tri_later, n_sub)
    acc = jnp.zeros((nqh.shape[0], LANES), F32)
    for r0, valid, log_beta, inside, later_blocks in parts:
        w = jnp.where(valid, jnp.exp(log_beta + inside + later_blocks), 0.0)
        acc = acc + _pad_rows(_dot(w.astype(BF16), v_blk[r0:r0 + SB_SUB]), r0)
    return run, acc


def _sb_valid(i, j, tq, tk):
    qpos = i * tq + lax.broadcasted_iota(jnp.int32, (tq, tk), 0)
    kpos = j * tk + lax.broadcasted_iota(jnp.int32, (tq, tk), 1)
    return qpos > kpos


def sb_fwd(projb, kv, name, gather=()):
    s = projb.shape[0]
    tq, tk = _rows_tile(s, SB_ROWS), _rows_tile(s, SB_KEYS)
    n_sub = tk // SB_SUB
    n_pairs = SB_WIDTH // LANES
    assert s // tk <= LANES and tk % tq == 0

    n_g = len(gather)
    n_i = s // tq

    def body(*refs):
        q_ref, k_ref, v_ref = refs[:3]
        o_ref, c_ref = refs[3 + n_g:5 + n_g]
        i = pl.program_id(1)
        if n_g:
            start, wait = _exchange_ops(refs[3:3 + n_g], refs[5 + n_g:5 + 2 * n_g], refs[5 + 2 * n_g:], 0)
            pl.when((pl.program_id(0) == 0) & (i == 0))(start)
        j_diag = (i * tq) // tk
        masks = _head_masks((tq, LANES))
        nq = q_ref[...] * (-QK_SCALE)
        tri = _sb_tri(later=True)
        lane = lax.broadcasted_iota(jnp.int32, (tq, LANES), 1)
        nqh = [jnp.where(hm, nq, 0.0).astype(BF16) for hm in masks]

        def block(j, carry, valid):
            off = pl.multiple_of(j * tk, tk)
            k_blk, v_blk = k_ref[pl.ds(off, tk), :], v_ref[pl.ds(off, tk), :]
            out = []
            for hh in range(2):
                c, acc, cmat = carry[3 * hh:3 * hh + 3]
                log_not, log_beta = _sb_logs(nqh[hh], k_blk, valid)
                w, c_next = _sb_weights(log_not, log_beta, tri, c, valid, n_sub)
                out += [c_next, acc + _dot(w.astype(BF16), v_blk), jnp.where(lane == j, c, cmat)]
            return tuple(out)

        zero = jnp.zeros((tq, LANES), F32)
        if tq == tk:
            off = pl.multiple_of(i * tk, tk)
            k_blk, v_blk = k_ref[pl.ds(off, tk), :], v_ref[pl.ds(off, tk), :]
            carry = ()
            for hh in range(2):
                carry += _sb_diag_fwd(nqh[hh], k_blk, v_blk, tri, n_sub) + (zero,)
        else:
            carry = block(j_diag, (zero,) * 6, _sb_valid(i, j_diag, tq, tk))
        def run_blocks(last, count):
            def trip(t, cr):
                for u in range(count):
                    cr = block(last - count * t - u, cr, None)
                return cr
            return trip

        n_quads, rest = j_diag // 4, j_diag % 4
        carry = lax.fori_loop(0, n_quads, run_blocks(j_diag - 1, 4), carry)
        carry = lax.fori_loop(0, rest // 2, run_blocks(rest - 1, 2), carry)
        carry = lax.fori_loop(0, rest % 2, lambda jj, cr: block(jj, cr, None), carry)
        c_ref[0] = carry[2]
        c_ref[1] = carry[5]
        o_ref[...] = jnp.where(masks[0], carry[1], carry[4]).astype(BF16)
        if n_g:
            pl.when((pl.program_id(0) == n_pairs - 1) & (i == n_i - 1))(wait)

    outs = pl.pallas_call(
        body, name=name, grid=(n_pairs, n_i),
        in_specs=[pl.BlockSpec((tq, LANES), lambda p, i: (i, p)),
                  pl.BlockSpec((s, LANES), lambda p, i: (0, p)),
                  pl.BlockSpec((s, LANES), lambda p, i: (0, n_pairs + p))] + [_ANY] * n_g,
        out_specs=[pl.BlockSpec((tq, LANES), lambda p, i: (i, p)),
                   pl.BlockSpec((2, tq, LANES), lambda p, i: (p, i, 0))] + [_ANY] * n_g,
        out_shape=[jax.ShapeDtypeStruct((s, SB_WIDTH + MEM_WIDTH), BF16),
                   jax.ShapeDtypeStruct((N_SB_HEADS, s, LANES), F32)] + _exchange_out_shape([], gather),
        scratch_shapes=_exchange_sems(n_g) if n_g else [],
        compiler_params=_params("arbitrary", "arbitrary"),
    )(projb, kv, kv, *gather)
    return outs[0], outs[1], outs[2:]


def sb_bwd(projb, kv, dcat, csave, name, scatter=()):
    s = projb.shape[0]
    tq, tk = _rows_tile(s, SB_ROWS), _rows_tile(s, SB_KEYS)
    n_sub = tk // SB_SUB
    n_pairs = SB_WIDTH // LANES
    n_x = len(scatter)
    n_i = s // tq

    def body(*refs):
        q_ref, k_ref, v_ref, do_ref, c_ref = refs[:5]
        dq_ref, dkb_ref, dvb_ref = refs[5 + n_x:8 + n_x]
        dk_ref, dv_ref = refs[8 + 2 * n_x:10 + 2 * n_x]
        i = pl.program_id(1)
        if n_x:
            start, wait = _exchange_ops(refs[5:5 + n_x], refs[8 + n_x:8 + 2 * n_x], refs[10 + 2 * n_x:], n_x)
            pl.when((pl.program_id(0) == 0) & (i == 0))(start)
        j_diag = (i * tq) // tk

        @pl.when(i == 0)
        def _():
            dk_ref[...] = jnp.zeros_like(dk_ref)
            dv_ref[...] = jnp.zeros_like(dv_ref)

        masks = _head_masks((tq, LANES))
        qs = q_ref[...] * QK_SCALE
        do = do_ref[...]
        tri_later = _sb_tri(later=True)
        tri_before = _sb_tri_pair_before()
        lane = lax.broadcasted_iota(jnp.int32, (tq, LANES), 1)
        qh = [jnp.where(hm, qs, 0.0).astype(BF16) for hm in masks]
        nqh = [jnp.where(hm, -qs, 0.0).astype(BF16) for hm in masks]
        doh = [jnp.where(hm, do, 0.0).astype(BF16) for hm in masks]

        def head_part(hh, rows, k_blk, v_blk, valid, c, e_before, dq):
            n_part = k_blk.shape[0] // SB_SUB
            log_not, log_beta = _sb_logs(nqh[hh][rows], k_blk, valid)
            w, _ = _sb_weights(log_not, log_beta, tri_later, c, valid, n_part)
            dv_h = _dot(w.astype(BF16), doh[hh][rows], _TN)
            e = w * _dot(doh[hh][rows], v_blk, _NT)
            sums = _sb_prefix_pairs(e, tri_before, n_part)
            run = e_before
            e_pre = []
            for m in range(n_part):
                e_pre.append(sums[m][0] + run)
                run = run + sums[m][1]
            dz = e - jnp.exp(log_beta) * (e + jnp.concatenate(e_pre, axis=1))
            if valid is not None:
                dz = jnp.where(valid, dz, 0.0)
            dz = dz.astype(BF16)
            return run, dq + _dot(dz, k_blk), _dot(dz, qh[hh][rows], _TN), dv_h

        def block(j, carry, valid):
            off = pl.multiple_of(j * tk, tk)
            k_blk, v_blk = k_ref[pl.ds(off, tk), :], v_ref[pl.ds(off, tk), :]
            out, dk, dv = [], None, None
            for hh in range(2):
                c = jnp.sum(jnp.where(lane == j, c_ref[hh], 0.0), axis=1, keepdims=True)
                run, dq, dk_h, dv_h = head_part(hh, slice(None), k_blk, v_blk, valid, jnp.broadcast_to(c, (tq, LANES)),
                                                carry[2 * hh], carry[2 * hh + 1])
                dk, dv = (dk_h, dv_h) if hh == 0 else (dk + dk_h, dv + dv_h)
                out += [run, dq]
            dk_ref[pl.ds(off, tk), :] += dk
            dv_ref[pl.ds(off, tk), :] += dv
            return tuple(out)

        def diagonal(carry):
            half = tq // 2
            off = pl.multiple_of(i * tk, tk)
            k_blk, v_blk = k_ref[pl.ds(off, tk), :], v_ref[pl.ds(off, tk), :]
            row_a = lax.broadcasted_iota(jnp.int32, (half, half), 0)
            col_a = lax.broadcasted_iota(jnp.int32, (half, half), 1)
            row_b = lax.broadcasted_iota(jnp.int32, (half, tk), 0) + half
            col_b = lax.broadcasted_iota(jnp.int32, (half, tk), 1)
            zero_c = jnp.zeros((half, LANES), F32)
            dqs, dk, dv = [], None, None
            for hh in range(2):
                e_before, dq = carry[2 * hh], carry[2 * hh + 1]
                _, dq_a, dk_a, dv_a = head_part(hh, slice(0, half), k_blk[:half], v_blk[:half], row_a > col_a,
                                                zero_c, e_before[:half], dq[:half])
                _, dq_b, dk_b, dv_b = head_part(hh, slice(half, tq), k_blk, v_blk, row_b > col_b,
                                                zero_c, e_before[half:], dq[half:])
                zeros_k = jnp.zeros((tk - half, LANES), F32)
                dk_h = dk_b + jnp.concatenate([dk_a, zeros_k], axis=0)
                dv_h = dv_b + jnp.concatenate([dv_a, zeros_k], axis=0)
                dk, dv = (dk_h, dv_h) if hh == 0 else (dk + dk_h, dv + dv_h)
                dqs.append(jnp.concatenate([dq_a, dq_b], axis=0))
            dk_ref[pl.ds(off, tk), :] += dk
            dv_ref[pl.ds(off, tk), :] += dv
            return dqs

        zero = jnp.zeros((tq, LANES), F32)
        def run_blocks(first, count):
            def trip(t, cr):
                for u in range(count):
                    cr = block(first + count * t + u, cr, None)
                return cr
            return trip

        carry = lax.fori_loop(0, j_diag // 2, run_blocks(0, 2), (zero,) * 4)
        carry = lax.fori_loop(0, j_diag % 2, run_blocks(j_diag - 1, 1), carry)
        if tq == tk:
            dq0, dq1 = diagonal(carry)
        else:
            carry = block(j_diag, carry, _sb_valid(i, j_diag, tq, tk))
            dq0, dq1 = carry[1], carry[3]
        dq_ref[...] = (jnp.where(masks[0], dq0, dq1) * QK_SCALE).astype(BF16)

        @pl.when(i == n_i - 1)
        def _():
            dkb_ref[...] = dk_ref[...].astype(BF16)
            dvb_ref[...] = dv_ref[...].astype(BF16)

        if n_x:
            pl.when((pl.program_id(0) == n_pairs - 1) & (i == n_i - 1))(wait)

    tile = pl.BlockSpec((tq, LANES), lambda p, i: (i, p))
    full = pl.BlockSpec((s, LANES), lambda p, i: (0, p))
    outs = pl.pallas_call(
        body, name=name, grid=(n_pairs, n_i),
        in_specs=[tile, full, pl.BlockSpec((s, LANES), lambda p, i: (0, n_pairs + p)), tile,
                  pl.BlockSpec((2, tq, LANES), lambda p, i: (p, i, 0))] + [_ANY] * n_x,
        out_specs=[tile, full, full] + [_ANY] * n_x,
        out_shape=[jax.ShapeDtypeStruct((s, SB_WIDTH + MEM_WIDTH), BF16), jax.ShapeDtypeStruct((s, SB_WIDTH), BF16),
                   jax.ShapeDtypeStruct((s, SB_WIDTH), BF16)] + _exchange_out_shape(scatter, []),
        scratch_shapes=[pltpu.VMEM((s, LANES), F32), pltpu.VMEM((s, LANES), F32)]
        + (_exchange_sems(n_x) if n_x else []),
        compiler_params=_params("arbitrary", "arbitrary"),
    )(projb, kv, kv, dcat, csave, *scatter)
    return outs[0], outs[1], outs[2], outs[3:]


def reduce_adamw(parts, w, m, v, name, after=None):
    n, r, c = parts.shape
    tr = next((t for t in (128, 176) if r % t == 0), r)
    bias1 = 1.0 - ADAM_B1 ** ADAM_STEP
    bias2 = 1.0 - ADAM_B2 ** ADAM_STEP

    def body(p_ref, w_ref, m_ref, v_ref, *rest):
        g_ref, d_ref, nm_ref, nv_ref = rest[-4:]
        g = p_ref[0].astype(F32)
        for k in range(1, n):
            g = g + p_ref[k].astype(F32)
        new_m = ADAM_B1 * m_ref[...] + (1.0 - ADAM_B1) * g
        new_v = ADAM_B2 * v_ref[...] + (1.0 - ADAM_B2) * (g * g)
        m_hat = new_m / bias1
        v_hat = new_v / bias2
        g_ref[...] = g
        d_ref[...] = -ADAM_LR * (m_hat / (jnp.sqrt(v_hat) + ADAM_EPS) + ADAM_WD * w_ref[...])
        nm_ref[...] = new_m
        nv_ref[...] = new_v

    row = pl.BlockSpec((tr, c), lambda i: (i, 0))
    return pl.pallas_call(
        body, name=name, grid=(r // tr,),
        in_specs=[pl.BlockSpec((n, tr, c), lambda i: (0, i, 0)), row, row, row] + ([] if after is None else [_ANY]),
        out_specs=[row, row, row, row],
        out_shape=[jax.ShapeDtypeStruct((r, c), F32)] * 4,
        compiler_params=_params("parallel"),
    )(parts, w, m, v, *(() if after is None else (after,)))


EARLY_0 = ("a_w_in", "a_w_mem_kv")
EARLY_1 = ("a_w_out",)
EARLY_2 = ("a_w_gu", "a_w_down")
EARLY_3 = ("w_kv", "b_w_q", "b_w_mem_kv")
EARLY = EARLY_0 + EARLY_1 + EARLY_2 + EARLY_3
LATE = ("b_w_out", "b_w_gu", "b_w_down")
BIG = EARLY + LATE
GATHER_3 = ("w_kv", "b_w_q")
GATHER_LATE = LATE + ("b_w_mem_kv",)
COL_SHARDED = ("a_w_gu", "w_kv", "b_w_gu")
SMALL_SHARDED = ("a_norm_mix", "a_scale", "a_norm_ffn")
SMALL_REPL = ("mem_norm", "kv_norm", "b_norm_mix", "b_norm_ffn", "final_norm")
WEIGHTS = ("mem_norm", "a_norm_mix", "a_w_in", "a_w_group", "a_scale", "a_w_mem_kv", "a_w_out", "a_norm_ffn", "a_w_gu",
           "a_w_down", "kv_norm", "w_kv", "b_norm_mix", "b_w_q", "b_w_mem_kv", "b_w_out", "b_norm_ffn", "b_w_gu",
           "b_w_down", "final_norm")


def _pad_row(v):
    v = v.reshape(1, -1)
    return jnp.pad(v, ((0, 0), (0, PACK_COLS - v.shape[1])))


def kernel(x, mem, mem_norm, a_norm_mix, a_w_in, a_w_group, a_scale, a_w_mem_kv, a_w_out, a_norm_ffn, a_w_gu, a_w_down, kv_norm, w_kv, b_norm_mix, b_w_q, b_w_mem_kv, b_w_out, b_norm_ffn, b_w_gu, b_w_down, final_norm, loss_target, m_mem_norm, m_a_norm_mix, m_a_w_in, m_a_w_group, m_a_scale, m_a_w_mem_kv, m_a_w_out, m_a_norm_ffn, m_a_w_gu, m_a_w_down, m_kv_norm, m_w_kv, m_b_norm_mix, m_b_w_q, m_b_w_mem_kv, m_b_w_out, m_b_norm_ffn, m_b_w_gu, m_b_w_down, m_final_norm, v_mem_norm, v_a_norm_mix, v_a_w_in, v_a_w_group, v_a_scale, v_a_w_mem_kv, v_a_w_out, v_a_norm_ffn, v_a_w_gu, v_a_w_down, v_kv_norm, v_w_kv, v_b_norm_mix, v_b_w_q, v_b_w_mem_kv, v_b_w_out, v_b_norm_ffn, v_b_w_gu, v_b_w_down, v_final_norm):
    w = dict(mem_norm=mem_norm, a_norm_mix=a_norm_mix, a_w_in=a_w_in, a_w_group=a_w_group, a_scale=a_scale,
             a_w_mem_kv=a_w_mem_kv, a_w_out=a_w_out, a_norm_ffn=a_norm_ffn, a_w_gu=a_w_gu, a_w_down=a_w_down,
             kv_norm=kv_norm, w_kv=w_kv, b_norm_mix=b_norm_mix, b_w_q=b_w_q, b_w_mem_kv=b_w_mem_kv, b_w_out=b_w_out,
             b_norm_ffn=b_norm_ffn, b_w_gu=b_w_gu, b_w_down=b_w_down, final_norm=final_norm)
    mom = dict(mem_norm=m_mem_norm, a_norm_mix=m_a_norm_mix, a_w_in=m_a_w_in, a_w_group=m_a_w_group, a_scale=m_a_scale,
               a_w_mem_kv=m_a_w_mem_kv, a_w_out=m_a_w_out, a_norm_ffn=m_a_norm_ffn, a_w_gu=m_a_w_gu,
               a_w_down=m_a_w_down, kv_norm=m_kv_norm, w_kv=m_w_kv, b_norm_mix=m_b_norm_mix, b_w_q=m_b_w_q,
               b_w_mem_kv=m_b_w_mem_kv, b_w_out=m_b_w_out, b_norm_ffn=m_b_norm_ffn, b_w_gu=m_b_w_gu,
               b_w_down=m_b_w_down, final_norm=m_final_norm)
    var = dict(mem_norm=v_mem_norm, a_norm_mix=v_a_norm_mix, a_w_in=v_a_w_in, a_w_group=v_a_w_group, a_scale=v_a_scale,
               a_w_mem_kv=v_a_w_mem_kv, a_w_out=v_a_w_out, a_norm_ffn=v_a_norm_ffn, a_w_gu=v_a_w_gu,
               a_w_down=v_a_w_down, kv_norm=v_kv_norm, w_kv=v_w_kv, b_norm_mix=v_b_norm_mix, b_w_q=v_b_w_q,
               b_w_mem_kv=v_b_w_mem_kv, b_w_out=v_b_w_out, b_norm_ffn=v_b_norm_ffn, b_w_gu=v_b_w_gu,
               b_w_down=v_b_w_down, final_norm=v_final_norm)

    me = 4 * lax.axis_index("x") + 2 * lax.axis_index("y") + lax.axis_index("c")
    shard2d = {n: w[n].shape[-2:] for n in BIG}
    shard = {n: w[n].reshape(shard2d[n]).astype(BF16) for n in BIG}
    n_sh = N_DEV // 2
    s_len = x.shape[1]

    small_send = jnp.concatenate([_pad_row(w[n]) for n in SMALL_SHARDED]
                                 + [jnp.zeros((8 - len(SMALL_SHARDED), PACK_COLS), F32)], axis=0)
    def own_slot(land, own):
        return lax.dynamic_update_slice(land, own[None], (me, 0, 0))

    def gather_done(handle, names, after, name):
        got = exchange_wait(handle, after, name)
        gathered.update({n: own_slot(g, shard[n]) for n, g in zip(names, got)})
        return got

    gathered = {}
    gather_0 = exchange_start([], [shard[n] for n in EARLY_0] + [small_send], x, "gather_0_start")
    gather_1 = exchange_start([], [shard[n] for n in EARLY_1], gather_0["token"], "gather_1_start")
    gather_2 = exchange_start([], [shard["a_w_gu"]], gather_1["token"], "gather_2_start")
    gather_2b = exchange_start([], [shard["a_w_down"]], gather_2["token"], "gather_2b_start")
    gather_3 = exchange_start([], [shard[n] for n in GATHER_3], gather_2b["token"], "gather_3_start")
    small_all = own_slot(gather_done(gather_0, EARLY_0, gather_3["token"], "gather_0_wait")[-1], small_send)
    a_norm_mix_f = small_all[:, 0, :a_norm_mix.shape[1]].reshape(-1)
    a_scale_f = small_all[:, 1, :a_scale.shape[1]].reshape(-1)
    a_norm_ffn_f = small_all[:, 2, :a_norm_ffn.shape[1]].reshape(-1)

    def rows_full(n):
        return gathered[n].reshape(-1, shard2d[n][1])

    def gu_views(tag):
        gu8 = gathered[tag + "_w_gu"]
        return gu8.reshape((2, n_sh) + gu8.shape[1:]), gu8

    def down_view(tag):
        down = gathered[tag + "_w_down"]
        return down.reshape(n_sh, -1, down.shape[2])

    wbd = jnp.zeros((SB_WIDTH, SB_WIDTH), BF16)
    for g in range(4):
        sl = slice(g * POOL_GROUP, (g + 1) * POOL_GROUP)
        wbd = wbd.at[sl, sl].set(a_w_group[0, g].astype(BF16))

    xs, mems, tgt = x[0], mem[0], loss_target[0]

    memn = rmsnorm_fwd(mems, mem_norm, "memn")
    (h0, proj), = norm_matmul(xs, [(a_norm_mix_f, rows_full("a_w_in"), F32)], "proj_a")
    mkv_a = matmul([(memn, rows_full("a_w_mem_kv"))], "nn", BF16, "mkv_a")
    pooled, cat_a = pool_fwd(proj, wbd, a_scale_f, "pool_fwd")
    cat_a = mem_attn_fwd(proj, mkv_a, cat_a, "mem_fwd_a")
    gather_done(gather_1, EARLY_1, cat_a, "gather_1_wait")
    x1 = matmul([(cat_a, rows_full("a_w_out"))], "nn", F32, "out_a", res=xs)
    h1 = rmsnorm_fwd(x1, a_norm_ffn_f, "h1")
    gather_done(gather_2, ("a_w_gu",), h1, "gather_2_wait")
    wgu_a, wgu8_a = gu_views("a")
    gu_a, act_a = ffn_up(h1, wgu_a, "ffn_up_a")
    gather_done(gather_2b, ("a_w_down",), act_a, "gather_2b_wait")
    wdown_a = down_view("a")
    x2 = matmul([(act_a, wdown_a)], "nn", F32, "down_a", res=x1, roles=("reduce", "reduce"), count=n_sh)

    gather_done(gather_3, GATHER_3, x2, "gather_3_wait")
    wkv = jnp.transpose(gathered["w_kv"], (1, 0, 2)).reshape(shard2d["w_kv"][0], -1)
    (hk, kv), (h3, projb) = norm_matmul(
        x2, [(kv_norm, wkv, BF16), (b_norm_mix[0], rows_full("b_w_q"), F32)], "kv_proj_b")
    cat_b, csave, got = sb_fwd(projb, kv, "sb_fwd", gather=[shard[n] for n in GATHER_LATE])
    gathered.update(zip(GATHER_LATE, got))
    mkv_b = matmul([(memn, rows_full("b_w_mem_kv"))], "nn", BF16, "mkv_b")
    cat_b = mem_attn_fwd(projb, mkv_b, cat_b, "mem_fwd_b")
    x3 = matmul([(cat_b, rows_full("b_w_out"))], "nn", F32, "out_b", res=x2)
    h4 = rmsnorm_fwd(x3, b_norm_ffn[0], "h4")
    (wgu_b, wgu8_b), wdown_b = gu_views("b"), down_view("b")
    gu_b, act_b = ffn_up(h4, wgu_b, "ffn_up_b")
    x4 = matmul([(act_b, wdown_b)], "nn", F32, "down_b", res=x3, roles=("reduce", "reduce"), count=n_sh)

    grads = {}
    dx4, dx4_bf, grads["final_norm"], loss_part = loss_head(x4, tgt, final_norm, "loss_head")

    def blocks(n, g):
        return g.reshape((N_DEV,) + tuple(shard2d[n]))

    def ffn_weight_grads(tag, dx_out_bf, h, gu, act, wgu8, wdown):
        fs = wgu8.shape[2]
        dgu8 = ffn_bwd_act(dx_out_bf, wdown, gu, "ffn_bwd_act_" + tag).reshape(N_DEV, s_len, fs)
        d_wdown = matmul([(act, dx_out_bf)], "tn", BF16, "dw_down_" + tag, tm=fs, tn=1024, roles=("batch", None),
                         count=n_sh)
        grads[tag + "_w_gu"] = matmul([(dgu8, h)], "tn", BF16, "dw_gu_" + tag, tm=fs, tn=1024, roles=("batch", None),
                                      count=N_DEV)
        grads[tag + "_w_down"] = blocks(tag + "_w_down", d_wdown)
        return dgu8

    def ffn_input_grads(tag, x_in, dx_out, dgu8, wgu8, g_ffn):
        dx_in, dx_in_bf, dgs = matmul_norm_bwd(x_in, dx_out, [(g_ffn, [(dgu8, wgu8)])], "norm_bwd_ffn_" + tag)
        return dx_in, dx_in_bf, dgs[0]

    def grads_start(names, small, after, name):
        return exchange_start([grads[n] for n in names], small, after, name)

    def grads_done(handle, names, after, name):
        got = exchange_wait(handle, after, name)
        for n, g in zip(names, got):
            recv[n] = lax.dynamic_update_slice(g, lax.dynamic_slice(grads[n], (me, 0, 0), (1,) + g.shape[1:]), (me, 0, 0))
        return got

    recv = {}
    dgu8_b = ffn_weight_grads("b", dx4_bf, h4, gu_b, act_b, wgu8_b, wdown_b)
    dx3, dx3_bf, grads["b_norm_ffn"] = ffn_input_grads("b", x3, dx4, dgu8_b, wgu8_b, b_norm_ffn[0])

    dcat_b = matmul([(dx3_bf, rows_full("b_w_out"))], "nt", F32, "dcat_b")
    grads["b_w_out"] = blocks("b_w_out", matmul([(cat_b, dx3_bf)], "tn", BF16, "dw_out_b"))
    dprojb, dk, dv, recv_late = sb_bwd(projb, kv, dcat_b, csave, "sb_bwd", scatter=[grads[n] for n in LATE])
    dprojb, dmkv_b = mem_attn_bwd(projb, mkv_b, dcat_b, dprojb, "mem_bwd_b")
    grads["b_w_q"] = blocks("b_w_q", matmul([(h3, dprojb)], "tn", BF16, "dw_q_b"))
    d_wkv_t = jnp.concatenate([matmul([(dk, hk)], "tn", BF16, "dw_k", tm=SB_WIDTH),
                               matmul([(dv, hk)], "tn", BF16, "dw_v", tm=SB_WIDTH)], axis=0)
    grads["w_kv"] = d_wkv_t.reshape(N_DEV, -1, d_wkv_t.shape[1])
    dx2, dx2_bf, dgs = matmul_norm_bwd(
        x2, dx3, [(b_norm_mix[0], [(dprojb, rows_full("b_w_q"))]),
                  (kv_norm, [(dk, wkv[:, :SB_WIDTH]), (dv, wkv[:, SB_WIDTH:])])], "norm_bwd_x2")
    grads["b_norm_mix"], grads["kv_norm"] = dgs
    dmkv_b_bf = dmkv_b.astype(BF16)
    grads["b_w_mem_kv"] = blocks("b_w_mem_kv", matmul([(memn, dmkv_b_bf)], "tn", BF16, "dw_mkv_b"))

    grads_3 = grads_start(EARLY_3, [], grads["b_w_mem_kv"], "grads_3_start")
    dgu8_a = ffn_weight_grads("a", dx2_bf, h1, gu_a, act_a, wgu8_a, wdown_a)
    grads_2 = grads_start(EARLY_2, [], grads_3["token"], "grads_2_start")
    dx1, dx1_bf, grads["a_norm_ffn"] = ffn_input_grads(
        "a", x1, dx2, dgu8_a, wgu8_a, a_norm_ffn_f + grads_2["token"][0, 0])

    dcat_a = matmul([(dx1_bf, rows_full("a_w_out"))], "nt", F32, "dcat_a")
    grads["a_w_out"] = blocks("a_w_out", matmul([(cat_a, dx1_bf)], "tn", BF16, "dw_out_a"))
    dproj, d_wbd, grads["a_scale"] = pool_bwd(dcat_a, pooled, wbd, a_scale_f, "pool_bwd")
    grads["a_w_group"] = jnp.stack(
        [d_wbd[g * POOL_GROUP:(g + 1) * POOL_GROUP, g * POOL_GROUP:(g + 1) * POOL_GROUP] for g in range(4)])
    group_grads = grads["a_w_group"].reshape(-1, PACK_COLS).astype(BF16)
    grads_1 = exchange_start([grads[n] for n in EARLY_1], [group_grads], grads_2["token"], "grads_1_start")
    dproj, dmkv_a = mem_attn_bwd(proj, mkv_a, dcat_a, dproj, "mem_bwd_a")
    grads["a_w_in"] = blocks("a_w_in", matmul([(h0, dproj)], "tn", BF16, "dw_in_a"))
    dmkv_a_bf = dmkv_a.astype(BF16)
    grads["a_w_mem_kv"] = blocks("a_w_mem_kv", matmul([(memn, dmkv_a_bf)], "tn", BF16, "dw_mkv_a"))
    grads_0 = grads_start(EARLY_0, [], grads_1["token"], "grads_0_start")
    dx, _, dgs = matmul_norm_bwd(
        xs, dx1, [(a_norm_mix_f + grads_0["token"][0, 0], [(dproj, rows_full("a_w_in"))])], "norm_bwd_x")
    grads["a_norm_mix"] = dgs[0]
    dmemn = matmul([(dmkv_a_bf, rows_full("a_w_mem_kv")), (dmkv_b_bf, rows_full("b_w_mem_kv"))], "nt", F32, "dmemn")
    _, _, dgs = rmsnorm_bwd(mems, jnp.zeros_like(mems), [(mem_norm, dmemn)], "norm_bwd_mem")
    grads["mem_norm"] = dgs[0]

    small_names = SMALL_REPL + SMALL_SHARDED
    head_rows = 16

    def small_pack(rows_of):
        head = jnp.zeros((head_rows, PACK_COLS), F32)
        for r, (v, col) in enumerate(rows_of):
            v = v.reshape(1, -1)
            if col is None:
                head = head + jnp.pad(v, ((r, head_rows - 1 - r), (0, PACK_COLS - v.shape[1])))
            else:
                head = lax.dynamic_update_slice(head, v, (r, col))
        return head

    small_grads = small_pack([(grads[n], None) for n in small_names] + [(loss_part[:, :1], None)])
    grads_s = exchange_start([], [small_grads], dx, "grads_s_start")

    def small_state(src):
        return small_pack([(src[n], None) for n in SMALL_REPL]
                          + [(src[n], me * src[n].shape[-1]) for n in SMALL_SHARDED])

    def state2d(src, n):
        a = src[n].reshape(shard2d[n])
        return a.T if n in COL_SHARDED else a

    def adamw(names, after):
        for n in names:
            big_out[n] = reduce_adamw(recv[n], state2d(w, n), state2d(mom, n), state2d(var, n), "adamw_" + n, after=after)
        return big_out[names[-1]][0]

    big_out = {}
    recv.update(zip(LATE, recv_late))
    done = adamw(LATE, grads_s["token"])
    grads_done(grads_3, EARLY_3, done, "grads_3_wait")
    done = adamw(EARLY_3, None)
    grads_done(grads_2, EARLY_2, done, "grads_2_wait")
    done = adamw(EARLY_2, None)
    group_recv = own_slot(grads_done(grads_1, EARLY_1, done, "grads_1_wait")[-1], group_grads)
    done = adamw(EARLY_1, None)
    group_out = reduce_adamw(group_recv, *[src["a_w_group"].reshape(-1, PACK_COLS) for src in (w, mom, var)], "adamw_group")
    grads_done(grads_0, EARLY_0, group_out[0], "grads_0_wait")
    done = adamw(EARLY_0, None)
    small_recv = own_slot(exchange_wait(grads_s, done, "grads_s_wait")[0], small_grads)
    small_out = reduce_adamw(small_recv, small_state(w), small_state(mom), small_state(var), "adamw_small")

    def unpack(kind):
        out = {n: (big_out[n][kind].T if n in COL_SHARDED else big_out[n][kind]).reshape(w[n].shape) for n in BIG}
        so = small_out[kind]
        for r, n in enumerate(SMALL_REPL):
            out[n] = so[r, :w[n].shape[-1]].reshape(w[n].shape)
        for r, n in enumerate(SMALL_SHARDED):
            width = w[n].shape[-1]
            out[n] = lax.dynamic_slice(so, (len(SMALL_REPL) + r, me * width), (1, width)).reshape(w[n].shape)
        out["a_w_group"] = group_out[kind].reshape(a_w_group.shape)
        return out

    loss = small_out[0][len(small_names), 0]
    results = [loss, dx.reshape(x.shape)]
    for kind in range(4):
        out = unpack(kind)
        results += [out[n] for n in WEIGHTS]
    return tuple(results)
```
